```python
import math
import jax, jax.numpy as jnp
from jax import lax
import numpy as np

D_MODEL = 1024
BATCH = 8
SEQ = 4096
DEPTH = 1

D_MIX = D_MODEL
D_POOL = D_MIX // 2
D_SSM = D_MIX - D_POOL
POOL_WINDOWS = (2, 4, 8, 16)
N_POOL_GROUPS = len(POOL_WINDOWS)
POOL_GROUP = D_POOL // N_POOL_GROUPS
SSM_GROUP = 16
N_SSM_GROUPS = D_SSM // SSM_GROUP
SSM_STATE = 64
DT_MIN = 0.001
DT_MAX = 0.1
D_FF = 2816
CONV_WIDTH = 3
EPS = 1e-6

kernel_name = "hybrid_pool_s5_convffn_encoder"


def _rmsnorm(x, g):
    xf = x.astype(jnp.float32)
    inv = lax.rsqrt(jnp.mean(xf * xf, axis=-1, keepdims=True) + EPS)
    return (xf * inv).astype(x.dtype) * g


def _window_bounds(window, length):
    half = window // 2
    t = np.arange(length)
    lo = np.clip(t - half, 0, length)
    hi = np.clip(t + half, 0, length)
    return lo.astype(np.int32), hi.astype(np.int32), (hi - lo).astype(np.float32)


def _pool_mixer(u, pool_w, pool_scale):
    L = u.shape[1]
    outs = []
    for gi, w in enumerate(POOL_WINDOWS):
        ug = u[..., gi * POOL_GROUP:(gi + 1) * POOL_GROUP].astype(jnp.float32)
        cs = jnp.concatenate([jnp.zeros_like(ug[:, :1]), jnp.cumsum(ug, axis=1)], axis=1)
        lo, hi, cnt = _window_bounds(w, L)
        mean = (jnp.take(cs, hi, axis=1) - jnp.take(cs, lo, axis=1)) / cnt[None, :, None]
        pooled = (mean - ug).astype(u.dtype)
        outs.append(jnp.einsum('blc,cd->bld', pooled, pool_w[gi]))
    return jnp.concatenate(outs, axis=-1) * pool_scale


def _cmul(ar, ai, br, bi):
    return ar * br - ai * bi, ar * bi + ai * br


def _scan_combine(e_i, e_j):
    ai_re, ai_im, bi_re, bi_im = e_i
    aj_re, aj_im, bj_re, bj_im = e_j
    a_re, a_im = _cmul(aj_re, aj_im, ai_re, ai_im)
    t_re, t_im = _cmul(aj_re, aj_im, bi_re, bi_im)
    return a_re, a_im, t_re + bj_re, t_im + bj_im


def _s5_direction(u, log_neg_a_re, a_im, log_dt, b_re, b_im, c_re, c_im, reverse):
    L = u.shape[1]
    log_neg_a_re = log_neg_a_re.astype(jnp.float32)
    a_im = a_im.astype(jnp.float32)
    dt = jnp.exp(log_dt.astype(jnp.float32))[:, None]
    a_re = -jnp.exp(log_neg_a_re)
    mag = jnp.exp(a_re * dt)
    ang = a_im * dt
    lam_re, lam_im = mag * jnp.cos(ang), mag * jnp.sin(ang)
    den = a_re * a_re + a_im * a_im
    f_re = ((lam_re - 1.0) * a_re + lam_im * a_im) / den
    f_im = (lam_im * a_re - (lam_re - 1.0) * a_im) / den
    bu_re = jnp.einsum('blgh,gnh->blgn', u, b_re.astype(jnp.float32))
    bu_im = jnp.einsum('blgh,gnh->blgn', u, b_im.astype(jnp.float32))
    in_re, in_im = _cmul(f_re, f_im, bu_re, bu_im)
    shape = (1, L) + lam_re.shape
    lam_re_l = jnp.broadcast_to(lam_re[None, None], shape)
    lam_im_l = jnp.broadcast_to(lam_im[None, None], shape)
    _, _, s_re, s_im = lax.associative_scan(
        _scan_combine, (lam_re_l, lam_im_l, in_re, in_im), reverse=reverse, axis=1)
    return (jnp.einsum('blgn,ghn->blgh', s_re, c_re.astype(jnp.float32))
            - jnp.einsum('blgn,ghn->blgh', s_im, c_im.astype(jnp.float32)))


def _s5_mixer(u, ssm_log_neg_a_re, ssm_a_im, ssm_log_dt, ssm_b_re, ssm_b_im,
              ssm_c_re, ssm_c_im, ssm_d, glu_w, glu_b):
    Bsz, L, _ = u.shape
    uf = u.astype(jnp.float32).reshape(Bsz, L, N_SSM_GROUPS, SSM_GROUP)
    y_fwd = _s5_direction(uf, ssm_log_neg_a_re[0], ssm_a_im[0], ssm_log_dt[0], ssm_b_re[0],
                          ssm_b_im[0], ssm_c_re[0], ssm_c_im[0], reverse=False)
    y_bwd = _s5_direction(uf, ssm_log_neg_a_re[1], ssm_a_im[1], ssm_log_dt[1], ssm_b_re[1],
                          ssm_b_im[1], ssm_c_re[1], ssm_c_im[1], reverse=True)
    y = (y_fwd + y_bwd).reshape(Bsz, L, D_SSM).astype(u.dtype) + ssm_d * u
    z = jax.nn.gelu(y)
    return z * jax.nn.sigmoid(jnp.einsum('bld,de->ble', z, glu_w) + glu_b)


def _dwconv_centred(h, w, b):
    hp = jnp.pad(h, ((0, 0), (1, 1), (0, 0)))
    return hp[:, :-2] * w[0] + hp[:, 1:-1] * w[1] + hp[:, 2:] * w[2] + b


def _fwd_setup_inputs(seed: int = 0) -> dict:
    key = jax.random.key(seed)
    ks = jax.random.split(key, 24)
    G, N, H = N_SSM_GROUPS, SSM_STATE, SSM_GROUP
    f32 = jnp.float32
    nrm = lambda k, shape, s: (jax.random.normal(k, shape, f32) * s)
    x = jax.random.normal(ks[0], (BATCH, SEQ, D_MODEL), f32)
    norm_mix_g = 1.0 + nrm(ks[1], (D_MODEL,), 0.02)
    w_in = nrm(ks[2], (D_MODEL, D_MIX), D_MODEL ** -0.5)
    pool_w = nrm(ks[3], (N_POOL_GROUPS, POOL_GROUP, POOL_GROUP), POOL_GROUP ** -0.5)
    pool_scale = 1.0 + nrm(ks[4], (D_POOL,), 0.02)
    ssm_log_neg_a_re = math.log(0.5) + nrm(ks[5], (2, G, N), 0.01)
    ssm_a_im = math.pi * jnp.arange(N, dtype=f32)[None, None, :] + nrm(ks[6], (2, G, N), 0.01)
    ssm_log_dt = jax.random.uniform(ks[7], (2, G), f32, math.log(DT_MIN), math.log(DT_MAX))
    ssm_b_re = nrm(ks[8], (2, G, N, H), (2.0 * H) ** -0.5)
    ssm_b_im = nrm(ks[9], (2, G, N, H), (2.0 * H) ** -0.5)
    ssm_c_re = nrm(ks[10], (2, G, H, N), (2.0 * N) ** -0.5)
    ssm_c_im = nrm(ks[11], (2, G, H, N), (2.0 * N) ** -0.5)
    ssm_d = nrm(ks[12], (D_SSM,), 1.0)
    glu_w = nrm(ks[13], (D_SSM, D_SSM), D_SSM ** -0.5)
    glu_b = nrm(ks[14], (D_SSM,), 0.01)
    out_norm_pool_g = 1.0 + nrm(ks[15], (D_POOL,), 0.02)
    out_norm_ssm_g = 1.0 + nrm(ks[16], (D_SSM,), 0.02)
    w_out = nrm(ks[17], (D_MIX, D_MODEL), D_MIX ** -0.5)
    norm_ffn_g = 1.0 + nrm(ks[18], (D_MODEL,), 0.02)
    w_up = nrm(ks[19], (D_MODEL, 2 * D_FF), D_MODEL ** -0.5)
    conv_w = nrm(ks[20], (CONV_WIDTH, 2 * D_FF), CONV_WIDTH ** -0.5)
    conv_b = nrm(ks[21], (2 * D_FF,), 0.01)
    w_down = nrm(ks[22], (D_FF, D_MODEL), D_FF ** -0.5)
    final_norm_g = 1.0 + nrm(ks[23], (D_MODEL,), 0.02)
    return {"x": x, "norm_mix_g": norm_mix_g, "w_in": w_in, "pool_w": pool_w,
            "pool_scale": pool_scale, "ssm_log_neg_a_re": ssm_log_neg_a_re, "ssm_a_im": ssm_a_im,
            "ssm_log_dt": ssm_log_dt, "ssm_b_re": ssm_b_re, "ssm_b_im": ssm_b_im,
            "ssm_c_re": ssm_c_re, "ssm_c_im": ssm_c_im, "ssm_d": ssm_d, "glu_w": glu_w,
            "glu_b": glu_b, "out_norm_pool_g": out_norm_pool_g, "out_norm_ssm_g": out_norm_ssm_g,
            "w_out": w_out, "norm_ffn_g": norm_ffn_g, "w_up": w_up, "conv_w": conv_w,
            "conv_b": conv_b, "w_down": w_down, "final_norm_g": final_norm_g}


def _fwd_reference(x, norm_mix_g, w_in, pool_w, pool_scale, ssm_log_neg_a_re, ssm_a_im, ssm_log_dt,
              ssm_b_re, ssm_b_im, ssm_c_re, ssm_c_im, ssm_d, glu_w, glu_b, out_norm_pool_g,
              out_norm_ssm_g, w_out, norm_ffn_g, w_up, conv_w, conv_b, w_down, final_norm_g):
    h = x
    for _ in range(DEPTH):
        u = jnp.einsum('bld,de->ble', _rmsnorm(h, norm_mix_g), w_in)
        u_pool, u_ssm = u[..., :D_POOL], u[..., D_POOL:]
        y_pool = _pool_mixer(u_pool, pool_w, pool_scale)
        y_ssm = _s5_mixer(u_ssm, ssm_log_neg_a_re, ssm_a_im, ssm_log_dt, ssm_b_re, ssm_b_im,
                          ssm_c_re, ssm_c_im, ssm_d, glu_w, glu_b)
        y = jnp.concatenate([_rmsnorm(y_pool, out_norm_pool_g),
                             _rmsnorm(y_ssm, out_norm_ssm_g)], axis=-1)
        h = h + jnp.einsum('ble,ed->bld', y, w_out)
        up = jnp.einsum('bld,df->blf', _rmsnorm(h, norm_ffn_g), w_up)
        up = _dwconv_centred(up, conv_w, conv_b)
        val, gate = up[..., :D_FF], up[..., D_FF:]
        h = h + jnp.einsum('blf,fd->bld', val * jax.nn.silu(gate), w_down)
    return _rmsnorm(h, final_norm_g)


import jax as _jax
import jax.numpy as _jnp

TWIN_FORMAT = 'train_step'
FWD_PARAMS = ['x', 'norm_mix_g', 'w_in', 'pool_w', 'pool_scale', 'ssm_log_neg_a_re', 'ssm_a_im', 'ssm_log_dt', 'ssm_b_re', 'ssm_b_im', 'ssm_c_re', 'ssm_c_im', 'ssm_d', 'glu_w', 'glu_b', 'out_norm_pool_g', 'out_norm_ssm_g', 'w_out', 'norm_ffn_g', 'w_up', 'conv_w', 'conv_b', 'w_down', 'final_norm_g']
TWIN_WEIGHTS = ['norm_mix_g', 'w_in', 'pool_w', 'pool_scale', 'ssm_log_neg_a_re', 'ssm_a_im', 'ssm_log_dt', 'ssm_b_re', 'ssm_b_im', 'ssm_c_re', 'ssm_c_im', 'ssm_d', 'glu_w', 'glu_b', 'out_norm_pool_g', 'out_norm_ssm_g', 'w_out', 'norm_ffn_g', 'w_up', 'conv_w', 'conv_b', 'w_down', 'final_norm_g']
TWIN_DIFF_INPUT = 'x'
TWIN_INPUTS = ['x', 'norm_mix_g', 'w_in', 'pool_w', 'pool_scale', 'ssm_log_neg_a_re', 'ssm_a_im', 'ssm_log_dt', 'ssm_b_re', 'ssm_b_im', 'ssm_c_re', 'ssm_c_im', 'ssm_d', 'glu_w', 'glu_b', 'out_norm_pool_g', 'out_norm_ssm_g', 'w_out', 'norm_ffn_g', 'w_up', 'conv_w', 'conv_b', 'w_down', 'final_norm_g', 'loss_target', 'm_norm_mix_g', 'm_w_in', 'm_pool_w', 'm_pool_scale', 'm_ssm_log_neg_a_re', 'm_ssm_a_im', 'm_ssm_log_dt', 'm_ssm_b_re', 'm_ssm_b_im', 'm_ssm_c_re', 'm_ssm_c_im', 'm_ssm_d', 'm_glu_w', 'm_glu_b', 'm_out_norm_pool_g', 'm_out_norm_ssm_g', 'm_w_out', 'm_norm_ffn_g', 'm_w_up', 'm_conv_w', 'm_conv_b', 'm_w_down', 'm_final_norm_g', 'v_norm_mix_g', 'v_w_in', 'v_pool_w', 'v_pool_scale', 'v_ssm_log_neg_a_re', 'v_ssm_a_im', 'v_ssm_log_dt', 'v_ssm_b_re', 'v_ssm_b_im', 'v_ssm_c_re', 'v_ssm_c_im', 'v_ssm_d', 'v_glu_w', 'v_glu_b', 'v_out_norm_pool_g', 'v_out_norm_ssm_g', 'v_w_out', 'v_norm_ffn_g', 'v_w_up', 'v_conv_w', 'v_conv_b', 'v_w_down', 'v_final_norm_g']
TWIN_OUTPUTS = ['loss', 'grad_x', 'grad_norm_mix_g', 'grad_w_in', 'grad_pool_w', 'grad_pool_scale', 'grad_ssm_log_neg_a_re', 'grad_ssm_a_im', 'grad_ssm_log_dt', 'grad_ssm_b_re', 'grad_ssm_b_im', 'grad_ssm_c_re', 'grad_ssm_c_im', 'grad_ssm_d', 'grad_glu_w', 'grad_glu_b', 'grad_out_norm_pool_g', 'grad_out_norm_ssm_g', 'grad_w_out', 'grad_norm_ffn_g', 'grad_w_up', 'grad_conv_w', 'grad_conv_b', 'grad_w_down', 'grad_final_norm_g', 'delta_norm_mix_g', 'delta_w_in', 'delta_pool_w', 'delta_pool_scale', 'delta_ssm_log_neg_a_re', 'delta_ssm_a_im', 'delta_ssm_log_dt', 'delta_ssm_b_re', 'delta_ssm_b_im', 'delta_ssm_c_re', 'delta_ssm_c_im', 'delta_ssm_d', 'delta_glu_w', 'delta_glu_b', 'delta_out_norm_pool_g', 'delta_out_norm_ssm_g', 'delta_w_out', 'delta_norm_ffn_g', 'delta_w_up', 'delta_conv_w', 'delta_conv_b', 'delta_w_down', 'delta_final_norm_g', 'new_m_norm_mix_g', 'new_m_w_in', 'new_m_pool_w', 'new_m_pool_scale', 'new_m_ssm_log_neg_a_re', 'new_m_ssm_a_im', 'new_m_ssm_log_dt', 'new_m_ssm_b_re', 'new_m_ssm_b_im', 'new_m_ssm_c_re', 'new_m_ssm_c_im', 'new_m_ssm_d', 'new_m_glu_w', 'new_m_glu_b', 'new_m_out_norm_pool_g', 'new_m_out_norm_ssm_g', 'new_m_w_out', 'new_m_norm_ffn_g', 'new_m_w_up', 'new_m_conv_w', 'new_m_conv_b', 'new_m_w_down', 'new_m_final_norm_g', 'new_v_norm_mix_g', 'new_v_w_in', 'new_v_pool_w', 'new_v_pool_scale', 'new_v_ssm_log_neg_a_re', 'new_v_ssm_a_im', 'new_v_ssm_log_dt', 'new_v_ssm_b_re', 'new_v_ssm_b_im', 'new_v_ssm_c_re', 'new_v_ssm_c_im', 'new_v_ssm_d', 'new_v_glu_w', 'new_v_glu_b', 'new_v_out_norm_pool_g', 'new_v_out_norm_ssm_g', 'new_v_w_out', 'new_v_norm_ffn_g', 'new_v_w_up', 'new_v_conv_w', 'new_v_conv_b', 'new_v_w_down', 'new_v_final_norm_g']
TWIN_LEAF_KINDS = {'loss': 'loss', 'grad_x': 'grad_x', 'grad_norm_mix_g': 'grad_w', 'grad_w_in': 'grad_w', 'grad_pool_w': 'grad_w', 'grad_pool_scale': 'grad_w', 'grad_ssm_log_neg_a_re': 'grad_w', 'grad_ssm_a_im': 'grad_w', 'grad_ssm_log_dt': 'grad_w', 'grad_ssm_b_re': 'grad_w', 'grad_ssm_b_im': 'grad_w', 'grad_ssm_c_re': 'grad_w', 'grad_ssm_c_im': 'grad_w', 'grad_ssm_d': 'grad_w', 'grad_glu_w': 'grad_w', 'grad_glu_b': 'grad_w', 'grad_out_norm_pool_g': 'grad_w', 'grad_out_norm_ssm_g': 'grad_w', 'grad_w_out': 'grad_w', 'grad_norm_ffn_g': 'grad_w', 'grad_w_up': 'grad_w', 'grad_conv_w': 'grad_w', 'grad_conv_b': 'grad_w', 'grad_w_down': 'grad_w', 'grad_final_norm_g': 'grad_w', 'delta_norm_mix_g': 'delta_w', 'delta_w_in': 'delta_w', 'delta_pool_w': 'delta_w', 'delta_pool_scale': 'delta_w', 'delta_ssm_log_neg_a_re': 'delta_w', 'delta_ssm_a_im': 'delta_w', 'delta_ssm_log_dt': 'delta_w', 'delta_ssm_b_re': 'delta_w', 'delta_ssm_b_im': 'delta_w', 'delta_ssm_c_re': 'delta_w', 'delta_ssm_c_im': 'delta_w', 'delta_ssm_d': 'delta_w', 'delta_glu_w': 'delta_w', 'delta_glu_b': 'delta_w', 'delta_out_norm_pool_g': 'delta_w', 'delta_out_norm_ssm_g': 'delta_w', 'delta_w_out': 'delta_w', 'delta_norm_ffn_g': 'delta_w', 'delta_w_up': 'delta_w', 'delta_conv_w': 'delta_w', 'delta_conv_b': 'delta_w', 'delta_w_down': 'delta_w', 'delta_final_norm_g': 'delta_w', 'new_m_norm_mix_g': 'new_m', 'new_m_w_in': 'new_m', 'new_m_pool_w': 'new_m', 'new_m_pool_scale': 'new_m', 'new_m_ssm_log_neg_a_re': 'new_m', 'new_m_ssm_a_im': 'new_m', 'new_m_ssm_log_dt': 'new_m', 'new_m_ssm_b_re': 'new_m', 'new_m_ssm_b_im': 'new_m', 'new_m_ssm_c_re': 'new_m', 'new_m_ssm_c_im': 'new_m', 'new_m_ssm_d': 'new_m', 'new_m_glu_w': 'new_m', 'new_m_glu_b': 'new_m', 'new_m_out_norm_pool_g': 'new_m', 'new_m_out_norm_ssm_g': 'new_m', 'new_m_w_out': 'new_m', 'new_m_norm_ffn_g': 'new_m', 'new_m_w_up': 'new_m', 'new_m_conv_w': 'new_m', 'new_m_conv_b': 'new_m', 'new_m_w_down': 'new_m', 'new_m_final_norm_g': 'new_m', 'new_v_norm_mix_g': 'new_v', 'new_v_w_in': 'new_v', 'new_v_pool_w': 'new_v', 'new_v_pool_scale': 'new_v', 'new_v_ssm_log_neg_a_re': 'new_v', 'new_v_ssm_a_im': 'new_v', 'new_v_ssm_log_dt': 'new_v', 'new_v_ssm_b_re': 'new_v', 'new_v_ssm_b_im': 'new_v', 'new_v_ssm_c_re': 'new_v', 'new_v_ssm_c_im': 'new_v', 'new_v_ssm_d': 'new_v', 'new_v_glu_w': 'new_v', 'new_v_glu_b': 'new_v', 'new_v_out_norm_pool_g': 'new_v', 'new_v_out_norm_ssm_g': 'new_v', 'new_v_w_out': 'new_v', 'new_v_norm_ffn_g': 'new_v', 'new_v_w_up': 'new_v', 'new_v_conv_w': 'new_v', 'new_v_conv_b': 'new_v', 'new_v_w_down': 'new_v', 'new_v_final_norm_g': 'new_v'}


def _forward(args):
    return _fwd_reference(*[args[k] for k in FWD_PARAMS])


def _output_shape():
    def fwd():
        inp = _fwd_setup_inputs(0)
        return _fwd_reference(*[inp[k] for k in FWD_PARAMS])
    out = _jax.eval_shape(fwd)
    return out.shape, out.dtype

N_MICROBATCH = 1
ADAM_LR = 0.001
ADAM_B1 = 0.9
ADAM_B2 = 0.999
ADAM_EPS = 1e-08
ADAM_WD = 0.01
ADAM_STEP = 10
PER_EXAMPLE_BATCH_AXIS = {'x': 0, 'loss_target': 0}
SHARED_INPUTS = []
_WEIGHT_DTYPES = {'norm_mix_g': _jnp.float32, 'w_in': _jnp.float32, 'pool_w': _jnp.float32, 'pool_scale': _jnp.float32, 'ssm_log_neg_a_re': _jnp.float32, 'ssm_a_im': _jnp.float32, 'ssm_log_dt': _jnp.float32, 'ssm_b_re': _jnp.float32, 'ssm_b_im': _jnp.float32, 'ssm_c_re': _jnp.float32, 'ssm_c_im': _jnp.float32, 'ssm_d': _jnp.float32, 'glu_w': _jnp.float32, 'glu_b': _jnp.float32, 'out_norm_pool_g': _jnp.float32, 'out_norm_ssm_g': _jnp.float32, 'w_out': _jnp.float32, 'norm_ffn_g': _jnp.float32, 'w_up': _jnp.float32, 'conv_w': _jnp.float32, 'conv_b': _jnp.float32, 'w_down': _jnp.float32, 'final_norm_g': _jnp.float32}
MOMENT_SCALE = {'norm_mix_g': 1.399845e-01, 'w_in': 1.414806e-01, 'pool_w': 1.370914e-01, 'pool_scale': 1.388738e-01, 'ssm_log_neg_a_re': 3.847026e-03, 'ssm_a_im': 6.727134e-03, 'ssm_log_dt': 3.895971e+00, 'ssm_b_re': 4.623115e-03, 'ssm_b_im': 4.717535e-03, 'ssm_c_re': 9.529098e-03, 'ssm_c_im': 9.360025e-03, 'ssm_d': 1.703974e-01, 'glu_w': 3.801253e-02, 'glu_b': 5.921651e-02, 'out_norm_pool_g': 1.395128e-01, 'out_norm_ssm_g': 1.410463e-01, 'w_out': 1.452304e-01, 'norm_ffn_g': 1.036418e-01, 'w_up': 4.238675e-02, 'conv_w': 4.295419e-02, 'conv_b': 4.310125e-02, 'w_down': 6.985060e-02, 'final_norm_g': 3.210010e+01}


def _to_microbatches(a, axis):
    t = _jnp.moveaxis(a, axis, 0)
    t = t.reshape((N_MICROBATCH, t.shape[0] // N_MICROBATCH) + t.shape[1:])
    return _jnp.moveaxis(t, 1, axis + 1)


def setup_inputs(seed: int = 0) -> dict:
    inp = _fwd_setup_inputs(seed)
    key = _jax.random.fold_in(_jax.random.key(seed), 7919)
    shape, _ = _output_shape()
    out = dict(inp)
    out["loss_target"] = _jax.random.normal(_jax.random.fold_in(key, 0), shape, _jnp.float32)
    for i, name in enumerate(TWIN_WEIGHTS):
        w = inp[name].astype(_jnp.float32)
        if MOMENT_SCALE is None:
            s = _jnp.sqrt(_jnp.mean(_jnp.square(w)) + 1e-30)
        else:
            s = MOMENT_SCALE[name]
        km, kv = _jax.random.split(_jax.random.fold_in(key, i + 1))
        out[name] = w
        out["m_" + name] = s * _jax.random.normal(km, w.shape, _jnp.float32)
        out["v_" + name] = (s * s) * _jax.random.uniform(kv, w.shape, _jnp.float32, 0.5, 1.5)
    if N_MICROBATCH > 1:
        for name, axis in PER_EXAMPLE_BATCH_AXIS.items():
            out[name] = _to_microbatches(out[name], axis)
    return {'x': out['x'], 'norm_mix_g': out['norm_mix_g'], 'w_in': out['w_in'], 'pool_w': out['pool_w'], 'pool_scale': out['pool_scale'], 'ssm_log_neg_a_re': out['ssm_log_neg_a_re'], 'ssm_a_im': out['ssm_a_im'], 'ssm_log_dt': out['ssm_log_dt'], 'ssm_b_re': out['ssm_b_re'], 'ssm_b_im': out['ssm_b_im'], 'ssm_c_re': out['ssm_c_re'], 'ssm_c_im': out['ssm_c_im'], 'ssm_d': out['ssm_d'], 'glu_w': out['glu_w'], 'glu_b': out['glu_b'], 'out_norm_pool_g': out['out_norm_pool_g'], 'out_norm_ssm_g': out['out_norm_ssm_g'], 'w_out': out['w_out'], 'norm_ffn_g': out['norm_ffn_g'], 'w_up': out['w_up'], 'conv_w': out['conv_w'], 'conv_b': out['conv_b'], 'w_down': out['w_down'], 'final_norm_g': out['final_norm_g'], 'loss_target': out['loss_target'], 'm_norm_mix_g': out['m_norm_mix_g'], 'm_w_in': out['m_w_in'], 'm_pool_w': out['m_pool_w'], 'm_pool_scale': out['m_pool_scale'], 'm_ssm_log_neg_a_re': out['m_ssm_log_neg_a_re'], 'm_ssm_a_im': out['m_ssm_a_im'], 'm_ssm_log_dt': out['m_ssm_log_dt'], 'm_ssm_b_re': out['m_ssm_b_re'], 'm_ssm_b_im': out['m_ssm_b_im'], 'm_ssm_c_re': out['m_ssm_c_re'], 'm_ssm_c_im': out['m_ssm_c_im'], 'm_ssm_d': out['m_ssm_d'], 'm_glu_w': out['m_glu_w'], 'm_glu_b': out['m_glu_b'], 'm_out_norm_pool_g': out['m_out_norm_pool_g'], 'm_out_norm_ssm_g': out['m_out_norm_ssm_g'], 'm_w_out': out['m_w_out'], 'm_norm_ffn_g': out['m_norm_ffn_g'], 'm_w_up': out['m_w_up'], 'm_conv_w': out['m_conv_w'], 'm_conv_b': out['m_conv_b'], 'm_w_down': out['m_w_down'], 'm_final_norm_g': out['m_final_norm_g'], 'v_norm_mix_g': out['v_norm_mix_g'], 'v_w_in': out['v_w_in'], 'v_pool_w': out['v_pool_w'], 'v_pool_scale': out['v_pool_scale'], 'v_ssm_log_neg_a_re': out['v_ssm_log_neg_a_re'], 'v_ssm_a_im': out['v_ssm_a_im'], 'v_ssm_log_dt': out['v_ssm_log_dt'], 'v_ssm_b_re': out['v_ssm_b_re'], 'v_ssm_b_im': out['v_ssm_b_im'], 'v_ssm_c_re': out['v_ssm_c_re'], 'v_ssm_c_im': out['v_ssm_c_im'], 'v_ssm_d': out['v_ssm_d'], 'v_glu_w': out['v_glu_w'], 'v_glu_b': out['v_glu_b'], 'v_out_norm_pool_g': out['v_out_norm_pool_g'], 'v_out_norm_ssm_g': out['v_out_norm_ssm_g'], 'v_w_out': out['v_w_out'], 'v_norm_ffn_g': out['v_norm_ffn_g'], 'v_w_up': out['v_w_up'], 'v_conv_w': out['v_conv_w'], 'v_conv_b': out['v_conv_b'], 'v_w_down': out['v_w_down'], 'v_final_norm_g': out['v_final_norm_g']}


def _loss(weights, diff, rest, loss_target):
    with _jax.named_scope("forward"):
        args = {**rest, TWIN_DIFF_INPUT: diff, **{k: w.astype(_WEIGHT_DTYPES[k]) for k, w in weights.items()}}
        y = _forward(args)
    with _jax.named_scope("loss_head"):
        err = _jnp.square(y.astype(_jnp.float32) - loss_target)
        return 0.5 * _jnp.sum(_jnp.mean(err, axis=-1)) if err.ndim else 0.5 * err


def _adamw(w, g, m, v):
    m = ADAM_B1 * m + (1.0 - ADAM_B1) * g
    v = ADAM_B2 * v + (1.0 - ADAM_B2) * _jnp.square(g)
    m_hat = m / (1.0 - ADAM_B1 ** ADAM_STEP)
    v_hat = v / (1.0 - ADAM_B2 ** ADAM_STEP)
    delta = -ADAM_LR * (m_hat / (_jnp.sqrt(v_hat) + ADAM_EPS) + ADAM_WD * w)
    return delta, m, v


def reference(x, norm_mix_g, w_in, pool_w, pool_scale, ssm_log_neg_a_re, ssm_a_im, ssm_log_dt, ssm_b_re, ssm_b_im, ssm_c_re, ssm_c_im, ssm_d, glu_w, glu_b, out_norm_pool_g, out_norm_ssm_g, w_out, norm_ffn_g, w_up, conv_w, conv_b, w_down, final_norm_g, loss_target, m_norm_mix_g, m_w_in, m_pool_w, m_pool_scale, m_ssm_log_neg_a_re, m_ssm_a_im, m_ssm_log_dt, m_ssm_b_re, m_ssm_b_im, m_ssm_c_re, m_ssm_c_im, m_ssm_d, m_glu_w, m_glu_b, m_out_norm_pool_g, m_out_norm_ssm_g, m_w_out, m_norm_ffn_g, m_w_up, m_conv_w, m_conv_b, m_w_down, m_final_norm_g, v_norm_mix_g, v_w_in, v_pool_w, v_pool_scale, v_ssm_log_neg_a_re, v_ssm_a_im, v_ssm_log_dt, v_ssm_b_re, v_ssm_b_im, v_ssm_c_re, v_ssm_c_im, v_ssm_d, v_glu_w, v_glu_b, v_out_norm_pool_g, v_out_norm_ssm_g, v_w_out, v_norm_ffn_g, v_w_up, v_conv_w, v_conv_b, v_w_down, v_final_norm_g):
    given = dict(x=x, norm_mix_g=norm_mix_g, w_in=w_in, pool_w=pool_w, pool_scale=pool_scale, ssm_log_neg_a_re=ssm_log_neg_a_re, ssm_a_im=ssm_a_im, ssm_log_dt=ssm_log_dt, ssm_b_re=ssm_b_re, ssm_b_im=ssm_b_im, ssm_c_re=ssm_c_re, ssm_c_im=ssm_c_im, ssm_d=ssm_d, glu_w=glu_w, glu_b=glu_b, out_norm_pool_g=out_norm_pool_g, out_norm_ssm_g=out_norm_ssm_g, w_out=w_out, norm_ffn_g=norm_ffn_g, w_up=w_up, conv_w=conv_w, conv_b=conv_b, w_down=w_down, final_norm_g=final_norm_g, loss_target=loss_target, m_norm_mix_g=m_norm_mix_g, m_w_in=m_w_in, m_pool_w=m_pool_w, m_pool_scale=m_pool_scale, m_ssm_log_neg_a_re=m_ssm_log_neg_a_re, m_ssm_a_im=m_ssm_a_im, m_ssm_log_dt=m_ssm_log_dt, m_ssm_b_re=m_ssm_b_re, m_ssm_b_im=m_ssm_b_im, m_ssm_c_re=m_ssm_c_re, m_ssm_c_im=m_ssm_c_im, m_ssm_d=m_ssm_d, m_glu_w=m_glu_w, m_glu_b=m_glu_b, m_out_norm_pool_g=m_out_norm_pool_g, m_out_norm_ssm_g=m_out_norm_ssm_g, m_w_out=m_w_out, m_norm_ffn_g=m_norm_ffn_g, m_w_up=m_w_up, m_conv_w=m_conv_w, m_conv_b=m_conv_b, m_w_down=m_w_down, m_final_norm_g=m_final_norm_g, v_norm_mix_g=v_norm_mix_g, v_w_in=v_w_in, v_pool_w=v_pool_w, v_pool_scale=v_pool_scale, v_ssm_log_neg_a_re=v_ssm_log_neg_a_re, v_ssm_a_im=v_ssm_a_im, v_ssm_log_dt=v_ssm_log_dt, v_ssm_b_re=v_ssm_b_re, v_ssm_b_im=v_ssm_b_im, v_ssm_c_re=v_ssm_c_re, v_ssm_c_im=v_ssm_c_im, v_ssm_d=v_ssm_d, v_glu_w=v_glu_w, v_glu_b=v_glu_b, v_out_norm_pool_g=v_out_norm_pool_g, v_out_norm_ssm_g=v_out_norm_ssm_g, v_w_out=v_w_out, v_norm_ffn_g=v_norm_ffn_g, v_w_up=v_w_up, v_conv_w=v_conv_w, v_conv_b=v_conv_b, v_w_down=v_w_down, v_final_norm_g=v_final_norm_g)
    weights = {n: given[n] for n in TWIN_WEIGHTS}
    shared = {n: given[n] for n in SHARED_INPUTS}
    per_example = {n: given[n] for n in ['x']}
    grad_fn = _jax.value_and_grad(_loss, argnums=(0, 1))

    def one_microbatch(ex, loss_target):
        ex = dict(ex)
        diff = ex.pop(TWIN_DIFF_INPUT)
        return grad_fn(weights, diff, {**shared, **ex}, loss_target)

    if N_MICROBATCH == 1:
        loss, (grad_w, grad_x) = one_microbatch(per_example, given["loss_target"])
    else:
        def body(carry, xs):
            loss_sum, grad_sum = carry
            l_k, (gw_k, gx_k) = one_microbatch(xs[0], xs[1])
            with _jax.named_scope("update"):
                return (loss_sum + l_k, _jax.tree.map(_jnp.add, grad_sum, gw_k)), gx_k

        init = (_jnp.zeros((), _jnp.float32), _jax.tree.map(_jnp.zeros_like, weights))
        (loss, grad_w), grad_x = _jax.lax.scan(body, init, (per_example, given["loss_target"]))
    with _jax.named_scope("update"):
        delta_w, new_m, new_v = {}, {}, {}
        for n in TWIN_WEIGHTS:
            delta_w[n], new_m[n], new_v[n] = _adamw(weights[n], grad_w[n], given["m_" + n], given["v_" + n])
    return (loss, grad_x, *[grad_w[n] for n in TWIN_WEIGHTS], *[delta_w[n] for n in TWIN_WEIGHTS],
            *[new_m[n] for n in TWIN_WEIGHTS], *[new_v[n] for n in TWIN_WEIGHTS])
```

```python
import functools

import jax
import jax.numpy as jnp
import numpy as np
from jax import lax
from jax.experimental import pallas as pl
from jax.experimental.pallas import tpu as pltpu

F32 = jnp.float32
BF16 = jnp.bfloat16
MESH_ID = pl.DeviceIdType.MESH

N_DEV = 8
EPS = 1e-6
POOL_WINDOWS = (2, 4, 8, 16)
POOL_GROUP = 128
POOL_PAD = 64
SSM_GROUP = 16
SSM_STATE = 64
SSM_BLOCK_GROUPS = 8
N_CHUNK = 8
LANES = 128
PACK_W = 1024
VMEM_LIMIT = 56 * 1024 * 1024

ADAM_LR = 0.001
ADAM_B1 = 0.9
ADAM_B2 = 0.999
ADAM_EPS = 1e-08
ADAM_WD = 0.01
ADAM_STEP = 10

WEIGHTS = ['norm_mix_g', 'w_in', 'pool_w', 'pool_scale', 'ssm_log_neg_a_re', 'ssm_a_im', 'ssm_log_dt',
           'ssm_b_re', 'ssm_b_im', 'ssm_c_re', 'ssm_c_im', 'ssm_d', 'glu_w', 'glu_b', 'out_norm_pool_g',
           'out_norm_ssm_g', 'w_out', 'norm_ffn_g', 'w_up', 'conv_w', 'conv_b', 'w_down', 'final_norm_g']
SHARDED = ('w_in', 'w_out', 'w_down', 'glu_w', 'w_up', 'conv_w')
REPLICATED = tuple(n for n in WEIGHTS if n not in SHARDED)


def _pick(n, prefs):
    for p in prefs:
        if n % p == 0:
            return p
    return n


def _params(sem, vmem=None):
    return pltpu.CompilerParams(dimension_semantics=sem, vmem_limit_bytes=vmem or VMEM_LIMIT)


def _mm(name, a, b, mode, out_dtype, add=None):
    if mode == 'nn':
        (M, K), (_, N) = a.shape, b.shape
    elif mode == 'nt':
        (M, K), (N, _) = a.shape, b.shape
    else:
        (K, M), (_, N) = a.shape, b.shape
    tm = _pick(M, (512, 256, 128))
    tn = _pick(N, (512, 256, 128))
    tk = _pick(K, (1024, 512, 256, 128))
    nk = K // tk
    dims = {'nn': (((1,), (0,)), ((), ())), 'nt': (((1,), (1,)), ((), ())), 'tn': (((0,), (0,)), ((), ()))}[mode]

    def kern(*refs):
        if add is None:
            a_ref, b_ref, o_ref, acc = refs
            add_ref = None
        else:
            a_ref, b_ref, add_ref, o_ref, acc = refs
        k = pl.program_id(2)
        part = lax.dot_general(a_ref[...].astype(BF16), b_ref[...].astype(BF16), dims, preferred_element_type=F32)

        @pl.when(k == 0)
        def _():
            acc[...] = part

        @pl.when(k != 0)
        def _():
            acc[...] += part

        @pl.when(k == nk - 1)
        def _():
            r = acc[...]
            if add_ref is not None:
                r = r + add_ref[...]
            o_ref[...] = r.astype(o_ref.dtype)

    a_spec = pl.BlockSpec((tk, tm), lambda i, j, k: (k, i)) if mode == 'tn' else pl.BlockSpec((tm, tk), lambda i, j, k: (i, k))
    b_spec = pl.BlockSpec((tn, tk), lambda i, j, k: (j, k)) if mode == 'nt' else pl.BlockSpec((tk, tn), lambda i, j, k: (k, j))
    in_specs = [a_spec, b_spec]
    args = [a, b]
    if add is not None:
        in_specs.append(pl.BlockSpec((tm, tn), lambda i, j, k: (i, j)))
        args.append(add)
    return pl.pallas_call(
        kern, name=name, grid=(M // tm, N // tn, nk), in_specs=in_specs,
        out_specs=pl.BlockSpec((tm, tn), lambda i, j, k: (i, j)),
        out_shape=jax.ShapeDtypeStruct((M, N), out_dtype),
        scratch_shapes=[pltpu.VMEM((tm, tn), F32)],
        compiler_params=_params(("parallel", "parallel", "arbitrary")),
    )(*args)


def _rowmap(name, fn, rows, fulls, out_rows, out_accs=(), tm=256):
    T = rows[0].shape[0]
    tm = min(tm, T)
    assert T % tm == 0
    n_in, n_row = len(rows) + len(fulls), len(out_rows)

    def kern(*refs):
        i = pl.program_id(0)
        res = fn(*[r[...] for r in refs[:n_in]])
        res = res if isinstance(res, (tuple, list)) else (res,)
        outs = refs[n_in:]
        for o, v in zip(outs[:n_row], res[:n_row]):
            o[...] = v.astype(o.dtype)
        for o, v in zip(outs[n_row:], res[n_row:]):
            @pl.when(i == 0)
            def _(o=o, v=v):
                o[...] = v

            @pl.when(i != 0)
            def _(o=o, v=v):
                o[...] += v

    def full_spec(shape):
        nd = len(shape)
        return pl.BlockSpec(tuple(shape), lambda i: (0,) * nd)

    in_specs = [pl.BlockSpec((tm, r.shape[1]), lambda i: (i, 0)) for r in rows] + [full_spec(f.shape) for f in fulls]
    out_specs = [pl.BlockSpec((tm, c), lambda i: (i, 0)) for c, _ in out_rows] + [full_spec(s) for s in out_accs]
    out_shape = [jax.ShapeDtypeStruct((T, c), dt) for c, dt in out_rows] + [jax.ShapeDtypeStruct(tuple(s), F32) for s in out_accs]
    return pl.pallas_call(
        kern, name=name, grid=(T // tm,), in_specs=in_specs, out_specs=out_specs, out_shape=out_shape,
        compiler_params=_params(("arbitrary",)),
    )(*rows, *fulls)


def _colsum(v):
    return jnp.sum(v, axis=0, keepdims=True)


def _rowmean(v):
    return jnp.mean(v, axis=-1, keepdims=True)


def _rms_fwd(x, g):
    r = lax.rsqrt(_rowmean(x * x) + EPS)
    return x * r * g


def _rms_bwd(x, g, dy):
    r = lax.rsqrt(_rowmean(x * x) + EPS)
    xh = x * r
    dxh = dy * g
    return r * (dxh - xh * _rowmean(dxh * xh)), _colsum(dy * xh)


def _gelu(y):
    c = np.sqrt(2.0 / np.pi).astype(np.float32)
    return 0.5 * y * (1.0 + jnp.tanh(c * (y + 0.044715 * (y * y * y))))


def _gelu_grad(y):
    c = np.sqrt(2.0 / np.pi).astype(np.float32)
    th = jnp.tanh(c * (y + 0.044715 * (y * y * y)))
    return 0.5 * (1.0 + th) + 0.5 * y * (1.0 - th * th) * c * (1.0 + 3.0 * 0.044715 * (y * y))


def _split3(v):
    hi = v.astype(BF16)
    r1 = v - hi.astype(F32)
    mid = r1.astype(BF16)
    lo = (r1 - mid.astype(F32)).astype(BF16)
    return hi, mid, lo


def _band(tm, lo_off, hi_off):
    shape = (tm, tm + 2 * POOL_PAD)
    d = lax.broadcasted_iota(jnp.int32, shape, 1) - lax.broadcasted_iota(jnp.int32, shape, 0) - POOL_PAD
    return ((d >= lo_off) & (d < hi_off)).astype(BF16)


def _band_sum(band, ext):
    hi, mid, lo = _split3(ext)
    dot = functools.partial(jnp.dot, preferred_element_type=F32)
    return dot(band, hi) + dot(band, mid) + dot(band, lo)


def _window_count(r0, tm, half, T):
    t = r0 + lax.broadcasted_iota(jnp.int32, (tm, 1), 0)
    return (jnp.minimum(t + half, T) - jnp.maximum(t - half, 0)).astype(F32)


def _pool_block(u_ref, w_ref, r0, tm, T):
    ext = u_ref[pl.ds(r0, tm + 2 * POOL_PAD), :]
    ctr = u_ref[pl.ds(r0 + POOL_PAD, tm), :]
    pooled, conc = [], []
    for gi, w in enumerate(POOL_WINDOWS):
        half = w // 2
        cols = slice(gi * POOL_GROUP, (gi + 1) * POOL_GROUP)
        ws = _band_sum(_band(tm, -half, half), ext[:, cols])
        p = ws / _window_count(r0, tm, half, T) - ctr[:, cols]
        pooled.append(p)
        conc.append(jnp.dot(p.astype(BF16), w_ref[gi], preferred_element_type=F32))
    return pooled, jnp.concatenate(conc, axis=1)


def _pool_fwd(u_pad, pool_w_bf, pool_scale, g_pool, T, tm):
    nw = len(POOL_WINDOWS)
    C = nw * POOL_GROUP

    def kern(u_ref, w_ref, sc_ref, g_ref, o_ref):
        r0 = pl.multiple_of(pl.program_id(0) * tm, tm)
        _, conc = _pool_block(u_ref, w_ref, r0, tm, T)
        o_ref[...] = _rms_fwd(conc * sc_ref[...], g_ref[...]).astype(o_ref.dtype)

    return pl.pallas_call(
        kern, name="pool_fwd", grid=(T // tm,),
        in_specs=[pl.BlockSpec(u_pad.shape, lambda i: (0, 0)), pl.BlockSpec(pool_w_bf.shape, lambda i: (0, 0, 0)),
                  pl.BlockSpec((1, C), lambda i: (0, 0)), pl.BlockSpec((1, C), lambda i: (0, 0))],
        out_specs=pl.BlockSpec((tm, C), lambda i: (i, 0)),
        out_shape=jax.ShapeDtypeStruct((T, C), BF16),
        compiler_params=_params(("arbitrary",)),
    )(u_pad, pool_w_bf, pool_scale, g_pool)


def _pool_bwd_rows(u_pad, d_ypn, pool_w_bf, pool_scale, g_pool, T, tm):
    nw = len(POOL_WINDOWS)
    C = nw * POOL_GROUP

    def kern(u_ref, dy_ref, w_ref, sc_ref, g_ref, q_ref, dp_ref, dw_ref, dsc_ref, dg_ref):
        i = pl.program_id(0)
        r0 = pl.multiple_of(i * tm, tm)
        pooled, conc = _pool_block(u_ref, w_ref, r0, tm, T)
        sc = sc_ref[...]
        dyp, dg = _rms_bwd(conc * sc, g_ref[...], dy_ref[...])
        dsc = _colsum(dyp * conc)
        dconc = (dyp * sc).astype(BF16)
        dws = []
        for gi, w in enumerate(POOL_WINDOWS):
            cols = slice(gi * POOL_GROUP, (gi + 1) * POOL_GROUP)
            dc = dconc[:, cols]
            dp = lax.dot_general(dc, w_ref[gi], (((1,), (1,)), ((), ())), preferred_element_type=F32)
            dp_ref[:, cols] = dp
            q_ref[:, cols] = dp / _window_count(r0, tm, w // 2, T)
            dws.append(lax.dot_general(pooled[gi].astype(BF16), dc, (((0,), (0,)), ((), ())), preferred_element_type=F32))

        @pl.when(i == 0)
        def _():
            for gi in range(nw):
                dw_ref[gi] = dws[gi]
            dsc_ref[...] = dsc
            dg_ref[...] = dg

        @pl.when(i != 0)
        def _():
            for gi in range(nw):
                dw_ref[gi] += dws[gi]
            dsc_ref[...] += dsc
            dg_ref[...] += dg

    full2 = pl.BlockSpec((1, C), lambda i: (0, 0))
    return pl.pallas_call(
        kern, name="pool_bwd_rows", grid=(T // tm,),
        in_specs=[pl.BlockSpec(u_pad.shape, lambda i: (0, 0)), pl.BlockSpec((tm, C), lambda i: (i, 0)),
                  pl.BlockSpec(pool_w_bf.shape, lambda i: (0, 0, 0)), full2, full2],
        out_specs=[pl.BlockSpec((tm, C), lambda i: (i, 0)), pl.BlockSpec((tm, C), lambda i: (i, 0)),
                   pl.BlockSpec((nw, POOL_GROUP, POOL_GROUP), lambda i: (0, 0, 0)), full2, full2],
        out_shape=[jax.ShapeDtypeStruct((T, C), F32), jax.ShapeDtypeStruct((T, C), F32),
                   jax.ShapeDtypeStruct((nw, POOL_GROUP, POOL_GROUP), F32),
                   jax.ShapeDtypeStruct((1, C), F32), jax.ShapeDtypeStruct((1, C), F32)],
        compiler_params=_params(("arbitrary",)),
    )(u_pad, d_ypn, pool_w_bf, pool_scale, g_pool)


def _pool_bwd_band(q_pad, dpooled, T, tm):
    C = dpooled.shape[1]

    def kern(q_ref, dp_ref, o_ref):
        r0 = pl.multiple_of(pl.program_id(0) * tm, tm)
        ext = q_ref[pl.ds(r0, tm + 2 * POOL_PAD), :]
        for gi, w in enumerate(POOL_WINDOWS):
            half = w // 2
            cols = slice(gi * POOL_GROUP, (gi + 1) * POOL_GROUP)
            o_ref[:, cols] = _band_sum(_band(tm, -half + 1, half + 1), ext[:, cols]) - dp_ref[:, cols]

    return pl.pallas_call(
        kern, name="pool_bwd_band", grid=(T // tm,),
        in_specs=[pl.BlockSpec(q_pad.shape, lambda i: (0, 0)), pl.BlockSpec((tm, C), lambda i: (i, 0))],
        out_specs=pl.BlockSpec((tm, C), lambda i: (i, 0)),
        out_shape=jax.ShapeDtypeStruct((T, C), F32),
        compiler_params=_params(("arbitrary",)),
    )(q_pad, dpooled)


def _cmul(ar, ai, br, bi):
    return ar * br - ai * bi, ar * bi + ai * br


def _ssm_discretise(log_neg_a_re, a_im, log_dt):
    dt = jnp.exp(log_dt)
    a_re = -jnp.exp(log_neg_a_re)
    mag = jnp.exp(a_re * dt)
    ang = a_im * dt
    lam_re, lam_im = mag * jnp.cos(ang), mag * jnp.sin(ang)
    den = a_re * a_re + a_im * a_im
    f_re = ((lam_re - 1.0) * a_re + lam_im * a_im) / den
    f_im = (lam_im * a_re - (lam_re - 1.0) * a_im) / den
    return lam_re, lam_im, f_re, f_im


def _ssm_params_fwd(log_neg_a_re, a_im, log_dt):
    rows, n = log_neg_a_re.shape

    def kern(a_ref, b_ref, c_ref, o1, o2, o3, o4):
        for o, v in zip((o1, o2, o3, o4), _ssm_discretise(a_ref[...], b_ref[...], c_ref[...])):
            o[...] = v

    return pl.pallas_call(kern, name="ssm_params_fwd", out_shape=[jax.ShapeDtypeStruct((rows, n), F32)] * 4)(
        log_neg_a_re, a_im, log_dt)


def _ssm_params_bwd(log_neg_a_re, a_im, log_dt, cots):
    rows, n = log_neg_a_re.shape

    def kern(a_ref, b_ref, c_ref, g1, g2, g3, g4, o1, o2, o3):
        _, vjp = jax.vjp(_ssm_discretise, a_ref[...], b_ref[...], c_ref[...])
        d1, d2, d3 = vjp((g1[...], g2[...], g3[...], g4[...]))
        o1[...] = d1
        o2[...] = d2
        o3[...] = d3

    return pl.pallas_call(
        kern, name="ssm_params_bwd",
        out_shape=[jax.ShapeDtypeStruct((rows, n), F32), jax.ShapeDtypeStruct((rows, n), F32),
                   jax.ShapeDtypeStruct((rows, 1), F32)])(log_neg_a_re, a_im, log_dt, *cots)


def _cpow(lr, li, n):
    out = None
    br, bi = lr, li
    while n:
        if n & 1:
            out = (br, bi) if out is None else _cmul(out[0], out[1], br, bi)
        n >>= 1
        if n:
            br, bi = _cmul(br, bi, br, bi)
    return out


def _slab(t):
    return pl.ds(pl.multiple_of(t * N_CHUNK, N_CHUNK), N_CHUNK)


def _shift_chunks(vr, vi, reverse):
    sub = lax.broadcasted_iota(jnp.int32, vr.shape, 0)
    if reverse:
        keep = sub != N_CHUNK - 1
        return jnp.where(keep, pltpu.roll(vr, N_CHUNK - 1, 0), 0.0), jnp.where(keep, pltpu.roll(vi, N_CHUNK - 1, 0), 0.0)
    keep = sub != 0
    return jnp.where(keep, pltpu.roll(vr, 1, 0), 0.0), jnp.where(keep, pltpu.roll(vi, 1, 0), 0.0)


def _chunk_scan(xr, xi, lr, li, Lc, reverse, prev=None):
    C = xr.shape[1]
    lrb, lib = jnp.broadcast_to(lr, (N_CHUNK, C)), jnp.broadcast_to(li, (N_CHUNK, C))
    zero = jnp.zeros((N_CHUNK, C), F32)

    def step_of(k):
        return (Lc - 1 - k) if reverse else k

    def advance(sr, si, t):
        return lrb * sr - lib * si + xr[_slab(t), :], lrb * si + lib * sr + xi[_slab(t), :]

    def pass1(k, c):
        return advance(c[0], c[1], step_of(k))

    er, ei = lax.fori_loop(0, Lc, pass1, (zero, zero), unroll=4)
    pr, pi = _cpow(lr, li, Lc)
    prb, pib = jnp.broadcast_to(pr, (N_CHUNK, C)), jnp.broadcast_to(pi, (N_CHUNK, C))
    cr, ci = zero, zero
    for _ in range(N_CHUNK - 1):
        cr, ci = _shift_chunks(er + prb * cr - pib * ci, ei + prb * ci + pib * cr, reverse)

    if prev is None:
        def pass2(k, c):
            t = step_of(k)
            nr, ni = advance(c[0], c[1], t)
            xr[_slab(t), :] = nr
            xi[_slab(t), :] = ni
            return nr, ni

        lax.fori_loop(0, Lc, pass2, (cr, ci), unroll=4)
        return None

    qr, qi = prev
    def pair(ar, ai, wr, wi, acc):
        return acc[0] + ar * wr + ai * wi, acc[1] + ai * wr - ar * wi

    def pass2(k, c):
        sr, si, accr, acci = c
        t = step_of(k)
        nr, ni = advance(sr, si, t)
        xr[_slab(t), :] = nr
        xi[_slab(t), :] = ni
        tp = t - 1 if reverse else t + 1
        accr, acci = pair(nr, ni, qr[_slab(tp), :], qi[_slab(tp), :], (accr, acci))
        return nr, ni, accr, acci

    sr, si, accr, acci = lax.fori_loop(0, Lc - 1, pass2, (cr, ci, zero, zero), unroll=4)
    t = step_of(Lc - 1)
    nr, ni = advance(sr, si, t)
    xr[_slab(t), :] = nr
    xi[_slab(t), :] = ni
    edge = step_of(0)
    wr, wi = _shift_chunks(qr[_slab(edge), :], qi[_slab(edge), :], not reverse)
    return pair(nr, ni, wr, wi, (accr, acci))


def _ssm_dims(T):
    assert T % (N_CHUNK * 64) == 0
    rc = _pick(T, (512, 256, 128, 64))
    return T // N_CHUNK, rc, T // rc


def _ssm_project_in(u_ref, bre_ref, bim_ref, xr, xi, rc, nrc):
    for c in range(nrc):
        rows = pl.ds(c * rc, rc)
        ub = u_ref[rows, :]
        xr[rows, :] = jnp.dot(ub, bre_ref[...], preferred_element_type=F32)
        xi[rows, :] = jnp.dot(ub, bim_ref[...], preferred_element_type=F32)


def _ssm_fwd(name, u_bf, mats, vecs, reverse):
    T, W = u_bf.shape
    nb = W // LANES
    SC = SSM_BLOCK_GROUPS * SSM_STATE
    Lc, rc, nrc = _ssm_dims(T)
    bre, bim, cre, cim = mats
    lam_re, lam_im, f_re, f_im = vecs

    def kern(u_ref, bre_ref, bim_ref, cre_ref, cim_ref, lr_ref, li_ref, fr_ref, fi_ref, y_ref, xr, xi):
        _ssm_project_in(u_ref, bre_ref, bim_ref, xr, xi, rc, nrc)
        _chunk_scan(xr, xi, lr_ref[...], li_ref[...], Lc, reverse)
        fr, fi = fr_ref[...], fi_ref[...]
        for c in range(nrc):
            rows = pl.ds(c * rc, rc)
            sr, si = _cmul(fr, fi, xr[rows, :], xi[rows, :])
            y_ref[rows, :] = (jnp.dot(sr.astype(BF16), cre_ref[...], preferred_element_type=F32)
                              - jnp.dot(si.astype(BF16), cim_ref[...], preferred_element_type=F32))

    col = pl.BlockSpec((T, LANES), lambda m: (0, m))
    mat_in = pl.BlockSpec((None, LANES, SC), lambda m: (m, 0, 0))
    mat_out = pl.BlockSpec((None, SC, LANES), lambda m: (m, 0, 0))
    vec = pl.BlockSpec((None, 1, SC), lambda m: (m, 0, 0))
    return pl.pallas_call(
        kern, name=name, grid=(nb,),
        in_specs=[col, mat_in, mat_in, mat_out, mat_out, vec, vec, vec, vec],
        out_specs=col, out_shape=jax.ShapeDtypeStruct((T, W), F32),
        scratch_shapes=[pltpu.VMEM((T, SC), F32), pltpu.VMEM((T, SC), F32)],
        compiler_params=_params(("arbitrary",)),
    )(u_bf, bre, bim, cre, cim, lam_re, lam_im, f_re, f_im)


def _ssm_bwd(name, u_bf, dy_bf, du_in, mats, mats_t, vecs, reverse):
    T, W = u_bf.shape
    nb = W // LANES
    SC = SSM_BLOCK_GROUPS * SSM_STATE
    Lc, rc, nrc = _ssm_dims(T)
    bre, bim = mats
    bre_t, bim_t, cre_t, cim_t = mats_t
    lam_re, lam_im, f_re, f_im = vecs
    contract_rows = (((0,), (0,)), ((), ()))

    def kern(u_ref, dy_ref, dui_ref, bre_ref, bim_ref, bret_ref, bimt_ref, cret_ref, cimt_ref,
             lr_ref, li_ref, fr_ref, fi_ref,
             du_ref, dbre_ref, dbim_ref, dcre_ref, dcim_ref, dlr_ref, dli_ref, dfr_ref, dfi_ref,
             xr, xi, gr, gi):
        lr, li, fr, fi = lr_ref[...], li_ref[...], fr_ref[...], fi_ref[...]
        _ssm_project_in(u_ref, bre_ref, bim_ref, xr, xi, rc, nrc)
        _chunk_scan(xr, xi, lr, li, Lc, reverse)
        mre = jnp.zeros((LANES, SC), F32)
        mim = jnp.zeros((LANES, SC), F32)
        dfr = jnp.zeros((1, SC), F32)
        dfi = jnp.zeros((1, SC), F32)
        for c in range(nrc):
            rows = pl.ds(c * rc, rc)
            dy = dy_ref[rows, :]
            x_r, x_i = xr[rows, :], xi[rows, :]
            g_r = jnp.dot(dy, cret_ref[...], preferred_element_type=F32)
            g_i = -jnp.dot(dy, cimt_ref[...], preferred_element_type=F32)
            mre += lax.dot_general(dy, x_r.astype(BF16), contract_rows, preferred_element_type=F32)
            mim += lax.dot_general(dy, x_i.astype(BF16), contract_rows, preferred_element_type=F32)
            dfr += _colsum(g_r * x_r + g_i * x_i)
            dfi += _colsum(g_i * x_r - g_r * x_i)
            gr[rows, :] = fr * g_r + fi * g_i
            gi[rows, :] = fr * g_i - fi * g_r
        dcre_ref[...] = fr * mre - fi * mim
        dcim_ref[...] = -(fr * mim + fi * mre)
        dfr_ref[...] = dfr
        dfi_ref[...] = dfi
        accr, acci = _chunk_scan(gr, gi, lr, -li, Lc, not reverse, prev=(xr, xi))
        dlr_ref[...] = _colsum(accr)
        dli_ref[...] = _colsum(acci)
        dbre = jnp.zeros((LANES, SC), F32)
        dbim = jnp.zeros((LANES, SC), F32)
        for c in range(nrc):
            rows = pl.ds(c * rc, rc)
            ub = u_ref[rows, :]
            a_r, a_i = gr[rows, :].astype(BF16), gi[rows, :].astype(BF16)
            dbre += lax.dot_general(ub, a_r, contract_rows, preferred_element_type=F32)
            dbim += lax.dot_general(ub, a_i, contract_rows, preferred_element_type=F32)
            du_ref[rows, :] = (dui_ref[rows, :] + jnp.dot(a_r, bret_ref[...], preferred_element_type=F32)
                               + jnp.dot(a_i, bimt_ref[...], preferred_element_type=F32))
        dbre_ref[...] = dbre
        dbim_ref[...] = dbim

    col = pl.BlockSpec((T, LANES), lambda m: (0, m))
    mat_in = pl.BlockSpec((None, LANES, SC), lambda m: (m, 0, 0))
    mat_out = pl.BlockSpec((None, SC, LANES), lambda m: (m, 0, 0))
    vec = pl.BlockSpec((None, 1, SC), lambda m: (m, 0, 0))
    mat_shape = jax.ShapeDtypeStruct((nb, LANES, SC), F32)
    vec_shape = jax.ShapeDtypeStruct((nb, 1, SC), F32)
    return pl.pallas_call(
        kern, name=name, grid=(nb,),
        in_specs=[col, col, col, mat_in, mat_in, mat_out, mat_out, mat_in, mat_in, vec, vec, vec, vec],
        out_specs=[col, mat_in, mat_in, mat_in, mat_in, vec, vec, vec, vec],
        out_shape=[jax.ShapeDtypeStruct((T, W), F32), mat_shape, mat_shape, mat_shape, mat_shape,
                   vec_shape, vec_shape, vec_shape, vec_shape],
        scratch_shapes=[pltpu.VMEM((T, SC), F32)] * 4,
        compiler_params=_params(("arbitrary",)),
    )(u_bf, dy_bf, du_in, bre, bim, bre_t, bim_t, cre_t, cim_t, lam_re, lam_im, f_re, f_im)


def _block_diag_in(b):
    d, G, N, H = b.shape
    nb = G // SSM_BLOCK_GROUPS
    eye = jnp.eye(SSM_BLOCK_GROUPS, dtype=b.dtype)
    v = b.reshape(d, nb, SSM_BLOCK_GROUPS, N, H)
    return jnp.einsum('dmgnh,gk->dmghkn', v, eye).reshape(d, nb, SSM_BLOCK_GROUPS * H, SSM_BLOCK_GROUPS * N)


def _block_diag_in_grad(m, N, H):
    d, nb = m.shape[:2]
    eye = jnp.eye(SSM_BLOCK_GROUPS, dtype=m.dtype)
    v = m.reshape(d, nb, SSM_BLOCK_GROUPS, H, SSM_BLOCK_GROUPS, N)
    return jnp.einsum('dmghkn,gk->dmgnh', v, eye).reshape(d, nb * SSM_BLOCK_GROUPS, N, H)


def _block_diag_out(c):
    return jnp.swapaxes(_block_diag_in(jnp.swapaxes(c, 2, 3)), 2, 3)


def _block_diag_out_grad(m, N, H):
    return jnp.swapaxes(_block_diag_in_grad(m, N, H), 2, 3)


def _to_chunk_rows(a):
    T, C = a.shape
    return a.reshape(N_CHUNK, T // N_CHUNK, C).transpose(1, 0, 2).reshape(T, C)


def _from_chunk_rows(a):
    T, C = a.shape
    return a.reshape(T // N_CHUNK, N_CHUNK, C).transpose(1, 0, 2).reshape(T, C)


def _conv_taps(ref, c, R, T):
    n = T // R
    r0 = pl.multiple_of(c * R, R)
    main = ref[pl.ds(r0, R), :]
    before = ref[pl.ds(pl.multiple_of(jnp.maximum(r0 - 8, 0), 8), 8), :][7:8, :]
    after = ref[pl.ds(pl.multiple_of(jnp.minimum(r0 + R, T - 8), 8), 8), :][0:1, :]
    before = jnp.where(c > 0, before, 0.0)
    after = jnp.where(c < n - 1, after, 0.0)
    rid = lax.broadcasted_iota(jnp.int32, main.shape, 0)
    m1 = jnp.where(rid == 0, before, pltpu.roll(main, 1, 0))
    p1 = jnp.where(rid == R - 1, after, pltpu.roll(main, R - 1, 0))
    return m1, main, p1, r0


def _conv_specs(T, F, cb):
    nj = F // cb
    lo = lambda rows: pl.BlockSpec((rows, cb), lambda j: (0, j))
    hi = lambda rows: pl.BlockSpec((rows, cb), lambda j: (0, j + nj))
    return nj, lo, hi


def _convact_fwd(up, conv_w, conv_b, F):
    T = up.shape[0]
    cb = _pick(F, (256, 128))
    R = _pick(T, (256, 128, 64))
    nj, lo, hi = _conv_specs(T, F, cb)

    def kern(uv, ug, wv, wg, bv, bg, o_ref):
        def body(c, carry):
            v1, v0, v2, r0 = _conv_taps(uv, c, R, T)
            g1, g0, g2, _ = _conv_taps(ug, c, R, T)
            val = v1 * wv[0:1, :] + v0 * wv[1:2, :] + v2 * wv[2:3, :] + bv[...]
            gate = g1 * wg[0:1, :] + g0 * wg[1:2, :] + g2 * wg[2:3, :] + bg[...]
            o_ref[pl.ds(r0, R), :] = (val * (gate * jax.nn.sigmoid(gate))).astype(o_ref.dtype)
            return carry

        lax.fori_loop(0, T // R, body, 0)

    return pl.pallas_call(
        kern, name="convact_fwd", grid=(nj,),
        in_specs=[lo(T), hi(T), lo(3), hi(3), lo(1), hi(1)],
        out_specs=lo(T), out_shape=jax.ShapeDtypeStruct((T, F), BF16),
        compiler_params=_params(("arbitrary",)),
    )(up, up, conv_w, conv_w, conv_b, conv_b)


def _convact_bwd(up, dact, conv_w, conv_b, F):
    T = up.shape[0]
    cb = _pick(F, (256, 128))
    R = _pick(T, (256, 128, 64))
    nj, lo, hi = _conv_specs(T, F, cb)

    def kern(uv, ug, da, wv, wg, bv, bg, dupv, dupg, dwv, dwg, dbv, dbg, sv, sg):
        zero = jnp.zeros((1, cb), F32)

        def pass_a(c, acc):
            v1, v0, v2, r0 = _conv_taps(uv, c, R, T)
            g1, g0, g2, _ = _conv_taps(ug, c, R, T)
            val = v1 * wv[0:1, :] + v0 * wv[1:2, :] + v2 * wv[2:3, :] + bv[...]
            gate = g1 * wg[0:1, :] + g0 * wg[1:2, :] + g2 * wg[2:3, :] + bg[...]
            sig = jax.nn.sigmoid(gate)
            d = da[pl.ds(r0, R), :]
            dval = d * (gate * sig)
            dgate = d * val * (sig * (1.0 + gate * (1.0 - sig)))
            sv[pl.ds(r0, R), :] = dval
            sg[pl.ds(r0, R), :] = dgate
            terms = (dval * v1, dval * v0, dval * v2, dval, dgate * g1, dgate * g0, dgate * g2, dgate)
            return tuple(a + _colsum(t) for a, t in zip(acc, terms))

        acc = lax.fori_loop(0, T // R, pass_a, (zero,) * 8)
        for k in range(3):
            dwv[k:k + 1, :] = acc[k]
            dwg[k:k + 1, :] = acc[4 + k]
        dbv[...] = acc[3]
        dbg[...] = acc[7]

        def pass_b(c, carry):
            v1, v0, v2, r0 = _conv_taps(sv, c, R, T)
            g1, g0, g2, _ = _conv_taps(sg, c, R, T)
            dupv[pl.ds(r0, R), :] = (v2 * wv[0:1, :] + v0 * wv[1:2, :] + v1 * wv[2:3, :]).astype(dupv.dtype)
            dupg[pl.ds(r0, R), :] = (g2 * wg[0:1, :] + g0 * wg[1:2, :] + g1 * wg[2:3, :]).astype(dupg.dtype)
            return carry

        lax.fori_loop(0, T // R, pass_b, 0)

    dupv, dupg, dwv, dwg, dbv, dbg = pl.pallas_call(
        kern, name="convact_bwd", grid=(nj,),
        in_specs=[lo(T), hi(T), lo(T), lo(3), hi(3), lo(1), hi(1)],
        out_specs=[lo(T), lo(T), lo(3), lo(3), lo(1), lo(1)],
        out_shape=[jax.ShapeDtypeStruct((T, F), BF16), jax.ShapeDtypeStruct((T, F), BF16),
                   jax.ShapeDtypeStruct((3, F), F32), jax.ShapeDtypeStruct((3, F), F32),
                   jax.ShapeDtypeStruct((1, F), F32), jax.ShapeDtypeStruct((1, F), F32)],
        scratch_shapes=[pltpu.VMEM((T, cb), F32), pltpu.VMEM((T, cb), F32)],
        compiler_params=_params(("arbitrary",)),
    )(up, up, dact, conv_w, conv_w, conv_b, conv_b)
    return (jnp.concatenate([dupv, dupg], axis=1), jnp.concatenate([dwv, dwg], axis=1),
            jnp.concatenate([dbv, dbg], axis=1))


def _exchange(name, bufs, modes):
    n = len(bufs)

    def kern(*refs):
        ins, outs = refs[:n], refs[n:2 * n]
        send_sems, recv_sems, local_sems = refs[2 * n:]
        x, y, c = lax.axis_index("x"), lax.axis_index("y"), lax.axis_index("c")
        me = 4 * x + 2 * y + c
        started = []
        for b in range(n):
            gather = modes[b] == 'gather'
            own = pltpu.make_async_copy(ins[b] if gather else ins[b].at[me], outs[b].at[me], local_sems.at[b])
            own.start()
            started.append(own)
            for k in range(1, N_DEV):
                px = 1 - x if k & 4 else x
                py = 1 - y if k & 2 else y
                pc = 1 - c if k & 1 else c
                peer = 4 * px + 2 * py + pc
                sem = b * (N_DEV - 1) + k - 1
                pltpu.make_async_remote_copy(
                    src_ref=ins[b] if gather else ins[b].at[peer], dst_ref=outs[b].at[me],
                    send_sem=send_sems.at[sem], recv_sem=recv_sems.at[sem],
                    device_id=(px, py, pc), device_id_type=MESH_ID).start()
        for b in range(n):
            gather = modes[b] == 'gather'
            for k in range(1, N_DEV):
                px = 1 - x if k & 4 else x
                py = 1 - y if k & 2 else y
                pc = 1 - c if k & 1 else c
                peer = 4 * px + 2 * py + pc
                sem = b * (N_DEV - 1) + k - 1
                arrived = pltpu.make_async_remote_copy(
                    src_ref=ins[b] if gather else ins[b].at[peer], dst_ref=outs[b].at[peer],
                    send_sem=send_sems.at[sem], recv_sem=recv_sems.at[sem],
                    device_id=(px, py, pc), device_id_type=MESH_ID)
                arrived.wait_recv()
                arrived.wait_send()
        for own in started:
            own.wait()

    hbm = pl.BlockSpec(memory_space=pltpu.HBM)
    out_shape = [jax.ShapeDtypeStruct((N_DEV,) + tuple(b.shape[-2:]), b.dtype) for b in bufs]
    return pl.pallas_call(
        kern, name=name, in_specs=[hbm] * n, out_specs=[hbm] * n, out_shape=out_shape,
        scratch_shapes=[pltpu.SemaphoreType.DMA((n * (N_DEV - 1),)), pltpu.SemaphoreType.DMA((n * (N_DEV - 1),)),
                        pltpu.SemaphoreType.DMA((n,))],
    )(*bufs)


def _sum_slots(name, landed, tr):
    _, R, C = landed.shape

    def kern(x_ref, o_ref):
        s = x_ref[0].astype(F32)
        for q in range(1, N_DEV):
            s = s + x_ref[q].astype(F32)
        o_ref[...] = s

    return pl.pallas_call(
        kern, name=name, grid=(R // tr,),
        in_specs=[pl.BlockSpec((N_DEV, tr, C), lambda i: (0, i, 0))],
        out_specs=pl.BlockSpec((tr, C), lambda i: (i, 0)),
        out_shape=jax.ShapeDtypeStruct((R, C), F32),
        compiler_params=_params(("arbitrary",)),
    )(landed)


def _rows_of(a, width=PACK_W, mult=8):
    flat = a.reshape(-1)
    rows = -(-flat.shape[0] // width)
    rows = -(-rows // mult) * mult
    return jnp.pad(flat, (0, rows * width - flat.shape[0])).reshape(rows, width)


def _adamw(w, g, m, v):
    m2 = ADAM_B1 * m + (1.0 - ADAM_B1) * g
    v2 = ADAM_B2 * v + (1.0 - ADAM_B2) * (g * g)
    m_hat = m2 / (1.0 - ADAM_B1 ** ADAM_STEP)
    v_hat = v2 / (1.0 - ADAM_B2 ** ADAM_STEP)
    return -ADAM_LR * (m_hat / (jnp.sqrt(v_hat) + ADAM_EPS) + ADAM_WD * w), m2, v2


def kernel(x, norm_mix_g, w_in, pool_w, pool_scale, ssm_log_neg_a_re, ssm_a_im, ssm_log_dt, ssm_b_re, ssm_b_im, ssm_c_re, ssm_c_im, ssm_d, glu_w, glu_b, out_norm_pool_g, out_norm_ssm_g, w_out, norm_ffn_g, w_up, conv_w, conv_b, w_down, final_norm_g, loss_target, m_norm_mix_g, m_w_in, m_pool_w, m_pool_scale, m_ssm_log_neg_a_re, m_ssm_a_im, m_ssm_log_dt, m_ssm_b_re, m_ssm_b_im, m_ssm_c_re, m_ssm_c_im, m_ssm_d, m_glu_w, m_glu_b, m_out_norm_pool_g, m_out_norm_ssm_g, m_w_out, m_norm_ffn_g, m_w_up, m_conv_w, m_conv_b, m_w_down, m_final_norm_g, v_norm_mix_g, v_w_in, v_pool_w, v_pool_scale, v_ssm_log_neg_a_re, v_ssm_a_im, v_ssm_log_dt, v_ssm_b_re, v_ssm_b_im, v_ssm_c_re, v_ssm_c_im, v_ssm_d, v_glu_w, v_glu_b, v_out_norm_pool_g, v_out_norm_ssm_g, v_w_out, v_norm_ffn_g, v_w_up, v_conv_w, v_conv_b, v_w_down, v_final_norm_g):
    given = dict(locals())
    weights = {n: given[n] for n in WEIGHTS}
    mom1 = {n: given["m_" + n] for n in WEIGHTS}
    mom2 = {n: given["v_" + n] for n in WEIGHTS}

    xs = x[0]
    tgt = loss_target[0]
    T, D = xs.shape
    DP = len(POOL_WINDOWS) * POOL_GROUP
    DS = D - DP
    G = DS // SSM_GROUP
    N, H = SSM_STATE, SSM_GROUP
    F2 = w_up.shape[1] * N_DEV
    F = F2 // 2
    fs = w_up.shape[1]
    row = lambda a: a.reshape(1, -1)

    glu_rows = glu_w.reshape(-1, PACK_W)
    pack = jnp.concatenate([w_in, w_out, w_down, glu_rows, w_up.reshape(fs, D)], axis=0).astype(BF16)
    sizes = [w_in.shape[0], w_out.shape[0], w_down.shape[0], glu_rows.shape[0], fs]
    offs = np.concatenate([[0], np.cumsum(sizes)])
    got, got_cw = _exchange("gather_weights", [pack, _rows_of(conv_w)], ['gather', 'gather'])
    part = lambda k: got[:, offs[k]:offs[k + 1], :]
    W_in = part(0).reshape(D, D)
    W_out = part(1).reshape(D, D)
    W_down = part(2).reshape(F, D)
    W_glu = part(3).reshape(DS, DS)
    W_up = part(4).reshape(N_DEV, D, fs).transpose(1, 0, 2).reshape(D, F2)
    CW = got_cw.reshape(N_DEV, -1)[:, :3 * fs].reshape(N_DEV, 3, fs).transpose(1, 0, 2).reshape(3, F2)
    CB = row(conv_b)

    g1, g2, g3 = row(norm_mix_g), row(norm_ffn_g), row(final_norm_g)
    gp, gs = row(out_norm_pool_g), row(out_norm_ssm_g)
    (xn,) = _rowmap("norm_mix", _rms_fwd, [xs], [g1], [(D, BF16)])
    u = _mm("proj_in", xn, W_in, 'nn', F32)
    tmp = _pick(T, (256, 128))
    u_pad = jnp.pad(u[:, :DP], ((POOL_PAD, POOL_PAD), (0, 0)))
    pool_w_bf = pool_w.astype(BF16)
    ypn = _pool_fwd(u_pad, pool_w_bf, row(pool_scale), gp, T, tmp)

    u_ssm = _to_chunk_rows(u[:, DP:])
    u_ssm_bf = u_ssm.astype(BF16)
    a_rows = (ssm_log_neg_a_re.reshape(2 * G, N), ssm_a_im.reshape(2 * G, N), ssm_log_dt.reshape(2 * G, 1))
    disc = _ssm_params_fwd(*a_rows)
    nb = G // SSM_BLOCK_GROUPS
    vecs = [d.reshape(2, nb, 1, SSM_BLOCK_GROUPS * N) for d in disc]
    b_in = [_block_diag_in(b) for b in (ssm_b_re, ssm_b_im)]
    c_out = [_block_diag_out(c) for c in (ssm_c_re, ssm_c_im)]
    mats_bf = [[m[d].astype(BF16) for m in b_in + c_out] for d in range(2)]
    mats_t_bf = [[jnp.swapaxes(m[d], 1, 2).astype(BF16) for m in b_in + c_out] for d in range(2)]
    dir_vecs = [[v[d] for v in vecs] for d in range(2)]
    y_dir = [_ssm_fwd("ssm_fwd_%d" % d, u_ssm_bf, mats_bf[d], dir_vecs[d], reverse=bool(d)) for d in range(2)]

    def mix_post(yf, yb, us, d, gw, gb, g):
        y = yf + yb + d * us
        z = _gelu(y)
        gate = jax.nn.sigmoid(jnp.dot(z.astype(BF16), gw, preferred_element_type=F32) + gb)
        return _rms_fwd(z * gate, g), y

    ysn, y_ssm = _rowmap("ssm_post", mix_post, [y_dir[0], y_dir[1], u_ssm], [row(ssm_d), W_glu, row(glu_b), gs],
                         [(DS, BF16), (DS, F32)])
    ycat = jnp.concatenate([ypn, _from_chunk_rows(ysn)], axis=1)
    h1 = _mm("proj_out", ycat, W_out, 'nn', F32, add=xs)
    (hn,) = _rowmap("norm_ffn", _rms_fwd, [h1], [g2], [(D, BF16)])
    up = _mm("ffn_up", hn, W_up, 'nn', F32)
    act = _convact_fwd(up, CW, CB, F)
    h2 = _mm("ffn_down", act, W_down, 'nn', F32, add=h1)

    def head(h, t, g):
        r = lax.rsqrt(_rowmean(h * h) + EPS)
        hh = h * r
        e = hh * g - t
        loss = 0.5 * jnp.sum(_rowmean(e * e), keepdims=True)
        dy = e * (1.0 / D)
        dxh = dy * g
        return r * (dxh - hh * _rowmean(dxh * hh)), jnp.broadcast_to(loss, (1, LANES)), _colsum(dy * hh)

    dh2, loss_acc, dg3 = _rowmap("loss_head", head, [h2, tgt], [g3], [(D, F32)], [(1, LANES), (1, D)])
    loss = lax.psum(loss_acc[0, 0], ("x", "y", "c"))

    dact = _mm("ffn_down_dx", dh2, W_down, 'nt', F32)
    dW_down = _mm("ffn_down_dw", act, dh2, 'tn', F32)
    dup, dCW, dCB = _convact_bwd(up, dact, CW, CB, F)
    dhn = _mm("ffn_up_dx", dup, W_up, 'nt', F32)
    dW_up = _mm("ffn_up_dw", hn, dup, 'tn', F32)

    def norm_bwd_add(hx, dy, res, g):
        dx, dg = _rms_bwd(hx, g, dy)
        return dx + res, dg

    dh1, dg2 = _rowmap("norm_ffn_bwd", norm_bwd_add, [h1, dhn, dh2], [g2], [(D, F32)], [(1, D)])
    dycat = _mm("proj_out_dx", dh1, W_out, 'nt', F32)
    dW_out = _mm("proj_out_dw", ycat, dh1, 'tn', F32)

    q, dpooled, dpool_w, dpool_scale, dgp = _pool_bwd_rows(u_pad, dycat[:, :DP], pool_w_bf, row(pool_scale), gp, T, tmp)
    du_pool = _pool_bwd_band(jnp.pad(q, ((POOL_PAD, POOL_PAD), (0, 0))), dpooled, T, tmp)

    def mix_post_bwd(y, us, dyn, d, gw, gb, g):
        z = _gelu(y)
        gz = z.astype(BF16)
        sig = jax.nn.sigmoid(jnp.dot(gz, gw, preferred_element_type=F32) + gb)
        dys, dg = _rms_bwd(z * sig, g, dyn)
        dgl = dys * z * sig * (1.0 - sig)
        dgl_bf = dgl.astype(BF16)
        dz = dys * sig + lax.dot_general(dgl_bf, gw, (((1,), (1,)), ((), ())), preferred_element_type=F32)
        dgw = lax.dot_general(gz, dgl_bf, (((0,), (0,)), ((), ())), preferred_element_type=F32)
        dyv = dz * _gelu_grad(y)
        return dyv, dyv * d, dgw, _colsum(dgl), _colsum(dyv * us), dg

    d_ysn = _to_chunk_rows(dycat[:, DP:])
    dyv, du_dir, dglu_w, dglu_b, dssm_d, dgs = _rowmap(
        "ssm_post_bwd", mix_post_bwd, [y_ssm, u_ssm, d_ysn], [row(ssm_d), W_glu, row(glu_b), gs],
        [(DS, BF16), (DS, F32)], [(DS, DS), (1, DS), (1, DS), (1, DS)])

    du_ssm = du_dir
    dB, dC, dvec = [], [], []
    for d in range(2):
        b_re, b_im, c_re, c_im = mats_bf[d]
        b_re_t, b_im_t, c_re_t, c_im_t = mats_t_bf[d]
        res = _ssm_bwd("ssm_bwd_%d" % d, u_ssm_bf, dyv, du_ssm, (b_re, b_im), (b_re_t, b_im_t, c_re_t, c_im_t),
                       dir_vecs[d], reverse=bool(d))
        du_ssm = res[0]
        dB.append(res[1:3])
        dC.append(res[3:5])
        dvec.append(res[5:9])
    grad_b = [_block_diag_in_grad(jnp.stack([dB[0][k], dB[1][k]]), N, H) for k in range(2)]
    grad_c = [_block_diag_out_grad(jnp.stack([dC[0][k], dC[1][k]]), N, H) for k in range(2)]
    cots = [jnp.stack([dvec[0][k], dvec[1][k]]).reshape(2 * G, N) for k in range(4)]
    d_a_re, d_a_im, d_log_dt = _ssm_params_bwd(*a_rows, cots)

    du = jnp.concatenate([du_pool, _from_chunk_rows(du_ssm)], axis=1)
    dxn = _mm("proj_in_dx", du, W_in, 'nt', F32)
    dW_in = _mm("proj_in_dw", xn, du, 'tn', F32)
    dx, dg1 = _rowmap("norm_mix_bwd", norm_bwd_add, [xs, dxn, dh1], [g1], [(D, F32)], [(1, D)])

    def shards(full_grad):
        return full_grad.reshape(N_DEV, -1, PACK_W)

    dW_up_sh = dW_up.reshape(D, N_DEV, fs).transpose(1, 0, 2).reshape(N_DEV, fs, D)
    gpack = jnp.concatenate([shards(dW_in), shards(dW_out), shards(dW_down), shards(dglu_w), dW_up_sh], axis=1).astype(BF16)
    dCW_sh = dCW.reshape(3, N_DEV, fs).transpose(1, 0, 2)
    rep_grads = {
        'norm_mix_g': dg1, 'pool_w': dpool_w, 'pool_scale': dpool_scale, 'ssm_log_neg_a_re': d_a_re, 'ssm_a_im': d_a_im,
        'ssm_log_dt': d_log_dt, 'ssm_b_re': grad_b[0], 'ssm_b_im': grad_b[1], 'ssm_c_re': grad_c[0], 'ssm_c_im': grad_c[1],
        'ssm_d': dssm_d, 'glu_b': dglu_b, 'out_norm_pool_g': dgp, 'out_norm_ssm_g': dgs, 'norm_ffn_g': dg2,
        'conv_b': dCB, 'final_norm_g': dg3}
    rep_flat = jnp.concatenate([rep_grads[n].reshape(-1) for n in REPLICATED])
    n_rep = rep_flat.shape[0]
    rep_rows = -(-n_rep // (N_DEV * PACK_W))
    rep_rows = -(-rep_rows // 8) * 8
    rep_sh = jnp.pad(rep_flat, (0, N_DEV * rep_rows * PACK_W - n_rep)).reshape(N_DEV, rep_rows, PACK_W)
    cw_sh = jnp.stack([_rows_of(dCW_sh[p]) for p in range(N_DEV)])
    cw_rows = cw_sh.shape[1]
    spack = jnp.concatenate([rep_sh, cw_sh], axis=1)
    landed_g, landed_s = _exchange("scatter_grads", [gpack, spack], ['scatter', 'scatter'])
    g_sharded = _sum_slots("sum_grads", landed_g, _pick(landed_g.shape[1], (336, 256, 128, 64, 32, 16)))
    g_small = _sum_slots("sum_small_grads", landed_s, landed_s.shape[1])
    (rep_all,) = _exchange("gather_small_grads", [g_small[:rep_rows]], ['gather'])
    rep_red = rep_all.reshape(-1)[:n_rep]
    dcw_mine = g_small[rep_rows:rep_rows + cw_rows].reshape(-1)[:3 * fs].reshape(3, fs)

    grads = {}
    o = 0
    for n in REPLICATED:
        size = int(np.prod(weights[n].shape))
        grads[n] = rep_red[o:o + size].reshape(weights[n].shape)
        o += size
    goffs = np.concatenate([[0], np.cumsum(sizes)])
    gpart = lambda k: g_sharded[goffs[k]:goffs[k + 1]]
    grads['w_in'] = gpart(0)
    grads['w_out'] = gpart(1)
    grads['w_down'] = gpart(2)
    grads['glu_w'] = gpart(3).reshape(glu_w.shape)
    grads['w_up'] = gpart(4).reshape(D, fs)
    grads['conv_w'] = dcw_mine

    def packed(tree):
        return jnp.concatenate([_rows_of(tree[n].astype(F32)) for n in WEIGHTS], axis=0)

    pw, pg, pm, pv = packed(weights), packed(grads), packed(mom1), packed(mom2)
    rows_total = pw.shape[0]
    tr = 256
    pad_rows = -(-rows_total // tr) * tr - rows_total
    padr = lambda a: jnp.pad(a, ((0, pad_rows), (0, 0)))
    delta_p, m_p, v_p = _rowmap("adamw", _adamw, [padr(pw), padr(pg), padr(pm), padr(pv)], [],
                                [(PACK_W, F32)] * 3, tm=tr)

    def unpack(p):
        out, r = {}, 0
        for n in WEIGHTS:
            size = int(np.prod(weights[n].shape))
            rows = -(-(-(-size // PACK_W)) // 8) * 8
            out[n] = p[r:r + rows].reshape(-1)[:size].reshape(weights[n].shape)
            r += rows
        return out

    delta, new_m, new_v = unpack(delta_p), unpack(m_p), unpack(v_p)
    return (loss, dx[None], *[grads[n] for n in WEIGHTS], *[delta[n] for n in WEIGHTS],
            *[new_m[n] for n in WEIGHTS], *[new_v[n] for n in WEIGHTS])
```

```python
import functools

import jax
import jax.numpy as jnp
import numpy as np
from jax import lax
from jax.experimental import pallas as pl
from jax.experimental.pallas import tpu as pltpu

F32 = jnp.float32
BF16 = jnp.bfloat16
MESH_ID = pl.DeviceIdType.MESH

N_DEV = 8
EPS = 1e-6
POOL_WINDOWS = (2, 4, 8, 16)
POOL_GROUP = 128
POOL_PAD = 64
SSM_GROUP = 16
SSM_STATE = 64
SSM_BLOCK_GROUPS = 8
N_CHUNK = 8
LANES = 128
PACK_W = 1024
VMEM_LIMIT = 56 * 1024 * 1024
MM_VMEM_BUDGET = 40 * 1024 * 1024
MM_TILE_CAP = 1408
ADAM_TILE_BYTES = 4 * 1024 * 1024

ADAM_LR = 0.001
ADAM_B1 = 0.9
ADAM_B2 = 0.999
ADAM_EPS = 1e-08
ADAM_WD = 0.01
ADAM_STEP = 10

WEIGHTS = ['norm_mix_g', 'w_in', 'pool_w', 'pool_scale', 'ssm_log_neg_a_re', 'ssm_a_im', 'ssm_log_dt',
           'ssm_b_re', 'ssm_b_im', 'ssm_c_re', 'ssm_c_im', 'ssm_d', 'glu_w', 'glu_b', 'out_norm_pool_g',
           'out_norm_ssm_g', 'w_out', 'norm_ffn_g', 'w_up', 'conv_w', 'conv_b', 'w_down', 'final_norm_g']
SHARDED = ('w_in', 'w_out', 'w_down', 'glu_w', 'w_up', 'conv_w')
REPLICATED = tuple(n for n in WEIGHTS if n not in SHARDED)


def _pick(n, prefs):
    for p in prefs:
        if n % p == 0:
            return p
    return n


def _params(sem, vmem=None):
    return pltpu.CompilerParams(dimension_semantics=sem, vmem_limit_bytes=vmem or VMEM_LIMIT)


def _tiles(n, cap):
    return [d for d in range(LANES, min(n, cap) + 1, LANES) if n % d == 0] or [n]


def _mm(name, a, b, mode, out_dtype, add=None):
    if mode == 'nn':
        (M, K), (_, N) = a.shape, b.shape
    elif mode == 'nt':
        (M, K), (N, _) = a.shape, b.shape
    else:
        (K, M), (_, N) = a.shape, b.shape
    dims = {'nn': (((1,), (0,)), ((), ())), 'nt': (((1,), (1,)), ((), ())), 'tn': (((0,), (0,)), ((), ()))}[mode]
    sa, sb, so = a.dtype.itemsize, b.dtype.itemsize, jnp.dtype(out_dtype).itemsize
    best = None
    for tm in _tiles(M, MM_TILE_CAP):
        for tn in _tiles(N, MM_TILE_CAP):
            need = 2 * (tm * K * sa + tn * K * sb + tm * tn * (so + (4 if add is not None else 0)))
            if need <= MM_VMEM_BUDGET:
                key = (tm * tn / (tm + tn), tm * tn)
                if best is None or key > best[0]:
                    best = (key, tm, tn)
    _, tm, tn = best
    rows_inner = a.size * sa * (N // tn) + b.size * sb < a.size * sa + b.size * sb * (M // tm)

    def kern(*refs):
        a_ref, b_ref = refs[:2]
        o_ref = refs[-1]
        r = lax.dot_general(a_ref[...].astype(BF16), b_ref[...].astype(BF16), dims, preferred_element_type=F32)
        if add is not None:
            r = r + refs[2][...]
        o_ref[...] = r.astype(o_ref.dtype)

    if rows_inner:
        grid = (N // tn, M // tm)
        ij = lambda g0, g1: (g1, g0)
    else:
        grid = (M // tm, N // tn)
        ij = lambda g0, g1: (g0, g1)
    a_spec = (pl.BlockSpec((K, tm), lambda g0, g1: (0, ij(g0, g1)[0])) if mode == 'tn'
              else pl.BlockSpec((tm, K), lambda g0, g1: (ij(g0, g1)[0], 0)))
    b_spec = (pl.BlockSpec((tn, K), lambda g0, g1: (ij(g0, g1)[1], 0)) if mode == 'nt'
              else pl.BlockSpec((K, tn), lambda g0, g1: (0, ij(g0, g1)[1])))
    in_specs = [a_spec, b_spec]
    args = [a, b]
    if add is not None:
        in_specs.append(pl.BlockSpec((tm, tn), lambda g0, g1: ij(g0, g1)))
        args.append(add)
    return pl.pallas_call(
        kern, name=name, grid=grid, in_specs=in_specs,
        out_specs=pl.BlockSpec((tm, tn), lambda g0, g1: ij(g0, g1)),
        out_shape=jax.ShapeDtypeStruct((M, N), out_dtype),
        compiler_params=_params(("parallel", "parallel")),
    )(*args)


def _rowmap(name, fn, rows, fulls, out_rows, out_accs=(), tm=256):
    T = rows[0].shape[0]
    tm = min(tm, T)
    assert T % tm == 0
    n_in, n_row = len(rows) + len(fulls), len(out_rows)

    def kern(*refs):
        i = pl.program_id(0)
        res = fn(*[r[...] for r in refs[:n_in]])
        res = res if isinstance(res, (tuple, list)) else (res,)
        outs = refs[n_in:]
        for o, v in zip(outs[:n_row], res[:n_row]):
            o[...] = v.astype(o.dtype)
        for o, v in zip(outs[n_row:], res[n_row:]):
            @pl.when(i == 0)
            def _(o=o, v=v):
                o[...] = v

            @pl.when(i != 0)
            def _(o=o, v=v):
                o[...] += v

    def full_spec(shape):
        nd = len(shape)
        return pl.BlockSpec(tuple(shape), lambda i: (0,) * nd)

    in_specs = [pl.BlockSpec((tm, r.shape[1]), lambda i: (i, 0)) for r in rows] + [full_spec(f.shape) for f in fulls]
    out_specs = [pl.BlockSpec((tm, c), lambda i: (i, 0)) for c, _ in out_rows] + [full_spec(s) for s in out_accs]
    out_shape = [jax.ShapeDtypeStruct((T, c), dt) for c, dt in out_rows] + [jax.ShapeDtypeStruct(tuple(s), F32) for s in out_accs]
    return pl.pallas_call(
        kern, name=name, grid=(T // tm,), in_specs=in_specs, out_specs=out_specs, out_shape=out_shape,
        compiler_params=_params(("arbitrary",)),
    )(*rows, *fulls)


def _colsum(v):
    return jnp.sum(v, axis=0, keepdims=True)


def _rowmean(v):
    return jnp.mean(v, axis=-1, keepdims=True)


def _rms_fwd(x, g):
    r = lax.rsqrt(_rowmean(x * x) + EPS)
    return x * r * g


def _rms_bwd(x, g, dy):
    r = lax.rsqrt(_rowmean(x * x) + EPS)
    xh = x * r
    dxh = dy * g
    return r * (dxh - xh * _rowmean(dxh * xh)), _colsum(dy * xh)


def _gelu(y):
    c = np.sqrt(2.0 / np.pi).astype(np.float32)
    return 0.5 * y * (1.0 + jnp.tanh(c * (y + 0.044715 * (y * y * y))))


def _gelu_grad(y):
    c = np.sqrt(2.0 / np.pi).astype(np.float32)
    th = jnp.tanh(c * (y + 0.044715 * (y * y * y)))
    return 0.5 * (1.0 + th) + 0.5 * y * (1.0 - th * th) * c * (1.0 + 3.0 * 0.044715 * (y * y))


def _split3(v):
    hi = v.astype(BF16)
    r1 = v - hi.astype(F32)
    mid = r1.astype(BF16)
    lo = (r1 - mid.astype(F32)).astype(BF16)
    return hi, mid, lo


def _band(tm, lo_off, hi_off):
    shape = (tm, tm + 2 * POOL_PAD)
    d = lax.broadcasted_iota(jnp.int32, shape, 1) - lax.broadcasted_iota(jnp.int32, shape, 0) - POOL_PAD
    return ((d >= lo_off) & (d < hi_off)).astype(BF16)


def _band_sum(band, ext):
    hi, mid, lo = _split3(ext)
    dot = functools.partial(jnp.dot, preferred_element_type=F32)
    return dot(band, hi) + dot(band, mid) + dot(band, lo)


def _window_count(r0, tm, half, T):
    t = r0 + lax.broadcasted_iota(jnp.int32, (tm, 1), 0)
    return (jnp.minimum(t + half, T) - jnp.maximum(t - half, 0)).astype(F32)


def _pool_block(u_ref, w_ref, r0, tm, T):
    ext = u_ref[pl.ds(r0, tm + 2 * POOL_PAD), :]
    ctr = u_ref[pl.ds(r0 + POOL_PAD, tm), :]
    pooled, conc = [], []
    for gi, w in enumerate(POOL_WINDOWS):
        half = w // 2
        cols = slice(gi * POOL_GROUP, (gi + 1) * POOL_GROUP)
        ws = _band_sum(_band(tm, -half, half), ext[:, cols])
        p = ws / _window_count(r0, tm, half, T) - ctr[:, cols]
        pooled.append(p)
        conc.append(jnp.dot(p.astype(BF16), w_ref[gi], preferred_element_type=F32))
    return pooled, jnp.concatenate(conc, axis=1)


def _pool_fwd(u_pad, pool_w_bf, pool_scale, g_pool, T, tm):
    nw = len(POOL_WINDOWS)
    C = nw * POOL_GROUP

    def kern(u_ref, w_ref, sc_ref, g_ref, o_ref):
        r0 = pl.multiple_of(pl.program_id(0) * tm, tm)
        _, conc = _pool_block(u_ref, w_ref, r0, tm, T)
        o_ref[...] = _rms_fwd(conc * sc_ref[...], g_ref[...]).astype(o_ref.dtype)

    return pl.pallas_call(
        kern, name="pool_fwd", grid=(T // tm,),
        in_specs=[pl.BlockSpec(u_pad.shape, lambda i: (0, 0)), pl.BlockSpec(pool_w_bf.shape, lambda i: (0, 0, 0)),
                  pl.BlockSpec((1, C), lambda i: (0, 0)), pl.BlockSpec((1, C), lambda i: (0, 0))],
        out_specs=pl.BlockSpec((tm, C), lambda i: (i, 0)),
        out_shape=jax.ShapeDtypeStruct((T, C), BF16),
        compiler_params=_params(("arbitrary",)),
    )(u_pad, pool_w_bf, pool_scale, g_pool)


def _pool_bwd_rows(u_pad, d_ypn, pool_w_bf, pool_scale, g_pool, T, tm):
    nw = len(POOL_WINDOWS)
    C = nw * POOL_GROUP

    def kern(u_ref, dy_ref, w_ref, sc_ref, g_ref, q_ref, dp_ref, dw_ref, dsc_ref, dg_ref):
        i = pl.program_id(0)
        r0 = pl.multiple_of(i * tm, tm)
        pooled, conc = _pool_block(u_ref, w_ref, r0, tm, T)
        sc = sc_ref[...]
        dyp, dg = _rms_bwd(conc * sc, g_ref[...], dy_ref[...])
        dsc = _colsum(dyp * conc)
        dconc = (dyp * sc).astype(BF16)
        dws = []
        for gi, w in enumerate(POOL_WINDOWS):
            cols = slice(gi * POOL_GROUP, (gi + 1) * POOL_GROUP)
            dc = dconc[:, cols]
            dp = lax.dot_general(dc, w_ref[gi], (((1,), (1,)), ((), ())), preferred_element_type=F32)
            dp_ref[:, cols] = dp
            q_ref[:, cols] = dp / _window_count(r0, tm, w // 2, T)
            dws.append(lax.dot_general(pooled[gi].astype(BF16), dc, (((0,), (0,)), ((), ())), preferred_element_type=F32))

        @pl.when(i == 0)
        def _():
            for gi in range(nw):
                dw_ref[gi] = dws[gi]
            dsc_ref[...] = dsc
            dg_ref[...] = dg

        @pl.when(i != 0)
        def _():
            for gi in range(nw):
                dw_ref[gi] += dws[gi]
            dsc_ref[...] += dsc
            dg_ref[...] += dg

    full2 = pl.BlockSpec((1, C), lambda i: (0, 0))
    return pl.pallas_call(
        kern, name="pool_bwd_rows", grid=(T // tm,),
        in_specs=[pl.BlockSpec(u_pad.shape, lambda i: (0, 0)), pl.BlockSpec((tm, C), lambda i: (i, 0)),
                  pl.BlockSpec(pool_w_bf.shape, lambda i: (0, 0, 0)), full2, full2],
        out_specs=[pl.BlockSpec((tm, C), lambda i: (i, 0)), pl.BlockSpec((tm, C), lambda i: (i, 0)),
                   pl.BlockSpec((nw, POOL_GROUP, POOL_GROUP), lambda i: (0, 0, 0)), full2, full2],
        out_shape=[jax.ShapeDtypeStruct((T, C), F32), jax.ShapeDtypeStruct((T, C), F32),
                   jax.ShapeDtypeStruct((nw, POOL_GROUP, POOL_GROUP), F32),
                   jax.ShapeDtypeStruct((1, C), F32), jax.ShapeDtypeStruct((1, C), F32)],
        compiler_params=_params(("arbitrary",)),
    )(u_pad, d_ypn, pool_w_bf, pool_scale, g_pool)


def _pool_bwd_band(q_pad, dpooled, T, tm):
    C = dpooled.shape[1]

    def kern(q_ref, dp_ref, o_ref):
        r0 = pl.multiple_of(pl.program_id(0) * tm, tm)
        ext = q_ref[pl.ds(r0, tm + 2 * POOL_PAD), :]
        for gi, w in enumerate(POOL_WINDOWS):
            half = w // 2
            cols = slice(gi * POOL_GROUP, (gi + 1) * POOL_GROUP)
            o_ref[:, cols] = _band_sum(_band(tm, -half + 1, half + 1), ext[:, cols]) - dp_ref[:, cols]

    return pl.pallas_call(
        kern, name="pool_bwd_band", grid=(T // tm,),
        in_specs=[pl.BlockSpec(q_pad.shape, lambda i: (0, 0)), pl.BlockSpec((tm, C), lambda i: (i, 0))],
        out_specs=pl.BlockSpec((tm, C), lambda i: (i, 0)),
        out_shape=jax.ShapeDtypeStruct((T, C), F32),
        compiler_params=_params(("arbitrary",)),
    )(q_pad, dpooled)


def _cmul(ar, ai, br, bi):
    return ar * br - ai * bi, ar * bi + ai * br


def _ssm_discretise(log_neg_a_re, a_im, log_dt):
    dt = jnp.exp(log_dt)
    a_re = -jnp.exp(log_neg_a_re)
    mag = jnp.exp(a_re * dt)
    ang = a_im * dt
    lam_re, lam_im = mag * jnp.cos(ang), mag * jnp.sin(ang)
    den = a_re * a_re + a_im * a_im
    f_re = ((lam_re - 1.0) * a_re + lam_im * a_im) / den
    f_im = (lam_im * a_re - (lam_re - 1.0) * a_im) / den
    return lam_re, lam_im, f_re, f_im


def _ssm_params_fwd(log_neg_a_re, a_im, log_dt):
    rows, n = log_neg_a_re.shape

    def kern(a_ref, b_ref, c_ref, o1, o2, o3, o4):
        for o, v in zip((o1, o2, o3, o4), _ssm_discretise(a_ref[...], b_ref[...], c_ref[...])):
            o[...] = v

    return pl.pallas_call(kern, name="ssm_params_fwd", out_shape=[jax.ShapeDtypeStruct((rows, n), F32)] * 4)(
        log_neg_a_re, a_im, log_dt)


def _ssm_params_bwd(log_neg_a_re, a_im, log_dt, cots):
    rows, n = log_neg_a_re.shape

    def kern(a_ref, b_ref, c_ref, g1, g2, g3, g4, o1, o2, o3):
        _, vjp = jax.vjp(_ssm_discretise, a_ref[...], b_ref[...], c_ref[...])
        d1, d2, d3 = vjp((g1[...], g2[...], g3[...], g4[...]))
        o1[...] = d1
        o2[...] = d2
        o3[...] = d3

    return pl.pallas_call(
        kern, name="ssm_params_bwd",
        out_shape=[jax.ShapeDtypeStruct((rows, n), F32), jax.ShapeDtypeStruct((rows, n), F32),
                   jax.ShapeDtypeStruct((rows, 1), F32)])(log_neg_a_re, a_im, log_dt, *cots)


def _cpow(lr, li, n):
    out = None
    br, bi = lr, li
    while n:
        if n & 1:
            out = (br, bi) if out is None else _cmul(out[0], out[1], br, bi)
        n >>= 1
        if n:
            br, bi = _cmul(br, bi, br, bi)
    return out


def _slab(t):
    return pl.ds(pl.multiple_of(t * N_CHUNK, N_CHUNK), N_CHUNK)


def _shift_chunks(vr, vi, reverse):
    sub = lax.broadcasted_iota(jnp.int32, vr.shape, 0)
    if reverse:
        keep = sub != N_CHUNK - 1
        return jnp.where(keep, pltpu.roll(vr, N_CHUNK - 1, 0), 0.0), jnp.where(keep, pltpu.roll(vi, N_CHUNK - 1, 0), 0.0)
    keep = sub != 0
    return jnp.where(keep, pltpu.roll(vr, 1, 0), 0.0), jnp.where(keep, pltpu.roll(vi, 1, 0), 0.0)


def _chunk_scan(xr, xi, lr, li, Lc, reverse, prev=None):
    C = xr.shape[1]
    lrb, lib = jnp.broadcast_to(lr, (N_CHUNK, C)), jnp.broadcast_to(li, (N_CHUNK, C))
    zero = jnp.zeros((N_CHUNK, C), F32)

    def step_of(k):
        return (Lc - 1 - k) if reverse else k

    def advance(sr, si, t):
        return lrb * sr - lib * si + xr[_slab(t), :], lrb * si + lib * sr + xi[_slab(t), :]

    def pass1(k, c):
        return advance(c[0], c[1], step_of(k))

    er, ei = lax.fori_loop(0, Lc, pass1, (zero, zero), unroll=4)
    pr, pi = _cpow(lr, li, Lc)
    prb, pib = jnp.broadcast_to(pr, (N_CHUNK, C)), jnp.broadcast_to(pi, (N_CHUNK, C))
    cr, ci = zero, zero
    for _ in range(N_CHUNK - 1):
        cr, ci = _shift_chunks(er + prb * cr - pib * ci, ei + prb * ci + pib * cr, reverse)

    if prev is None:
        def pass2(k, c):
            t = step_of(k)
            nr, ni = advance(c[0], c[1], t)
            xr[_slab(t), :] = nr
            xi[_slab(t), :] = ni
            return nr, ni

        lax.fori_loop(0, Lc, pass2, (cr, ci), unroll=4)
        return None

    qr, qi = prev
    def pair(ar, ai, wr, wi, acc):
        return acc[0] + ar * wr + ai * wi, acc[1] + ai * wr - ar * wi

    def pass2(k, c):
        sr, si, accr, acci = c
        t = step_of(k)
        nr, ni = advance(sr, si, t)
        xr[_slab(t), :] = nr
        xi[_slab(t), :] = ni
        tp = t - 1 if reverse else t + 1
        accr, acci = pair(nr, ni, qr[_slab(tp), :], qi[_slab(tp), :], (accr, acci))
        return nr, ni, accr, acci

    sr, si, accr, acci = lax.fori_loop(0, Lc - 1, pass2, (cr, ci, zero, zero), unroll=4)
    t = step_of(Lc - 1)
    nr, ni = advance(sr, si, t)
    xr[_slab(t), :] = nr
    xi[_slab(t), :] = ni
    edge = step_of(0)
    wr, wi = _shift_chunks(qr[_slab(edge), :], qi[_slab(edge), :], not reverse)
    return pair(nr, ni, wr, wi, (accr, acci))


def _ssm_dims(T):
    assert T % (N_CHUNK * 64) == 0
    rc = _pick(T, (512, 256, 128, 64))
    return T // N_CHUNK, rc, T // rc


def _ssm_project_in(u_ref, bre_ref, bim_ref, xr, xi, rc, nrc):
    for c in range(nrc):
        rows = pl.ds(c * rc, rc)
        ub = u_ref[rows, :]
        xr[rows, :] = jnp.dot(ub, bre_ref[...], preferred_element_type=F32)
        xi[rows, :] = jnp.dot(ub, bim_ref[...], preferred_element_type=F32)


def _ssm_fwd(name, u_bf, mats, vecs, reverse):
    T, W = u_bf.shape
    nb = W // LANES
    SC = SSM_BLOCK_GROUPS * SSM_STATE
    Lc, rc, nrc = _ssm_dims(T)
    bre, bim, cre, cim = mats
    lam_re, lam_im, f_re, f_im = vecs

    def kern(u_ref, bre_ref, bim_ref, cre_ref, cim_ref, lr_ref, li_ref, fr_ref, fi_ref, y_ref, xr, xi):
        _ssm_project_in(u_ref, bre_ref, bim_ref, xr, xi, rc, nrc)
        _chunk_scan(xr, xi, lr_ref[...], li_ref[...], Lc, reverse)
        fr, fi = fr_ref[...], fi_ref[...]
        for c in range(nrc):
            rows = pl.ds(c * rc, rc)
            sr, si = _cmul(fr, fi, xr[rows, :], xi[rows, :])
            y_ref[rows, :] = (jnp.dot(sr.astype(BF16), cre_ref[...], preferred_element_type=F32)
                              - jnp.dot(si.astype(BF16), cim_ref[...], preferred_element_type=F32))

    col = pl.BlockSpec((T, LANES), lambda m: (0, m))
    mat_in = pl.BlockSpec((None, LANES, SC), lambda m: (m, 0, 0))
    mat_out = pl.BlockSpec((None, SC, LANES), lambda m: (m, 0, 0))
    vec = pl.BlockSpec((None, 1, SC), lambda m: (m, 0, 0))
    return pl.pallas_call(
        kern, name=name, grid=(nb,),
        in_specs=[col, mat_in, mat_in, mat_out, mat_out, vec, vec, vec, vec],
        out_specs=col, out_shape=jax.ShapeDtypeStruct((T, W), F32),
        scratch_shapes=[pltpu.VMEM((T, SC), F32), pltpu.VMEM((T, SC), F32)],
        compiler_params=_params(("arbitrary",)),
    )(u_bf, bre, bim, cre, cim, lam_re, lam_im, f_re, f_im)


def _ssm_bwd(name, u_bf, dy_bf, du_in, mats, mats_t, vecs, reverse):
    T, W = u_bf.shape
    nb = W // LANES
    SC = SSM_BLOCK_GROUPS * SSM_STATE
    Lc, rc, nrc = _ssm_dims(T)
    bre, bim = mats
    bre_t, bim_t, cre_t, cim_t = mats_t
    lam_re, lam_im, f_re, f_im = vecs
    contract_rows = (((0,), (0,)), ((), ()))

    def kern(u_ref, dy_ref, dui_ref, bre_ref, bim_ref, bret_ref, bimt_ref, cret_ref, cimt_ref,
             lr_ref, li_ref, fr_ref, fi_ref,
             du_ref, dbre_ref, dbim_ref, dcre_ref, dcim_ref, dlr_ref, dli_ref, dfr_ref, dfi_ref,
             xr, xi, gr, gi):
        lr, li, fr, fi = lr_ref[...], li_ref[...], fr_ref[...], fi_ref[...]
        _ssm_project_in(u_ref, bre_ref, bim_ref, xr, xi, rc, nrc)
        _chunk_scan(xr, xi, lr, li, Lc, reverse)
        mre = jnp.zeros((LANES, SC), F32)
        mim = jnp.zeros((LANES, SC), F32)
        dfr = jnp.zeros((1, SC), F32)
        dfi = jnp.zeros((1, SC), F32)
        for c in range(nrc):
            rows = pl.ds(c * rc, rc)
            dy = dy_ref[rows, :]
            x_r, x_i = xr[rows, :], xi[rows, :]
            g_r = jnp.dot(dy, cret_ref[...], preferred_element_type=F32)
            g_i = -jnp.dot(dy, cimt_ref[...], preferred_element_type=F32)
            mre += lax.dot_general(dy, x_r.astype(BF16), contract_rows, preferred_element_type=F32)
            mim += lax.dot_general(dy, x_i.astype(BF16), contract_rows, preferred_element_type=F32)
            dfr += _colsum(g_r * x_r + g_i * x_i)
            dfi += _colsum(g_i * x_r - g_r * x_i)
            gr[rows, :] = fr * g_r + fi * g_i
            gi[rows, :] = fr * g_i - fi * g_r
        dcre_ref[...] = fr * mre - fi * mim
        dcim_ref[...] = -(fr * mim + fi * mre)
        dfr_ref[...] = dfr
        dfi_ref[...] = dfi
        accr, acci = _chunk_scan(gr, gi, lr, -li, Lc, not reverse, prev=(xr, xi))
        dlr_ref[...] = _colsum(accr)
        dli_ref[...] = _colsum(acci)
        dbre = jnp.zeros((LANES, SC), F32)
        dbim = jnp.zeros((LANES, SC), F32)
        for c in range(nrc):
            rows = pl.ds(c * rc, rc)
            ub = u_ref[rows, :]
            a_r, a_i = gr[rows, :].astype(BF16), gi[rows, :].astype(BF16)
            dbre += lax.dot_general(ub, a_r, contract_rows, preferred_element_type=F32)
            dbim += lax.dot_general(ub, a_i, contract_rows, preferred_element_type=F32)
            du_ref[rows, :] = (dui_ref[rows, :] + jnp.dot(a_r, bret_ref[...], preferred_element_type=F32)
                               + jnp.dot(a_i, bimt_ref[...], preferred_element_type=F32))
        dbre_ref[...] = dbre
        dbim_ref[...] = dbim

    col = pl.BlockSpec((T, LANES), lambda m: (0, m))
    mat_in = pl.BlockSpec((None, LANES, SC), lambda m: (m, 0, 0))
    mat_out = pl.BlockSpec((None, SC, LANES), lambda m: (m, 0, 0))
    vec = pl.BlockSpec((None, 1, SC), lambda m: (m, 0, 0))
    mat_shape = jax.ShapeDtypeStruct((nb, LANES, SC), F32)
    vec_shape = jax.ShapeDtypeStruct((nb, 1, SC), F32)
    return pl.pallas_call(
        kern, name=name, grid=(nb,),
        in_specs=[col, col, col, mat_in, mat_in, mat_out, mat_out, mat_in, mat_in, vec, vec, vec, vec],
        out_specs=[col, mat_in, mat_in, mat_in, mat_in, vec, vec, vec, vec],
        out_shape=[jax.ShapeDtypeStruct((T, W), F32), mat_shape, mat_shape, mat_shape, mat_shape,
                   vec_shape, vec_shape, vec_shape, vec_shape],
        scratch_shapes=[pltpu.VMEM((T, SC), F32)] * 4,
        compiler_params=_params(("arbitrary",)),
    )(u_bf, dy_bf, du_in, bre, bim, bre_t, bim_t, cre_t, cim_t, lam_re, lam_im, f_re, f_im)


def _block_diag_in(b):
    d, G, N, H = b.shape
    nb = G // SSM_BLOCK_GROUPS
    eye = jnp.eye(SSM_BLOCK_GROUPS, dtype=b.dtype)
    v = b.reshape(d, nb, SSM_BLOCK_GROUPS, N, H)
    return jnp.einsum('dmgnh,gk->dmghkn', v, eye).reshape(d, nb, SSM_BLOCK_GROUPS * H, SSM_BLOCK_GROUPS * N)


def _block_diag_in_grad(m, N, H):
    d, nb = m.shape[:2]
    eye = jnp.eye(SSM_BLOCK_GROUPS, dtype=m.dtype)
    v = m.reshape(d, nb, SSM_BLOCK_GROUPS, H, SSM_BLOCK_GROUPS, N)
    return jnp.einsum('dmghkn,gk->dmgnh', v, eye).reshape(d, nb * SSM_BLOCK_GROUPS, N, H)


def _block_diag_out(c):
    return jnp.swapaxes(_block_diag_in(jnp.swapaxes(c, 2, 3)), 2, 3)


def _block_diag_out_grad(m, N, H):
    return jnp.swapaxes(_block_diag_in_grad(m, N, H), 2, 3)


def _to_chunk_rows(a):
    T, C = a.shape
    return a.reshape(N_CHUNK, T // N_CHUNK, C).transpose(1, 0, 2).reshape(T, C)


def _from_chunk_rows(a):
    T, C = a.shape
    return a.reshape(T // N_CHUNK, N_CHUNK, C).transpose(1, 0, 2).reshape(T, C)


def _conv_taps(ref, c, R, T):
    n = T // R
    r0 = pl.multiple_of(c * R, R)
    main = ref[pl.ds(r0, R), :]
    before = ref[pl.ds(pl.multiple_of(jnp.maximum(r0 - 8, 0), 8), 8), :][7:8, :]
    after = ref[pl.ds(pl.multiple_of(jnp.minimum(r0 + R, T - 8), 8), 8), :][0:1, :]
    before = jnp.where(c > 0, before, 0.0)
    after = jnp.where(c < n - 1, after, 0.0)
    rid = lax.broadcasted_iota(jnp.int32, main.shape, 0)
    m1 = jnp.where(rid == 0, before, pltpu.roll(main, 1, 0))
    p1 = jnp.where(rid == R - 1, after, pltpu.roll(main, R - 1, 0))
    return m1, main, p1, r0


def _conv_specs(T, F, cb):
    nj = F // cb
    lo = lambda rows: pl.BlockSpec((rows, cb), lambda j: (0, j))
    hi = lambda rows: pl.BlockSpec((rows, cb), lambda j: (0, j + nj))
    return nj, lo, hi


def _convact_fwd(up, conv_w, conv_b, F):
    T = up.shape[0]
    cb = _pick(F, (256, 128))
    R = _pick(T, (256, 128, 64))
    nj, lo, hi = _conv_specs(T, F, cb)

    def kern(uv, ug, wv, wg, bv, bg, o_ref):
        def body(c, carry):
            v1, v0, v2, r0 = _conv_taps(uv, c, R, T)
            g1, g0, g2, _ = _conv_taps(ug, c, R, T)
            val = v1 * wv[0:1, :] + v0 * wv[1:2, :] + v2 * wv[2:3, :] + bv[...]
            gate = g1 * wg[0:1, :] + g0 * wg[1:2, :] + g2 * wg[2:3, :] + bg[...]
            o_ref[pl.ds(r0, R), :] = (val * (gate * jax.nn.sigmoid(gate))).astype(o_ref.dtype)
            return carry

        lax.fori_loop(0, T // R, body, 0)

    return pl.pallas_call(
        kern, name="convact_fwd", grid=(nj,),
        in_specs=[lo(T), hi(T), lo(3), hi(3), lo(1), hi(1)],
        out_specs=lo(T), out_shape=jax.ShapeDtypeStruct((T, F), BF16),
        compiler_params=_params(("arbitrary",)),
    )(up, up, conv_w, conv_w, conv_b, conv_b)


def _convact_bwd(up, dact, conv_w, conv_b, F):
    T = up.shape[0]
    cb = _pick(F, (256, 128))
    R = _pick(T, (256, 128, 64))
    nj, lo, hi = _conv_specs(T, F, cb)

    def kern(uv, ug, da, wv, wg, bv, bg, dupv, dupg, dwv, dwg, dbv, dbg, sv, sg):
        zero = jnp.zeros((1, cb), F32)

        def pass_a(c, acc):
            v1, v0, v2, r0 = _conv_taps(uv, c, R, T)
            g1, g0, g2, _ = _conv_taps(ug, c, R, T)
            val = v1 * wv[0:1, :] + v0 * wv[1:2, :] + v2 * wv[2:3, :] + bv[...]
            gate = g1 * wg[0:1, :] + g0 * wg[1:2, :] + g2 * wg[2:3, :] + bg[...]
            sig = jax.nn.sigmoid(gate)
            d = da[pl.ds(r0, R), :]
            dval = d * (gate * sig)
            dgate = d * val * (sig * (1.0 + gate * (1.0 - sig)))
            sv[pl.ds(r0, R), :] = dval
            sg[pl.ds(r0, R), :] = dgate
            terms = (dval * v1, dval * v0, dval * v2, dval, dgate * g1, dgate * g0, dgate * g2, dgate)
            return tuple(a + _colsum(t) for a, t in zip(acc, terms))

        acc = lax.fori_loop(0, T // R, pass_a, (zero,) * 8)
        for k in range(3):
            dwv[k:k + 1, :] = acc[k]
            dwg[k:k + 1, :] = acc[4 + k]
        dbv[...] = acc[3]
        dbg[...] = acc[7]

        def pass_b(c, carry):
            v1, v0, v2, r0 = _conv_taps(sv, c, R, T)
            g1, g0, g2, _ = _conv_taps(sg, c, R, T)
            dupv[pl.ds(r0, R), :] = (v2 * wv[0:1, :] + v0 * wv[1:2, :] + v1 * wv[2:3, :]).astype(dupv.dtype)
            dupg[pl.ds(r0, R), :] = (g2 * wg[0:1, :] + g0 * wg[1:2, :] + g1 * wg[2:3, :]).astype(dupg.dtype)
            return carry

        lax.fori_loop(0, T // R, pass_b, 0)

    dupv, dupg, dwv, dwg, dbv, dbg = pl.pallas_call(
        kern, name="convact_bwd", grid=(nj,),
        in_specs=[lo(T), hi(T), lo(T), lo(3), hi(3), lo(1), hi(1)],
        out_specs=[lo(T), lo(T), lo(3), lo(3), lo(1), lo(1)],
        out_shape=[jax.ShapeDtypeStruct((T, F), BF16), jax.ShapeDtypeStruct((T, F), BF16),
                   jax.ShapeDtypeStruct((3, F), F32), jax.ShapeDtypeStruct((3, F), F32),
                   jax.ShapeDtypeStruct((1, F), F32), jax.ShapeDtypeStruct((1, F), F32)],
        scratch_shapes=[pltpu.VMEM((T, cb), F32), pltpu.VMEM((T, cb), F32)],
        compiler_params=_params(("arbitrary",)),
    )(up, up, dact, conv_w, conv_w, conv_b, conv_b)
    return (jnp.concatenate([dupv, dupg], axis=1), jnp.concatenate([dwv, dwg], axis=1),
            jnp.concatenate([dbv, dbg], axis=1))


def _exchange(name, bufs, modes):
    n = len(bufs)

    def kern(*refs):
        ins, outs = refs[:n], refs[n:2 * n]
        send_sems, recv_sems, local_sems = refs[2 * n:]
        x, y, c = lax.axis_index("x"), lax.axis_index("y"), lax.axis_index("c")
        me = 4 * x + 2 * y + c
        started = []
        for b in range(n):
            gather = modes[b] == 'gather'
            own = pltpu.make_async_copy(ins[b] if gather else ins[b].at[me], outs[b].at[me], local_sems.at[b])
            own.start()
            started.append(own)
            for k in range(1, N_DEV):
                px = 1 - x if k & 4 else x
                py = 1 - y if k & 2 else y
                pc = 1 - c if k & 1 else c
                peer = 4 * px + 2 * py + pc
                sem = b * (N_DEV - 1) + k - 1
                pltpu.make_async_remote_copy(
                    src_ref=ins[b] if gather else ins[b].at[peer], dst_ref=outs[b].at[me],
                    send_sem=send_sems.at[sem], recv_sem=recv_sems.at[sem],
                    device_id=(px, py, pc), device_id_type=MESH_ID).start()
        for b in range(n):
            gather = modes[b] == 'gather'
            for k in range(1, N_DEV):
                px = 1 - x if k & 4 else x
                py = 1 - y if k & 2 else y
                pc = 1 - c if k & 1 else c
                peer = 4 * px + 2 * py + pc
                sem = b * (N_DEV - 1) + k - 1
                arrived = pltpu.make_async_remote_copy(
                    src_ref=ins[b] if gather else ins[b].at[peer], dst_ref=outs[b].at[peer],
                    send_sem=send_sems.at[sem], recv_sem=recv_sems.at[sem],
                    device_id=(px, py, pc), device_id_type=MESH_ID)
                arrived.wait_recv()
                arrived.wait_send()
        for own in started:
            own.wait()

    hbm = pl.BlockSpec(memory_space=pltpu.HBM)
    out_shape = [jax.ShapeDtypeStruct((N_DEV,) + tuple(b.shape[-2:]), b.dtype) for b in bufs]
    return pl.pallas_call(
        kern, name=name, in_specs=[hbm] * n, out_specs=[hbm] * n, out_shape=out_shape,
        scratch_shapes=[pltpu.SemaphoreType.DMA((n * (N_DEV - 1),)), pltpu.SemaphoreType.DMA((n * (N_DEV - 1),)),
                        pltpu.SemaphoreType.DMA((n,))],
    )(*bufs)


def _adamw(w, g, m, v):
    m2 = ADAM_B1 * m + (1.0 - ADAM_B1) * g
    v2 = ADAM_B2 * v + (1.0 - ADAM_B2) * (g * g)
    m_hat = m2 / (1.0 - ADAM_B1 ** ADAM_STEP)
    v_hat = v2 / (1.0 - ADAM_B2 ** ADAM_STEP)
    return -ADAM_LR * (m_hat / (jnp.sqrt(v_hat) + ADAM_EPS) + ADAM_WD * w), m2, v2


def _sum_adamw(name, landed, w, m, v):
    R, C = w.shape
    row_bytes = N_DEV * (-(-C // LANES) * LANES) * landed.dtype.itemsize
    tiles = [d for d in range(16, R, 16) if R % d == 0 and d * row_bytes <= ADAM_TILE_BYTES]
    tr = max(tiles) if tiles and R * row_bytes > ADAM_TILE_BYTES else R

    def kern(x_ref, w_ref, m_ref, v_ref, g_out, d_out, m_out, v_out):
        g = x_ref[0].astype(F32)
        for q in range(1, N_DEV):
            g = g + x_ref[q].astype(F32)
        g_out[...] = g
        d_out[...], m_out[...], v_out[...] = _adamw(w_ref[...], g, m_ref[...], v_ref[...])

    blk = pl.BlockSpec((tr, C), lambda i: (i, 0))
    return pl.pallas_call(
        kern, name=name, grid=(R // tr,),
        in_specs=[pl.BlockSpec((N_DEV, tr, C), lambda i: (0, i, 0)), blk, blk, blk],
        out_specs=[blk] * 4, out_shape=[jax.ShapeDtypeStruct((R, C), F32)] * 4,
        compiler_params=_params(("arbitrary",)),
    )(landed, w, m, v)


def _view2d(a):
    if a.ndim == 1:
        return a.reshape(1, -1)
    return a.reshape(-1, a.shape[-1])


def kernel(x, norm_mix_g, w_in, pool_w, pool_scale, ssm_log_neg_a_re, ssm_a_im, ssm_log_dt, ssm_b_re, ssm_b_im, ssm_c_re, ssm_c_im, ssm_d, glu_w, glu_b, out_norm_pool_g, out_norm_ssm_g, w_out, norm_ffn_g, w_up, conv_w, conv_b, w_down, final_norm_g, loss_target, m_norm_mix_g, m_w_in, m_pool_w, m_pool_scale, m_ssm_log_neg_a_re, m_ssm_a_im, m_ssm_log_dt, m_ssm_b_re, m_ssm_b_im, m_ssm_c_re, m_ssm_c_im, m_ssm_d, m_glu_w, m_glu_b, m_out_norm_pool_g, m_out_norm_ssm_g, m_w_out, m_norm_ffn_g, m_w_up, m_conv_w, m_conv_b, m_w_down, m_final_norm_g, v_norm_mix_g, v_w_in, v_pool_w, v_pool_scale, v_ssm_log_neg_a_re, v_ssm_a_im, v_ssm_log_dt, v_ssm_b_re, v_ssm_b_im, v_ssm_c_re, v_ssm_c_im, v_ssm_d, v_glu_w, v_glu_b, v_out_norm_pool_g, v_out_norm_ssm_g, v_w_out, v_norm_ffn_g, v_w_up, v_conv_w, v_conv_b, v_w_down, v_final_norm_g):
    given = dict(locals())
    weights = {n: given[n] for n in WEIGHTS}
    mom1 = {n: given["m_" + n] for n in WEIGHTS}
    mom2 = {n: given["v_" + n] for n in WEIGHTS}

    xs = x[0]
    tgt = loss_target[0]
    T, D = xs.shape
    DP = len(POOL_WINDOWS) * POOL_GROUP
    DS = D - DP
    G = DS // SSM_GROUP
    N, H = SSM_STATE, SSM_GROUP
    F2 = w_up.shape[1] * N_DEV
    F = F2 // 2
    fs = w_up.shape[1]
    row = lambda a: a.reshape(1, -1)

    big = ('w_in', 'w_out', 'w_down', 'glu_w', 'w_up')
    got = _exchange("gather_weights", [weights[n].astype(BF16) for n in big] + [conv_w], ['gather'] * 6)
    W_in = got[0].reshape(D, D)
    W_out = got[1].reshape(D, D)
    W_down = got[2].reshape(F, D)
    W_glu = got[3].reshape(DS, DS)
    W_up = got[4].transpose(1, 0, 2).reshape(D, F2)
    CW = got[5].transpose(1, 0, 2).reshape(3, F2)
    CB = row(conv_b)

    g1, g2, g3 = row(norm_mix_g), row(norm_ffn_g), row(final_norm_g)
    gp, gs = row(out_norm_pool_g), row(out_norm_ssm_g)
    (xn,) = _rowmap("norm_mix", _rms_fwd, [xs], [g1], [(D, BF16)])
    u = _mm("proj_in", xn, W_in, 'nn', F32)
    tmp = _pick(T, (256, 128))
    u_pad = jnp.pad(u[:, :DP], ((POOL_PAD, POOL_PAD), (0, 0)))
    pool_w_bf = pool_w.astype(BF16)
    ypn = _pool_fwd(u_pad, pool_w_bf, row(pool_scale), gp, T, tmp)

    u_ssm = _to_chunk_rows(u[:, DP:])
    u_ssm_bf = u_ssm.astype(BF16)
    a_rows = (ssm_log_neg_a_re.reshape(2 * G, N), ssm_a_im.reshape(2 * G, N), ssm_log_dt.reshape(2 * G, 1))
    disc = _ssm_params_fwd(*a_rows)
    nb = G // SSM_BLOCK_GROUPS
    vecs = [d.reshape(2, nb, 1, SSM_BLOCK_GROUPS * N) for d in disc]
    b_in = [_block_diag_in(b) for b in (ssm_b_re, ssm_b_im)]
    c_out = [_block_diag_out(c) for c in (ssm_c_re, ssm_c_im)]
    mats_bf = [[m[d].astype(BF16) for m in b_in + c_out] for d in range(2)]
    mats_t_bf = [[jnp.swapaxes(m[d], 1, 2).astype(BF16) for m in b_in + c_out] for d in range(2)]
    dir_vecs = [[v[d] for v in vecs] for d in range(2)]
    y_dir = [_ssm_fwd("ssm_fwd_%d" % d, u_ssm_bf, mats_bf[d], dir_vecs[d], reverse=bool(d)) for d in range(2)]

    def mix_post(yf, yb, us, d, gw, gb, g):
        y = yf + yb + d * us
        z = _gelu(y)
        gate = jax.nn.sigmoid(jnp.dot(z.astype(BF16), gw, preferred_element_type=F32) + gb)
        return _rms_fwd(z * gate, g), y

    ysn, y_ssm = _rowmap("ssm_post", mix_post, [y_dir[0], y_dir[1], u_ssm], [row(ssm_d), W_glu, row(glu_b), gs],
                         [(DS, BF16), (DS, F32)])
    ycat = jnp.concatenate([ypn, _from_chunk_rows(ysn)], axis=1)
    h1 = _mm("proj_out", ycat, W_out, 'nn', F32, add=xs)
    (hn,) = _rowmap("norm_ffn", _rms_fwd, [h1], [g2], [(D, BF16)])
    up = _mm("ffn_up", hn, W_up, 'nn', F32)
    act = _convact_fwd(up, CW, CB, F)
    h2 = _mm("ffn_down", act, W_down, 'nn', F32, add=h1)

    def head(h, t, g):
        r = lax.rsqrt(_rowmean(h * h) + EPS)
        hh = h * r
        e = hh * g - t
        loss = 0.5 * jnp.sum(_rowmean(e * e), keepdims=True)
        dy = e * (1.0 / D)
        dxh = dy * g
        dh = r * (dxh - hh * _rowmean(dxh * hh))
        return dh, dh, jnp.broadcast_to(loss, (1, LANES)), _colsum(dy * hh)

    dh2, dh2_bf, loss_acc, dg3 = _rowmap("loss_head", head, [h2, tgt], [g3], [(D, F32), (D, BF16)], [(1, LANES), (1, D)])
    loss = lax.psum(loss_acc[0, 0], ("x", "y", "c"))

    dact = _mm("ffn_down_dx", dh2_bf, W_down, 'nt', F32)
    dW_down = _mm("ffn_down_dw", act, dh2_bf, 'tn', BF16)
    dup, dCW, dCB = _convact_bwd(up, dact, CW, CB, F)
    dhn = _mm("ffn_up_dx", dup, W_up, 'nt', F32)
    dW_up = _mm("ffn_up_dw", hn, dup, 'tn', BF16)

    def norm_bwd_add(hx, dy, res, g):
        dx, dg = _rms_bwd(hx, g, dy)
        return dx + res, dg

    def norm_bwd_add2(hx, dy, res, g):
        dx, dg = _rms_bwd(hx, g, dy)
        return dx + res, dx + res, dg

    dh1, dh1_bf, dg2 = _rowmap("norm_ffn_bwd", norm_bwd_add2, [h1, dhn, dh2], [g2], [(D, F32), (D, BF16)], [(1, D)])
    dycat = _mm("proj_out_dx", dh1_bf, W_out, 'nt', F32)
    dW_out = _mm("proj_out_dw", ycat, dh1_bf, 'tn', BF16)

    q, dpooled, dpool_w, dpool_scale, dgp = _pool_bwd_rows(u_pad, dycat[:, :DP], pool_w_bf, row(pool_scale), gp, T, tmp)
    du_pool = _pool_bwd_band(jnp.pad(q, ((POOL_PAD, POOL_PAD), (0, 0))), dpooled, T, tmp)

    def mix_post_bwd(y, us, dyn, d, gw, gb, g):
        z = _gelu(y)
        gz = z.astype(BF16)
        sig = jax.nn.sigmoid(jnp.dot(gz, gw, preferred_element_type=F32) + gb)
        dys, dg = _rms_bwd(z * sig, g, dyn)
        dgl = dys * z * sig * (1.0 - sig)
        dgl_bf = dgl.astype(BF16)
        dz = dys * sig + lax.dot_general(dgl_bf, gw, (((1,), (1,)), ((), ())), preferred_element_type=F32)
        dgw = lax.dot_general(gz, dgl_bf, (((0,), (0,)), ((), ())), preferred_element_type=F32)
        dyv = dz * _gelu_grad(y)
        return dyv, dyv * d, dgw, _colsum(dgl), _colsum(dyv * us), dg

    d_ysn = _to_chunk_rows(dycat[:, DP:])
    dyv, du_dir, dglu_w, dglu_b, dssm_d, dgs = _rowmap(
        "ssm_post_bwd", mix_post_bwd, [y_ssm, u_ssm, d_ysn], [row(ssm_d), W_glu, row(glu_b), gs],
        [(DS, BF16), (DS, F32)], [(DS, DS), (1, DS), (1, DS), (1, DS)])

    du_ssm = du_dir
    dB, dC, dvec = [], [], []
    for d in range(2):
        b_re, b_im, c_re, c_im = mats_bf[d]
        b_re_t, b_im_t, c_re_t, c_im_t = mats_t_bf[d]
        res = _ssm_bwd("ssm_bwd_%d" % d, u_ssm_bf, dyv, du_ssm, (b_re, b_im), (b_re_t, b_im_t, c_re_t, c_im_t),
                       dir_vecs[d], reverse=bool(d))
        du_ssm = res[0]
        dB.append(res[1:3])
        dC.append(res[3:5])
        dvec.append(res[5:9])
    grad_b = [_block_diag_in_grad(jnp.stack([dB[0][k], dB[1][k]]), N, H) for k in range(2)]
    grad_c = [_block_diag_out_grad(jnp.stack([dC[0][k], dC[1][k]]), N, H) for k in range(2)]
    cots = [jnp.stack([dvec[0][k], dvec[1][k]]).reshape(2 * G, N) for k in range(4)]
    d_a_re, d_a_im, d_log_dt = _ssm_params_bwd(*a_rows, cots)

    du = jnp.concatenate([du_pool, _from_chunk_rows(du_ssm)], axis=1).astype(BF16)
    dxn = _mm("proj_in_dx", du, W_in, 'nt', F32)
    dW_in = _mm("proj_in_dw", xn, du, 'tn', BF16)
    dx, dg1 = _rowmap("norm_mix_bwd", norm_bwd_add, [xs, dxn, dh1], [g1], [(D, F32)], [(1, D)])

    def shards(full_grad):
        return full_grad.reshape((N_DEV, full_grad.shape[0] // N_DEV) + full_grad.shape[1:])

    sent = [shards(dW_in), shards(dW_out), shards(dW_down), shards(dglu_w.astype(BF16)),
            dW_up.reshape(D, N_DEV, fs).transpose(1, 0, 2), dCW.reshape(3, N_DEV, fs).transpose(1, 0, 2)]
    landed = _exchange("scatter_grads", sent, ['scatter'] * 6)
    grads, delta, new_m, new_v = {}, {}, {}, {}

    def update(n, landed_n):
        shape = weights[n].shape
        res = _sum_adamw("adamw_" + n, landed_n, _view2d(weights[n]), _view2d(mom1[n]), _view2d(mom2[n]))
        grads[n], delta[n], new_m[n], new_v[n] = [r.reshape(shape) for r in res]

    for n, landed_n in zip(big + ('conv_w',), landed):
        update(n, landed_n)

    rep_grads = {
        'norm_mix_g': dg1, 'pool_w': dpool_w, 'pool_scale': dpool_scale, 'ssm_log_neg_a_re': d_a_re, 'ssm_a_im': d_a_im,
        'ssm_log_dt': d_log_dt, 'ssm_b_re': grad_b[0], 'ssm_b_im': grad_b[1], 'ssm_c_re': grad_c[0], 'ssm_c_im': grad_c[1],
        'ssm_d': dssm_d, 'glu_b': dglu_b, 'out_norm_pool_g': dgp, 'out_norm_ssm_g': dgs, 'norm_ffn_g': dg2,
        'conv_b': dCB, 'final_norm_g': dg3}
    wide = ('pool_w', 'ssm_b_re', 'ssm_b_im', 'ssm_c_re', 'ssm_c_im')
    partial = [_view2d(rep_grads[n].reshape(weights[n].shape)).astype(BF16 if n in wide else F32) for n in REPLICATED]
    gathered = _exchange("gather_small_grads", partial, ['gather'] * len(partial))
    for n, landed_n in zip(REPLICATED, gathered):
        update(n, landed_n)

    return (loss, dx[None], *[grads[n] for n in WEIGHTS], *[delta[n] for n in WEIGHTS],
            *[new_m[n] for n in WEIGHTS], *[new_v[n] for n in WEIGHTS])
```

```python
import functools

import jax
import jax.numpy as jnp
import numpy as np
from jax import lax
from jax.experimental import pallas as pl
from jax.experimental.pallas import tpu as pltpu

F32 = jnp.float32
BF16 = jnp.bfloat16
MESH_ID = pl.DeviceIdType.MESH

N_DEV = 8
EPS = 1e-6
POOL_WINDOWS = (2, 4, 8, 16)
POOL_GROUP = 128
POOL_PAD = 64
SSM_GROUP = 16
SSM_STATE = 64
SSM_BLOCK_GROUPS = 8
N_CHUNK = 8
LANES = 128
PACK_W = 1024
VMEM_LIMIT = 56 * 1024 * 1024
MM_VMEM_BUDGET = 40 * 1024 * 1024
MM_TILE_CAP = 1408
ADAM_TILE_BYTES = 4 * 1024 * 1024

ADAM_LR = 0.001
ADAM_B1 = 0.9
ADAM_B2 = 0.999
ADAM_EPS = 1e-08
ADAM_WD = 0.01
ADAM_STEP = 10

WEIGHTS = ['norm_mix_g', 'w_in', 'pool_w', 'pool_scale', 'ssm_log_neg_a_re', 'ssm_a_im', 'ssm_log_dt',
           'ssm_b_re', 'ssm_b_im', 'ssm_c_re', 'ssm_c_im', 'ssm_d', 'glu_w', 'glu_b', 'out_norm_pool_g',
           'out_norm_ssm_g', 'w_out', 'norm_ffn_g', 'w_up', 'conv_w', 'conv_b', 'w_down', 'final_norm_g']
SHARDED = ('w_in', 'w_out', 'w_down', 'glu_w', 'w_up', 'conv_w')
REPLICATED = tuple(n for n in WEIGHTS if n not in SHARDED)


def _pick(n, prefs):
    for p in prefs:
        if n % p == 0:
            return p
    return n


def _params(sem, vmem=None):
    return pltpu.CompilerParams(dimension_semantics=sem, vmem_limit_bytes=vmem or VMEM_LIMIT)


def _tiles(n, cap):
    return [d for d in range(LANES, min(n, cap) + 1, LANES) if n % d == 0] or [n]


def _mm(name, a, b, mode, out_dtype, add=None):
    if mode == 'nn':
        (M, K), (_, N) = a.shape, b.shape
    elif mode == 'nt':
        (M, K), (N, _) = a.shape, b.shape
    else:
        (K, M), (_, N) = a.shape, b.shape
    dims = {'nn': (((1,), (0,)), ((), ())), 'nt': (((1,), (1,)), ((), ())), 'tn': (((0,), (0,)), ((), ()))}[mode]
    sa, sb, so = a.dtype.itemsize, b.dtype.itemsize, jnp.dtype(out_dtype).itemsize
    best = None
    for tm in _tiles(M, MM_TILE_CAP):
        for tn in _tiles(N, MM_TILE_CAP):
            need = 2 * (tm * K * sa + tn * K * sb + tm * tn * (so + (4 if add is not None else 0)))
            if need <= MM_VMEM_BUDGET:
                key = (tm * tn / (tm + tn), tm * tn)
                if best is None or key > best[0]:
                    best = (key, tm, tn)
    _, tm, tn = best
    rows_inner = a.size * sa * (N // tn) + b.size * sb < a.size * sa + b.size * sb * (M // tm)

    def kern(*refs):
        a_ref, b_ref = refs[:2]
        o_ref = refs[-1]
        r = lax.dot_general(a_ref[...].astype(BF16), b_ref[...].astype(BF16), dims, preferred_element_type=F32)
        if add is not None:
            r = r + refs[2][...]
        o_ref[...] = r.astype(o_ref.dtype)

    if rows_inner:
        grid = (N // tn, M // tm)
        ij = lambda g0, g1: (g1, g0)
    else:
        grid = (M // tm, N // tn)
        ij = lambda g0, g1: (g0, g1)
    a_spec = (pl.BlockSpec((K, tm), lambda g0, g1: (0, ij(g0, g1)[0])) if mode == 'tn'
              else pl.BlockSpec((tm, K), lambda g0, g1: (ij(g0, g1)[0], 0)))
    b_spec = (pl.BlockSpec((tn, K), lambda g0, g1: (ij(g0, g1)[1], 0)) if mode == 'nt'
              else pl.BlockSpec((K, tn), lambda g0, g1: (0, ij(g0, g1)[1])))
    in_specs = [a_spec, b_spec]
    args = [a, b]
    if add is not None:
        in_specs.append(pl.BlockSpec((tm, tn), lambda g0, g1: ij(g0, g1)))
        args.append(add)
    return pl.pallas_call(
        kern, name=name, grid=grid, in_specs=in_specs,
        out_specs=pl.BlockSpec((tm, tn), lambda g0, g1: ij(g0, g1)),
        out_shape=jax.ShapeDtypeStruct((M, N), out_dtype),
        compiler_params=_params(("parallel", "parallel")),
    )(*args)


def _rowmap(name, fn, rows, fulls, out_rows, out_accs=(), tm=256):
    T = rows[0].shape[0]
    tm = min(tm, T)
    assert T % tm == 0
    n_in, n_row = len(rows) + len(fulls), len(out_rows)

    def kern(*refs):
        i = pl.program_id(0)
        res = fn(*[r[...] for r in refs[:n_in]])
        res = res if isinstance(res, (tuple, list)) else (res,)
        outs = refs[n_in:]
        for o, v in zip(outs[:n_row], res[:n_row]):
            o[...] = v.astype(o.dtype)
        for o, v in zip(outs[n_row:], res[n_row:]):
            @pl.when(i == 0)
            def _(o=o, v=v):
                o[...] = v

            @pl.when(i != 0)
            def _(o=o, v=v):
                o[...] += v

    def full_spec(shape):
        nd = len(shape)
        return pl.BlockSpec(tuple(shape), lambda i: (0,) * nd)

    in_specs = [pl.BlockSpec((tm, r.shape[1]), lambda i: (i, 0)) for r in rows] + [full_spec(f.shape) for f in fulls]
    out_specs = [pl.BlockSpec((tm, c), lambda i: (i, 0)) for c, _ in out_rows] + [full_spec(s) for s in out_accs]
    out_shape = [jax.ShapeDtypeStruct((T, c), dt) for c, dt in out_rows] + [jax.ShapeDtypeStruct(tuple(s), F32) for s in out_accs]
    return pl.pallas_call(
        kern, name=name, grid=(T // tm,), in_specs=in_specs, out_specs=out_specs, out_shape=out_shape,
        compiler_params=_params(("arbitrary",)),
    )(*rows, *fulls)


def _colsum(v):
    return jnp.sum(v, axis=0, keepdims=True)


def _rowmean(v):
    return jnp.mean(v, axis=-1, keepdims=True)


def _rms_fwd(x, g):
    r = lax.rsqrt(_rowmean(x * x) + EPS)
    return x * r * g


def _rms_bwd(x, g, dy):
    r = lax.rsqrt(_rowmean(x * x) + EPS)
    xh = x * r
    dxh = dy * g
    return r * (dxh - xh * _rowmean(dxh * xh)), _colsum(dy * xh)


def _gelu(y):
    c = np.sqrt(2.0 / np.pi).astype(np.float32)
    return 0.5 * y * (1.0 + jnp.tanh(c * (y + 0.044715 * (y * y * y))))


def _gelu_grad(y):
    c = np.sqrt(2.0 / np.pi).astype(np.float32)
    th = jnp.tanh(c * (y + 0.044715 * (y * y * y)))
    return 0.5 * (1.0 + th) + 0.5 * y * (1.0 - th * th) * c * (1.0 + 3.0 * 0.044715 * (y * y))


def _split3(v):
    hi = v.astype(BF16)
    r1 = v - hi.astype(F32)
    mid = r1.astype(BF16)
    lo = (r1 - mid.astype(F32)).astype(BF16)
    return hi, mid, lo


def _band(tm, lo_off, hi_off):
    shape = (tm, tm + 2 * POOL_PAD)
    d = lax.broadcasted_iota(jnp.int32, shape, 1) - lax.broadcasted_iota(jnp.int32, shape, 0) - POOL_PAD
    return ((d >= lo_off) & (d < hi_off)).astype(BF16)


def _band_sum(band, ext):
    hi, mid, lo = _split3(ext)
    dot = functools.partial(jnp.dot, preferred_element_type=F32)
    return dot(band, hi) + dot(band, mid) + dot(band, lo)


def _window_count(r0, tm, half, T):
    t = r0 + lax.broadcasted_iota(jnp.int32, (tm, 1), 0)
    return (jnp.minimum(t + half, T) - jnp.maximum(t - half, 0)).astype(F32)


def _pool_block(u_ref, w_ref, r0, tm, T):
    ext = u_ref[pl.ds(r0, tm + 2 * POOL_PAD), :]
    ctr = u_ref[pl.ds(r0 + POOL_PAD, tm), :]
    pooled, conc = [], []
    for gi, w in enumerate(POOL_WINDOWS):
        half = w // 2
        cols = slice(gi * POOL_GROUP, (gi + 1) * POOL_GROUP)
        ws = _band_sum(_band(tm, -half, half), ext[:, cols])
        p = ws / _window_count(r0, tm, half, T) - ctr[:, cols]
        pooled.append(p)
        conc.append(jnp.dot(p.astype(BF16), w_ref[gi], preferred_element_type=F32))
    return pooled, jnp.concatenate(conc, axis=1)


def _pool_fwd(u_pad, pool_w_bf, pool_scale, g_pool, T, tm):
    nw = len(POOL_WINDOWS)
    C = nw * POOL_GROUP

    def kern(u_ref, w_ref, sc_ref, g_ref, o_ref):
        r0 = pl.multiple_of(pl.program_id(0) * tm, tm)
        _, conc = _pool_block(u_ref, w_ref, r0, tm, T)
        o_ref[...] = _rms_fwd(conc * sc_ref[...], g_ref[...]).astype(o_ref.dtype)

    return pl.pallas_call(
        kern, name="pool_fwd", grid=(T // tm,),
        in_specs=[pl.BlockSpec(u_pad.shape, lambda i: (0, 0)), pl.BlockSpec(pool_w_bf.shape, lambda i: (0, 0, 0)),
                  pl.BlockSpec((1, C), lambda i: (0, 0)), pl.BlockSpec((1, C), lambda i: (0, 0))],
        out_specs=pl.BlockSpec((tm, C), lambda i: (i, 0)),
        out_shape=jax.ShapeDtypeStruct((T, C), BF16),
        compiler_params=_params(("arbitrary",)),
    )(u_pad, pool_w_bf, pool_scale, g_pool)


def _pool_bwd_rows(u_pad, d_ypn, pool_w_bf, pool_scale, g_pool, T, tm):
    nw = len(POOL_WINDOWS)
    C = nw * POOL_GROUP

    def kern(u_ref, dy_ref, w_ref, sc_ref, g_ref, q_ref, dp_ref, dw_ref, dsc_ref, dg_ref):
        i = pl.program_id(0)
        r0 = pl.multiple_of(i * tm, tm)
        pooled, conc = _pool_block(u_ref, w_ref, r0, tm, T)
        sc = sc_ref[...]
        dyp, dg = _rms_bwd(conc * sc, g_ref[...], dy_ref[...])
        dsc = _colsum(dyp * conc)
        dconc = (dyp * sc).astype(BF16)
        dws = []
        for gi, w in enumerate(POOL_WINDOWS):
            cols = slice(gi * POOL_GROUP, (gi + 1) * POOL_GROUP)
            dc = dconc[:, cols]
            dp = lax.dot_general(dc, w_ref[gi], (((1,), (1,)), ((), ())), preferred_element_type=F32)
            dp_ref[:, cols] = dp
            q_ref[:, cols] = dp / _window_count(r0, tm, w // 2, T)
            dws.append(lax.dot_general(pooled[gi].astype(BF16), dc, (((0,), (0,)), ((), ())), preferred_element_type=F32))

        @pl.when(i == 0)
        def _():
            for gi in range(nw):
                dw_ref[gi] = dws[gi]
            dsc_ref[...] = dsc
            dg_ref[...] = dg

        @pl.when(i != 0)
        def _():
            for gi in range(nw):
                dw_ref[gi] += dws[gi]
            dsc_ref[...] += dsc
            dg_ref[...] += dg

    full2 = pl.BlockSpec((1, C), lambda i: (0, 0))
    return pl.pallas_call(
        kern, name="pool_bwd_rows", grid=(T // tm,),
        in_specs=[pl.BlockSpec(u_pad.shape, lambda i: (0, 0)), pl.BlockSpec((tm, C), lambda i: (i, 0)),
                  pl.BlockSpec(pool_w_bf.shape, lambda i: (0, 0, 0)), full2, full2],
        out_specs=[pl.BlockSpec((tm, C), lambda i: (i, 0)), pl.BlockSpec((tm, C), lambda i: (i, 0)),
                   pl.BlockSpec((nw, POOL_GROUP, POOL_GROUP), lambda i: (0, 0, 0)), full2, full2],
        out_shape=[jax.ShapeDtypeStruct((T, C), F32), jax.ShapeDtypeStruct((T, C), F32),
                   jax.ShapeDtypeStruct((nw, POOL_GROUP, POOL_GROUP), F32),
                   jax.ShapeDtypeStruct((1, C), F32), jax.ShapeDtypeStruct((1, C), F32)],
        compiler_params=_params(("arbitrary",)),
    )(u_pad, d_ypn, pool_w_bf, pool_scale, g_pool)


def _pool_bwd_band(q_pad, dpooled, T, tm):
    C = dpooled.shape[1]

    def kern(q_ref, dp_ref, o_ref):
        r0 = pl.multiple_of(pl.program_id(0) * tm, tm)
        ext = q_ref[pl.ds(r0, tm + 2 * POOL_PAD), :]
        for gi, w in enumerate(POOL_WINDOWS):
            half = w // 2
            cols = slice(gi * POOL_GROUP, (gi + 1) * POOL_GROUP)
            o_ref[:, cols] = _band_sum(_band(tm, -half + 1, half + 1), ext[:, cols]) - dp_ref[:, cols]

    return pl.pallas_call(
        kern, name="pool_bwd_band", grid=(T // tm,),
        in_specs=[pl.BlockSpec(q_pad.shape, lambda i: (0, 0)), pl.BlockSpec((tm, C), lambda i: (i, 0))],
        out_specs=pl.BlockSpec((tm, C), lambda i: (i, 0)),
        out_shape=jax.ShapeDtypeStruct((T, C), F32),
        compiler_params=_params(("arbitrary",)),
    )(q_pad, dpooled)


def _cmul(ar, ai, br, bi):
    return ar * br - ai * bi, ar * bi + ai * br


def _ssm_discretise(log_neg_a_re, a_im, log_dt):
    dt = jnp.exp(log_dt)
    a_re = -jnp.exp(log_neg_a_re)
    mag = jnp.exp(a_re * dt)
    ang = a_im * dt
    lam_re, lam_im = mag * jnp.cos(ang), mag * jnp.sin(ang)
    den = a_re * a_re + a_im * a_im
    f_re = ((lam_re - 1.0) * a_re + lam_im * a_im) / den
    f_im = (lam_im * a_re - (lam_re - 1.0) * a_im) / den
    return lam_re, lam_im, f_re, f_im


def _ssm_params_fwd(log_neg_a_re, a_im, log_dt):
    rows, n = log_neg_a_re.shape

    def kern(a_ref, b_ref, c_ref, o1, o2, o3, o4):
        for o, v in zip((o1, o2, o3, o4), _ssm_discretise(a_ref[...], b_ref[...], c_ref[...])):
            o[...] = v

    return pl.pallas_call(kern, name="ssm_params_fwd", out_shape=[jax.ShapeDtypeStruct((rows, n), F32)] * 4)(
        log_neg_a_re, a_im, log_dt)


def _ssm_params_bwd(log_neg_a_re, a_im, log_dt, cots):
    rows, n = log_neg_a_re.shape

    def kern(a_ref, b_ref, c_ref, g1, g2, g3, g4, o1, o2, o3):
        _, vjp = jax.vjp(_ssm_discretise, a_ref[...], b_ref[...], c_ref[...])
        d1, d2, d3 = vjp((g1[...], g2[...], g3[...], g4[...]))
        o1[...] = d1
        o2[...] = d2
        o3[...] = d3

    return pl.pallas_call(
        kern, name="ssm_params_bwd",
        out_shape=[jax.ShapeDtypeStruct((rows, n), F32), jax.ShapeDtypeStruct((rows, n), F32),
                   jax.ShapeDtypeStruct((rows, 1), F32)])(log_neg_a_re, a_im, log_dt, *cots)


def _cpow(lr, li, n):
    out = None
    br, bi = lr, li
    while n:
        if n & 1:
            out = (br, bi) if out is None else _cmul(out[0], out[1], br, bi)
        n >>= 1
        if n:
            br, bi = _cmul(br, bi, br, bi)
    return out


def _slab(t):
    return pl.ds(pl.multiple_of(t * N_CHUNK, N_CHUNK), N_CHUNK)


def _shift_chunks(vr, vi, reverse):
    sub = lax.broadcasted_iota(jnp.int32, vr.shape, 0)
    if reverse:
        keep = sub != N_CHUNK - 1
        return jnp.where(keep, pltpu.roll(vr, N_CHUNK - 1, 0), 0.0), jnp.where(keep, pltpu.roll(vi, N_CHUNK - 1, 0), 0.0)
    keep = sub != 0
    return jnp.where(keep, pltpu.roll(vr, 1, 0), 0.0), jnp.where(keep, pltpu.roll(vi, 1, 0), 0.0)


def _chunk_scan(xr, xi, lr, li, Lc, reverse, prev=None):
    C = xr.shape[1]
    lrb, lib = jnp.broadcast_to(lr, (N_CHUNK, C)), jnp.broadcast_to(li, (N_CHUNK, C))
    zero = jnp.zeros((N_CHUNK, C), F32)

    def step_of(k):
        return (Lc - 1 - k) if reverse else k

    def advance(sr, si, t):
        return lrb * sr - lib * si + xr[_slab(t), :], lrb * si + lib * sr + xi[_slab(t), :]

    def pass1(k, c):
        return advance(c[0], c[1], step_of(k))

    er, ei = lax.fori_loop(0, Lc, pass1, (zero, zero), unroll=4)
    pr, pi = _cpow(lr, li, Lc)
    prb, pib = jnp.broadcast_to(pr, (N_CHUNK, C)), jnp.broadcast_to(pi, (N_CHUNK, C))
    cr, ci = zero, zero
    for _ in range(N_CHUNK - 1):
        cr, ci = _shift_chunks(er + prb * cr - pib * ci, ei + prb * ci + pib * cr, reverse)

    if prev is None:
        def pass2(k, c):
            t = step_of(k)
            nr, ni = advance(c[0], c[1], t)
            xr[_slab(t), :] = nr
            xi[_slab(t), :] = ni
            return nr, ni

        lax.fori_loop(0, Lc, pass2, (cr, ci), unroll=4)
        return None

    qr, qi = prev
    def pair(ar, ai, wr, wi, acc):
        return acc[0] + ar * wr + ai * wi, acc[1] + ai * wr - ar * wi

    def pass2(k, c):
        sr, si, accr, acci = c
        t = step_of(k)
        nr, ni = advance(sr, si, t)
        xr[_slab(t), :] = nr
        xi[_slab(t), :] = ni
        tp = t - 1 if reverse else t + 1
        accr, acci = pair(nr, ni, qr[_slab(tp), :], qi[_slab(tp), :], (accr, acci))
        return nr, ni, accr, acci

    sr, si, accr, acci = lax.fori_loop(0, Lc - 1, pass2, (cr, ci, zero, zero), unroll=4)
    t = step_of(Lc - 1)
    nr, ni = advance(sr, si, t)
    xr[_slab(t), :] = nr
    xi[_slab(t), :] = ni
    edge = step_of(0)
    wr, wi = _shift_chunks(qr[_slab(edge), :], qi[_slab(edge), :], not reverse)
    return pair(nr, ni, wr, wi, (accr, acci))


def _ssm_dims(T):
    assert T % (N_CHUNK * 64) == 0
    rc = _pick(T, (512, 256, 128, 64))
    return T // N_CHUNK, rc, T // rc


def _ssm_project_in(u_ref, bre_ref, bim_ref, xr, xi, rc, nrc):
    for c in range(nrc):
        rows = pl.ds(c * rc, rc)
        ub = u_ref[rows, :]
        xr[rows, :] = jnp.dot(ub, bre_ref[...], preferred_element_type=F32)
        xi[rows, :] = jnp.dot(ub, bim_ref[...], preferred_element_type=F32)


def _ssm_fwd(name, u_bf, mats, vecs, reverse):
    T, W = u_bf.shape
    nb = W // LANES
    SC = SSM_BLOCK_GROUPS * SSM_STATE
    Lc, rc, nrc = _ssm_dims(T)
    bre, bim, cre, cim = mats
    lam_re, lam_im, f_re, f_im = vecs

    def kern(u_ref, bre_ref, bim_ref, cre_ref, cim_ref, lr_ref, li_ref, fr_ref, fi_ref, y_ref, xr, xi):
        _ssm_project_in(u_ref, bre_ref, bim_ref, xr, xi, rc, nrc)
        _chunk_scan(xr, xi, lr_ref[...], li_ref[...], Lc, reverse)
        fr, fi = fr_ref[...], fi_ref[...]
        for c in range(nrc):
            rows = pl.ds(c * rc, rc)
            sr, si = _cmul(fr, fi, xr[rows, :], xi[rows, :])
            y_ref[rows, :] = (jnp.dot(sr.astype(BF16), cre_ref[...], preferred_element_type=F32)
                              - jnp.dot(si.astype(BF16), cim_ref[...], preferred_element_type=F32))

    col = pl.BlockSpec((T, LANES), lambda m: (0, m))
    mat_in = pl.BlockSpec((None, LANES, SC), lambda m: (m, 0, 0))
    mat_out = pl.BlockSpec((None, SC, LANES), lambda m: (m, 0, 0))
    vec = pl.BlockSpec((None, 1, SC), lambda m: (m, 0, 0))
    return pl.pallas_call(
        kern, name=name, grid=(nb,),
        in_specs=[col, mat_in, mat_in, mat_out, mat_out, vec, vec, vec, vec],
        out_specs=col, out_shape=jax.ShapeDtypeStruct((T, W), F32),
        scratch_shapes=[pltpu.VMEM((T, SC), F32), pltpu.VMEM((T, SC), F32)],
        compiler_params=_params(("arbitrary",)),
    )(u_bf, bre, bim, cre, cim, lam_re, lam_im, f_re, f_im)


def _ssm_bwd(name, u_bf, dy_bf, du_in, mats, mats_t, vecs, reverse):
    T, W = u_bf.shape
    nb = W // LANES
    SC = SSM_BLOCK_GROUPS * SSM_STATE
    Lc, rc, nrc = _ssm_dims(T)
    bre, bim = mats
    bre_t, bim_t, cre_t, cim_t = mats_t
    lam_re, lam_im, f_re, f_im = vecs
    contract_rows = (((0,), (0,)), ((), ()))

    def kern(u_ref, dy_ref, dui_ref, bre_ref, bim_ref, bret_ref, bimt_ref, cret_ref, cimt_ref,
             lr_ref, li_ref, fr_ref, fi_ref,
             du_ref, dbre_ref, dbim_ref, dcre_ref, dcim_ref, dlr_ref, dli_ref, dfr_ref, dfi_ref,
             xr, xi, gr, gi):
        lr, li, fr, fi = lr_ref[...], li_ref[...], fr_ref[...], fi_ref[...]
        _ssm_project_in(u_ref, bre_ref, bim_ref, xr, xi, rc, nrc)
        _chunk_scan(xr, xi, lr, li, Lc, reverse)
        mre = jnp.zeros((LANES, SC), F32)
        mim = jnp.zeros((LANES, SC), F32)
        dfr = jnp.zeros((1, SC), F32)
        dfi = jnp.zeros((1, SC), F32)
        for c in range(nrc):
            rows = pl.ds(c * rc, rc)
            dy = dy_ref[rows, :]
            x_r, x_i = xr[rows, :], xi[rows, :]
            g_r = jnp.dot(dy, cret_ref[...], preferred_element_type=F32)
            g_i = -jnp.dot(dy, cimt_ref[...], preferred_element_type=F32)
            mre += lax.dot_general(dy, x_r.astype(BF16), contract_rows, preferred_element_type=F32)
            mim += lax.dot_general(dy, x_i.astype(BF16), contract_rows, preferred_element_type=F32)
            dfr += _colsum(g_r * x_r + g_i * x_i)
            dfi += _colsum(g_i * x_r - g_r * x_i)
            gr[rows, :] = fr * g_r + fi * g_i
            gi[rows, :] = fr * g_i - fi * g_r
        dcre_ref[...] = fr * mre - fi * mim
        dcim_ref[...] = -(fr * mim + fi * mre)
        dfr_ref[...] = dfr
        dfi_ref[...] = dfi
        accr, acci = _chunk_scan(gr, gi, lr, -li, Lc, not reverse, prev=(xr, xi))
        dlr_ref[...] = _colsum(accr)
        dli_ref[...] = _colsum(acci)
        dbre = jnp.zeros((LANES, SC), F32)
        dbim = jnp.zeros((LANES, SC), F32)
        for c in range(nrc):
            rows = pl.ds(c * rc, rc)
            ub = u_ref[rows, :]
            a_r, a_i = gr[rows, :].astype(BF16), gi[rows, :].astype(BF16)
            dbre += lax.dot_general(ub, a_r, contract_rows, preferred_element_type=F32)
            dbim += lax.dot_general(ub, a_i, contract_rows, preferred_element_type=F32)
            du_ref[rows, :] = (dui_ref[rows, :] + jnp.dot(a_r, bret_ref[...], preferred_element_type=F32)
                               + jnp.dot(a_i, bimt_ref[...], preferred_element_type=F32))
        dbre_ref[...] = dbre
        dbim_ref[...] = dbim

    col = pl.BlockSpec((T, LANES), lambda m: (0, m))
    mat_in = pl.BlockSpec((None, LANES, SC), lambda m: (m, 0, 0))
    mat_out = pl.BlockSpec((None, SC, LANES), lambda m: (m, 0, 0))
    vec = pl.BlockSpec((None, 1, SC), lambda m: (m, 0, 0))
    mat_shape = jax.ShapeDtypeStruct((nb, LANES, SC), F32)
    vec_shape = jax.ShapeDtypeStruct((nb, 1, SC), F32)
    return pl.pallas_call(
        kern, name=name, grid=(nb,),
        in_specs=[col, col, col, mat_in, mat_in, mat_out, mat_out, mat_in, mat_in, vec, vec, vec, vec],
        out_specs=[col, mat_in, mat_in, mat_in, mat_in, vec, vec, vec, vec],
        out_shape=[jax.ShapeDtypeStruct((T, W), F32), mat_shape, mat_shape, mat_shape, mat_shape,
                   vec_shape, vec_shape, vec_shape, vec_shape],
        scratch_shapes=[pltpu.VMEM((T, SC), F32)] * 4,
        compiler_params=_params(("arbitrary",)),
    )(u_bf, dy_bf, du_in, bre, bim, bre_t, bim_t, cre_t, cim_t, lam_re, lam_im, f_re, f_im)


def _block_diag_in(b):
    d, G, N, H = b.shape
    nb = G // SSM_BLOCK_GROUPS
    eye = jnp.eye(SSM_BLOCK_GROUPS, dtype=b.dtype)
    v = b.reshape(d, nb, SSM_BLOCK_GROUPS, N, H)
    return jnp.einsum('dmgnh,gk->dmghkn', v, eye).reshape(d, nb, SSM_BLOCK_GROUPS * H, SSM_BLOCK_GROUPS * N)


def _block_diag_in_grad(m, N, H):
    d, nb = m.shape[:2]
    eye = jnp.eye(SSM_BLOCK_GROUPS, dtype=m.dtype)
    v = m.reshape(d, nb, SSM_BLOCK_GROUPS, H, SSM_BLOCK_GROUPS, N)
    return jnp.einsum('dmghkn,gk->dmgnh', v, eye).reshape(d, nb * SSM_BLOCK_GROUPS, N, H)


def _block_diag_out(c):
    return jnp.swapaxes(_block_diag_in(jnp.swapaxes(c, 2, 3)), 2, 3)


def _block_diag_out_grad(m, N, H):
    return jnp.swapaxes(_block_diag_in_grad(m, N, H), 2, 3)


def _to_chunk_rows(a):
    T, C = a.shape
    return a.reshape(N_CHUNK, T // N_CHUNK, C).transpose(1, 0, 2).reshape(T, C)


def _from_chunk_rows(a):
    T, C = a.shape
    return a.reshape(T // N_CHUNK, N_CHUNK, C).transpose(1, 0, 2).reshape(T, C)


def _conv_taps(ref, c, R, T):
    n = T // R
    r0 = pl.multiple_of(c * R, R)
    main = ref[pl.ds(r0, R), :]
    before = ref[pl.ds(pl.multiple_of(jnp.maximum(r0 - 8, 0), 8), 8), :][7:8, :]
    after = ref[pl.ds(pl.multiple_of(jnp.minimum(r0 + R, T - 8), 8), 8), :][0:1, :]
    before = jnp.where(c > 0, before, 0.0)
    after = jnp.where(c < n - 1, after, 0.0)
    rid = lax.broadcasted_iota(jnp.int32, main.shape, 0)
    m1 = jnp.where(rid == 0, before, pltpu.roll(main, 1, 0))
    p1 = jnp.where(rid == R - 1, after, pltpu.roll(main, R - 1, 0))
    return m1, main, p1, r0


def _conv_specs(T, F, cb):
    nj = F // cb
    lo = lambda rows: pl.BlockSpec((rows, cb), lambda j: (0, j))
    hi = lambda rows: pl.BlockSpec((rows, cb), lambda j: (0, j + nj))
    return nj, lo, hi


def _convact_fwd(up, conv_w, conv_b, F):
    T = up.shape[0]
    cb = _pick(F, (256, 128))
    R = _pick(T, (256, 128, 64))
    nj, lo, hi = _conv_specs(T, F, cb)

    def kern(uv, ug, wv, wg, bv, bg, o_ref):
        def body(c, carry):
            v1, v0, v2, r0 = _conv_taps(uv, c, R, T)
            g1, g0, g2, _ = _conv_taps(ug, c, R, T)
            val = v1 * wv[0:1, :] + v0 * wv[1:2, :] + v2 * wv[2:3, :] + bv[...]
            gate = g1 * wg[0:1, :] + g0 * wg[1:2, :] + g2 * wg[2:3, :] + bg[...]
            o_ref[pl.ds(r0, R), :] = (val * (gate * jax.nn.sigmoid(gate))).astype(o_ref.dtype)
            return carry

        lax.fori_loop(0, T // R, body, 0)

    return pl.pallas_call(
        kern, name="convact_fwd", grid=(nj,),
        in_specs=[lo(T), hi(T), lo(3), hi(3), lo(1), hi(1)],
        out_specs=lo(T), out_shape=jax.ShapeDtypeStruct((T, F), BF16),
        compiler_params=_params(("arbitrary",)),
    )(up, up, conv_w, conv_w, conv_b, conv_b)


def _convact_bwd(up, dact, conv_w, conv_b, F):
    T = up.shape[0]
    cb = _pick(F, (256, 128))
    R = _pick(T, (256, 128, 64))
    nj, lo, hi = _conv_specs(T, F, cb)

    def kern(uv, ug, da, wv, wg, bv, bg, dupv, dupg, dwv, dwg, dbv, dbg, sv, sg):
        zero = jnp.zeros((1, cb), F32)

        def pass_a(c, acc):
            v1, v0, v2, r0 = _conv_taps(uv, c, R, T)
            g1, g0, g2, _ = _conv_taps(ug, c, R, T)
            val = v1 * wv[0:1, :] + v0 * wv[1:2, :] + v2 * wv[2:3, :] + bv[...]
            gate = g1 * wg[0:1, :] + g0 * wg[1:2, :] + g2 * wg[2:3, :] + bg[...]
            sig = jax.nn.sigmoid(gate)
            d = da[pl.ds(r0, R), :]
            dval = d * (gate * sig)
            dgate = d * val * (sig * (1.0 + gate * (1.0 - sig)))
            sv[pl.ds(r0, R), :] = dval
            sg[pl.ds(r0, R), :] = dgate
            terms = (dval * v1, dval * v0, dval * v2, dval, dgate * g1, dgate * g0, dgate * g2, dgate)
            return tuple(a + _colsum(t) for a, t in zip(acc, terms))

        acc = lax.fori_loop(0, T // R, pass_a, (zero,) * 8)
        for k in range(3):
            dwv[k:k + 1, :] = acc[k]
            dwg[k:k + 1, :] = acc[4 + k]
        dbv[...] = acc[3]
        dbg[...] = acc[7]

        def pass_b(c, carry):
            v1, v0, v2, r0 = _conv_taps(sv, c, R, T)
            g1, g0, g2, _ = _conv_taps(sg, c, R, T)
            dupv[pl.ds(r0, R), :] = (v2 * wv[0:1, :] + v0 * wv[1:2, :] + v1 * wv[2:3, :]).astype(dupv.dtype)
            dupg[pl.ds(r0, R), :] = (g2 * wg[0:1, :] + g0 * wg[1:2, :] + g1 * wg[2:3, :]).astype(dupg.dtype)
            return carry

        lax.fori_loop(0, T // R, pass_b, 0)

    dupv, dupg, dwv, dwg, dbv, dbg = pl.pallas_call(
        kern, name="convact_bwd", grid=(nj,),
        in_specs=[lo(T), hi(T), lo(T), lo(3), hi(3), lo(1), hi(1)],
        out_specs=[lo(T), lo(T), lo(3), lo(3), lo(1), lo(1)],
        out_shape=[jax.ShapeDtypeStruct((T, F), BF16), jax.ShapeDtypeStruct((T, F), BF16),
                   jax.ShapeDtypeStruct((3, F), F32), jax.ShapeDtypeStruct((3, F), F32),
                   jax.ShapeDtypeStruct((1, F), F32), jax.ShapeDtypeStruct((1, F), F32)],
        scratch_shapes=[pltpu.VMEM((T, cb), F32), pltpu.VMEM((T, cb), F32)],
        compiler_params=_params(("arbitrary",)),
    )(up, up, dact, conv_w, conv_w, conv_b, conv_b)
    return (jnp.concatenate([dupv, dupg], axis=1), jnp.concatenate([dwv, dwg], axis=1),
            jnp.concatenate([dbv, dbg], axis=1))


def _exchange(name, bufs, modes):
    n = len(bufs)

    def kern(*refs):
        ins, outs = refs[:n], refs[n:2 * n]
        send_sems, recv_sems, local_sems = refs[2 * n:]
        x, y, c = lax.axis_index("x"), lax.axis_index("y"), lax.axis_index("c")
        me = 4 * x + 2 * y + c
        started = []
        for b in range(n):
            gather = modes[b] == 'gather'
            own = pltpu.make_async_copy(ins[b] if gather else ins[b].at[me], outs[b].at[me], local_sems.at[b])
            own.start()
            started.append(own)
            for k in range(1, N_DEV):
                px = 1 - x if k & 4 else x
                py = 1 - y if k & 2 else y
                pc = 1 - c if k & 1 else c
                peer = 4 * px + 2 * py + pc
                sem = b * (N_DEV - 1) + k - 1
                pltpu.make_async_remote_copy(
                    src_ref=ins[b] if gather else ins[b].at[peer], dst_ref=outs[b].at[me],
                    send_sem=send_sems.at[sem], recv_sem=recv_sems.at[sem],
                    device_id=(px, py, pc), device_id_type=MESH_ID).start()
        for b in range(n):
            gather = modes[b] == 'gather'
            for k in range(1, N_DEV):
                px = 1 - x if k & 4 else x
                py = 1 - y if k & 2 else y
                pc = 1 - c if k & 1 else c
                peer = 4 * px + 2 * py + pc
                sem = b * (N_DEV - 1) + k - 1
                arrived = pltpu.make_async_remote_copy(
                    src_ref=ins[b] if gather else ins[b].at[peer], dst_ref=outs[b].at[peer],
                    send_sem=send_sems.at[sem], recv_sem=recv_sems.at[sem],
                    device_id=(px, py, pc), device_id_type=MESH_ID)
                arrived.wait_recv()
                arrived.wait_send()
        for own in started:
            own.wait()

    hbm = pl.BlockSpec(memory_space=pltpu.HBM)
    out_shape = [jax.ShapeDtypeStruct((N_DEV,) + tuple(b.shape[-2:]), b.dtype) for b in bufs]
    return pl.pallas_call(
        kern, name=name, in_specs=[hbm] * n, out_specs=[hbm] * n, out_shape=out_shape,
        scratch_shapes=[pltpu.SemaphoreType.DMA((n * (N_DEV - 1),)), pltpu.SemaphoreType.DMA((n * (N_DEV - 1),)),
                        pltpu.SemaphoreType.DMA((n,))],
    )(*bufs)


def _peers(x, y, c):
    out = []
    for k in range(1, N_DEV):
        px = 1 - x if k & 4 else x
        py = 1 - y if k & 2 else y
        pc = 1 - c if k & 1 else c
        out.append((k, (px, py, pc), 4 * px + 2 * py + pc))
    return out


def _exchange_start(name, bufs, modes):
    n = len(bufs)
    ncopy = n * (N_DEV - 1)
    lands = [lax.empty((N_DEV,) + tuple(b.shape[-2:]), b.dtype) for b in bufs]

    def body(*refs):
        ins, land_in = refs[:n], refs[n:2 * n]
        send_sems, recv_sems = refs[2 * n], refs[2 * n + 1]
        token = refs[-1]
        x, y, c = lax.axis_index("x"), lax.axis_index("y"), lax.axis_index("c")
        me = 4 * x + 2 * y + c
        for b in range(n):
            for k, peer, slot in _peers(x, y, c):
                sem = b * (N_DEV - 1) + k - 1
                pltpu.make_async_remote_copy(
                    src_ref=ins[b] if modes[b] == 'gather' else ins[b].at[slot], dst_ref=land_in[b].at[me],
                    send_sem=send_sems.at[sem], recv_sem=recv_sems.at[sem],
                    device_id=peer, device_id_type=MESH_ID).start()
        token[...] = jnp.zeros_like(token)

    hbm = pl.BlockSpec(memory_space=pltpu.HBM)
    sem_spec = pl.BlockSpec(memory_space=pltpu.SEMAPHORE)
    operands = [pltpu.with_memory_space_constraint(a, pltpu.HBM) for a in list(bufs) + lands]
    res = pl.pallas_call(
        body, name=name,
        out_shape=(pltpu.SemaphoreType.DMA((ncopy,)), pltpu.SemaphoreType.DMA((ncopy,)),
                   *[pltpu.HBM(a.shape, a.dtype) for a in operands], jax.ShapeDtypeStruct((8, LANES), F32)),
        in_specs=[hbm] * (2 * n),
        out_specs=(sem_spec, sem_spec, *([hbm] * (2 * n)), pl.BlockSpec(memory_space=pltpu.VMEM)),
        input_output_aliases={i: 2 + i for i in range(2 * n)},
        compiler_params=pltpu.CompilerParams(has_side_effects=pltpu.SideEffectType.DATAFLOW_SIDE_EFFECTING),
    )(*operands)
    return (res[0], res[1], list(res[2:2 + n]), list(res[2 + n:2 + 2 * n]), tuple(modes)), res[-1]


def _exchange_wait(name, handle, after):
    send_sems, recv_sems, srcs, lands, modes = handle
    n = len(srcs)

    def body(*refs):
        src_in, land_in = refs[:n], refs[n:2 * n]
        send_ref, recv_ref = refs[2 * n], refs[2 * n + 1]
        x, y, c = lax.axis_index("x"), lax.axis_index("y"), lax.axis_index("c")
        for b in range(n):
            for k, peer, slot in _peers(x, y, c):
                sem = b * (N_DEV - 1) + k - 1
                copy = pltpu.make_async_remote_copy(
                    src_ref=src_in[b] if modes[b] == 'gather' else src_in[b].at[slot], dst_ref=land_in[b].at[slot],
                    send_sem=send_ref.at[sem], recv_sem=recv_ref.at[sem],
                    device_id=peer, device_id_type=MESH_ID)
                copy.wait_send()
                copy.wait_recv()

    hbm = pl.BlockSpec(memory_space=pltpu.HBM)
    sem_spec = pl.BlockSpec(memory_space=pltpu.SEMAPHORE)
    res = pl.pallas_call(
        body, name=name,
        out_shape=tuple(pltpu.HBM(a.shape, a.dtype) for a in srcs + lands),
        in_specs=[hbm] * (2 * n) + [sem_spec, sem_spec, pl.BlockSpec(memory_space=pl.ANY)],
        out_specs=tuple([hbm] * (2 * n)),
        input_output_aliases={i: i for i in range(2 * n)},
        compiler_params=pltpu.CompilerParams(has_side_effects=pltpu.SideEffectType.DATAFLOW_SIDE_EFFECTING),
    )(*srcs, *lands, send_sems, recv_sems, after)
    return list(res[n:])


def _with_own(landed, own, me):
    return lax.dynamic_update_slice(landed, own[None], (me,) + (0,) * own.ndim)


def _adamw(w, g, m, v):
    m2 = ADAM_B1 * m + (1.0 - ADAM_B1) * g
    v2 = ADAM_B2 * v + (1.0 - ADAM_B2) * (g * g)
    m_hat = m2 / (1.0 - ADAM_B1 ** ADAM_STEP)
    v_hat = v2 / (1.0 - ADAM_B2 ** ADAM_STEP)
    return -ADAM_LR * (m_hat / (jnp.sqrt(v_hat) + ADAM_EPS) + ADAM_WD * w), m2, v2


def _sum_adamw(name, landed, w, m, v):
    R, C = w.shape
    row_bytes = N_DEV * (-(-C // LANES) * LANES) * landed.dtype.itemsize
    tiles = [d for d in range(16, R, 16) if R % d == 0 and d * row_bytes <= ADAM_TILE_BYTES]
    tr = max(tiles) if tiles and R * row_bytes > ADAM_TILE_BYTES else R

    def kern(x_ref, w_ref, m_ref, v_ref, g_out, d_out, m_out, v_out):
        g = x_ref[0].astype(F32)
        for q in range(1, N_DEV):
            g = g + x_ref[q].astype(F32)
        g_out[...] = g
        d_out[...], m_out[...], v_out[...] = _adamw(w_ref[...], g, m_ref[...], v_ref[...])

    blk = pl.BlockSpec((tr, C), lambda i: (i, 0))
    return pl.pallas_call(
        kern, name=name, grid=(R // tr,),
        in_specs=[pl.BlockSpec((N_DEV, tr, C), lambda i: (0, i, 0)), blk, blk, blk],
        out_specs=[blk] * 4, out_shape=[jax.ShapeDtypeStruct((R, C), F32)] * 4,
        compiler_params=_params(("arbitrary",)),
    )(landed, w, m, v)


def _view2d(a):
    if a.ndim == 1:
        return a.reshape(1, -1)
    return a.reshape(-1, a.shape[-1])


def kernel(x, norm_mix_g, w_in, pool_w, pool_scale, ssm_log_neg_a_re, ssm_a_im, ssm_log_dt, ssm_b_re, ssm_b_im, ssm_c_re, ssm_c_im, ssm_d, glu_w, glu_b, out_norm_pool_g, out_norm_ssm_g, w_out, norm_ffn_g, w_up, conv_w, conv_b, w_down, final_norm_g, loss_target, m_norm_mix_g, m_w_in, m_pool_w, m_pool_scale, m_ssm_log_neg_a_re, m_ssm_a_im, m_ssm_log_dt, m_ssm_b_re, m_ssm_b_im, m_ssm_c_re, m_ssm_c_im, m_ssm_d, m_glu_w, m_glu_b, m_out_norm_pool_g, m_out_norm_ssm_g, m_w_out, m_norm_ffn_g, m_w_up, m_conv_w, m_conv_b, m_w_down, m_final_norm_g, v_norm_mix_g, v_w_in, v_pool_w, v_pool_scale, v_ssm_log_neg_a_re, v_ssm_a_im, v_ssm_log_dt, v_ssm_b_re, v_ssm_b_im, v_ssm_c_re, v_ssm_c_im, v_ssm_d, v_glu_w, v_glu_b, v_out_norm_pool_g, v_out_norm_ssm_g, v_w_out, v_norm_ffn_g, v_w_up, v_conv_w, v_conv_b, v_w_down, v_final_norm_g):
    given = dict(locals())
    weights = {n: given[n] for n in WEIGHTS}
    mom1 = {n: given["m_" + n] for n in WEIGHTS}
    mom2 = {n: given["v_" + n] for n in WEIGHTS}

    xs = x[0]
    tgt = loss_target[0]
    T, D = xs.shape
    DP = len(POOL_WINDOWS) * POOL_GROUP
    DS = D - DP
    G = DS // SSM_GROUP
    N, H = SSM_STATE, SSM_GROUP
    F2 = w_up.shape[1] * N_DEV
    F = F2 // 2
    fs = w_up.shape[1]
    row = lambda a: a.reshape(1, -1)

    me = 4 * lax.axis_index("x") + 2 * lax.axis_index("y") + lax.axis_index("c")
    late_src = [w_out.astype(BF16), w_down.astype(BF16), w_up.astype(BF16), conv_w]
    late_handle, token = _exchange_start("gather_late_start", late_src, ['gather'] * 4)
    early = _exchange("gather_early", [(w_in + token[0, 0]).astype(BF16), glu_w.astype(BF16)], ['gather'] * 2)
    W_in = early[0].reshape(D, D)
    W_glu = early[1].reshape(DS, DS)
    CB = row(conv_b)

    g1, g2, g3 = row(norm_mix_g), row(norm_ffn_g), row(final_norm_g)
    gp, gs = row(out_norm_pool_g), row(out_norm_ssm_g)
    (xn,) = _rowmap("norm_mix", _rms_fwd, [xs], [g1], [(D, BF16)])
    u = _mm("proj_in", xn, W_in, 'nn', F32)
    tmp = _pick(T, (256, 128))
    u_pad = jnp.pad(u[:, :DP], ((POOL_PAD, POOL_PAD), (0, 0)))
    pool_w_bf = pool_w.astype(BF16)
    ypn = _pool_fwd(u_pad, pool_w_bf, row(pool_scale), gp, T, tmp)

    u_ssm = _to_chunk_rows(u[:, DP:])
    u_ssm_bf = u_ssm.astype(BF16)
    a_rows = (ssm_log_neg_a_re.reshape(2 * G, N), ssm_a_im.reshape(2 * G, N), ssm_log_dt.reshape(2 * G, 1))
    disc = _ssm_params_fwd(*a_rows)
    nb = G // SSM_BLOCK_GROUPS
    vecs = [d.reshape(2, nb, 1, SSM_BLOCK_GROUPS * N) for d in disc]
    b_in = [_block_diag_in(b) for b in (ssm_b_re, ssm_b_im)]
    c_out = [_block_diag_out(c) for c in (ssm_c_re, ssm_c_im)]
    mats_bf = [[m[d].astype(BF16) for m in b_in + c_out] for d in range(2)]
    mats_t_bf = [[jnp.swapaxes(m[d], 1, 2).astype(BF16) for m in b_in + c_out] for d in range(2)]
    dir_vecs = [[v[d] for v in vecs] for d in range(2)]
    y_dir = [_ssm_fwd("ssm_fwd_%d" % d, u_ssm_bf, mats_bf[d], dir_vecs[d], reverse=bool(d)) for d in range(2)]

    def mix_post(yf, yb, us, d, gw, gb, g):
        y = yf + yb + d * us
        z = _gelu(y)
        gate = jax.nn.sigmoid(jnp.dot(z.astype(BF16), gw, preferred_element_type=F32) + gb)
        return _rms_fwd(z * gate, g), y

    ysn, y_ssm = _rowmap("ssm_post", mix_post, [y_dir[0], y_dir[1], u_ssm], [row(ssm_d), W_glu, row(glu_b), gs],
                         [(DS, BF16), (DS, F32)])
    ycat = jnp.concatenate([ypn, _from_chunk_rows(ysn)], axis=1)
    late = [_with_own(l, s, me) for l, s in zip(_exchange_wait("gather_late_wait", late_handle, ysn), late_src)]
    W_out = late[0].reshape(D, D)
    W_down = late[1].reshape(F, D)
    W_up = late[2].transpose(1, 0, 2).reshape(D, F2)
    CW = late[3].transpose(1, 0, 2).reshape(3, F2)
    h1 = _mm("proj_out", ycat, W_out, 'nn', F32, add=xs)
    (hn,) = _rowmap("norm_ffn", _rms_fwd, [h1], [g2], [(D, BF16)])
    up = _mm("ffn_up", hn, W_up, 'nn', F32)
    act = _convact_fwd(up, CW, CB, F)
    h2 = _mm("ffn_down", act, W_down, 'nn', F32, add=h1)

    def head(h, t, g):
        r = lax.rsqrt(_rowmean(h * h) + EPS)
        hh = h * r
        e = hh * g - t
        loss = 0.5 * jnp.sum(_rowmean(e * e), keepdims=True)
        dy = e * (1.0 / D)
        dxh = dy * g
        dh = r * (dxh - hh * _rowmean(dxh * hh))
        return dh, dh, jnp.broadcast_to(loss, (1, LANES)), _colsum(dy * hh)

    dh2, dh2_bf, loss_acc, dg3 = _rowmap("loss_head", head, [h2, tgt], [g3], [(D, F32), (D, BF16)], [(1, LANES), (1, D)])
    loss = lax.psum(loss_acc[0, 0], ("x", "y", "c"))

    dact = _mm("ffn_down_dx", dh2_bf, W_down, 'nt', F32)
    dW_down = _mm("ffn_down_dw", act, dh2_bf, 'tn', BF16)
    dup, dCW, dCB = _convact_bwd(up, dact, CW, CB, F)
    dhn = _mm("ffn_up_dx", dup, W_up, 'nt', F32)
    dW_up = _mm("ffn_up_dw", hn, dup, 'tn', BF16)

    def shards(full_grad):
        return full_grad.reshape((N_DEV, full_grad.shape[0] // N_DEV) + full_grad.shape[1:])

    ffn_sent = [shards(dW_down), dW_up.reshape(D, N_DEV, fs).transpose(1, 0, 2), dCW.reshape(3, N_DEV, fs).transpose(1, 0, 2)]
    ffn_handle, token = _exchange_start("scatter_ffn_start", ffn_sent, ['scatter'] * 3)
    g2_late = g2 + token[0:1, 0:1]

    def norm_bwd_add(hx, dy, res, g):
        dx, dg = _rms_bwd(hx, g, dy)
        return dx + res, dg

    def norm_bwd_add2(hx, dy, res, g):
        dx, dg = _rms_bwd(hx, g, dy)
        return dx + res, dx + res, dg

    dh1, dh1_bf, dg2 = _rowmap("norm_ffn_bwd", norm_bwd_add2, [h1, dhn, dh2], [g2_late], [(D, F32), (D, BF16)], [(1, D)])
    dycat = _mm("proj_out_dx", dh1_bf, W_out, 'nt', F32)
    dW_out = _mm("proj_out_dw", ycat, dh1_bf, 'tn', BF16)

    q, dpooled, dpool_w, dpool_scale, dgp = _pool_bwd_rows(u_pad, dycat[:, :DP], pool_w_bf, row(pool_scale), gp, T, tmp)
    du_pool = _pool_bwd_band(jnp.pad(q, ((POOL_PAD, POOL_PAD), (0, 0))), dpooled, T, tmp)

    def mix_post_bwd(y, us, dyn, d, gw, gb, g):
        z = _gelu(y)
        gz = z.astype(BF16)
        sig = jax.nn.sigmoid(jnp.dot(gz, gw, preferred_element_type=F32) + gb)
        dys, dg = _rms_bwd(z * sig, g, dyn)
        dgl = dys * z * sig * (1.0 - sig)
        dgl_bf = dgl.astype(BF16)
        dz = dys * sig + lax.dot_general(dgl_bf, gw, (((1,), (1,)), ((), ())), preferred_element_type=F32)
        dgw = lax.dot_general(gz, dgl_bf, (((0,), (0,)), ((), ())), preferred_element_type=F32)
        dyv = dz * _gelu_grad(y)
        return dyv, dyv * d, dgw, _colsum(dgl), _colsum(dyv * us), dg

    d_ysn = _to_chunk_rows(dycat[:, DP:])
    dyv, du_dir, dglu_w, dglu_b, dssm_d, dgs = _rowmap(
        "ssm_post_bwd", mix_post_bwd, [y_ssm, u_ssm, d_ysn], [row(ssm_d), W_glu, row(glu_b), gs],
        [(DS, BF16), (DS, F32)], [(DS, DS), (1, DS), (1, DS), (1, DS)])

    du_ssm = du_dir
    dB, dC, dvec = [], [], []
    for d in range(2):
        b_re, b_im, c_re, c_im = mats_bf[d]
        b_re_t, b_im_t, c_re_t, c_im_t = mats_t_bf[d]
        res = _ssm_bwd("ssm_bwd_%d" % d, u_ssm_bf, dyv, du_ssm, (b_re, b_im), (b_re_t, b_im_t, c_re_t, c_im_t),
                       dir_vecs[d], reverse=bool(d))
        du_ssm = res[0]
        dB.append(res[1:3])
        dC.append(res[3:5])
        dvec.append(res[5:9])
    grad_b = [_block_diag_in_grad(jnp.stack([dB[0][k], dB[1][k]]), N, H) for k in range(2)]
    grad_c = [_block_diag_out_grad(jnp.stack([dC[0][k], dC[1][k]]), N, H) for k in range(2)]
    cots = [jnp.stack([dvec[0][k], dvec[1][k]]).reshape(2 * G, N) for k in range(4)]
    d_a_re, d_a_im, d_log_dt = _ssm_params_bwd(*a_rows, cots)

    du = jnp.concatenate([du_pool, _from_chunk_rows(du_ssm)], axis=1).astype(BF16)
    dxn = _mm("proj_in_dx", du, W_in, 'nt', F32)
    dW_in = _mm("proj_in_dw", xn, du, 'tn', BF16)
    dx, dg1 = _rowmap("norm_mix_bwd", norm_bwd_add, [xs, dxn, dh1], [g1], [(D, F32)], [(1, D)])

    grads, delta, new_m, new_v = {}, {}, {}, {}

    def update(n, landed_n):
        shape = weights[n].shape
        landed_n = landed_n.reshape((N_DEV,) + _view2d(weights[n]).shape)
        res = _sum_adamw("adamw_" + n, landed_n, _view2d(weights[n]), _view2d(mom1[n]), _view2d(mom2[n]))
        grads[n], delta[n], new_m[n], new_v[n] = [r.reshape(shape) for r in res]

    ffn_landed = _exchange_wait("scatter_ffn_wait", ffn_handle, dx)
    for n, l, s in zip(('w_down', 'w_up', 'conv_w'), ffn_landed, ffn_sent):
        update(n, _with_own(l, lax.dynamic_index_in_dim(s, me, 0, keepdims=False), me))

    rep_grads = {
        'norm_mix_g': dg1, 'pool_w': dpool_w, 'pool_scale': dpool_scale, 'ssm_log_neg_a_re': d_a_re, 'ssm_a_im': d_a_im,
        'ssm_log_dt': d_log_dt, 'ssm_b_re': grad_b[0], 'ssm_b_im': grad_b[1], 'ssm_c_re': grad_c[0], 'ssm_c_im': grad_c[1],
        'ssm_d': dssm_d, 'glu_b': dglu_b, 'out_norm_pool_g': dgp, 'out_norm_ssm_g': dgs, 'norm_ffn_g': dg2,
        'conv_b': dCB, 'final_norm_g': dg3}
    wide = ('pool_w', 'ssm_b_re', 'ssm_b_im', 'ssm_c_re', 'ssm_c_im')
    def travel(n):
        g = rep_grads[n]
        if n in wide:
            return g.reshape(-1, PACK_W if g.size % PACK_W == 0 else LANES).astype(BF16)
        return _view2d(g.reshape(weights[n].shape))

    mixer = ('w_in', 'w_out', 'glu_w')
    sent = [shards(dW_in), shards(dW_out), shards(dglu_w.astype(BF16))] + [travel(n) for n in REPLICATED]
    landed = _exchange("reduce_rest", sent, ['scatter'] * 3 + ['gather'] * len(REPLICATED))
    for n, landed_n in zip(mixer + REPLICATED, landed):
        update(n, landed_n)

    return (loss, dx[None], *[grads[n] for n in WEIGHTS], *[delta[n] for n in WEIGHTS],
            *[new_m[n] for n in WEIGHTS], *[new_v[n] for n in WEIGHTS])
```

```python
import functools

import jax
import jax.numpy as jnp
import numpy as np
from jax import lax
from jax.experimental import pallas as pl
from jax.experimental.pallas import tpu as pltpu

F32 = jnp.float32
BF16 = jnp.bfloat16
MESH_ID = pl.DeviceIdType.MESH

N_DEV = 8
EPS = 1e-6
POOL_WINDOWS = (2, 4, 8, 16)
POOL_GROUP = 128
POOL_PAD = 64
SSM_GROUP = 16
SSM_STATE = 64
SSM_BLOCK_GROUPS = 8
N_CHUNK = 8
LANES = 128
PACK_W = 1024
VMEM_LIMIT = 56 * 1024 * 1024
MM_VMEM_BUDGET = 40 * 1024 * 1024
MM_TILE_CAP = 1408
ADAM_TILE_BYTES = 4 * 1024 * 1024

ADAM_LR = 0.001
ADAM_B1 = 0.9
ADAM_B2 = 0.999
ADAM_EPS = 1e-08
ADAM_WD = 0.01
ADAM_STEP = 10

WEIGHTS = ['norm_mix_g', 'w_in', 'pool_w', 'pool_scale', 'ssm_log_neg_a_re', 'ssm_a_im', 'ssm_log_dt',
           'ssm_b_re', 'ssm_b_im', 'ssm_c_re', 'ssm_c_im', 'ssm_d', 'glu_w', 'glu_b', 'out_norm_pool_g',
           'out_norm_ssm_g', 'w_out', 'norm_ffn_g', 'w_up', 'conv_w', 'conv_b', 'w_down', 'final_norm_g']
SHARDED = ('w_in', 'w_out', 'w_down', 'glu_w', 'w_up', 'conv_w')
REPLICATED = tuple(n for n in WEIGHTS if n not in SHARDED)


def _pick(n, prefs):
    for p in prefs:
        if n % p == 0:
            return p
    return n


def _params(sem, vmem=None):
    return pltpu.CompilerParams(dimension_semantics=sem, vmem_limit_bytes=vmem or VMEM_LIMIT)


def _tiles(n, cap):
    return [d for d in range(LANES, min(n, cap) + 1, LANES) if n % d == 0] or [n]


def _mm(name, a, b, mode, out_dtype, add=None):
    if mode == 'nn':
        (M, K), (_, N) = a.shape, b.shape
    elif mode == 'nt':
        (M, K), (N, _) = a.shape, b.shape
    else:
        (K, M), (_, N) = a.shape, b.shape
    dims = {'nn': (((1,), (0,)), ((), ())), 'nt': (((1,), (1,)), ((), ())), 'tn': (((0,), (0,)), ((), ()))}[mode]
    sa, sb, so = a.dtype.itemsize, b.dtype.itemsize, jnp.dtype(out_dtype).itemsize
    best = None
    for tm in _tiles(M, MM_TILE_CAP):
        for tn in _tiles(N, MM_TILE_CAP):
            need = 2 * (tm * K * sa + tn * K * sb + tm * tn * (so + (4 if add is not None else 0)))
            if need <= MM_VMEM_BUDGET:
                key = (tm * tn / (tm + tn), tm * tn)
                if best is None or key > best[0]:
                    best = (key, tm, tn)
    _, tm, tn = best
    rows_inner = a.size * sa * (N // tn) + b.size * sb < a.size * sa + b.size * sb * (M // tm)

    def kern(*refs):
        a_ref, b_ref = refs[:2]
        o_ref = refs[-1]
        r = lax.dot_general(a_ref[...].astype(BF16), b_ref[...].astype(BF16), dims, preferred_element_type=F32)
        if add is not None:
            r = r + refs[2][...]
        o_ref[...] = r.astype(o_ref.dtype)

    if rows_inner:
        grid = (N // tn, M // tm)
        ij = lambda g0, g1: (g1, g0)
    else:
        grid = (M // tm, N // tn)
        ij = lambda g0, g1: (g0, g1)
    a_spec = (pl.BlockSpec((K, tm), lambda g0, g1: (0, ij(g0, g1)[0])) if mode == 'tn'
              else pl.BlockSpec((tm, K), lambda g0, g1: (ij(g0, g1)[0], 0)))
    b_spec = (pl.BlockSpec((tn, K), lambda g0, g1: (ij(g0, g1)[1], 0)) if mode == 'nt'
              else pl.BlockSpec((K, tn), lambda g0, g1: (0, ij(g0, g1)[1])))
    in_specs = [a_spec, b_spec]
    args = [a, b]
    if add is not None:
        in_specs.append(pl.BlockSpec((tm, tn), lambda g0, g1: ij(g0, g1)))
        args.append(add)
    return pl.pallas_call(
        kern, name=name, grid=grid, in_specs=in_specs,
        out_specs=pl.BlockSpec((tm, tn), lambda g0, g1: ij(g0, g1)),
        out_shape=jax.ShapeDtypeStruct((M, N), out_dtype),
        compiler_params=_params(("parallel", "parallel")),
    )(*args)


def _rowmap(name, fn, rows, fulls, out_rows, out_accs=(), tm=256):
    T = rows[0].shape[0]
    tm = min(tm, T)
    assert T % tm == 0
    n_in, n_row = len(rows) + len(fulls), len(out_rows)

    def kern(*refs):
        i = pl.program_id(0)
        res = fn(*[r[...] for r in refs[:n_in]])
        res = res if isinstance(res, (tuple, list)) else (res,)
        outs = refs[n_in:]
        for o, v in zip(outs[:n_row], res[:n_row]):
            o[...] = v.astype(o.dtype)
        for o, v in zip(outs[n_row:], res[n_row:]):
            @pl.when(i == 0)
            def _(o=o, v=v):
                o[...] = v

            @pl.when(i != 0)
            def _(o=o, v=v):
                o[...] += v

    def full_spec(shape):
        nd = len(shape)
        return pl.BlockSpec(tuple(shape), lambda i: (0,) * nd)

    in_specs = [pl.BlockSpec((tm, r.shape[1]), lambda i: (i, 0)) for r in rows] + [full_spec(f.shape) for f in fulls]
    out_specs = [pl.BlockSpec((tm, c), lambda i: (i, 0)) for c, _ in out_rows] + [full_spec(s) for s in out_accs]
    out_shape = [jax.ShapeDtypeStruct((T, c), dt) for c, dt in out_rows] + [jax.ShapeDtypeStruct(tuple(s), F32) for s in out_accs]
    return pl.pallas_call(
        kern, name=name, grid=(T // tm,), in_specs=in_specs, out_specs=out_specs, out_shape=out_shape,
        compiler_params=_params(("arbitrary",)),
    )(*rows, *fulls)


def _colsum(v):
    return jnp.sum(v, axis=0, keepdims=True)


def _rowmean(v):
    return jnp.mean(v, axis=-1, keepdims=True)


def _rms_fwd(x, g):
    r = lax.rsqrt(_rowmean(x * x) + EPS)
    return x * r * g


def _rms_bwd(x, g, dy):
    r = lax.rsqrt(_rowmean(x * x) + EPS)
    xh = x * r
    dxh = dy * g
    return r * (dxh - xh * _rowmean(dxh * xh)), _colsum(dy * xh)


def _gelu(y):
    c = np.sqrt(2.0 / np.pi).astype(np.float32)
    return 0.5 * y * (1.0 + jnp.tanh(c * (y + 0.044715 * (y * y * y))))


def _gelu_grad(y):
    c = np.sqrt(2.0 / np.pi).astype(np.float32)
    th = jnp.tanh(c * (y + 0.044715 * (y * y * y)))
    return 0.5 * (1.0 + th) + 0.5 * y * (1.0 - th * th) * c * (1.0 + 3.0 * 0.044715 * (y * y))


def _split3(v):
    hi = v.astype(BF16)
    r1 = v - hi.astype(F32)
    mid = r1.astype(BF16)
    lo = (r1 - mid.astype(F32)).astype(BF16)
    return hi, mid, lo


def _band(tm, lo_off, hi_off):
    shape = (tm, tm + 2 * POOL_PAD)
    d = lax.broadcasted_iota(jnp.int32, shape, 1) - lax.broadcasted_iota(jnp.int32, shape, 0) - POOL_PAD
    return ((d >= lo_off) & (d < hi_off)).astype(BF16)


def _band_sum(band, ext):
    hi, mid, lo = _split3(ext)
    dot = functools.partial(jnp.dot, preferred_element_type=F32)
    return dot(band, hi) + dot(band, mid) + dot(band, lo)


def _window_count(r0, tm, half, T):
    t = r0 + lax.broadcasted_iota(jnp.int32, (tm, 1), 0)
    return (jnp.minimum(t + half, T) - jnp.maximum(t - half, 0)).astype(F32)


def _pool_block(u_ref, w_ref, r0, tm, T):
    ext = u_ref[pl.ds(r0, tm + 2 * POOL_PAD), :]
    ctr = u_ref[pl.ds(r0 + POOL_PAD, tm), :]
    pooled, conc = [], []
    for gi, w in enumerate(POOL_WINDOWS):
        half = w // 2
        cols = slice(gi * POOL_GROUP, (gi + 1) * POOL_GROUP)
        ws = _band_sum(_band(tm, -half, half), ext[:, cols])
        p = ws / _window_count(r0, tm, half, T) - ctr[:, cols]
        pooled.append(p)
        conc.append(jnp.dot(p.astype(BF16), w_ref[gi], preferred_element_type=F32))
    return pooled, jnp.concatenate(conc, axis=1)


def _pool_fwd(u_pad, pool_w_bf, pool_scale, g_pool, T, tm):
    nw = len(POOL_WINDOWS)
    C = nw * POOL_GROUP

    def kern(u_ref, w_ref, sc_ref, g_ref, o_ref):
        r0 = pl.multiple_of(pl.program_id(0) * tm, tm)
        _, conc = _pool_block(u_ref, w_ref, r0, tm, T)
        o_ref[...] = _rms_fwd(conc * sc_ref[...], g_ref[...]).astype(o_ref.dtype)

    return pl.pallas_call(
        kern, name="pool_fwd", grid=(T // tm,),
        in_specs=[pl.BlockSpec(u_pad.shape, lambda i: (0, 0)), pl.BlockSpec(pool_w_bf.shape, lambda i: (0, 0, 0)),
                  pl.BlockSpec((1, C), lambda i: (0, 0)), pl.BlockSpec((1, C), lambda i: (0, 0))],
        out_specs=pl.BlockSpec((tm, C), lambda i: (i, 0)),
        out_shape=jax.ShapeDtypeStruct((T, C), BF16),
        compiler_params=_params(("arbitrary",)),
    )(u_pad, pool_w_bf, pool_scale, g_pool)


def _pool_bwd_rows(u_pad, d_ypn, pool_w_bf, pool_scale, g_pool, T, tm):
    nw = len(POOL_WINDOWS)
    C = nw * POOL_GROUP

    def kern(u_ref, dy_ref, w_ref, sc_ref, g_ref, q_ref, dp_ref, dw_ref, dsc_ref, dg_ref):
        i = pl.program_id(0)
        r0 = pl.multiple_of(i * tm, tm)
        pooled, conc = _pool_block(u_ref, w_ref, r0, tm, T)
        sc = sc_ref[...]
        dyp, dg = _rms_bwd(conc * sc, g_ref[...], dy_ref[...])
        dsc = _colsum(dyp * conc)
        dconc = (dyp * sc).astype(BF16)
        dws = []
        for gi, w in enumerate(POOL_WINDOWS):
            cols = slice(gi * POOL_GROUP, (gi + 1) * POOL_GROUP)
            dc = dconc[:, cols]
            dp = lax.dot_general(dc, w_ref[gi], (((1,), (1,)), ((), ())), preferred_element_type=F32)
            dp_ref[:, cols] = dp
            q_ref[:, cols] = dp / _window_count(r0, tm, w // 2, T)
            dws.append(lax.dot_general(pooled[gi].astype(BF16), dc, (((0,), (0,)), ((), ())), preferred_element_type=F32))

        @pl.when(i == 0)
        def _():
            for gi in range(nw):
                dw_ref[gi] = dws[gi]
            dsc_ref[...] = dsc
            dg_ref[...] = dg

        @pl.when(i != 0)
        def _():
            for gi in range(nw):
                dw_ref[gi] += dws[gi]
            dsc_ref[...] += dsc
            dg_ref[...] += dg

    full2 = pl.BlockSpec((1, C), lambda i: (0, 0))
    return pl.pallas_call(
        kern, name="pool_bwd_rows", grid=(T // tm,),
        in_specs=[pl.BlockSpec(u_pad.shape, lambda i: (0, 0)), pl.BlockSpec((tm, C), lambda i: (i, 0)),
                  pl.BlockSpec(pool_w_bf.shape, lambda i: (0, 0, 0)), full2, full2],
        out_specs=[pl.BlockSpec((tm, C), lambda i: (i, 0)), pl.BlockSpec((tm, C), lambda i: (i, 0)),
                   pl.BlockSpec((nw, POOL_GROUP, POOL_GROUP), lambda i: (0, 0, 0)), full2, full2],
        out_shape=[jax.ShapeDtypeStruct((T, C), F32), jax.ShapeDtypeStruct((T, C), F32),
                   jax.ShapeDtypeStruct((nw, POOL_GROUP, POOL_GROUP), F32),
                   jax.ShapeDtypeStruct((1, C), F32), jax.ShapeDtypeStruct((1, C), F32)],
        compiler_params=_params(("arbitrary",)),
    )(u_pad, d_ypn, pool_w_bf, pool_scale, g_pool)


def _pool_bwd_band(q_pad, dpooled, T, tm):
    C = dpooled.shape[1]

    def kern(q_ref, dp_ref, o_ref):
        r0 = pl.multiple_of(pl.program_id(0) * tm, tm)
        ext = q_ref[pl.ds(r0, tm + 2 * POOL_PAD), :]
        for gi, w in enumerate(POOL_WINDOWS):
            half = w // 2
            cols = slice(gi * POOL_GROUP, (gi + 1) * POOL_GROUP)
            o_ref[:, cols] = _band_sum(_band(tm, -half + 1, half + 1), ext[:, cols]) - dp_ref[:, cols]

    return pl.pallas_call(
        kern, name="pool_bwd_band", grid=(T // tm,),
        in_specs=[pl.BlockSpec(q_pad.shape, lambda i: (0, 0)), pl.BlockSpec((tm, C), lambda i: (i, 0))],
        out_specs=pl.BlockSpec((tm, C), lambda i: (i, 0)),
        out_shape=jax.ShapeDtypeStruct((T, C), F32),
        compiler_params=_params(("arbitrary",)),
    )(q_pad, dpooled)


def _cmul(ar, ai, br, bi):
    return ar * br - ai * bi, ar * bi + ai * br


def _ssm_discretise(log_neg_a_re, a_im, log_dt):
    dt = jnp.exp(log_dt)
    a_re = -jnp.exp(log_neg_a_re)
    mag = jnp.exp(a_re * dt)
    ang = a_im * dt
    lam_re, lam_im = mag * jnp.cos(ang), mag * jnp.sin(ang)
    den = a_re * a_re + a_im * a_im
    f_re = ((lam_re - 1.0) * a_re + lam_im * a_im) / den
    f_im = (lam_im * a_re - (lam_re - 1.0) * a_im) / den
    return lam_re, lam_im, f_re, f_im


def _ssm_params_fwd(log_neg_a_re, a_im, log_dt):
    rows, n = log_neg_a_re.shape

    def kern(a_ref, b_ref, c_ref, o1, o2, o3, o4):
        for o, v in zip((o1, o2, o3, o4), _ssm_discretise(a_ref[...], b_ref[...], c_ref[...])):
            o[...] = v

    return pl.pallas_call(kern, name="ssm_params_fwd", out_shape=[jax.ShapeDtypeStruct((rows, n), F32)] * 4)(
        log_neg_a_re, a_im, log_dt)


def _ssm_params_bwd(log_neg_a_re, a_im, log_dt, cots):
    rows, n = log_neg_a_re.shape

    def kern(a_ref, b_ref, c_ref, g1, g2, g3, g4, o1, o2, o3):
        _, vjp = jax.vjp(_ssm_discretise, a_ref[...], b_ref[...], c_ref[...])
        d1, d2, d3 = vjp((g1[...], g2[...], g3[...], g4[...]))
        o1[...] = d1
        o2[...] = d2
        o3[...] = d3

    return pl.pallas_call(
        kern, name="ssm_params_bwd",
        out_shape=[jax.ShapeDtypeStruct((rows, n), F32), jax.ShapeDtypeStruct((rows, n), F32),
                   jax.ShapeDtypeStruct((rows, 1), F32)])(log_neg_a_re, a_im, log_dt, *cots)


def _cpow(lr, li, n):
    out = None
    br, bi = lr, li
    while n:
        if n & 1:
            out = (br, bi) if out is None else _cmul(out[0], out[1], br, bi)
        n >>= 1
        if n:
            br, bi = _cmul(br, bi, br, bi)
    return out


def _slab(t):
    return pl.ds(pl.multiple_of(t * N_CHUNK, N_CHUNK), N_CHUNK)


def _shift_chunks(vr, vi, reverse):
    sub = lax.broadcasted_iota(jnp.int32, vr.shape, 0)
    if reverse:
        keep = sub != N_CHUNK - 1
        return jnp.where(keep, pltpu.roll(vr, N_CHUNK - 1, 0), 0.0), jnp.where(keep, pltpu.roll(vi, N_CHUNK - 1, 0), 0.0)
    keep = sub != 0
    return jnp.where(keep, pltpu.roll(vr, 1, 0), 0.0), jnp.where(keep, pltpu.roll(vi, 1, 0), 0.0)


def _chunk_scan(xr, xi, lr, li, Lc, reverse, prev=None):
    C = xr.shape[1]
    lrb, lib = jnp.broadcast_to(lr, (N_CHUNK, C)), jnp.broadcast_to(li, (N_CHUNK, C))
    zero = jnp.zeros((N_CHUNK, C), F32)

    def step_of(k):
        return (Lc - 1 - k) if reverse else k

    def advance(sr, si, t):
        return lrb * sr - lib * si + xr[_slab(t), :], lrb * si + lib * sr + xi[_slab(t), :]

    def pass1(k, c):
        return advance(c[0], c[1], step_of(k))

    er, ei = lax.fori_loop(0, Lc, pass1, (zero, zero), unroll=4)
    pr, pi = _cpow(lr, li, Lc)
    prb, pib = jnp.broadcast_to(pr, (N_CHUNK, C)), jnp.broadcast_to(pi, (N_CHUNK, C))
    cr, ci = zero, zero
    for _ in range(N_CHUNK - 1):
        cr, ci = _shift_chunks(er + prb * cr - pib * ci, ei + prb * ci + pib * cr, reverse)

    if prev is None:
        def pass2(k, c):
            t = step_of(k)
            nr, ni = advance(c[0], c[1], t)
            xr[_slab(t), :] = nr
            xi[_slab(t), :] = ni
            return nr, ni

        lax.fori_loop(0, Lc, pass2, (cr, ci), unroll=4)
        return None

    qr, qi = prev
    def pair(ar, ai, wr, wi, acc):
        return acc[0] + ar * wr + ai * wi, acc[1] + ai * wr - ar * wi

    def pass2(k, c):
        sr, si, accr, acci = c
        t = step_of(k)
        nr, ni = advance(sr, si, t)
        xr[_slab(t), :] = nr
        xi[_slab(t), :] = ni
        tp = t - 1 if reverse else t + 1
        accr, acci = pair(nr, ni, qr[_slab(tp), :], qi[_slab(tp), :], (accr, acci))
        return nr, ni, accr, acci

    sr, si, accr, acci = lax.fori_loop(0, Lc - 1, pass2, (cr, ci, zero, zero), unroll=4)
    t = step_of(Lc - 1)
    nr, ni = advance(sr, si, t)
    xr[_slab(t), :] = nr
    xi[_slab(t), :] = ni
    edge = step_of(0)
    wr, wi = _shift_chunks(qr[_slab(edge), :], qi[_slab(edge), :], not reverse)
    return pair(nr, ni, wr, wi, (accr, acci))


def _ssm_dims(T):
    assert T % (N_CHUNK * 64) == 0
    rc = _pick(T, (512, 256, 128, 64))
    return T // N_CHUNK, rc, T // rc


def _ssm_project_in(u_ref, bre_ref, bim_ref, xr, xi, rc, nrc):
    for c in range(nrc):
        rows = pl.ds(c * rc, rc)
        ub = u_ref[rows, :]
        xr[rows, :] = jnp.dot(ub, bre_ref[...], preferred_element_type=F32)
        xi[rows, :] = jnp.dot(ub, bim_ref[...], preferred_element_type=F32)


def _ssm_fwd(name, u_bf, mats, vecs, reverse):
    T, W = u_bf.shape
    nb = W // LANES
    SC = SSM_BLOCK_GROUPS * SSM_STATE
    Lc, rc, nrc = _ssm_dims(T)
    bre, bim, cre, cim = mats
    lam_re, lam_im, f_re, f_im = vecs

    def kern(u_ref, bre_ref, bim_ref, cre_ref, cim_ref, lr_ref, li_ref, fr_ref, fi_ref, y_ref, xr, xi):
        _ssm_project_in(u_ref, bre_ref, bim_ref, xr, xi, rc, nrc)
        _chunk_scan(xr, xi, lr_ref[...], li_ref[...], Lc, reverse)
        fr, fi = fr_ref[...], fi_ref[...]
        for c in range(nrc):
            rows = pl.ds(c * rc, rc)
            sr, si = _cmul(fr, fi, xr[rows, :], xi[rows, :])
            y_ref[rows, :] = (jnp.dot(sr.astype(BF16), cre_ref[...], preferred_element_type=F32)
                              - jnp.dot(si.astype(BF16), cim_ref[...], preferred_element_type=F32))

    col = pl.BlockSpec((T, LANES), lambda m: (0, m))
    mat_in = pl.BlockSpec((None, LANES, SC), lambda m: (m, 0, 0))
    mat_out = pl.BlockSpec((None, SC, LANES), lambda m: (m, 0, 0))
    vec = pl.BlockSpec((None, 1, SC), lambda m: (m, 0, 0))
    return pl.pallas_call(
        kern, name=name, grid=(nb,),
        in_specs=[col, mat_in, mat_in, mat_out, mat_out, vec, vec, vec, vec],
        out_specs=col, out_shape=jax.ShapeDtypeStruct((T, W), F32),
        scratch_shapes=[pltpu.VMEM((T, SC), F32), pltpu.VMEM((T, SC), F32)],
        compiler_params=_params(("arbitrary",)),
    )(u_bf, bre, bim, cre, cim, lam_re, lam_im, f_re, f_im)


def _ssm_bwd(name, u_bf, dy_bf, du_in, mats, mats_t, vecs, reverse):
    T, W = u_bf.shape
    nb = W // LANES
    SC = SSM_BLOCK_GROUPS * SSM_STATE
    Lc, rc, nrc = _ssm_dims(T)
    bre, bim = mats
    bre_t, bim_t, cre_t, cim_t = mats_t
    lam_re, lam_im, f_re, f_im = vecs
    contract_rows = (((0,), (0,)), ((), ()))

    def kern(u_ref, dy_ref, dui_ref, bre_ref, bim_ref, bret_ref, bimt_ref, cret_ref, cimt_ref,
             lr_ref, li_ref, fr_ref, fi_ref,
             du_ref, dbre_ref, dbim_ref, dcre_ref, dcim_ref, dlr_ref, dli_ref, dfr_ref, dfi_ref,
             xr, xi, gr, gi):
        lr, li, fr, fi = lr_ref[...], li_ref[...], fr_ref[...], fi_ref[...]
        _ssm_project_in(u_ref, bre_ref, bim_ref, xr, xi, rc, nrc)
        _chunk_scan(xr, xi, lr, li, Lc, reverse)
        mre = jnp.zeros((LANES, SC), F32)
        mim = jnp.zeros((LANES, SC), F32)
        dfr = jnp.zeros((1, SC), F32)
        dfi = jnp.zeros((1, SC), F32)
        for c in range(nrc):
            rows = pl.ds(c * rc, rc)
            dy = dy_ref[rows, :]
            x_r, x_i = xr[rows, :], xi[rows, :]
            g_r = jnp.dot(dy, cret_ref[...], preferred_element_type=F32)
            g_i = -jnp.dot(dy, cimt_ref[...], preferred_element_type=F32)
            mre += lax.dot_general(dy, x_r.astype(BF16), contract_rows, preferred_element_type=F32)
            mim += lax.dot_general(dy, x_i.astype(BF16), contract_rows, preferred_element_type=F32)
            dfr += _colsum(g_r * x_r + g_i * x_i)
            dfi += _colsum(g_i * x_r - g_r * x_i)
            gr[rows, :] = fr * g_r + fi * g_i
            gi[rows, :] = fr * g_i - fi * g_r
        dcre_ref[...] = fr * mre - fi * mim
        dcim_ref[...] = -(fr * mim + fi * mre)
        dfr_ref[...] = dfr
        dfi_ref[...] = dfi
        accr, acci = _chunk_scan(gr, gi, lr, -li, Lc, not reverse, prev=(xr, xi))
        dlr_ref[...] = _colsum(accr)
        dli_ref[...] = _colsum(acci)
        dbre = jnp.zeros((LANES, SC), F32)
        dbim = jnp.zeros((LANES, SC), F32)
        for c in range(nrc):
            rows = pl.ds(c * rc, rc)
            ub = u_ref[rows, :]
            a_r, a_i = gr[rows, :].astype(BF16), gi[rows, :].astype(BF16)
            dbre += lax.dot_general(ub, a_r, contract_rows, preferred_element_type=F32)
            dbim += lax.dot_general(ub, a_i, contract_rows, preferred_element_type=F32)
            du_ref[rows, :] = (dui_ref[rows, :] + jnp.dot(a_r, bret_ref[...], preferred_element_type=F32)
                               + jnp.dot(a_i, bimt_ref[...], preferred_element_type=F32))
        dbre_ref[...] = dbre
        dbim_ref[...] = dbim

    col = pl.BlockSpec((T, LANES), lambda m: (0, m))
    mat_in = pl.BlockSpec((None, LANES, SC), lambda m: (m, 0, 0))
    mat_out = pl.BlockSpec((None, SC, LANES), lambda m: (m, 0, 0))
    vec = pl.BlockSpec((None, 1, SC), lambda m: (m, 0, 0))
    mat_shape = jax.ShapeDtypeStruct((nb, LANES, SC), F32)
    vec_shape = jax.ShapeDtypeStruct((nb, 1, SC), F32)
    return pl.pallas_call(
        kern, name=name, grid=(nb,),
        in_specs=[col, col, col, mat_in, mat_in, mat_out, mat_out, mat_in, mat_in, vec, vec, vec, vec],
        out_specs=[col, mat_in, mat_in, mat_in, mat_in, vec, vec, vec, vec],
        out_shape=[jax.ShapeDtypeStruct((T, W), F32), mat_shape, mat_shape, mat_shape, mat_shape,
                   vec_shape, vec_shape, vec_shape, vec_shape],
        scratch_shapes=[pltpu.VMEM((T, SC), F32)] * 4,
        compiler_params=_params(("arbitrary",)),
    )(u_bf, dy_bf, du_in, bre, bim, bre_t, bim_t, cre_t, cim_t, lam_re, lam_im, f_re, f_im)


def _block_diag_in(b):
    d, G, N, H = b.shape
    nb = G // SSM_BLOCK_GROUPS
    eye = jnp.eye(SSM_BLOCK_GROUPS, dtype=b.dtype)
    v = b.reshape(d, nb, SSM_BLOCK_GROUPS, N, H)
    return jnp.einsum('dmgnh,gk->dmghkn', v, eye).reshape(d, nb, SSM_BLOCK_GROUPS * H, SSM_BLOCK_GROUPS * N)


def _block_diag_in_grad(m, N, H):
    d, nb = m.shape[:2]
    eye = jnp.eye(SSM_BLOCK_GROUPS, dtype=m.dtype)
    v = m.reshape(d, nb, SSM_BLOCK_GROUPS, H, SSM_BLOCK_GROUPS, N)
    return jnp.einsum('dmghkn,gk->dmgnh', v, eye).reshape(d, nb * SSM_BLOCK_GROUPS, N, H)


def _block_diag_out(c):
    return jnp.swapaxes(_block_diag_in(jnp.swapaxes(c, 2, 3)), 2, 3)


def _block_diag_out_grad(m, N, H):
    return jnp.swapaxes(_block_diag_in_grad(m, N, H), 2, 3)


def _to_chunk_rows(a):
    T, C = a.shape
    return a.reshape(N_CHUNK, T // N_CHUNK, C).transpose(1, 0, 2).reshape(T, C)


def _from_chunk_rows(a):
    T, C = a.shape
    return a.reshape(T // N_CHUNK, N_CHUNK, C).transpose(1, 0, 2).reshape(T, C)


def _conv_taps(ref, c, R, T):
    n = T // R
    r0 = pl.multiple_of(c * R, R)
    main = ref[pl.ds(r0, R), :]
    before = ref[pl.ds(pl.multiple_of(jnp.maximum(r0 - 8, 0), 8), 8), :][7:8, :]
    after = ref[pl.ds(pl.multiple_of(jnp.minimum(r0 + R, T - 8), 8), 8), :][0:1, :]
    before = jnp.where(c > 0, before, 0.0)
    after = jnp.where(c < n - 1, after, 0.0)
    rid = lax.broadcasted_iota(jnp.int32, main.shape, 0)
    m1 = jnp.where(rid == 0, before, pltpu.roll(main, 1, 0))
    p1 = jnp.where(rid == R - 1, after, pltpu.roll(main, R - 1, 0))
    return m1, main, p1, r0


def _conv_specs(T, F, cb):
    nj = F // cb
    lo = lambda rows: pl.BlockSpec((rows, cb), lambda j: (0, j))
    hi = lambda rows: pl.BlockSpec((rows, cb), lambda j: (0, j + nj))
    return nj, lo, hi


def _convact_fwd(up, conv_w, conv_b, F):
    T = up.shape[0]
    cb = _pick(F, (256, 128))
    R = _pick(T, (256, 128, 64))
    nj, lo, hi = _conv_specs(T, F, cb)

    def kern(uv, ug, wv, wg, bv, bg, o_ref):
        def body(c, carry):
            v1, v0, v2, r0 = _conv_taps(uv, c, R, T)
            g1, g0, g2, _ = _conv_taps(ug, c, R, T)
            val = v1 * wv[0:1, :] + v0 * wv[1:2, :] + v2 * wv[2:3, :] + bv[...]
            gate = g1 * wg[0:1, :] + g0 * wg[1:2, :] + g2 * wg[2:3, :] + bg[...]
            o_ref[pl.ds(r0, R), :] = (val * (gate * jax.nn.sigmoid(gate))).astype(o_ref.dtype)
            return carry

        lax.fori_loop(0, T // R, body, 0)

    return pl.pallas_call(
        kern, name="convact_fwd", grid=(nj,),
        in_specs=[lo(T), hi(T), lo(3), hi(3), lo(1), hi(1)],
        out_specs=lo(T), out_shape=jax.ShapeDtypeStruct((T, F), BF16),
        compiler_params=_params(("arbitrary",)),
    )(up, up, conv_w, conv_w, conv_b, conv_b)


def _convact_bwd(up, dact, conv_w, conv_b, F):
    T = up.shape[0]
    cb = _pick(F, (256, 128))
    R = _pick(T, (256, 128, 64))
    nj, lo, hi = _conv_specs(T, F, cb)

    def kern(uv, ug, da, wv, wg, bv, bg, dupv, dupg, dwv, dwg, dbv, dbg, sv, sg):
        zero = jnp.zeros((1, cb), F32)

        def pass_a(c, acc):
            v1, v0, v2, r0 = _conv_taps(uv, c, R, T)
            g1, g0, g2, _ = _conv_taps(ug, c, R, T)
            val = v1 * wv[0:1, :] + v0 * wv[1:2, :] + v2 * wv[2:3, :] + bv[...]
            gate = g1 * wg[0:1, :] + g0 * wg[1:2, :] + g2 * wg[2:3, :] + bg[...]
            sig = jax.nn.sigmoid(gate)
            d = da[pl.ds(r0, R), :]
            dval = d * (gate * sig)
            dgate = d * val * (sig * (1.0 + gate * (1.0 - sig)))
            sv[pl.ds(r0, R), :] = dval
            sg[pl.ds(r0, R), :] = dgate
            terms = (dval * v1, dval * v0, dval * v2, dval, dgate * g1, dgate * g0, dgate * g2, dgate)
            return tuple(a + _colsum(t) for a, t in zip(acc, terms))

        acc = lax.fori_loop(0, T // R, pass_a, (zero,) * 8)
        for k in range(3):
            dwv[k:k + 1, :] = acc[k]
            dwg[k:k + 1, :] = acc[4 + k]
        dbv[...] = acc[3]
        dbg[...] = acc[7]

        def pass_b(c, carry):
            v1, v0, v2, r0 = _conv_taps(sv, c, R, T)
            g1, g0, g2, _ = _conv_taps(sg, c, R, T)
            dupv[pl.ds(r0, R), :] = (v2 * wv[0:1, :] + v0 * wv[1:2, :] + v1 * wv[2:3, :]).astype(dupv.dtype)
            dupg[pl.ds(r0, R), :] = (g2 * wg[0:1, :] + g0 * wg[1:2, :] + g1 * wg[2:3, :]).astype(dupg.dtype)
            return carry

        lax.fori_loop(0, T // R, pass_b, 0)

    dupv, dupg, dwv, dwg, dbv, dbg = pl.pallas_call(
        kern, name="convact_bwd", grid=(nj,),
        in_specs=[lo(T), hi(T), lo(T), lo(3), hi(3), lo(1), hi(1)],
        out_specs=[lo(T), lo(T), lo(3), lo(3), lo(1), lo(1)],
        out_shape=[jax.ShapeDtypeStruct((T, F), BF16), jax.ShapeDtypeStruct((T, F), BF16),
                   jax.ShapeDtypeStruct((3, F), F32), jax.ShapeDtypeStruct((3, F), F32),
                   jax.ShapeDtypeStruct((1, F), F32), jax.ShapeDtypeStruct((1, F), F32)],
        scratch_shapes=[pltpu.VMEM((T, cb), F32), pltpu.VMEM((T, cb), F32)],
        compiler_params=_params(("arbitrary",)),
    )(up, up, dact, conv_w, conv_w, conv_b, conv_b)
    return (jnp.concatenate([dupv, dupg], axis=1), jnp.concatenate([dwv, dwg], axis=1),
            jnp.concatenate([dbv, dbg], axis=1))


def _exchange(name, bufs, modes):
    n = len(bufs)

    def kern(*refs):
        ins, outs = refs[:n], refs[n:2 * n]
        send_sems, recv_sems, local_sems = refs[2 * n:]
        x, y, c = lax.axis_index("x"), lax.axis_index("y"), lax.axis_index("c")
        me = 4 * x + 2 * y + c
        started = []
        for b in range(n):
            gather = modes[b] == 'gather'
            own = pltpu.make_async_copy(ins[b] if gather else ins[b].at[me], outs[b].at[me], local_sems.at[b])
            own.start()
            started.append(own)
            for k in range(1, N_DEV):
                px = 1 - x if k & 4 else x
                py = 1 - y if k & 2 else y
                pc = 1 - c if k & 1 else c
                peer = 4 * px + 2 * py + pc
                sem = b * (N_DEV - 1) + k - 1
                pltpu.make_async_remote_copy(
                    src_ref=ins[b] if gather else ins[b].at[peer], dst_ref=outs[b].at[me],
                    send_sem=send_sems.at[sem], recv_sem=recv_sems.at[sem],
                    device_id=(px, py, pc), device_id_type=MESH_ID).start()
        for b in range(n):
            gather = modes[b] == 'gather'
            for k in range(1, N_DEV):
                px = 1 - x if k & 4 else x
                py = 1 - y if k & 2 else y
                pc = 1 - c if k & 1 else c
                peer = 4 * px + 2 * py + pc
                sem = b * (N_DEV - 1) + k - 1
                arrived = pltpu.make_async_remote_copy(
                    src_ref=ins[b] if gather else ins[b].at[peer], dst_ref=outs[b].at[peer],
                    send_sem=send_sems.at[sem], recv_sem=recv_sems.at[sem],
                    device_id=(px, py, pc), device_id_type=MESH_ID)
                arrived.wait_recv()
                arrived.wait_send()
        for own in started:
            own.wait()

    hbm = pl.BlockSpec(memory_space=pltpu.HBM)
    out_shape = [jax.ShapeDtypeStruct((N_DEV,) + tuple(b.shape[-2:]), b.dtype) for b in bufs]
    return pl.pallas_call(
        kern, name=name, in_specs=[hbm] * n, out_specs=[hbm] * n, out_shape=out_shape,
        scratch_shapes=[pltpu.SemaphoreType.DMA((n * (N_DEV - 1),)), pltpu.SemaphoreType.DMA((n * (N_DEV - 1),)),
                        pltpu.SemaphoreType.DMA((n,))],
    )(*bufs)


def _peers(x, y, c):
    out = []
    for k in range(1, N_DEV):
        px = 1 - x if k & 4 else x
        py = 1 - y if k & 2 else y
        pc = 1 - c if k & 1 else c
        out.append((k, (px, py, pc), 4 * px + 2 * py + pc))
    return out


def _exchange_start(name, bufs, modes, after):
    n = len(bufs)
    ncopy = n * (N_DEV - 1)
    lands = [lax.empty((N_DEV,) + tuple(b.shape[-2:]), b.dtype) for b in bufs]

    def body(*refs):
        ins, land_in = refs[:n], refs[n:2 * n]
        send_sems, recv_sems = refs[2 * n + 1], refs[2 * n + 2]
        token = refs[-1]
        x, y, c = lax.axis_index("x"), lax.axis_index("y"), lax.axis_index("c")
        me = 4 * x + 2 * y + c
        for b in range(n):
            for k, peer, slot in _peers(x, y, c):
                sem = b * (N_DEV - 1) + k - 1
                pltpu.make_async_remote_copy(
                    src_ref=ins[b] if modes[b] == 'gather' else ins[b].at[slot], dst_ref=land_in[b].at[me],
                    send_sem=send_sems.at[sem], recv_sem=recv_sems.at[sem],
                    device_id=peer, device_id_type=MESH_ID).start()
        token[...] = jnp.zeros_like(token)

    hbm = pl.BlockSpec(memory_space=pltpu.HBM)
    sem_spec = pl.BlockSpec(memory_space=pltpu.SEMAPHORE)
    operands = [pltpu.with_memory_space_constraint(a, pltpu.HBM) for a in list(bufs) + lands]
    res = pl.pallas_call(
        body, name=name,
        out_shape=(pltpu.SemaphoreType.DMA((ncopy,)), pltpu.SemaphoreType.DMA((ncopy,)),
                   *[pltpu.HBM(a.shape, a.dtype) for a in operands], jax.ShapeDtypeStruct((8, LANES), F32)),
        in_specs=[hbm] * (2 * n) + [pl.BlockSpec(memory_space=pl.ANY)],
        out_specs=(sem_spec, sem_spec, *([hbm] * (2 * n)), pl.BlockSpec(memory_space=pltpu.VMEM)),
        input_output_aliases={i: 2 + i for i in range(2 * n)},
        compiler_params=pltpu.CompilerParams(has_side_effects=pltpu.SideEffectType.DATAFLOW_SIDE_EFFECTING),
    )(*operands, after)
    return (res[0], res[1], list(res[2:2 + n]), list(res[2 + n:2 + 2 * n]), tuple(modes)), res[-1]


def _exchange_wait(name, handle, after):
    send_sems, recv_sems, srcs, lands, modes = handle
    n = len(srcs)

    def body(*refs):
        src_in, land_in = refs[:n], refs[n:2 * n]
        send_ref, recv_ref = refs[2 * n], refs[2 * n + 1]
        x, y, c = lax.axis_index("x"), lax.axis_index("y"), lax.axis_index("c")
        for b in range(n):
            for k, peer, slot in _peers(x, y, c):
                sem = b * (N_DEV - 1) + k - 1
                copy = pltpu.make_async_remote_copy(
                    src_ref=src_in[b] if modes[b] == 'gather' else src_in[b].at[slot], dst_ref=land_in[b].at[slot],
                    send_sem=send_ref.at[sem], recv_sem=recv_ref.at[sem],
                    device_id=peer, device_id_type=MESH_ID)
                copy.wait_send()
                copy.wait_recv()

    hbm = pl.BlockSpec(memory_space=pltpu.HBM)
    sem_spec = pl.BlockSpec(memory_space=pltpu.SEMAPHORE)
    res = pl.pallas_call(
        body, name=name,
        out_shape=tuple(pltpu.HBM(a.shape, a.dtype) for a in srcs + lands),
        in_specs=[hbm] * (2 * n) + [sem_spec, sem_spec, pl.BlockSpec(memory_space=pl.ANY)],
        out_specs=tuple([hbm] * (2 * n)),
        input_output_aliases={i: i for i in range(2 * n)},
        compiler_params=pltpu.CompilerParams(has_side_effects=pltpu.SideEffectType.DATAFLOW_SIDE_EFFECTING),
    )(*srcs, *lands, send_sems, recv_sems, after)
    return list(res[n:])


def _with_own(landed, own, me):
    return lax.dynamic_update_slice(landed, own[None], (me,) + (0,) * own.ndim)


def _adamw(w, g, m, v):
    m2 = ADAM_B1 * m + (1.0 - ADAM_B1) * g
    v2 = ADAM_B2 * v + (1.0 - ADAM_B2) * (g * g)
    m_hat = m2 / (1.0 - ADAM_B1 ** ADAM_STEP)
    v_hat = v2 / (1.0 - ADAM_B2 ** ADAM_STEP)
    return -ADAM_LR * (m_hat / (jnp.sqrt(v_hat) + ADAM_EPS) + ADAM_WD * w), m2, v2


def _sum_adamw(name, landed, w, m, v):
    R, C = w.shape
    row_bytes = N_DEV * (-(-C // LANES) * LANES) * landed.dtype.itemsize
    tiles = [d for d in range(16, R, 16) if R % d == 0 and d * row_bytes <= ADAM_TILE_BYTES]
    tr = max(tiles) if tiles and R * row_bytes > ADAM_TILE_BYTES else R

    def kern(x_ref, w_ref, m_ref, v_ref, g_out, d_out, m_out, v_out):
        g = x_ref[0].astype(F32)
        for q in range(1, N_DEV):
            g = g + x_ref[q].astype(F32)
        g_out[...] = g
        d_out[...], m_out[...], v_out[...] = _adamw(w_ref[...], g, m_ref[...], v_ref[...])

    blk = pl.BlockSpec((tr, C), lambda i: (i, 0))
    return pl.pallas_call(
        kern, name=name, grid=(R // tr,),
        in_specs=[pl.BlockSpec((N_DEV, tr, C), lambda i: (0, i, 0)), blk, blk, blk],
        out_specs=[blk] * 4, out_shape=[jax.ShapeDtypeStruct((R, C), F32)] * 4,
        compiler_params=_params(("arbitrary",)),
    )(landed, w, m, v)


def _view2d(a):
    if a.ndim == 1:
        return a.reshape(1, -1)
    return a.reshape(-1, a.shape[-1])


def kernel(x, norm_mix_g, w_in, pool_w, pool_scale, ssm_log_neg_a_re, ssm_a_im, ssm_log_dt, ssm_b_re, ssm_b_im, ssm_c_re, ssm_c_im, ssm_d, glu_w, glu_b, out_norm_pool_g, out_norm_ssm_g, w_out, norm_ffn_g, w_up, conv_w, conv_b, w_down, final_norm_g, loss_target, m_norm_mix_g, m_w_in, m_pool_w, m_pool_scale, m_ssm_log_neg_a_re, m_ssm_a_im, m_ssm_log_dt, m_ssm_b_re, m_ssm_b_im, m_ssm_c_re, m_ssm_c_im, m_ssm_d, m_glu_w, m_glu_b, m_out_norm_pool_g, m_out_norm_ssm_g, m_w_out, m_norm_ffn_g, m_w_up, m_conv_w, m_conv_b, m_w_down, m_final_norm_g, v_norm_mix_g, v_w_in, v_pool_w, v_pool_scale, v_ssm_log_neg_a_re, v_ssm_a_im, v_ssm_log_dt, v_ssm_b_re, v_ssm_b_im, v_ssm_c_re, v_ssm_c_im, v_ssm_d, v_glu_w, v_glu_b, v_out_norm_pool_g, v_out_norm_ssm_g, v_w_out, v_norm_ffn_g, v_w_up, v_conv_w, v_conv_b, v_w_down, v_final_norm_g):
    given = dict(locals())
    weights = {n: given[n] for n in WEIGHTS}
    mom1 = {n: given["m_" + n] for n in WEIGHTS}
    mom2 = {n: given["v_" + n] for n in WEIGHTS}

    xs = x[0]
    tgt = loss_target[0]
    T, D = xs.shape
    DP = len(POOL_WINDOWS) * POOL_GROUP
    DS = D - DP
    G = DS // SSM_GROUP
    N, H = SSM_STATE, SSM_GROUP
    F2 = w_up.shape[1] * N_DEV
    F = F2 // 2
    fs = w_up.shape[1]
    row = lambda a: a.reshape(1, -1)

    me = 4 * lax.axis_index("x") + 2 * lax.axis_index("y") + lax.axis_index("c")
    early = _exchange("gather_early", [w_in.astype(BF16), glu_w.astype(BF16)], ['gather'] * 2)
    W_in = early[0].reshape(D, D)
    W_glu = early[1].reshape(DS, DS)
    late_src = [w_out.astype(BF16), w_down.astype(BF16), w_up.astype(BF16), conv_w]
    late_handle, token = _exchange_start("gather_late_start", late_src, ['gather'] * 4, after=early[0])
    CB = row(conv_b)

    g1, g2, g3 = row(norm_mix_g), row(norm_ffn_g), row(final_norm_g)
    gp, gs = row(out_norm_pool_g), row(out_norm_ssm_g)
    (xn,) = _rowmap("norm_mix", _rms_fwd, [xs], [g1 + token[0:1, 0:1]], [(D, BF16)])
    u = _mm("proj_in", xn, W_in, 'nn', F32)
    tmp = _pick(T, (256, 128))
    u_pad = jnp.pad(u[:, :DP], ((POOL_PAD, POOL_PAD), (0, 0)))
    pool_w_bf = pool_w.astype(BF16)
    ypn = _pool_fwd(u_pad, pool_w_bf, row(pool_scale), gp, T, tmp)

    u_ssm = _to_chunk_rows(u[:, DP:])
    u_ssm_bf = u_ssm.astype(BF16)
    a_rows = (ssm_log_neg_a_re.reshape(2 * G, N), ssm_a_im.reshape(2 * G, N), ssm_log_dt.reshape(2 * G, 1))
    disc = _ssm_params_fwd(*a_rows)
    nb = G // SSM_BLOCK_GROUPS
    vecs = [d.reshape(2, nb, 1, SSM_BLOCK_GROUPS * N) for d in disc]
    b_in = [_block_diag_in(b) for b in (ssm_b_re, ssm_b_im)]
    c_out = [_block_diag_out(c) for c in (ssm_c_re, ssm_c_im)]
    mats_bf = [[m[d].astype(BF16) for m in b_in + c_out] for d in range(2)]
    mats_t_bf = [[jnp.swapaxes(m[d], 1, 2).astype(BF16) for m in b_in + c_out] for d in range(2)]
    dir_vecs = [[v[d] for v in vecs] for d in range(2)]
    y_dir = [_ssm_fwd("ssm_fwd_%d" % d, u_ssm_bf, mats_bf[d], dir_vecs[d], reverse=bool(d)) for d in range(2)]

    def mix_post(yf, yb, us, d, gw, gb, g):
        y = yf + yb + d * us
        z = _gelu(y)
        gate = jax.nn.sigmoid(jnp.dot(z.astype(BF16), gw, preferred_element_type=F32) + gb)
        return _rms_fwd(z * gate, g), y

    ysn, y_ssm = _rowmap("ssm_post", mix_post, [y_dir[0], y_dir[1], u_ssm], [row(ssm_d), W_glu, row(glu_b), gs],
                         [(DS, BF16), (DS, F32)])
    ycat = jnp.concatenate([ypn, _from_chunk_rows(ysn)], axis=1)
    late = [_with_own(l, s, me) for l, s in zip(_exchange_wait("gather_late_wait", late_handle, ysn), late_src)]
    W_out = late[0].reshape(D, D)
    W_down = late[1].reshape(F, D)
    W_up = late[2].transpose(1, 0, 2).reshape(D, F2)
    CW = late[3].transpose(1, 0, 2).reshape(3, F2)
    h1 = _mm("proj_out", ycat, W_out, 'nn', F32, add=xs)
    (hn,) = _rowmap("norm_ffn", _rms_fwd, [h1], [g2], [(D, BF16)])
    up = _mm("ffn_up", hn, W_up, 'nn', F32)
    act = _convact_fwd(up, CW, CB, F)
    h2 = _mm("ffn_down", act, W_down, 'nn', F32, add=h1)

    def head(h, t, g):
        r = lax.rsqrt(_rowmean(h * h) + EPS)
        hh = h * r
        e = hh * g - t
        loss = 0.5 * jnp.sum(_rowmean(e * e), keepdims=True)
        dy = e * (1.0 / D)
        dxh = dy * g
        dh = r * (dxh - hh * _rowmean(dxh * hh))
        return dh, dh, jnp.broadcast_to(loss, (1, LANES)), _colsum(dy * hh)

    dh2, dh2_bf, loss_acc, dg3 = _rowmap("loss_head", head, [h2, tgt], [g3], [(D, F32), (D, BF16)], [(1, LANES), (1, D)])

    dact = _mm("ffn_down_dx", dh2_bf, W_down, 'nt', F32)
    dW_down = _mm("ffn_down_dw", act, dh2_bf, 'tn', BF16)
    dup, dCW, dCB = _convact_bwd(up, dact, CW, CB, F)
    dhn = _mm("ffn_up_dx", dup, W_up, 'nt', F32)
    dW_up = _mm("ffn_up_dw", hn, dup, 'tn', BF16)

    def shards(full_grad):
        return full_grad.reshape((N_DEV, full_grad.shape[0] // N_DEV) + full_grad.shape[1:])

    ffn_sent = [shards(dW_down), dW_up.reshape(D, N_DEV, fs).transpose(1, 0, 2), dCW.reshape(3, N_DEV, fs).transpose(1, 0, 2)]
    ffn_handle, token = _exchange_start("scatter_ffn_start", ffn_sent, ['scatter'] * 3, after=dhn)
    g2_late = g2 + token[0:1, 0:1]

    def norm_bwd_add(hx, dy, res, g):
        dx, dg = _rms_bwd(hx, g, dy)
        return dx + res, dg

    def norm_bwd_add2(hx, dy, res, g):
        dx, dg = _rms_bwd(hx, g, dy)
        return dx + res, dx + res, dg

    dh1, dh1_bf, dg2 = _rowmap("norm_ffn_bwd", norm_bwd_add2, [h1, dhn, dh2], [g2_late], [(D, F32), (D, BF16)], [(1, D)])
    dycat = _mm("proj_out_dx", dh1_bf, W_out, 'nt', F32)
    dW_out = _mm("proj_out_dw", ycat, dh1_bf, 'tn', BF16)

    out_sent = [shards(dW_out)]
    out_handle, token = _exchange_start("scatter_out_start", out_sent, ['scatter'], after=dycat)
    q, dpooled, dpool_w, dpool_scale, dgp = _pool_bwd_rows(u_pad, dycat[:, :DP], pool_w_bf, row(pool_scale),
                                                           gp + token[0:1, 0:1], T, tmp)
    du_pool = _pool_bwd_band(jnp.pad(q, ((POOL_PAD, POOL_PAD), (0, 0))), dpooled, T, tmp)

    def mix_post_bwd(y, us, dyn, d, gw, gb, g):
        z = _gelu(y)
        gz = z.astype(BF16)
        sig = jax.nn.sigmoid(jnp.dot(gz, gw, preferred_element_type=F32) + gb)
        dys, dg = _rms_bwd(z * sig, g, dyn)
        dgl = dys * z * sig * (1.0 - sig)
        dgl_bf = dgl.astype(BF16)
        dz = dys * sig + lax.dot_general(dgl_bf, gw, (((1,), (1,)), ((), ())), preferred_element_type=F32)
        dgw = lax.dot_general(gz, dgl_bf, (((0,), (0,)), ((), ())), preferred_element_type=F32)
        dyv = dz * _gelu_grad(y)
        return dyv, dyv * d, dgw, _colsum(dgl), _colsum(dyv * us), dg

    d_ysn = _to_chunk_rows(dycat[:, DP:])
    dyv, du_dir, dglu_w, dglu_b, dssm_d, dgs = _rowmap(
        "ssm_post_bwd", mix_post_bwd, [y_ssm, u_ssm, d_ysn], [row(ssm_d), W_glu, row(glu_b), gs],
        [(DS, BF16), (DS, F32)], [(DS, DS), (1, DS), (1, DS), (1, DS)])

    du_ssm = du_dir
    dB, dC, dvec = [], [], []
    for d in range(2):
        b_re, b_im, c_re, c_im = mats_bf[d]
        b_re_t, b_im_t, c_re_t, c_im_t = mats_t_bf[d]
        res = _ssm_bwd("ssm_bwd_%d" % d, u_ssm_bf, dyv, du_ssm, (b_re, b_im), (b_re_t, b_im_t, c_re_t, c_im_t),
                       dir_vecs[d], reverse=bool(d))
        du_ssm = res[0]
        dB.append(res[1:3])
        dC.append(res[3:5])
        dvec.append(res[5:9])
    grad_b = [_block_diag_in_grad(jnp.stack([dB[0][k], dB[1][k]]), N, H) for k in range(2)]
    grad_c = [_block_diag_out_grad(jnp.stack([dC[0][k], dC[1][k]]), N, H) for k in range(2)]
    cots = [jnp.stack([dvec[0][k], dvec[1][k]]).reshape(2 * G, N) for k in range(4)]
    d_a_re, d_a_im, d_log_dt = _ssm_params_bwd(*a_rows, cots)

    du = jnp.concatenate([du_pool, _from_chunk_rows(du_ssm)], axis=1).astype(BF16)
    dxn = _mm("proj_in_dx", du, W_in, 'nt', F32)
    dW_in = _mm("proj_in_dw", xn, du, 'tn', BF16)
    dx, dg1 = _rowmap("norm_mix_bwd", norm_bwd_add, [xs, dxn, dh1], [g1], [(D, F32)], [(1, D)])

    grads, delta, new_m, new_v = {}, {}, {}, {}

    def update(n, landed_n):
        shape = weights[n].shape
        landed_n = landed_n.reshape((N_DEV,) + _view2d(weights[n]).shape)
        res = _sum_adamw("adamw_" + n, landed_n, _view2d(weights[n]), _view2d(mom1[n]), _view2d(mom2[n]))
        grads[n], delta[n], new_m[n], new_v[n] = [r.reshape(shape) for r in res]

    ffn_landed = _exchange_wait("scatter_ffn_wait", ffn_handle, dx)
    for n, l, s in zip(('w_down', 'w_up', 'conv_w'), ffn_landed, ffn_sent):
        update(n, _with_own(l, lax.dynamic_index_in_dim(s, me, 0, keepdims=False), me))

    rep_grads = {
        'norm_mix_g': dg1, 'pool_w': dpool_w, 'pool_scale': dpool_scale, 'ssm_log_neg_a_re': d_a_re, 'ssm_a_im': d_a_im,
        'ssm_log_dt': d_log_dt, 'ssm_b_re': grad_b[0], 'ssm_b_im': grad_b[1], 'ssm_c_re': grad_c[0], 'ssm_c_im': grad_c[1],
        'ssm_d': dssm_d, 'glu_b': dglu_b, 'out_norm_pool_g': dgp, 'out_norm_ssm_g': dgs, 'norm_ffn_g': dg2,
        'conv_b': dCB, 'final_norm_g': dg3}
    wide = ('pool_w', 'ssm_b_re', 'ssm_b_im', 'ssm_c_re', 'ssm_c_im')
    def travel(n):
        g = rep_grads[n]
        if n in wide:
            return g.reshape(-1, PACK_W if g.size % PACK_W == 0 else LANES).astype(BF16)
        return _view2d(g.reshape(weights[n].shape))

    (out_landed,) = _exchange_wait("scatter_out_wait", out_handle, dx)
    update('w_out', _with_own(out_landed, lax.dynamic_index_in_dim(out_sent[0], me, 0, keepdims=False), me))

    sent = [shards(dW_in), shards(dglu_w.astype(BF16))] + [travel(n) for n in REPLICATED] + [loss_acc]
    landed = _exchange("reduce_rest", sent, ['scatter'] * 2 + ['gather'] * (len(REPLICATED) + 1))
    for n, landed_n in zip(('w_in', 'glu_w') + REPLICATED, landed):
        update(n, landed_n)

    def loss_sum(parts_ref, o_ref):
        s = parts_ref[0]
        for q in range(1, N_DEV):
            s = s + parts_ref[q]
        o_ref[...] = s

    loss = pl.pallas_call(loss_sum, name="loss_sum", out_shape=jax.ShapeDtypeStruct((1, LANES), F32))(landed[-1])[0, 0]

    return (loss, dx[None], *[grads[n] for n in WEIGHTS], *[delta[n] for n in WEIGHTS],
            *[new_m[n] for n in WEIGHTS], *[new_v[n] for n in WEIGHTS])
```

```python
import functools

import jax
import jax.numpy as jnp
import numpy as np
from jax import lax
from jax.experimental import pallas as pl
from jax.experimental.pallas import tpu as pltpu

F32 = jnp.float32
BF16 = jnp.bfloat16
MESH_ID = pl.DeviceIdType.MESH

N_DEV = 8
EPS = 1e-6
POOL_WINDOWS = (2, 4, 8, 16)
POOL_GROUP = 128
POOL_PAD = 64
SSM_GROUP = 16
SSM_STATE = 64
SSM_BLOCK_GROUPS = 8
N_CHUNK = 8
LANES = 128
PACK_W = 1024
VMEM_LIMIT = 56 * 1024 * 1024
MM_VMEM_BUDGET = 40 * 1024 * 1024
MM_TILE_CAP = 1408
ADAM_TILE_BYTES = 4 * 1024 * 1024

ADAM_LR = 0.001
ADAM_B1 = 0.9
ADAM_B2 = 0.999
ADAM_EPS = 1e-08
ADAM_WD = 0.01
ADAM_STEP = 10

WEIGHTS = ['norm_mix_g', 'w_in', 'pool_w', 'pool_scale', 'ssm_log_neg_a_re', 'ssm_a_im', 'ssm_log_dt',
           'ssm_b_re', 'ssm_b_im', 'ssm_c_re', 'ssm_c_im', 'ssm_d', 'glu_w', 'glu_b', 'out_norm_pool_g',
           'out_norm_ssm_g', 'w_out', 'norm_ffn_g', 'w_up', 'conv_w', 'conv_b', 'w_down', 'final_norm_g']
SHARDED = ('w_in', 'w_out', 'w_down', 'glu_w', 'w_up', 'conv_w')
REPLICATED = tuple(n for n in WEIGHTS if n not in SHARDED)


def _pick(n, prefs):
    for p in prefs:
        if n % p == 0:
            return p
    return n


def _params(sem, vmem=None):
    return pltpu.CompilerParams(dimension_semantics=sem, vmem_limit_bytes=vmem or VMEM_LIMIT)


def _tiles(n, cap):
    return [d for d in range(LANES, min(n, cap) + 1, LANES) if n % d == 0] or [n]


def _mm(name, a, b, mode, out_dtype, add=None):
    if mode == 'nn':
        (M, K), (_, N) = a.shape, b.shape
    elif mode == 'nt':
        (M, K), (N, _) = a.shape, b.shape
    else:
        (K, M), (_, N) = a.shape, b.shape
    dims = {'nn': (((1,), (0,)), ((), ())), 'nt': (((1,), (1,)), ((), ())), 'tn': (((0,), (0,)), ((), ()))}[mode]
    sa, sb, so = a.dtype.itemsize, b.dtype.itemsize, jnp.dtype(out_dtype).itemsize
    best = None
    for tm in _tiles(M, MM_TILE_CAP):
        for tn in _tiles(N, MM_TILE_CAP):
            need = 2 * (tm * K * sa + tn * K * sb + tm * tn * (so + (4 if add is not None else 0)))
            if need <= MM_VMEM_BUDGET:
                key = (tm * tn / (tm + tn), tm * tn)
                if best is None or key > best[0]:
                    best = (key, tm, tn)
    _, tm, tn = best
    rows_inner = a.size * sa * (N // tn) + b.size * sb < a.size * sa + b.size * sb * (M // tm)

    def kern(*refs):
        a_ref, b_ref = refs[:2]
        o_ref = refs[-1]
        r = lax.dot_general(a_ref[...].astype(BF16), b_ref[...].astype(BF16), dims, preferred_element_type=F32)
        if add is not None:
            r = r + refs[2][...]
        o_ref[...] = r.astype(o_ref.dtype)

    if rows_inner:
        grid = (N // tn, M // tm)
        ij = lambda g0, g1: (g1, g0)
    else:
        grid = (M // tm, N // tn)
        ij = lambda g0, g1: (g0, g1)
    a_spec = (pl.BlockSpec((K, tm), lambda g0, g1: (0, ij(g0, g1)[0])) if mode == 'tn'
              else pl.BlockSpec((tm, K), lambda g0, g1: (ij(g0, g1)[0], 0)))
    b_spec = (pl.BlockSpec((tn, K), lambda g0, g1: (ij(g0, g1)[1], 0)) if mode == 'nt'
              else pl.BlockSpec((K, tn), lambda g0, g1: (0, ij(g0, g1)[1])))
    in_specs = [a_spec, b_spec]
    args = [a, b]
    if add is not None:
        in_specs.append(pl.BlockSpec((tm, tn), lambda g0, g1: ij(g0, g1)))
        args.append(add)
    return pl.pallas_call(
        kern, name=name, grid=grid, in_specs=in_specs,
        out_specs=pl.BlockSpec((tm, tn), lambda g0, g1: ij(g0, g1)),
        out_shape=jax.ShapeDtypeStruct((M, N), out_dtype),
        compiler_params=_params(("parallel", "parallel")),
    )(*args)


def _rowmap(name, fn, rows, fulls, out_rows, out_accs=(), tm=256):
    T = rows[0].shape[0]
    tm = min(tm, T)
    assert T % tm == 0
    n_in, n_row = len(rows) + len(fulls), len(out_rows)

    def kern(*refs):
        i = pl.program_id(0)
        res = fn(*[r[...] for r in refs[:n_in]])
        res = res if isinstance(res, (tuple, list)) else (res,)
        outs = refs[n_in:]
        for o, v in zip(outs[:n_row], res[:n_row]):
            o[...] = v.astype(o.dtype)
        for o, v in zip(outs[n_row:], res[n_row:]):
            @pl.when(i == 0)
            def _(o=o, v=v):
                o[...] = v

            @pl.when(i != 0)
            def _(o=o, v=v):
                o[...] += v

    def full_spec(shape):
        nd = len(shape)
        return pl.BlockSpec(tuple(shape), lambda i: (0,) * nd)

    in_specs = [pl.BlockSpec((tm, r.shape[1]), lambda i: (i, 0)) for r in rows] + [full_spec(f.shape) for f in fulls]
    out_specs = [pl.BlockSpec((tm, c), lambda i: (i, 0)) for c, _ in out_rows] + [full_spec(s) for s in out_accs]
    out_shape = [jax.ShapeDtypeStruct((T, c), dt) for c, dt in out_rows] + [jax.ShapeDtypeStruct(tuple(s), F32) for s in out_accs]
    return pl.pallas_call(
        kern, name=name, grid=(T // tm,), in_specs=in_specs, out_specs=out_specs, out_shape=out_shape,
        compiler_params=_params(("arbitrary",)),
    )(*rows, *fulls)


def _colsum(v):
    return jnp.sum(v, axis=0, keepdims=True)


def _rowmean(v):
    return jnp.mean(v, axis=-1, keepdims=True)


def _rms_fwd(x, g):
    r = lax.rsqrt(_rowmean(x * x) + EPS)
    return x * r * g


def _rms_bwd(x, g, dy):
    r = lax.rsqrt(_rowmean(x * x) + EPS)
    xh = x * r
    dxh = dy * g
    return r * (dxh - xh * _rowmean(dxh * xh)), _colsum(dy * xh)


def _sigmoid(v):
    return 0.5 * jnp.tanh(0.5 * v) + 0.5


def _gelu(y):
    c = np.sqrt(2.0 / np.pi).astype(np.float32)
    return 0.5 * y * (1.0 + jnp.tanh(c * (y + 0.044715 * (y * y * y))))


def _gelu_grad(y):
    c = np.sqrt(2.0 / np.pi).astype(np.float32)
    th = jnp.tanh(c * (y + 0.044715 * (y * y * y)))
    return 0.5 * (1.0 + th) + 0.5 * y * (1.0 - th * th) * c * (1.0 + 3.0 * 0.044715 * (y * y))


def _split3(v):
    hi = v.astype(BF16)
    r1 = v - hi.astype(F32)
    mid = r1.astype(BF16)
    lo = (r1 - mid.astype(F32)).astype(BF16)
    return hi, mid, lo


def _band(tm, lo_off, hi_off):
    shape = (tm, tm + 2 * POOL_PAD)
    d = lax.broadcasted_iota(jnp.int32, shape, 1) - lax.broadcasted_iota(jnp.int32, shape, 0) - POOL_PAD
    return ((d >= lo_off) & (d < hi_off)).astype(BF16)


def _band_sum(band, ext):
    hi, mid, lo = _split3(ext)
    dot = functools.partial(jnp.dot, preferred_element_type=F32)
    return dot(band, hi) + dot(band, mid) + dot(band, lo)


def _window_count(r0, tm, half, T):
    t = r0 + lax.broadcasted_iota(jnp.int32, (tm, 1), 0)
    return (jnp.minimum(t + half, T) - jnp.maximum(t - half, 0)).astype(F32)


def _pool_block(u_ref, w_ref, r0, tm, T):
    ext = u_ref[pl.ds(r0, tm + 2 * POOL_PAD), :]
    ctr = u_ref[pl.ds(r0 + POOL_PAD, tm), :]
    pooled, conc = [], []
    for gi, w in enumerate(POOL_WINDOWS):
        half = w // 2
        cols = slice(gi * POOL_GROUP, (gi + 1) * POOL_GROUP)
        ws = _band_sum(_band(tm, -half, half), ext[:, cols])
        p = ws / _window_count(r0, tm, half, T) - ctr[:, cols]
        pooled.append(p)
        conc.append(jnp.dot(p.astype(BF16), w_ref[gi], preferred_element_type=F32))
    return pooled, jnp.concatenate(conc, axis=1)


def _pool_fwd(u_pad, pool_w_bf, pool_scale, g_pool, T, tm):
    nw = len(POOL_WINDOWS)
    C = nw * POOL_GROUP

    def kern(u_ref, w_ref, sc_ref, g_ref, o_ref):
        r0 = pl.multiple_of(pl.program_id(0) * tm, tm)
        _, conc = _pool_block(u_ref, w_ref, r0, tm, T)
        o_ref[...] = _rms_fwd(conc * sc_ref[...], g_ref[...]).astype(o_ref.dtype)

    return pl.pallas_call(
        kern, name="pool_fwd", grid=(T // tm,),
        in_specs=[pl.BlockSpec(u_pad.shape, lambda i: (0, 0)), pl.BlockSpec(pool_w_bf.shape, lambda i: (0, 0, 0)),
                  pl.BlockSpec((1, C), lambda i: (0, 0)), pl.BlockSpec((1, C), lambda i: (0, 0))],
        out_specs=pl.BlockSpec((tm, C), lambda i: (i, 0)),
        out_shape=jax.ShapeDtypeStruct((T, C), BF16),
        compiler_params=_params(("arbitrary",)),
    )(u_pad, pool_w_bf, pool_scale, g_pool)


def _pool_bwd_rows(u_pad, d_ypn, pool_w_bf, pool_scale, g_pool, T, tm):
    nw = len(POOL_WINDOWS)
    C = nw * POOL_GROUP

    def kern(u_ref, dy_ref, w_ref, sc_ref, g_ref, q_ref, dp_ref, dw_ref, dsc_ref, dg_ref):
        i = pl.program_id(0)
        r0 = pl.multiple_of(i * tm, tm)
        pooled, conc = _pool_block(u_ref, w_ref, r0, tm, T)
        sc = sc_ref[...]
        dyp, dg = _rms_bwd(conc * sc, g_ref[...], dy_ref[...])
        dsc = _colsum(dyp * conc)
        dconc = (dyp * sc).astype(BF16)
        dws = []
        for gi, w in enumerate(POOL_WINDOWS):
            cols = slice(gi * POOL_GROUP, (gi + 1) * POOL_GROUP)
            dc = dconc[:, cols]
            dp = lax.dot_general(dc, w_ref[gi], (((1,), (1,)), ((), ())), preferred_element_type=F32)
            dp_ref[:, cols] = dp
            q_ref[:, cols] = dp / _window_count(r0, tm, w // 2, T)
            dws.append(lax.dot_general(pooled[gi].astype(BF16), dc, (((0,), (0,)), ((), ())), preferred_element_type=F32))

        @pl.when(i == 0)
        def _():
            for gi in range(nw):
                dw_ref[gi] = dws[gi]
            dsc_ref[...] = dsc
            dg_ref[...] = dg

        @pl.when(i != 0)
        def _():
            for gi in range(nw):
                dw_ref[gi] += dws[gi]
            dsc_ref[...] += dsc
            dg_ref[...] += dg

    full2 = pl.BlockSpec((1, C), lambda i: (0, 0))
    return pl.pallas_call(
        kern, name="pool_bwd_rows", grid=(T // tm,),
        in_specs=[pl.BlockSpec(u_pad.shape, lambda i: (0, 0)), pl.BlockSpec((tm, C), lambda i: (i, 0)),
                  pl.BlockSpec(pool_w_bf.shape, lambda i: (0, 0, 0)), full2, full2],
        out_specs=[pl.BlockSpec((tm, C), lambda i: (i, 0)), pl.BlockSpec((tm, C), lambda i: (i, 0)),
                   pl.BlockSpec((nw, POOL_GROUP, POOL_GROUP), lambda i: (0, 0, 0)), full2, full2],
        out_shape=[jax.ShapeDtypeStruct((T, C), F32), jax.ShapeDtypeStruct((T, C), F32),
                   jax.ShapeDtypeStruct((nw, POOL_GROUP, POOL_GROUP), F32),
                   jax.ShapeDtypeStruct((1, C), F32), jax.ShapeDtypeStruct((1, C), F32)],
        compiler_params=_params(("arbitrary",)),
    )(u_pad, d_ypn, pool_w_bf, pool_scale, g_pool)


def _pool_bwd_band(q_pad, dpooled, T, tm):
    C = dpooled.shape[1]

    def kern(q_ref, dp_ref, o_ref):
        r0 = pl.multiple_of(pl.program_id(0) * tm, tm)
        ext = q_ref[pl.ds(r0, tm + 2 * POOL_PAD), :]
        for gi, w in enumerate(POOL_WINDOWS):
            half = w // 2
            cols = slice(gi * POOL_GROUP, (gi + 1) * POOL_GROUP)
            o_ref[:, cols] = _band_sum(_band(tm, -half + 1, half + 1), ext[:, cols]) - dp_ref[:, cols]

    return pl.pallas_call(
        kern, name="pool_bwd_band", grid=(T // tm,),
        in_specs=[pl.BlockSpec(q_pad.shape, lambda i: (0, 0)), pl.BlockSpec((tm, C), lambda i: (i, 0))],
        out_specs=pl.BlockSpec((tm, C), lambda i: (i, 0)),
        out_shape=jax.ShapeDtypeStruct((T, C), F32),
        compiler_params=_params(("arbitrary",)),
    )(q_pad, dpooled)


def _cmul(ar, ai, br, bi):
    return ar * br - ai * bi, ar * bi + ai * br


def _ssm_discretise(log_neg_a_re, a_im, log_dt):
    dt = jnp.exp(log_dt)
    a_re = -jnp.exp(log_neg_a_re)
    mag = jnp.exp(a_re * dt)
    ang = a_im * dt
    lam_re, lam_im = mag * jnp.cos(ang), mag * jnp.sin(ang)
    den = a_re * a_re + a_im * a_im
    f_re = ((lam_re - 1.0) * a_re + lam_im * a_im) / den
    f_im = (lam_im * a_re - (lam_re - 1.0) * a_im) / den
    return lam_re, lam_im, f_re, f_im


def _ssm_params_fwd(log_neg_a_re, a_im, log_dt):
    rows, n = log_neg_a_re.shape

    def kern(a_ref, b_ref, c_ref, o1, o2, o3, o4):
        for o, v in zip((o1, o2, o3, o4), _ssm_discretise(a_ref[...], b_ref[...], c_ref[...])):
            o[...] = v

    return pl.pallas_call(kern, name="ssm_params_fwd", out_shape=[jax.ShapeDtypeStruct((rows, n), F32)] * 4)(
        log_neg_a_re, a_im, log_dt)


def _ssm_params_bwd(log_neg_a_re, a_im, log_dt, cots):
    rows, n = log_neg_a_re.shape

    def kern(a_ref, b_ref, c_ref, g1, g2, g3, g4, o1, o2, o3):
        _, vjp = jax.vjp(_ssm_discretise, a_ref[...], b_ref[...], c_ref[...])
        d1, d2, d3 = vjp((g1[...], g2[...], g3[...], g4[...]))
        o1[...] = d1
        o2[...] = d2
        o3[...] = d3

    return pl.pallas_call(
        kern, name="ssm_params_bwd",
        out_shape=[jax.ShapeDtypeStruct((rows, n), F32), jax.ShapeDtypeStruct((rows, n), F32),
                   jax.ShapeDtypeStruct((rows, 1), F32)])(log_neg_a_re, a_im, log_dt, *cots)


def _cpow(lr, li, n):
    out = None
    br, bi = lr, li
    while n:
        if n & 1:
            out = (br, bi) if out is None else _cmul(out[0], out[1], br, bi)
        n >>= 1
        if n:
            br, bi = _cmul(br, bi, br, bi)
    return out


def _slab(t):
    return pl.ds(pl.multiple_of(t * N_CHUNK, N_CHUNK), N_CHUNK)


def _shift_chunks(vr, vi, reverse):
    sub = lax.broadcasted_iota(jnp.int32, vr.shape, 0)
    if reverse:
        keep = sub != N_CHUNK - 1
        return jnp.where(keep, pltpu.roll(vr, N_CHUNK - 1, 0), 0.0), jnp.where(keep, pltpu.roll(vi, N_CHUNK - 1, 0), 0.0)
    keep = sub != 0
    return jnp.where(keep, pltpu.roll(vr, 1, 0), 0.0), jnp.where(keep, pltpu.roll(vi, 1, 0), 0.0)


def _chunk_scan(xr, xi, lr, li, Lc, reverse, prev=None):
    C = xr.shape[1]
    lrb, lib = jnp.broadcast_to(lr, (N_CHUNK, C)), jnp.broadcast_to(li, (N_CHUNK, C))
    zero = jnp.zeros((N_CHUNK, C), F32)

    def step_of(k):
        return (Lc - 1 - k) if reverse else k

    def advance(sr, si, t):
        return lrb * sr - lib * si + xr[_slab(t), :], lrb * si + lib * sr + xi[_slab(t), :]

    def pass1(k, c):
        return advance(c[0], c[1], step_of(k))

    er, ei = lax.fori_loop(0, Lc, pass1, (zero, zero), unroll=4)
    pr, pi = _cpow(lr, li, Lc)
    prb, pib = jnp.broadcast_to(pr, (N_CHUNK, C)), jnp.broadcast_to(pi, (N_CHUNK, C))
    cr, ci = zero, zero
    for _ in range(N_CHUNK - 1):
        cr, ci = _shift_chunks(er + prb * cr - pib * ci, ei + prb * ci + pib * cr, reverse)

    if prev is None:
        def pass2(k, c):
            t = step_of(k)
            nr, ni = advance(c[0], c[1], t)
            xr[_slab(t), :] = nr
            xi[_slab(t), :] = ni
            return nr, ni

        lax.fori_loop(0, Lc, pass2, (cr, ci), unroll=4)
        return None

    qr, qi = prev
    def pair(ar, ai, wr, wi, acc):
        return acc[0] + ar * wr + ai * wi, acc[1] + ai * wr - ar * wi

    def pass2(k, c):
        sr, si, accr, acci = c
        t = step_of(k)
        nr, ni = advance(sr, si, t)
        xr[_slab(t), :] = nr
        xi[_slab(t), :] = ni
        tp = t - 1 if reverse else t + 1
        accr, acci = pair(nr, ni, qr[_slab(tp), :], qi[_slab(tp), :], (accr, acci))
        return nr, ni, accr, acci

    sr, si, accr, acci = lax.fori_loop(0, Lc - 1, pass2, (cr, ci, zero, zero), unroll=4)
    t = step_of(Lc - 1)
    nr, ni = advance(sr, si, t)
    xr[_slab(t), :] = nr
    xi[_slab(t), :] = ni
    edge = step_of(0)
    wr, wi = _shift_chunks(qr[_slab(edge), :], qi[_slab(edge), :], not reverse)
    return pair(nr, ni, wr, wi, (accr, acci))


def _ssm_dims(T):
    assert T % (N_CHUNK * 64) == 0
    rc = _pick(T, (512, 256, 128, 64))
    return T // N_CHUNK, rc, T // rc


def _ssm_project_in(u_ref, bre_ref, bim_ref, xr, xi, rc, nrc):
    for c in range(nrc):
        rows = pl.ds(c * rc, rc)
        ub = u_ref[rows, :]
        xr[rows, :] = jnp.dot(ub, bre_ref[...], preferred_element_type=F32)
        xi[rows, :] = jnp.dot(ub, bim_ref[...], preferred_element_type=F32)


def _ssm_fwd(name, u_bf, mats, vecs, reverse):
    T, W = u_bf.shape
    nb = W // LANES
    SC = SSM_BLOCK_GROUPS * SSM_STATE
    Lc, rc, nrc = _ssm_dims(T)
    bre, bim, cre, cim = mats
    lam_re, lam_im, f_re, f_im = vecs

    def kern(u_ref, bre_ref, bim_ref, cre_ref, cim_ref, lr_ref, li_ref, fr_ref, fi_ref, y_ref, xr, xi):
        _ssm_project_in(u_ref, bre_ref, bim_ref, xr, xi, rc, nrc)
        _chunk_scan(xr, xi, lr_ref[...], li_ref[...], Lc, reverse)
        fr, fi = fr_ref[...], fi_ref[...]
        for c in range(nrc):
            rows = pl.ds(c * rc, rc)
            sr, si = _cmul(fr, fi, xr[rows, :], xi[rows, :])
            y_ref[rows, :] = (jnp.dot(sr.astype(BF16), cre_ref[...], preferred_element_type=F32)
                              - jnp.dot(si.astype(BF16), cim_ref[...], preferred_element_type=F32))

    col = pl.BlockSpec((T, LANES), lambda m: (0, m))
    mat_in = pl.BlockSpec((None, LANES, SC), lambda m: (m, 0, 0))
    mat_out = pl.BlockSpec((None, SC, LANES), lambda m: (m, 0, 0))
    vec = pl.BlockSpec((None, 1, SC), lambda m: (m, 0, 0))
    return pl.pallas_call(
        kern, name=name, grid=(nb,),
        in_specs=[col, mat_in, mat_in, mat_out, mat_out, vec, vec, vec, vec],
        out_specs=col, out_shape=jax.ShapeDtypeStruct((T, W), F32),
        scratch_shapes=[pltpu.VMEM((T, SC), F32), pltpu.VMEM((T, SC), F32)],
        compiler_params=_params(("arbitrary",)),
    )(u_bf, bre, bim, cre, cim, lam_re, lam_im, f_re, f_im)


def _ssm_bwd(name, u_bf, dy_bf, du_in, mats, mats_t, vecs, reverse):
    T, W = u_bf.shape
    nb = W // LANES
    SC = SSM_BLOCK_GROUPS * SSM_STATE
    Lc, rc, nrc = _ssm_dims(T)
    bre, bim = mats
    bre_t, bim_t, cre_t, cim_t = mats_t
    lam_re, lam_im, f_re, f_im = vecs
    contract_rows = (((0,), (0,)), ((), ()))

    def kern(u_ref, dy_ref, dui_ref, bre_ref, bim_ref, bret_ref, bimt_ref, cret_ref, cimt_ref,
             lr_ref, li_ref, fr_ref, fi_ref,
             du_ref, dbre_ref, dbim_ref, dcre_ref, dcim_ref, dlr_ref, dli_ref, dfr_ref, dfi_ref,
             xr, xi, gr, gi):
        lr, li, fr, fi = lr_ref[...], li_ref[...], fr_ref[...], fi_ref[...]
        _ssm_project_in(u_ref, bre_ref, bim_ref, xr, xi, rc, nrc)
        _chunk_scan(xr, xi, lr, li, Lc, reverse)
        mre = jnp.zeros((LANES, SC), F32)
        mim = jnp.zeros((LANES, SC), F32)
        dfr = jnp.zeros((1, SC), F32)
        dfi = jnp.zeros((1, SC), F32)
        for c in range(nrc):
            rows = pl.ds(c * rc, rc)
            dy = dy_ref[rows, :]
            x_r, x_i = xr[rows, :], xi[rows, :]
            g_r = jnp.dot(dy, cret_ref[...], preferred_element_type=F32)
            g_i = -jnp.dot(dy, cimt_ref[...], preferred_element_type=F32)
            mre += lax.dot_general(dy, x_r.astype(BF16), contract_rows, preferred_element_type=F32)
            mim += lax.dot_general(dy, x_i.astype(BF16), contract_rows, preferred_element_type=F32)
            dfr += _colsum(g_r * x_r + g_i * x_i)
            dfi += _colsum(g_i * x_r - g_r * x_i)
            gr[rows, :] = fr * g_r + fi * g_i
            gi[rows, :] = fr * g_i - fi * g_r
        dcre_ref[...] = fr * mre - fi * mim
        dcim_ref[...] = -(fr * mim + fi * mre)
        dfr_ref[...] = dfr
        dfi_ref[...] = dfi
        accr, acci = _chunk_scan(gr, gi, lr, -li, Lc, not reverse, prev=(xr, xi))
        dlr_ref[...] = _colsum(accr)
        dli_ref[...] = _colsum(acci)
        dbre = jnp.zeros((LANES, SC), F32)
        dbim = jnp.zeros((LANES, SC), F32)
        for c in range(nrc):
            rows = pl.ds(c * rc, rc)
            ub = u_ref[rows, :]
            a_r, a_i = gr[rows, :].astype(BF16), gi[rows, :].astype(BF16)
            dbre += lax.dot_general(ub, a_r, contract_rows, preferred_element_type=F32)
            dbim += lax.dot_general(ub, a_i, contract_rows, preferred_element_type=F32)
            du_ref[rows, :] = (dui_ref[rows, :] + jnp.dot(a_r, bret_ref[...], preferred_element_type=F32)
                               + jnp.dot(a_i, bimt_ref[...], preferred_element_type=F32))
        dbre_ref[...] = dbre
        dbim_ref[...] = dbim

    col = pl.BlockSpec((T, LANES), lambda m: (0, m))
    mat_in = pl.BlockSpec((None, LANES, SC), lambda m: (m, 0, 0))
    mat_out = pl.BlockSpec((None, SC, LANES), lambda m: (m, 0, 0))
    vec = pl.BlockSpec((None, 1, SC), lambda m: (m, 0, 0))
    mat_shape = jax.ShapeDtypeStruct((nb, LANES, SC), F32)
    vec_shape = jax.ShapeDtypeStruct((nb, 1, SC), F32)
    return pl.pallas_call(
        kern, name=name, grid=(nb,),
        in_specs=[col, col, col, mat_in, mat_in, mat_out, mat_out, mat_in, mat_in, vec, vec, vec, vec],
        out_specs=[col, mat_in, mat_in, mat_in, mat_in, vec, vec, vec, vec],
        out_shape=[jax.ShapeDtypeStruct((T, W), F32), mat_shape, mat_shape, mat_shape, mat_shape,
                   vec_shape, vec_shape, vec_shape, vec_shape],
        scratch_shapes=[pltpu.VMEM((T, SC), F32)] * 4,
        compiler_params=_params(("arbitrary",)),
    )(u_bf, dy_bf, du_in, bre, bim, bre_t, bim_t, cre_t, cim_t, lam_re, lam_im, f_re, f_im)


def _block_diag_in(b):
    d, G, N, H = b.shape
    nb = G // SSM_BLOCK_GROUPS
    eye = jnp.eye(SSM_BLOCK_GROUPS, dtype=b.dtype)
    v = b.reshape(d, nb, SSM_BLOCK_GROUPS, N, H)
    return jnp.einsum('dmgnh,gk->dmghkn', v, eye).reshape(d, nb, SSM_BLOCK_GROUPS * H, SSM_BLOCK_GROUPS * N)


def _block_diag_in_grad(m, N, H):
    d, nb = m.shape[:2]
    eye = jnp.eye(SSM_BLOCK_GROUPS, dtype=m.dtype)
    v = m.reshape(d, nb, SSM_BLOCK_GROUPS, H, SSM_BLOCK_GROUPS, N)
    return jnp.einsum('dmghkn,gk->dmgnh', v, eye).reshape(d, nb * SSM_BLOCK_GROUPS, N, H)


def _block_diag_out(c):
    return jnp.swapaxes(_block_diag_in(jnp.swapaxes(c, 2, 3)), 2, 3)


def _block_diag_out_grad(m, N, H):
    return jnp.swapaxes(_block_diag_in_grad(m, N, H), 2, 3)


def _to_chunk_rows(a):
    T, C = a.shape
    return a.reshape(N_CHUNK, T // N_CHUNK, C).transpose(1, 0, 2).reshape(T, C)


def _from_chunk_rows(a):
    T, C = a.shape
    return a.reshape(T // N_CHUNK, N_CHUNK, C).transpose(1, 0, 2).reshape(T, C)


def _conv_taps(ref, c, R, T):
    n = T // R
    r0 = pl.multiple_of(c * R, R)
    main = ref[pl.ds(r0, R), :]
    before = ref[pl.ds(pl.multiple_of(jnp.maximum(r0 - 8, 0), 8), 8), :]
    after = ref[pl.ds(pl.multiple_of(jnp.minimum(r0 + R, T - 8), 8), 8), :]
    ext = jnp.concatenate([jnp.where(c > 0, before, 0.0), main, jnp.where(c < n - 1, after, 0.0)], axis=0)
    m1 = pltpu.roll(ext, 1, 0)[8:8 + R]
    p1 = pltpu.roll(ext, R + 15, 0)[8:8 + R]
    return m1, main, p1, r0


def _conv_specs(T, F, cb):
    nj = F // cb
    lo = lambda rows: pl.BlockSpec((rows, cb), lambda j: (0, j))
    hi = lambda rows: pl.BlockSpec((rows, cb), lambda j: (0, j + nj))
    return nj, lo, hi


def _convact_fwd(up, conv_w, conv_b, F):
    T = up.shape[0]
    cb = _pick(F, (256, 128))
    R = _pick(T, (256, 128, 64))
    nj, lo, hi = _conv_specs(T, F, cb)

    def kern(uv, ug, wv, wg, bv, bg, o_ref):
        def body(c, carry):
            v1, v0, v2, r0 = _conv_taps(uv, c, R, T)
            g1, g0, g2, _ = _conv_taps(ug, c, R, T)
            val = v1 * wv[0:1, :] + v0 * wv[1:2, :] + v2 * wv[2:3, :] + bv[...]
            gate = g1 * wg[0:1, :] + g0 * wg[1:2, :] + g2 * wg[2:3, :] + bg[...]
            o_ref[pl.ds(r0, R), :] = (val * (gate * _sigmoid(gate))).astype(o_ref.dtype)
            return carry

        lax.fori_loop(0, T // R, body, 0)

    return pl.pallas_call(
        kern, name="convact_fwd", grid=(nj,),
        in_specs=[lo(T), hi(T), lo(3), hi(3), lo(1), hi(1)],
        out_specs=lo(T), out_shape=jax.ShapeDtypeStruct((T, F), BF16),
        compiler_params=_params(("arbitrary",)),
    )(up, up, conv_w, conv_w, conv_b, conv_b)


def _convact_bwd(up, dact, conv_w, conv_b, F):
    T = up.shape[0]
    cb = _pick(F, (256, 128))
    R = _pick(T, (256, 128, 64))
    nj, lo, hi = _conv_specs(T, F, cb)

    def kern(uv, ug, da, wv, wg, bv, bg, dupv, dupg, dwv, dwg, dbv, dbg, sv, sg):
        zero = jnp.zeros((1, cb), F32)

        def pass_a(c, acc):
            v1, v0, v2, r0 = _conv_taps(uv, c, R, T)
            g1, g0, g2, _ = _conv_taps(ug, c, R, T)
            val = v1 * wv[0:1, :] + v0 * wv[1:2, :] + v2 * wv[2:3, :] + bv[...]
            gate = g1 * wg[0:1, :] + g0 * wg[1:2, :] + g2 * wg[2:3, :] + bg[...]
            sig = _sigmoid(gate)
            d = da[pl.ds(r0, R), :]
            dval = d * (gate * sig)
            dgate = d * val * (sig * (1.0 + gate * (1.0 - sig)))
            sv[pl.ds(r0, R), :] = dval
            sg[pl.ds(r0, R), :] = dgate
            terms = (dval * v1, dval * v0, dval * v2, dval, dgate * g1, dgate * g0, dgate * g2, dgate)
            return tuple(a + _colsum(t) for a, t in zip(acc, terms))

        acc = lax.fori_loop(0, T // R, pass_a, (zero,) * 8)
        for k in range(3):
            dwv[k:k + 1, :] = acc[k]
            dwg[k:k + 1, :] = acc[4 + k]
        dbv[...] = acc[3]
        dbg[...] = acc[7]

        def pass_b(c, carry):
            v1, v0, v2, r0 = _conv_taps(sv, c, R, T)
            g1, g0, g2, _ = _conv_taps(sg, c, R, T)
            dupv[pl.ds(r0, R), :] = (v2 * wv[0:1, :] + v0 * wv[1:2, :] + v1 * wv[2:3, :]).astype(dupv.dtype)
            dupg[pl.ds(r0, R), :] = (g2 * wg[0:1, :] + g0 * wg[1:2, :] + g1 * wg[2:3, :]).astype(dupg.dtype)
            return carry

        lax.fori_loop(0, T // R, pass_b, 0)

    dupv, dupg, dwv, dwg, dbv, dbg = pl.pallas_call(
        kern, name="convact_bwd", grid=(nj,),
        in_specs=[lo(T), hi(T), lo(T), lo(3), hi(3), lo(1), hi(1)],
        out_specs=[lo(T), lo(T), lo(3), lo(3), lo(1), lo(1)],
        out_shape=[jax.ShapeDtypeStruct((T, F), BF16), jax.ShapeDtypeStruct((T, F), BF16),
                   jax.ShapeDtypeStruct((3, F), F32), jax.ShapeDtypeStruct((3, F), F32),
                   jax.ShapeDtypeStruct((1, F), F32), jax.ShapeDtypeStruct((1, F), F32)],
        scratch_shapes=[pltpu.VMEM((T, cb), F32), pltpu.VMEM((T, cb), F32)],
        compiler_params=_params(("arbitrary",)),
    )(up, up, dact, conv_w, conv_w, conv_b, conv_b)
    return (jnp.concatenate([dupv, dupg], axis=1), jnp.concatenate([dwv, dwg], axis=1),
            jnp.concatenate([dbv, dbg], axis=1))


def _exchange(name, bufs, modes):
    n = len(bufs)

    def kern(*refs):
        ins, outs = refs[:n], refs[n:2 * n]
        send_sems, recv_sems, local_sems = refs[2 * n:]
        x, y, c = lax.axis_index("x"), lax.axis_index("y"), lax.axis_index("c")
        me = 4 * x + 2 * y + c
        started = []
        for b in range(n):
            gather = modes[b] == 'gather'
            own = pltpu.make_async_copy(ins[b] if gather else ins[b].at[me], outs[b].at[me], local_sems.at[b])
            own.start()
            started.append(own)
            for k in range(1, N_DEV):
                px = 1 - x if k & 4 else x
                py = 1 - y if k & 2 else y
                pc = 1 - c if k & 1 else c
                peer = 4 * px + 2 * py + pc
                sem = b * (N_DEV - 1) + k - 1
                pltpu.make_async_remote_copy(
                    src_ref=ins[b] if gather else ins[b].at[peer], dst_ref=outs[b].at[me],
                    send_sem=send_sems.at[sem], recv_sem=recv_sems.at[sem],
                    device_id=(px, py, pc), device_id_type=MESH_ID).start()
        for b in range(n):
            gather = modes[b] == 'gather'
            for k in range(1, N_DEV):
                px = 1 - x if k & 4 else x
                py = 1 - y if k & 2 else y
                pc = 1 - c if k & 1 else c
                peer = 4 * px + 2 * py + pc
                sem = b * (N_DEV - 1) + k - 1
                arrived = pltpu.make_async_remote_copy(
                    src_ref=ins[b] if gather else ins[b].at[peer], dst_ref=outs[b].at[peer],
                    send_sem=send_sems.at[sem], recv_sem=recv_sems.at[sem],
                    device_id=(px, py, pc), device_id_type=MESH_ID)
                arrived.wait_recv()
                arrived.wait_send()
        for own in started:
            own.wait()

    hbm = pl.BlockSpec(memory_space=pltpu.HBM)
    out_shape = [jax.ShapeDtypeStruct((N_DEV,) + tuple(b.shape[-2:]), b.dtype) for b in bufs]
    return pl.pallas_call(
        kern, name=name, in_specs=[hbm] * n, out_specs=[hbm] * n, out_shape=out_shape,
        scratch_shapes=[pltpu.SemaphoreType.DMA((n * (N_DEV - 1),)), pltpu.SemaphoreType.DMA((n * (N_DEV - 1),)),
                        pltpu.SemaphoreType.DMA((n,))],
    )(*bufs)


def _peers(x, y, c):
    out = []
    for k in range(1, N_DEV):
        px = 1 - x if k & 4 else x
        py = 1 - y if k & 2 else y
        pc = 1 - c if k & 1 else c
        out.append((k, (px, py, pc), 4 * px + 2 * py + pc))
    return out


def _exchange_start(name, bufs, modes, after):
    n = len(bufs)
    ncopy = n * (N_DEV - 1)
    lands = [lax.empty((N_DEV,) + tuple(b.shape[-2:]), b.dtype) for b in bufs]

    def body(*refs):
        ins, land_in = refs[:n], refs[n:2 * n]
        send_sems, recv_sems = refs[2 * n + 1], refs[2 * n + 2]
        token = refs[-1]
        x, y, c = lax.axis_index("x"), lax.axis_index("y"), lax.axis_index("c")
        me = 4 * x + 2 * y + c
        for b in range(n):
            for k, peer, slot in _peers(x, y, c):
                sem = b * (N_DEV - 1) + k - 1
                pltpu.make_async_remote_copy(
                    src_ref=ins[b] if modes[b] == 'gather' else ins[b].at[slot], dst_ref=land_in[b].at[me],
                    send_sem=send_sems.at[sem], recv_sem=recv_sems.at[sem],
                    device_id=peer, device_id_type=MESH_ID).start()
        token[...] = jnp.zeros_like(token)

    hbm = pl.BlockSpec(memory_space=pltpu.HBM)
    sem_spec = pl.BlockSpec(memory_space=pltpu.SEMAPHORE)
    operands = [pltpu.with_memory_space_constraint(a, pltpu.HBM) for a in list(bufs) + lands]
    res = pl.pallas_call(
        body, name=name,
        out_shape=(pltpu.SemaphoreType.DMA((ncopy,)), pltpu.SemaphoreType.DMA((ncopy,)),
                   *[pltpu.HBM(a.shape, a.dtype) for a in operands], jax.ShapeDtypeStruct((8, LANES), F32)),
        in_specs=[hbm] * (2 * n) + [pl.BlockSpec(memory_space=pl.ANY)],
        out_specs=(sem_spec, sem_spec, *([hbm] * (2 * n)), pl.BlockSpec(memory_space=pltpu.VMEM)),
        input_output_aliases={i: 2 + i for i in range(2 * n)},
        compiler_params=pltpu.CompilerParams(has_side_effects=pltpu.SideEffectType.DATAFLOW_SIDE_EFFECTING),
    )(*operands, after)
    return (res[0], res[1], list(res[2:2 + n]), list(res[2 + n:2 + 2 * n]), tuple(modes)), res[-1]


def _exchange_wait(name, handle, after):
    send_sems, recv_sems, srcs, lands, modes = handle
    n = len(srcs)

    def body(*refs):
        src_in, land_in = refs[:n], refs[n:2 * n]
        send_ref, recv_ref = refs[2 * n], refs[2 * n + 1]
        x, y, c = lax.axis_index("x"), lax.axis_index("y"), lax.axis_index("c")
        for b in range(n):
            for k, peer, slot in _peers(x, y, c):
                sem = b * (N_DEV - 1) + k - 1
                copy = pltpu.make_async_remote_copy(
                    src_ref=src_in[b] if modes[b] == 'gather' else src_in[b].at[slot], dst_ref=land_in[b].at[slot],
                    send_sem=send_ref.at[sem], recv_sem=recv_ref.at[sem],
                    device_id=peer, device_id_type=MESH_ID)
                copy.wait_send()
                copy.wait_recv()

    hbm = pl.BlockSpec(memory_space=pltpu.HBM)
    sem_spec = pl.BlockSpec(memory_space=pltpu.SEMAPHORE)
    res = pl.pallas_call(
        body, name=name,
        out_shape=tuple(pltpu.HBM(a.shape, a.dtype) for a in srcs + lands),
        in_specs=[hbm] * (2 * n) + [sem_spec, sem_spec, pl.BlockSpec(memory_space=pl.ANY)],
        out_specs=tuple([hbm] * (2 * n)),
        input_output_aliases={i: i for i in range(2 * n)},
        compiler_params=pltpu.CompilerParams(has_side_effects=pltpu.SideEffectType.DATAFLOW_SIDE_EFFECTING),
    )(*srcs, *lands, send_sems, recv_sems, after)
    return list(res[n:])


def _with_own(landed, own, me):
    return lax.dynamic_update_slice(landed, own[None], (me,) + (0,) * own.ndim)


def _adamw(w, g, m, v):
    m2 = ADAM_B1 * m + (1.0 - ADAM_B1) * g
    v2 = ADAM_B2 * v + (1.0 - ADAM_B2) * (g * g)
    m_hat = m2 / (1.0 - ADAM_B1 ** ADAM_STEP)
    v_hat = v2 / (1.0 - ADAM_B2 ** ADAM_STEP)
    return -ADAM_LR * (m_hat / (jnp.sqrt(v_hat) + ADAM_EPS) + ADAM_WD * w), m2, v2


def _sum_adamw(name, landed, w, m, v):
    R, C = w.shape
    row_bytes = N_DEV * (-(-C // LANES) * LANES) * landed.dtype.itemsize
    tiles = [d for d in range(16, R, 16) if R % d == 0 and d * row_bytes <= ADAM_TILE_BYTES]
    tr = max(tiles) if tiles and R * row_bytes > ADAM_TILE_BYTES else R

    def kern(x_ref, w_ref, m_ref, v_ref, g_out, d_out, m_out, v_out):
        g = x_ref[0].astype(F32)
        for q in range(1, N_DEV):
            g = g + x_ref[q].astype(F32)
        g_out[...] = g
        d_out[...], m_out[...], v_out[...] = _adamw(w_ref[...], g, m_ref[...], v_ref[...])

    blk = pl.BlockSpec((tr, C), lambda i: (i, 0))
    return pl.pallas_call(
        kern, name=name, grid=(R // tr,),
        in_specs=[pl.BlockSpec((N_DEV, tr, C), lambda i: (0, i, 0)), blk, blk, blk],
        out_specs=[blk] * 4, out_shape=[jax.ShapeDtypeStruct((R, C), F32)] * 4,
        compiler_params=_params(("arbitrary",)),
    )(landed, w, m, v)


def _view2d(a):
    if a.ndim == 1:
        return a.reshape(1, -1)
    return a.reshape(-1, a.shape[-1])


def kernel(x, norm_mix_g, w_in, pool_w, pool_scale, ssm_log_neg_a_re, ssm_a_im, ssm_log_dt, ssm_b_re, ssm_b_im, ssm_c_re, ssm_c_im, ssm_d, glu_w, glu_b, out_norm_pool_g, out_norm_ssm_g, w_out, norm_ffn_g, w_up, conv_w, conv_b, w_down, final_norm_g, loss_target, m_norm_mix_g, m_w_in, m_pool_w, m_pool_scale, m_ssm_log_neg_a_re, m_ssm_a_im, m_ssm_log_dt, m_ssm_b_re, m_ssm_b_im, m_ssm_c_re, m_ssm_c_im, m_ssm_d, m_glu_w, m_glu_b, m_out_norm_pool_g, m_out_norm_ssm_g, m_w_out, m_norm_ffn_g, m_w_up, m_conv_w, m_conv_b, m_w_down, m_final_norm_g, v_norm_mix_g, v_w_in, v_pool_w, v_pool_scale, v_ssm_log_neg_a_re, v_ssm_a_im, v_ssm_log_dt, v_ssm_b_re, v_ssm_b_im, v_ssm_c_re, v_ssm_c_im, v_ssm_d, v_glu_w, v_glu_b, v_out_norm_pool_g, v_out_norm_ssm_g, v_w_out, v_norm_ffn_g, v_w_up, v_conv_w, v_conv_b, v_w_down, v_final_norm_g):
    given = dict(locals())
    weights = {n: given[n] for n in WEIGHTS}
    mom1 = {n: given["m_" + n] for n in WEIGHTS}
    mom2 = {n: given["v_" + n] for n in WEIGHTS}

    xs = x[0]
    tgt = loss_target[0]
    T, D = xs.shape
    DP = len(POOL_WINDOWS) * POOL_GROUP
    DS = D - DP
    G = DS // SSM_GROUP
    N, H = SSM_STATE, SSM_GROUP
    F2 = w_up.shape[1] * N_DEV
    F = F2 // 2
    fs = w_up.shape[1]
    row = lambda a: a.reshape(1, -1)

    me = 4 * lax.axis_index("x") + 2 * lax.axis_index("y") + lax.axis_index("c")
    early = _exchange("gather_early", [w_in.astype(BF16), glu_w.astype(BF16)], ['gather'] * 2)
    W_in = early[0].reshape(D, D)
    W_glu = early[1].reshape(DS, DS)
    late_src = [w_out.astype(BF16), w_down.astype(BF16), w_up.astype(BF16), conv_w]
    late_handle, token = _exchange_start("gather_late_start", late_src, ['gather'] * 4, after=early[0])
    CB = row(conv_b)

    g1, g2, g3 = row(norm_mix_g), row(norm_ffn_g), row(final_norm_g)
    gp, gs = row(out_norm_pool_g), row(out_norm_ssm_g)
    (xn,) = _rowmap("norm_mix", _rms_fwd, [xs], [g1 + token[0:1, 0:1]], [(D, BF16)])
    u = _mm("proj_in", xn, W_in, 'nn', F32)
    tmp = _pick(T, (256, 128))
    u_pad = jnp.pad(u[:, :DP], ((POOL_PAD, POOL_PAD), (0, 0)))
    pool_w_bf = pool_w.astype(BF16)
    ypn = _pool_fwd(u_pad, pool_w_bf, row(pool_scale), gp, T, tmp)

    u_ssm = _to_chunk_rows(u[:, DP:])
    u_ssm_bf = u_ssm.astype(BF16)
    a_rows = (ssm_log_neg_a_re.reshape(2 * G, N), ssm_a_im.reshape(2 * G, N), ssm_log_dt.reshape(2 * G, 1))
    disc = _ssm_params_fwd(*a_rows)
    nb = G // SSM_BLOCK_GROUPS
    vecs = [d.reshape(2, nb, 1, SSM_BLOCK_GROUPS * N) for d in disc]
    b_in = [_block_diag_in(b) for b in (ssm_b_re, ssm_b_im)]
    c_out = [_block_diag_out(c) for c in (ssm_c_re, ssm_c_im)]
    mats_bf = [[m[d].astype(BF16) for m in b_in + c_out] for d in range(2)]
    mats_t_bf = [[jnp.swapaxes(m[d], 1, 2).astype(BF16) for m in b_in + c_out] for d in range(2)]
    dir_vecs = [[v[d] for v in vecs] for d in range(2)]
    y_dir = [_ssm_fwd("ssm_fwd_%d" % d, u_ssm_bf, mats_bf[d], dir_vecs[d], reverse=bool(d)) for d in range(2)]

    def mix_post(yf, yb, us, d, gw, gb, g):
        y = yf + yb + d * us
        z = _gelu(y)
        gate = _sigmoid(jnp.dot(z.astype(BF16), gw, preferred_element_type=F32) + gb)
        return _rms_fwd(z * gate, g), y

    ysn, y_ssm = _rowmap("ssm_post", mix_post, [y_dir[0], y_dir[1], u_ssm], [row(ssm_d), W_glu, row(glu_b), gs],
                         [(DS, BF16), (DS, F32)])
    ycat = jnp.concatenate([ypn, _from_chunk_rows(ysn)], axis=1)
    late = [_with_own(l, s, me) for l, s in zip(_exchange_wait("gather_late_wait", late_handle, ysn), late_src)]
    W_out = late[0].reshape(D, D)
    W_down = late[1].reshape(F, D)
    W_up = late[2].transpose(1, 0, 2).reshape(D, F2)
    CW = late[3].transpose(1, 0, 2).reshape(3, F2)
    h1 = _mm("proj_out", ycat, W_out, 'nn', F32, add=xs)
    (hn,) = _rowmap("norm_ffn", _rms_fwd, [h1], [g2], [(D, BF16)])
    up = _mm("ffn_up", hn, W_up, 'nn', F32)
    act = _convact_fwd(up, CW, CB, F)
    h2 = _mm("ffn_down", act, W_down, 'nn', F32, add=h1)

    def head(h, t, g):
        r = lax.rsqrt(_rowmean(h * h) + EPS)
        hh = h * r
        e = hh * g - t
        loss = 0.5 * jnp.sum(_rowmean(e * e), keepdims=True)
        dy = e * (1.0 / D)
        dxh = dy * g
        dh = r * (dxh - hh * _rowmean(dxh * hh))
        return dh, dh, jnp.broadcast_to(loss, (1, LANES)), _colsum(dy * hh)

    dh2, dh2_bf, loss_acc, dg3 = _rowmap("loss_head", head, [h2, tgt], [g3], [(D, F32), (D, BF16)], [(1, LANES), (1, D)])

    dact = _mm("ffn_down_dx", dh2_bf, W_down, 'nt', F32)
    dW_down = _mm("ffn_down_dw", act, dh2_bf, 'tn', BF16)
    dup, dCW, dCB = _convact_bwd(up, dact, CW, CB, F)
    dhn = _mm("ffn_up_dx", dup, W_up, 'nt', F32)
    dW_up = _mm("ffn_up_dw", hn, dup, 'tn', BF16)

    def shards(full_grad):
        return full_grad.reshape((N_DEV, full_grad.shape[0] // N_DEV) + full_grad.shape[1:])

    ffn_sent = [shards(dW_down), dW_up.reshape(D, N_DEV, fs).transpose(1, 0, 2), dCW.reshape(3, N_DEV, fs).transpose(1, 0, 2)]
    ffn_handle, token = _exchange_start("scatter_ffn_start", ffn_sent, ['scatter'] * 3, after=dhn)
    g2_late = g2 + token[0:1, 0:1]

    def norm_bwd_add(hx, dy, res, g):
        dx, dg = _rms_bwd(hx, g, dy)
        return dx + res, dg

    def norm_bwd_add2(hx, dy, res, g):
        dx, dg = _rms_bwd(hx, g, dy)
        return dx + res, dx + res, dg

    dh1, dh1_bf, dg2 = _rowmap("norm_ffn_bwd", norm_bwd_add2, [h1, dhn, dh2], [g2_late], [(D, F32), (D, BF16)], [(1, D)])
    dycat = _mm("proj_out_dx", dh1_bf, W_out, 'nt', F32)
    dW_out = _mm("proj_out_dw", ycat, dh1_bf, 'tn', BF16)

    out_sent = [shards(dW_out)]
    out_handle, token = _exchange_start("scatter_out_start", out_sent, ['scatter'], after=dycat)
    q, dpooled, dpool_w, dpool_scale, dgp = _pool_bwd_rows(u_pad, dycat[:, :DP], pool_w_bf, row(pool_scale),
                                                           gp + token[0:1, 0:1], T, tmp)
    du_pool = _pool_bwd_band(jnp.pad(q, ((POOL_PAD, POOL_PAD), (0, 0))), dpooled, T, tmp)

    def mix_post_bwd(y, us, dyn, d, gw, gb, g):
        z = _gelu(y)
        gz = z.astype(BF16)
        sig = _sigmoid(jnp.dot(gz, gw, preferred_element_type=F32) + gb)
        dys, dg = _rms_bwd(z * sig, g, dyn)
        dgl = dys * z * sig * (1.0 - sig)
        dgl_bf = dgl.astype(BF16)
        dz = dys * sig + lax.dot_general(dgl_bf, gw, (((1,), (1,)), ((), ())), preferred_element_type=F32)
        dgw = lax.dot_general(gz, dgl_bf, (((0,), (0,)), ((), ())), preferred_element_type=F32)
        dyv = dz * _gelu_grad(y)
        return dyv, dyv * d, dgw, _colsum(dgl), _colsum(dyv * us), dg

    d_ysn = _to_chunk_rows(dycat[:, DP:])
    dyv, du_dir, dglu_w, dglu_b, dssm_d, dgs = _rowmap(
        "ssm_post_bwd", mix_post_bwd, [y_ssm, u_ssm, d_ysn], [row(ssm_d), W_glu, row(glu_b), gs],
        [(DS, BF16), (DS, F32)], [(DS, DS), (1, DS), (1, DS), (1, DS)])

    du_ssm = du_dir
    dB, dC, dvec = [], [], []
    for d in range(2):
        b_re, b_im, c_re, c_im = mats_bf[d]
        b_re_t, b_im_t, c_re_t, c_im_t = mats_t_bf[d]
        res = _ssm_bwd("ssm_bwd_%d" % d, u_ssm_bf, dyv, du_ssm, (b_re, b_im), (b_re_t, b_im_t, c_re_t, c_im_t),
                       dir_vecs[d], reverse=bool(d))
        du_ssm = res[0]
        dB.append(res[1:3])
        dC.append(res[3:5])
        dvec.append(res[5:9])
    grad_b = [_block_diag_in_grad(jnp.stack([dB[0][k], dB[1][k]]), N, H) for k in range(2)]
    grad_c = [_block_diag_out_grad(jnp.stack([dC[0][k], dC[1][k]]), N, H) for k in range(2)]
    cots = [jnp.stack([dvec[0][k], dvec[1][k]]).reshape(2 * G, N) for k in range(4)]
    d_a_re, d_a_im, d_log_dt = _ssm_params_bwd(*a_rows, cots)

    rep_grads = {
        'pool_w': dpool_w, 'pool_scale': dpool_scale, 'ssm_log_neg_a_re': d_a_re, 'ssm_a_im': d_a_im,
        'ssm_log_dt': d_log_dt, 'ssm_b_re': grad_b[0], 'ssm_b_im': grad_b[1], 'ssm_c_re': grad_c[0], 'ssm_c_im': grad_c[1],
        'ssm_d': dssm_d, 'glu_b': dglu_b, 'out_norm_pool_g': dgp, 'out_norm_ssm_g': dgs, 'norm_ffn_g': dg2,
        'conv_b': dCB, 'final_norm_g': dg3}
    wide = ('pool_w', 'ssm_b_re', 'ssm_b_im', 'ssm_c_re', 'ssm_c_im')

    def travel(n):
        g = rep_grads[n]
        if n in wide:
            return g.reshape(-1, PACK_W if g.size % PACK_W == 0 else LANES).astype(BF16)
        return _view2d(g.reshape(weights[n].shape))

    small = tuple(n for n in REPLICATED if n != 'norm_mix_g')
    small_sent = [shards(dglu_w.astype(BF16))] + [travel(n) for n in small] + [loss_acc]
    small_handle, token = _exchange_start("reduce_small_start", small_sent, ['scatter'] + ['gather'] * (len(small) + 1),
                                          after=d_a_re)

    du = (jnp.concatenate([du_pool, _from_chunk_rows(du_ssm)], axis=1) + token[0:1, 0:1]).astype(BF16)
    dxn = _mm("proj_in_dx", du, W_in, 'nt', F32)
    dW_in = _mm("proj_in_dw", xn, du, 'tn', BF16)
    dx, dg1 = _rowmap("norm_mix_bwd", norm_bwd_add, [xs, dxn, dh1], [g1], [(D, F32)], [(1, D)])
    rep_grads['norm_mix_g'] = dg1
    last_sent = [shards(dW_in), travel('norm_mix_g')]
    last_handle, token = _exchange_start("reduce_last_start", last_sent, ['scatter', 'gather'], after=dx)

    grads, delta, new_m, new_v = {}, {}, {}, {}

    def update(n, landed_n):
        shape = weights[n].shape
        landed_n = landed_n.reshape((N_DEV,) + _view2d(weights[n]).shape)
        res = _sum_adamw("adamw_" + n, landed_n, _view2d(weights[n]), _view2d(mom1[n]), _view2d(mom2[n]))
        grads[n], delta[n], new_m[n], new_v[n] = [r.reshape(shape) for r in res]

    def own_shard(sent_n):
        return lax.dynamic_index_in_dim(sent_n, me, 0, keepdims=False)

    ffn_landed = _exchange_wait("scatter_ffn_wait", ffn_handle, token)
    for n, l, s in zip(('w_down', 'w_up', 'conv_w'), ffn_landed, ffn_sent):
        update(n, _with_own(l, own_shard(s), me))
    (out_landed,) = _exchange_wait("scatter_out_wait", out_handle, token)
    update('w_out', _with_own(out_landed, own_shard(out_sent[0]), me))
    small_landed = _exchange_wait("reduce_small_wait", small_handle, token)
    update('glu_w', _with_own(small_landed[0], own_shard(small_sent[0]), me))
    for n, l, s in zip(small, small_landed[1:], small_sent[1:]):
        update(n, _with_own(l, s, me))

    def loss_sum(parts_ref, o_ref):
        s = parts_ref[0]
        for q in range(1, N_DEV):
            s = s + parts_ref[q]
        o_ref[...] = s

    loss_parts = _with_own(small_landed[-1], small_sent[-1], me)
    loss = pl.pallas_call(loss_sum, name="loss_sum", out_shape=jax.ShapeDtypeStruct((1, LANES), F32))(loss_parts)[0, 0]

    last_landed = _exchange_wait("reduce_last_wait", last_handle, grads[small[-1]])
    update('w_in', _with_own(last_landed[0], own_shard(last_sent[0]), me))
    update('norm_mix_g', _with_own(last_landed[1], last_sent[1], me))

    return (loss, dx[None], *[grads[n] for n in WEIGHTS], *[delta[n] for n in WEIGHTS],
            *[new_m[n] for n in WEIGHTS], *[new_v[n] for n in WEIGHTS])
```

```python
import functools

import jax
import jax.numpy as jnp
import numpy as np
from jax import lax
from jax.experimental import pallas as pl
from jax.experimental.pallas import tpu as pltpu

F32 = jnp.float32
BF16 = jnp.bfloat16
MESH_ID = pl.DeviceIdType.MESH

N_DEV = 8
EPS = 1e-6
POOL_WINDOWS = (2, 4, 8, 16)
POOL_GROUP = 128
POOL_PAD = 64
SSM_GROUP = 16
SSM_STATE = 64
SSM_BLOCK_GROUPS = 8
N_CHUNK = 8
LANES = 128
PACK_W = 1024
VMEM_LIMIT = 56 * 1024 * 1024
MM_VMEM_BUDGET = 40 * 1024 * 1024
MM_TILE_CAP = 1408
ADAM_TILE_BYTES = 4 * 1024 * 1024

ADAM_LR = 0.001
ADAM_B1 = 0.9
ADAM_B2 = 0.999
ADAM_EPS = 1e-08
ADAM_WD = 0.01
ADAM_STEP = 10

WEIGHTS = ['norm_mix_g', 'w_in', 'pool_w', 'pool_scale', 'ssm_log_neg_a_re', 'ssm_a_im', 'ssm_log_dt',
           'ssm_b_re', 'ssm_b_im', 'ssm_c_re', 'ssm_c_im', 'ssm_d', 'glu_w', 'glu_b', 'out_norm_pool_g',
           'out_norm_ssm_g', 'w_out', 'norm_ffn_g', 'w_up', 'conv_w', 'conv_b', 'w_down', 'final_norm_g']
SHARDED = ('w_in', 'w_out', 'w_down', 'glu_w', 'w_up', 'conv_w')
REPLICATED = tuple(n for n in WEIGHTS if n not in SHARDED)


def _pick(n, prefs):
    for p in prefs:
        if n % p == 0:
            return p
    return n


def _params(sem, vmem=None):
    return pltpu.CompilerParams(dimension_semantics=sem, vmem_limit_bytes=vmem or VMEM_LIMIT)


def _tiles(n, cap):
    return [d for d in range(LANES, min(n, cap) + 1, LANES) if n % d == 0] or [n]


def _mm(name, a, b, mode, out_dtype, add=None):
    if mode == 'nn':
        (M, K), (_, N) = a.shape, b.shape
    elif mode == 'nt':
        (M, K), (N, _) = a.shape, b.shape
    else:
        (K, M), (_, N) = a.shape, b.shape
    dims = {'nn': (((1,), (0,)), ((), ())), 'nt': (((1,), (1,)), ((), ())), 'tn': (((0,), (0,)), ((), ()))}[mode]
    sa, sb, so = a.dtype.itemsize, b.dtype.itemsize, jnp.dtype(out_dtype).itemsize
    best = None
    for tm in _tiles(M, MM_TILE_CAP):
        for tn in _tiles(N, MM_TILE_CAP):
            need = 2 * (tm * K * sa + tn * K * sb + tm * tn * (so + (4 if add is not None else 0)))
            if need <= MM_VMEM_BUDGET:
                key = (tm * tn / (tm + tn), tm * tn)
                if best is None or key > best[0]:
                    best = (key, tm, tn)
    _, tm, tn = best
    rows_inner = a.size * sa * (N // tn) + b.size * sb < a.size * sa + b.size * sb * (M // tm)

    def kern(*refs):
        a_ref, b_ref = refs[:2]
        o_ref = refs[-1]
        r = lax.dot_general(a_ref[...].astype(BF16), b_ref[...].astype(BF16), dims, preferred_element_type=F32)
        if add is not None:
            r = r + refs[2][...]
        o_ref[...] = r.astype(o_ref.dtype)

    if rows_inner:
        grid = (N // tn, M // tm)
        ij = lambda g0, g1: (g1, g0)
    else:
        grid = (M // tm, N // tn)
        ij = lambda g0, g1: (g0, g1)
    a_spec = (pl.BlockSpec((K, tm), lambda g0, g1: (0, ij(g0, g1)[0])) if mode == 'tn'
              else pl.BlockSpec((tm, K), lambda g0, g1: (ij(g0, g1)[0], 0)))
    b_spec = (pl.BlockSpec((tn, K), lambda g0, g1: (ij(g0, g1)[1], 0)) if mode == 'nt'
              else pl.BlockSpec((K, tn), lambda g0, g1: (0, ij(g0, g1)[1])))
    in_specs = [a_spec, b_spec]
    args = [a, b]
    if add is not None:
        in_specs.append(pl.BlockSpec((tm, tn), lambda g0, g1: ij(g0, g1)))
        args.append(add)
    return pl.pallas_call(
        kern, name=name, grid=grid, in_specs=in_specs,
        out_specs=pl.BlockSpec((tm, tn), lambda g0, g1: ij(g0, g1)),
        out_shape=jax.ShapeDtypeStruct((M, N), out_dtype),
        compiler_params=_params(("parallel", "parallel")),
    )(*args)


def _rowmap(name, fn, rows, fulls, out_rows, out_accs=(), tm=256):
    T = rows[0].shape[0]
    tm = min(tm, T)
    assert T % tm == 0
    n_in, n_row = len(rows) + len(fulls), len(out_rows)

    def kern(*refs):
        i = pl.program_id(0)
        res = fn(*[r[...] for r in refs[:n_in]])
        res = res if isinstance(res, (tuple, list)) else (res,)
        outs = refs[n_in:]
        for o, v in zip(outs[:n_row], res[:n_row]):
            o[...] = v.astype(o.dtype)
        for o, v in zip(outs[n_row:], res[n_row:]):
            @pl.when(i == 0)
            def _(o=o, v=v):
                o[...] = v

            @pl.when(i != 0)
            def _(o=o, v=v):
                o[...] += v

    def full_spec(shape):
        nd = len(shape)
        return pl.BlockSpec(tuple(shape), lambda i: (0,) * nd)

    in_specs = [pl.BlockSpec((tm, r.shape[1]), lambda i: (i, 0)) for r in rows] + [full_spec(f.shape) for f in fulls]
    out_specs = [pl.BlockSpec((tm, c), lambda i: (i, 0)) for c, _ in out_rows] + [full_spec(s) for s in out_accs]
    out_shape = [jax.ShapeDtypeStruct((T, c), dt) for c, dt in out_rows] + [jax.ShapeDtypeStruct(tuple(s), F32) for s in out_accs]
    return pl.pallas_call(
        kern, name=name, grid=(T // tm,), in_specs=in_specs, out_specs=out_specs, out_shape=out_shape,
        compiler_params=_params(("arbitrary",)),
    )(*rows, *fulls)


def _colsum(v):
    return jnp.sum(v, axis=0, keepdims=True)


def _rowmean(v):
    return jnp.mean(v, axis=-1, keepdims=True)


def _rms_fwd(x, g):
    r = lax.rsqrt(_rowmean(x * x) + EPS)
    return x * r * g


def _rms_bwd(x, g, dy):
    r = lax.rsqrt(_rowmean(x * x) + EPS)
    xh = x * r
    dxh = dy * g
    return r * (dxh - xh * _rowmean(dxh * xh)), _colsum(dy * xh)


def _sigmoid(v):
    return 0.5 * jnp.tanh(0.5 * v) + 0.5


def _gelu(y):
    c = np.sqrt(2.0 / np.pi).astype(np.float32)
    return 0.5 * y * (1.0 + jnp.tanh(c * (y + 0.044715 * (y * y * y))))


def _gelu_grad(y):
    c = np.sqrt(2.0 / np.pi).astype(np.float32)
    th = jnp.tanh(c * (y + 0.044715 * (y * y * y)))
    return 0.5 * (1.0 + th) + 0.5 * y * (1.0 - th * th) * c * (1.0 + 3.0 * 0.044715 * (y * y))


def _split3(v):
    hi = v.astype(BF16)
    r1 = v - hi.astype(F32)
    mid = r1.astype(BF16)
    lo = (r1 - mid.astype(F32)).astype(BF16)
    return hi, mid, lo


def _band(tm, lo_off, hi_off):
    shape = (tm, tm + 2 * POOL_PAD)
    d = lax.broadcasted_iota(jnp.int32, shape, 1) - lax.broadcasted_iota(jnp.int32, shape, 0) - POOL_PAD
    return ((d >= lo_off) & (d < hi_off)).astype(BF16)


def _band_sum(band, ext):
    hi, mid, lo = _split3(ext)
    dot = functools.partial(jnp.dot, preferred_element_type=F32)
    return dot(band, hi) + dot(band, mid) + dot(band, lo)


def _window_count(r0, tm, half, T):
    t = r0 + lax.broadcasted_iota(jnp.int32, (tm, 1), 0)
    return (jnp.minimum(t + half, T) - jnp.maximum(t - half, 0)).astype(F32)


def _pool_block(u_ref, w_ref, r0, tm, T):
    ext = u_ref[pl.ds(r0, tm + 2 * POOL_PAD), :]
    ctr = u_ref[pl.ds(r0 + POOL_PAD, tm), :]
    pooled, conc = [], []
    for gi, w in enumerate(POOL_WINDOWS):
        half = w // 2
        cols = slice(gi * POOL_GROUP, (gi + 1) * POOL_GROUP)
        ws = _band_sum(_band(tm, -half, half), ext[:, cols])
        p = ws / _window_count(r0, tm, half, T) - ctr[:, cols]
        pooled.append(p)
        conc.append(jnp.dot(p.astype(BF16), w_ref[gi], preferred_element_type=F32))
    return pooled, jnp.concatenate(conc, axis=1)


def _pool_fwd(u_pad, pool_w_bf, pool_scale, g_pool, T, tm):
    nw = len(POOL_WINDOWS)
    C = nw * POOL_GROUP

    def kern(u_ref, w_ref, sc_ref, g_ref, o_ref):
        r0 = pl.multiple_of(pl.program_id(0) * tm, tm)
        _, conc = _pool_block(u_ref, w_ref, r0, tm, T)
        o_ref[...] = _rms_fwd(conc * sc_ref[...], g_ref[...]).astype(o_ref.dtype)

    return pl.pallas_call(
        kern, name="pool_fwd", grid=(T // tm,),
        in_specs=[pl.BlockSpec(u_pad.shape, lambda i: (0, 0)), pl.BlockSpec(pool_w_bf.shape, lambda i: (0, 0, 0)),
                  pl.BlockSpec((1, C), lambda i: (0, 0)), pl.BlockSpec((1, C), lambda i: (0, 0))],
        out_specs=pl.BlockSpec((tm, C), lambda i: (i, 0)),
        out_shape=jax.ShapeDtypeStruct((T, C), BF16),
        compiler_params=_params(("arbitrary",)),
    )(u_pad, pool_w_bf, pool_scale, g_pool)


def _pool_bwd_rows(u_pad, d_ypn, pool_w_bf, pool_scale, g_pool, T, tm):
    nw = len(POOL_WINDOWS)
    C = nw * POOL_GROUP

    def kern(u_ref, dy_ref, w_ref, sc_ref, g_ref, q_ref, dp_ref, dw_ref, dsc_ref, dg_ref):
        i = pl.program_id(0)
        r0 = pl.multiple_of(i * tm, tm)
        pooled, conc = _pool_block(u_ref, w_ref, r0, tm, T)
        sc = sc_ref[...]
        dyp, dg = _rms_bwd(conc * sc, g_ref[...], dy_ref[...])
        dsc = _colsum(dyp * conc)
        dconc = (dyp * sc).astype(BF16)
        dws = []
        for gi, w in enumerate(POOL_WINDOWS):
            cols = slice(gi * POOL_GROUP, (gi + 1) * POOL_GROUP)
            dc = dconc[:, cols]
            dp = lax.dot_general(dc, w_ref[gi], (((1,), (1,)), ((), ())), preferred_element_type=F32)
            dp_ref[:, cols] = dp
            q_ref[:, cols] = dp / _window_count(r0, tm, w // 2, T)
            dws.append(lax.dot_general(pooled[gi].astype(BF16), dc, (((0,), (0,)), ((), ())), preferred_element_type=F32))

        @pl.when(i == 0)
        def _():
            for gi in range(nw):
                dw_ref[gi] = dws[gi]
            dsc_ref[...] = dsc
            dg_ref[...] = dg

        @pl.when(i != 0)
        def _():
            for gi in range(nw):
                dw_ref[gi] += dws[gi]
            dsc_ref[...] += dsc
            dg_ref[...] += dg

    full2 = pl.BlockSpec((1, C), lambda i: (0, 0))
    return pl.pallas_call(
        kern, name="pool_bwd_rows", grid=(T // tm,),
        in_specs=[pl.BlockSpec(u_pad.shape, lambda i: (0, 0)), pl.BlockSpec((tm, C), lambda i: (i, 0)),
                  pl.BlockSpec(pool_w_bf.shape, lambda i: (0, 0, 0)), full2, full2],
        out_specs=[pl.BlockSpec((tm, C), lambda i: (i, 0)), pl.BlockSpec((tm, C), lambda i: (i, 0)),
                   pl.BlockSpec((nw, POOL_GROUP, POOL_GROUP), lambda i: (0, 0, 0)), full2, full2],
        out_shape=[jax.ShapeDtypeStruct((T, C), F32), jax.ShapeDtypeStruct((T, C), F32),
                   jax.ShapeDtypeStruct((nw, POOL_GROUP, POOL_GROUP), F32),
                   jax.ShapeDtypeStruct((1, C), F32), jax.ShapeDtypeStruct((1, C), F32)],
        compiler_params=_params(("arbitrary",)),
    )(u_pad, d_ypn, pool_w_bf, pool_scale, g_pool)


def _pool_bwd_band(q_pad, dpooled, T, tm):
    C = dpooled.shape[1]

    def kern(q_ref, dp_ref, o_ref):
        r0 = pl.multiple_of(pl.program_id(0) * tm, tm)
        ext = q_ref[pl.ds(r0, tm + 2 * POOL_PAD), :]
        for gi, w in enumerate(POOL_WINDOWS):
            half = w // 2
            cols = slice(gi * POOL_GROUP, (gi + 1) * POOL_GROUP)
            o_ref[:, cols] = _band_sum(_band(tm, -half + 1, half + 1), ext[:, cols]) - dp_ref[:, cols]

    return pl.pallas_call(
        kern, name="pool_bwd_band", grid=(T // tm,),
        in_specs=[pl.BlockSpec(q_pad.shape, lambda i: (0, 0)), pl.BlockSpec((tm, C), lambda i: (i, 0))],
        out_specs=pl.BlockSpec((tm, C), lambda i: (i, 0)),
        out_shape=jax.ShapeDtypeStruct((T, C), F32),
        compiler_params=_params(("arbitrary",)),
    )(q_pad, dpooled)


def _cmul(ar, ai, br, bi):
    return ar * br - ai * bi, ar * bi + ai * br


def _ssm_discretise(log_neg_a_re, a_im, log_dt):
    dt = jnp.exp(log_dt)
    a_re = -jnp.exp(log_neg_a_re)
    mag = jnp.exp(a_re * dt)
    ang = a_im * dt
    lam_re, lam_im = mag * jnp.cos(ang), mag * jnp.sin(ang)
    den = a_re * a_re + a_im * a_im
    f_re = ((lam_re - 1.0) * a_re + lam_im * a_im) / den
    f_im = (lam_im * a_re - (lam_re - 1.0) * a_im) / den
    return lam_re, lam_im, f_re, f_im


def _ssm_params_fwd(log_neg_a_re, a_im, log_dt):
    rows, n = log_neg_a_re.shape

    def kern(a_ref, b_ref, c_ref, o1, o2, o3, o4):
        for o, v in zip((o1, o2, o3, o4), _ssm_discretise(a_ref[...], b_ref[...], c_ref[...])):
            o[...] = v

    return pl.pallas_call(kern, name="ssm_params_fwd", out_shape=[jax.ShapeDtypeStruct((rows, n), F32)] * 4)(
        log_neg_a_re, a_im, log_dt)


def _ssm_params_bwd(log_neg_a_re, a_im, log_dt, cots):
    rows, n = log_neg_a_re.shape

    def kern(a_ref, b_ref, c_ref, g1, g2, g3, g4, o1, o2, o3):
        _, vjp = jax.vjp(_ssm_discretise, a_ref[...], b_ref[...], c_ref[...])
        d1, d2, d3 = vjp((g1[...], g2[...], g3[...], g4[...]))
        o1[...] = d1
        o2[...] = d2
        o3[...] = d3

    return pl.pallas_call(
        kern, name="ssm_params_bwd",
        out_shape=[jax.ShapeDtypeStruct((rows, n), F32), jax.ShapeDtypeStruct((rows, n), F32),
                   jax.ShapeDtypeStruct((rows, 1), F32)])(log_neg_a_re, a_im, log_dt, *cots)


def _cpow(lr, li, n):
    out = None
    br, bi = lr, li
    while n:
        if n & 1:
            out = (br, bi) if out is None else _cmul(out[0], out[1], br, bi)
        n >>= 1
        if n:
            br, bi = _cmul(br, bi, br, bi)
    return out


def _slab(t):
    return pl.ds(pl.multiple_of(t * N_CHUNK, N_CHUNK), N_CHUNK)


def _shift_chunks(vr, vi, reverse):
    sub = lax.broadcasted_iota(jnp.int32, vr.shape, 0)
    if reverse:
        keep = sub != N_CHUNK - 1
        return jnp.where(keep, pltpu.roll(vr, N_CHUNK - 1, 0), 0.0), jnp.where(keep, pltpu.roll(vi, N_CHUNK - 1, 0), 0.0)
    keep = sub != 0
    return jnp.where(keep, pltpu.roll(vr, 1, 0), 0.0), jnp.where(keep, pltpu.roll(vi, 1, 0), 0.0)


def _chunk_scan(xr, xi, lr, li, Lc, reverse, prev=None):
    C = xr.shape[1]
    lrb, lib = jnp.broadcast_to(lr, (N_CHUNK, C)), jnp.broadcast_to(li, (N_CHUNK, C))
    zero = jnp.zeros((N_CHUNK, C), F32)

    def step_of(k):
        return (Lc - 1 - k) if reverse else k

    def advance(sr, si, t):
        return lrb * sr - lib * si + xr[_slab(t), :], lrb * si + lib * sr + xi[_slab(t), :]

    def pass1(k, c):
        return advance(c[0], c[1], step_of(k))

    er, ei = lax.fori_loop(0, Lc, pass1, (zero, zero), unroll=4)
    pr, pi = _cpow(lr, li, Lc)
    prb, pib = jnp.broadcast_to(pr, (N_CHUNK, C)), jnp.broadcast_to(pi, (N_CHUNK, C))
    cr, ci = zero, zero
    for _ in range(N_CHUNK - 1):
        cr, ci = _shift_chunks(er + prb * cr - pib * ci, ei + prb * ci + pib * cr, reverse)

    if prev is None:
        def pass2(k, c):
            t = step_of(k)
            nr, ni = advance(c[0], c[1], t)
            xr[_slab(t), :] = nr
            xi[_slab(t), :] = ni
            return nr, ni

        lax.fori_loop(0, Lc, pass2, (cr, ci), unroll=4)
        return None

    qr, qi = prev
    def pair(ar, ai, wr, wi, acc):
        return acc[0] + ar * wr + ai * wi, acc[1] + ai * wr - ar * wi

    def pass2(k, c):
        sr, si, accr, acci = c
        t = step_of(k)
        nr, ni = advance(sr, si, t)
        xr[_slab(t), :] = nr
        xi[_slab(t), :] = ni
        tp = t - 1 if reverse else t + 1
        accr, acci = pair(nr, ni, qr[_slab(tp), :], qi[_slab(tp), :], (accr, acci))
        return nr, ni, accr, acci

    sr, si, accr, acci = lax.fori_loop(0, Lc - 1, pass2, (cr, ci, zero, zero), unroll=4)
    t = step_of(Lc - 1)
    nr, ni = advance(sr, si, t)
    xr[_slab(t), :] = nr
    xi[_slab(t), :] = ni
    edge = step_of(0)
    wr, wi = _shift_chunks(qr[_slab(edge), :], qi[_slab(edge), :], not reverse)
    return pair(nr, ni, wr, wi, (accr, acci))


def _ssm_dims(T):
    assert T % (N_CHUNK * 64) == 0
    rc = _pick(T, (512, 256, 128, 64))
    return T // N_CHUNK, rc, T // rc


def _ssm_project_in(u_ref, bre_ref, bim_ref, xr, xi, rc, nrc):
    for c in range(nrc):
        rows = pl.ds(c * rc, rc)
        ub = u_ref[rows, :]
        xr[rows, :] = jnp.dot(ub, bre_ref[...], preferred_element_type=F32)
        xi[rows, :] = jnp.dot(ub, bim_ref[...], preferred_element_type=F32)


def _ssm_fwd(name, u_bf, mats, vecs, reverse):
    T, W = u_bf.shape
    nb = W // LANES
    SC = SSM_BLOCK_GROUPS * SSM_STATE
    Lc, rc, nrc = _ssm_dims(T)
    bre, bim, cre, cim = mats
    lam_re, lam_im, f_re, f_im = vecs

    def kern(u_ref, bre_ref, bim_ref, cre_ref, cim_ref, lr_ref, li_ref, fr_ref, fi_ref, y_ref, xr, xi):
        _ssm_project_in(u_ref, bre_ref, bim_ref, xr, xi, rc, nrc)
        _chunk_scan(xr, xi, lr_ref[...], li_ref[...], Lc, reverse)
        fr, fi = fr_ref[...], fi_ref[...]
        for c in range(nrc):
            rows = pl.ds(c * rc, rc)
            sr, si = _cmul(fr, fi, xr[rows, :], xi[rows, :])
            y_ref[rows, :] = (jnp.dot(sr.astype(BF16), cre_ref[...], preferred_element_type=F32)
                              - jnp.dot(si.astype(BF16), cim_ref[...], preferred_element_type=F32))

    col = pl.BlockSpec((T, LANES), lambda m: (0, m))
    mat_in = pl.BlockSpec((None, LANES, SC), lambda m: (m, 0, 0))
    mat_out = pl.BlockSpec((None, SC, LANES), lambda m: (m, 0, 0))
    vec = pl.BlockSpec((None, 1, SC), lambda m: (m, 0, 0))
    return pl.pallas_call(
        kern, name=name, grid=(nb,),
        in_specs=[col, mat_in, mat_in, mat_out, mat_out, vec, vec, vec, vec],
        out_specs=col, out_shape=jax.ShapeDtypeStruct((T, W), F32),
        scratch_shapes=[pltpu.VMEM((T, SC), F32), pltpu.VMEM((T, SC), F32)],
        compiler_params=_params(("arbitrary",)),
    )(u_bf, bre, bim, cre, cim, lam_re, lam_im, f_re, f_im)


def _ssm_bwd(name, u_bf, dy_bf, du_in, mats, mats_t, vecs, reverse):
    T, W = u_bf.shape
    nb = W // LANES
    SC = SSM_BLOCK_GROUPS * SSM_STATE
    Lc, rc, nrc = _ssm_dims(T)
    bre, bim = mats
    bre_t, bim_t, cre_t, cim_t = mats_t
    lam_re, lam_im, f_re, f_im = vecs
    contract_rows = (((0,), (0,)), ((), ()))

    def kern(u_ref, dy_ref, dui_ref, bre_ref, bim_ref, bret_ref, bimt_ref, cret_ref, cimt_ref,
             lr_ref, li_ref, fr_ref, fi_ref,
             du_ref, dbre_ref, dbim_ref, dcre_ref, dcim_ref, dlr_ref, dli_ref, dfr_ref, dfi_ref,
             xr, xi, gr, gi):
        lr, li, fr, fi = lr_ref[...], li_ref[...], fr_ref[...], fi_ref[...]
        _ssm_project_in(u_ref, bre_ref, bim_ref, xr, xi, rc, nrc)
        _chunk_scan(xr, xi, lr, li, Lc, reverse)
        mre = jnp.zeros((LANES, SC), F32)
        mim = jnp.zeros((LANES, SC), F32)
        dfr = jnp.zeros((1, SC), F32)
        dfi = jnp.zeros((1, SC), F32)
        for c in range(nrc):
            rows = pl.ds(c * rc, rc)
            dy = dy_ref[rows, :]
            x_r, x_i = xr[rows, :], xi[rows, :]
            g_r = jnp.dot(dy, cret_ref[...], preferred_element_type=F32)
            g_i = -jnp.dot(dy, cimt_ref[...], preferred_element_type=F32)
            mre += lax.dot_general(dy, x_r.astype(BF16), contract_rows, preferred_element_type=F32)
            mim += lax.dot_general(dy, x_i.astype(BF16), contract_rows, preferred_element_type=F32)
            dfr += _colsum(g_r * x_r + g_i * x_i)
            dfi += _colsum(g_i * x_r - g_r * x_i)
            gr[rows, :] = fr * g_r + fi * g_i
            gi[rows, :] = fr * g_i - fi * g_r
        dcre_ref[...] = fr * mre - fi * mim
        dcim_ref[...] = -(fr * mim + fi * mre)
        dfr_ref[...] = dfr
        dfi_ref[...] = dfi
        accr, acci = _chunk_scan(gr, gi, lr, -li, Lc, not reverse, prev=(xr, xi))
        dlr_ref[...] = _colsum(accr)
        dli_ref[...] = _colsum(acci)
        dbre = jnp.zeros((LANES, SC), F32)
        dbim = jnp.zeros((LANES, SC), F32)
        for c in range(nrc):
            rows = pl.ds(c * rc, rc)
            ub = u_ref[rows, :]
            a_r, a_i = gr[rows, :].astype(BF16), gi[rows, :].astype(BF16)
            dbre += lax.dot_general(ub, a_r, contract_rows, preferred_element_type=F32)
            dbim += lax.dot_general(ub, a_i, contract_rows, preferred_element_type=F32)
            du_ref[rows, :] = (dui_ref[rows, :] + jnp.dot(a_r, bret_ref[...], preferred_element_type=F32)
                               + jnp.dot(a_i, bimt_ref[...], preferred_element_type=F32))
        dbre_ref[...] = dbre
        dbim_ref[...] = dbim

    col = pl.BlockSpec((T, LANES), lambda m: (0, m))
    mat_in = pl.BlockSpec((None, LANES, SC), lambda m: (m, 0, 0))
    mat_out = pl.BlockSpec((None, SC, LANES), lambda m: (m, 0, 0))
    vec = pl.BlockSpec((None, 1, SC), lambda m: (m, 0, 0))
    mat_shape = jax.ShapeDtypeStruct((nb, LANES, SC), F32)
    vec_shape = jax.ShapeDtypeStruct((nb, 1, SC), F32)
    return pl.pallas_call(
        kern, name=name, grid=(nb,),
        in_specs=[col, col, col, mat_in, mat_in, mat_out, mat_out, mat_in, mat_in, vec, vec, vec, vec],
        out_specs=[col, mat_in, mat_in, mat_in, mat_in, vec, vec, vec, vec],
        out_shape=[jax.ShapeDtypeStruct((T, W), F32), mat_shape, mat_shape, mat_shape, mat_shape,
                   vec_shape, vec_shape, vec_shape, vec_shape],
        scratch_shapes=[pltpu.VMEM((T, SC), F32)] * 4,
        compiler_params=_params(("arbitrary",)),
    )(u_bf, dy_bf, du_in, bre, bim, bre_t, bim_t, cre_t, cim_t, lam_re, lam_im, f_re, f_im)


def _block_diag_in(b):
    d, G, N, H = b.shape
    nb = G // SSM_BLOCK_GROUPS
    eye = jnp.eye(SSM_BLOCK_GROUPS, dtype=b.dtype)
    v = b.reshape(d, nb, SSM_BLOCK_GROUPS, N, H)
    return jnp.einsum('dmgnh,gk->dmghkn', v, eye).reshape(d, nb, SSM_BLOCK_GROUPS * H, SSM_BLOCK_GROUPS * N)


def _block_diag_in_grad(m, N, H):
    d, nb = m.shape[:2]
    eye = jnp.eye(SSM_BLOCK_GROUPS, dtype=m.dtype)
    v = m.reshape(d, nb, SSM_BLOCK_GROUPS, H, SSM_BLOCK_GROUPS, N)
    return jnp.einsum('dmghkn,gk->dmgnh', v, eye).reshape(d, nb * SSM_BLOCK_GROUPS, N, H)


def _block_diag_out(c):
    return jnp.swapaxes(_block_diag_in(jnp.swapaxes(c, 2, 3)), 2, 3)


def _block_diag_out_grad(m, N, H):
    return jnp.swapaxes(_block_diag_in_grad(m, N, H), 2, 3)


def _to_chunk_rows(a):
    T, C = a.shape
    return a.reshape(N_CHUNK, T // N_CHUNK, C).transpose(1, 0, 2).reshape(T, C)


def _from_chunk_rows(a):
    T, C = a.shape
    return a.reshape(T // N_CHUNK, N_CHUNK, C).transpose(1, 0, 2).reshape(T, C)


def _conv_taps(ref, c, R, T):
    n = T // R
    r0 = pl.multiple_of(c * R, R)
    main = ref[pl.ds(r0, R), :]
    before = ref[pl.ds(pl.multiple_of(jnp.maximum(r0 - 8, 0), 8), 8), :]
    after = ref[pl.ds(pl.multiple_of(jnp.minimum(r0 + R, T - 8), 8), 8), :]
    ext = jnp.concatenate([jnp.where(c > 0, before, 0.0), main, jnp.where(c < n - 1, after, 0.0)], axis=0)
    m1 = pltpu.roll(ext, 1, 0)[8:8 + R]
    p1 = pltpu.roll(ext, R + 15, 0)[8:8 + R]
    return m1, main, p1, r0


def _conv_specs(T, F, cb):
    nj = F // cb
    lo = lambda rows: pl.BlockSpec((rows, cb), lambda j: (0, j))
    hi = lambda rows: pl.BlockSpec((rows, cb), lambda j: (0, j + nj))
    return nj, lo, hi


def _convact_fwd(up, conv_w, conv_b, F):
    T = up.shape[0]
    cb = _pick(F, (256, 128))
    R = _pick(T, (256, 128, 64))
    nj, lo, hi = _conv_specs(T, F, cb)

    def kern(uv, ug, wv, wg, bv, bg, o_ref):
        def body(c, carry):
            v1, v0, v2, r0 = _conv_taps(uv, c, R, T)
            g1, g0, g2, _ = _conv_taps(ug, c, R, T)
            val = v1 * wv[0:1, :] + v0 * wv[1:2, :] + v2 * wv[2:3, :] + bv[...]
            gate = g1 * wg[0:1, :] + g0 * wg[1:2, :] + g2 * wg[2:3, :] + bg[...]
            o_ref[pl.ds(r0, R), :] = (val * (gate * _sigmoid(gate))).astype(o_ref.dtype)
            return carry

        lax.fori_loop(0, T // R, body, 0)

    return pl.pallas_call(
        kern, name="convact_fwd", grid=(nj,),
        in_specs=[lo(T), hi(T), lo(3), hi(3), lo(1), hi(1)],
        out_specs=lo(T), out_shape=jax.ShapeDtypeStruct((T, F), BF16),
        compiler_params=_params(("arbitrary",)),
    )(up, up, conv_w, conv_w, conv_b, conv_b)


def _convact_bwd(up, dact, conv_w, conv_b, F):
    T = up.shape[0]
    cb = _pick(F, (256, 128))
    R = _pick(T, (256, 128, 64))
    nj, lo, hi = _conv_specs(T, F, cb)

    def kern(uv, ug, da, wv, wg, bv, bg, dupv, dupg, dwv, dwg, dbv, dbg, sv, sg):
        zero = jnp.zeros((1, cb), F32)

        def pass_a(c, acc):
            v1, v0, v2, r0 = _conv_taps(uv, c, R, T)
            g1, g0, g2, _ = _conv_taps(ug, c, R, T)
            val = v1 * wv[0:1, :] + v0 * wv[1:2, :] + v2 * wv[2:3, :] + bv[...]
            gate = g1 * wg[0:1, :] + g0 * wg[1:2, :] + g2 * wg[2:3, :] + bg[...]
            sig = _sigmoid(gate)
            d = da[pl.ds(r0, R), :]
            dval = d * (gate * sig)
            dgate = d * val * (sig * (1.0 + gate * (1.0 - sig)))
            sv[pl.ds(r0, R), :] = dval
            sg[pl.ds(r0, R), :] = dgate
            terms = (dval * v1, dval * v0, dval * v2, dval, dgate * g1, dgate * g0, dgate * g2, dgate)
            return tuple(a + _colsum(t) for a, t in zip(acc, terms))

        acc = lax.fori_loop(0, T // R, pass_a, (zero,) * 8)
        for k in range(3):
            dwv[k:k + 1, :] = acc[k]
            dwg[k:k + 1, :] = acc[4 + k]
        dbv[...] = acc[3]
        dbg[...] = acc[7]

        def pass_b(c, carry):
            v1, v0, v2, r0 = _conv_taps(sv, c, R, T)
            g1, g0, g2, _ = _conv_taps(sg, c, R, T)
            dupv[pl.ds(r0, R), :] = (v2 * wv[0:1, :] + v0 * wv[1:2, :] + v1 * wv[2:3, :]).astype(dupv.dtype)
            dupg[pl.ds(r0, R), :] = (g2 * wg[0:1, :] + g0 * wg[1:2, :] + g1 * wg[2:3, :]).astype(dupg.dtype)
            return carry

        lax.fori_loop(0, T // R, pass_b, 0)

    dupv, dupg, dwv, dwg, dbv, dbg = pl.pallas_call(
        kern, name="convact_bwd", grid=(nj,),
        in_specs=[lo(T), hi(T), lo(T), lo(3), hi(3), lo(1), hi(1)],
        out_specs=[lo(T), lo(T), lo(3), lo(3), lo(1), lo(1)],
        out_shape=[jax.ShapeDtypeStruct((T, F), BF16), jax.ShapeDtypeStruct((T, F), BF16),
                   jax.ShapeDtypeStruct((3, F), F32), jax.ShapeDtypeStruct((3, F), F32),
                   jax.ShapeDtypeStruct((1, F), F32), jax.ShapeDtypeStruct((1, F), F32)],
        scratch_shapes=[pltpu.VMEM((T, cb), F32), pltpu.VMEM((T, cb), F32)],
        compiler_params=_params(("arbitrary",)),
    )(up, up, dact, conv_w, conv_w, conv_b, conv_b)
    return (jnp.concatenate([dupv, dupg], axis=1), jnp.concatenate([dwv, dwg], axis=1),
            jnp.concatenate([dbv, dbg], axis=1))


def _exchange(name, bufs, modes):
    n = len(bufs)

    def kern(*refs):
        ins, outs = refs[:n], refs[n:2 * n]
        send_sems, recv_sems, local_sems = refs[2 * n:]
        x, y, c = lax.axis_index("x"), lax.axis_index("y"), lax.axis_index("c")
        me = 4 * x + 2 * y + c
        started = []
        for b in range(n):
            gather = modes[b] == 'gather'
            own = pltpu.make_async_copy(ins[b] if gather else ins[b].at[me], outs[b].at[me], local_sems.at[b])
            own.start()
            started.append(own)
            for k in range(1, N_DEV):
                px = 1 - x if k & 4 else x
                py = 1 - y if k & 2 else y
                pc = 1 - c if k & 1 else c
                peer = 4 * px + 2 * py + pc
                sem = b * (N_DEV - 1) + k - 1
                pltpu.make_async_remote_copy(
                    src_ref=ins[b] if gather else ins[b].at[peer], dst_ref=outs[b].at[me],
                    send_sem=send_sems.at[sem], recv_sem=recv_sems.at[sem],
                    device_id=(px, py, pc), device_id_type=MESH_ID).start()
        for b in range(n):
            gather = modes[b] == 'gather'
            for k in range(1, N_DEV):
                px = 1 - x if k & 4 else x
                py = 1 - y if k & 2 else y
                pc = 1 - c if k & 1 else c
                peer = 4 * px + 2 * py + pc
                sem = b * (N_DEV - 1) + k - 1
                arrived = pltpu.make_async_remote_copy(
                    src_ref=ins[b] if gather else ins[b].at[peer], dst_ref=outs[b].at[peer],
                    send_sem=send_sems.at[sem], recv_sem=recv_sems.at[sem],
                    device_id=(px, py, pc), device_id_type=MESH_ID)
                arrived.wait_recv()
                arrived.wait_send()
        for own in started:
            own.wait()

    hbm = pl.BlockSpec(memory_space=pltpu.HBM)
    out_shape = [jax.ShapeDtypeStruct((N_DEV,) + tuple(b.shape[-2:]), b.dtype) for b in bufs]
    return pl.pallas_call(
        kern, name=name, in_specs=[hbm] * n, out_specs=[hbm] * n, out_shape=out_shape,
        scratch_shapes=[pltpu.SemaphoreType.DMA((n * (N_DEV - 1),)), pltpu.SemaphoreType.DMA((n * (N_DEV - 1),)),
                        pltpu.SemaphoreType.DMA((n,))],
    )(*bufs)


def _peers(x, y, c, with_self=False):
    out = []
    for k in range(0 if with_self else 1, N_DEV):
        px = 1 - x if k & 4 else x
        py = 1 - y if k & 2 else y
        pc = 1 - c if k & 1 else c
        out.append((k, (px, py, pc), 4 * px + 2 * py + pc))
    return out


def _exchange_start(name, bufs, modes, after):
    n = len(bufs)
    ncopy = n * N_DEV
    lands = [lax.empty((N_DEV,) + tuple(b.shape[-2:]), b.dtype) for b in bufs]

    def body(*refs):
        ins, land_in = refs[:n], refs[n:2 * n]
        send_sems, recv_sems = refs[2 * n + 1], refs[2 * n + 2]
        token = refs[-1]
        x, y, c = lax.axis_index("x"), lax.axis_index("y"), lax.axis_index("c")
        me = 4 * x + 2 * y + c
        for b in range(n):
            for k, peer, slot in _peers(x, y, c, with_self=True):
                sem = b * N_DEV + k
                pltpu.make_async_remote_copy(
                    src_ref=ins[b] if modes[b] == 'gather' else ins[b].at[slot], dst_ref=land_in[b].at[me],
                    send_sem=send_sems.at[sem], recv_sem=recv_sems.at[sem],
                    device_id=peer, device_id_type=MESH_ID).start()
        token[...] = jnp.zeros_like(token)

    hbm = pl.BlockSpec(memory_space=pltpu.HBM)
    sem_spec = pl.BlockSpec(memory_space=pltpu.SEMAPHORE)
    operands = [pltpu.with_memory_space_constraint(a, pltpu.HBM) for a in list(bufs) + lands]
    res = pl.pallas_call(
        body, name=name,
        out_shape=(pltpu.SemaphoreType.DMA((ncopy,)), pltpu.SemaphoreType.DMA((ncopy,)),
                   *[pltpu.HBM(a.shape, a.dtype) for a in operands], jax.ShapeDtypeStruct((8, LANES), F32)),
        in_specs=[hbm] * (2 * n) + [pl.BlockSpec(memory_space=pl.ANY)],
        out_specs=(sem_spec, sem_spec, *([hbm] * (2 * n)), pl.BlockSpec(memory_space=pltpu.VMEM)),
        input_output_aliases={i: 2 + i for i in range(2 * n)},
        compiler_params=pltpu.CompilerParams(has_side_effects=pltpu.SideEffectType.DATAFLOW_SIDE_EFFECTING),
    )(*operands, after)
    return (res[0], res[1], list(res[2:2 + n]), list(res[2 + n:2 + 2 * n]), tuple(modes)), res[-1]


def _exchange_wait(name, handle, after):
    send_sems, recv_sems, srcs, lands, modes = handle
    n = len(srcs)

    def body(*refs):
        src_in, land_in = refs[:n], refs[n:2 * n]
        send_ref, recv_ref = refs[2 * n], refs[2 * n + 1]
        x, y, c = lax.axis_index("x"), lax.axis_index("y"), lax.axis_index("c")
        for b in range(n):
            for k, peer, slot in _peers(x, y, c, with_self=True):
                sem = b * N_DEV + k
                copy = pltpu.make_async_remote_copy(
                    src_ref=src_in[b] if modes[b] == 'gather' else src_in[b].at[slot], dst_ref=land_in[b].at[slot],
                    send_sem=send_ref.at[sem], recv_sem=recv_ref.at[sem],
                    device_id=peer, device_id_type=MESH_ID)
                copy.wait_send()
                copy.wait_recv()

    hbm = pl.BlockSpec(memory_space=pltpu.HBM)
    sem_spec = pl.BlockSpec(memory_space=pltpu.SEMAPHORE)
    res = pl.pallas_call(
        body, name=name,
        out_shape=tuple(pltpu.HBM(a.shape, a.dtype) for a in srcs + lands),
        in_specs=[hbm] * (2 * n) + [sem_spec, sem_spec, pl.BlockSpec(memory_space=pl.ANY)],
        out_specs=tuple([hbm] * (2 * n)),
        input_output_aliases={i: i for i in range(2 * n)},
        compiler_params=pltpu.CompilerParams(has_side_effects=pltpu.SideEffectType.DATAFLOW_SIDE_EFFECTING),
    )(*srcs, *lands, send_sems, recv_sems, after)
    return list(res[n:])


def _adamw(w, g, m, v):
    m2 = ADAM_B1 * m + (1.0 - ADAM_B1) * g
    v2 = ADAM_B2 * v + (1.0 - ADAM_B2) * (g * g)
    m_hat = m2 / (1.0 - ADAM_B1 ** ADAM_STEP)
    v_hat = v2 / (1.0 - ADAM_B2 ** ADAM_STEP)
    return -ADAM_LR * (m_hat / (jnp.sqrt(v_hat) + ADAM_EPS) + ADAM_WD * w), m2, v2


def _sum_adamw(name, landed, w, m, v):
    R, C = w.shape
    row_bytes = N_DEV * (-(-C // LANES) * LANES) * landed.dtype.itemsize
    tiles = [d for d in range(16, R, 16) if R % d == 0 and d * row_bytes <= ADAM_TILE_BYTES]
    tr = max(tiles) if tiles and R * row_bytes > ADAM_TILE_BYTES else R

    def kern(x_ref, w_ref, m_ref, v_ref, g_out, d_out, m_out, v_out):
        g = x_ref[0].astype(F32)
        for q in range(1, N_DEV):
            g = g + x_ref[q].astype(F32)
        g_out[...] = g
        d_out[...], m_out[...], v_out[...] = _adamw(w_ref[...], g, m_ref[...], v_ref[...])

    blk = pl.BlockSpec((tr, C), lambda i: (i, 0))
    return pl.pallas_call(
        kern, name=name, grid=(R // tr,),
        in_specs=[pl.BlockSpec((N_DEV, tr, C), lambda i: (0, i, 0)), blk, blk, blk],
        out_specs=[blk] * 4, out_shape=[jax.ShapeDtypeStruct((R, C), F32)] * 4,
        compiler_params=_params(("arbitrary",)),
    )(landed, w, m, v)


def _sum_slots(name, landed):
    def kern(x_ref, o_ref):
        g = x_ref[0].astype(F32)
        for q in range(1, N_DEV):
            g = g + x_ref[q].astype(F32)
        o_ref[...] = g

    return pl.pallas_call(kern, name=name, out_shape=jax.ShapeDtypeStruct(landed.shape[1:], F32))(landed)


def _adamw_whole(name, g, w, m, v):
    def kern(g_ref, w_ref, m_ref, v_ref, d_out, m_out, v_out):
        d_out[...], m_out[...], v_out[...] = _adamw(w_ref[...], g_ref[...], m_ref[...], v_ref[...])

    return pl.pallas_call(kern, name=name, out_shape=[jax.ShapeDtypeStruct(w.shape, F32)] * 3)(g, w, m, v)


def _view2d(a):
    if a.ndim == 1:
        return a.reshape(1, -1)
    return a.reshape(-1, a.shape[-1])


def kernel(x, norm_mix_g, w_in, pool_w, pool_scale, ssm_log_neg_a_re, ssm_a_im, ssm_log_dt, ssm_b_re, ssm_b_im, ssm_c_re, ssm_c_im, ssm_d, glu_w, glu_b, out_norm_pool_g, out_norm_ssm_g, w_out, norm_ffn_g, w_up, conv_w, conv_b, w_down, final_norm_g, loss_target, m_norm_mix_g, m_w_in, m_pool_w, m_pool_scale, m_ssm_log_neg_a_re, m_ssm_a_im, m_ssm_log_dt, m_ssm_b_re, m_ssm_b_im, m_ssm_c_re, m_ssm_c_im, m_ssm_d, m_glu_w, m_glu_b, m_out_norm_pool_g, m_out_norm_ssm_g, m_w_out, m_norm_ffn_g, m_w_up, m_conv_w, m_conv_b, m_w_down, m_final_norm_g, v_norm_mix_g, v_w_in, v_pool_w, v_pool_scale, v_ssm_log_neg_a_re, v_ssm_a_im, v_ssm_log_dt, v_ssm_b_re, v_ssm_b_im, v_ssm_c_re, v_ssm_c_im, v_ssm_d, v_glu_w, v_glu_b, v_out_norm_pool_g, v_out_norm_ssm_g, v_w_out, v_norm_ffn_g, v_w_up, v_conv_w, v_conv_b, v_w_down, v_final_norm_g):
    given = dict(locals())
    weights = {n: given[n] for n in WEIGHTS}
    mom1 = {n: given["m_" + n] for n in WEIGHTS}
    mom2 = {n: given["v_" + n] for n in WEIGHTS}

    xs = x[0]
    tgt = loss_target[0]
    T, D = xs.shape
    DP = len(POOL_WINDOWS) * POOL_GROUP
    DS = D - DP
    G = DS // SSM_GROUP
    N, H = SSM_STATE, SSM_GROUP
    F2 = w_up.shape[1] * N_DEV
    F = F2 // 2
    fs = w_up.shape[1]
    row = lambda a: a.reshape(1, -1)

    early = _exchange("gather_early", [w_in.astype(BF16), glu_w.astype(BF16)], ['gather'] * 2)
    W_in = early[0].reshape(D, D)
    W_glu = early[1].reshape(DS, DS)
    late_src = [w_out.astype(BF16), w_down.astype(BF16), w_up.astype(BF16), conv_w]
    late_handle, token = _exchange_start("gather_late_start", late_src, ['gather'] * 4, after=early[0])
    CB = row(conv_b)

    g1, g2, g3 = row(norm_mix_g), row(norm_ffn_g), row(final_norm_g)
    gp, gs = row(out_norm_pool_g), row(out_norm_ssm_g)
    (xn,) = _rowmap("norm_mix", _rms_fwd, [xs], [g1 + token[0:1, 0:1]], [(D, BF16)])
    u = _mm("proj_in", xn, W_in, 'nn', F32)
    tmp = _pick(T, (256, 128))
    u_pad = jnp.pad(u[:, :DP], ((POOL_PAD, POOL_PAD), (0, 0)))
    pool_w_bf = pool_w.astype(BF16)
    ypn = _pool_fwd(u_pad, pool_w_bf, row(pool_scale), gp, T, tmp)

    u_ssm = _to_chunk_rows(u[:, DP:])
    u_ssm_bf = u_ssm.astype(BF16)
    a_rows = (ssm_log_neg_a_re.reshape(2 * G, N), ssm_a_im.reshape(2 * G, N), ssm_log_dt.reshape(2 * G, 1))
    disc = _ssm_params_fwd(*a_rows)
    nb = G // SSM_BLOCK_GROUPS
    vecs = [d.reshape(2, nb, 1, SSM_BLOCK_GROUPS * N) for d in disc]
    b_in = [_block_diag_in(b) for b in (ssm_b_re, ssm_b_im)]
    c_out = [_block_diag_out(c) for c in (ssm_c_re, ssm_c_im)]
    mats_bf = [[m[d].astype(BF16) for m in b_in + c_out] for d in range(2)]
    mats_t_bf = [[jnp.swapaxes(m[d], 1, 2).astype(BF16) for m in b_in + c_out] for d in range(2)]
    dir_vecs = [[v[d] for v in vecs] for d in range(2)]
    y_dir = [_ssm_fwd("ssm_fwd_%d" % d, u_ssm_bf, mats_bf[d], dir_vecs[d], reverse=bool(d)) for d in range(2)]

    def mix_post(yf, yb, us, d, gw, gb, g):
        y = yf + yb + d * us
        z = _gelu(y)
        gate = _sigmoid(jnp.dot(z.astype(BF16), gw, preferred_element_type=F32) + gb)
        return _rms_fwd(z * gate, g), y

    ysn, y_ssm = _rowmap("ssm_post", mix_post, [y_dir[0], y_dir[1], u_ssm], [row(ssm_d), W_glu, row(glu_b), gs],
                         [(DS, BF16), (DS, F32)])
    ycat = jnp.concatenate([ypn, _from_chunk_rows(ysn)], axis=1)
    late = _exchange_wait("gather_late_wait", late_handle, ysn)
    W_out = late[0].reshape(D, D)
    W_down = late[1].reshape(F, D)
    W_up = late[2].transpose(1, 0, 2).reshape(D, F2)
    CW = late[3].transpose(1, 0, 2).reshape(3, F2)
    h1 = _mm("proj_out", ycat, W_out, 'nn', F32, add=xs)
    (hn,) = _rowmap("norm_ffn", _rms_fwd, [h1], [g2], [(D, BF16)])
    up = _mm("ffn_up", hn, W_up, 'nn', F32)
    act = _convact_fwd(up, CW, CB, F)
    h2 = _mm("ffn_down", act, W_down, 'nn', F32, add=h1)

    def head(h, t, g):
        r = lax.rsqrt(_rowmean(h * h) + EPS)
        hh = h * r
        e = hh * g - t
        loss = 0.5 * jnp.sum(_rowmean(e * e), keepdims=True)
        dy = e * (1.0 / D)
        dxh = dy * g
        dh = r * (dxh - hh * _rowmean(dxh * hh))
        return dh, dh, jnp.broadcast_to(loss, (1, LANES)), _colsum(dy * hh)

    dh2, dh2_bf, loss_acc, dg3 = _rowmap("loss_head", head, [h2, tgt], [g3], [(D, F32), (D, BF16)], [(1, LANES), (1, D)])

    dact = _mm("ffn_down_dx", dh2_bf, W_down, 'nt', F32)
    dW_down = _mm("ffn_down_dw", act, dh2_bf, 'tn', BF16)
    dup, dCW, dCB = _convact_bwd(up, dact, CW, CB, F)
    dhn = _mm("ffn_up_dx", dup, W_up, 'nt', F32)
    dW_up = _mm("ffn_up_dw", hn, dup, 'tn', BF16)

    def shards(full_grad):
        return full_grad.reshape((N_DEV, full_grad.shape[0] // N_DEV) + full_grad.shape[1:])

    ffn_sent = [shards(dW_down), dW_up.reshape(D, N_DEV, fs).transpose(1, 0, 2), dCW.reshape(3, N_DEV, fs).transpose(1, 0, 2)]
    ffn_handle, token = _exchange_start("scatter_ffn_start", ffn_sent, ['scatter'] * 3, after=dhn)
    g2_late = g2 + token[0:1, 0:1]

    def norm_bwd_add(hx, dy, res, g):
        dx, dg = _rms_bwd(hx, g, dy)
        return dx + res, dg

    def norm_bwd_add2(hx, dy, res, g):
        dx, dg = _rms_bwd(hx, g, dy)
        return dx + res, dx + res, dg

    dh1, dh1_bf, dg2 = _rowmap("norm_ffn_bwd", norm_bwd_add2, [h1, dhn, dh2], [g2_late], [(D, F32), (D, BF16)], [(1, D)])
    dycat = _mm("proj_out_dx", dh1_bf, W_out, 'nt', F32)
    dW_out = _mm("proj_out_dw", ycat, dh1_bf, 'tn', BF16)

    out_sent = [shards(dW_out)]
    out_handle, token = _exchange_start("scatter_out_start", out_sent, ['scatter'], after=dycat)
    q, dpooled, dpool_w, dpool_scale, dgp = _pool_bwd_rows(u_pad, dycat[:, :DP], pool_w_bf, row(pool_scale),
                                                           gp + token[0:1, 0:1], T, tmp)
    du_pool = _pool_bwd_band(jnp.pad(q, ((POOL_PAD, POOL_PAD), (0, 0))), dpooled, T, tmp)

    def mix_post_bwd(y, us, dyn, d, gw, gb, g):
        z = _gelu(y)
        gz = z.astype(BF16)
        sig = _sigmoid(jnp.dot(gz, gw, preferred_element_type=F32) + gb)
        dys, dg = _rms_bwd(z * sig, g, dyn)
        dgl = dys * z * sig * (1.0 - sig)
        dgl_bf = dgl.astype(BF16)
        dz = dys * sig + lax.dot_general(dgl_bf, gw, (((1,), (1,)), ((), ())), preferred_element_type=F32)
        dgw = lax.dot_general(gz, dgl_bf, (((0,), (0,)), ((), ())), preferred_element_type=F32)
        dyv = dz * _gelu_grad(y)
        return dyv, dyv * d, dgw, _colsum(dgl), _colsum(dyv * us), dg

    d_ysn = _to_chunk_rows(dycat[:, DP:])
    dyv, du_dir, dglu_w, dglu_b, dssm_d, dgs = _rowmap(
        "ssm_post_bwd", mix_post_bwd, [y_ssm, u_ssm, d_ysn], [row(ssm_d), W_glu, row(glu_b), gs],
        [(DS, BF16), (DS, F32)], [(DS, DS), (1, DS), (1, DS), (1, DS)])

    du_ssm = du_dir
    dB, dC, dvec = [], [], []
    for d in range(2):
        b_re, b_im, c_re, c_im = mats_bf[d]
        b_re_t, b_im_t, c_re_t, c_im_t = mats_t_bf[d]
        res = _ssm_bwd("ssm_bwd_%d" % d, u_ssm_bf, dyv, du_ssm, (b_re, b_im), (b_re_t, b_im_t, c_re_t, c_im_t),
                       dir_vecs[d], reverse=bool(d))
        du_ssm = res[0]
        dB.append(res[1:3])
        dC.append(res[3:5])
        dvec.append(res[5:9])
    grad_b = [_block_diag_in_grad(jnp.stack([dB[0][k], dB[1][k]]), N, H) for k in range(2)]
    grad_c = [_block_diag_out_grad(jnp.stack([dC[0][k], dC[1][k]]), N, H) for k in range(2)]
    cots = [jnp.stack([dvec[0][k], dvec[1][k]]).reshape(2 * G, N) for k in range(4)]
    d_a_re, d_a_im, d_log_dt = _ssm_params_bwd(*a_rows, cots)

    rep_grads = {
        'pool_w': dpool_w, 'pool_scale': dpool_scale, 'ssm_log_neg_a_re': d_a_re, 'ssm_a_im': d_a_im,
        'ssm_log_dt': d_log_dt, 'ssm_b_re': grad_b[0], 'ssm_b_im': grad_b[1], 'ssm_c_re': grad_c[0], 'ssm_c_im': grad_c[1],
        'ssm_d': dssm_d, 'glu_b': dglu_b, 'out_norm_pool_g': dgp, 'out_norm_ssm_g': dgs, 'norm_ffn_g': dg2,
        'conv_b': dCB, 'final_norm_g': dg3}
    wide = ('pool_w', 'ssm_b_re', 'ssm_b_im', 'ssm_c_re', 'ssm_c_im')

    def travel(n):
        g = rep_grads[n]
        if n in wide:
            return g.reshape(-1, PACK_W if g.size % PACK_W == 0 else LANES).astype(BF16)
        return _view2d(g.reshape(weights[n].shape))

    small = tuple(n for n in REPLICATED if n != 'norm_mix_g')
    small_sent = [shards(dglu_w.astype(BF16))] + [travel(n) for n in small] + [loss_acc]
    small_handle, token = _exchange_start("reduce_small_start", small_sent, ['scatter'] + ['gather'] * (len(small) + 1),
                                          after=d_a_re)

    du = (jnp.concatenate([du_pool, _from_chunk_rows(du_ssm)], axis=1) + token[0:1, 0:1]).astype(BF16)
    dxn = _mm("proj_in_dx", du, W_in, 'nt', F32)
    dW_in = _mm("proj_in_dw", xn, du, 'tn', BF16)
    dx, dg1 = _rowmap("norm_mix_bwd", norm_bwd_add, [xs, dxn, dh1], [g1], [(D, F32)], [(1, D)])
    rep_grads['norm_mix_g'] = dg1
    last_sent = [shards(dW_in), travel('norm_mix_g')]
    last_handle, token = _exchange_start("reduce_last_start", last_sent, ['scatter', 'gather'], after=dx)

    grads, delta, new_m, new_v = {}, {}, {}, {}

    def update(n, landed_n):
        shape = weights[n].shape
        if len(shape) > 2 and shape[-1] < LANES // 2:
            grads[n] = _sum_slots("sum_" + n, landed_n).reshape(shape)
            delta[n], new_m[n], new_v[n] = _adamw_whole("adamw_" + n, grads[n], weights[n], mom1[n], mom2[n])
            return
        landed_n = landed_n.reshape((N_DEV,) + _view2d(weights[n]).shape)
        res = _sum_adamw("adamw_" + n, landed_n, _view2d(weights[n]), _view2d(mom1[n]), _view2d(mom2[n]))
        grads[n], delta[n], new_m[n], new_v[n] = [r.reshape(shape) for r in res]

    ffn_landed = _exchange_wait("scatter_ffn_wait", ffn_handle, token)
    for n, l in zip(('w_down', 'w_up', 'conv_w'), ffn_landed):
        update(n, l)
    (out_landed,) = _exchange_wait("scatter_out_wait", out_handle, token)
    update('w_out', out_landed)
    small_landed = _exchange_wait("reduce_small_wait", small_handle, token)
    for n, l in zip(('glu_w',) + small, small_landed):
        update(n, l)

    def loss_sum(parts_ref, o_ref):
        s = parts_ref[0]
        for q in range(1, N_DEV):
            s = s + parts_ref[q]
        o_ref[...] = s

    loss = pl.pallas_call(loss_sum, name="loss_sum", out_shape=jax.ShapeDtypeStruct((1, LANES), F32))(small_landed[-1])[0, 0]

    last_landed = _exchange_wait("reduce_last_wait", last_handle, grads[small[-1]])
    update('w_in', last_landed[0])
    update('norm_mix_g', last_landed[1])

    return (loss, dx[None], *[grads[n] for n in WEIGHTS], *[delta[n] for n in WEIGHTS],
            *[new_m[n] for n in WEIGHTS], *[new_v[n] for n in WEIGHTS])
```

```python
import functools

import jax
import jax.numpy as jnp
import numpy as np
from jax import lax
from jax.experimental import pallas as pl
from jax.experimental.pallas import tpu as pltpu

F32 = jnp.float32
BF16 = jnp.bfloat16
MESH_ID = pl.DeviceIdType.MESH

N_DEV = 8
EPS = 1e-6
POOL_WINDOWS = (2, 4, 8, 16)
POOL_GROUP = 128
POOL_PAD = 64
SSM_GROUP = 16
SSM_STATE = 64
SSM_BLOCK_GROUPS = 8
N_CHUNK = 8
LANES = 128
PACK_W = 1024
VMEM_LIMIT = 56 * 1024 * 1024
MM_VMEM_BUDGET = 40 * 1024 * 1024
MM_TILE_CAP = 1408
ADAM_TILE_BYTES = 4 * 1024 * 1024

ADAM_LR = 0.001
ADAM_B1 = 0.9
ADAM_B2 = 0.999
ADAM_EPS = 1e-08
ADAM_WD = 0.01
ADAM_STEP = 10

WEIGHTS = ['norm_mix_g', 'w_in', 'pool_w', 'pool_scale', 'ssm_log_neg_a_re', 'ssm_a_im', 'ssm_log_dt',
           'ssm_b_re', 'ssm_b_im', 'ssm_c_re', 'ssm_c_im', 'ssm_d', 'glu_w', 'glu_b', 'out_norm_pool_g',
           'out_norm_ssm_g', 'w_out', 'norm_ffn_g', 'w_up', 'conv_w', 'conv_b', 'w_down', 'final_norm_g']
SHARDED = ('w_in', 'w_out', 'w_down', 'glu_w', 'w_up', 'conv_w')
REPLICATED = tuple(n for n in WEIGHTS if n not in SHARDED)


def _pick(n, prefs):
    for p in prefs:
        if n % p == 0:
            return p
    return n


def _params(sem, vmem=None):
    return pltpu.CompilerParams(dimension_semantics=sem, vmem_limit_bytes=vmem or VMEM_LIMIT)


def _tiles(n, cap):
    return [d for d in range(LANES, min(n, cap) + 1, LANES) if n % d == 0] or [n]


def _mm(name, a, b, mode, out_dtype, add=None):
    if mode == 'nn':
        (M, K), (_, N) = a.shape, b.shape
    elif mode == 'nt':
        (M, K), (N, _) = a.shape, b.shape
    else:
        (K, M), (_, N) = a.shape, b.shape
    dims = {'nn': (((1,), (0,)), ((), ())), 'nt': (((1,), (1,)), ((), ())), 'tn': (((0,), (0,)), ((), ()))}[mode]
    sa, sb, so = a.dtype.itemsize, b.dtype.itemsize, jnp.dtype(out_dtype).itemsize
    best = None
    for tm in _tiles(M, MM_TILE_CAP):
        for tn in _tiles(N, MM_TILE_CAP):
            need = 2 * (tm * K * sa + tn * K * sb + tm * tn * (so + (4 if add is not None else 0)))
            if need <= MM_VMEM_BUDGET:
                key = (tm * tn / (tm + tn), tm * tn)
                if best is None or key > best[0]:
                    best = (key, tm, tn)
    _, tm, tn = best
    rows_inner = a.size * sa * (N // tn) + b.size * sb < a.size * sa + b.size * sb * (M // tm)

    def kern(*refs):
        a_ref, b_ref = refs[:2]
        o_ref = refs[-1]
        r = lax.dot_general(a_ref[...].astype(BF16), b_ref[...].astype(BF16), dims, preferred_element_type=F32)
        if add is not None:
            r = r + refs[2][...]
        o_ref[...] = r.astype(o_ref.dtype)

    if rows_inner:
        grid = (N // tn, M // tm)
        ij = lambda g0, g1: (g1, g0)
    else:
        grid = (M // tm, N // tn)
        ij = lambda g0, g1: (g0, g1)
    a_spec = (pl.BlockSpec((K, tm), lambda g0, g1: (0, ij(g0, g1)[0])) if mode == 'tn'
              else pl.BlockSpec((tm, K), lambda g0, g1: (ij(g0, g1)[0], 0)))
    b_spec = (pl.BlockSpec((tn, K), lambda g0, g1: (ij(g0, g1)[1], 0)) if mode == 'nt'
              else pl.BlockSpec((K, tn), lambda g0, g1: (0, ij(g0, g1)[1])))
    in_specs = [a_spec, b_spec]
    args = [a, b]
    if add is not None:
        in_specs.append(pl.BlockSpec((tm, tn), lambda g0, g1: ij(g0, g1)))
        args.append(add)
    return pl.pallas_call(
        kern, name=name, grid=grid, in_specs=in_specs,
        out_specs=pl.BlockSpec((tm, tn), lambda g0, g1: ij(g0, g1)),
        out_shape=jax.ShapeDtypeStruct((M, N), out_dtype),
        compiler_params=_params(("parallel", "parallel")),
    )(*args)


def _rowmap(name, fn, rows, fulls, out_rows, out_accs=(), tm=256):
    T = rows[0].shape[0]
    tm = min(tm, T)
    assert T % tm == 0
    n_in, n_row = len(rows) + len(fulls), len(out_rows)

    def kern(*refs):
        i = pl.program_id(0)
        res = fn(*[r[...] for r in refs[:n_in]])
        res = res if isinstance(res, (tuple, list)) else (res,)
        outs = refs[n_in:]
        for o, v in zip(outs[:n_row], res[:n_row]):
            o[...] = v.astype(o.dtype)
        for o, v in zip(outs[n_row:], res[n_row:]):
            @pl.when(i == 0)
            def _(o=o, v=v):
                o[...] = v

            @pl.when(i != 0)
            def _(o=o, v=v):
                o[...] += v

    def full_spec(shape):
        nd = len(shape)
        return pl.BlockSpec(tuple(shape), lambda i: (0,) * nd)

    in_specs = [pl.BlockSpec((tm, r.shape[1]), lambda i: (i, 0)) for r in rows] + [full_spec(f.shape) for f in fulls]
    out_specs = [pl.BlockSpec((tm, c), lambda i: (i, 0)) for c, _ in out_rows] + [full_spec(s) for s in out_accs]
    out_shape = [jax.ShapeDtypeStruct((T, c), dt) for c, dt in out_rows] + [jax.ShapeDtypeStruct(tuple(s), F32) for s in out_accs]
    return pl.pallas_call(
        kern, name=name, grid=(T // tm,), in_specs=in_specs, out_specs=out_specs, out_shape=out_shape,
        compiler_params=_params(("arbitrary",)),
    )(*rows, *fulls)


def _colsum(v):
    return jnp.sum(v, axis=0, keepdims=True)


def _rowmean(v):
    return jnp.mean(v, axis=-1, keepdims=True)


def _rms_fwd(x, g):
    r = lax.rsqrt(_rowmean(x * x) + EPS)
    return x * r * g


def _rms_bwd(x, g, dy):
    r = lax.rsqrt(_rowmean(x * x) + EPS)
    xh = x * r
    dxh = dy * g
    return r * (dxh - xh * _rowmean(dxh * xh)), _colsum(dy * xh)


def _sigmoid(v):
    return 0.5 * jnp.tanh(0.5 * v) + 0.5


def _gelu(y):
    c = np.sqrt(2.0 / np.pi).astype(np.float32)
    return 0.5 * y * (1.0 + jnp.tanh(c * (y + 0.044715 * (y * y * y))))


def _gelu_grad(y):
    c = np.sqrt(2.0 / np.pi).astype(np.float32)
    th = jnp.tanh(c * (y + 0.044715 * (y * y * y)))
    return 0.5 * (1.0 + th) + 0.5 * y * (1.0 - th * th) * c * (1.0 + 3.0 * 0.044715 * (y * y))


def _split3(v):
    hi = v.astype(BF16)
    r1 = v - hi.astype(F32)
    mid = r1.astype(BF16)
    lo = (r1 - mid.astype(F32)).astype(BF16)
    return hi, mid, lo


def _band(tm, lo_off, hi_off):
    shape = (tm, tm + 2 * POOL_PAD)
    d = lax.broadcasted_iota(jnp.int32, shape, 1) - lax.broadcasted_iota(jnp.int32, shape, 0) - POOL_PAD
    return ((d >= lo_off) & (d < hi_off)).astype(BF16)


def _band_sum(band, ext):
    hi, mid, lo = _split3(ext)
    dot = functools.partial(jnp.dot, preferred_element_type=F32)
    return dot(band, hi) + dot(band, mid) + dot(band, lo)


def _window_count(r0, tm, half, T):
    t = r0 + lax.broadcasted_iota(jnp.int32, (tm, 1), 0)
    return (jnp.minimum(t + half, T) - jnp.maximum(t - half, 0)).astype(F32)


def _halo_specs(tm, C, T):
    per = tm // POOL_PAD
    last = T // POOL_PAD - 1
    return [pl.BlockSpec((POOL_PAD, C), lambda i: (jnp.maximum(i * per - 1, 0), 0)),
            pl.BlockSpec((tm, C), lambda i: (i, 0)),
            pl.BlockSpec((POOL_PAD, C), lambda i: (jnp.minimum((i + 1) * per, last), 0))]


def _with_halo(before_ref, main_ref, after_ref, i, n):
    before = jnp.where(i > 0, before_ref[...], 0.0)
    after = jnp.where(i < n - 1, after_ref[...], 0.0)
    return jnp.concatenate([before, main_ref[...], after], axis=0)


def _pool_block(ext, ctr, w_ref, r0, tm, T):
    pooled, conc = [], []
    for gi, w in enumerate(POOL_WINDOWS):
        half = w // 2
        cols = slice(gi * POOL_GROUP, (gi + 1) * POOL_GROUP)
        ws = _band_sum(_band(tm, -half, half), ext[:, cols])
        p = ws / _window_count(r0, tm, half, T) - ctr[:, cols]
        pooled.append(p)
        conc.append(jnp.dot(p.astype(BF16), w_ref[gi], preferred_element_type=F32))
    return pooled, jnp.concatenate(conc, axis=1)


def _pool_fwd(u, pool_w_bf, pool_scale, g_pool, tm):
    T = u.shape[0]
    nw = len(POOL_WINDOWS)
    C = nw * POOL_GROUP
    n = T // tm

    def kern(ub_ref, u_ref, ua_ref, w_ref, sc_ref, g_ref, o_ref):
        i = pl.program_id(0)
        r0 = pl.multiple_of(i * tm, tm)
        _, conc = _pool_block(_with_halo(ub_ref, u_ref, ua_ref, i, n), u_ref[...], w_ref, r0, tm, T)
        o_ref[...] = _rms_fwd(conc * sc_ref[...], g_ref[...]).astype(o_ref.dtype)

    return pl.pallas_call(
        kern, name="pool_fwd", grid=(n,),
        in_specs=_halo_specs(tm, C, T) + [pl.BlockSpec(pool_w_bf.shape, lambda i: (0, 0, 0)),
                                          pl.BlockSpec((1, C), lambda i: (0, 0)), pl.BlockSpec((1, C), lambda i: (0, 0))],
        out_specs=pl.BlockSpec((tm, C), lambda i: (i, 0)),
        out_shape=jax.ShapeDtypeStruct((T, C), BF16),
        compiler_params=_params(("arbitrary",)),
    )(u, u, u, pool_w_bf, pool_scale, g_pool)


def _pool_bwd_rows(u, d_y, pool_w_bf, pool_scale, g_pool, tm):
    T = u.shape[0]
    nw = len(POOL_WINDOWS)
    C = nw * POOL_GROUP
    n = T // tm

    def kern(ub_ref, u_ref, ua_ref, dy_ref, w_ref, sc_ref, g_ref, q_ref, dp_ref, dw_ref, dsc_ref, dg_ref):
        i = pl.program_id(0)
        r0 = pl.multiple_of(i * tm, tm)
        pooled, conc = _pool_block(_with_halo(ub_ref, u_ref, ua_ref, i, n), u_ref[...], w_ref, r0, tm, T)
        sc = sc_ref[...]
        dyp, dg = _rms_bwd(conc * sc, g_ref[...], dy_ref[...])
        dsc = _colsum(dyp * conc)
        dconc = (dyp * sc).astype(BF16)
        dws = []
        for gi, w in enumerate(POOL_WINDOWS):
            cols = slice(gi * POOL_GROUP, (gi + 1) * POOL_GROUP)
            dc = dconc[:, cols]
            dp = lax.dot_general(dc, w_ref[gi], (((1,), (1,)), ((), ())), preferred_element_type=F32)
            dp_ref[:, cols] = dp
            q_ref[:, cols] = dp / _window_count(r0, tm, w // 2, T)
            dws.append(lax.dot_general(pooled[gi].astype(BF16), dc, (((0,), (0,)), ((), ())), preferred_element_type=F32))

        @pl.when(i == 0)
        def _():
            for gi in range(nw):
                dw_ref[gi] = dws[gi]
            dsc_ref[...] = dsc
            dg_ref[...] = dg

        @pl.when(i != 0)
        def _():
            for gi in range(nw):
                dw_ref[gi] += dws[gi]
            dsc_ref[...] += dsc
            dg_ref[...] += dg

    full2 = pl.BlockSpec((1, C), lambda i: (0, 0))
    row_blk = pl.BlockSpec((tm, C), lambda i: (i, 0))
    return pl.pallas_call(
        kern, name="pool_bwd_rows", grid=(n,),
        in_specs=_halo_specs(tm, C, T) + [row_blk, pl.BlockSpec(pool_w_bf.shape, lambda i: (0, 0, 0)), full2, full2],
        out_specs=[row_blk, row_blk, pl.BlockSpec((nw, POOL_GROUP, POOL_GROUP), lambda i: (0, 0, 0)), full2, full2],
        out_shape=[jax.ShapeDtypeStruct((T, C), F32), jax.ShapeDtypeStruct((T, C), F32),
                   jax.ShapeDtypeStruct((nw, POOL_GROUP, POOL_GROUP), F32),
                   jax.ShapeDtypeStruct((1, C), F32), jax.ShapeDtypeStruct((1, C), F32)],
        compiler_params=_params(("arbitrary",)),
    )(u, u, u, d_y, pool_w_bf, pool_scale, g_pool)


def _pool_bwd_band(q, dpooled, tm):
    T, C = dpooled.shape
    n = T // tm

    def kern(qb_ref, q_ref, qa_ref, dp_ref, o_ref):
        ext = _with_halo(qb_ref, q_ref, qa_ref, pl.program_id(0), n)
        for gi, w in enumerate(POOL_WINDOWS):
            half = w // 2
            cols = slice(gi * POOL_GROUP, (gi + 1) * POOL_GROUP)
            o_ref[:, cols] = _band_sum(_band(tm, -half + 1, half + 1), ext[:, cols]) - dp_ref[:, cols]

    return pl.pallas_call(
        kern, name="pool_bwd_band", grid=(n,),
        in_specs=_halo_specs(tm, C, T) + [pl.BlockSpec((tm, C), lambda i: (i, 0))],
        out_specs=pl.BlockSpec((tm, C), lambda i: (i, 0)),
        out_shape=jax.ShapeDtypeStruct((T, C), F32),
        compiler_params=_params(("arbitrary",)),
    )(q, q, q, dpooled)


def _cmul(ar, ai, br, bi):
    return ar * br - ai * bi, ar * bi + ai * br


def _ssm_discretise(log_neg_a_re, a_im, log_dt):
    dt = jnp.exp(log_dt)
    a_re = -jnp.exp(log_neg_a_re)
    mag = jnp.exp(a_re * dt)
    ang = a_im * dt
    lam_re, lam_im = mag * jnp.cos(ang), mag * jnp.sin(ang)
    den = a_re * a_re + a_im * a_im
    f_re = ((lam_re - 1.0) * a_re + lam_im * a_im) / den
    f_im = (lam_im * a_re - (lam_re - 1.0) * a_im) / den
    return lam_re, lam_im, f_re, f_im


def _ssm_params_fwd(log_neg_a_re, a_im, log_dt):
    rows, n = log_neg_a_re.shape

    def kern(a_ref, b_ref, c_ref, o1, o2, o3, o4):
        for o, v in zip((o1, o2, o3, o4), _ssm_discretise(a_ref[...], b_ref[...], c_ref[...])):
            o[...] = v

    return pl.pallas_call(kern, name="ssm_params_fwd", out_shape=[jax.ShapeDtypeStruct((rows, n), F32)] * 4)(
        log_neg_a_re, a_im, log_dt)


def _ssm_params_bwd(log_neg_a_re, a_im, log_dt, cots):
    rows, n = log_neg_a_re.shape

    def kern(a_ref, b_ref, c_ref, g1, g2, g3, g4, o1, o2, o3):
        _, vjp = jax.vjp(_ssm_discretise, a_ref[...], b_ref[...], c_ref[...])
        d1, d2, d3 = vjp((g1[...], g2[...], g3[...], g4[...]))
        o1[...] = d1
        o2[...] = d2
        o3[...] = d3

    return pl.pallas_call(
        kern, name="ssm_params_bwd",
        out_shape=[jax.ShapeDtypeStruct((rows, n), F32), jax.ShapeDtypeStruct((rows, n), F32),
                   jax.ShapeDtypeStruct((rows, 1), F32)])(log_neg_a_re, a_im, log_dt, *cots)


def _cpow(lr, li, n):
    out = None
    br, bi = lr, li
    while n:
        if n & 1:
            out = (br, bi) if out is None else _cmul(out[0], out[1], br, bi)
        n >>= 1
        if n:
            br, bi = _cmul(br, bi, br, bi)
    return out


def _slab(t):
    return pl.ds(pl.multiple_of(t * N_CHUNK, N_CHUNK), N_CHUNK)


def _shift_chunks(vr, vi, reverse):
    sub = lax.broadcasted_iota(jnp.int32, vr.shape, 0)
    if reverse:
        keep = sub != N_CHUNK - 1
        return jnp.where(keep, pltpu.roll(vr, N_CHUNK - 1, 0), 0.0), jnp.where(keep, pltpu.roll(vi, N_CHUNK - 1, 0), 0.0)
    keep = sub != 0
    return jnp.where(keep, pltpu.roll(vr, 1, 0), 0.0), jnp.where(keep, pltpu.roll(vi, 1, 0), 0.0)


def _chunk_scan(xr, xi, lr, li, Lc, reverse, prev=None):
    C = xr.shape[1]
    lrb, lib = jnp.broadcast_to(lr, (N_CHUNK, C)), jnp.broadcast_to(li, (N_CHUNK, C))
    zero = jnp.zeros((N_CHUNK, C), F32)

    def step_of(k):
        return (Lc - 1 - k) if reverse else k

    def advance(sr, si, t):
        return lrb * sr - lib * si + xr[_slab(t), :], lrb * si + lib * sr + xi[_slab(t), :]

    def pass1(k, c):
        return advance(c[0], c[1], step_of(k))

    er, ei = lax.fori_loop(0, Lc, pass1, (zero, zero), unroll=4)
    pr, pi = _cpow(lr, li, Lc)
    prb, pib = jnp.broadcast_to(pr, (N_CHUNK, C)), jnp.broadcast_to(pi, (N_CHUNK, C))
    cr, ci = zero, zero
    for _ in range(N_CHUNK - 1):
        cr, ci = _shift_chunks(er + prb * cr - pib * ci, ei + prb * ci + pib * cr, reverse)

    if prev is None:
        def pass2(k, c):
            t = step_of(k)
            nr, ni = advance(c[0], c[1], t)
            xr[_slab(t), :] = nr
            xi[_slab(t), :] = ni
            return nr, ni

        lax.fori_loop(0, Lc, pass2, (cr, ci), unroll=4)
        return None

    qr, qi = prev
    def pair(ar, ai, wr, wi, acc):
        return acc[0] + ar * wr + ai * wi, acc[1] + ai * wr - ar * wi

    def pass2(k, c):
        sr, si, accr, acci = c
        t = step_of(k)
        nr, ni = advance(sr, si, t)
        xr[_slab(t), :] = nr
        xi[_slab(t), :] = ni
        tp = t - 1 if reverse else t + 1
        accr, acci = pair(nr, ni, qr[_slab(tp), :], qi[_slab(tp), :], (accr, acci))
        return nr, ni, accr, acci

    sr, si, accr, acci = lax.fori_loop(0, Lc - 1, pass2, (cr, ci, zero, zero), unroll=4)
    t = step_of(Lc - 1)
    nr, ni = advance(sr, si, t)
    xr[_slab(t), :] = nr
    xi[_slab(t), :] = ni
    edge = step_of(0)
    wr, wi = _shift_chunks(qr[_slab(edge), :], qi[_slab(edge), :], not reverse)
    return pair(nr, ni, wr, wi, (accr, acci))


def _ssm_dims(T):
    assert T % (N_CHUNK * 64) == 0
    rc = _pick(T, (512, 256, 128, 64))
    return T // N_CHUNK, rc, T // rc


def _ssm_project_in(u_ref, bre_ref, bim_ref, xr, xi, rc, nrc):
    for c in range(nrc):
        rows = pl.ds(c * rc, rc)
        ub = u_ref[rows, :]
        xr[rows, :] = jnp.dot(ub, bre_ref[...], preferred_element_type=F32)
        xi[rows, :] = jnp.dot(ub, bim_ref[...], preferred_element_type=F32)


def _ssm_fwd(name, u_bf, mats, vecs, reverse):
    T, W = u_bf.shape
    nb = W // LANES
    SC = SSM_BLOCK_GROUPS * SSM_STATE
    Lc, rc, nrc = _ssm_dims(T)
    bre, bim, cre, cim = mats
    lam_re, lam_im, f_re, f_im = vecs

    def kern(u_ref, bre_ref, bim_ref, cre_ref, cim_ref, lr_ref, li_ref, fr_ref, fi_ref, y_ref, xr, xi):
        _ssm_project_in(u_ref, bre_ref, bim_ref, xr, xi, rc, nrc)
        _chunk_scan(xr, xi, lr_ref[...], li_ref[...], Lc, reverse)
        fr, fi = fr_ref[...], fi_ref[...]
        for c in range(nrc):
            rows = pl.ds(c * rc, rc)
            sr, si = _cmul(fr, fi, xr[rows, :], xi[rows, :])
            y_ref[rows, :] = (jnp.dot(sr.astype(BF16), cre_ref[...], preferred_element_type=F32)
                              - jnp.dot(si.astype(BF16), cim_ref[...], preferred_element_type=F32))

    col = pl.BlockSpec((T, LANES), lambda m: (0, m))
    mat_in = pl.BlockSpec((None, LANES, SC), lambda m: (m, 0, 0))
    mat_out = pl.BlockSpec((None, SC, LANES), lambda m: (m, 0, 0))
    vec = pl.BlockSpec((None, 1, SC), lambda m: (m, 0, 0))
    return pl.pallas_call(
        kern, name=name, grid=(nb,),
        in_specs=[col, mat_in, mat_in, mat_out, mat_out, vec, vec, vec, vec],
        out_specs=col, out_shape=jax.ShapeDtypeStruct((T, W), F32),
        scratch_shapes=[pltpu.VMEM((T, SC), F32), pltpu.VMEM((T, SC), F32)],
        compiler_params=_params(("arbitrary",)),
    )(u_bf, bre, bim, cre, cim, lam_re, lam_im, f_re, f_im)


def _ssm_bwd(name, u_bf, dy_bf, du_in, mats, mats_t, vecs, reverse):
    T, W = u_bf.shape
    nb = W // LANES
    SC = SSM_BLOCK_GROUPS * SSM_STATE
    Lc, rc, nrc = _ssm_dims(T)
    bre, bim = mats
    bre_t, bim_t, cre_t, cim_t = mats_t
    lam_re, lam_im, f_re, f_im = vecs
    contract_rows = (((0,), (0,)), ((), ()))

    def kern(u_ref, dy_ref, dui_ref, bre_ref, bim_ref, bret_ref, bimt_ref, cret_ref, cimt_ref,
             lr_ref, li_ref, fr_ref, fi_ref,
             du_ref, dbre_ref, dbim_ref, dcre_ref, dcim_ref, dlr_ref, dli_ref, dfr_ref, dfi_ref,
             xr, xi, gr, gi):
        lr, li, fr, fi = lr_ref[...], li_ref[...], fr_ref[...], fi_ref[...]
        _ssm_project_in(u_ref, bre_ref, bim_ref, xr, xi, rc, nrc)
        _chunk_scan(xr, xi, lr, li, Lc, reverse)
        mre = jnp.zeros((LANES, SC), F32)
        mim = jnp.zeros((LANES, SC), F32)
        dfr = jnp.zeros((1, SC), F32)
        dfi = jnp.zeros((1, SC), F32)
        for c in range(nrc):
            rows = pl.ds(c * rc, rc)
            dy = dy_ref[rows, :]
            x_r, x_i = xr[rows, :], xi[rows, :]
            g_r = jnp.dot(dy, cret_ref[...], preferred_element_type=F32)
            g_i = -jnp.dot(dy, cimt_ref[...], preferred_element_type=F32)
            mre += lax.dot_general(dy, x_r.astype(BF16), contract_rows, preferred_element_type=F32)
            mim += lax.dot_general(dy, x_i.astype(BF16), contract_rows, preferred_element_type=F32)
            dfr += _colsum(g_r * x_r + g_i * x_i)
            dfi += _colsum(g_i * x_r - g_r * x_i)
            gr[rows, :] = fr * g_r + fi * g_i
            gi[rows, :] = fr * g_i - fi * g_r
        dcre_ref[...] = fr * mre - fi * mim
        dcim_ref[...] = -(fr * mim + fi * mre)
        dfr_ref[...] = dfr
        dfi_ref[...] = dfi
        accr, acci = _chunk_scan(gr, gi, lr, -li, Lc, not reverse, prev=(xr, xi))
        dlr_ref[...] = _colsum(accr)
        dli_ref[...] = _colsum(acci)
        dbre = jnp.zeros((LANES, SC), F32)
        dbim = jnp.zeros((LANES, SC), F32)
        for c in range(nrc):
            rows = pl.ds(c * rc, rc)
            ub = u_ref[rows, :]
            a_r, a_i = gr[rows, :].astype(BF16), gi[rows, :].astype(BF16)
            dbre += lax.dot_general(ub, a_r, contract_rows, preferred_element_type=F32)
            dbim += lax.dot_general(ub, a_i, contract_rows, preferred_element_type=F32)
            du_ref[rows, :] = (dui_ref[rows, :] + jnp.dot(a_r, bret_ref[...], preferred_element_type=F32)
                               + jnp.dot(a_i, bimt_ref[...], preferred_element_type=F32))
        dbre_ref[...] = dbre
        dbim_ref[...] = dbim

    col = pl.BlockSpec((T, LANES), lambda m: (0, m))
    mat_in = pl.BlockSpec((None, LANES, SC), lambda m: (m, 0, 0))
    mat_out = pl.BlockSpec((None, SC, LANES), lambda m: (m, 0, 0))
    vec = pl.BlockSpec((None, 1, SC), lambda m: (m, 0, 0))
    mat_shape = jax.ShapeDtypeStruct((nb, LANES, SC), F32)
    vec_shape = jax.ShapeDtypeStruct((nb, 1, SC), F32)
    return pl.pallas_call(
        kern, name=name, grid=(nb,),
        in_specs=[col, col, col, mat_in, mat_in, mat_out, mat_out, mat_in, mat_in, vec, vec, vec, vec],
        out_specs=[col, mat_in, mat_in, mat_in, mat_in, vec, vec, vec, vec],
        out_shape=[jax.ShapeDtypeStruct((T, W), F32), mat_shape, mat_shape, mat_shape, mat_shape,
                   vec_shape, vec_shape, vec_shape, vec_shape],
        scratch_shapes=[pltpu.VMEM((T, SC), F32)] * 4,
        compiler_params=_params(("arbitrary",)),
    )(u_bf, dy_bf, du_in, bre, bim, bre_t, bim_t, cre_t, cim_t, lam_re, lam_im, f_re, f_im)


def _block_diag_in(b):
    d, G, N, H = b.shape
    nb = G // SSM_BLOCK_GROUPS
    eye = jnp.eye(SSM_BLOCK_GROUPS, dtype=b.dtype)
    v = b.reshape(d, nb, SSM_BLOCK_GROUPS, N, H)
    return jnp.einsum('dmgnh,gk->dmghkn', v, eye).reshape(d, nb, SSM_BLOCK_GROUPS * H, SSM_BLOCK_GROUPS * N)


def _block_diag_in_grad(m, N, H):
    d, nb = m.shape[:2]
    eye = jnp.eye(SSM_BLOCK_GROUPS, dtype=m.dtype)
    v = m.reshape(d, nb, SSM_BLOCK_GROUPS, H, SSM_BLOCK_GROUPS, N)
    return jnp.einsum('dmghkn,gk->dmgnh', v, eye).reshape(d, nb * SSM_BLOCK_GROUPS, N, H)


def _block_diag_out(c):
    return jnp.swapaxes(_block_diag_in(jnp.swapaxes(c, 2, 3)), 2, 3)


def _block_diag_out_grad(m, N, H):
    return jnp.swapaxes(_block_diag_in_grad(m, N, H), 2, 3)


def _to_chunk_rows(a):
    T, C = a.shape
    return a.reshape(N_CHUNK, T // N_CHUNK, C).transpose(1, 0, 2).reshape(T, C)


def _from_chunk_rows(a):
    T, C = a.shape
    return a.reshape(T // N_CHUNK, N_CHUNK, C).transpose(1, 0, 2).reshape(T, C)


def _conv_taps(ref, c, R, T):
    n = T // R
    r0 = pl.multiple_of(c * R, R)
    main = ref[pl.ds(r0, R), :]
    before = ref[pl.ds(pl.multiple_of(jnp.maximum(r0 - 8, 0), 8), 8), :]
    after = ref[pl.ds(pl.multiple_of(jnp.minimum(r0 + R, T - 8), 8), 8), :]
    ext = jnp.concatenate([jnp.where(c > 0, before, 0.0), main, jnp.where(c < n - 1, after, 0.0)], axis=0)
    m1 = pltpu.roll(ext, 1, 0)[8:8 + R]
    p1 = pltpu.roll(ext, R + 15, 0)[8:8 + R]
    return m1, main, p1, r0


def _conv_specs(T, F, cb):
    nj = F // cb
    lo = lambda rows: pl.BlockSpec((rows, cb), lambda j: (0, j))
    hi = lambda rows: pl.BlockSpec((rows, cb), lambda j: (0, j + nj))
    return nj, lo, hi


def _convact_fwd(up, conv_w, conv_b, F):
    T = up.shape[0]
    cb = _pick(F, (256, 128))
    R = _pick(T, (256, 128, 64))
    nj, lo, hi = _conv_specs(T, F, cb)

    def kern(uv, ug, wv, wg, bv, bg, o_ref):
        def body(c, carry):
            v1, v0, v2, r0 = _conv_taps(uv, c, R, T)
            g1, g0, g2, _ = _conv_taps(ug, c, R, T)
            val = v1 * wv[0:1, :] + v0 * wv[1:2, :] + v2 * wv[2:3, :] + bv[...]
            gate = g1 * wg[0:1, :] + g0 * wg[1:2, :] + g2 * wg[2:3, :] + bg[...]
            o_ref[pl.ds(r0, R), :] = (val * (gate * _sigmoid(gate))).astype(o_ref.dtype)
            return carry

        lax.fori_loop(0, T // R, body, 0)

    return pl.pallas_call(
        kern, name="convact_fwd", grid=(nj,),
        in_specs=[lo(T), hi(T), lo(3), hi(3), lo(1), hi(1)],
        out_specs=lo(T), out_shape=jax.ShapeDtypeStruct((T, F), BF16),
        compiler_params=_params(("arbitrary",)),
    )(up, up, conv_w, conv_w, conv_b, conv_b)


def _convact_bwd(up, dact, conv_w, conv_b, F):
    T = up.shape[0]
    cb = _pick(F, (256, 128))
    R = _pick(T, (256, 128, 64))
    nj, lo, hi = _conv_specs(T, F, cb)

    def kern(uv, ug, da, wv, wg, bv, bg, dupv, dupg, dwv, dwg, dbv, dbg, sv, sg):
        zero = jnp.zeros((1, cb), F32)

        def pass_a(c, acc):
            v1, v0, v2, r0 = _conv_taps(uv, c, R, T)
            g1, g0, g2, _ = _conv_taps(ug, c, R, T)
            val = v1 * wv[0:1, :] + v0 * wv[1:2, :] + v2 * wv[2:3, :] + bv[...]
            gate = g1 * wg[0:1, :] + g0 * wg[1:2, :] + g2 * wg[2:3, :] + bg[...]
            sig = _sigmoid(gate)
            d = da[pl.ds(r0, R), :]
            dval = d * (gate * sig)
            dgate = d * val * (sig * (1.0 + gate * (1.0 - sig)))
            sv[pl.ds(r0, R), :] = dval
            sg[pl.ds(r0, R), :] = dgate
            terms = (dval * v1, dval * v0, dval * v2, dval, dgate * g1, dgate * g0, dgate * g2, dgate)
            return tuple(a + _colsum(t) for a, t in zip(acc, terms))

        acc = lax.fori_loop(0, T // R, pass_a, (zero,) * 8)
        for k in range(3):
            dwv[k:k + 1, :] = acc[k]
            dwg[k:k + 1, :] = acc[4 + k]
        dbv[...] = acc[3]
        dbg[...] = acc[7]

        def pass_b(c, carry):
            v1, v0, v2, r0 = _conv_taps(sv, c, R, T)
            g1, g0, g2, _ = _conv_taps(sg, c, R, T)
            dupv[pl.ds(r0, R), :] = (v2 * wv[0:1, :] + v0 * wv[1:2, :] + v1 * wv[2:3, :]).astype(dupv.dtype)
            dupg[pl.ds(r0, R), :] = (g2 * wg[0:1, :] + g0 * wg[1:2, :] + g1 * wg[2:3, :]).astype(dupg.dtype)
            return carry

        lax.fori_loop(0, T // R, pass_b, 0)

    dupv, dupg, dwv, dwg, dbv, dbg = pl.pallas_call(
        kern, name="convact_bwd", grid=(nj,),
        in_specs=[lo(T), hi(T), lo(T), lo(3), hi(3), lo(1), hi(1)],
        out_specs=[lo(T), lo(T), lo(3), lo(3), lo(1), lo(1)],
        out_shape=[jax.ShapeDtypeStruct((T, F), BF16), jax.ShapeDtypeStruct((T, F), BF16),
                   jax.ShapeDtypeStruct((3, F), F32), jax.ShapeDtypeStruct((3, F), F32),
                   jax.ShapeDtypeStruct((1, F), F32), jax.ShapeDtypeStruct((1, F), F32)],
        scratch_shapes=[pltpu.VMEM((T, cb), F32), pltpu.VMEM((T, cb), F32)],
        compiler_params=_params(("arbitrary",)),
    )(up, up, dact, conv_w, conv_w, conv_b, conv_b)
    return (jnp.concatenate([dupv, dupg], axis=1), jnp.concatenate([dwv, dwg], axis=1),
            jnp.concatenate([dbv, dbg], axis=1))


def _peers(x, y, c, with_self=False):
    out = []
    for k in range(0 if with_self else 1, N_DEV):
        px = 1 - x if k & 4 else x
        py = 1 - y if k & 2 else y
        pc = 1 - c if k & 1 else c
        out.append((k, (px, py, pc), 4 * px + 2 * py + pc))
    return out


def _exchange_start(name, bufs, modes, after):
    n = len(bufs)
    ncopy = n * N_DEV
    lands = [lax.empty((N_DEV,) + tuple(b.shape[-2:]), b.dtype) for b in bufs]

    def body(*refs):
        ins, land_in = refs[:n], refs[n:2 * n]
        send_sems, recv_sems = refs[2 * n + 1], refs[2 * n + 2]
        token = refs[-1]
        x, y, c = lax.axis_index("x"), lax.axis_index("y"), lax.axis_index("c")
        me = 4 * x + 2 * y + c
        for b in range(n):
            for k, peer, slot in _peers(x, y, c, with_self=True):
                sem = b * N_DEV + k
                pltpu.make_async_remote_copy(
                    src_ref=ins[b] if modes[b] == 'gather' else ins[b].at[slot], dst_ref=land_in[b].at[me],
                    send_sem=send_sems.at[sem], recv_sem=recv_sems.at[sem],
                    device_id=peer, device_id_type=MESH_ID).start()
        token[...] = jnp.zeros_like(token)

    hbm = pl.BlockSpec(memory_space=pltpu.HBM)
    sem_spec = pl.BlockSpec(memory_space=pltpu.SEMAPHORE)
    operands = [pltpu.with_memory_space_constraint(a, pltpu.HBM) for a in list(bufs) + lands]
    res = pl.pallas_call(
        body, name=name,
        out_shape=(pltpu.SemaphoreType.DMA((ncopy,)), pltpu.SemaphoreType.DMA((ncopy,)),
                   *[pltpu.HBM(a.shape, a.dtype) for a in operands], jax.ShapeDtypeStruct((8, LANES), F32)),
        in_specs=[hbm] * (2 * n) + [pl.BlockSpec(memory_space=pl.ANY)],
        out_specs=(sem_spec, sem_spec, *([hbm] * (2 * n)), pl.BlockSpec(memory_space=pltpu.VMEM)),
        input_output_aliases={i: 2 + i for i in range(2 * n)},
        compiler_params=pltpu.CompilerParams(has_side_effects=pltpu.SideEffectType.DATAFLOW_SIDE_EFFECTING),
    )(*operands, after)
    return (res[0], res[1], list(res[2:2 + n]), list(res[2 + n:2 + 2 * n]), tuple(modes)), res[-1]


def _exchange_wait(name, handle, after):
    send_sems, recv_sems, srcs, lands, modes = handle
    n = len(srcs)

    def body(*refs):
        src_in, land_in = refs[:n], refs[n:2 * n]
        send_ref, recv_ref = refs[2 * n], refs[2 * n + 1]
        x, y, c = lax.axis_index("x"), lax.axis_index("y"), lax.axis_index("c")
        for b in range(n):
            for k, peer, slot in _peers(x, y, c, with_self=True):
                sem = b * N_DEV + k
                copy = pltpu.make_async_remote_copy(
                    src_ref=src_in[b] if modes[b] == 'gather' else src_in[b].at[slot], dst_ref=land_in[b].at[slot],
                    send_sem=send_ref.at[sem], recv_sem=recv_ref.at[sem],
                    device_id=peer, device_id_type=MESH_ID)
                copy.wait_send()
                copy.wait_recv()

    hbm = pl.BlockSpec(memory_space=pltpu.HBM)
    sem_spec = pl.BlockSpec(memory_space=pltpu.SEMAPHORE)
    res = pl.pallas_call(
        body, name=name,
        out_shape=tuple(pltpu.HBM(a.shape, a.dtype) for a in srcs + lands),
        in_specs=[hbm] * (2 * n) + [sem_spec, sem_spec, pl.BlockSpec(memory_space=pl.ANY)],
        out_specs=tuple([hbm] * (2 * n)),
        input_output_aliases={i: i for i in range(2 * n)},
        compiler_params=pltpu.CompilerParams(has_side_effects=pltpu.SideEffectType.DATAFLOW_SIDE_EFFECTING),
    )(*srcs, *lands, send_sems, recv_sems, after)
    return list(res[n:])


def _adamw(w, g, m, v):
    m2 = ADAM_B1 * m + (1.0 - ADAM_B1) * g
    v2 = ADAM_B2 * v + (1.0 - ADAM_B2) * (g * g)
    m_hat = m2 / (1.0 - ADAM_B1 ** ADAM_STEP)
    v_hat = v2 / (1.0 - ADAM_B2 ** ADAM_STEP)
    return -ADAM_LR * (m_hat / (jnp.sqrt(v_hat) + ADAM_EPS) + ADAM_WD * w), m2, v2


def _sum_adamw(name, landed, w, m, v):
    R, C = w.shape
    row_bytes = N_DEV * (-(-C // LANES) * LANES) * landed.dtype.itemsize
    tiles = [d for d in range(16, R, 16) if R % d == 0 and d * row_bytes <= ADAM_TILE_BYTES]
    tr = max(tiles) if tiles and R * row_bytes > ADAM_TILE_BYTES else R

    def kern(x_ref, w_ref, m_ref, v_ref, g_out, d_out, m_out, v_out):
        g = x_ref[0].astype(F32)
        for q in range(1, N_DEV):
            g = g + x_ref[q].astype(F32)
        g_out[...] = g
        d_out[...], m_out[...], v_out[...] = _adamw(w_ref[...], g, m_ref[...], v_ref[...])

    blk = pl.BlockSpec((tr, C), lambda i: (i, 0))
    return pl.pallas_call(
        kern, name=name, grid=(R // tr,),
        in_specs=[pl.BlockSpec((N_DEV, tr, C), lambda i: (0, i, 0)), blk, blk, blk],
        out_specs=[blk] * 4, out_shape=[jax.ShapeDtypeStruct((R, C), F32)] * 4,
        compiler_params=_params(("arbitrary",)),
    )(landed, w, m, v)


def _sum_slots(name, landed):
    def kern(x_ref, o_ref):
        g = x_ref[0].astype(F32)
        for q in range(1, N_DEV):
            g = g + x_ref[q].astype(F32)
        o_ref[...] = g

    return pl.pallas_call(kern, name=name, out_shape=jax.ShapeDtypeStruct(landed.shape[1:], F32))(landed)


def _adamw_whole(name, g, w, m, v):
    def kern(g_ref, w_ref, m_ref, v_ref, d_out, m_out, v_out):
        d_out[...], m_out[...], v_out[...] = _adamw(w_ref[...], g_ref[...], m_ref[...], v_ref[...])

    return pl.pallas_call(kern, name=name, out_shape=[jax.ShapeDtypeStruct(w.shape, F32)] * 3)(g, w, m, v)


def _view2d(a):
    if a.ndim == 1:
        return a.reshape(1, -1)
    return a.reshape(-1, a.shape[-1])


def kernel(x, norm_mix_g, w_in, pool_w, pool_scale, ssm_log_neg_a_re, ssm_a_im, ssm_log_dt, ssm_b_re, ssm_b_im, ssm_c_re, ssm_c_im, ssm_d, glu_w, glu_b, out_norm_pool_g, out_norm_ssm_g, w_out, norm_ffn_g, w_up, conv_w, conv_b, w_down, final_norm_g, loss_target, m_norm_mix_g, m_w_in, m_pool_w, m_pool_scale, m_ssm_log_neg_a_re, m_ssm_a_im, m_ssm_log_dt, m_ssm_b_re, m_ssm_b_im, m_ssm_c_re, m_ssm_c_im, m_ssm_d, m_glu_w, m_glu_b, m_out_norm_pool_g, m_out_norm_ssm_g, m_w_out, m_norm_ffn_g, m_w_up, m_conv_w, m_conv_b, m_w_down, m_final_norm_g, v_norm_mix_g, v_w_in, v_pool_w, v_pool_scale, v_ssm_log_neg_a_re, v_ssm_a_im, v_ssm_log_dt, v_ssm_b_re, v_ssm_b_im, v_ssm_c_re, v_ssm_c_im, v_ssm_d, v_glu_w, v_glu_b, v_out_norm_pool_g, v_out_norm_ssm_g, v_w_out, v_norm_ffn_g, v_w_up, v_conv_w, v_conv_b, v_w_down, v_final_norm_g):
    given = dict(locals())
    weights = {n: given[n] for n in WEIGHTS}
    mom1 = {n: given["m_" + n] for n in WEIGHTS}
    mom2 = {n: given["v_" + n] for n in WEIGHTS}

    xs = x[0]
    tgt = loss_target[0]
    T, D = xs.shape
    DP = len(POOL_WINDOWS) * POOL_GROUP
    DS = D - DP
    G = DS // SSM_GROUP
    N, H = SSM_STATE, SSM_GROUP
    F2 = w_up.shape[1] * N_DEV
    F = F2 // 2
    fs = w_up.shape[1]
    row = lambda a: a.reshape(1, -1)

    early_handle, token = _exchange_start("gather_early_start", [w_in.astype(BF16), glu_w.astype(BF16)],
                                          ['gather'] * 2, after=w_in)
    late_src = [w_out.astype(BF16), w_down.astype(BF16), w_up.astype(BF16), conv_w]
    late_handle, token = _exchange_start("gather_late_start", late_src, ['gather'] * 4, after=token)
    CB = row(conv_b)

    g1, g2, g3 = row(norm_mix_g), row(norm_ffn_g), row(final_norm_g)
    gp, gs = row(out_norm_pool_g), row(out_norm_ssm_g)
    (xn,) = _rowmap("norm_mix", _rms_fwd, [xs], [g1 + token[0:1, 0:1]], [(D, BF16)])
    early = _exchange_wait("gather_early_wait", early_handle, xn)
    W_in = early[0].reshape(D, D)
    W_glu = early[1].reshape(DS, DS)
    u = _mm("proj_in", xn, W_in, 'nn', F32)
    tmp = _pick(T, (256, 128))
    pool_w_bf = pool_w.astype(BF16)
    ypn = _pool_fwd(u, pool_w_bf, row(pool_scale), gp, tmp)

    u_ssm = _to_chunk_rows(u[:, DP:])
    u_ssm_bf = u_ssm.astype(BF16)
    a_rows = (ssm_log_neg_a_re.reshape(2 * G, N), ssm_a_im.reshape(2 * G, N), ssm_log_dt.reshape(2 * G, 1))
    disc = _ssm_params_fwd(*a_rows)
    nb = G // SSM_BLOCK_GROUPS
    vecs = [d.reshape(2, nb, 1, SSM_BLOCK_GROUPS * N) for d in disc]
    b_in = [_block_diag_in(b) for b in (ssm_b_re, ssm_b_im)]
    c_out = [_block_diag_out(c) for c in (ssm_c_re, ssm_c_im)]
    mats_bf = [[m[d].astype(BF16) for m in b_in + c_out] for d in range(2)]
    mats_t_bf = [[jnp.swapaxes(m[d], 1, 2).astype(BF16) for m in b_in + c_out] for d in range(2)]
    dir_vecs = [[v[d] for v in vecs] for d in range(2)]
    y_dir = [_ssm_fwd("ssm_fwd_%d" % d, u_ssm_bf, mats_bf[d], dir_vecs[d], reverse=bool(d)) for d in range(2)]

    def mix_post(yf, yb, us, d, gw, gb, g):
        y = yf + yb + d * us
        z = _gelu(y)
        gate = _sigmoid(jnp.dot(z.astype(BF16), gw, preferred_element_type=F32) + gb)
        return _rms_fwd(z * gate, g), y

    ysn, y_ssm = _rowmap("ssm_post", mix_post, [y_dir[0], y_dir[1], u_ssm], [row(ssm_d), W_glu, row(glu_b), gs],
                         [(DS, BF16), (DS, F32)])
    ycat = jnp.concatenate([ypn, _from_chunk_rows(ysn)], axis=1)
    late = _exchange_wait("gather_late_wait", late_handle, ysn)
    W_out = late[0].reshape(D, D)
    W_down = late[1].reshape(F, D)
    W_up = late[2].transpose(1, 0, 2).reshape(D, F2)
    CW = late[3].transpose(1, 0, 2).reshape(3, F2)
    h1 = _mm("proj_out", ycat, W_out, 'nn', F32, add=xs)
    (hn,) = _rowmap("norm_ffn", _rms_fwd, [h1], [g2], [(D, BF16)])
    up = _mm("ffn_up", hn, W_up, 'nn', F32)
    act = _convact_fwd(up, CW, CB, F)
    h2 = _mm("ffn_down", act, W_down, 'nn', F32, add=h1)

    def head(h, t, g):
        r = lax.rsqrt(_rowmean(h * h) + EPS)
        hh = h * r
        e = hh * g - t
        loss = 0.5 * jnp.sum(_rowmean(e * e), keepdims=True)
        dy = e * (1.0 / D)
        dxh = dy * g
        dh = r * (dxh - hh * _rowmean(dxh * hh))
        return dh, dh, jnp.broadcast_to(loss, (1, LANES)), _colsum(dy * hh)

    dh2, dh2_bf, loss_acc, dg3 = _rowmap("loss_head", head, [h2, tgt], [g3], [(D, F32), (D, BF16)], [(1, LANES), (1, D)])

    dact = _mm("ffn_down_dx", dh2_bf, W_down, 'nt', F32)
    dW_down = _mm("ffn_down_dw", act, dh2_bf, 'tn', BF16)
    dup, dCW, dCB = _convact_bwd(up, dact, CW, CB, F)
    dhn = _mm("ffn_up_dx", dup, W_up, 'nt', F32)
    dW_up = _mm("ffn_up_dw", hn, dup, 'tn', BF16)

    def shards(full_grad):
        return full_grad.reshape((N_DEV, full_grad.shape[0] // N_DEV) + full_grad.shape[1:])

    ffn_sent = [shards(dW_down), dW_up.reshape(D, N_DEV, fs).transpose(1, 0, 2), dCW.reshape(3, N_DEV, fs).transpose(1, 0, 2)]
    ffn_handle, token = _exchange_start("scatter_ffn_start", ffn_sent, ['scatter'] * 3, after=dhn)
    g2_late = g2 + token[0:1, 0:1]

    def norm_bwd_add(hx, dy, res, g):
        dx, dg = _rms_bwd(hx, g, dy)
        return dx + res, dg

    def norm_bwd_add2(hx, dy, res, g):
        dx, dg = _rms_bwd(hx, g, dy)
        return dx + res, dx + res, dg

    dh1, dh1_bf, dg2 = _rowmap("norm_ffn_bwd", norm_bwd_add2, [h1, dhn, dh2], [g2_late], [(D, F32), (D, BF16)], [(1, D)])
    dycat = _mm("proj_out_dx", dh1_bf, W_out, 'nt', F32)
    dW_out = _mm("proj_out_dw", ycat, dh1_bf, 'tn', BF16)

    out_sent = [shards(dW_out)]
    out_handle, token = _exchange_start("scatter_out_start", out_sent, ['scatter'], after=dycat)
    q, dpooled, dpool_w, dpool_scale, dgp = _pool_bwd_rows(u, dycat, pool_w_bf, row(pool_scale),
                                                           gp + token[0:1, 0:1], tmp)
    du_pool = _pool_bwd_band(q, dpooled, tmp)

    def mix_post_bwd(y, us, dyn, d, gw, gb, g):
        z = _gelu(y)
        gz = z.astype(BF16)
        sig = _sigmoid(jnp.dot(gz, gw, preferred_element_type=F32) + gb)
        dys, dg = _rms_bwd(z * sig, g, dyn)
        dgl = dys * z * sig * (1.0 - sig)
        dgl_bf = dgl.astype(BF16)
        dz = dys * sig + lax.dot_general(dgl_bf, gw, (((1,), (1,)), ((), ())), preferred_element_type=F32)
        dgw = lax.dot_general(gz, dgl_bf, (((0,), (0,)), ((), ())), preferred_element_type=F32)
        dyv = dz * _gelu_grad(y)
        return dyv, dyv * d, dgw, _colsum(dgl), _colsum(dyv * us), dg

    d_ysn = _to_chunk_rows(dycat[:, DP:])
    dyv, du_dir, dglu_w, dglu_b, dssm_d, dgs = _rowmap(
        "ssm_post_bwd", mix_post_bwd, [y_ssm, u_ssm, d_ysn], [row(ssm_d), W_glu, row(glu_b), gs],
        [(DS, BF16), (DS, F32)], [(DS, DS), (1, DS), (1, DS), (1, DS)])

    du_ssm = du_dir
    dB, dC, dvec = [], [], []
    for d in range(2):
        b_re, b_im, c_re, c_im = mats_bf[d]
        b_re_t, b_im_t, c_re_t, c_im_t = mats_t_bf[d]
        res = _ssm_bwd("ssm_bwd_%d" % d, u_ssm_bf, dyv, du_ssm, (b_re, b_im), (b_re_t, b_im_t, c_re_t, c_im_t),
                       dir_vecs[d], reverse=bool(d))
        du_ssm = res[0]
        dB.append(res[1:3])
        dC.append(res[3:5])
        dvec.append(res[5:9])
    grad_b = [_block_diag_in_grad(jnp.stack([dB[0][k], dB[1][k]]), N, H) for k in range(2)]
    grad_c = [_block_diag_out_grad(jnp.stack([dC[0][k], dC[1][k]]), N, H) for k in range(2)]
    cots = [jnp.stack([dvec[0][k], dvec[1][k]]).reshape(2 * G, N) for k in range(4)]
    d_a_re, d_a_im, d_log_dt = _ssm_params_bwd(*a_rows, cots)

    rep_grads = {
        'pool_w': dpool_w, 'pool_scale': dpool_scale, 'ssm_log_neg_a_re': d_a_re, 'ssm_a_im': d_a_im,
        'ssm_log_dt': d_log_dt, 'ssm_b_re': grad_b[0], 'ssm_b_im': grad_b[1], 'ssm_c_re': grad_c[0], 'ssm_c_im': grad_c[1],
        'ssm_d': dssm_d, 'glu_b': dglu_b, 'out_norm_pool_g': dgp, 'out_norm_ssm_g': dgs, 'norm_ffn_g': dg2,
        'conv_b': dCB, 'final_norm_g': dg3}
    wide = ('pool_w', 'ssm_b_re', 'ssm_b_im', 'ssm_c_re', 'ssm_c_im')

    def travel(n):
        g = rep_grads[n]
        if n in wide:
            return g.reshape(-1, PACK_W if g.size % PACK_W == 0 else LANES).astype(BF16)
        return _view2d(g.reshape(weights[n].shape))

    small = tuple(n for n in REPLICATED if n != 'norm_mix_g')
    small_sent = [shards(dglu_w.astype(BF16))] + [travel(n) for n in small] + [loss_acc]
    small_handle, token = _exchange_start("reduce_small_start", small_sent, ['scatter'] + ['gather'] * (len(small) + 1),
                                          after=d_a_re)

    du = (jnp.concatenate([du_pool, _from_chunk_rows(du_ssm)], axis=1) + token[0:1, 0:1]).astype(BF16)
    dW_in = _mm("proj_in_dw", xn, du, 'tn', BF16)
    in_handle, token = _exchange_start("scatter_in_start", [shards(dW_in)], ['scatter'], after=dW_in)
    dxn = _mm("proj_in_dx", du, W_in + token[0, 0].astype(BF16), 'nt', F32)
    dx, dg1 = _rowmap("norm_mix_bwd", norm_bwd_add, [xs, dxn, dh1], [g1], [(D, F32)], [(1, D)])
    rep_grads['norm_mix_g'] = dg1
    last_handle, token = _exchange_start("reduce_last_start", [travel('norm_mix_g')], ['gather'], after=dx)

    grads, delta, new_m, new_v = {}, {}, {}, {}

    def update(n, landed_n):
        shape = weights[n].shape
        if len(shape) > 2 and shape[-1] < LANES // 2:
            grads[n] = _sum_slots("sum_" + n, landed_n).reshape(shape)
            delta[n], new_m[n], new_v[n] = _adamw_whole("adamw_" + n, grads[n], weights[n], mom1[n], mom2[n])
            return
        landed_n = landed_n.reshape((N_DEV,) + _view2d(weights[n]).shape)
        res = _sum_adamw("adamw_" + n, landed_n, _view2d(weights[n]), _view2d(mom1[n]), _view2d(mom2[n]))
        grads[n], delta[n], new_m[n], new_v[n] = [r.reshape(shape) for r in res]

    ffn_landed = _exchange_wait("scatter_ffn_wait", ffn_handle, token)
    for n, l in zip(('w_down', 'w_up', 'conv_w'), ffn_landed):
        update(n, l)
    (out_landed,) = _exchange_wait("scatter_out_wait", out_handle, token)
    update('w_out', out_landed)
    small_landed = _exchange_wait("reduce_small_wait", small_handle, token)
    for n, l in zip(('glu_w',) + small, small_landed):
        update(n, l)

    def loss_sum(parts_ref, o_ref):
        s = parts_ref[0]
        for q in range(1, N_DEV):
            s = s + parts_ref[q]
        o_ref[...] = s

    loss = pl.pallas_call(loss_sum, name="loss_sum", out_shape=jax.ShapeDtypeStruct((1, LANES), F32))(small_landed[-1])[0, 0]

    update('w_in', _exchange_wait("scatter_in_wait", in_handle, grads[small[-1]])[0])
    update('norm_mix_g', _exchange_wait("reduce_last_wait", last_handle, grads['w_in'])[0])

    return (loss, dx[None], *[grads[n] for n in WEIGHTS], *[delta[n] for n in WEIGHTS],
            *[new_m[n] for n in WEIGHTS], *[new_v[n] for n in WEIGHTS])
```

```python
import functools

import jax
import jax.numpy as jnp
import numpy as np
from jax import lax
from jax.experimental import pallas as pl
from jax.experimental.pallas import tpu as pltpu

F32 = jnp.float32
BF16 = jnp.bfloat16
MESH_ID = pl.DeviceIdType.MESH

N_DEV = 8
EPS = 1e-6
POOL_WINDOWS = (2, 4, 8, 16)
POOL_GROUP = 128
POOL_PAD = 64
SSM_GROUP = 16
SSM_STATE = 64
SSM_BLOCK_GROUPS = 8
N_CHUNK = 8
STEP_BLOCK = 64
STEP_SHIFT = 6
ROW_BLOCK = N_CHUNK * STEP_BLOCK
GROUP_SHIFT = 4
STATE_SHIFT = 6
_NT = (((1,), (1,)), ((), ()))
_TN = (((0,), (0,)), ((), ()))
LANES = 128
PACK_W = 1024
VMEM_LIMIT = 56 * 1024 * 1024
MM_VMEM_BUDGET = 40 * 1024 * 1024
MM_TILE_CAP = 1408
ADAM_TILE_BYTES = 4 * 1024 * 1024

ADAM_LR = 0.001
ADAM_B1 = 0.9
ADAM_B2 = 0.999
ADAM_EPS = 1e-08
ADAM_WD = 0.01
ADAM_STEP = 10

WEIGHTS = ['norm_mix_g', 'w_in', 'pool_w', 'pool_scale', 'ssm_log_neg_a_re', 'ssm_a_im', 'ssm_log_dt',
           'ssm_b_re', 'ssm_b_im', 'ssm_c_re', 'ssm_c_im', 'ssm_d', 'glu_w', 'glu_b', 'out_norm_pool_g',
           'out_norm_ssm_g', 'w_out', 'norm_ffn_g', 'w_up', 'conv_w', 'conv_b', 'w_down', 'final_norm_g']
SHARDED = ('w_in', 'w_out', 'w_down', 'glu_w', 'w_up', 'conv_w')
REPLICATED = tuple(n for n in WEIGHTS if n not in SHARDED)


def _pick(n, prefs):
    for p in prefs:
        if n % p == 0:
            return p
    return n


def _params(sem, vmem=None):
    return pltpu.CompilerParams(dimension_semantics=sem, vmem_limit_bytes=vmem or VMEM_LIMIT)


def _tiles(n, cap):
    return [d for d in range(LANES, min(n, cap) + 1, LANES) if n % d == 0] or [n]


def _mm(name, a, b, mode, out_dtype, add=None):
    if mode == 'nn':
        (M, K), (_, N) = a.shape, b.shape
    elif mode == 'nt':
        (M, K), (N, _) = a.shape, b.shape
    else:
        (K, M), (_, N) = a.shape, b.shape
    dims = {'nn': (((1,), (0,)), ((), ())), 'nt': (((1,), (1,)), ((), ())), 'tn': (((0,), (0,)), ((), ()))}[mode]
    sa, sb, so = a.dtype.itemsize, b.dtype.itemsize, jnp.dtype(out_dtype).itemsize
    best = None
    for tm in _tiles(M, MM_TILE_CAP):
        for tn in _tiles(N, MM_TILE_CAP):
            need = 2 * (tm * K * sa + tn * K * sb + tm * tn * (so + (4 if add is not None else 0)))
            if need <= MM_VMEM_BUDGET:
                key = (tm * tn / (tm + tn), tm * tn)
                if best is None or key > best[0]:
                    best = (key, tm, tn)
    _, tm, tn = best
    rows_inner = a.size * sa * (N // tn) + b.size * sb < a.size * sa + b.size * sb * (M // tm)

    def kern(*refs):
        a_ref, b_ref = refs[:2]
        o_ref = refs[-1]
        r = lax.dot_general(a_ref[...].astype(BF16), b_ref[...].astype(BF16), dims, preferred_element_type=F32)
        if add is not None:
            r = r + refs[2][...]
        o_ref[...] = r.astype(o_ref.dtype)

    if rows_inner:
        grid = (N // tn, M // tm)
        ij = lambda g0, g1: (g1, g0)
    else:
        grid = (M // tm, N // tn)
        ij = lambda g0, g1: (g0, g1)
    a_spec = (pl.BlockSpec((K, tm), lambda g0, g1: (0, ij(g0, g1)[0])) if mode == 'tn'
              else pl.BlockSpec((tm, K), lambda g0, g1: (ij(g0, g1)[0], 0)))
    b_spec = (pl.BlockSpec((tn, K), lambda g0, g1: (ij(g0, g1)[1], 0)) if mode == 'nt'
              else pl.BlockSpec((K, tn), lambda g0, g1: (0, ij(g0, g1)[1])))
    in_specs = [a_spec, b_spec]
    args = [a, b]
    if add is not None:
        in_specs.append(pl.BlockSpec((tm, tn), lambda g0, g1: ij(g0, g1)))
        args.append(add)
    return pl.pallas_call(
        kern, name=name, grid=grid, in_specs=in_specs,
        out_specs=pl.BlockSpec((tm, tn), lambda g0, g1: ij(g0, g1)),
        out_shape=jax.ShapeDtypeStruct((M, N), out_dtype),
        compiler_params=_params(("parallel", "parallel")),
    )(*args)


def _rowmap(name, fn, rows, fulls, out_rows, out_accs=(), tm=256):
    rows = [r if isinstance(r, tuple) else (r, r.shape[1], 0) for r in rows]
    T = rows[0][0].shape[0]
    tm = min(tm, T)
    assert T % tm == 0
    n_in, n_row = len(rows) + len(fulls), len(out_rows)

    def kern(*refs):
        i = pl.program_id(0)
        res = fn(*[r[...] for r in refs[:n_in]])
        res = res if isinstance(res, (tuple, list)) else (res,)
        outs = refs[n_in:]
        for o, v in zip(outs[:n_row], res[:n_row]):
            o[...] = v.astype(o.dtype)
        for o, v in zip(outs[n_row:], res[n_row:]):
            @pl.when(i == 0)
            def _(o=o, v=v):
                o[...] = v

            @pl.when(i != 0)
            def _(o=o, v=v):
                o[...] += v

    def full_spec(shape):
        nd = len(shape)
        return pl.BlockSpec(tuple(shape), lambda i: (0,) * nd)

    in_specs = [pl.BlockSpec((tm, w), lambda i, k=k: (i, k)) for _, w, k in rows] + [full_spec(f.shape) for f in fulls]
    out_specs = [pl.BlockSpec((tm, c), lambda i: (i, 0)) for c, _ in out_rows] + [full_spec(s) for s in out_accs]
    out_shape = [jax.ShapeDtypeStruct((T, c), dt) for c, dt in out_rows] + [jax.ShapeDtypeStruct(tuple(s), F32) for s in out_accs]
    return pl.pallas_call(
        kern, name=name, grid=(T // tm,), in_specs=in_specs, out_specs=out_specs, out_shape=out_shape,
        compiler_params=_params(("arbitrary",)),
    )(*[r for r, _, _ in rows], *fulls)


def _colsum(v):
    return jnp.sum(v, axis=0, keepdims=True)


def _rowmean(v):
    return jnp.mean(v, axis=-1, keepdims=True)


def _rms_fwd(x, g):
    r = lax.rsqrt(_rowmean(x * x) + EPS)
    return x * r * g


def _rms_bwd(x, g, dy):
    r = lax.rsqrt(_rowmean(x * x) + EPS)
    xh = x * r
    dxh = dy * g
    return r * (dxh - xh * _rowmean(dxh * xh)), _colsum(dy * xh)


def _sigmoid(v):
    return 0.5 * jnp.tanh(0.5 * v) + 0.5


def _gelu(y):
    c = np.sqrt(2.0 / np.pi).astype(np.float32)
    return 0.5 * y * (1.0 + jnp.tanh(c * (y + 0.044715 * (y * y * y))))


def _gelu_grad(y):
    c = np.sqrt(2.0 / np.pi).astype(np.float32)
    th = jnp.tanh(c * (y + 0.044715 * (y * y * y)))
    return 0.5 * (1.0 + th) + 0.5 * y * (1.0 - th * th) * c * (1.0 + 3.0 * 0.044715 * (y * y))


def _split3(v):
    hi = v.astype(BF16)
    r1 = v - hi.astype(F32)
    mid = r1.astype(BF16)
    lo = (r1 - mid.astype(F32)).astype(BF16)
    return hi, mid, lo


def _band(tm, lo_off, hi_off):
    shape = (tm, tm + 2 * POOL_PAD)
    d = lax.broadcasted_iota(jnp.int32, shape, 1) - lax.broadcasted_iota(jnp.int32, shape, 0) - POOL_PAD
    return ((d >= lo_off) & (d < hi_off)).astype(BF16)


def _band_sum(band, ext):
    hi, mid, lo = _split3(ext)
    dot = functools.partial(jnp.dot, preferred_element_type=F32)
    return dot(band, hi) + dot(band, mid) + dot(band, lo)


def _window_count(r0, tm, half, T):
    t = r0 + lax.broadcasted_iota(jnp.int32, (tm, 1), 0)
    return (jnp.minimum(t + half, T) - jnp.maximum(t - half, 0)).astype(F32)


def _halo_specs(tm, C, T):
    per = tm // POOL_PAD
    last = T // POOL_PAD - 1
    return [pl.BlockSpec((POOL_PAD, C), lambda i: (jnp.maximum(i * per - 1, 0), 0)),
            pl.BlockSpec((tm, C), lambda i: (i, 0)),
            pl.BlockSpec((POOL_PAD, C), lambda i: (jnp.minimum((i + 1) * per, last), 0))]


def _with_halo(before_ref, main_ref, after_ref, i, n):
    before = jnp.where(i > 0, before_ref[...], 0.0)
    after = jnp.where(i < n - 1, after_ref[...], 0.0)
    return jnp.concatenate([before, main_ref[...], after], axis=0)


def _pool_block(ext, ctr, w_ref, r0, tm, T):
    pooled, conc = [], []
    for gi, w in enumerate(POOL_WINDOWS):
        half = w // 2
        cols = slice(gi * POOL_GROUP, (gi + 1) * POOL_GROUP)
        ws = _band_sum(_band(tm, -half, half), ext[:, cols])
        p = ws / _window_count(r0, tm, half, T) - ctr[:, cols]
        pooled.append(p)
        conc.append(jnp.dot(p.astype(BF16), w_ref[gi], preferred_element_type=F32))
    return pooled, jnp.concatenate(conc, axis=1)


def _pool_fwd(u, pool_w_bf, pool_scale, g_pool, tm):
    T = u.shape[0]
    nw = len(POOL_WINDOWS)
    C = nw * POOL_GROUP
    n = T // tm

    def kern(ub_ref, u_ref, ua_ref, w_ref, sc_ref, g_ref, o_ref):
        i = pl.program_id(0)
        r0 = pl.multiple_of(i * tm, tm)
        _, conc = _pool_block(_with_halo(ub_ref, u_ref, ua_ref, i, n), u_ref[...], w_ref, r0, tm, T)
        o_ref[...] = _rms_fwd(conc * sc_ref[...], g_ref[...]).astype(o_ref.dtype)

    return pl.pallas_call(
        kern, name="pool_fwd", grid=(n,),
        in_specs=_halo_specs(tm, C, T) + [pl.BlockSpec(pool_w_bf.shape, lambda i: (0, 0, 0)),
                                          pl.BlockSpec((1, C), lambda i: (0, 0)), pl.BlockSpec((1, C), lambda i: (0, 0))],
        out_specs=pl.BlockSpec((tm, C), lambda i: (i, 0)),
        out_shape=jax.ShapeDtypeStruct((T, C), BF16),
        compiler_params=_params(("arbitrary",)),
    )(u, u, u, pool_w_bf, pool_scale, g_pool)


def _pool_bwd_rows(u, d_y, pool_w_bf, pool_scale, g_pool, tm):
    T = u.shape[0]
    nw = len(POOL_WINDOWS)
    C = nw * POOL_GROUP
    n = T // tm

    def kern(ub_ref, u_ref, ua_ref, dy_ref, w_ref, sc_ref, g_ref, q_ref, dp_ref, dw_ref, dsc_ref, dg_ref):
        i = pl.program_id(0)
        r0 = pl.multiple_of(i * tm, tm)
        pooled, conc = _pool_block(_with_halo(ub_ref, u_ref, ua_ref, i, n), u_ref[...], w_ref, r0, tm, T)
        sc = sc_ref[...]
        dyp, dg = _rms_bwd(conc * sc, g_ref[...], dy_ref[...])
        dsc = _colsum(dyp * conc)
        dconc = (dyp * sc).astype(BF16)
        dws = []
        for gi, w in enumerate(POOL_WINDOWS):
            cols = slice(gi * POOL_GROUP, (gi + 1) * POOL_GROUP)
            dc = dconc[:, cols]
            dp = lax.dot_general(dc, w_ref[gi], (((1,), (1,)), ((), ())), preferred_element_type=F32)
            dp_ref[:, cols] = dp
            q_ref[:, cols] = dp / _window_count(r0, tm, w // 2, T)
            dws.append(lax.dot_general(pooled[gi].astype(BF16), dc, (((0,), (0,)), ((), ())), preferred_element_type=F32))

        @pl.when(i == 0)
        def _():
            for gi in range(nw):
                dw_ref[gi] = dws[gi]
            dsc_ref[...] = dsc
            dg_ref[...] = dg

        @pl.when(i != 0)
        def _():
            for gi in range(nw):
                dw_ref[gi] += dws[gi]
            dsc_ref[...] += dsc
            dg_ref[...] += dg

    full2 = pl.BlockSpec((1, C), lambda i: (0, 0))
    row_blk = pl.BlockSpec((tm, C), lambda i: (i, 0))
    return pl.pallas_call(
        kern, name="pool_bwd_rows", grid=(n,),
        in_specs=_halo_specs(tm, C, T) + [row_blk, pl.BlockSpec(pool_w_bf.shape, lambda i: (0, 0, 0)), full2, full2],
        out_specs=[row_blk, row_blk, pl.BlockSpec((nw, POOL_GROUP, POOL_GROUP), lambda i: (0, 0, 0)), full2, full2],
        out_shape=[jax.ShapeDtypeStruct((T, C), F32), jax.ShapeDtypeStruct((T, C), F32),
                   jax.ShapeDtypeStruct((nw, POOL_GROUP, POOL_GROUP), F32),
                   jax.ShapeDtypeStruct((1, C), F32), jax.ShapeDtypeStruct((1, C), F32)],
        compiler_params=_params(("arbitrary",)),
    )(u, u, u, d_y, pool_w_bf, pool_scale, g_pool)


def _pool_bwd_band(q, dpooled, tm):
    T, C = dpooled.shape
    n = T // tm

    def kern(qb_ref, q_ref, qa_ref, dp_ref, o_ref):
        ext = _with_halo(qb_ref, q_ref, qa_ref, pl.program_id(0), n)
        for gi, w in enumerate(POOL_WINDOWS):
            half = w // 2
            cols = slice(gi * POOL_GROUP, (gi + 1) * POOL_GROUP)
            o_ref[:, cols] = _band_sum(_band(tm, -half + 1, half + 1), ext[:, cols]) - dp_ref[:, cols]

    return pl.pallas_call(
        kern, name="pool_bwd_band", grid=(n,),
        in_specs=_halo_specs(tm, C, T) + [pl.BlockSpec((tm, C), lambda i: (i, 0))],
        out_specs=pl.BlockSpec((tm, C), lambda i: (i, 0)),
        out_shape=jax.ShapeDtypeStruct((T, C), F32),
        compiler_params=_params(("arbitrary",)),
    )(q, q, q, dpooled)


def _cmul(ar, ai, br, bi):
    return ar * br - ai * bi, ar * bi + ai * br


def _ssm_discretise(log_neg_a_re, a_im, log_dt):
    dt = jnp.exp(log_dt)
    a_re = -jnp.exp(log_neg_a_re)
    mag = jnp.exp(a_re * dt)
    ang = a_im * dt
    lam_re, lam_im = mag * jnp.cos(ang), mag * jnp.sin(ang)
    den = a_re * a_re + a_im * a_im
    f_re = ((lam_re - 1.0) * a_re + lam_im * a_im) / den
    f_im = (lam_im * a_re - (lam_re - 1.0) * a_im) / den
    return lam_re, lam_im, f_re, f_im


def _ssm_params_fwd(log_neg_a_re, a_im, log_dt):
    rows, n = log_neg_a_re.shape

    def kern(a_ref, b_ref, c_ref, o1, o2, o3, o4):
        for o, v in zip((o1, o2, o3, o4), _ssm_discretise(a_ref[...], b_ref[...], c_ref[...])):
            o[...] = v

    return pl.pallas_call(kern, name="ssm_params_fwd", out_shape=[jax.ShapeDtypeStruct((rows, n), F32)] * 4)(
        log_neg_a_re, a_im, log_dt)


def _ssm_params_bwd(log_neg_a_re, a_im, log_dt, cots):
    rows, n = log_neg_a_re.shape

    def kern(a_ref, b_ref, c_ref, g1, g2, g3, g4, o1, o2, o3):
        _, vjp = jax.vjp(_ssm_discretise, a_ref[...], b_ref[...], c_ref[...])
        d1, d2, d3 = vjp((g1[...], g2[...], g3[...], g4[...]))
        o1[...] = d1
        o2[...] = d2
        o3[...] = d3

    return pl.pallas_call(
        kern, name="ssm_params_bwd",
        out_shape=[jax.ShapeDtypeStruct((rows, n), F32), jax.ShapeDtypeStruct((rows, n), F32),
                   jax.ShapeDtypeStruct((rows, 1), F32)])(log_neg_a_re, a_im, log_dt, *cots)


def _cpow(lr, li, n):
    out = None
    br, bi = lr, li
    while n:
        if n & 1:
            out = (br, bi) if out is None else _cmul(out[0], out[1], br, bi)
        n >>= 1
        if n:
            br, bi = _cmul(br, bi, br, bi)
    return out


def _slab(t):
    return pl.ds(pl.multiple_of(t * N_CHUNK, N_CHUNK), N_CHUNK)


def _shift_chunks(vr, vi, reverse):
    sub = lax.broadcasted_iota(jnp.int32, vr.shape, 0)
    if reverse:
        keep = sub != N_CHUNK - 1
        return jnp.where(keep, pltpu.roll(vr, N_CHUNK - 1, 0), 0.0), jnp.where(keep, pltpu.roll(vi, N_CHUNK - 1, 0), 0.0)
    keep = sub != 0
    return jnp.where(keep, pltpu.roll(vr, 1, 0), 0.0), jnp.where(keep, pltpu.roll(vi, 1, 0), 0.0)


def _chunk_scan(xr, xi, lr, li, Lc, reverse, prev=None):
    C = xr.shape[1]
    lrb, lib = jnp.broadcast_to(lr, (N_CHUNK, C)), jnp.broadcast_to(li, (N_CHUNK, C))
    zero = jnp.zeros((N_CHUNK, C), F32)

    def step_of(k):
        return (Lc - 1 - k) if reverse else k

    def advance(sr, si, t):
        return lrb * sr - lib * si + xr[_slab(t), :], lrb * si + lib * sr + xi[_slab(t), :]

    def pass1(k, c):
        return advance(c[0], c[1], step_of(k))

    er, ei = lax.fori_loop(0, Lc, pass1, (zero, zero), unroll=4)
    pr, pi = _cpow(lr, li, Lc)
    prb, pib = jnp.broadcast_to(pr, (N_CHUNK, C)), jnp.broadcast_to(pi, (N_CHUNK, C))
    cr, ci = zero, zero
    for _ in range(N_CHUNK - 1):
        cr, ci = _shift_chunks(er + prb * cr - pib * ci, ei + prb * ci + pib * cr, reverse)

    if prev is None:
        def pass2(k, c):
            t = step_of(k)
            nr, ni = advance(c[0], c[1], t)
            xr[_slab(t), :] = nr
            xi[_slab(t), :] = ni
            return nr, ni

        lax.fori_loop(0, Lc, pass2, (cr, ci), unroll=4)
        return None

    qr, qi = prev
    def pair(ar, ai, wr, wi, acc):
        return acc[0] + ar * wr + ai * wi, acc[1] + ai * wr - ar * wi

    def pass2(k, c):
        sr, si, accr, acci = c
        t = step_of(k)
        nr, ni = advance(sr, si, t)
        xr[_slab(t), :] = nr
        xi[_slab(t), :] = ni
        tp = t - 1 if reverse else t + 1
        accr, acci = pair(nr, ni, qr[_slab(tp), :], qi[_slab(tp), :], (accr, acci))
        return nr, ni, accr, acci

    sr, si, accr, acci = lax.fori_loop(0, Lc - 1, pass2, (cr, ci, zero, zero), unroll=4)
    t = step_of(Lc - 1)
    nr, ni = advance(sr, si, t)
    xr[_slab(t), :] = nr
    xi[_slab(t), :] = ni
    edge = step_of(0)
    wr, wi = _shift_chunks(qr[_slab(edge), :], qi[_slab(edge), :], not reverse)
    return pair(nr, ni, wr, wi, (accr, acci))


def _ssm_dims(T):
    assert T % ROW_BLOCK == 0
    return T // N_CHUNK, T // ROW_BLOCK


def _order_matrices():
    shape = (ROW_BLOCK, ROW_BLOCK)
    a = lax.broadcasted_iota(jnp.int32, shape, 0)
    b = lax.broadcasted_iota(jnp.int32, shape, 1)
    to_chunk = a == N_CHUNK * (b & (STEP_BLOCK - 1)) + (b >> STEP_SHIFT)
    to_time = b == N_CHUNK * (a & (STEP_BLOCK - 1)) + (a >> STEP_SHIFT)
    return to_chunk.astype(BF16), to_time.astype(BF16)


def _strips(ref, blk, Lc):
    return jnp.concatenate([ref[pl.ds(j * Lc + blk * STEP_BLOCK, STEP_BLOCK), :] for j in range(N_CHUNK)], axis=0)


def _put_strips(ref, blk, Lc, val):
    for j in range(N_CHUNK):
        ref[pl.ds(j * Lc + blk * STEP_BLOCK, STEP_BLOCK), :] = val[j * STEP_BLOCK:(j + 1) * STEP_BLOCK]


def _exact_dot(sel, val):
    hi, mid, lo = _split3(val)
    dot = functools.partial(jnp.dot, preferred_element_type=F32)
    return dot(sel, hi) + dot(sel, mid) + dot(sel, lo)


def _group_mask():
    shape = (SSM_BLOCK_GROUPS * SSM_GROUP, SSM_BLOCK_GROUPS * SSM_STATE)
    rows = lax.broadcasted_iota(jnp.int32, shape, 0) >> GROUP_SHIFT
    cols = lax.broadcasted_iota(jnp.int32, shape, 1) >> STATE_SHIFT
    return rows == cols


def _expand_groups(compact):
    return jnp.where(_group_mask(), jnp.tile(compact, (SSM_BLOCK_GROUPS, 1)), 0.0).astype(BF16)


def _collect_groups(block):
    kept = jnp.where(_group_mask(), block, 0.0)
    out = kept[0:SSM_GROUP]
    for g in range(1, SSM_BLOCK_GROUPS):
        out = out + kept[g * SSM_GROUP:(g + 1) * SSM_GROUP]
    return out


def _ssm_specs(T, first_block, direction):
    SC = SSM_BLOCK_GROUPS * SSM_STATE
    col_u = pl.BlockSpec((T, LANES), lambda m: (0, first_block + m))
    col = pl.BlockSpec((T, LANES), lambda m: (0, m))
    mat = pl.BlockSpec((None, None, SSM_GROUP, SC), lambda m: (direction, m, 0, 0))
    vec = pl.BlockSpec((None, None, 1, SC), lambda m: (direction, m, 0, 0))
    return col_u, col, mat, vec


def _ssm_project_in(u_ref, bre, bim, to_chunk, xr, xi, Lc, nblk):
    for blk in range(nblk):
        rows = pl.ds(blk * ROW_BLOCK, ROW_BLOCK)
        ub = jnp.dot(to_chunk, _strips(u_ref, blk, Lc).astype(BF16), preferred_element_type=F32).astype(BF16)
        xr[rows, :] = jnp.dot(ub, bre, preferred_element_type=F32)
        xi[rows, :] = jnp.dot(ub, bim, preferred_element_type=F32)


def _ssm_fwd(name, u, mats, vecs, direction):
    T, D = u.shape
    SC = SSM_BLOCK_GROUPS * SSM_STATE
    nb = mats[0].shape[1]
    W = nb * LANES
    Lc, nblk = _ssm_dims(T)
    reverse = bool(direction)

    def kern(u_ref, bre_ref, bim_ref, cre_ref, cim_ref, lr_ref, li_ref, fr_ref, fi_ref, y_ref, xr, xi):
        to_chunk, to_time = _order_matrices()
        _ssm_project_in(u_ref, _expand_groups(bre_ref[...]), _expand_groups(bim_ref[...]), to_chunk, xr, xi, Lc, nblk)
        _chunk_scan(xr, xi, lr_ref[...], li_ref[...], Lc, reverse)
        fr, fi = fr_ref[...], fi_ref[...]
        cre, cim = _expand_groups(cre_ref[...]), _expand_groups(cim_ref[...])
        for blk in range(nblk):
            rows = pl.ds(blk * ROW_BLOCK, ROW_BLOCK)
            sr, si = _cmul(fr, fi, xr[rows, :], xi[rows, :])
            y = (lax.dot_general(sr.astype(BF16), cre, _NT, preferred_element_type=F32)
                 - lax.dot_general(si.astype(BF16), cim, _NT, preferred_element_type=F32))
            _put_strips(y_ref, blk, Lc, _exact_dot(to_time, y))

    col_u, col, mat, vec = _ssm_specs(T, (D - W) // LANES, direction)
    return pl.pallas_call(
        kern, name=name, grid=(nb,),
        in_specs=[col_u, mat, mat, mat, mat, vec, vec, vec, vec],
        out_specs=col, out_shape=jax.ShapeDtypeStruct((T, W), F32),
        scratch_shapes=[pltpu.VMEM((T, SC), F32), pltpu.VMEM((T, SC), F32)],
        compiler_params=_params(("arbitrary",)),
    )(u, *mats, *vecs)


def _ssm_bwd(name, u, dy_bf, du_in, mats, vecs, direction):
    T, D = u.shape
    SC = SSM_BLOCK_GROUPS * SSM_STATE
    nb = mats[0].shape[1]
    W = nb * LANES
    Lc, nblk = _ssm_dims(T)
    reverse = bool(direction)

    def kern(u_ref, dy_ref, dui_ref, bre_ref, bim_ref, cre_ref, cim_ref, lr_ref, li_ref, fr_ref, fi_ref,
             du_ref, dbre_ref, dbim_ref, dcre_ref, dcim_ref, dlr_ref, dli_ref, dfr_ref, dfi_ref,
             xr, xi, gr, gi):
        to_chunk, to_time = _order_matrices()
        lr, li, fr, fi = lr_ref[...], li_ref[...], fr_ref[...], fi_ref[...]
        bre, bim = _expand_groups(bre_ref[...]), _expand_groups(bim_ref[...])
        cre, cim = _expand_groups(cre_ref[...]), _expand_groups(cim_ref[...])
        _ssm_project_in(u_ref, bre, bim, to_chunk, xr, xi, Lc, nblk)
        _chunk_scan(xr, xi, lr, li, Lc, reverse)
        mre = jnp.zeros((LANES, SC), F32)
        mim = jnp.zeros((LANES, SC), F32)
        dfr = jnp.zeros((1, SC), F32)
        dfi = jnp.zeros((1, SC), F32)
        for blk in range(nblk):
            rows = pl.ds(blk * ROW_BLOCK, ROW_BLOCK)
            dy = jnp.dot(to_chunk, _strips(dy_ref, blk, Lc), preferred_element_type=F32).astype(BF16)
            x_r, x_i = xr[rows, :], xi[rows, :]
            g_r = jnp.dot(dy, cre, preferred_element_type=F32)
            g_i = -jnp.dot(dy, cim, preferred_element_type=F32)
            mre += lax.dot_general(dy, x_r.astype(BF16), _TN, preferred_element_type=F32)
            mim += lax.dot_general(dy, x_i.astype(BF16), _TN, preferred_element_type=F32)
            dfr += _colsum(g_r * x_r + g_i * x_i)
            dfi += _colsum(g_i * x_r - g_r * x_i)
            gr[rows, :] = fr * g_r + fi * g_i
            gi[rows, :] = fr * g_i - fi * g_r
        dcre_ref[...] = _collect_groups(fr * mre - fi * mim)
        dcim_ref[...] = _collect_groups(-(fr * mim + fi * mre))
        dfr_ref[...] = dfr
        dfi_ref[...] = dfi
        accr, acci = _chunk_scan(gr, gi, lr, -li, Lc, not reverse, prev=(xr, xi))
        dlr_ref[...] = _colsum(accr)
        dli_ref[...] = _colsum(acci)
        dbre = jnp.zeros((LANES, SC), F32)
        dbim = jnp.zeros((LANES, SC), F32)
        for blk in range(nblk):
            rows = pl.ds(blk * ROW_BLOCK, ROW_BLOCK)
            ub = jnp.dot(to_chunk, _strips(u_ref, blk, Lc).astype(BF16), preferred_element_type=F32).astype(BF16)
            a_r, a_i = gr[rows, :].astype(BF16), gi[rows, :].astype(BF16)
            dbre += lax.dot_general(ub, a_r, _TN, preferred_element_type=F32)
            dbim += lax.dot_general(ub, a_i, _TN, preferred_element_type=F32)
            du = (lax.dot_general(a_r, bre, _NT, preferred_element_type=F32)
                  + lax.dot_general(a_i, bim, _NT, preferred_element_type=F32))
            _put_strips(du_ref, blk, Lc, _strips(dui_ref, blk, Lc) + _exact_dot(to_time, du))
        dbre_ref[...] = _collect_groups(dbre)
        dbim_ref[...] = _collect_groups(dbim)

    col_u, col, mat, vec = _ssm_specs(T, (D - W) // LANES, direction)
    mat_out = pl.BlockSpec((None, SSM_GROUP, SC), lambda m: (m, 0, 0))
    vec_out = pl.BlockSpec((None, 1, SC), lambda m: (m, 0, 0))
    mat_shape = jax.ShapeDtypeStruct((nb, SSM_GROUP, SC), F32)
    vec_shape = jax.ShapeDtypeStruct((nb, 1, SC), F32)
    return pl.pallas_call(
        kern, name=name, grid=(nb,),
        in_specs=[col_u, col, col, mat, mat, mat, mat, vec, vec, vec, vec],
        out_specs=[col, mat_out, mat_out, mat_out, mat_out, vec_out, vec_out, vec_out, vec_out],
        out_shape=[jax.ShapeDtypeStruct((T, W), F32), mat_shape, mat_shape, mat_shape, mat_shape,
                   vec_shape, vec_shape, vec_shape, vec_shape],
        scratch_shapes=[pltpu.VMEM((T, SC), F32)] * 4,
        compiler_params=_params(("arbitrary",)),
    )(u, dy_bf, du_in, *mats, *vecs)


def _groups_side_by_side(p, channel_axis):
    d, G = p.shape[:2]
    nb = G // SSM_BLOCK_GROUPS
    v = p.reshape(d, nb, SSM_BLOCK_GROUPS, p.shape[2], p.shape[3])
    v = v.transpose(0, 1, 4, 2, 3) if channel_axis == 3 else v.transpose(0, 1, 3, 2, 4)
    return v.reshape(d, nb, SSM_GROUP, SSM_BLOCK_GROUPS * SSM_STATE)


def _groups_apart(m, channel_axis):
    d, nb = m.shape[:2]
    v = m.reshape(d, nb, SSM_GROUP, SSM_BLOCK_GROUPS, SSM_STATE)
    v = v.transpose(0, 1, 3, 4, 2) if channel_axis == 3 else v.transpose(0, 1, 3, 2, 4)
    return v.reshape((d, nb * SSM_BLOCK_GROUPS) + v.shape[3:])


def _conv_taps(ref, c, R, T):
    n = T // R
    r0 = pl.multiple_of(c * R, R)
    main = ref[pl.ds(r0, R), :]
    before = ref[pl.ds(pl.multiple_of(jnp.maximum(r0 - 8, 0), 8), 8), :]
    after = ref[pl.ds(pl.multiple_of(jnp.minimum(r0 + R, T - 8), 8), 8), :]
    ext = jnp.concatenate([jnp.where(c > 0, before, 0.0), main, jnp.where(c < n - 1, after, 0.0)], axis=0)
    m1 = pltpu.roll(ext, 1, 0)[8:8 + R]
    p1 = pltpu.roll(ext, R + 15, 0)[8:8 + R]
    return m1, main, p1, r0


def _conv_specs(T, F, cb):
    nj = F // cb
    lo = lambda rows: pl.BlockSpec((rows, cb), lambda j: (0, j))
    hi = lambda rows: pl.BlockSpec((rows, cb), lambda j: (0, j + nj))
    return nj, lo, hi


def _convact_fwd(up, conv_w, conv_b, F):
    T = up.shape[0]
    cb = _pick(F, (256, 128))
    R = _pick(T, (256, 128, 64))
    nj, lo, hi = _conv_specs(T, F, cb)

    def kern(uv, ug, wv, wg, bv, bg, o_ref):
        def body(c, carry):
            v1, v0, v2, r0 = _conv_taps(uv, c, R, T)
            g1, g0, g2, _ = _conv_taps(ug, c, R, T)
            val = v1 * wv[0:1, :] + v0 * wv[1:2, :] + v2 * wv[2:3, :] + bv[...]
            gate = g1 * wg[0:1, :] + g0 * wg[1:2, :] + g2 * wg[2:3, :] + bg[...]
            o_ref[pl.ds(r0, R), :] = (val * (gate * _sigmoid(gate))).astype(o_ref.dtype)
            return carry

        lax.fori_loop(0, T // R, body, 0)

    return pl.pallas_call(
        kern, name="convact_fwd", grid=(nj,),
        in_specs=[lo(T), hi(T), lo(3), hi(3), lo(1), hi(1)],
        out_specs=lo(T), out_shape=jax.ShapeDtypeStruct((T, F), BF16),
        compiler_params=_params(("arbitrary",)),
    )(up, up, conv_w, conv_w, conv_b, conv_b)


def _convact_bwd(up, dact, conv_w, conv_b, F):
    T = up.shape[0]
    cb = _pick(F, (256, 128))
    R = _pick(T, (256, 128, 64))
    nj, lo, hi = _conv_specs(T, F, cb)

    def kern(uv, ug, da, wv, wg, bv, bg, dupv, dupg, dwv, dwg, dbv, dbg, sv, sg):
        zero = jnp.zeros((1, cb), F32)

        def pass_a(c, acc):
            v1, v0, v2, r0 = _conv_taps(uv, c, R, T)
            g1, g0, g2, _ = _conv_taps(ug, c, R, T)
            val = v1 * wv[0:1, :] + v0 * wv[1:2, :] + v2 * wv[2:3, :] + bv[...]
            gate = g1 * wg[0:1, :] + g0 * wg[1:2, :] + g2 * wg[2:3, :] + bg[...]
            sig = _sigmoid(gate)
            d = da[pl.ds(r0, R), :]
            dval = d * (gate * sig)
            dgate = d * val * (sig * (1.0 + gate * (1.0 - sig)))
            sv[pl.ds(r0, R), :] = dval
            sg[pl.ds(r0, R), :] = dgate
            terms = (dval * v1, dval * v0, dval * v2, dval, dgate * g1, dgate * g0, dgate * g2, dgate)
            return tuple(a + _colsum(t) for a, t in zip(acc, terms))

        acc = lax.fori_loop(0, T // R, pass_a, (zero,) * 8)
        for k in range(3):
            dwv[k:k + 1, :] = acc[k]
            dwg[k:k + 1, :] = acc[4 + k]
        dbv[...] = acc[3]
        dbg[...] = acc[7]

        def pass_b(c, carry):
            v1, v0, v2, r0 = _conv_taps(sv, c, R, T)
            g1, g0, g2, _ = _conv_taps(sg, c, R, T)
            dupv[pl.ds(r0, R), :] = (v2 * wv[0:1, :] + v0 * wv[1:2, :] + v1 * wv[2:3, :]).astype(dupv.dtype)
            dupg[pl.ds(r0, R), :] = (g2 * wg[0:1, :] + g0 * wg[1:2, :] + g1 * wg[2:3, :]).astype(dupg.dtype)
            return carry

        lax.fori_loop(0, T // R, pass_b, 0)

    dupv, dupg, dwv, dwg, dbv, dbg = pl.pallas_call(
        kern, name="convact_bwd", grid=(nj,),
        in_specs=[lo(T), hi(T), lo(T), lo(3), hi(3), lo(1), hi(1)],
        out_specs=[lo(T), lo(T), lo(3), lo(3), lo(1), lo(1)],
        out_shape=[jax.ShapeDtypeStruct((T, F), BF16), jax.ShapeDtypeStruct((T, F), BF16),
                   jax.ShapeDtypeStruct((3, F), F32), jax.ShapeDtypeStruct((3, F), F32),
                   jax.ShapeDtypeStruct((1, F), F32), jax.ShapeDtypeStruct((1, F), F32)],
        scratch_shapes=[pltpu.VMEM((T, cb), F32), pltpu.VMEM((T, cb), F32)],
        compiler_params=_params(("arbitrary",)),
    )(up, up, dact, conv_w, conv_w, conv_b, conv_b)
    return (jnp.concatenate([dupv, dupg], axis=1), jnp.concatenate([dwv, dwg], axis=1),
            jnp.concatenate([dbv, dbg], axis=1))


def _peers(x, y, c, with_self=False):
    out = []
    for k in range(0 if with_self else 1, N_DEV):
        px = 1 - x if k & 4 else x
        py = 1 - y if k & 2 else y
        pc = 1 - c if k & 1 else c
        out.append((k, (px, py, pc), 4 * px + 2 * py + pc))
    return out


def _exchange_start(name, bufs, modes, after):
    n = len(bufs)
    ncopy = n * N_DEV
    lands = [lax.empty((N_DEV,) + tuple(b.shape[-2:]), b.dtype) for b in bufs]

    def body(*refs):
        ins, land_in = refs[:n], refs[n:2 * n]
        send_sems, recv_sems = refs[2 * n + 1], refs[2 * n + 2]
        token = refs[-1]
        x, y, c = lax.axis_index("x"), lax.axis_index("y"), lax.axis_index("c")
        me = 4 * x + 2 * y + c
        for b in range(n):
            for k, peer, slot in _peers(x, y, c, with_self=True):
                sem = b * N_DEV + k
                pltpu.make_async_remote_copy(
                    src_ref=ins[b] if modes[b] == 'gather' else ins[b].at[slot], dst_ref=land_in[b].at[me],
                    send_sem=send_sems.at[sem], recv_sem=recv_sems.at[sem],
                    device_id=peer, device_id_type=MESH_ID).start()
        token[...] = jnp.zeros_like(token)

    hbm = pl.BlockSpec(memory_space=pltpu.HBM)
    sem_spec = pl.BlockSpec(memory_space=pltpu.SEMAPHORE)
    operands = [pltpu.with_memory_space_constraint(a, pltpu.HBM) for a in list(bufs) + lands]
    res = pl.pallas_call(
        body, name=name,
        out_shape=(pltpu.SemaphoreType.DMA((ncopy,)), pltpu.SemaphoreType.DMA((ncopy,)),
                   *[pltpu.HBM(a.shape, a.dtype) for a in operands], jax.ShapeDtypeStruct((8, LANES), F32)),
        in_specs=[hbm] * (2 * n) + [pl.BlockSpec(memory_space=pl.ANY)],
        out_specs=(sem_spec, sem_spec, *([hbm] * (2 * n)), pl.BlockSpec(memory_space=pltpu.VMEM)),
        input_output_aliases={i: 2 + i for i in range(2 * n)},
        compiler_params=pltpu.CompilerParams(has_side_effects=pltpu.SideEffectType.DATAFLOW_SIDE_EFFECTING),
    )(*operands, after)
    return (res[0], res[1], list(res[2:2 + n]), list(res[2 + n:2 + 2 * n]), tuple(modes)), res[-1]


def _exchange_wait(name, handle, after):
    send_sems, recv_sems, srcs, lands, modes = handle
    n = len(srcs)

    def body(*refs):
        src_in, land_in = refs[:n], refs[n:2 * n]
        send_ref, recv_ref = refs[2 * n], refs[2 * n + 1]
        x, y, c = lax.axis_index("x"), lax.axis_index("y"), lax.axis_index("c")
        for b in range(n):
            for k, peer, slot in _peers(x, y, c, with_self=True):
                sem = b * N_DEV + k
                copy = pltpu.make_async_remote_copy(
                    src_ref=src_in[b] if modes[b] == 'gather' else src_in[b].at[slot], dst_ref=land_in[b].at[slot],
                    send_sem=send_ref.at[sem], recv_sem=recv_ref.at[sem],
                    device_id=peer, device_id_type=MESH_ID)
                copy.wait_send()
                copy.wait_recv()

    hbm = pl.BlockSpec(memory_space=pltpu.HBM)
    sem_spec = pl.BlockSpec(memory_space=pltpu.SEMAPHORE)
    res = pl.pallas_call(
        body, name=name,
        out_shape=tuple(pltpu.HBM(a.shape, a.dtype) for a in srcs + lands),
        in_specs=[hbm] * (2 * n) + [sem_spec, sem_spec, pl.BlockSpec(memory_space=pl.ANY)],
        out_specs=tuple([hbm] * (2 * n)),
        input_output_aliases={i: i for i in range(2 * n)},
        compiler_params=pltpu.CompilerParams(has_side_effects=pltpu.SideEffectType.DATAFLOW_SIDE_EFFECTING),
    )(*srcs, *lands, send_sems, recv_sems, after)
    return list(res[n:])


def _adamw(w, g, m, v):
    m2 = ADAM_B1 * m + (1.0 - ADAM_B1) * g
    v2 = ADAM_B2 * v + (1.0 - ADAM_B2) * (g * g)
    m_hat = m2 / (1.0 - ADAM_B1 ** ADAM_STEP)
    v_hat = v2 / (1.0 - ADAM_B2 ** ADAM_STEP)
    return -ADAM_LR * (m_hat / (jnp.sqrt(v_hat) + ADAM_EPS) + ADAM_WD * w), m2, v2


def _sum_adamw(name, landed, w, m, v):
    R, C = w.shape
    row_bytes = N_DEV * (-(-C // LANES) * LANES) * landed.dtype.itemsize
    tiles = [d for d in range(16, R, 16) if R % d == 0 and d * row_bytes <= ADAM_TILE_BYTES]
    tr = max(tiles) if tiles and R * row_bytes > ADAM_TILE_BYTES else R

    def kern(x_ref, w_ref, m_ref, v_ref, g_out, d_out, m_out, v_out):
        g = x_ref[0].astype(F32)
        for q in range(1, N_DEV):
            g = g + x_ref[q].astype(F32)
        g_out[...] = g
        d_out[...], m_out[...], v_out[...] = _adamw(w_ref[...], g, m_ref[...], v_ref[...])

    blk = pl.BlockSpec((tr, C), lambda i: (i, 0))
    return pl.pallas_call(
        kern, name=name, grid=(R // tr,),
        in_specs=[pl.BlockSpec((N_DEV, tr, C), lambda i: (0, i, 0)), blk, blk, blk],
        out_specs=[blk] * 4, out_shape=[jax.ShapeDtypeStruct((R, C), F32)] * 4,
        compiler_params=_params(("arbitrary",)),
    )(landed, w, m, v)


def _sum_slots(name, landed):
    def kern(x_ref, o_ref):
        g = x_ref[0].astype(F32)
        for q in range(1, N_DEV):
            g = g + x_ref[q].astype(F32)
        o_ref[...] = g

    return pl.pallas_call(kern, name=name, out_shape=jax.ShapeDtypeStruct(landed.shape[1:], F32))(landed)


def _adamw_whole(name, g, w, m, v):
    def kern(g_ref, w_ref, m_ref, v_ref, d_out, m_out, v_out):
        d_out[...], m_out[...], v_out[...] = _adamw(w_ref[...], g_ref[...], m_ref[...], v_ref[...])

    return pl.pallas_call(kern, name=name, out_shape=[jax.ShapeDtypeStruct(w.shape, F32)] * 3)(g, w, m, v)


def _view2d(a):
    if a.ndim == 1:
        return a.reshape(1, -1)
    return a.reshape(-1, a.shape[-1])


def kernel(x, norm_mix_g, w_in, pool_w, pool_scale, ssm_log_neg_a_re, ssm_a_im, ssm_log_dt, ssm_b_re, ssm_b_im, ssm_c_re, ssm_c_im, ssm_d, glu_w, glu_b, out_norm_pool_g, out_norm_ssm_g, w_out, norm_ffn_g, w_up, conv_w, conv_b, w_down, final_norm_g, loss_target, m_norm_mix_g, m_w_in, m_pool_w, m_pool_scale, m_ssm_log_neg_a_re, m_ssm_a_im, m_ssm_log_dt, m_ssm_b_re, m_ssm_b_im, m_ssm_c_re, m_ssm_c_im, m_ssm_d, m_glu_w, m_glu_b, m_out_norm_pool_g, m_out_norm_ssm_g, m_w_out, m_norm_ffn_g, m_w_up, m_conv_w, m_conv_b, m_w_down, m_final_norm_g, v_norm_mix_g, v_w_in, v_pool_w, v_pool_scale, v_ssm_log_neg_a_re, v_ssm_a_im, v_ssm_log_dt, v_ssm_b_re, v_ssm_b_im, v_ssm_c_re, v_ssm_c_im, v_ssm_d, v_glu_w, v_glu_b, v_out_norm_pool_g, v_out_norm_ssm_g, v_w_out, v_norm_ffn_g, v_w_up, v_conv_w, v_conv_b, v_w_down, v_final_norm_g):
    given = dict(locals())
    weights = {n: given[n] for n in WEIGHTS}
    mom1 = {n: given["m_" + n] for n in WEIGHTS}
    mom2 = {n: given["v_" + n] for n in WEIGHTS}

    xs = x[0]
    tgt = loss_target[0]
    T, D = xs.shape
    DP = len(POOL_WINDOWS) * POOL_GROUP
    DS = D - DP
    G = DS // SSM_GROUP
    N, H = SSM_STATE, SSM_GROUP
    F2 = w_up.shape[1] * N_DEV
    F = F2 // 2
    fs = w_up.shape[1]
    row = lambda a: a.reshape(1, -1)

    early_handle, token = _exchange_start("gather_early_start", [w_in.astype(BF16), glu_w.astype(BF16)],
                                          ['gather'] * 2, after=w_in)
    late_src = [w_out.astype(BF16), w_down.astype(BF16), w_up.astype(BF16), conv_w]
    late_handle, token = _exchange_start("gather_late_start", late_src, ['gather'] * 4, after=token)
    CB = row(conv_b)

    g1, g2, g3 = row(norm_mix_g), row(norm_ffn_g), row(final_norm_g)
    gp, gs = row(out_norm_pool_g), row(out_norm_ssm_g)
    (xn,) = _rowmap("norm_mix", _rms_fwd, [xs], [g1 + token[0:1, 0:1]], [(D, BF16)])
    early = _exchange_wait("gather_early_wait", early_handle, xn)
    W_in = early[0].reshape(D, D)
    W_glu = early[1].reshape(DS, DS)
    u = _mm("proj_in", xn, W_in, 'nn', F32)
    tmp = _pick(T, (256, 128))
    pool_w_bf = pool_w.astype(BF16)
    ypn = _pool_fwd(u, pool_w_bf, row(pool_scale), gp, tmp)

    u_ssm = (u, DS, DP // DS)
    a_rows = (ssm_log_neg_a_re.reshape(2 * G, N), ssm_a_im.reshape(2 * G, N), ssm_log_dt.reshape(2 * G, 1))
    disc = _ssm_params_fwd(*a_rows)
    nb = G // SSM_BLOCK_GROUPS
    vecs = [d.reshape(2, nb, 1, SSM_BLOCK_GROUPS * N) for d in disc]
    mats = [_groups_side_by_side(ssm_b_re, 3), _groups_side_by_side(ssm_b_im, 3),
            _groups_side_by_side(ssm_c_re, 2), _groups_side_by_side(ssm_c_im, 2)]
    y_dir = [_ssm_fwd("ssm_fwd_%d" % d, u, mats, vecs, d) for d in range(2)]

    def mix_post(yf, yb, us, d, gw, gb, g):
        y = yf + yb + d * us
        z = _gelu(y)
        gate = _sigmoid(jnp.dot(z.astype(BF16), gw, preferred_element_type=F32) + gb)
        return _rms_fwd(z * gate, g), y

    ysn, y_ssm = _rowmap("ssm_post", mix_post, [y_dir[0], y_dir[1], u_ssm], [row(ssm_d), W_glu, row(glu_b), gs],
                         [(DS, BF16), (DS, F32)])
    ycat = jnp.concatenate([ypn, ysn], axis=1)
    late = _exchange_wait("gather_late_wait", late_handle, ysn)
    W_out = late[0].reshape(D, D)
    W_down = late[1].reshape(F, D)
    W_up = late[2].transpose(1, 0, 2).reshape(D, F2)
    CW = late[3].transpose(1, 0, 2).reshape(3, F2)
    h1 = _mm("proj_out", ycat, W_out, 'nn', F32, add=xs)
    (hn,) = _rowmap("norm_ffn", _rms_fwd, [h1], [g2], [(D, BF16)])
    up = _mm("ffn_up", hn, W_up, 'nn', F32)
    act = _convact_fwd(up, CW, CB, F)
    h2 = _mm("ffn_down", act, W_down, 'nn', F32, add=h1)

    def head(h, t, g):
        r = lax.rsqrt(_rowmean(h * h) + EPS)
        hh = h * r
        e = hh * g - t
        loss = 0.5 * jnp.sum(_rowmean(e * e), keepdims=True)
        dy = e * (1.0 / D)
        dxh = dy * g
        dh = r * (dxh - hh * _rowmean(dxh * hh))
        return dh, dh, jnp.broadcast_to(loss, (1, LANES)), _colsum(dy * hh)

    dh2, dh2_bf, loss_acc, dg3 = _rowmap("loss_head", head, [h2, tgt], [g3], [(D, F32), (D, BF16)], [(1, LANES), (1, D)])

    dact = _mm("ffn_down_dx", dh2_bf, W_down, 'nt', F32)
    dW_down = _mm("ffn_down_dw", act, dh2_bf, 'tn', BF16)
    dup, dCW, dCB = _convact_bwd(up, dact, CW, CB, F)
    dhn = _mm("ffn_up_dx", dup, W_up, 'nt', F32)
    dW_up = _mm("ffn_up_dw", hn, dup, 'tn', BF16)

    def shards(full_grad):
        return full_grad.reshape((N_DEV, full_grad.shape[0] // N_DEV) + full_grad.shape[1:])

    ffn_sent = [shards(dW_down), dW_up.reshape(D, N_DEV, fs).transpose(1, 0, 2), dCW.reshape(3, N_DEV, fs).transpose(1, 0, 2)]
    ffn_handle, token = _exchange_start("scatter_ffn_start", ffn_sent, ['scatter'] * 3, after=dhn)
    g2_late = g2 + token[0:1, 0:1]

    def norm_bwd_add(hx, dy, res, g):
        dx, dg = _rms_bwd(hx, g, dy)
        return dx + res, dg

    def norm_bwd_add2(hx, dy, res, g):
        dx, dg = _rms_bwd(hx, g, dy)
        return dx + res, dx + res, dg

    dh1, dh1_bf, dg2 = _rowmap("norm_ffn_bwd", norm_bwd_add2, [h1, dhn, dh2], [g2_late], [(D, F32), (D, BF16)], [(1, D)])
    dycat = _mm("proj_out_dx", dh1_bf, W_out, 'nt', F32)
    dW_out = _mm("proj_out_dw", ycat, dh1_bf, 'tn', BF16)

    out_sent = [shards(dW_out)]
    out_handle, token = _exchange_start("scatter_out_start", out_sent, ['scatter'], after=dycat)
    q, dpooled, dpool_w, dpool_scale, dgp = _pool_bwd_rows(u, dycat, pool_w_bf, row(pool_scale),
                                                           gp + token[0:1, 0:1], tmp)
    du_pool = _pool_bwd_band(q, dpooled, tmp)

    def mix_post_bwd(y, us, dyn, d, gw, gb, g):
        z = _gelu(y)
        gz = z.astype(BF16)
        sig = _sigmoid(jnp.dot(gz, gw, preferred_element_type=F32) + gb)
        dys, dg = _rms_bwd(z * sig, g, dyn)
        dgl = dys * z * sig * (1.0 - sig)
        dgl_bf = dgl.astype(BF16)
        dz = dys * sig + lax.dot_general(dgl_bf, gw, (((1,), (1,)), ((), ())), preferred_element_type=F32)
        dgw = lax.dot_general(gz, dgl_bf, (((0,), (0,)), ((), ())), preferred_element_type=F32)
        dyv = dz * _gelu_grad(y)
        return dyv, dyv * d, dgw, _colsum(dgl), _colsum(dyv * us), dg

    dyv, du_dir, dglu_w, dglu_b, dssm_d, dgs = _rowmap(
        "ssm_post_bwd", mix_post_bwd, [y_ssm, u_ssm, (dycat, DS, DP // DS)], [row(ssm_d), W_glu, row(glu_b), gs],
        [(DS, BF16), (DS, F32)], [(DS, DS), (1, DS), (1, DS), (1, DS)])

    du_ssm = du_dir
    dB, dC, dvec = [], [], []
    for d in range(2):
        res = _ssm_bwd("ssm_bwd_%d" % d, u, dyv, du_ssm, mats, vecs, d)
        du_ssm = res[0]
        dB.append(res[1:3])
        dC.append(res[3:5])
        dvec.append(res[5:9])
    grad_b = [_groups_apart(jnp.stack([dB[0][k], dB[1][k]]), 3) for k in range(2)]
    grad_c = [_groups_apart(jnp.stack([dC[0][k], dC[1][k]]), 2) for k in range(2)]
    cots = [jnp.stack([dvec[0][k], dvec[1][k]]).reshape(2 * G, N) for k in range(4)]
    d_a_re, d_a_im, d_log_dt = _ssm_params_bwd(*a_rows, cots)

    rep_grads = {
        'pool_w': dpool_w, 'pool_scale': dpool_scale, 'ssm_log_neg_a_re': d_a_re, 'ssm_a_im': d_a_im,
        'ssm_log_dt': d_log_dt, 'ssm_b_re': grad_b[0], 'ssm_b_im': grad_b[1], 'ssm_c_re': grad_c[0], 'ssm_c_im': grad_c[1],
        'ssm_d': dssm_d, 'glu_b': dglu_b, 'out_norm_pool_g': dgp, 'out_norm_ssm_g': dgs, 'norm_ffn_g': dg2,
        'conv_b': dCB, 'final_norm_g': dg3}
    wide = ('pool_w', 'ssm_b_re', 'ssm_b_im', 'ssm_c_re', 'ssm_c_im')

    def travel(n):
        g = rep_grads[n]
        if n in wide:
            return g.reshape(-1, PACK_W if g.size % PACK_W == 0 else LANES).astype(BF16)
        return _view2d(g.reshape(weights[n].shape))

    small = tuple(n for n in REPLICATED if n != 'norm_mix_g')
    small_sent = [shards(dglu_w.astype(BF16))] + [travel(n) for n in small] + [loss_acc]
    small_handle, token = _exchange_start("reduce_small_start", small_sent, ['scatter'] + ['gather'] * (len(small) + 1),
                                          after=d_a_re)

    du = (jnp.concatenate([du_pool, du_ssm], axis=1) + token[0:1, 0:1]).astype(BF16)
    dW_in = _mm("proj_in_dw", xn, du, 'tn', BF16)
    in_handle, token = _exchange_start("scatter_in_start", [shards(dW_in)], ['scatter'], after=dW_in)
    dxn = _mm("proj_in_dx", du, W_in + token[0, 0].astype(BF16), 'nt', F32)
    dx, dg1 = _rowmap("norm_mix_bwd", norm_bwd_add, [xs, dxn, dh1], [g1], [(D, F32)], [(1, D)])
    rep_grads['norm_mix_g'] = dg1
    last_handle, token = _exchange_start("reduce_last_start", [travel('norm_mix_g')], ['gather'], after=dx)

    grads, delta, new_m, new_v = {}, {}, {}, {}

    def update(n, landed_n):
        shape = weights[n].shape
        if len(shape) > 2 and shape[-1] < LANES // 2:
            grads[n] = _sum_slots("sum_" + n, landed_n).reshape(shape)
            delta[n], new_m[n], new_v[n] = _adamw_whole("adamw_" + n, grads[n], weights[n], mom1[n], mom2[n])
            return
        landed_n = landed_n.reshape((N_DEV,) + _view2d(weights[n]).shape)
        res = _sum_adamw("adamw_" + n, landed_n, _view2d(weights[n]), _view2d(mom1[n]), _view2d(mom2[n]))
        grads[n], delta[n], new_m[n], new_v[n] = [r.reshape(shape) for r in res]

    ffn_landed = _exchange_wait("scatter_ffn_wait", ffn_handle, token)
    for n, l in zip(('w_down', 'w_up', 'conv_w'), ffn_landed):
        update(n, l)
    (out_landed,) = _exchange_wait("scatter_out_wait", out_handle, token)
    update('w_out', out_landed)
    small_landed = _exchange_wait("reduce_small_wait", small_handle, token)
    for n, l in zip(('glu_w',) + small, small_landed):
        update(n, l)

    def loss_sum(parts_ref, o_ref):
        s = parts_ref[0]
        for q in range(1, N_DEV):
            s = s + parts_ref[q]
        o_ref[...] = s

    loss = pl.pallas_call(loss_sum, name="loss_sum", out_shape=jax.ShapeDtypeStruct((1, LANES), F32))(small_landed[-1])[0, 0]

    update('w_in', _exchange_wait("scatter_in_wait", in_handle, grads[small[-1]])[0])
    update('norm_mix_g', _exchange_wait("reduce_last_wait", last_handle, grads['w_in'])[0])

    return (loss, dx[None], *[grads[n] for n in WEIGHTS], *[delta[n] for n in WEIGHTS],
            *[new_m[n] for n in WEIGHTS], *[new_v[n] for n in WEIGHTS])
```

```python
import functools

import jax
import jax.numpy as jnp
import numpy as np
from jax import lax
from jax.experimental import pallas as pl
from jax.experimental.pallas import tpu as pltpu

F32 = jnp.float32
BF16 = jnp.bfloat16
MESH_ID = pl.DeviceIdType.MESH

N_DEV = 8
EPS = 1e-6
POOL_WINDOWS = (2, 4, 8, 16)
POOL_GROUP = 128
POOL_PAD = 64
SSM_GROUP = 16
SSM_STATE = 64
SSM_BLOCK_GROUPS = 8
N_CHUNK = 8
ROW_BLOCK = 512
GROUP_SHIFT = 4
STATE_SHIFT = 6
_NT = (((1,), (1,)), ((), ()))
_TN = (((0,), (0,)), ((), ()))
LANES = 128
PACK_W = 1024
VMEM_LIMIT = 56 * 1024 * 1024
MM_VMEM_BUDGET = 40 * 1024 * 1024
MM_TILE_CAP = 1408
ADAM_TILE_BYTES = 4 * 1024 * 1024

ADAM_LR = 0.001
ADAM_B1 = 0.9
ADAM_B2 = 0.999
ADAM_EPS = 1e-08
ADAM_WD = 0.01
ADAM_STEP = 10

WEIGHTS = ['norm_mix_g', 'w_in', 'pool_w', 'pool_scale', 'ssm_log_neg_a_re', 'ssm_a_im', 'ssm_log_dt',
           'ssm_b_re', 'ssm_b_im', 'ssm_c_re', 'ssm_c_im', 'ssm_d', 'glu_w', 'glu_b', 'out_norm_pool_g',
           'out_norm_ssm_g', 'w_out', 'norm_ffn_g', 'w_up', 'conv_w', 'conv_b', 'w_down', 'final_norm_g']
SHARDED = ('w_in', 'w_out', 'w_down', 'glu_w', 'w_up', 'conv_w')
REPLICATED = tuple(n for n in WEIGHTS if n not in SHARDED)


def _pick(n, prefs):
    for p in prefs:
        if n % p == 0:
            return p
    return n


def _params(sem, vmem=None):
    return pltpu.CompilerParams(dimension_semantics=sem, vmem_limit_bytes=vmem or VMEM_LIMIT)


def _tiles(n, cap):
    return [d for d in range(LANES, min(n, cap) + 1, LANES) if n % d == 0] or [n]


def _mm(name, a, b, mode, out_dtype, add=None):
    if mode == 'nn':
        (M, K), (_, N) = a.shape, b.shape
    elif mode == 'nt':
        (M, K), (N, _) = a.shape, b.shape
    else:
        (K, M), (_, N) = a.shape, b.shape
    dims = {'nn': (((1,), (0,)), ((), ())), 'nt': (((1,), (1,)), ((), ())), 'tn': (((0,), (0,)), ((), ()))}[mode]
    sa, sb, so = a.dtype.itemsize, b.dtype.itemsize, jnp.dtype(out_dtype).itemsize
    best = None
    for tm in _tiles(M, MM_TILE_CAP):
        for tn in _tiles(N, MM_TILE_CAP):
            need = 2 * (tm * K * sa + tn * K * sb + tm * tn * (so + (4 if add is not None else 0)))
            if need <= MM_VMEM_BUDGET:
                key = (tm * tn / (tm + tn), tm * tn)
                if best is None or key > best[0]:
                    best = (key, tm, tn)
    _, tm, tn = best
    rows_inner = a.size * sa * (N // tn) + b.size * sb < a.size * sa + b.size * sb * (M // tm)

    def kern(*refs):
        a_ref, b_ref = refs[:2]
        o_ref = refs[-1]
        r = lax.dot_general(a_ref[...].astype(BF16), b_ref[...].astype(BF16), dims, preferred_element_type=F32)
        if add is not None:
            r = r + refs[2][...]
        o_ref[...] = r.astype(o_ref.dtype)

    if rows_inner:
        grid = (N // tn, M // tm)
        ij = lambda g0, g1: (g1, g0)
    else:
        grid = (M // tm, N // tn)
        ij = lambda g0, g1: (g0, g1)
    a_spec = (pl.BlockSpec((K, tm), lambda g0, g1: (0, ij(g0, g1)[0])) if mode == 'tn'
              else pl.BlockSpec((tm, K), lambda g0, g1: (ij(g0, g1)[0], 0)))
    b_spec = (pl.BlockSpec((tn, K), lambda g0, g1: (ij(g0, g1)[1], 0)) if mode == 'nt'
              else pl.BlockSpec((K, tn), lambda g0, g1: (0, ij(g0, g1)[1])))
    in_specs = [a_spec, b_spec]
    args = [a, b]
    if add is not None:
        in_specs.append(pl.BlockSpec((tm, tn), lambda g0, g1: ij(g0, g1)))
        args.append(add)
    return pl.pallas_call(
        kern, name=name, grid=grid, in_specs=in_specs,
        out_specs=pl.BlockSpec((tm, tn), lambda g0, g1: ij(g0, g1)),
        out_shape=jax.ShapeDtypeStruct((M, N), out_dtype),
        compiler_params=_params(("parallel", "parallel")),
    )(*args)


def _rowmap(name, fn, rows, fulls, out_rows, out_accs=(), tm=256):
    rows = [r if isinstance(r, tuple) else (r, r.shape[1], 0) for r in rows]
    T = rows[0][0].shape[0]
    tm = min(tm, T)
    assert T % tm == 0
    n_in, n_row = len(rows) + len(fulls), len(out_rows)

    def kern(*refs):
        i = pl.program_id(0)
        res = fn(*[r[...] for r in refs[:n_in]])
        res = res if isinstance(res, (tuple, list)) else (res,)
        outs = refs[n_in:]
        for o, v in zip(outs[:n_row], res[:n_row]):
            o[...] = v.astype(o.dtype)
        for o, v in zip(outs[n_row:], res[n_row:]):
            @pl.when(i == 0)
            def _(o=o, v=v):
                o[...] = v

            @pl.when(i != 0)
            def _(o=o, v=v):
                o[...] += v

    def full_spec(shape):
        nd = len(shape)
        return pl.BlockSpec(tuple(shape), lambda i: (0,) * nd)

    in_specs = [pl.BlockSpec((tm, w), lambda i, k=k: (i, k)) for _, w, k in rows] + [full_spec(f.shape) for f in fulls]
    out_specs = [pl.BlockSpec((tm, c), lambda i: (i, 0)) for c, _ in out_rows] + [full_spec(s) for s in out_accs]
    out_shape = [jax.ShapeDtypeStruct((T, c), dt) for c, dt in out_rows] + [jax.ShapeDtypeStruct(tuple(s), F32) for s in out_accs]
    return pl.pallas_call(
        kern, name=name, grid=(T // tm,), in_specs=in_specs, out_specs=out_specs, out_shape=out_shape,
        compiler_params=_params(("arbitrary",)),
    )(*[r for r, _, _ in rows], *fulls)


def _colsum(v):
    return jnp.sum(v, axis=0, keepdims=True)


def _rowmean(v):
    return jnp.mean(v, axis=-1, keepdims=True)


def _rms_fwd(x, g):
    r = lax.rsqrt(_rowmean(x * x) + EPS)
    return x * r * g


def _rms_bwd(x, g, dy):
    r = lax.rsqrt(_rowmean(x * x) + EPS)
    xh = x * r
    dxh = dy * g
    return r * (dxh - xh * _rowmean(dxh * xh)), _colsum(dy * xh)


def _sigmoid(v):
    return 0.5 * jnp.tanh(0.5 * v) + 0.5


def _gelu(y):
    c = np.sqrt(2.0 / np.pi).astype(np.float32)
    return 0.5 * y * (1.0 + jnp.tanh(c * (y + 0.044715 * (y * y * y))))


def _gelu_grad(y):
    c = np.sqrt(2.0 / np.pi).astype(np.float32)
    th = jnp.tanh(c * (y + 0.044715 * (y * y * y)))
    return 0.5 * (1.0 + th) + 0.5 * y * (1.0 - th * th) * c * (1.0 + 3.0 * 0.044715 * (y * y))


def _split3(v):
    hi = v.astype(BF16)
    r1 = v - hi.astype(F32)
    mid = r1.astype(BF16)
    lo = (r1 - mid.astype(F32)).astype(BF16)
    return hi, mid, lo


def _band(tm, lo_off, hi_off):
    shape = (tm, tm + 2 * POOL_PAD)
    d = lax.broadcasted_iota(jnp.int32, shape, 1) - lax.broadcasted_iota(jnp.int32, shape, 0) - POOL_PAD
    return ((d >= lo_off) & (d < hi_off)).astype(BF16)


def _band_sum(band, ext):
    hi, mid, lo = _split3(ext)
    dot = functools.partial(jnp.dot, preferred_element_type=F32)
    return dot(band, hi) + dot(band, mid) + dot(band, lo)


def _window_count(r0, tm, half, T):
    t = r0 + lax.broadcasted_iota(jnp.int32, (tm, 1), 0)
    return (jnp.minimum(t + half, T) - jnp.maximum(t - half, 0)).astype(F32)


def _halo_specs(tm, C, T):
    per = tm // POOL_PAD
    last = T // POOL_PAD - 1
    return [pl.BlockSpec((POOL_PAD, C), lambda i: (jnp.maximum(i * per - 1, 0), 0)),
            pl.BlockSpec((tm, C), lambda i: (i, 0)),
            pl.BlockSpec((POOL_PAD, C), lambda i: (jnp.minimum((i + 1) * per, last), 0))]


def _with_halo(before_ref, main_ref, after_ref, i, n):
    before = jnp.where(i > 0, before_ref[...], 0.0)
    after = jnp.where(i < n - 1, after_ref[...], 0.0)
    return jnp.concatenate([before, main_ref[...], after], axis=0)


def _pool_block(ext, ctr, w_ref, r0, tm, T):
    pooled, conc = [], []
    for gi, w in enumerate(POOL_WINDOWS):
        half = w // 2
        cols = slice(gi * POOL_GROUP, (gi + 1) * POOL_GROUP)
        ws = _band_sum(_band(tm, -half, half), ext[:, cols])
        p = ws / _window_count(r0, tm, half, T) - ctr[:, cols]
        pooled.append(p)
        conc.append(jnp.dot(p.astype(BF16), w_ref[gi], preferred_element_type=F32))
    return pooled, jnp.concatenate(conc, axis=1)


def _pool_fwd(u, pool_w_bf, pool_scale, g_pool, tm):
    T = u.shape[0]
    nw = len(POOL_WINDOWS)
    C = nw * POOL_GROUP
    n = T // tm

    def kern(ub_ref, u_ref, ua_ref, w_ref, sc_ref, g_ref, o_ref):
        i = pl.program_id(0)
        r0 = pl.multiple_of(i * tm, tm)
        _, conc = _pool_block(_with_halo(ub_ref, u_ref, ua_ref, i, n), u_ref[...], w_ref, r0, tm, T)
        o_ref[...] = _rms_fwd(conc * sc_ref[...], g_ref[...]).astype(o_ref.dtype)

    return pl.pallas_call(
        kern, name="pool_fwd", grid=(n,),
        in_specs=_halo_specs(tm, C, T) + [pl.BlockSpec(pool_w_bf.shape, lambda i: (0, 0, 0)),
                                          pl.BlockSpec((1, C), lambda i: (0, 0)), pl.BlockSpec((1, C), lambda i: (0, 0))],
        out_specs=pl.BlockSpec((tm, C), lambda i: (i, 0)),
        out_shape=jax.ShapeDtypeStruct((T, C), BF16),
        compiler_params=_params(("arbitrary",)),
    )(u, u, u, pool_w_bf, pool_scale, g_pool)


def _pool_bwd_rows(u, d_y, pool_w_bf, pool_scale, g_pool, tm):
    T = u.shape[0]
    nw = len(POOL_WINDOWS)
    C = nw * POOL_GROUP
    n = T // tm

    def kern(ub_ref, u_ref, ua_ref, dy_ref, w_ref, sc_ref, g_ref, q_ref, dp_ref, dw_ref, dsc_ref, dg_ref):
        i = pl.program_id(0)
        r0 = pl.multiple_of(i * tm, tm)
        pooled, conc = _pool_block(_with_halo(ub_ref, u_ref, ua_ref, i, n), u_ref[...], w_ref, r0, tm, T)
        sc = sc_ref[...]
        dyp, dg = _rms_bwd(conc * sc, g_ref[...], dy_ref[...])
        dsc = _colsum(dyp * conc)
        dconc = (dyp * sc).astype(BF16)
        dws = []
        for gi, w in enumerate(POOL_WINDOWS):
            cols = slice(gi * POOL_GROUP, (gi + 1) * POOL_GROUP)
            dc = dconc[:, cols]
            dp = lax.dot_general(dc, w_ref[gi], (((1,), (1,)), ((), ())), preferred_element_type=F32)
            dp_ref[:, cols] = dp
            q_ref[:, cols] = dp / _window_count(r0, tm, w // 2, T)
            dws.append(lax.dot_general(pooled[gi].astype(BF16), dc, (((0,), (0,)), ((), ())), preferred_element_type=F32))

        @pl.when(i == 0)
        def _():
            for gi in range(nw):
                dw_ref[gi] = dws[gi]
            dsc_ref[...] = dsc
            dg_ref[...] = dg

        @pl.when(i != 0)
        def _():
            for gi in range(nw):
                dw_ref[gi] += dws[gi]
            dsc_ref[...] += dsc
            dg_ref[...] += dg

    full2 = pl.BlockSpec((1, C), lambda i: (0, 0))
    row_blk = pl.BlockSpec((tm, C), lambda i: (i, 0))
    return pl.pallas_call(
        kern, name="pool_bwd_rows", grid=(n,),
        in_specs=_halo_specs(tm, C, T) + [row_blk, pl.BlockSpec(pool_w_bf.shape, lambda i: (0, 0, 0)), full2, full2],
        out_specs=[row_blk, row_blk, pl.BlockSpec((nw, POOL_GROUP, POOL_GROUP), lambda i: (0, 0, 0)), full2, full2],
        out_shape=[jax.ShapeDtypeStruct((T, C), F32), jax.ShapeDtypeStruct((T, C), F32),
                   jax.ShapeDtypeStruct((nw, POOL_GROUP, POOL_GROUP), F32),
                   jax.ShapeDtypeStruct((1, C), F32), jax.ShapeDtypeStruct((1, C), F32)],
        compiler_params=_params(("arbitrary",)),
    )(u, u, u, d_y, pool_w_bf, pool_scale, g_pool)


def _pool_bwd_band(q, dpooled, tm):
    T, C = dpooled.shape
    n = T // tm

    def kern(qb_ref, q_ref, qa_ref, dp_ref, o_ref):
        ext = _with_halo(qb_ref, q_ref, qa_ref, pl.program_id(0), n)
        for gi, w in enumerate(POOL_WINDOWS):
            half = w // 2
            cols = slice(gi * POOL_GROUP, (gi + 1) * POOL_GROUP)
            o_ref[:, cols] = _band_sum(_band(tm, -half + 1, half + 1), ext[:, cols]) - dp_ref[:, cols]

    return pl.pallas_call(
        kern, name="pool_bwd_band", grid=(n,),
        in_specs=_halo_specs(tm, C, T) + [pl.BlockSpec((tm, C), lambda i: (i, 0))],
        out_specs=pl.BlockSpec((tm, C), lambda i: (i, 0)),
        out_shape=jax.ShapeDtypeStruct((T, C), F32),
        compiler_params=_params(("arbitrary",)),
    )(q, q, q, dpooled)


def _cmul(ar, ai, br, bi):
    return ar * br - ai * bi, ar * bi + ai * br


def _ssm_discretise(log_neg_a_re, a_im, log_dt):
    dt = jnp.exp(log_dt)
    a_re = -jnp.exp(log_neg_a_re)
    mag = jnp.exp(a_re * dt)
    ang = a_im * dt
    lam_re, lam_im = mag * jnp.cos(ang), mag * jnp.sin(ang)
    den = a_re * a_re + a_im * a_im
    f_re = ((lam_re - 1.0) * a_re + lam_im * a_im) / den
    f_im = (lam_im * a_re - (lam_re - 1.0) * a_im) / den
    return lam_re, lam_im, f_re, f_im


def _ssm_params_fwd(log_neg_a_re, a_im, log_dt):
    rows, n = log_neg_a_re.shape

    def kern(a_ref, b_ref, c_ref, o1, o2, o3, o4):
        for o, v in zip((o1, o2, o3, o4), _ssm_discretise(a_ref[...], b_ref[...], c_ref[...])):
            o[...] = v

    return pl.pallas_call(kern, name="ssm_params_fwd", out_shape=[jax.ShapeDtypeStruct((rows, n), F32)] * 4)(
        log_neg_a_re, a_im, log_dt)


def _ssm_params_bwd(log_neg_a_re, a_im, log_dt, cots):
    rows, n = log_neg_a_re.shape

    def kern(a_ref, b_ref, c_ref, g1, g2, g3, g4, o1, o2, o3):
        _, vjp = jax.vjp(_ssm_discretise, a_ref[...], b_ref[...], c_ref[...])
        d1, d2, d3 = vjp((g1[...], g2[...], g3[...], g4[...]))
        o1[...] = d1
        o2[...] = d2
        o3[...] = d3

    return pl.pallas_call(
        kern, name="ssm_params_bwd",
        out_shape=[jax.ShapeDtypeStruct((rows, n), F32), jax.ShapeDtypeStruct((rows, n), F32),
                   jax.ShapeDtypeStruct((rows, 1), F32)])(log_neg_a_re, a_im, log_dt, *cots)


def _cpow(lr, li, n):
    out = None
    br, bi = lr, li
    while n:
        if n & 1:
            out = (br, bi) if out is None else _cmul(out[0], out[1], br, bi)
        n >>= 1
        if n:
            br, bi = _cmul(br, bi, br, bi)
    return out


def _slab(t):
    return pl.ds(pl.multiple_of(t * N_CHUNK, N_CHUNK), N_CHUNK)


def _shift_chunks(vr, vi, reverse):
    sub = lax.broadcasted_iota(jnp.int32, vr.shape, 0)
    if reverse:
        keep = sub != N_CHUNK - 1
        return jnp.where(keep, pltpu.roll(vr, N_CHUNK - 1, 0), 0.0), jnp.where(keep, pltpu.roll(vi, N_CHUNK - 1, 0), 0.0)
    keep = sub != 0
    return jnp.where(keep, pltpu.roll(vr, 1, 0), 0.0), jnp.where(keep, pltpu.roll(vi, 1, 0), 0.0)


def _chunk_scan(xr, xi, lr, li, Lc, reverse, prev=None):
    C = xr.shape[1]
    lrb, lib = jnp.broadcast_to(lr, (N_CHUNK, C)), jnp.broadcast_to(li, (N_CHUNK, C))
    zero = jnp.zeros((N_CHUNK, C), F32)

    def step_of(k):
        return (Lc - 1 - k) if reverse else k

    def advance(sr, si, t):
        return lrb * sr - lib * si + xr[_slab(t), :], lrb * si + lib * sr + xi[_slab(t), :]

    def pass1(k, c):
        return advance(c[0], c[1], step_of(k))

    er, ei = lax.fori_loop(0, Lc, pass1, (zero, zero), unroll=4)
    pr, pi = _cpow(lr, li, Lc)
    prb, pib = jnp.broadcast_to(pr, (N_CHUNK, C)), jnp.broadcast_to(pi, (N_CHUNK, C))
    cr, ci = zero, zero
    for _ in range(N_CHUNK - 1):
        cr, ci = _shift_chunks(er + prb * cr - pib * ci, ei + prb * ci + pib * cr, reverse)

    if prev is None:
        def pass2(k, c):
            t = step_of(k)
            nr, ni = advance(c[0], c[1], t)
            xr[_slab(t), :] = nr
            xi[_slab(t), :] = ni
            return nr, ni

        lax.fori_loop(0, Lc, pass2, (cr, ci), unroll=4)
        return None

    qr, qi = prev
    def pair(ar, ai, wr, wi, acc):
        return acc[0] + ar * wr + ai * wi, acc[1] + ai * wr - ar * wi

    def pass2(k, c):
        sr, si, accr, acci = c
        t = step_of(k)
        nr, ni = advance(sr, si, t)
        xr[_slab(t), :] = nr
        xi[_slab(t), :] = ni
        tp = t - 1 if reverse else t + 1
        accr, acci = pair(nr, ni, qr[_slab(tp), :], qi[_slab(tp), :], (accr, acci))
        return nr, ni, accr, acci

    sr, si, accr, acci = lax.fori_loop(0, Lc - 1, pass2, (cr, ci, zero, zero), unroll=4)
    t = step_of(Lc - 1)
    nr, ni = advance(sr, si, t)
    xr[_slab(t), :] = nr
    xi[_slab(t), :] = ni
    edge = step_of(0)
    wr, wi = _shift_chunks(qr[_slab(edge), :], qi[_slab(edge), :], not reverse)
    return pair(nr, ni, wr, wi, (accr, acci))


def _ssm_dims(T):
    assert T % ROW_BLOCK == 0
    return T // N_CHUNK, T // ROW_BLOCK


def _group_mask():
    shape = (SSM_BLOCK_GROUPS * SSM_GROUP, SSM_BLOCK_GROUPS * SSM_STATE)
    rows = lax.broadcasted_iota(jnp.int32, shape, 0) >> GROUP_SHIFT
    cols = lax.broadcasted_iota(jnp.int32, shape, 1) >> STATE_SHIFT
    return rows == cols


def _expand_groups(compact):
    return jnp.where(_group_mask(), jnp.tile(compact, (SSM_BLOCK_GROUPS, 1)), 0.0).astype(BF16)


def _collect_groups(block):
    kept = jnp.where(_group_mask(), block, 0.0)
    out = kept[0:SSM_GROUP]
    for g in range(1, SSM_BLOCK_GROUPS):
        out = out + kept[g * SSM_GROUP:(g + 1) * SSM_GROUP]
    return out


def _ssm_specs(T, direction):
    SC = SSM_BLOCK_GROUPS * SSM_STATE
    col = pl.BlockSpec((T, LANES), lambda m: (0, m))
    mat = pl.BlockSpec((None, None, SSM_GROUP, SC), lambda m: (direction, m, 0, 0))
    vec = pl.BlockSpec((None, None, 1, SC), lambda m: (direction, m, 0, 0))
    return col, mat, vec


def _ssm_project_in(u_ref, bre, bim, xr, xi, nblk):
    for blk in range(nblk):
        rows = pl.ds(blk * ROW_BLOCK, ROW_BLOCK)
        ub = u_ref[rows, :]
        xr[rows, :] = jnp.dot(ub, bre, preferred_element_type=F32)
        xi[rows, :] = jnp.dot(ub, bim, preferred_element_type=F32)


def _ssm_fwd(name, u_bf, mats, vecs, direction):
    T, W = u_bf.shape
    SC = SSM_BLOCK_GROUPS * SSM_STATE
    nb = W // LANES
    Lc, nblk = _ssm_dims(T)
    reverse = bool(direction)

    def kern(u_ref, bre_ref, bim_ref, cre_ref, cim_ref, lr_ref, li_ref, fr_ref, fi_ref, y_ref, xr, xi):
        _ssm_project_in(u_ref, _expand_groups(bre_ref[...]), _expand_groups(bim_ref[...]), xr, xi, nblk)
        _chunk_scan(xr, xi, lr_ref[...], li_ref[...], Lc, reverse)
        fr, fi = fr_ref[...], fi_ref[...]
        cre, cim = _expand_groups(cre_ref[...]), _expand_groups(cim_ref[...])
        for blk in range(nblk):
            rows = pl.ds(blk * ROW_BLOCK, ROW_BLOCK)
            sr, si = _cmul(fr, fi, xr[rows, :], xi[rows, :])
            y_ref[rows, :] = (lax.dot_general(sr.astype(BF16), cre, _NT, preferred_element_type=F32)
                              - lax.dot_general(si.astype(BF16), cim, _NT, preferred_element_type=F32))

    col, mat, vec = _ssm_specs(T, direction)
    return pl.pallas_call(
        kern, name=name, grid=(nb,),
        in_specs=[col, mat, mat, mat, mat, vec, vec, vec, vec],
        out_specs=col, out_shape=jax.ShapeDtypeStruct((T, W), F32),
        scratch_shapes=[pltpu.VMEM((T, SC), F32), pltpu.VMEM((T, SC), F32)],
        compiler_params=_params(("arbitrary",)),
    )(u_bf, *mats, *vecs)


def _ssm_bwd(name, u_bf, dy_bf, du_in, mats, vecs, direction):
    T, W = u_bf.shape
    SC = SSM_BLOCK_GROUPS * SSM_STATE
    nb = W // LANES
    Lc, nblk = _ssm_dims(T)
    reverse = bool(direction)

    def kern(u_ref, dy_ref, dui_ref, bre_ref, bim_ref, cre_ref, cim_ref, lr_ref, li_ref, fr_ref, fi_ref,
             du_ref, dbre_ref, dbim_ref, dcre_ref, dcim_ref, dlr_ref, dli_ref, dfr_ref, dfi_ref,
             xr, xi, gr, gi):
        lr, li, fr, fi = lr_ref[...], li_ref[...], fr_ref[...], fi_ref[...]
        bre, bim = _expand_groups(bre_ref[...]), _expand_groups(bim_ref[...])
        cre, cim = _expand_groups(cre_ref[...]), _expand_groups(cim_ref[...])
        _ssm_project_in(u_ref, bre, bim, xr, xi, nblk)
        _chunk_scan(xr, xi, lr, li, Lc, reverse)
        mre = jnp.zeros((LANES, SC), F32)
        mim = jnp.zeros((LANES, SC), F32)
        dfr = jnp.zeros((1, SC), F32)
        dfi = jnp.zeros((1, SC), F32)
        for blk in range(nblk):
            rows = pl.ds(blk * ROW_BLOCK, ROW_BLOCK)
            dy = dy_ref[rows, :]
            x_r, x_i = xr[rows, :], xi[rows, :]
            g_r = jnp.dot(dy, cre, preferred_element_type=F32)
            g_i = -jnp.dot(dy, cim, preferred_element_type=F32)
            mre += lax.dot_general(dy, x_r.astype(BF16), _TN, preferred_element_type=F32)
            mim += lax.dot_general(dy, x_i.astype(BF16), _TN, preferred_element_type=F32)
            dfr += _colsum(g_r * x_r + g_i * x_i)
            dfi += _colsum(g_i * x_r - g_r * x_i)
            gr[rows, :] = fr * g_r + fi * g_i
            gi[rows, :] = fr * g_i - fi * g_r
        dcre_ref[...] = _collect_groups(fr * mre - fi * mim)
        dcim_ref[...] = _collect_groups(-(fr * mim + fi * mre))
        dfr_ref[...] = dfr
        dfi_ref[...] = dfi
        accr, acci = _chunk_scan(gr, gi, lr, -li, Lc, not reverse, prev=(xr, xi))
        dlr_ref[...] = _colsum(accr)
        dli_ref[...] = _colsum(acci)
        dbre = jnp.zeros((LANES, SC), F32)
        dbim = jnp.zeros((LANES, SC), F32)
        for blk in range(nblk):
            rows = pl.ds(blk * ROW_BLOCK, ROW_BLOCK)
            ub = u_ref[rows, :]
            a_r, a_i = gr[rows, :].astype(BF16), gi[rows, :].astype(BF16)
            dbre += lax.dot_general(ub, a_r, _TN, preferred_element_type=F32)
            dbim += lax.dot_general(ub, a_i, _TN, preferred_element_type=F32)
            du_ref[rows, :] = (dui_ref[rows, :] + lax.dot_general(a_r, bre, _NT, preferred_element_type=F32)
                               + lax.dot_general(a_i, bim, _NT, preferred_element_type=F32))
        dbre_ref[...] = _collect_groups(dbre)
        dbim_ref[...] = _collect_groups(dbim)

    col, mat, vec = _ssm_specs(T, direction)
    mat_out = pl.BlockSpec((None, SSM_GROUP, SC), lambda m: (m, 0, 0))
    vec_out = pl.BlockSpec((None, 1, SC), lambda m: (m, 0, 0))
    mat_shape = jax.ShapeDtypeStruct((nb, SSM_GROUP, SC), F32)
    vec_shape = jax.ShapeDtypeStruct((nb, 1, SC), F32)
    return pl.pallas_call(
        kern, name=name, grid=(nb,),
        in_specs=[col, col, col, mat, mat, mat, mat, vec, vec, vec, vec],
        out_specs=[col, mat_out, mat_out, mat_out, mat_out, vec_out, vec_out, vec_out, vec_out],
        out_shape=[jax.ShapeDtypeStruct((T, W), F32), mat_shape, mat_shape, mat_shape, mat_shape,
                   vec_shape, vec_shape, vec_shape, vec_shape],
        scratch_shapes=[pltpu.VMEM((T, SC), F32)] * 4,
        compiler_params=_params(("arbitrary",)),
    )(u_bf, dy_bf, du_in, *mats, *vecs)


def _to_chunk_rows(a):
    T, C = a.shape
    return a.reshape(N_CHUNK, T // N_CHUNK, C).transpose(1, 0, 2).reshape(T, C)


def _from_chunk_rows(a):
    T, C = a.shape
    return a.reshape(T // N_CHUNK, N_CHUNK, C).transpose(1, 0, 2).reshape(T, C)


def _groups_side_by_side(p, channel_axis):
    d, G = p.shape[:2]
    nb = G // SSM_BLOCK_GROUPS
    v = p.reshape(d, nb, SSM_BLOCK_GROUPS, p.shape[2], p.shape[3])
    v = v.transpose(0, 1, 4, 2, 3) if channel_axis == 3 else v.transpose(0, 1, 3, 2, 4)
    return v.reshape(d, nb, SSM_GROUP, SSM_BLOCK_GROUPS * SSM_STATE)


def _groups_apart(m, channel_axis):
    d, nb = m.shape[:2]
    v = m.reshape(d, nb, SSM_GROUP, SSM_BLOCK_GROUPS, SSM_STATE)
    v = v.transpose(0, 1, 3, 4, 2) if channel_axis == 3 else v.transpose(0, 1, 3, 2, 4)
    return v.reshape((d, nb * SSM_BLOCK_GROUPS) + v.shape[3:])


def _conv_taps(ref, c, R, T):
    n = T // R
    r0 = pl.multiple_of(c * R, R)
    main = ref[pl.ds(r0, R), :]
    before = ref[pl.ds(pl.multiple_of(jnp.maximum(r0 - 8, 0), 8), 8), :]
    after = ref[pl.ds(pl.multiple_of(jnp.minimum(r0 + R, T - 8), 8), 8), :]
    ext = jnp.concatenate([jnp.where(c > 0, before, 0.0), main, jnp.where(c < n - 1, after, 0.0)], axis=0)
    m1 = pltpu.roll(ext, 1, 0)[8:8 + R]
    p1 = pltpu.roll(ext, R + 15, 0)[8:8 + R]
    return m1, main, p1, r0


def _conv_specs(T, F, cb):
    nj = F // cb
    lo = lambda rows: pl.BlockSpec((rows, cb), lambda j: (0, j))
    hi = lambda rows: pl.BlockSpec((rows, cb), lambda j: (0, j + nj))
    return nj, lo, hi


def _convact_fwd(up, conv_w, conv_b, F):
    T = up.shape[0]
    cb = _pick(F, (256, 128))
    R = _pick(T, (256, 128, 64))
    nj, lo, hi = _conv_specs(T, F, cb)

    def kern(uv, ug, wv, wg, bv, bg, o_ref):
        def body(c, carry):
            v1, v0, v2, r0 = _conv_taps(uv, c, R, T)
            g1, g0, g2, _ = _conv_taps(ug, c, R, T)
            val = v1 * wv[0:1, :] + v0 * wv[1:2, :] + v2 * wv[2:3, :] + bv[...]
            gate = g1 * wg[0:1, :] + g0 * wg[1:2, :] + g2 * wg[2:3, :] + bg[...]
            o_ref[pl.ds(r0, R), :] = (val * (gate * _sigmoid(gate))).astype(o_ref.dtype)
            return carry

        lax.fori_loop(0, T // R, body, 0)

    return pl.pallas_call(
        kern, name="convact_fwd", grid=(nj,),
        in_specs=[lo(T), hi(T), lo(3), hi(3), lo(1), hi(1)],
        out_specs=lo(T), out_shape=jax.ShapeDtypeStruct((T, F), BF16),
        compiler_params=_params(("arbitrary",)),
    )(up, up, conv_w, conv_w, conv_b, conv_b)


def _convact_bwd(up, dact, conv_w, conv_b, F):
    T = up.shape[0]
    cb = _pick(F, (256, 128))
    R = _pick(T, (256, 128, 64))
    nj, lo, hi = _conv_specs(T, F, cb)

    def kern(uv, ug, da, wv, wg, bv, bg, dupv, dupg, dwv, dwg, dbv, dbg, sv, sg):
        zero = jnp.zeros((1, cb), F32)

        def pass_a(c, acc):
            v1, v0, v2, r0 = _conv_taps(uv, c, R, T)
            g1, g0, g2, _ = _conv_taps(ug, c, R, T)
            val = v1 * wv[0:1, :] + v0 * wv[1:2, :] + v2 * wv[2:3, :] + bv[...]
            gate = g1 * wg[0:1, :] + g0 * wg[1:2, :] + g2 * wg[2:3, :] + bg[...]
            sig = _sigmoid(gate)
            d = da[pl.ds(r0, R), :]
            dval = d * (gate * sig)
            dgate = d * val * (sig * (1.0 + gate * (1.0 - sig)))
            sv[pl.ds(r0, R), :] = dval
            sg[pl.ds(r0, R), :] = dgate
            terms = (dval * v1, dval * v0, dval * v2, dval, dgate * g1, dgate * g0, dgate * g2, dgate)
            return tuple(a + _colsum(t) for a, t in zip(acc, terms))

        acc = lax.fori_loop(0, T // R, pass_a, (zero,) * 8)
        for k in range(3):
            dwv[k:k + 1, :] = acc[k]
            dwg[k:k + 1, :] = acc[4 + k]
        dbv[...] = acc[3]
        dbg[...] = acc[7]

        def pass_b(c, carry):
            v1, v0, v2, r0 = _conv_taps(sv, c, R, T)
            g1, g0, g2, _ = _conv_taps(sg, c, R, T)
            dupv[pl.ds(r0, R), :] = (v2 * wv[0:1, :] + v0 * wv[1:2, :] + v1 * wv[2:3, :]).astype(dupv.dtype)
            dupg[pl.ds(r0, R), :] = (g2 * wg[0:1, :] + g0 * wg[1:2, :] + g1 * wg[2:3, :]).astype(dupg.dtype)
            return carry

        lax.fori_loop(0, T // R, pass_b, 0)

    dupv, dupg, dwv, dwg, dbv, dbg = pl.pallas_call(
        kern, name="convact_bwd", grid=(nj,),
        in_specs=[lo(T), hi(T), lo(T), lo(3), hi(3), lo(1), hi(1)],
        out_specs=[lo(T), lo(T), lo(3), lo(3), lo(1), lo(1)],
        out_shape=[jax.ShapeDtypeStruct((T, F), BF16), jax.ShapeDtypeStruct((T, F), BF16),
                   jax.ShapeDtypeStruct((3, F), F32), jax.ShapeDtypeStruct((3, F), F32),
                   jax.ShapeDtypeStruct((1, F), F32), jax.ShapeDtypeStruct((1, F), F32)],
        scratch_shapes=[pltpu.VMEM((T, cb), F32), pltpu.VMEM((T, cb), F32)],
        compiler_params=_params(("arbitrary",)),
    )(up, up, dact, conv_w, conv_w, conv_b, conv_b)
    return (jnp.concatenate([dupv, dupg], axis=1), jnp.concatenate([dwv, dwg], axis=1),
            jnp.concatenate([dbv, dbg], axis=1))


def _peers(x, y, c, with_self=False):
    out = []
    for k in range(0 if with_self else 1, N_DEV):
        px = 1 - x if k & 4 else x
        py = 1 - y if k & 2 else y
        pc = 1 - c if k & 1 else c
        out.append((k, (px, py, pc), 4 * px + 2 * py + pc))
    return out


def _exchange_start(name, bufs, modes, after):
    n = len(bufs)
    ncopy = n * N_DEV
    lands = [lax.empty((N_DEV,) + tuple(b.shape[-2:]), b.dtype) for b in bufs]

    def body(*refs):
        ins, land_in = refs[:n], refs[n:2 * n]
        send_sems, recv_sems = refs[2 * n + 1], refs[2 * n + 2]
        token = refs[-1]
        x, y, c = lax.axis_index("x"), lax.axis_index("y"), lax.axis_index("c")
        me = 4 * x + 2 * y + c
        for b in range(n):
            for k, peer, slot in _peers(x, y, c, with_self=True):
                sem = b * N_DEV + k
                pltpu.make_async_remote_copy(
                    src_ref=ins[b] if modes[b] == 'gather' else ins[b].at[slot], dst_ref=land_in[b].at[me],
                    send_sem=send_sems.at[sem], recv_sem=recv_sems.at[sem],
                    device_id=peer, device_id_type=MESH_ID).start()
        token[...] = jnp.zeros_like(token)

    hbm = pl.BlockSpec(memory_space=pltpu.HBM)
    sem_spec = pl.BlockSpec(memory_space=pltpu.SEMAPHORE)
    operands = [pltpu.with_memory_space_constraint(a, pltpu.HBM) for a in list(bufs) + lands]
    res = pl.pallas_call(
        body, name=name,
        out_shape=(pltpu.SemaphoreType.DMA((ncopy,)), pltpu.SemaphoreType.DMA((ncopy,)),
                   *[pltpu.HBM(a.shape, a.dtype) for a in operands], jax.ShapeDtypeStruct((8, LANES), F32)),
        in_specs=[hbm] * (2 * n) + [pl.BlockSpec(memory_space=pl.ANY)],
        out_specs=(sem_spec, sem_spec, *([hbm] * (2 * n)), pl.BlockSpec(memory_space=pltpu.VMEM)),
        input_output_aliases={i: 2 + i for i in range(2 * n)},
        compiler_params=pltpu.CompilerParams(has_side_effects=pltpu.SideEffectType.DATAFLOW_SIDE_EFFECTING),
    )(*operands, after)
    return (res[0], res[1], list(res[2:2 + n]), list(res[2 + n:2 + 2 * n]), tuple(modes)), res[-1]


def _exchange_wait(name, handle, after):
    send_sems, recv_sems, srcs, lands, modes = handle
    n = len(srcs)

    def body(*refs):
        src_in, land_in = refs[:n], refs[n:2 * n]
        send_ref, recv_ref = refs[2 * n], refs[2 * n + 1]
        x, y, c = lax.axis_index("x"), lax.axis_index("y"), lax.axis_index("c")
        for b in range(n):
            for k, peer, slot in _peers(x, y, c, with_self=True):
                sem = b * N_DEV + k
                copy = pltpu.make_async_remote_copy(
                    src_ref=src_in[b] if modes[b] == 'gather' else src_in[b].at[slot], dst_ref=land_in[b].at[slot],
                    send_sem=send_ref.at[sem], recv_sem=recv_ref.at[sem],
                    device_id=peer, device_id_type=MESH_ID)
                copy.wait_send()
                copy.wait_recv()

    hbm = pl.BlockSpec(memory_space=pltpu.HBM)
    sem_spec = pl.BlockSpec(memory_space=pltpu.SEMAPHORE)
    res = pl.pallas_call(
        body, name=name,
        out_shape=tuple(pltpu.HBM(a.shape, a.dtype) for a in srcs + lands),
        in_specs=[hbm] * (2 * n) + [sem_spec, sem_spec, pl.BlockSpec(memory_space=pl.ANY)],
        out_specs=tuple([hbm] * (2 * n)),
        input_output_aliases={i: i for i in range(2 * n)},
        compiler_params=pltpu.CompilerParams(has_side_effects=pltpu.SideEffectType.DATAFLOW_SIDE_EFFECTING),
    )(*srcs, *lands, send_sems, recv_sems, after)
    return list(res[n:])


def _adamw(w, g, m, v):
    m2 = ADAM_B1 * m + (1.0 - ADAM_B1) * g
    v2 = ADAM_B2 * v + (1.0 - ADAM_B2) * (g * g)
    m_hat = m2 / (1.0 - ADAM_B1 ** ADAM_STEP)
    v_hat = v2 / (1.0 - ADAM_B2 ** ADAM_STEP)
    return -ADAM_LR * (m_hat / (jnp.sqrt(v_hat) + ADAM_EPS) + ADAM_WD * w), m2, v2


def _sum_adamw(name, landed, w, m, v):
    R, C = w.shape
    row_bytes = N_DEV * (-(-C // LANES) * LANES) * landed.dtype.itemsize
    tiles = [d for d in range(16, R, 16) if R % d == 0 and d * row_bytes <= ADAM_TILE_BYTES]
    tr = max(tiles) if tiles and R * row_bytes > ADAM_TILE_BYTES else R

    def kern(x_ref, w_ref, m_ref, v_ref, g_out, d_out, m_out, v_out):
        g = x_ref[0].astype(F32)
        for q in range(1, N_DEV):
            g = g + x_ref[q].astype(F32)
        g_out[...] = g
        d_out[...], m_out[...], v_out[...] = _adamw(w_ref[...], g, m_ref[...], v_ref[...])

    blk = pl.BlockSpec((tr, C), lambda i: (i, 0))
    return pl.pallas_call(
        kern, name=name, grid=(R // tr,),
        in_specs=[pl.BlockSpec((N_DEV, tr, C), lambda i: (0, i, 0)), blk, blk, blk],
        out_specs=[blk] * 4, out_shape=[jax.ShapeDtypeStruct((R, C), F32)] * 4,
        compiler_params=_params(("arbitrary",)),
    )(landed, w, m, v)


def _sum_slots(name, landed):
    def kern(x_ref, o_ref):
        g = x_ref[0].astype(F32)
        for q in range(1, N_DEV):
            g = g + x_ref[q].astype(F32)
        o_ref[...] = g

    return pl.pallas_call(kern, name=name, out_shape=jax.ShapeDtypeStruct(landed.shape[1:], F32))(landed)


def _adamw_whole(name, g, w, m, v):
    def kern(g_ref, w_ref, m_ref, v_ref, d_out, m_out, v_out):
        d_out[...], m_out[...], v_out[...] = _adamw(w_ref[...], g_ref[...], m_ref[...], v_ref[...])

    return pl.pallas_call(kern, name=name, out_shape=[jax.ShapeDtypeStruct(w.shape, F32)] * 3)(g, w, m, v)


def _view2d(a):
    if a.ndim == 1:
        return a.reshape(1, -1)
    return a.reshape(-1, a.shape[-1])


def kernel(x, norm_mix_g, w_in, pool_w, pool_scale, ssm_log_neg_a_re, ssm_a_im, ssm_log_dt, ssm_b_re, ssm_b_im, ssm_c_re, ssm_c_im, ssm_d, glu_w, glu_b, out_norm_pool_g, out_norm_ssm_g, w_out, norm_ffn_g, w_up, conv_w, conv_b, w_down, final_norm_g, loss_target, m_norm_mix_g, m_w_in, m_pool_w, m_pool_scale, m_ssm_log_neg_a_re, m_ssm_a_im, m_ssm_log_dt, m_ssm_b_re, m_ssm_b_im, m_ssm_c_re, m_ssm_c_im, m_ssm_d, m_glu_w, m_glu_b, m_out_norm_pool_g, m_out_norm_ssm_g, m_w_out, m_norm_ffn_g, m_w_up, m_conv_w, m_conv_b, m_w_down, m_final_norm_g, v_norm_mix_g, v_w_in, v_pool_w, v_pool_scale, v_ssm_log_neg_a_re, v_ssm_a_im, v_ssm_log_dt, v_ssm_b_re, v_ssm_b_im, v_ssm_c_re, v_ssm_c_im, v_ssm_d, v_glu_w, v_glu_b, v_out_norm_pool_g, v_out_norm_ssm_g, v_w_out, v_norm_ffn_g, v_w_up, v_conv_w, v_conv_b, v_w_down, v_final_norm_g):
    given = dict(locals())
    weights = {n: given[n] for n in WEIGHTS}
    mom1 = {n: given["m_" + n] for n in WEIGHTS}
    mom2 = {n: given["v_" + n] for n in WEIGHTS}

    xs = x[0]
    tgt = loss_target[0]
    T, D = xs.shape
    DP = len(POOL_WINDOWS) * POOL_GROUP
    DS = D - DP
    G = DS // SSM_GROUP
    N, H = SSM_STATE, SSM_GROUP
    F2 = w_up.shape[1] * N_DEV
    F = F2 // 2
    fs = w_up.shape[1]
    row = lambda a: a.reshape(1, -1)

    early_handle, token = _exchange_start("gather_early_start", [w_in.astype(BF16), glu_w.astype(BF16)],
                                          ['gather'] * 2, after=w_in)
    late_src = [w_out.astype(BF16), w_down.astype(BF16), w_up.astype(BF16), conv_w]
    late_handle, token = _exchange_start("gather_late_start", late_src, ['gather'] * 4, after=token)
    CB = row(conv_b)

    g1, g2, g3 = row(norm_mix_g), row(norm_ffn_g), row(final_norm_g)
    gp, gs = row(out_norm_pool_g), row(out_norm_ssm_g)
    (xn,) = _rowmap("norm_mix", _rms_fwd, [xs], [g1 + token[0:1, 0:1]], [(D, BF16)])
    early = _exchange_wait("gather_early_wait", early_handle, xn)
    W_in = early[0].reshape(D, D)
    W_glu = early[1].reshape(DS, DS)
    u = _mm("proj_in", xn, W_in, 'nn', F32)
    tmp = _pick(T, (256, 128))
    pool_w_bf = pool_w.astype(BF16)
    ypn = _pool_fwd(u, pool_w_bf, row(pool_scale), gp, tmp)

    u_ssm = _to_chunk_rows(u[:, DP:])
    u_ssm_bf = u_ssm.astype(BF16)
    a_rows = (ssm_log_neg_a_re.reshape(2 * G, N), ssm_a_im.reshape(2 * G, N), ssm_log_dt.reshape(2 * G, 1))
    disc = _ssm_params_fwd(*a_rows)
    nb = G // SSM_BLOCK_GROUPS
    vecs = [d.reshape(2, nb, 1, SSM_BLOCK_GROUPS * N) for d in disc]
    mats = [_groups_side_by_side(ssm_b_re, 3), _groups_side_by_side(ssm_b_im, 3),
            _groups_side_by_side(ssm_c_re, 2), _groups_side_by_side(ssm_c_im, 2)]
    y_dir = [_ssm_fwd("ssm_fwd_%d" % d, u_ssm_bf, mats, vecs, d) for d in range(2)]

    def mix_post(yf, yb, us, d, gw, gb, g):
        y = yf + yb + d * us
        z = _gelu(y)
        gate = _sigmoid(jnp.dot(z.astype(BF16), gw, preferred_element_type=F32) + gb)
        return _rms_fwd(z * gate, g), y

    ysn, y_ssm = _rowmap("ssm_post", mix_post, [y_dir[0], y_dir[1], u_ssm], [row(ssm_d), W_glu, row(glu_b), gs],
                         [(DS, BF16), (DS, F32)])
    ycat = jnp.concatenate([ypn, _from_chunk_rows(ysn)], axis=1)
    late = _exchange_wait("gather_late_wait", late_handle, ysn)
    W_out = late[0].reshape(D, D)
    W_down = late[1].reshape(F, D)
    W_up = late[2].transpose(1, 0, 2).reshape(D, F2)
    CW = late[3].transpose(1, 0, 2).reshape(3, F2)
    h1 = _mm("proj_out", ycat, W_out, 'nn', F32, add=xs)
    (hn,) = _rowmap("norm_ffn", _rms_fwd, [h1], [g2], [(D, BF16)])
    up = _mm("ffn_up", hn, W_up, 'nn', F32)
    act = _convact_fwd(up, CW, CB, F)
    h2 = _mm("ffn_down", act, W_down, 'nn', F32, add=h1)

    def head(h, t, g):
        r = lax.rsqrt(_rowmean(h * h) + EPS)
        hh = h * r
        e = hh * g - t
        loss = 0.5 * jnp.sum(_rowmean(e * e), keepdims=True)
        dy = e * (1.0 / D)
        dxh = dy * g
        dh = r * (dxh - hh * _rowmean(dxh * hh))
        return dh, dh, jnp.broadcast_to(loss, (1, LANES)), _colsum(dy * hh)

    dh2, dh2_bf, loss_acc, dg3 = _rowmap("loss_head", head, [h2, tgt], [g3], [(D, F32), (D, BF16)], [(1, LANES), (1, D)])

    dact = _mm("ffn_down_dx", dh2_bf, W_down, 'nt', F32)
    dW_down = _mm("ffn_down_dw", act, dh2_bf, 'tn', BF16)
    dup, dCW, dCB = _convact_bwd(up, dact, CW, CB, F)
    dhn = _mm("ffn_up_dx", dup, W_up, 'nt', F32)
    dW_up = _mm("ffn_up_dw", hn, dup, 'tn', BF16)

    def shards(full_grad):
        return full_grad.reshape((N_DEV, full_grad.shape[0] // N_DEV) + full_grad.shape[1:])

    ffn_sent = [shards(dW_down), dW_up.reshape(D, N_DEV, fs).transpose(1, 0, 2), dCW.reshape(3, N_DEV, fs).transpose(1, 0, 2)]
    ffn_handle, token = _exchange_start("scatter_ffn_start", ffn_sent, ['scatter'] * 3, after=dhn)
    g2_late = g2 + token[0:1, 0:1]

    def norm_bwd_add(hx, dy, res, g):
        dx, dg = _rms_bwd(hx, g, dy)
        return dx + res, dg

    def norm_bwd_add2(hx, dy, res, g):
        dx, dg = _rms_bwd(hx, g, dy)
        return dx + res, dx + res, dg

    dh1, dh1_bf, dg2 = _rowmap("norm_ffn_bwd", norm_bwd_add2, [h1, dhn, dh2], [g2_late], [(D, F32), (D, BF16)], [(1, D)])
    dycat = _mm("proj_out_dx", dh1_bf, W_out, 'nt', F32)
    dW_out = _mm("proj_out_dw", ycat, dh1_bf, 'tn', BF16)

    out_sent = [shards(dW_out)]
    out_handle, token = _exchange_start("scatter_out_start", out_sent, ['scatter'], after=dycat)
    q, dpooled, dpool_w, dpool_scale, dgp = _pool_bwd_rows(u, dycat, pool_w_bf, row(pool_scale),
                                                           gp + token[0:1, 0:1], tmp)
    du_pool = _pool_bwd_band(q, dpooled, tmp)

    def mix_post_bwd(y, us, dyn, d, gw, gb, g):
        z = _gelu(y)
        gz = z.astype(BF16)
        sig = _sigmoid(jnp.dot(gz, gw, preferred_element_type=F32) + gb)
        dys, dg = _rms_bwd(z * sig, g, dyn)
        dgl = dys * z * sig * (1.0 - sig)
        dgl_bf = dgl.astype(BF16)
        dz = dys * sig + lax.dot_general(dgl_bf, gw, (((1,), (1,)), ((), ())), preferred_element_type=F32)
        dgw = lax.dot_general(gz, dgl_bf, (((0,), (0,)), ((), ())), preferred_element_type=F32)
        dyv = dz * _gelu_grad(y)
        return dyv, dyv * d, dgw, _colsum(dgl), _colsum(dyv * us), dg

    dyv, du_dir, dglu_w, dglu_b, dssm_d, dgs = _rowmap(
        "ssm_post_bwd", mix_post_bwd, [y_ssm, u_ssm, _to_chunk_rows(dycat[:, DP:])], [row(ssm_d), W_glu, row(glu_b), gs],
        [(DS, BF16), (DS, F32)], [(DS, DS), (1, DS), (1, DS), (1, DS)])

    du_ssm = du_dir
    dB, dC, dvec = [], [], []
    for d in range(2):
        res = _ssm_bwd("ssm_bwd_%d" % d, u_ssm_bf, dyv, du_ssm, mats, vecs, d)
        du_ssm = res[0]
        dB.append(res[1:3])
        dC.append(res[3:5])
        dvec.append(res[5:9])
    grad_b = [_groups_apart(jnp.stack([dB[0][k], dB[1][k]]), 3) for k in range(2)]
    grad_c = [_groups_apart(jnp.stack([dC[0][k], dC[1][k]]), 2) for k in range(2)]
    cots = [jnp.stack([dvec[0][k], dvec[1][k]]).reshape(2 * G, N) for k in range(4)]
    d_a_re, d_a_im, d_log_dt = _ssm_params_bwd(*a_rows, cots)

    rep_grads = {
        'pool_w': dpool_w, 'pool_scale': dpool_scale, 'ssm_log_neg_a_re': d_a_re, 'ssm_a_im': d_a_im,
        'ssm_log_dt': d_log_dt, 'ssm_b_re': grad_b[0], 'ssm_b_im': grad_b[1], 'ssm_c_re': grad_c[0], 'ssm_c_im': grad_c[1],
        'ssm_d': dssm_d, 'glu_b': dglu_b, 'out_norm_pool_g': dgp, 'out_norm_ssm_g': dgs, 'norm_ffn_g': dg2,
        'conv_b': dCB, 'final_norm_g': dg3}
    wide = ('pool_w', 'ssm_b_re', 'ssm_b_im', 'ssm_c_re', 'ssm_c_im')

    def travel(n):
        g = rep_grads[n]
        if n in wide:
            return g.reshape(-1, PACK_W if g.size % PACK_W == 0 else LANES).astype(BF16)
        return _view2d(g.reshape(weights[n].shape))

    small = tuple(n for n in REPLICATED if n != 'norm_mix_g')
    small_sent = [shards(dglu_w.astype(BF16))] + [travel(n) for n in small] + [loss_acc]
    small_handle, token = _exchange_start("reduce_small_start", small_sent, ['scatter'] + ['gather'] * (len(small) + 1),
                                          after=d_a_re)

    du = (jnp.concatenate([du_pool, _from_chunk_rows(du_ssm)], axis=1) + token[0:1, 0:1]).astype(BF16)
    dW_in = _mm("proj_in_dw", xn, du, 'tn', BF16)
    in_handle, token = _exchange_start("scatter_in_start", [shards(dW_in)], ['scatter'], after=dW_in)
    dxn = _mm("proj_in_dx", du, W_in + token[0, 0].astype(BF16), 'nt', F32)
    dx, dg1 = _rowmap("norm_mix_bwd", norm_bwd_add, [xs, dxn, dh1], [g1], [(D, F32)], [(1, D)])
    rep_grads['norm_mix_g'] = dg1
    last_handle, token = _exchange_start("reduce_last_start", [travel('norm_mix_g')], ['gather'], after=dx)

    grads, delta, new_m, new_v = {}, {}, {}, {}

    def update(n, landed_n):
        shape = weights[n].shape
        if len(shape) > 2 and shape[-1] < LANES // 2:
            grads[n] = _sum_slots("sum_" + n, landed_n).reshape(shape)
            delta[n], new_m[n], new_v[n] = _adamw_whole("adamw_" + n, grads[n], weights[n], mom1[n], mom2[n])
            return
        landed_n = landed_n.reshape((N_DEV,) + _view2d(weights[n]).shape)
        res = _sum_adamw("adamw_" + n, landed_n, _view2d(weights[n]), _view2d(mom1[n]), _view2d(mom2[n]))
        grads[n], delta[n], new_m[n], new_v[n] = [r.reshape(shape) for r in res]

    ffn_landed = _exchange_wait("scatter_ffn_wait", ffn_handle, token)
    for n, l in zip(('w_down', 'w_up', 'conv_w'), ffn_landed):
        update(n, l)
    (out_landed,) = _exchange_wait("scatter_out_wait", out_handle, token)
    update('w_out', out_landed)
    small_landed = _exchange_wait("reduce_small_wait", small_handle, token)
    for n, l in zip(('glu_w',) + small, small_landed):
        update(n, l)

    def loss_sum(parts_ref, o_ref):
        s = parts_ref[0]
        for q in range(1, N_DEV):
            s = s + parts_ref[q]
        o_ref[...] = s

    loss = pl.pallas_call(loss_sum, name="loss_sum", out_shape=jax.ShapeDtypeStruct((1, LANES), F32))(small_landed[-1])[0, 0]

    update('w_in', _exchange_wait("scatter_in_wait", in_handle, grads[small[-1]])[0])
    update('norm_mix_g', _exchange_wait("reduce_last_wait", last_handle, grads['w_in'])[0])

    return (loss, dx[None], *[grads[n] for n in WEIGHTS], *[delta[n] for n in WEIGHTS],
            *[new_m[n] for n in WEIGHTS], *[new_v[n] for n in WEIGHTS])
```

```python
import functools

import jax
import jax.numpy as jnp
import numpy as np
from jax import lax
from jax.experimental import pallas as pl
from jax.experimental.pallas import tpu as pltpu

F32 = jnp.float32
BF16 = jnp.bfloat16
MESH_ID = pl.DeviceIdType.MESH

N_DEV = 8
EPS = 1e-6
POOL_WINDOWS = (2, 4, 8, 16)
POOL_GROUP = 128
POOL_PAD = 64
SSM_GROUP = 16
SSM_STATE = 64
SSM_BLOCK_GROUPS = 8
N_CHUNK = 8
ROW_BLOCK = 512
GROUP_SHIFT = 4
STATE_SHIFT = 6
_NT = (((1,), (1,)), ((), ()))
_TN = (((0,), (0,)), ((), ()))
LANES = 128
PACK_W = 1024
VMEM_LIMIT = 56 * 1024 * 1024
MM_VMEM_BUDGET = 40 * 1024 * 1024
MM_TILE_CAP = 1408
ADAM_TILE_BYTES = 4 * 1024 * 1024

ADAM_LR = 0.001
ADAM_B1 = 0.9
ADAM_B2 = 0.999
ADAM_EPS = 1e-08
ADAM_WD = 0.01
ADAM_STEP = 10

WEIGHTS = ['norm_mix_g', 'w_in', 'pool_w', 'pool_scale', 'ssm_log_neg_a_re', 'ssm_a_im', 'ssm_log_dt',
           'ssm_b_re', 'ssm_b_im', 'ssm_c_re', 'ssm_c_im', 'ssm_d', 'glu_w', 'glu_b', 'out_norm_pool_g',
           'out_norm_ssm_g', 'w_out', 'norm_ffn_g', 'w_up', 'conv_w', 'conv_b', 'w_down', 'final_norm_g']
SHARDED = ('w_in', 'w_out', 'w_down', 'glu_w', 'w_up', 'conv_w')
REPLICATED = tuple(n for n in WEIGHTS if n not in SHARDED)


def _pick(n, prefs):
    for p in prefs:
        if n % p == 0:
            return p
    return n


def _params(sem, vmem=None):
    return pltpu.CompilerParams(dimension_semantics=sem, vmem_limit_bytes=vmem or VMEM_LIMIT)


def _tiles(n, cap):
    return [d for d in range(LANES, min(n, cap) + 1, LANES) if n % d == 0] or [n]


def _mm(name, a, b, mode, out_dtype, add=None, b_rows=(1, 0)):
    if mode == 'nn':
        (M, K), N = a.shape, b.shape[1]
        assert b.shape[0] == K * b_rows[0]
    elif mode == 'nt':
        (M, K), (N, _) = a.shape, b.shape
    else:
        (K, M), (_, N) = a.shape, b.shape
    dims = {'nn': (((1,), (0,)), ((), ())), 'nt': (((1,), (1,)), ((), ())), 'tn': (((0,), (0,)), ((), ()))}[mode]
    sa, sb, so = a.dtype.itemsize, b.dtype.itemsize, jnp.dtype(out_dtype).itemsize
    best = None
    for tm in _tiles(M, MM_TILE_CAP):
        for tn in _tiles(N, MM_TILE_CAP):
            need = 2 * (tm * K * sa + tn * K * sb + tm * tn * (so + (4 if add is not None else 0)))
            if need <= MM_VMEM_BUDGET:
                key = (tm * tn / (tm + tn), tm * tn)
                if best is None or key > best[0]:
                    best = (key, tm, tn)
    _, tm, tn = best
    rows_inner = a.size * sa * (N // tn) + b.size * sb < a.size * sa + b.size * sb * (M // tm)

    def kern(*refs):
        a_ref, b_ref = refs[:2]
        o_ref = refs[-1]
        r = lax.dot_general(a_ref[...].astype(BF16), b_ref[...].astype(BF16), dims, preferred_element_type=F32)
        if add is not None:
            r = r + refs[2][...]
        o_ref[...] = r.astype(o_ref.dtype)

    if rows_inner:
        grid = (N // tn, M // tm)
        ij = lambda g0, g1: (g1, g0)
    else:
        grid = (M // tm, N // tn)
        ij = lambda g0, g1: (g0, g1)
    a_spec = (pl.BlockSpec((K, tm), lambda g0, g1: (0, ij(g0, g1)[0])) if mode == 'tn'
              else pl.BlockSpec((tm, K), lambda g0, g1: (ij(g0, g1)[0], 0)))
    b_spec = (pl.BlockSpec((tn, K), lambda g0, g1: (ij(g0, g1)[1], 0)) if mode == 'nt'
              else pl.BlockSpec((K, tn), lambda g0, g1: (b_rows[1], ij(g0, g1)[1])))
    in_specs = [a_spec, b_spec]
    args = [a, b]
    if add is not None:
        in_specs.append(pl.BlockSpec((tm, tn), lambda g0, g1: ij(g0, g1)))
        args.append(add)
    return pl.pallas_call(
        kern, name=name, grid=grid, in_specs=in_specs,
        out_specs=pl.BlockSpec((tm, tn), lambda g0, g1: ij(g0, g1)),
        out_shape=jax.ShapeDtypeStruct((M, N), out_dtype),
        compiler_params=_params(("parallel", "parallel")),
    )(*args)


def _rowmap(name, fn, rows, fulls, out_rows, out_accs=(), tm=256):
    rows = [r if isinstance(r, tuple) else (r, r.shape[1], 0) for r in rows]
    T = rows[0][0].shape[0]
    tm = min(tm, T)
    assert T % tm == 0
    n_in, n_row = len(rows) + len(fulls), len(out_rows)

    def kern(*refs):
        i = pl.program_id(0)
        res = fn(*[r[...] for r in refs[:n_in]])
        res = res if isinstance(res, (tuple, list)) else (res,)
        outs = refs[n_in:]
        for o, v in zip(outs[:n_row], res[:n_row]):
            o[...] = v.astype(o.dtype)
        for o, v in zip(outs[n_row:], res[n_row:]):
            @pl.when(i == 0)
            def _(o=o, v=v):
                o[...] = v

            @pl.when(i != 0)
            def _(o=o, v=v):
                o[...] += v

    def full_spec(shape):
        nd = len(shape)
        return pl.BlockSpec(tuple(shape), lambda i: (0,) * nd)

    in_specs = [pl.BlockSpec((tm, w), lambda i, k=k: (i, k)) for _, w, k in rows] + [full_spec(f.shape) for f in fulls]
    out_specs = [pl.BlockSpec((tm, c), lambda i: (i, 0)) for c, _ in out_rows] + [full_spec(s) for s in out_accs]
    out_shape = [jax.ShapeDtypeStruct((T, c), dt) for c, dt in out_rows] + [jax.ShapeDtypeStruct(tuple(s), F32) for s in out_accs]
    return pl.pallas_call(
        kern, name=name, grid=(T // tm,), in_specs=in_specs, out_specs=out_specs, out_shape=out_shape,
        compiler_params=_params(("arbitrary",)),
    )(*[r for r, _, _ in rows], *fulls)


def _colsum(v):
    return jnp.sum(v, axis=0, keepdims=True)


def _rowmean(v):
    return jnp.mean(v, axis=-1, keepdims=True)


def _rms_fwd(x, g):
    r = lax.rsqrt(_rowmean(x * x) + EPS)
    return x * r * g


def _rms_bwd(x, g, dy):
    r = lax.rsqrt(_rowmean(x * x) + EPS)
    xh = x * r
    dxh = dy * g
    return r * (dxh - xh * _rowmean(dxh * xh)), _colsum(dy * xh)


def _sigmoid(v):
    return 0.5 * jnp.tanh(0.5 * v) + 0.5


def _gelu(y):
    c = np.sqrt(2.0 / np.pi).astype(np.float32)
    return 0.5 * y * (1.0 + jnp.tanh(c * (y + 0.044715 * (y * y * y))))


def _gelu_grad(y):
    c = np.sqrt(2.0 / np.pi).astype(np.float32)
    th = jnp.tanh(c * (y + 0.044715 * (y * y * y)))
    return 0.5 * (1.0 + th) + 0.5 * y * (1.0 - th * th) * c * (1.0 + 3.0 * 0.044715 * (y * y))


def _split3(v):
    hi = v.astype(BF16)
    r1 = v - hi.astype(F32)
    mid = r1.astype(BF16)
    lo = (r1 - mid.astype(F32)).astype(BF16)
    return hi, mid, lo


def _band(tm, lo_off, hi_off):
    shape = (tm, tm + 2 * POOL_PAD)
    d = lax.broadcasted_iota(jnp.int32, shape, 1) - lax.broadcasted_iota(jnp.int32, shape, 0) - POOL_PAD
    return ((d >= lo_off) & (d < hi_off)).astype(BF16)


def _band_sum(band, ext):
    hi, mid, lo = _split3(ext)
    dot = functools.partial(jnp.dot, preferred_element_type=F32)
    return dot(band, hi) + dot(band, mid) + dot(band, lo)


def _window_count(r0, tm, half, T):
    t = r0 + lax.broadcasted_iota(jnp.int32, (tm, 1), 0)
    return (jnp.minimum(t + half, T) - jnp.maximum(t - half, 0)).astype(F32)


def _halo_specs(tm, C, T):
    per = tm // POOL_PAD
    last = T // POOL_PAD - 1
    return [pl.BlockSpec((POOL_PAD, C), lambda i: (jnp.maximum(i * per - 1, 0), 0)),
            pl.BlockSpec((tm, C), lambda i: (i, 0)),
            pl.BlockSpec((POOL_PAD, C), lambda i: (jnp.minimum((i + 1) * per, last), 0))]


def _with_halo(before_ref, main_ref, after_ref, i, n):
    before = jnp.where(i > 0, before_ref[...], 0.0)
    after = jnp.where(i < n - 1, after_ref[...], 0.0)
    return jnp.concatenate([before, main_ref[...], after], axis=0)


def _pool_block(ext, ctr, w_ref, r0, tm, T):
    pooled, conc = [], []
    for gi, w in enumerate(POOL_WINDOWS):
        half = w // 2
        cols = slice(gi * POOL_GROUP, (gi + 1) * POOL_GROUP)
        ws = _band_sum(_band(tm, -half, half), ext[:, cols])
        p = ws / _window_count(r0, tm, half, T) - ctr[:, cols]
        pooled.append(p)
        conc.append(jnp.dot(p.astype(BF16), w_ref[gi], preferred_element_type=F32))
    return pooled, jnp.concatenate(conc, axis=1)


def _pool_fwd(u, pool_w_bf, pool_scale, g_pool, tm):
    T = u.shape[0]
    nw = len(POOL_WINDOWS)
    C = nw * POOL_GROUP
    n = T // tm

    def kern(ub_ref, u_ref, ua_ref, w_ref, sc_ref, g_ref, o_ref):
        i = pl.program_id(0)
        r0 = pl.multiple_of(i * tm, tm)
        _, conc = _pool_block(_with_halo(ub_ref, u_ref, ua_ref, i, n), u_ref[...], w_ref, r0, tm, T)
        o_ref[...] = _rms_fwd(conc * sc_ref[...], g_ref[...]).astype(o_ref.dtype)

    return pl.pallas_call(
        kern, name="pool_fwd", grid=(n,),
        in_specs=_halo_specs(tm, C, T) + [pl.BlockSpec(pool_w_bf.shape, lambda i: (0, 0, 0)),
                                          pl.BlockSpec((1, C), lambda i: (0, 0)), pl.BlockSpec((1, C), lambda i: (0, 0))],
        out_specs=pl.BlockSpec((tm, C), lambda i: (i, 0)),
        out_shape=jax.ShapeDtypeStruct((T, C), BF16),
        compiler_params=_params(("arbitrary",)),
    )(u, u, u, pool_w_bf, pool_scale, g_pool)


def _pool_bwd_rows(u, d_y, pool_w_bf, pool_scale, g_pool, tm):
    T = u.shape[0]
    nw = len(POOL_WINDOWS)
    C = nw * POOL_GROUP
    n = T // tm

    def kern(ub_ref, u_ref, ua_ref, dy_ref, w_ref, sc_ref, g_ref, q_ref, dp_ref, dw_ref, dsc_ref, dg_ref):
        i = pl.program_id(0)
        r0 = pl.multiple_of(i * tm, tm)
        pooled, conc = _pool_block(_with_halo(ub_ref, u_ref, ua_ref, i, n), u_ref[...], w_ref, r0, tm, T)
        sc = sc_ref[...]
        dyp, dg = _rms_bwd(conc * sc, g_ref[...], dy_ref[...])
        dsc = _colsum(dyp * conc)
        dconc = (dyp * sc).astype(BF16)
        dws = []
        for gi, w in enumerate(POOL_WINDOWS):
            cols = slice(gi * POOL_GROUP, (gi + 1) * POOL_GROUP)
            dc = dconc[:, cols]
            dp = lax.dot_general(dc, w_ref[gi], (((1,), (1,)), ((), ())), preferred_element_type=F32)
            dp_ref[:, cols] = dp
            q_ref[:, cols] = dp / _window_count(r0, tm, w // 2, T)
            dws.append(lax.dot_general(pooled[gi].astype(BF16), dc, (((0,), (0,)), ((), ())), preferred_element_type=F32))

        @pl.when(i == 0)
        def _():
            for gi in range(nw):
                dw_ref[gi] = dws[gi]
            dsc_ref[...] = dsc
            dg_ref[...] = dg

        @pl.when(i != 0)
        def _():
            for gi in range(nw):
                dw_ref[gi] += dws[gi]
            dsc_ref[...] += dsc
            dg_ref[...] += dg

    full2 = pl.BlockSpec((1, C), lambda i: (0, 0))
    row_blk = pl.BlockSpec((tm, C), lambda i: (i, 0))
    return pl.pallas_call(
        kern, name="pool_bwd_rows", grid=(n,),
        in_specs=_halo_specs(tm, C, T) + [row_blk, pl.BlockSpec(pool_w_bf.shape, lambda i: (0, 0, 0)), full2, full2],
        out_specs=[row_blk, row_blk, pl.BlockSpec((nw, POOL_GROUP, POOL_GROUP), lambda i: (0, 0, 0)), full2, full2],
        out_shape=[jax.ShapeDtypeStruct((T, C), F32), jax.ShapeDtypeStruct((T, C), F32),
                   jax.ShapeDtypeStruct((nw, POOL_GROUP, POOL_GROUP), F32),
                   jax.ShapeDtypeStruct((1, C), F32), jax.ShapeDtypeStruct((1, C), F32)],
        compiler_params=_params(("arbitrary",)),
    )(u, u, u, d_y, pool_w_bf, pool_scale, g_pool)


def _pool_bwd_band(q, dpooled, tm):
    T, C = dpooled.shape
    n = T // tm

    def kern(qb_ref, q_ref, qa_ref, dp_ref, o_ref):
        ext = _with_halo(qb_ref, q_ref, qa_ref, pl.program_id(0), n)
        for gi, w in enumerate(POOL_WINDOWS):
            half = w // 2
            cols = slice(gi * POOL_GROUP, (gi + 1) * POOL_GROUP)
            o_ref[:, cols] = _band_sum(_band(tm, -half + 1, half + 1), ext[:, cols]) - dp_ref[:, cols]

    return pl.pallas_call(
        kern, name="pool_bwd_band", grid=(n,),
        in_specs=_halo_specs(tm, C, T) + [pl.BlockSpec((tm, C), lambda i: (i, 0))],
        out_specs=pl.BlockSpec((tm, C), lambda i: (i, 0)),
        out_shape=jax.ShapeDtypeStruct((T, C), F32),
        compiler_params=_params(("arbitrary",)),
    )(q, q, q, dpooled)


def _cmul(ar, ai, br, bi):
    return ar * br - ai * bi, ar * bi + ai * br


def _ssm_discretise(log_neg_a_re, a_im, log_dt):
    dt = jnp.exp(log_dt)
    a_re = -jnp.exp(log_neg_a_re)
    mag = jnp.exp(a_re * dt)
    ang = a_im * dt
    lam_re, lam_im = mag * jnp.cos(ang), mag * jnp.sin(ang)
    den = a_re * a_re + a_im * a_im
    f_re = ((lam_re - 1.0) * a_re + lam_im * a_im) / den
    f_im = (lam_im * a_re - (lam_re - 1.0) * a_im) / den
    return lam_re, lam_im, f_re, f_im


def _ssm_params_fwd(log_neg_a_re, a_im, log_dt):
    rows, n = log_neg_a_re.shape

    def kern(a_ref, b_ref, c_ref, o1, o2, o3, o4):
        for o, v in zip((o1, o2, o3, o4), _ssm_discretise(a_ref[...], b_ref[...], c_ref[...])):
            o[...] = v

    return pl.pallas_call(kern, name="ssm_params_fwd", out_shape=[jax.ShapeDtypeStruct((rows, n), F32)] * 4)(
        log_neg_a_re, a_im, log_dt)


def _ssm_params_bwd(log_neg_a_re, a_im, log_dt, cots):
    rows, n = log_neg_a_re.shape

    def kern(a_ref, b_ref, c_ref, g1, g2, g3, g4, o1, o2, o3):
        _, vjp = jax.vjp(_ssm_discretise, a_ref[...], b_ref[...], c_ref[...])
        d1, d2, d3 = vjp((g1[...], g2[...], g3[...], g4[...]))
        o1[...] = d1
        o2[...] = d2
        o3[...] = d3

    return pl.pallas_call(
        kern, name="ssm_params_bwd",
        out_shape=[jax.ShapeDtypeStruct((rows, n), F32), jax.ShapeDtypeStruct((rows, n), F32),
                   jax.ShapeDtypeStruct((rows, 1), F32)])(log_neg_a_re, a_im, log_dt, *cots)


def _cpow(lr, li, n):
    out = None
    br, bi = lr, li
    while n:
        if n & 1:
            out = (br, bi) if out is None else _cmul(out[0], out[1], br, bi)
        n >>= 1
        if n:
            br, bi = _cmul(br, bi, br, bi)
    return out


def _slab(t):
    return pl.ds(pl.multiple_of(t * N_CHUNK, N_CHUNK), N_CHUNK)


def _shift_chunks(vr, vi, reverse):
    sub = lax.broadcasted_iota(jnp.int32, vr.shape, 0)
    if reverse:
        keep = sub != N_CHUNK - 1
        return jnp.where(keep, pltpu.roll(vr, N_CHUNK - 1, 0), 0.0), jnp.where(keep, pltpu.roll(vi, N_CHUNK - 1, 0), 0.0)
    keep = sub != 0
    return jnp.where(keep, pltpu.roll(vr, 1, 0), 0.0), jnp.where(keep, pltpu.roll(vi, 1, 0), 0.0)


def _chunk_scan(xr, xi, lr, li, Lc, reverse, prev=None):
    C = xr.shape[1]
    lrb, lib = jnp.broadcast_to(lr, (N_CHUNK, C)), jnp.broadcast_to(li, (N_CHUNK, C))
    zero = jnp.zeros((N_CHUNK, C), F32)

    def step_of(k):
        return (Lc - 1 - k) if reverse else k

    def advance(sr, si, t):
        return lrb * sr - lib * si + xr[_slab(t), :], lrb * si + lib * sr + xi[_slab(t), :]

    def pass1(k, c):
        return advance(c[0], c[1], step_of(k))

    er, ei = lax.fori_loop(0, Lc, pass1, (zero, zero), unroll=4)
    pr, pi = _cpow(lr, li, Lc)
    prb, pib = jnp.broadcast_to(pr, (N_CHUNK, C)), jnp.broadcast_to(pi, (N_CHUNK, C))
    cr, ci = zero, zero
    for _ in range(N_CHUNK - 1):
        cr, ci = _shift_chunks(er + prb * cr - pib * ci, ei + prb * ci + pib * cr, reverse)

    if prev is None:
        def pass2(k, c):
            t = step_of(k)
            nr, ni = advance(c[0], c[1], t)
            xr[_slab(t), :] = nr
            xi[_slab(t), :] = ni
            return nr, ni

        lax.fori_loop(0, Lc, pass2, (cr, ci), unroll=4)
        return None

    qr, qi = prev
    def pair(ar, ai, wr, wi, acc):
        return acc[0] + ar * wr + ai * wi, acc[1] + ai * wr - ar * wi

    def pass2(k, c):
        sr, si, accr, acci = c
        t = step_of(k)
        nr, ni = advance(sr, si, t)
        xr[_slab(t), :] = nr
        xi[_slab(t), :] = ni
        tp = t - 1 if reverse else t + 1
        accr, acci = pair(nr, ni, qr[_slab(tp), :], qi[_slab(tp), :], (accr, acci))
        return nr, ni, accr, acci

    sr, si, accr, acci = lax.fori_loop(0, Lc - 1, pass2, (cr, ci, zero, zero), unroll=4)
    t = step_of(Lc - 1)
    nr, ni = advance(sr, si, t)
    xr[_slab(t), :] = nr
    xi[_slab(t), :] = ni
    edge = step_of(0)
    wr, wi = _shift_chunks(qr[_slab(edge), :], qi[_slab(edge), :], not reverse)
    return pair(nr, ni, wr, wi, (accr, acci))


def _ssm_dims(T):
    assert T % ROW_BLOCK == 0
    return T // N_CHUNK, T // ROW_BLOCK


def _group_mask():
    shape = (SSM_BLOCK_GROUPS * SSM_GROUP, SSM_BLOCK_GROUPS * SSM_STATE)
    rows = lax.broadcasted_iota(jnp.int32, shape, 0) >> GROUP_SHIFT
    cols = lax.broadcasted_iota(jnp.int32, shape, 1) >> STATE_SHIFT
    return rows == cols


def _expand_groups(compact):
    return jnp.where(_group_mask(), jnp.tile(compact, (SSM_BLOCK_GROUPS, 1)), 0.0).astype(BF16)


def _collect_groups(block):
    kept = jnp.where(_group_mask(), block, 0.0)
    out = kept[0:SSM_GROUP]
    for g in range(1, SSM_BLOCK_GROUPS):
        out = out + kept[g * SSM_GROUP:(g + 1) * SSM_GROUP]
    return out


def _ssm_specs(T, direction):
    SC = SSM_BLOCK_GROUPS * SSM_STATE
    col = pl.BlockSpec((T, LANES), lambda m: (0, m))
    mat = pl.BlockSpec((None, None, SSM_GROUP, SC), lambda m: (direction, m, 0, 0))
    vec = pl.BlockSpec((None, None, 1, SC), lambda m: (direction, m, 0, 0))
    return col, mat, vec


def _ssm_project_in(u_ref, bre, bim, xr, xi, nblk):
    for blk in range(nblk):
        rows = pl.ds(blk * ROW_BLOCK, ROW_BLOCK)
        ub = u_ref[rows, :]
        xr[rows, :] = jnp.dot(ub, bre, preferred_element_type=F32)
        xi[rows, :] = jnp.dot(ub, bim, preferred_element_type=F32)


def _ssm_fwd(name, u_bf, mats, vecs, direction):
    T, W = u_bf.shape
    SC = SSM_BLOCK_GROUPS * SSM_STATE
    nb = W // LANES
    Lc, nblk = _ssm_dims(T)
    reverse = bool(direction)

    def kern(u_ref, bre_ref, bim_ref, cre_ref, cim_ref, lr_ref, li_ref, fr_ref, fi_ref, y_ref, xr, xi):
        _ssm_project_in(u_ref, _expand_groups(bre_ref[...]), _expand_groups(bim_ref[...]), xr, xi, nblk)
        _chunk_scan(xr, xi, lr_ref[...], li_ref[...], Lc, reverse)
        fr, fi = fr_ref[...], fi_ref[...]
        cre, cim = _expand_groups(cre_ref[...]), _expand_groups(cim_ref[...])
        for blk in range(nblk):
            rows = pl.ds(blk * ROW_BLOCK, ROW_BLOCK)
            sr, si = _cmul(fr, fi, xr[rows, :], xi[rows, :])
            y_ref[rows, :] = (lax.dot_general(sr.astype(BF16), cre, _NT, preferred_element_type=F32)
                              - lax.dot_general(si.astype(BF16), cim, _NT, preferred_element_type=F32))

    col, mat, vec = _ssm_specs(T, direction)
    return pl.pallas_call(
        kern, name=name, grid=(nb,),
        in_specs=[col, mat, mat, mat, mat, vec, vec, vec, vec],
        out_specs=col, out_shape=jax.ShapeDtypeStruct((T, W), F32),
        scratch_shapes=[pltpu.VMEM((T, SC), F32), pltpu.VMEM((T, SC), F32)],
        compiler_params=_params(("arbitrary",)),
    )(u_bf, *mats, *vecs)


def _ssm_bwd(name, u_bf, dy_bf, du_in, mats, vecs, direction):
    T, W = u_bf.shape
    SC = SSM_BLOCK_GROUPS * SSM_STATE
    nb = W // LANES
    Lc, nblk = _ssm_dims(T)
    reverse = bool(direction)

    def kern(u_ref, dy_ref, dui_ref, bre_ref, bim_ref, cre_ref, cim_ref, lr_ref, li_ref, fr_ref, fi_ref,
             du_ref, dbre_ref, dbim_ref, dcre_ref, dcim_ref, dlr_ref, dli_ref, dfr_ref, dfi_ref,
             xr, xi, gr, gi):
        lr, li, fr, fi = lr_ref[...], li_ref[...], fr_ref[...], fi_ref[...]
        bre, bim = _expand_groups(bre_ref[...]), _expand_groups(bim_ref[...])
        cre, cim = _expand_groups(cre_ref[...]), _expand_groups(cim_ref[...])
        _ssm_project_in(u_ref, bre, bim, xr, xi, nblk)
        _chunk_scan(xr, xi, lr, li, Lc, reverse)
        mre = jnp.zeros((LANES, SC), F32)
        mim = jnp.zeros((LANES, SC), F32)
        dfr = jnp.zeros((1, SC), F32)
        dfi = jnp.zeros((1, SC), F32)
        for blk in range(nblk):
            rows = pl.ds(blk * ROW_BLOCK, ROW_BLOCK)
            dy = dy_ref[rows, :]
            x_r, x_i = xr[rows, :], xi[rows, :]
            g_r = jnp.dot(dy, cre, preferred_element_type=F32)
            g_i = -jnp.dot(dy, cim, preferred_element_type=F32)
            mre += lax.dot_general(dy, x_r.astype(BF16), _TN, preferred_element_type=F32)
            mim += lax.dot_general(dy, x_i.astype(BF16), _TN, preferred_element_type=F32)
            dfr += _colsum(g_r * x_r + g_i * x_i)
            dfi += _colsum(g_i * x_r - g_r * x_i)
            gr[rows, :] = fr * g_r + fi * g_i
            gi[rows, :] = fr * g_i - fi * g_r
        dcre_ref[...] = _collect_groups(fr * mre - fi * mim)
        dcim_ref[...] = _collect_groups(-(fr * mim + fi * mre))
        dfr_ref[...] = dfr
        dfi_ref[...] = dfi
        accr, acci = _chunk_scan(gr, gi, lr, -li, Lc, not reverse, prev=(xr, xi))
        dlr_ref[...] = _colsum(accr)
        dli_ref[...] = _colsum(acci)
        dbre = jnp.zeros((LANES, SC), F32)
        dbim = jnp.zeros((LANES, SC), F32)
        for blk in range(nblk):
            rows = pl.ds(blk * ROW_BLOCK, ROW_BLOCK)
            ub = u_ref[rows, :]
            a_r, a_i = gr[rows, :].astype(BF16), gi[rows, :].astype(BF16)
            dbre += lax.dot_general(ub, a_r, _TN, preferred_element_type=F32)
            dbim += lax.dot_general(ub, a_i, _TN, preferred_element_type=F32)
            du_ref[rows, :] = (dui_ref[rows, :] + lax.dot_general(a_r, bre, _NT, preferred_element_type=F32)
                               + lax.dot_general(a_i, bim, _NT, preferred_element_type=F32))
        dbre_ref[...] = _collect_groups(dbre)
        dbim_ref[...] = _collect_groups(dbim)

    col, mat, vec = _ssm_specs(T, direction)
    mat_out = pl.BlockSpec((None, SSM_GROUP, SC), lambda m: (m, 0, 0))
    vec_out = pl.BlockSpec((None, 1, SC), lambda m: (m, 0, 0))
    mat_shape = jax.ShapeDtypeStruct((nb, SSM_GROUP, SC), F32)
    vec_shape = jax.ShapeDtypeStruct((nb, 1, SC), F32)
    return pl.pallas_call(
        kern, name=name, grid=(nb,),
        in_specs=[col, col, col, mat, mat, mat, mat, vec, vec, vec, vec],
        out_specs=[col, mat_out, mat_out, mat_out, mat_out, vec_out, vec_out, vec_out, vec_out],
        out_shape=[jax.ShapeDtypeStruct((T, W), F32), mat_shape, mat_shape, mat_shape, mat_shape,
                   vec_shape, vec_shape, vec_shape, vec_shape],
        scratch_shapes=[pltpu.VMEM((T, SC), F32)] * 4,
        compiler_params=_params(("arbitrary",)),
    )(u_bf, dy_bf, du_in, *mats, *vecs)


def _to_chunk_rows(a):
    T, C = a.shape
    return a.reshape(N_CHUNK, T // N_CHUNK, C).transpose(1, 0, 2).reshape(T, C)


def _from_chunk_rows(a):
    T, C = a.shape
    return a.reshape(T // N_CHUNK, N_CHUNK, C).transpose(1, 0, 2).reshape(T, C)


def _groups_side_by_side(p, channel_axis):
    d, G = p.shape[:2]
    nb = G // SSM_BLOCK_GROUPS
    v = p.reshape(d, nb, SSM_BLOCK_GROUPS, p.shape[2], p.shape[3])
    v = v.transpose(0, 1, 4, 2, 3) if channel_axis == 3 else v.transpose(0, 1, 3, 2, 4)
    return v.reshape(d, nb, SSM_GROUP, SSM_BLOCK_GROUPS * SSM_STATE)


def _groups_apart(m, channel_axis):
    d, nb = m.shape[:2]
    v = m.reshape(d, nb, SSM_GROUP, SSM_BLOCK_GROUPS, SSM_STATE)
    v = v.transpose(0, 1, 3, 4, 2) if channel_axis == 3 else v.transpose(0, 1, 3, 2, 4)
    return v.reshape((d, nb * SSM_BLOCK_GROUPS) + v.shape[3:])


def _conv_taps(ref, c, R, T):
    n = T // R
    r0 = pl.multiple_of(c * R, R)
    main = ref[pl.ds(r0, R), :]
    before = ref[pl.ds(pl.multiple_of(jnp.maximum(r0 - 8, 0), 8), 8), :]
    after = ref[pl.ds(pl.multiple_of(jnp.minimum(r0 + R, T - 8), 8), 8), :]
    ext = jnp.concatenate([jnp.where(c > 0, before, 0.0), main, jnp.where(c < n - 1, after, 0.0)], axis=0)
    m1 = pltpu.roll(ext, 1, 0)[8:8 + R]
    p1 = pltpu.roll(ext, R + 15, 0)[8:8 + R]
    return m1, main, p1, r0


def _conv_specs(T, F, cb):
    nj = F // cb
    lo = lambda rows: pl.BlockSpec((rows, cb), lambda j: (0, j))
    hi = lambda rows: pl.BlockSpec((rows, cb), lambda j: (0, j + nj))
    return nj, lo, hi


def _convact_fwd(up, conv_w, conv_b, F):
    T = up.shape[0]
    cb = _pick(F, (256, 128))
    R = _pick(T, (256, 128, 64))
    nj, lo, hi = _conv_specs(T, F, cb)

    def kern(uv, ug, wv, wg, bv, bg, o_ref):
        def body(c, carry):
            v1, v0, v2, r0 = _conv_taps(uv, c, R, T)
            g1, g0, g2, _ = _conv_taps(ug, c, R, T)
            val = v1 * wv[0:1, :] + v0 * wv[1:2, :] + v2 * wv[2:3, :] + bv[...]
            gate = g1 * wg[0:1, :] + g0 * wg[1:2, :] + g2 * wg[2:3, :] + bg[...]
            o_ref[pl.ds(r0, R), :] = (val * (gate * _sigmoid(gate))).astype(o_ref.dtype)
            return carry

        lax.fori_loop(0, T // R, body, 0)

    return pl.pallas_call(
        kern, name="convact_fwd", grid=(nj,),
        in_specs=[lo(T), hi(T), lo(3), hi(3), lo(1), hi(1)],
        out_specs=lo(T), out_shape=jax.ShapeDtypeStruct((T, F), BF16),
        compiler_params=_params(("arbitrary",)),
    )(up, up, conv_w, conv_w, conv_b, conv_b)


def _convact_bwd(up, dact, conv_w, conv_b, F):
    T = up.shape[0]
    cb = _pick(F, (256, 128))
    R = _pick(T, (256, 128, 64))
    nj, lo, hi = _conv_specs(T, F, cb)

    def kern(uv, ug, da, wv, wg, bv, bg, dupv, dupg, dwv, dwg, dbv, dbg, sv, sg):
        zero = jnp.zeros((1, cb), F32)

        def pass_a(c, acc):
            v1, v0, v2, r0 = _conv_taps(uv, c, R, T)
            g1, g0, g2, _ = _conv_taps(ug, c, R, T)
            val = v1 * wv[0:1, :] + v0 * wv[1:2, :] + v2 * wv[2:3, :] + bv[...]
            gate = g1 * wg[0:1, :] + g0 * wg[1:2, :] + g2 * wg[2:3, :] + bg[...]
            sig = _sigmoid(gate)
            d = da[pl.ds(r0, R), :]
            dval = d * (gate * sig)
            dgate = d * val * (sig * (1.0 + gate * (1.0 - sig)))
            sv[pl.ds(r0, R), :] = dval
            sg[pl.ds(r0, R), :] = dgate
            terms = (dval * v1, dval * v0, dval * v2, dval, dgate * g1, dgate * g0, dgate * g2, dgate)
            return tuple(a + _colsum(t) for a, t in zip(acc, terms))

        acc = lax.fori_loop(0, T // R, pass_a, (zero,) * 8)
        for k in range(3):
            dwv[k:k + 1, :] = acc[k]
            dwg[k:k + 1, :] = acc[4 + k]
        dbv[...] = acc[3]
        dbg[...] = acc[7]

        def pass_b(c, carry):
            v1, v0, v2, r0 = _conv_taps(sv, c, R, T)
            g1, g0, g2, _ = _conv_taps(sg, c, R, T)
            dupv[pl.ds(r0, R), :] = (v2 * wv[0:1, :] + v0 * wv[1:2, :] + v1 * wv[2:3, :]).astype(dupv.dtype)
            dupg[pl.ds(r0, R), :] = (g2 * wg[0:1, :] + g0 * wg[1:2, :] + g1 * wg[2:3, :]).astype(dupg.dtype)
            return carry

        lax.fori_loop(0, T // R, pass_b, 0)

    dupv, dupg, dwv, dwg, dbv, dbg = pl.pallas_call(
        kern, name="convact_bwd", grid=(nj,),
        in_specs=[lo(T), hi(T), lo(T), lo(3), hi(3), lo(1), hi(1)],
        out_specs=[lo(T), lo(T), lo(3), lo(3), lo(1), lo(1)],
        out_shape=[jax.ShapeDtypeStruct((T, F), BF16), jax.ShapeDtypeStruct((T, F), BF16),
                   jax.ShapeDtypeStruct((3, F), F32), jax.ShapeDtypeStruct((3, F), F32),
                   jax.ShapeDtypeStruct((1, F), F32), jax.ShapeDtypeStruct((1, F), F32)],
        scratch_shapes=[pltpu.VMEM((T, cb), F32), pltpu.VMEM((T, cb), F32)],
        compiler_params=_params(("arbitrary",)),
    )(up, up, dact, conv_w, conv_w, conv_b, conv_b)
    return (dupv, dupg), (dwv, dwg), jnp.concatenate([dbv, dbg], axis=1)


def _peers(x, y, c, with_self=False):
    out = []
    for k in range(0 if with_self else 1, N_DEV):
        px = 1 - x if k & 4 else x
        py = 1 - y if k & 2 else y
        pc = 1 - c if k & 1 else c
        out.append((k, (px, py, pc), 4 * px + 2 * py + pc))
    return out


def _exchange_start(name, bufs, modes, after):
    n = len(bufs)
    ncopy = n * N_DEV
    lands = [lax.empty((N_DEV,) + tuple(b.shape[-2:]), b.dtype) for b in bufs]

    def body(*refs):
        ins, land_in = refs[:n], refs[n:2 * n]
        send_sems, recv_sems = refs[2 * n + 1], refs[2 * n + 2]
        token = refs[-1]
        x, y, c = lax.axis_index("x"), lax.axis_index("y"), lax.axis_index("c")
        me = 4 * x + 2 * y + c
        for b in range(n):
            for k, peer, slot in _peers(x, y, c, with_self=True):
                sem = b * N_DEV + k
                pltpu.make_async_remote_copy(
                    src_ref=ins[b] if modes[b] == 'gather' else ins[b].at[slot], dst_ref=land_in[b].at[me],
                    send_sem=send_sems.at[sem], recv_sem=recv_sems.at[sem],
                    device_id=peer, device_id_type=MESH_ID).start()
        token[...] = jnp.zeros_like(token)

    hbm = pl.BlockSpec(memory_space=pltpu.HBM)
    sem_spec = pl.BlockSpec(memory_space=pltpu.SEMAPHORE)
    operands = [pltpu.with_memory_space_constraint(a, pltpu.HBM) for a in list(bufs) + lands]
    res = pl.pallas_call(
        body, name=name,
        out_shape=(pltpu.SemaphoreType.DMA((ncopy,)), pltpu.SemaphoreType.DMA((ncopy,)),
                   *[pltpu.HBM(a.shape, a.dtype) for a in operands], jax.ShapeDtypeStruct((8, LANES), F32)),
        in_specs=[hbm] * (2 * n) + [pl.BlockSpec(memory_space=pl.ANY)],
        out_specs=(sem_spec, sem_spec, *([hbm] * (2 * n)), pl.BlockSpec(memory_space=pltpu.VMEM)),
        input_output_aliases={i: 2 + i for i in range(2 * n)},
        compiler_params=pltpu.CompilerParams(has_side_effects=pltpu.SideEffectType.DATAFLOW_SIDE_EFFECTING),
    )(*operands, after)
    return (res[0], res[1], list(res[2:2 + n]), list(res[2 + n:2 + 2 * n]), tuple(modes)), res[-1]


def _exchange_wait(name, handle, after):
    send_sems, recv_sems, srcs, lands, modes = handle
    n = len(srcs)

    def body(*refs):
        src_in, land_in = refs[:n], refs[n:2 * n]
        send_ref, recv_ref = refs[2 * n], refs[2 * n + 1]
        x, y, c = lax.axis_index("x"), lax.axis_index("y"), lax.axis_index("c")
        for b in range(n):
            for k, peer, slot in _peers(x, y, c, with_self=True):
                sem = b * N_DEV + k
                copy = pltpu.make_async_remote_copy(
                    src_ref=src_in[b] if modes[b] == 'gather' else src_in[b].at[slot], dst_ref=land_in[b].at[slot],
                    send_sem=send_ref.at[sem], recv_sem=recv_ref.at[sem],
                    device_id=peer, device_id_type=MESH_ID)
                copy.wait_send()
                copy.wait_recv()

    hbm = pl.BlockSpec(memory_space=pltpu.HBM)
    sem_spec = pl.BlockSpec(memory_space=pltpu.SEMAPHORE)
    res = pl.pallas_call(
        body, name=name,
        out_shape=tuple(pltpu.HBM(a.shape, a.dtype) for a in srcs + lands),
        in_specs=[hbm] * (2 * n) + [sem_spec, sem_spec, pl.BlockSpec(memory_space=pl.ANY)],
        out_specs=tuple([hbm] * (2 * n)),
        input_output_aliases={i: i for i in range(2 * n)},
        compiler_params=pltpu.CompilerParams(has_side_effects=pltpu.SideEffectType.DATAFLOW_SIDE_EFFECTING),
    )(*srcs, *lands, send_sems, recv_sems, after)
    return list(res[n:])


def _adamw(w, g, m, v):
    m2 = ADAM_B1 * m + (1.0 - ADAM_B1) * g
    v2 = ADAM_B2 * v + (1.0 - ADAM_B2) * (g * g)
    m_hat = m2 / (1.0 - ADAM_B1 ** ADAM_STEP)
    v_hat = v2 / (1.0 - ADAM_B2 ** ADAM_STEP)
    return -ADAM_LR * (m_hat / (jnp.sqrt(v_hat) + ADAM_EPS) + ADAM_WD * w), m2, v2


def _sum_adamw(name, landed, w, m, v):
    R, C = w.shape
    row_bytes = N_DEV * (-(-C // LANES) * LANES) * landed.dtype.itemsize
    tiles = [d for d in range(16, R, 16) if R % d == 0 and d * row_bytes <= ADAM_TILE_BYTES]
    tr = max(tiles) if tiles and R * row_bytes > ADAM_TILE_BYTES else R

    def kern(x_ref, w_ref, m_ref, v_ref, g_out, d_out, m_out, v_out):
        g = x_ref[0].astype(F32)
        for q in range(1, N_DEV):
            g = g + x_ref[q].astype(F32)
        g_out[...] = g
        d_out[...], m_out[...], v_out[...] = _adamw(w_ref[...], g, m_ref[...], v_ref[...])

    blk = pl.BlockSpec((tr, C), lambda i: (i, 0))
    return pl.pallas_call(
        kern, name=name, grid=(R // tr,),
        in_specs=[pl.BlockSpec((N_DEV, tr, C), lambda i: (0, i, 0)), blk, blk, blk],
        out_specs=[blk] * 4, out_shape=[jax.ShapeDtypeStruct((R, C), F32)] * 4,
        compiler_params=_params(("arbitrary",)),
    )(landed, w, m, v)


def _sum_slots(name, landed):
    def kern(x_ref, o_ref):
        g = x_ref[0].astype(F32)
        for q in range(1, N_DEV):
            g = g + x_ref[q].astype(F32)
        o_ref[...] = g

    return pl.pallas_call(kern, name=name, out_shape=jax.ShapeDtypeStruct(landed.shape[1:], F32))(landed)


def _adamw_whole(name, g, w, m, v):
    def kern(g_ref, w_ref, m_ref, v_ref, d_out, m_out, v_out):
        d_out[...], m_out[...], v_out[...] = _adamw(w_ref[...], g_ref[...], m_ref[...], v_ref[...])

    return pl.pallas_call(kern, name=name, out_shape=[jax.ShapeDtypeStruct(w.shape, F32)] * 3)(g, w, m, v)


def _view2d(a):
    if a.ndim == 1:
        return a.reshape(1, -1)
    return a.reshape(-1, a.shape[-1])


def kernel(x, norm_mix_g, w_in, pool_w, pool_scale, ssm_log_neg_a_re, ssm_a_im, ssm_log_dt, ssm_b_re, ssm_b_im, ssm_c_re, ssm_c_im, ssm_d, glu_w, glu_b, out_norm_pool_g, out_norm_ssm_g, w_out, norm_ffn_g, w_up, conv_w, conv_b, w_down, final_norm_g, loss_target, m_norm_mix_g, m_w_in, m_pool_w, m_pool_scale, m_ssm_log_neg_a_re, m_ssm_a_im, m_ssm_log_dt, m_ssm_b_re, m_ssm_b_im, m_ssm_c_re, m_ssm_c_im, m_ssm_d, m_glu_w, m_glu_b, m_out_norm_pool_g, m_out_norm_ssm_g, m_w_out, m_norm_ffn_g, m_w_up, m_conv_w, m_conv_b, m_w_down, m_final_norm_g, v_norm_mix_g, v_w_in, v_pool_w, v_pool_scale, v_ssm_log_neg_a_re, v_ssm_a_im, v_ssm_log_dt, v_ssm_b_re, v_ssm_b_im, v_ssm_c_re, v_ssm_c_im, v_ssm_d, v_glu_w, v_glu_b, v_out_norm_pool_g, v_out_norm_ssm_g, v_w_out, v_norm_ffn_g, v_w_up, v_conv_w, v_conv_b, v_w_down, v_final_norm_g):
    given = dict(locals())
    weights = {n: given[n] for n in WEIGHTS}
    mom1 = {n: given["m_" + n] for n in WEIGHTS}
    mom2 = {n: given["v_" + n] for n in WEIGHTS}

    xs = x[0]
    tgt = loss_target[0]
    T, D = xs.shape
    DP = len(POOL_WINDOWS) * POOL_GROUP
    DS = D - DP
    G = DS // SSM_GROUP
    N, H = SSM_STATE, SSM_GROUP
    F2 = w_up.shape[1] * N_DEV
    F = F2 // 2
    fs = w_up.shape[1]
    row = lambda a: a.reshape(1, -1)

    early_handle, token = _exchange_start("gather_early_start", [w_in.astype(BF16), glu_w.astype(BF16)],
                                          ['gather'] * 2, after=w_in)
    late_src = [w_out.astype(BF16), w_down.astype(BF16), w_up.T.astype(BF16), conv_w]
    late_handle, token = _exchange_start("gather_late_start", late_src, ['gather'] * 4, after=token)
    CB = row(conv_b)

    g1, g2, g3 = row(norm_mix_g), row(norm_ffn_g), row(final_norm_g)
    gp, gs = row(out_norm_pool_g), row(out_norm_ssm_g)
    (xn,) = _rowmap("norm_mix", _rms_fwd, [xs], [g1 + token[0:1, 0:1]], [(D, BF16)])
    early = _exchange_wait("gather_early_wait", early_handle, xn)
    W_in = early[0].reshape(D, D)
    W_glu = early[1].reshape(DS, DS)
    u = _mm("proj_in", xn, W_in, 'nn', F32)
    tmp = _pick(T, (256, 128))
    pool_w_bf = pool_w.astype(BF16)
    ypn = _pool_fwd(u, pool_w_bf, row(pool_scale), gp, tmp)

    u_ssm = _to_chunk_rows(u[:, DP:])
    u_ssm_bf = u_ssm.astype(BF16)
    a_rows = (ssm_log_neg_a_re.reshape(2 * G, N), ssm_a_im.reshape(2 * G, N), ssm_log_dt.reshape(2 * G, 1))
    disc = _ssm_params_fwd(*a_rows)
    nb = G // SSM_BLOCK_GROUPS
    vecs = [d.reshape(2, nb, 1, SSM_BLOCK_GROUPS * N) for d in disc]
    mats = [_groups_side_by_side(ssm_b_re, 3), _groups_side_by_side(ssm_b_im, 3),
            _groups_side_by_side(ssm_c_re, 2), _groups_side_by_side(ssm_c_im, 2)]
    y_dir = [_ssm_fwd("ssm_fwd_%d" % d, u_ssm_bf, mats, vecs, d) for d in range(2)]

    def mix_post(yf, yb, us, d, gw, gb, g):
        y = yf + yb + d * us
        z = _gelu(y)
        gate = _sigmoid(jnp.dot(z.astype(BF16), gw, preferred_element_type=F32) + gb)
        return _rms_fwd(z * gate, g), y

    ysn, y_ssm = _rowmap("ssm_post", mix_post, [y_dir[0], y_dir[1], u_ssm], [row(ssm_d), W_glu, row(glu_b), gs],
                         [(DS, BF16), (DS, F32)])
    ycat = jnp.concatenate([ypn, _from_chunk_rows(ysn)], axis=1)
    late = _exchange_wait("gather_late_wait", late_handle, ysn)
    W_out = late[0].reshape(D, D)
    W_down = late[1].reshape(F, D)
    W_up_t = late[2].reshape(F2, D)
    CW = late[3].transpose(1, 0, 2).reshape(3, F2)
    h1 = _mm("proj_out", ycat, W_out, 'nn', F32, add=xs)
    (hn,) = _rowmap("norm_ffn", _rms_fwd, [h1], [g2], [(D, BF16)])
    up = _mm("ffn_up", hn, W_up_t, 'nt', F32)
    act = _convact_fwd(up, CW, CB, F)
    h2 = _mm("ffn_down", act, W_down, 'nn', F32, add=h1)

    def head(h, t, g):
        r = lax.rsqrt(_rowmean(h * h) + EPS)
        hh = h * r
        e = hh * g - t
        loss = 0.5 * jnp.sum(_rowmean(e * e), keepdims=True)
        dy = e * (1.0 / D)
        dxh = dy * g
        dh = r * (dxh - hh * _rowmean(dxh * hh))
        return dh, dh, jnp.broadcast_to(loss, (1, LANES)), _colsum(dy * hh)

    dh2, dh2_bf, loss_acc, dg3 = _rowmap("loss_head", head, [h2, tgt], [g3], [(D, F32), (D, BF16)], [(1, LANES), (1, D)])

    dact = _mm("ffn_down_dx", dh2_bf, W_down, 'nt', F32)
    dW_down = _mm("ffn_down_dw", act, dh2_bf, 'tn', BF16)
    dup, dCW, dCB = _convact_bwd(up, dact, CW, CB, F)
    dhn = _mm("ffn_up_dx_val", dup[0], W_up_t, 'nn', F32, b_rows=(2, 0))
    dhn = _mm("ffn_up_dx_gate", dup[1], W_up_t, 'nn', F32, b_rows=(2, 1), add=dhn)
    dW_up_t = jnp.concatenate([_mm("ffn_up_dw_val", dup[0], hn, 'tn', BF16), _mm("ffn_up_dw_gate", dup[1], hn, 'tn', BF16)], axis=0)

    def shards(full_grad):
        return full_grad.reshape((N_DEV, full_grad.shape[0] // N_DEV) + full_grad.shape[1:])

    half = N_DEV // 2
    dCW_sh = jnp.concatenate([h.reshape(3, half, fs).transpose(1, 0, 2) for h in dCW], axis=0)
    ffn_sent = [shards(dW_down), shards(dW_up_t), dCW_sh]
    ffn_handle, token = _exchange_start("scatter_ffn_start", ffn_sent, ['scatter'] * 3, after=dhn)
    g2_late = g2 + token[0:1, 0:1]

    def norm_bwd_add(hx, dy, res, g):
        dx, dg = _rms_bwd(hx, g, dy)
        return dx + res, dg

    def norm_bwd_add2(hx, dy, res, g):
        dx, dg = _rms_bwd(hx, g, dy)
        return dx + res, dx + res, dg

    dh1, dh1_bf, dg2 = _rowmap("norm_ffn_bwd", norm_bwd_add2, [h1, dhn, dh2], [g2_late], [(D, F32), (D, BF16)], [(1, D)])
    dycat = _mm("proj_out_dx", dh1_bf, W_out, 'nt', F32)
    dW_out = _mm("proj_out_dw", ycat, dh1_bf, 'tn', BF16)

    out_sent = [shards(dW_out)]
    out_handle, token = _exchange_start("scatter_out_start", out_sent, ['scatter'], after=dycat)
    q, dpooled, dpool_w, dpool_scale, dgp = _pool_bwd_rows(u, dycat, pool_w_bf, row(pool_scale),
                                                           gp + token[0:1, 0:1], tmp)
    du_pool = _pool_bwd_band(q, dpooled, tmp)

    def mix_post_bwd(y, us, dyn, d, gw, gb, g):
        z = _gelu(y)
        gz = z.astype(BF16)
        sig = _sigmoid(jnp.dot(gz, gw, preferred_element_type=F32) + gb)
        dys, dg = _rms_bwd(z * sig, g, dyn)
        dgl = dys * z * sig * (1.0 - sig)
        dgl_bf = dgl.astype(BF16)
        dz = dys * sig + lax.dot_general(dgl_bf, gw, (((1,), (1,)), ((), ())), preferred_element_type=F32)
        dgw = lax.dot_general(gz, dgl_bf, (((0,), (0,)), ((), ())), preferred_element_type=F32)
        dyv = dz * _gelu_grad(y)
        return dyv, dyv * d, dgw, _colsum(dgl), _colsum(dyv * us), dg

    dyv, du_dir, dglu_w, dglu_b, dssm_d, dgs = _rowmap(
        "ssm_post_bwd", mix_post_bwd, [y_ssm, u_ssm, _to_chunk_rows(dycat[:, DP:])], [row(ssm_d), W_glu, row(glu_b), gs],
        [(DS, BF16), (DS, F32)], [(DS, DS), (1, DS), (1, DS), (1, DS)])

    du_ssm = du_dir
    dB, dC, dvec = [], [], []
    for d in range(2):
        res = _ssm_bwd("ssm_bwd_%d" % d, u_ssm_bf, dyv, du_ssm, mats, vecs, d)
        du_ssm = res[0]
        dB.append(res[1:3])
        dC.append(res[3:5])
        dvec.append(res[5:9])
    grad_b = [_groups_apart(jnp.stack([dB[0][k], dB[1][k]]), 3) for k in range(2)]
    grad_c = [_groups_apart(jnp.stack([dC[0][k], dC[1][k]]), 2) for k in range(2)]
    cots = [jnp.stack([dvec[0][k], dvec[1][k]]).reshape(2 * G, N) for k in range(4)]
    d_a_re, d_a_im, d_log_dt = _ssm_params_bwd(*a_rows, cots)

    rep_grads = {
        'pool_w': dpool_w, 'pool_scale': dpool_scale, 'ssm_log_neg_a_re': d_a_re, 'ssm_a_im': d_a_im,
        'ssm_log_dt': d_log_dt, 'ssm_b_re': grad_b[0], 'ssm_b_im': grad_b[1], 'ssm_c_re': grad_c[0], 'ssm_c_im': grad_c[1],
        'ssm_d': dssm_d, 'glu_b': dglu_b, 'out_norm_pool_g': dgp, 'out_norm_ssm_g': dgs, 'norm_ffn_g': dg2,
        'conv_b': dCB, 'final_norm_g': dg3}
    wide = ('pool_w', 'ssm_b_re', 'ssm_b_im', 'ssm_c_re', 'ssm_c_im')

    def travel(n):
        g = rep_grads[n]
        if n in wide:
            return g.reshape(-1, PACK_W if g.size % PACK_W == 0 else LANES).astype(BF16)
        return _view2d(g.reshape(weights[n].shape))

    small = tuple(n for n in REPLICATED if n != 'norm_mix_g')
    small_sent = [shards(dglu_w.astype(BF16))] + [travel(n) for n in small] + [loss_acc]
    small_handle, token = _exchange_start("reduce_small_start", small_sent, ['scatter'] + ['gather'] * (len(small) + 1),
                                          after=d_a_re)

    du = (jnp.concatenate([du_pool, _from_chunk_rows(du_ssm)], axis=1) + token[0:1, 0:1]).astype(BF16)
    dW_in = _mm("proj_in_dw", xn, du, 'tn', BF16)
    in_handle, token = _exchange_start("scatter_in_start", [shards(dW_in)], ['scatter'], after=dW_in)
    dxn = _mm("proj_in_dx", du, W_in + token[0, 0].astype(BF16), 'nt', F32)
    dx, dg1 = _rowmap("norm_mix_bwd", norm_bwd_add, [xs, dxn, dh1], [g1], [(D, F32)], [(1, D)])
    rep_grads['norm_mix_g'] = dg1
    last_handle, token = _exchange_start("reduce_last_start", [travel('norm_mix_g')], ['gather'], after=dx)

    grads, delta, new_m, new_v = {}, {}, {}, {}

    def update(n, landed_n):
        shape = weights[n].shape
        if len(shape) > 2 and shape[-1] < LANES // 2:
            grads[n] = _sum_slots("sum_" + n, landed_n).reshape(shape)
            delta[n], new_m[n], new_v[n] = _adamw_whole("adamw_" + n, grads[n], weights[n], mom1[n], mom2[n])
            return
        if n == 'w_up':
            res = _sum_adamw("adamw_" + n, landed_n, weights[n].T, mom1[n].T, mom2[n].T)
            grads[n], delta[n], new_m[n], new_v[n] = [r.T for r in res]
            return
        landed_n = landed_n.reshape((N_DEV,) + _view2d(weights[n]).shape)
        res = _sum_adamw("adamw_" + n, landed_n, _view2d(weights[n]), _view2d(mom1[n]), _view2d(mom2[n]))
        grads[n], delta[n], new_m[n], new_v[n] = [r.reshape(shape) for r in res]

    ffn_landed = _exchange_wait("scatter_ffn_wait", ffn_handle, token)
    for n, l in zip(('w_down', 'w_up', 'conv_w'), ffn_landed):
        update(n, l)
    (out_landed,) = _exchange_wait("scatter_out_wait", out_handle, token)
    update('w_out', out_landed)
    small_landed = _exchange_wait("reduce_small_wait", small_handle, token)
    for n, l in zip(('glu_w',) + small, small_landed):
        update(n, l)

    def loss_sum(parts_ref, o_ref):
        s = parts_ref[0]
        for q in range(1, N_DEV):
            s = s + parts_ref[q]
        o_ref[...] = s

    loss = pl.pallas_call(loss_sum, name="loss_sum", out_shape=jax.ShapeDtypeStruct((1, LANES), F32))(small_landed[-1])[0, 0]

    update('w_in', _exchange_wait("scatter_in_wait", in_handle, grads[small[-1]])[0])
    update('norm_mix_g', _exchange_wait("reduce_last_wait", last_handle, grads['w_in'])[0])

    return (loss, dx[None], *[grads[n] for n in WEIGHTS], *[delta[n] for n in WEIGHTS],
            *[new_m[n] for n in WEIGHTS], *[new_v[n] for n in WEIGHTS])
```

```python
import functools

import jax
import jax.numpy as jnp
import numpy as np
from jax import lax
from jax.experimental import pallas as pl
from jax.experimental.pallas import tpu as pltpu

F32 = jnp.float32
BF16 = jnp.bfloat16
MESH_ID = pl.DeviceIdType.MESH

N_DEV = 8
EPS = 1e-6
POOL_WINDOWS = (2, 4, 8, 16)
POOL_GROUP = 128
POOL_PAD = 64
SSM_GROUP = 16
SSM_STATE = 64
SSM_BLOCK_GROUPS = 8
N_CHUNK = 8
ROW_BLOCK = 512
SCAN_UNROLL = 4
GROUP_SHIFT = 4
STATE_SHIFT = 6
_NT = (((1,), (1,)), ((), ()))
_TN = (((0,), (0,)), ((), ()))
LANES = 128
PACK_W = 1024
VMEM_LIMIT = 56 * 1024 * 1024
MM_VMEM_BUDGET = 40 * 1024 * 1024
MM_TILE_CAP = 1408
ADAM_TILE_BYTES = 4 * 1024 * 1024

ADAM_LR = 0.001
ADAM_B1 = 0.9
ADAM_B2 = 0.999
ADAM_EPS = 1e-08
ADAM_WD = 0.01
ADAM_STEP = 10

WEIGHTS = ['norm_mix_g', 'w_in', 'pool_w', 'pool_scale', 'ssm_log_neg_a_re', 'ssm_a_im', 'ssm_log_dt',
           'ssm_b_re', 'ssm_b_im', 'ssm_c_re', 'ssm_c_im', 'ssm_d', 'glu_w', 'glu_b', 'out_norm_pool_g',
           'out_norm_ssm_g', 'w_out', 'norm_ffn_g', 'w_up', 'conv_w', 'conv_b', 'w_down', 'final_norm_g']
SHARDED = ('w_in', 'w_out', 'w_down', 'glu_w', 'w_up', 'conv_w')
REPLICATED = tuple(n for n in WEIGHTS if n not in SHARDED)


def _pick(n, prefs):
    for p in prefs:
        if n % p == 0:
            return p
    return n


def _params(sem, vmem=None):
    return pltpu.CompilerParams(dimension_semantics=sem, vmem_limit_bytes=vmem or VMEM_LIMIT)


def _tiles(n, cap):
    return [d for d in range(LANES, min(n, cap) + 1, LANES) if n % d == 0] or [n]


def _mm(name, a, b, mode, out_dtype, add=None, b_rows=(1, 0)):
    if mode == 'nn':
        (M, K), N = a.shape, b.shape[1]
        assert b.shape[0] == K * b_rows[0]
    elif mode == 'nt':
        (M, K), (N, _) = a.shape, b.shape
    else:
        (K, M), (_, N) = a.shape, b.shape
    dims = {'nn': (((1,), (0,)), ((), ())), 'nt': (((1,), (1,)), ((), ())), 'tn': (((0,), (0,)), ((), ()))}[mode]
    sa, sb, so = a.dtype.itemsize, b.dtype.itemsize, jnp.dtype(out_dtype).itemsize
    best = None
    for tm in _tiles(M, MM_TILE_CAP):
        for tn in _tiles(N, MM_TILE_CAP):
            need = 2 * (tm * K * sa + tn * K * sb + tm * tn * (so + (4 if add is not None else 0)))
            if need <= MM_VMEM_BUDGET:
                key = (tm * tn / (tm + tn), tm * tn)
                if best is None or key > best[0]:
                    best = (key, tm, tn)
    _, tm, tn = best
    rows_inner = a.size * sa * (N // tn) + b.size * sb < a.size * sa + b.size * sb * (M // tm)

    def kern(*refs):
        a_ref, b_ref = refs[:2]
        o_ref = refs[-1]
        r = lax.dot_general(a_ref[...].astype(BF16), b_ref[...].astype(BF16), dims, preferred_element_type=F32)
        if add is not None:
            r = r + refs[2][...]
        o_ref[...] = r.astype(o_ref.dtype)

    if rows_inner:
        grid = (N // tn, M // tm)
        ij = lambda g0, g1: (g1, g0)
    else:
        grid = (M // tm, N // tn)
        ij = lambda g0, g1: (g0, g1)
    a_spec = (pl.BlockSpec((K, tm), lambda g0, g1: (0, ij(g0, g1)[0])) if mode == 'tn'
              else pl.BlockSpec((tm, K), lambda g0, g1: (ij(g0, g1)[0], 0)))
    b_spec = (pl.BlockSpec((tn, K), lambda g0, g1: (ij(g0, g1)[1], 0)) if mode == 'nt'
              else pl.BlockSpec((K, tn), lambda g0, g1: (b_rows[1], ij(g0, g1)[1])))
    in_specs = [a_spec, b_spec]
    args = [a, b]
    if add is not None:
        in_specs.append(pl.BlockSpec((tm, tn), lambda g0, g1: ij(g0, g1)))
        args.append(add)
    return pl.pallas_call(
        kern, name=name, grid=grid, in_specs=in_specs,
        out_specs=pl.BlockSpec((tm, tn), lambda g0, g1: ij(g0, g1)),
        out_shape=jax.ShapeDtypeStruct((M, N), out_dtype),
        compiler_params=_params(("parallel", "parallel")),
    )(*args)


def _rowmap(name, fn, rows, fulls, out_rows, out_accs=(), tm=256):
    rows = [r if isinstance(r, tuple) else (r, r.shape[1], 0) for r in rows]
    T = rows[0][0].shape[0]
    tm = min(tm, T)
    assert T % tm == 0
    n_in, n_row = len(rows) + len(fulls), len(out_rows)

    def kern(*refs):
        i = pl.program_id(0)
        res = fn(*[r[...] for r in refs[:n_in]])
        res = res if isinstance(res, (tuple, list)) else (res,)
        outs = refs[n_in:]
        for o, v in zip(outs[:n_row], res[:n_row]):
            o[...] = v.astype(o.dtype)
        for o, v in zip(outs[n_row:], res[n_row:]):
            @pl.when(i == 0)
            def _(o=o, v=v):
                o[...] = v

            @pl.when(i != 0)
            def _(o=o, v=v):
                o[...] += v

    def full_spec(shape):
        nd = len(shape)
        return pl.BlockSpec(tuple(shape), lambda i: (0,) * nd)

    in_specs = [pl.BlockSpec((tm, w), lambda i, k=k: (i, k)) for _, w, k in rows] + [full_spec(f.shape) for f in fulls]
    out_specs = [pl.BlockSpec((tm, c), lambda i: (i, 0)) for c, _ in out_rows] + [full_spec(s) for s in out_accs]
    out_shape = [jax.ShapeDtypeStruct((T, c), dt) for c, dt in out_rows] + [jax.ShapeDtypeStruct(tuple(s), F32) for s in out_accs]
    return pl.pallas_call(
        kern, name=name, grid=(T // tm,), in_specs=in_specs, out_specs=out_specs, out_shape=out_shape,
        compiler_params=_params(("arbitrary",)),
    )(*[r for r, _, _ in rows], *fulls)


def _colsum(v):
    return jnp.sum(v, axis=0, keepdims=True)


def _rowmean(v):
    return jnp.mean(v, axis=-1, keepdims=True)


def _rms_fwd(x, g):
    r = lax.rsqrt(_rowmean(x * x) + EPS)
    return x * r * g


def _rms_bwd(x, g, dy):
    r = lax.rsqrt(_rowmean(x * x) + EPS)
    xh = x * r
    dxh = dy * g
    return r * (dxh - xh * _rowmean(dxh * xh)), _colsum(dy * xh)


def _sigmoid(v):
    return 0.5 * jnp.tanh(0.5 * v) + 0.5


def _gelu(y):
    c = np.sqrt(2.0 / np.pi).astype(np.float32)
    return 0.5 * y * (1.0 + jnp.tanh(c * (y + 0.044715 * (y * y * y))))


def _gelu_grad(y):
    c = np.sqrt(2.0 / np.pi).astype(np.float32)
    th = jnp.tanh(c * (y + 0.044715 * (y * y * y)))
    return 0.5 * (1.0 + th) + 0.5 * y * (1.0 - th * th) * c * (1.0 + 3.0 * 0.044715 * (y * y))


def _split3(v):
    hi = v.astype(BF16)
    r1 = v - hi.astype(F32)
    mid = r1.astype(BF16)
    lo = (r1 - mid.astype(F32)).astype(BF16)
    return hi, mid, lo


def _band(tm, lo_off, hi_off):
    shape = (tm, tm + 2 * POOL_PAD)
    d = lax.broadcasted_iota(jnp.int32, shape, 1) - lax.broadcasted_iota(jnp.int32, shape, 0) - POOL_PAD
    return ((d >= lo_off) & (d < hi_off)).astype(BF16)


def _band_sum(band, ext):
    hi, mid, lo = _split3(ext)
    dot = functools.partial(jnp.dot, preferred_element_type=F32)
    return dot(band, hi) + dot(band, mid) + dot(band, lo)


def _window_count(r0, tm, half, T):
    t = r0 + lax.broadcasted_iota(jnp.int32, (tm, 1), 0)
    return (jnp.minimum(t + half, T) - jnp.maximum(t - half, 0)).astype(F32)


def _halo_specs(tm, C, T):
    per = tm // POOL_PAD
    last = T // POOL_PAD - 1
    return [pl.BlockSpec((POOL_PAD, C), lambda i: (jnp.maximum(i * per - 1, 0), 0)),
            pl.BlockSpec((tm, C), lambda i: (i, 0)),
            pl.BlockSpec((POOL_PAD, C), lambda i: (jnp.minimum((i + 1) * per, last), 0))]


def _with_halo(before_ref, main_ref, after_ref, i, n):
    before = jnp.where(i > 0, before_ref[...], 0.0)
    after = jnp.where(i < n - 1, after_ref[...], 0.0)
    return jnp.concatenate([before, main_ref[...], after], axis=0)


def _pool_block(ext, ctr, w_ref, r0, tm, T):
    pooled, conc = [], []
    for gi, w in enumerate(POOL_WINDOWS):
        half = w // 2
        cols = slice(gi * POOL_GROUP, (gi + 1) * POOL_GROUP)
        ws = _band_sum(_band(tm, -half, half), ext[:, cols])
        p = ws / _window_count(r0, tm, half, T) - ctr[:, cols]
        pooled.append(p)
        conc.append(jnp.dot(p.astype(BF16), w_ref[gi], preferred_element_type=F32))
    return pooled, jnp.concatenate(conc, axis=1)


def _pool_fwd(u, pool_w_bf, pool_scale, g_pool, tm):
    T = u.shape[0]
    nw = len(POOL_WINDOWS)
    C = nw * POOL_GROUP
    n = T // tm

    def kern(ub_ref, u_ref, ua_ref, w_ref, sc_ref, g_ref, o_ref):
        i = pl.program_id(0)
        r0 = pl.multiple_of(i * tm, tm)
        _, conc = _pool_block(_with_halo(ub_ref, u_ref, ua_ref, i, n), u_ref[...], w_ref, r0, tm, T)
        o_ref[...] = _rms_fwd(conc * sc_ref[...], g_ref[...]).astype(o_ref.dtype)

    return pl.pallas_call(
        kern, name="pool_fwd", grid=(n,),
        in_specs=_halo_specs(tm, C, T) + [pl.BlockSpec(pool_w_bf.shape, lambda i: (0, 0, 0)),
                                          pl.BlockSpec((1, C), lambda i: (0, 0)), pl.BlockSpec((1, C), lambda i: (0, 0))],
        out_specs=pl.BlockSpec((tm, C), lambda i: (i, 0)),
        out_shape=jax.ShapeDtypeStruct((T, C), BF16),
        compiler_params=_params(("arbitrary",)),
    )(u, u, u, pool_w_bf, pool_scale, g_pool)


def _pool_bwd_rows(u, d_y, pool_w_bf, pool_scale, g_pool, tm):
    T = u.shape[0]
    nw = len(POOL_WINDOWS)
    C = nw * POOL_GROUP
    n = T // tm

    def kern(ub_ref, u_ref, ua_ref, dy_ref, w_ref, sc_ref, g_ref, q_ref, dp_ref, dw_ref, dsc_ref, dg_ref):
        i = pl.program_id(0)
        r0 = pl.multiple_of(i * tm, tm)
        pooled, conc = _pool_block(_with_halo(ub_ref, u_ref, ua_ref, i, n), u_ref[...], w_ref, r0, tm, T)
        sc = sc_ref[...]
        dyp, dg = _rms_bwd(conc * sc, g_ref[...], dy_ref[...])
        dsc = _colsum(dyp * conc)
        dconc = (dyp * sc).astype(BF16)
        dws = []
        for gi, w in enumerate(POOL_WINDOWS):
            cols = slice(gi * POOL_GROUP, (gi + 1) * POOL_GROUP)
            dc = dconc[:, cols]
            dp = lax.dot_general(dc, w_ref[gi], (((1,), (1,)), ((), ())), preferred_element_type=F32)
            dp_ref[:, cols] = dp
            q_ref[:, cols] = dp / _window_count(r0, tm, w // 2, T)
            dws.append(lax.dot_general(pooled[gi].astype(BF16), dc, (((0,), (0,)), ((), ())), preferred_element_type=F32))

        @pl.when(i == 0)
        def _():
            for gi in range(nw):
                dw_ref[gi] = dws[gi]
            dsc_ref[...] = dsc
            dg_ref[...] = dg

        @pl.when(i != 0)
        def _():
            for gi in range(nw):
                dw_ref[gi] += dws[gi]
            dsc_ref[...] += dsc
            dg_ref[...] += dg

    full2 = pl.BlockSpec((1, C), lambda i: (0, 0))
    row_blk = pl.BlockSpec((tm, C), lambda i: (i, 0))
    return pl.pallas_call(
        kern, name="pool_bwd_rows", grid=(n,),
        in_specs=_halo_specs(tm, C, T) + [row_blk, pl.BlockSpec(pool_w_bf.shape, lambda i: (0, 0, 0)), full2, full2],
        out_specs=[row_blk, row_blk, pl.BlockSpec((nw, POOL_GROUP, POOL_GROUP), lambda i: (0, 0, 0)), full2, full2],
        out_shape=[jax.ShapeDtypeStruct((T, C), F32), jax.ShapeDtypeStruct((T, C), F32),
                   jax.ShapeDtypeStruct((nw, POOL_GROUP, POOL_GROUP), F32),
                   jax.ShapeDtypeStruct((1, C), F32), jax.ShapeDtypeStruct((1, C), F32)],
        compiler_params=_params(("arbitrary",)),
    )(u, u, u, d_y, pool_w_bf, pool_scale, g_pool)


def _pool_bwd_band(q, dpooled, tm):
    T, C = dpooled.shape
    n = T // tm

    def kern(qb_ref, q_ref, qa_ref, dp_ref, o_ref):
        ext = _with_halo(qb_ref, q_ref, qa_ref, pl.program_id(0), n)
        for gi, w in enumerate(POOL_WINDOWS):
            half = w // 2
            cols = slice(gi * POOL_GROUP, (gi + 1) * POOL_GROUP)
            o_ref[:, cols] = _band_sum(_band(tm, -half + 1, half + 1), ext[:, cols]) - dp_ref[:, cols]

    return pl.pallas_call(
        kern, name="pool_bwd_band", grid=(n,),
        in_specs=_halo_specs(tm, C, T) + [pl.BlockSpec((tm, C), lambda i: (i, 0))],
        out_specs=pl.BlockSpec((tm, C), lambda i: (i, 0)),
        out_shape=jax.ShapeDtypeStruct((T, C), F32),
        compiler_params=_params(("arbitrary",)),
    )(q, q, q, dpooled)


def _cmul(ar, ai, br, bi):
    return ar * br - ai * bi, ar * bi + ai * br


def _ssm_discretise(log_neg_a_re, a_im, log_dt):
    dt = jnp.exp(log_dt)
    a_re = -jnp.exp(log_neg_a_re)
    mag = jnp.exp(a_re * dt)
    ang = a_im * dt
    lam_re, lam_im = mag * jnp.cos(ang), mag * jnp.sin(ang)
    den = a_re * a_re + a_im * a_im
    f_re = ((lam_re - 1.0) * a_re + lam_im * a_im) / den
    f_im = (lam_im * a_re - (lam_re - 1.0) * a_im) / den
    return lam_re, lam_im, f_re, f_im


def _ssm_params_fwd(log_neg_a_re, a_im, log_dt):
    rows, n = log_neg_a_re.shape

    def kern(a_ref, b_ref, c_ref, o1, o2, o3, o4):
        for o, v in zip((o1, o2, o3, o4), _ssm_discretise(a_ref[...], b_ref[...], c_ref[...])):
            o[...] = v

    return pl.pallas_call(kern, name="ssm_params_fwd", out_shape=[jax.ShapeDtypeStruct((rows, n), F32)] * 4)(
        log_neg_a_re, a_im, log_dt)


def _ssm_params_bwd(log_neg_a_re, a_im, log_dt, cots):
    rows, n = log_neg_a_re.shape

    def kern(a_ref, b_ref, c_ref, g1, g2, g3, g4, o1, o2, o3):
        _, vjp = jax.vjp(_ssm_discretise, a_ref[...], b_ref[...], c_ref[...])
        d1, d2, d3 = vjp((g1[...], g2[...], g3[...], g4[...]))
        o1[...] = d1
        o2[...] = d2
        o3[...] = d3

    return pl.pallas_call(
        kern, name="ssm_params_bwd",
        out_shape=[jax.ShapeDtypeStruct((rows, n), F32), jax.ShapeDtypeStruct((rows, n), F32),
                   jax.ShapeDtypeStruct((rows, 1), F32)])(log_neg_a_re, a_im, log_dt, *cots)


def _cpow(lr, li, n):
    out = None
    br, bi = lr, li
    while n:
        if n & 1:
            out = (br, bi) if out is None else _cmul(out[0], out[1], br, bi)
        n >>= 1
        if n:
            br, bi = _cmul(br, bi, br, bi)
    return out


def _slab(t):
    return pl.ds(pl.multiple_of(t * N_CHUNK, N_CHUNK), N_CHUNK)


def _steps(n, step, carry):
    main = n // SCAN_UNROLL

    def body(i, c):
        for r in range(SCAN_UNROLL):
            c = step(i * SCAN_UNROLL + r, c)
        return c

    carry = lax.fori_loop(0, main, body, carry)
    for k in range(main * SCAN_UNROLL, n):
        carry = step(jnp.int32(k), carry)
    return carry


def _shift_chunks(vr, vi, reverse):
    sub = lax.broadcasted_iota(jnp.int32, vr.shape, 0)
    if reverse:
        keep = sub != N_CHUNK - 1
        return jnp.where(keep, pltpu.roll(vr, N_CHUNK - 1, 0), 0.0), jnp.where(keep, pltpu.roll(vi, N_CHUNK - 1, 0), 0.0)
    keep = sub != 0
    return jnp.where(keep, pltpu.roll(vr, 1, 0), 0.0), jnp.where(keep, pltpu.roll(vi, 1, 0), 0.0)


def _chunk_scan(xr, xi, lr, li, Lc, reverse, prev=None):
    C = xr.shape[1]
    lrb, lib = jnp.broadcast_to(lr, (N_CHUNK, C)), jnp.broadcast_to(li, (N_CHUNK, C))
    zero = jnp.zeros((N_CHUNK, C), F32)

    def step_of(k):
        return (Lc - 1 - k) if reverse else k

    def advance(sr, si, t):
        return lrb * sr - lib * si + xr[_slab(t), :], lrb * si + lib * sr + xi[_slab(t), :]

    def pass1(k, c):
        return advance(c[0], c[1], step_of(k))

    er, ei = _steps(Lc, pass1, (zero, zero))
    pr, pi = _cpow(lr, li, Lc)
    prb, pib = jnp.broadcast_to(pr, (N_CHUNK, C)), jnp.broadcast_to(pi, (N_CHUNK, C))
    cr, ci = zero, zero
    for _ in range(N_CHUNK - 1):
        cr, ci = _shift_chunks(er + prb * cr - pib * ci, ei + prb * ci + pib * cr, reverse)

    if prev is None:
        def pass2(k, c):
            t = step_of(k)
            nr, ni = advance(c[0], c[1], t)
            xr[_slab(t), :] = nr
            xi[_slab(t), :] = ni
            return nr, ni

        _steps(Lc, pass2, (cr, ci))
        return None

    qr, qi = prev
    def pair(ar, ai, wr, wi, acc):
        return acc[0] + ar * wr + ai * wi, acc[1] + ai * wr - ar * wi

    def pass2(k, c):
        sr, si, accr, acci = c
        t = step_of(k)
        nr, ni = advance(sr, si, t)
        xr[_slab(t), :] = nr
        xi[_slab(t), :] = ni
        tp = t - 1 if reverse else t + 1
        accr, acci = pair(nr, ni, qr[_slab(tp), :], qi[_slab(tp), :], (accr, acci))
        return nr, ni, accr, acci

    sr, si, accr, acci = _steps(Lc - 1, pass2, (cr, ci, zero, zero))
    t = step_of(Lc - 1)
    nr, ni = advance(sr, si, t)
    xr[_slab(t), :] = nr
    xi[_slab(t), :] = ni
    edge = step_of(0)
    wr, wi = _shift_chunks(qr[_slab(edge), :], qi[_slab(edge), :], not reverse)
    return pair(nr, ni, wr, wi, (accr, acci))


def _ssm_dims(T):
    assert T % ROW_BLOCK == 0
    return T // N_CHUNK, T // ROW_BLOCK


def _group_mask():
    shape = (SSM_BLOCK_GROUPS * SSM_GROUP, SSM_BLOCK_GROUPS * SSM_STATE)
    rows = lax.broadcasted_iota(jnp.int32, shape, 0) >> GROUP_SHIFT
    cols = lax.broadcasted_iota(jnp.int32, shape, 1) >> STATE_SHIFT
    return rows == cols


def _expand_groups(compact):
    return jnp.where(_group_mask(), jnp.tile(compact, (SSM_BLOCK_GROUPS, 1)), 0.0).astype(BF16)


def _collect_groups(block):
    kept = jnp.where(_group_mask(), block, 0.0)
    out = kept[0:SSM_GROUP]
    for g in range(1, SSM_BLOCK_GROUPS):
        out = out + kept[g * SSM_GROUP:(g + 1) * SSM_GROUP]
    return out


def _ssm_specs(T, direction):
    SC = SSM_BLOCK_GROUPS * SSM_STATE
    col = pl.BlockSpec((T, LANES), lambda m: (0, m))
    mat = pl.BlockSpec((None, None, SSM_GROUP, SC), lambda m: (direction, m, 0, 0))
    vec = pl.BlockSpec((None, None, 1, SC), lambda m: (direction, m, 0, 0))
    return col, mat, vec


def _ssm_project_in(u_ref, bre, bim, xr, xi, nblk):
    for blk in range(nblk):
        rows = pl.ds(blk * ROW_BLOCK, ROW_BLOCK)
        ub = u_ref[rows, :]
        xr[rows, :] = jnp.dot(ub, bre, preferred_element_type=F32)
        xi[rows, :] = jnp.dot(ub, bim, preferred_element_type=F32)


def _ssm_fwd(name, u_bf, mats, vecs, direction):
    T, W = u_bf.shape
    SC = SSM_BLOCK_GROUPS * SSM_STATE
    nb = W // LANES
    Lc, nblk = _ssm_dims(T)
    reverse = bool(direction)

    def kern(u_ref, bre_ref, bim_ref, cre_ref, cim_ref, lr_ref, li_ref, fr_ref, fi_ref, y_ref, xr, xi):
        _ssm_project_in(u_ref, _expand_groups(bre_ref[...]), _expand_groups(bim_ref[...]), xr, xi, nblk)
        _chunk_scan(xr, xi, lr_ref[...], li_ref[...], Lc, reverse)
        fr, fi = fr_ref[...], fi_ref[...]
        cre, cim = _expand_groups(cre_ref[...]), _expand_groups(cim_ref[...])
        for blk in range(nblk):
            rows = pl.ds(blk * ROW_BLOCK, ROW_BLOCK)
            sr, si = _cmul(fr, fi, xr[rows, :], xi[rows, :])
            y_ref[rows, :] = (lax.dot_general(sr.astype(BF16), cre, _NT, preferred_element_type=F32)
                              - lax.dot_general(si.astype(BF16), cim, _NT, preferred_element_type=F32))

    col, mat, vec = _ssm_specs(T, direction)
    return pl.pallas_call(
        kern, name=name, grid=(nb,),
        in_specs=[col, mat, mat, mat, mat, vec, vec, vec, vec],
        out_specs=col, out_shape=jax.ShapeDtypeStruct((T, W), F32),
        scratch_shapes=[pltpu.VMEM((T, SC), F32), pltpu.VMEM((T, SC), F32)],
        compiler_params=_params(("arbitrary",)),
    )(u_bf, *mats, *vecs)


def _ssm_bwd(name, u_bf, dy_bf, du_in, mats, vecs, direction):
    T, W = u_bf.shape
    SC = SSM_BLOCK_GROUPS * SSM_STATE
    nb = W // LANES
    Lc, nblk = _ssm_dims(T)
    reverse = bool(direction)

    def kern(u_ref, dy_ref, dui_ref, bre_ref, bim_ref, cre_ref, cim_ref, lr_ref, li_ref, fr_ref, fi_ref,
             du_ref, dbre_ref, dbim_ref, dcre_ref, dcim_ref, dlr_ref, dli_ref, dfr_ref, dfi_ref,
             xr, xi, gr, gi):
        lr, li, fr, fi = lr_ref[...], li_ref[...], fr_ref[...], fi_ref[...]
        bre, bim = _expand_groups(bre_ref[...]), _expand_groups(bim_ref[...])
        cre, cim = _expand_groups(cre_ref[...]), _expand_groups(cim_ref[...])
        _ssm_project_in(u_ref, bre, bim, xr, xi, nblk)
        _chunk_scan(xr, xi, lr, li, Lc, reverse)
        mre = jnp.zeros((LANES, SC), F32)
        mim = jnp.zeros((LANES, SC), F32)
        dfr = jnp.zeros((1, SC), F32)
        dfi = jnp.zeros((1, SC), F32)
        for blk in range(nblk):
            rows = pl.ds(blk * ROW_BLOCK, ROW_BLOCK)
            dy = dy_ref[rows, :]
            x_r, x_i = xr[rows, :], xi[rows, :]
            g_r = jnp.dot(dy, cre, preferred_element_type=F32)
            g_i = -jnp.dot(dy, cim, preferred_element_type=F32)
            mre += lax.dot_general(dy, x_r.astype(BF16), _TN, preferred_element_type=F32)
            mim += lax.dot_general(dy, x_i.astype(BF16), _TN, preferred_element_type=F32)
            dfr += _colsum(g_r * x_r + g_i * x_i)
            dfi += _colsum(g_i * x_r - g_r * x_i)
            gr[rows, :] = fr * g_r + fi * g_i
            gi[rows, :] = fr * g_i - fi * g_r
        dcre_ref[...] = _collect_groups(fr * mre - fi * mim)
        dcim_ref[...] = _collect_groups(-(fr * mim + fi * mre))
        dfr_ref[...] = dfr
        dfi_ref[...] = dfi
        accr, acci = _chunk_scan(gr, gi, lr, -li, Lc, not reverse, prev=(xr, xi))
        dlr_ref[...] = _colsum(accr)
        dli_ref[...] = _colsum(acci)
        dbre = jnp.zeros((LANES, SC), F32)
        dbim = jnp.zeros((LANES, SC), F32)
        for blk in range(nblk):
            rows = pl.ds(blk * ROW_BLOCK, ROW_BLOCK)
            ub = u_ref[rows, :]
            a_r, a_i = gr[rows, :].astype(BF16), gi[rows, :].astype(BF16)
            dbre += lax.dot_general(ub, a_r, _TN, preferred_element_type=F32)
            dbim += lax.dot_general(ub, a_i, _TN, preferred_element_type=F32)
            du_ref[rows, :] = (dui_ref[rows, :] + lax.dot_general(a_r, bre, _NT, preferred_element_type=F32)
                               + lax.dot_general(a_i, bim, _NT, preferred_element_type=F32))
        dbre_ref[...] = _collect_groups(dbre)
        dbim_ref[...] = _collect_groups(dbim)

    col, mat, vec = _ssm_specs(T, direction)
    mat_out = pl.BlockSpec((None, SSM_GROUP, SC), lambda m: (m, 0, 0))
    vec_out = pl.BlockSpec((None, 1, SC), lambda m: (m, 0, 0))
    mat_shape = jax.ShapeDtypeStruct((nb, SSM_GROUP, SC), F32)
    vec_shape = jax.ShapeDtypeStruct((nb, 1, SC), F32)
    return pl.pallas_call(
        kern, name=name, grid=(nb,),
        in_specs=[col, col, col, mat, mat, mat, mat, vec, vec, vec, vec],
        out_specs=[col, mat_out, mat_out, mat_out, mat_out, vec_out, vec_out, vec_out, vec_out],
        out_shape=[jax.ShapeDtypeStruct((T, W), F32), mat_shape, mat_shape, mat_shape, mat_shape,
                   vec_shape, vec_shape, vec_shape, vec_shape],
        scratch_shapes=[pltpu.VMEM((T, SC), F32)] * 4,
        compiler_params=_params(("arbitrary",)),
    )(u_bf, dy_bf, du_in, *mats, *vecs)


def _to_chunk_rows(a):
    T, C = a.shape
    return a.reshape(N_CHUNK, T // N_CHUNK, C).transpose(1, 0, 2).reshape(T, C)


def _from_chunk_rows(a):
    T, C = a.shape
    return a.reshape(T // N_CHUNK, N_CHUNK, C).transpose(1, 0, 2).reshape(T, C)


def _groups_side_by_side(p, channel_axis):
    d, G = p.shape[:2]
    nb = G // SSM_BLOCK_GROUPS
    v = p.reshape(d, nb, SSM_BLOCK_GROUPS, p.shape[2], p.shape[3])
    v = v.transpose(0, 1, 4, 2, 3) if channel_axis == 3 else v.transpose(0, 1, 3, 2, 4)
    return v.reshape(d, nb, SSM_GROUP, SSM_BLOCK_GROUPS * SSM_STATE)


def _groups_apart(m, channel_axis):
    d, nb = m.shape[:2]
    v = m.reshape(d, nb, SSM_GROUP, SSM_BLOCK_GROUPS, SSM_STATE)
    v = v.transpose(0, 1, 3, 4, 2) if channel_axis == 3 else v.transpose(0, 1, 3, 2, 4)
    return v.reshape((d, nb * SSM_BLOCK_GROUPS) + v.shape[3:])


def _conv_taps(ref, c, R, T):
    n = T // R
    r0 = pl.multiple_of(c * R, R)
    main = ref[pl.ds(r0, R), :]
    before = ref[pl.ds(pl.multiple_of(jnp.maximum(r0 - 8, 0), 8), 8), :]
    after = ref[pl.ds(pl.multiple_of(jnp.minimum(r0 + R, T - 8), 8), 8), :]
    ext = jnp.concatenate([jnp.where(c > 0, before, 0.0), main, jnp.where(c < n - 1, after, 0.0)], axis=0)
    m1 = pltpu.roll(ext, 1, 0)[8:8 + R]
    p1 = pltpu.roll(ext, R + 15, 0)[8:8 + R]
    return m1, main, p1, r0


def _conv_specs(T, F, cb):
    nj = F // cb
    lo = lambda rows: pl.BlockSpec((rows, cb), lambda j: (0, j))
    hi = lambda rows: pl.BlockSpec((rows, cb), lambda j: (0, j + nj))
    return nj, lo, hi


def _convact_fwd(up, conv_w, conv_b, F):
    T = up.shape[0]
    cb = _pick(F, (256, 128))
    R = _pick(T, (256, 128, 64))
    nj, lo, hi = _conv_specs(T, F, cb)

    def kern(uv, ug, wv, wg, bv, bg, o_ref):
        def body(c, carry):
            v1, v0, v2, r0 = _conv_taps(uv, c, R, T)
            g1, g0, g2, _ = _conv_taps(ug, c, R, T)
            val = v1 * wv[0:1, :] + v0 * wv[1:2, :] + v2 * wv[2:3, :] + bv[...]
            gate = g1 * wg[0:1, :] + g0 * wg[1:2, :] + g2 * wg[2:3, :] + bg[...]
            o_ref[pl.ds(r0, R), :] = (val * (gate * _sigmoid(gate))).astype(o_ref.dtype)
            return carry

        lax.fori_loop(0, T // R, body, 0)

    return pl.pallas_call(
        kern, name="convact_fwd", grid=(nj,),
        in_specs=[lo(T), hi(T), lo(3), hi(3), lo(1), hi(1)],
        out_specs=lo(T), out_shape=jax.ShapeDtypeStruct((T, F), BF16),
        compiler_params=_params(("arbitrary",)),
    )(up, up, conv_w, conv_w, conv_b, conv_b)


def _convact_bwd(up, dact, conv_w, conv_b, F):
    T = up.shape[0]
    cb = _pick(F, (256, 128))
    R = _pick(T, (256, 128, 64))
    nj, lo, hi = _conv_specs(T, F, cb)

    def kern(uv, ug, da, wv, wg, bv, bg, dupv, dupg, dwv, dwg, dbv, dbg, sv, sg):
        zero = jnp.zeros((1, cb), F32)

        def pass_a(c, acc):
            v1, v0, v2, r0 = _conv_taps(uv, c, R, T)
            g1, g0, g2, _ = _conv_taps(ug, c, R, T)
            val = v1 * wv[0:1, :] + v0 * wv[1:2, :] + v2 * wv[2:3, :] + bv[...]
            gate = g1 * wg[0:1, :] + g0 * wg[1:2, :] + g2 * wg[2:3, :] + bg[...]
            sig = _sigmoid(gate)
            d = da[pl.ds(r0, R), :]
            dval = d * (gate * sig)
            dgate = d * val * (sig * (1.0 + gate * (1.0 - sig)))
            sv[pl.ds(r0, R), :] = dval
            sg[pl.ds(r0, R), :] = dgate
            terms = (dval * v1, dval * v0, dval * v2, dval, dgate * g1, dgate * g0, dgate * g2, dgate)
            return tuple(a + _colsum(t) for a, t in zip(acc, terms))

        acc = lax.fori_loop(0, T // R, pass_a, (zero,) * 8)
        for k in range(3):
            dwv[k:k + 1, :] = acc[k]
            dwg[k:k + 1, :] = acc[4 + k]
        dbv[...] = acc[3]
        dbg[...] = acc[7]

        def pass_b(c, carry):
            v1, v0, v2, r0 = _conv_taps(sv, c, R, T)
            g1, g0, g2, _ = _conv_taps(sg, c, R, T)
            dupv[pl.ds(r0, R), :] = (v2 * wv[0:1, :] + v0 * wv[1:2, :] + v1 * wv[2:3, :]).astype(dupv.dtype)
            dupg[pl.ds(r0, R), :] = (g2 * wg[0:1, :] + g0 * wg[1:2, :] + g1 * wg[2:3, :]).astype(dupg.dtype)
            return carry

        lax.fori_loop(0, T // R, pass_b, 0)

    dupv, dupg, dwv, dwg, dbv, dbg = pl.pallas_call(
        kern, name="convact_bwd", grid=(nj,),
        in_specs=[lo(T), hi(T), lo(T), lo(3), hi(3), lo(1), hi(1)],
        out_specs=[lo(T), lo(T), lo(3), lo(3), lo(1), lo(1)],
        out_shape=[jax.ShapeDtypeStruct((T, F), BF16), jax.ShapeDtypeStruct((T, F), BF16),
                   jax.ShapeDtypeStruct((3, F), F32), jax.ShapeDtypeStruct((3, F), F32),
                   jax.ShapeDtypeStruct((1, F), F32), jax.ShapeDtypeStruct((1, F), F32)],
        scratch_shapes=[pltpu.VMEM((T, cb), F32), pltpu.VMEM((T, cb), F32)],
        compiler_params=_params(("arbitrary",)),
    )(up, up, dact, conv_w, conv_w, conv_b, conv_b)
    return (dupv, dupg), (dwv, dwg), jnp.concatenate([dbv, dbg], axis=1)


def _peers(x, y, c, with_self=False):
    out = []
    for k in range(0 if with_self else 1, N_DEV):
        px = 1 - x if k & 4 else x
        py = 1 - y if k & 2 else y
        pc = 1 - c if k & 1 else c
        out.append((k, (px, py, pc), 4 * px + 2 * py + pc))
    return out


def _exchange_start(name, bufs, modes, after):
    n = len(bufs)
    ncopy = n * N_DEV
    lands = [lax.empty((N_DEV,) + tuple(b.shape[-2:]), b.dtype) for b in bufs]

    def body(*refs):
        ins, land_in = refs[:n], refs[n:2 * n]
        send_sems, recv_sems = refs[2 * n + 1], refs[2 * n + 2]
        token = refs[-1]
        x, y, c = lax.axis_index("x"), lax.axis_index("y"), lax.axis_index("c")
        me = 4 * x + 2 * y + c
        for b in range(n):
            for k, peer, slot in _peers(x, y, c, with_self=True):
                sem = b * N_DEV + k
                pltpu.make_async_remote_copy(
                    src_ref=ins[b] if modes[b] == 'gather' else ins[b].at[slot], dst_ref=land_in[b].at[me],
                    send_sem=send_sems.at[sem], recv_sem=recv_sems.at[sem],
                    device_id=peer, device_id_type=MESH_ID).start()
        token[...] = jnp.zeros_like(token)

    hbm = pl.BlockSpec(memory_space=pltpu.HBM)
    sem_spec = pl.BlockSpec(memory_space=pltpu.SEMAPHORE)
    operands = [pltpu.with_memory_space_constraint(a, pltpu.HBM) for a in list(bufs) + lands]
    res = pl.pallas_call(
        body, name=name,
        out_shape=(pltpu.SemaphoreType.DMA((ncopy,)), pltpu.SemaphoreType.DMA((ncopy,)),
                   *[pltpu.HBM(a.shape, a.dtype) for a in operands], jax.ShapeDtypeStruct((8, LANES), F32)),
        in_specs=[hbm] * (2 * n) + [pl.BlockSpec(memory_space=pl.ANY)],
        out_specs=(sem_spec, sem_spec, *([hbm] * (2 * n)), pl.BlockSpec(memory_space=pltpu.VMEM)),
        input_output_aliases={i: 2 + i for i in range(2 * n)},
        compiler_params=pltpu.CompilerParams(has_side_effects=pltpu.SideEffectType.DATAFLOW_SIDE_EFFECTING),
    )(*operands, after)
    return (res[0], res[1], list(res[2:2 + n]), list(res[2 + n:2 + 2 * n]), tuple(modes)), res[-1]


def _exchange_wait(name, handle, after):
    send_sems, recv_sems, srcs, lands, modes = handle
    n = len(srcs)

    def body(*refs):
        src_in, land_in = refs[:n], refs[n:2 * n]
        send_ref, recv_ref = refs[2 * n], refs[2 * n + 1]
        x, y, c = lax.axis_index("x"), lax.axis_index("y"), lax.axis_index("c")
        for b in range(n):
            for k, peer, slot in _peers(x, y, c, with_self=True):
                sem = b * N_DEV + k
                copy = pltpu.make_async_remote_copy(
                    src_ref=src_in[b] if modes[b] == 'gather' else src_in[b].at[slot], dst_ref=land_in[b].at[slot],
                    send_sem=send_ref.at[sem], recv_sem=recv_ref.at[sem],
                    device_id=peer, device_id_type=MESH_ID)
                copy.wait_send()
                copy.wait_recv()

    hbm = pl.BlockSpec(memory_space=pltpu.HBM)
    sem_spec = pl.BlockSpec(memory_space=pltpu.SEMAPHORE)
    res = pl.pallas_call(
        body, name=name,
        out_shape=tuple(pltpu.HBM(a.shape, a.dtype) for a in srcs + lands),
        in_specs=[hbm] * (2 * n) + [sem_spec, sem_spec, pl.BlockSpec(memory_space=pl.ANY)],
        out_specs=tuple([hbm] * (2 * n)),
        input_output_aliases={i: i for i in range(2 * n)},
        compiler_params=pltpu.CompilerParams(has_side_effects=pltpu.SideEffectType.DATAFLOW_SIDE_EFFECTING),
    )(*srcs, *lands, send_sems, recv_sems, after)
    return list(res[n:])


def _adamw(w, g, m, v):
    m2 = ADAM_B1 * m + (1.0 - ADAM_B1) * g
    v2 = ADAM_B2 * v + (1.0 - ADAM_B2) * (g * g)
    m_hat = m2 / (1.0 - ADAM_B1 ** ADAM_STEP)
    v_hat = v2 / (1.0 - ADAM_B2 ** ADAM_STEP)
    return -ADAM_LR * (m_hat / (jnp.sqrt(v_hat) + ADAM_EPS) + ADAM_WD * w), m2, v2


def _sum_adamw(name, landed, w, m, v):
    R, C = w.shape
    row_bytes = N_DEV * (-(-C // LANES) * LANES) * landed.dtype.itemsize
    tiles = [d for d in range(16, R, 16) if R % d == 0 and d * row_bytes <= ADAM_TILE_BYTES]
    tr = max(tiles) if tiles and R * row_bytes > ADAM_TILE_BYTES else R

    def kern(x_ref, w_ref, m_ref, v_ref, g_out, d_out, m_out, v_out):
        g = x_ref[0].astype(F32)
        for q in range(1, N_DEV):
            g = g + x_ref[q].astype(F32)
        g_out[...] = g
        d_out[...], m_out[...], v_out[...] = _adamw(w_ref[...], g, m_ref[...], v_ref[...])

    blk = pl.BlockSpec((tr, C), lambda i: (i, 0))
    return pl.pallas_call(
        kern, name=name, grid=(R // tr,),
        in_specs=[pl.BlockSpec((N_DEV, tr, C), lambda i: (0, i, 0)), blk, blk, blk],
        out_specs=[blk] * 4, out_shape=[jax.ShapeDtypeStruct((R, C), F32)] * 4,
        compiler_params=_params(("arbitrary",)),
    )(landed, w, m, v)


def _sum_slots(name, landed):
    def kern(x_ref, o_ref):
        g = x_ref[0].astype(F32)
        for q in range(1, N_DEV):
            g = g + x_ref[q].astype(F32)
        o_ref[...] = g

    return pl.pallas_call(kern, name=name, out_shape=jax.ShapeDtypeStruct(landed.shape[1:], F32))(landed)


def _adamw_whole(name, g, w, m, v):
    def kern(g_ref, w_ref, m_ref, v_ref, d_out, m_out, v_out):
        d_out[...], m_out[...], v_out[...] = _adamw(w_ref[...], g_ref[...], m_ref[...], v_ref[...])

    return pl.pallas_call(kern, name=name, out_shape=[jax.ShapeDtypeStruct(w.shape, F32)] * 3)(g, w, m, v)


def _view2d(a):
    if a.ndim == 1:
        return a.reshape(1, -1)
    return a.reshape(-1, a.shape[-1])


def kernel(x, norm_mix_g, w_in, pool_w, pool_scale, ssm_log_neg_a_re, ssm_a_im, ssm_log_dt, ssm_b_re, ssm_b_im, ssm_c_re, ssm_c_im, ssm_d, glu_w, glu_b, out_norm_pool_g, out_norm_ssm_g, w_out, norm_ffn_g, w_up, conv_w, conv_b, w_down, final_norm_g, loss_target, m_norm_mix_g, m_w_in, m_pool_w, m_pool_scale, m_ssm_log_neg_a_re, m_ssm_a_im, m_ssm_log_dt, m_ssm_b_re, m_ssm_b_im, m_ssm_c_re, m_ssm_c_im, m_ssm_d, m_glu_w, m_glu_b, m_out_norm_pool_g, m_out_norm_ssm_g, m_w_out, m_norm_ffn_g, m_w_up, m_conv_w, m_conv_b, m_w_down, m_final_norm_g, v_norm_mix_g, v_w_in, v_pool_w, v_pool_scale, v_ssm_log_neg_a_re, v_ssm_a_im, v_ssm_log_dt, v_ssm_b_re, v_ssm_b_im, v_ssm_c_re, v_ssm_c_im, v_ssm_d, v_glu_w, v_glu_b, v_out_norm_pool_g, v_out_norm_ssm_g, v_w_out, v_norm_ffn_g, v_w_up, v_conv_w, v_conv_b, v_w_down, v_final_norm_g):
    given = dict(locals())
    weights = {n: given[n] for n in WEIGHTS}
    mom1 = {n: given["m_" + n] for n in WEIGHTS}
    mom2 = {n: given["v_" + n] for n in WEIGHTS}

    xs = x[0]
    tgt = loss_target[0]
    T, D = xs.shape
    DP = len(POOL_WINDOWS) * POOL_GROUP
    DS = D - DP
    G = DS // SSM_GROUP
    N, H = SSM_STATE, SSM_GROUP
    F2 = w_up.shape[1] * N_DEV
    F = F2 // 2
    fs = w_up.shape[1]
    row = lambda a: a.reshape(1, -1)

    early_handle, token = _exchange_start("gather_early_start", [w_in.astype(BF16), glu_w.astype(BF16)],
                                          ['gather'] * 2, after=w_in)
    late_src = [w_out.astype(BF16), w_down.astype(BF16), w_up.T.astype(BF16), conv_w]
    late_handle, token = _exchange_start("gather_late_start", late_src, ['gather'] * 4, after=token)
    CB = row(conv_b)

    g1, g2, g3 = row(norm_mix_g), row(norm_ffn_g), row(final_norm_g)
    gp, gs = row(out_norm_pool_g), row(out_norm_ssm_g)
    (xn,) = _rowmap("norm_mix", _rms_fwd, [xs], [g1 + token[0:1, 0:1]], [(D, BF16)])
    early = _exchange_wait("gather_early_wait", early_handle, xn)
    W_in = early[0].reshape(D, D)
    W_glu = early[1].reshape(DS, DS)
    u = _mm("proj_in", xn, W_in, 'nn', F32)
    tmp = _pick(T, (256, 128))
    pool_w_bf = pool_w.astype(BF16)
    ypn = _pool_fwd(u, pool_w_bf, row(pool_scale), gp, tmp)

    u_ssm = _to_chunk_rows(u[:, DP:])
    u_ssm_bf = u_ssm.astype(BF16)
    a_rows = (ssm_log_neg_a_re.reshape(2 * G, N), ssm_a_im.reshape(2 * G, N), ssm_log_dt.reshape(2 * G, 1))
    disc = _ssm_params_fwd(*a_rows)
    nb = G // SSM_BLOCK_GROUPS
    vecs = [d.reshape(2, nb, 1, SSM_BLOCK_GROUPS * N) for d in disc]
    mats = [_groups_side_by_side(ssm_b_re, 3), _groups_side_by_side(ssm_b_im, 3),
            _groups_side_by_side(ssm_c_re, 2), _groups_side_by_side(ssm_c_im, 2)]
    y_dir = [_ssm_fwd("ssm_fwd_%d" % d, u_ssm_bf, mats, vecs, d) for d in range(2)]

    def mix_post(yf, yb, us, d, gw, gb, g):
        y = yf + yb + d * us
        z = _gelu(y)
        gate = _sigmoid(jnp.dot(z.astype(BF16), gw, preferred_element_type=F32) + gb)
        return _rms_fwd(z * gate, g), y

    ysn, y_ssm = _rowmap("ssm_post", mix_post, [y_dir[0], y_dir[1], u_ssm], [row(ssm_d), W_glu, row(glu_b), gs],
                         [(DS, BF16), (DS, F32)])
    ycat = jnp.concatenate([ypn, _from_chunk_rows(ysn)], axis=1)
    late = _exchange_wait("gather_late_wait", late_handle, ysn)
    W_out = late[0].reshape(D, D)
    W_down = late[1].reshape(F, D)
    W_up_t = late[2].reshape(F2, D)
    CW = late[3].transpose(1, 0, 2).reshape(3, F2)
    h1 = _mm("proj_out", ycat, W_out, 'nn', F32, add=xs)
    (hn,) = _rowmap("norm_ffn", _rms_fwd, [h1], [g2], [(D, BF16)])
    up = _mm("ffn_up", hn, W_up_t, 'nt', F32)
    act = _convact_fwd(up, CW, CB, F)
    h2 = _mm("ffn_down", act, W_down, 'nn', F32, add=h1)

    def head(h, t, g):
        r = lax.rsqrt(_rowmean(h * h) + EPS)
        hh = h * r
        e = hh * g - t
        loss = 0.5 * jnp.sum(_rowmean(e * e), keepdims=True)
        dy = e * (1.0 / D)
        dxh = dy * g
        dh = r * (dxh - hh * _rowmean(dxh * hh))
        return dh, dh, jnp.broadcast_to(loss, (1, LANES)), _colsum(dy * hh)

    dh2, dh2_bf, loss_acc, dg3 = _rowmap("loss_head", head, [h2, tgt], [g3], [(D, F32), (D, BF16)], [(1, LANES), (1, D)])

    dact = _mm("ffn_down_dx", dh2_bf, W_down, 'nt', F32)
    dW_down = _mm("ffn_down_dw", act, dh2_bf, 'tn', BF16)
    dup, dCW, dCB = _convact_bwd(up, dact, CW, CB, F)
    dhn = _mm("ffn_up_dx_val", dup[0], W_up_t, 'nn', F32, b_rows=(2, 0))
    dhn = _mm("ffn_up_dx_gate", dup[1], W_up_t, 'nn', F32, b_rows=(2, 1), add=dhn)
    dW_up_t = jnp.concatenate([_mm("ffn_up_dw_val", dup[0], hn, 'tn', BF16), _mm("ffn_up_dw_gate", dup[1], hn, 'tn', BF16)], axis=0)

    def shards(full_grad):
        return full_grad.reshape((N_DEV, full_grad.shape[0] // N_DEV) + full_grad.shape[1:])

    half = N_DEV // 2
    dCW_sh = jnp.concatenate([h.reshape(3, half, fs).transpose(1, 0, 2) for h in dCW], axis=0)
    ffn_sent = [shards(dW_down), shards(dW_up_t), dCW_sh]
    ffn_handle, token = _exchange_start("scatter_ffn_start", ffn_sent, ['scatter'] * 3, after=dhn)
    g2_late = g2 + token[0:1, 0:1]

    def norm_bwd_add(hx, dy, res, g):
        dx, dg = _rms_bwd(hx, g, dy)
        return dx + res, dg

    def norm_bwd_add2(hx, dy, res, g):
        dx, dg = _rms_bwd(hx, g, dy)
        return dx + res, dx + res, dg

    dh1, dh1_bf, dg2 = _rowmap("norm_ffn_bwd", norm_bwd_add2, [h1, dhn, dh2], [g2_late], [(D, F32), (D, BF16)], [(1, D)])
    dycat = _mm("proj_out_dx", dh1_bf, W_out, 'nt', F32)
    dW_out = _mm("proj_out_dw", ycat, dh1_bf, 'tn', BF16)

    out_sent = [shards(dW_out)]
    out_handle, token = _exchange_start("scatter_out_start", out_sent, ['scatter'], after=dycat)
    q, dpooled, dpool_w, dpool_scale, dgp = _pool_bwd_rows(u, dycat, pool_w_bf, row(pool_scale),
                                                           gp + token[0:1, 0:1], tmp)
    du_pool = _pool_bwd_band(q, dpooled, tmp)

    def mix_post_bwd(y, us, dyn, d, gw, gb, g):
        z = _gelu(y)
        gz = z.astype(BF16)
        sig = _sigmoid(jnp.dot(gz, gw, preferred_element_type=F32) + gb)
        dys, dg = _rms_bwd(z * sig, g, dyn)
        dgl = dys * z * sig * (1.0 - sig)
        dgl_bf = dgl.astype(BF16)
        dz = dys * sig + lax.dot_general(dgl_bf, gw, (((1,), (1,)), ((), ())), preferred_element_type=F32)
        dgw = lax.dot_general(gz, dgl_bf, (((0,), (0,)), ((), ())), preferred_element_type=F32)
        dyv = dz * _gelu_grad(y)
        return dyv, dyv * d, dgw, _colsum(dgl), _colsum(dyv * us), dg

    dyv, du_dir, dglu_w, dglu_b, dssm_d, dgs = _rowmap(
        "ssm_post_bwd", mix_post_bwd, [y_ssm, u_ssm, _to_chunk_rows(dycat[:, DP:])], [row(ssm_d), W_glu, row(glu_b), gs],
        [(DS, BF16), (DS, F32)], [(DS, DS), (1, DS), (1, DS), (1, DS)])

    du_ssm = du_dir
    dB, dC, dvec = [], [], []
    for d in range(2):
        res = _ssm_bwd("ssm_bwd_%d" % d, u_ssm_bf, dyv, du_ssm, mats, vecs, d)
        du_ssm = res[0]
        dB.append(res[1:3])
        dC.append(res[3:5])
        dvec.append(res[5:9])
    grad_b = [_groups_apart(jnp.stack([dB[0][k], dB[1][k]]), 3) for k in range(2)]
    grad_c = [_groups_apart(jnp.stack([dC[0][k], dC[1][k]]), 2) for k in range(2)]
    cots = [jnp.stack([dvec[0][k], dvec[1][k]]).reshape(2 * G, N) for k in range(4)]
    d_a_re, d_a_im, d_log_dt = _ssm_params_bwd(*a_rows, cots)

    rep_grads = {
        'pool_w': dpool_w, 'pool_scale': dpool_scale, 'ssm_log_neg_a_re': d_a_re, 'ssm_a_im': d_a_im,
        'ssm_log_dt': d_log_dt, 'ssm_b_re': grad_b[0], 'ssm_b_im': grad_b[1], 'ssm_c_re': grad_c[0], 'ssm_c_im': grad_c[1],
        'ssm_d': dssm_d, 'glu_b': dglu_b, 'out_norm_pool_g': dgp, 'out_norm_ssm_g': dgs, 'norm_ffn_g': dg2,
        'conv_b': dCB, 'final_norm_g': dg3}
    wide = ('pool_w', 'ssm_b_re', 'ssm_b_im', 'ssm_c_re', 'ssm_c_im')

    def travel(n):
        g = rep_grads[n]
        if n in wide:
            return g.reshape(-1, PACK_W if g.size % PACK_W == 0 else LANES).astype(BF16)
        return _view2d(g.reshape(weights[n].shape))

    small = tuple(n for n in REPLICATED if n != 'norm_mix_g')
    small_sent = [shards(dglu_w.astype(BF16))] + [travel(n) for n in small] + [loss_acc]
    small_handle, token = _exchange_start("reduce_small_start", small_sent, ['scatter'] + ['gather'] * (len(small) + 1),
                                          after=d_a_re)

    du = (jnp.concatenate([du_pool, _from_chunk_rows(du_ssm)], axis=1) + token[0:1, 0:1]).astype(BF16)
    dW_in = _mm("proj_in_dw", xn, du, 'tn', BF16)
    in_handle, token = _exchange_start("scatter_in_start", [shards(dW_in)], ['scatter'], after=dW_in)
    dxn = _mm("proj_in_dx", du, W_in + token[0, 0].astype(BF16), 'nt', F32)
    dx, dg1 = _rowmap("norm_mix_bwd", norm_bwd_add, [xs, dxn, dh1], [g1], [(D, F32)], [(1, D)])
    rep_grads['norm_mix_g'] = dg1
    last_handle, token = _exchange_start("reduce_last_start", [travel('norm_mix_g')], ['gather'], after=dx)

    grads, delta, new_m, new_v = {}, {}, {}, {}

    def update(n, landed_n):
        shape = weights[n].shape
        if len(shape) > 2 and shape[-1] < LANES // 2:
            grads[n] = _sum_slots("sum_" + n, landed_n).reshape(shape)
            delta[n], new_m[n], new_v[n] = _adamw_whole("adamw_" + n, grads[n], weights[n], mom1[n], mom2[n])
            return
        if n == 'w_up':
            res = _sum_adamw("adamw_" + n, landed_n, weights[n].T, mom1[n].T, mom2[n].T)
            grads[n], delta[n], new_m[n], new_v[n] = [r.T for r in res]
            return
        landed_n = landed_n.reshape((N_DEV,) + _view2d(weights[n]).shape)
        res = _sum_adamw("adamw_" + n, landed_n, _view2d(weights[n]), _view2d(mom1[n]), _view2d(mom2[n]))
        grads[n], delta[n], new_m[n], new_v[n] = [r.reshape(shape) for r in res]

    ffn_landed = _exchange_wait("scatter_ffn_wait", ffn_handle, token)
    for n, l in zip(('w_down', 'w_up', 'conv_w'), ffn_landed):
        update(n, l)
    (out_landed,) = _exchange_wait("scatter_out_wait", out_handle, token)
    update('w_out', out_landed)
    small_landed = _exchange_wait("reduce_small_wait", small_handle, token)
    for n, l in zip(('glu_w',) + small, small_landed):
        update(n, l)

    def loss_sum(parts_ref, o_ref):
        s = parts_ref[0]
        for q in range(1, N_DEV):
            s = s + parts_ref[q]
        o_ref[...] = s

    loss = pl.pallas_call(loss_sum, name="loss_sum", out_shape=jax.ShapeDtypeStruct((1, LANES), F32))(small_landed[-1])[0, 0]

    update('w_in', _exchange_wait("scatter_in_wait", in_handle, grads[small[-1]])[0])
    update('norm_mix_g', _exchange_wait("reduce_last_wait", last_handle, grads['w_in'])[0])

    return (loss, dx[None], *[grads[n] for n in WEIGHTS], *[delta[n] for n in WEIGHTS],
            *[new_m[n] for n in WEIGHTS], *[new_v[n] for n in WEIGHTS])
```

```python
import functools

import jax
import jax.numpy as jnp
import numpy as np
from jax import lax
from jax.experimental import pallas as pl
from jax.experimental.pallas import tpu as pltpu

F32 = jnp.float32
BF16 = jnp.bfloat16
MESH_ID = pl.DeviceIdType.MESH

N_DEV = 8
EPS = 1e-6
POOL_WINDOWS = (2, 4, 8, 16)
POOL_GROUP = 128
POOL_PAD = 64
SSM_GROUP = 16
SSM_STATE = 64
SSM_BLOCK_GROUPS = 8
N_CHUNK = 8
ROW_BLOCK = 512
SCAN_UNROLL = 4
GROUP_SHIFT = 4
STATE_SHIFT = 6
_NT = (((1,), (1,)), ((), ()))
_TN = (((0,), (0,)), ((), ()))
LANES = 128
PACK_W = 1024
VMEM_LIMIT = 56 * 1024 * 1024
MM_VMEM_BUDGET = 40 * 1024 * 1024
MM_TILE_CAP = 1408
ADAM_TILE_BYTES = 4 * 1024 * 1024
ADAM_TINY = 8192

ADAM_LR = 0.001
ADAM_B1 = 0.9
ADAM_B2 = 0.999
ADAM_EPS = 1e-08
ADAM_WD = 0.01
ADAM_STEP = 10

WEIGHTS = ['norm_mix_g', 'w_in', 'pool_w', 'pool_scale', 'ssm_log_neg_a_re', 'ssm_a_im', 'ssm_log_dt',
           'ssm_b_re', 'ssm_b_im', 'ssm_c_re', 'ssm_c_im', 'ssm_d', 'glu_w', 'glu_b', 'out_norm_pool_g',
           'out_norm_ssm_g', 'w_out', 'norm_ffn_g', 'w_up', 'conv_w', 'conv_b', 'w_down', 'final_norm_g']
SHARDED = ('w_in', 'w_out', 'w_down', 'glu_w', 'w_up', 'conv_w')
REPLICATED = tuple(n for n in WEIGHTS if n not in SHARDED)


def _pick(n, prefs):
    for p in prefs:
        if n % p == 0:
            return p
    return n


def _params(sem, vmem=None):
    return pltpu.CompilerParams(dimension_semantics=sem, vmem_limit_bytes=vmem or VMEM_LIMIT)


def _tiles(n, cap):
    return [d for d in range(LANES, min(n, cap) + 1, LANES) if n % d == 0] or [n]


def _mm(name, a, b, mode, out_dtype, add=None, b_rows=(1, 0), a_plane=None):
    batch = a.shape[0] if a.ndim == 3 and a_plane is None else None
    if mode == 'nn':
        (M, K), N = a.shape[-2:], b.shape[1]
        assert b.shape[0] == K * b_rows[0]
    elif mode == 'nt':
        (M, K), (N, _) = a.shape[-2:], b.shape
    else:
        (K, M), (_, N) = a.shape[-2:], b.shape
    dims = {'nn': (((1,), (0,)), ((), ())), 'nt': (((1,), (1,)), ((), ())), 'tn': (((0,), (0,)), ((), ()))}[mode]
    sa, sb, so = a.dtype.itemsize, b.dtype.itemsize, jnp.dtype(out_dtype).itemsize
    best = None
    for tm in _tiles(M, MM_TILE_CAP):
        for tn in _tiles(N, MM_TILE_CAP):
            need = 2 * (tm * K * sa + tn * K * sb + tm * tn * (so + (4 if add is not None else 0)))
            if need <= MM_VMEM_BUDGET:
                key = (tm * tn / (tm + tn), tm * tn)
                if best is None or key > best[0]:
                    best = (key, tm, tn)
    _, tm, tn = best
    rows_inner = a.size * sa * (N // tn) + b.size * sb < a.size * sa + b.size * sb * (M // tm)

    def kern(*refs):
        a_ref, b_ref = refs[:2]
        o_ref = refs[-1]
        r = lax.dot_general(a_ref[...].astype(BF16), b_ref[...].astype(BF16), dims, preferred_element_type=F32)
        if add is not None:
            r = r + refs[2][...]
        o_ref[...] = r.astype(o_ref.dtype)

    if rows_inner:
        grid = (N // tn, M // tm)
        ij = lambda g0, g1: (g1, g0)
    else:
        grid = (M // tm, N // tn)
        ij = lambda g0, g1: (g0, g1)

    def spec(block, index, batched=False):
        if batch is None:
            return pl.BlockSpec(block, lambda g0, g1: index(*ij(g0, g1)))
        if batched:
            return pl.BlockSpec((None,) + block, lambda p, g0, g1: (p,) + index(*ij(g0, g1)))
        return pl.BlockSpec(block, lambda p, g0, g1: index(*ij(g0, g1)))

    if a_plane is not None:
        assert mode != 'tn'
        a_spec = pl.BlockSpec((None, tm, K), lambda g0, g1: (a_plane, ij(g0, g1)[0], 0))
    else:
        a_spec = (spec((K, tm), lambda i, j: (0, i), True) if mode == 'tn' else spec((tm, K), lambda i, j: (i, 0), True))
    b_spec = (spec((tn, K), lambda i, j: (j, 0)) if mode == 'nt' else spec((K, tn), lambda i, j: (b_rows[1], j)))
    in_specs = [a_spec, b_spec]
    args = [a, b]
    if add is not None:
        in_specs.append(spec((tm, tn), lambda i, j: (i, j), True))
        args.append(add)
    lead = () if batch is None else (batch,)
    return pl.pallas_call(
        kern, name=name, grid=lead + grid, in_specs=in_specs,
        out_specs=spec((tm, tn), lambda i, j: (i, j), True),
        out_shape=jax.ShapeDtypeStruct(lead + (M, N), out_dtype),
        compiler_params=_params(("parallel",) * (len(lead) + 2)),
    )(*args)


def _rowmap(name, fn, rows, fulls, out_rows, out_accs=(), tm=512):
    rows = [r if isinstance(r, tuple) else (r, r.shape[1], 0) for r in rows]
    T = rows[0][0].shape[0]
    tm = min(tm, T)
    assert T % tm == 0
    n_in, n_row = len(rows) + len(fulls), len(out_rows)

    def kern(*refs):
        i = pl.program_id(0)
        res = fn(*[r[...] for r in refs[:n_in]])
        res = res if isinstance(res, (tuple, list)) else (res,)
        outs = refs[n_in:]
        for o, v in zip(outs[:n_row], res[:n_row]):
            o[...] = v.astype(o.dtype)
        for o, v in zip(outs[n_row:], res[n_row:]):
            @pl.when(i == 0)
            def _(o=o, v=v):
                o[...] = v

            @pl.when(i != 0)
            def _(o=o, v=v):
                o[...] += v

    def full_spec(shape):
        nd = len(shape)
        return pl.BlockSpec(tuple(shape), lambda i: (0,) * nd)

    in_specs = [pl.BlockSpec((tm, w), lambda i, k=k: (i, k)) for _, w, k in rows] + [full_spec(f.shape) for f in fulls]
    out_specs = [pl.BlockSpec((tm, c), lambda i: (i, 0)) for c, _ in out_rows] + [full_spec(s) for s in out_accs]
    out_shape = [jax.ShapeDtypeStruct((T, c), dt) for c, dt in out_rows] + [jax.ShapeDtypeStruct(tuple(s), F32) for s in out_accs]
    return pl.pallas_call(
        kern, name=name, grid=(T // tm,), in_specs=in_specs, out_specs=out_specs, out_shape=out_shape,
        compiler_params=_params(("arbitrary",)),
    )(*[r for r, _, _ in rows], *fulls)


def _colsum(v):
    return jnp.sum(v, axis=0, keepdims=True)


def _rowmean(v):
    return jnp.mean(v, axis=-1, keepdims=True)


def _rms_fwd(x, g):
    r = lax.rsqrt(_rowmean(x * x) + EPS)
    return x * r * g


def _rms_bwd(x, g, dy):
    r = lax.rsqrt(_rowmean(x * x) + EPS)
    xh = x * r
    dxh = dy * g
    return r * (dxh - xh * _rowmean(dxh * xh)), _colsum(dy * xh)


def _sigmoid(v):
    return 0.5 * jnp.tanh(0.5 * v) + 0.5


def _gelu(y):
    c = np.sqrt(2.0 / np.pi).astype(np.float32)
    return 0.5 * y * (1.0 + jnp.tanh(c * (y + 0.044715 * (y * y * y))))


def _gelu_grad(y):
    c = np.sqrt(2.0 / np.pi).astype(np.float32)
    th = jnp.tanh(c * (y + 0.044715 * (y * y * y)))
    return 0.5 * (1.0 + th) + 0.5 * y * (1.0 - th * th) * c * (1.0 + 3.0 * 0.044715 * (y * y))


def _split3(v):
    hi = v.astype(BF16)
    r1 = v - hi.astype(F32)
    mid = r1.astype(BF16)
    lo = (r1 - mid.astype(F32)).astype(BF16)
    return hi, mid, lo


def _band(tm, lo_off, hi_off):
    shape = (tm, tm + 2 * POOL_PAD)
    d = lax.broadcasted_iota(jnp.int32, shape, 1) - lax.broadcasted_iota(jnp.int32, shape, 0) - POOL_PAD
    return ((d >= lo_off) & (d < hi_off)).astype(BF16)


def _band_sum(band, ext):
    hi, mid, lo = _split3(ext)
    dot = functools.partial(jnp.dot, preferred_element_type=F32)
    return dot(band, hi) + dot(band, mid) + dot(band, lo)


def _window_count(r0, tm, half, T):
    t = r0 + lax.broadcasted_iota(jnp.int32, (tm, 1), 0)
    return (jnp.minimum(t + half, T) - jnp.maximum(t - half, 0)).astype(F32)


def _halo_specs(tm, C, T):
    per = tm // POOL_PAD
    last = T // POOL_PAD - 1
    return [pl.BlockSpec((POOL_PAD, C), lambda i: (jnp.maximum(i * per - 1, 0), 0)),
            pl.BlockSpec((tm, C), lambda i: (i, 0)),
            pl.BlockSpec((POOL_PAD, C), lambda i: (jnp.minimum((i + 1) * per, last), 0))]


def _with_halo(before_ref, main_ref, after_ref, i, n):
    before = jnp.where(i > 0, before_ref[...], 0.0)
    after = jnp.where(i < n - 1, after_ref[...], 0.0)
    return jnp.concatenate([before, main_ref[...], after], axis=0)


def _pool_block(ext, ctr, w_ref, r0, tm, T):
    pooled, conc = [], []
    for gi, w in enumerate(POOL_WINDOWS):
        half = w // 2
        cols = slice(gi * POOL_GROUP, (gi + 1) * POOL_GROUP)
        ws = _band_sum(_band(tm, -half, half), ext[:, cols])
        p = ws / _window_count(r0, tm, half, T) - ctr[:, cols]
        pooled.append(p)
        conc.append(jnp.dot(p.astype(BF16), w_ref[gi], preferred_element_type=F32))
    return pooled, jnp.concatenate(conc, axis=1)


def _pool_fwd(u, pool_w_bf, pool_scale, g_pool, tm):
    T = u.shape[0]
    nw = len(POOL_WINDOWS)
    C = nw * POOL_GROUP
    n = T // tm

    def kern(ub_ref, u_ref, ua_ref, w_ref, sc_ref, g_ref, o_ref):
        i = pl.program_id(0)
        r0 = pl.multiple_of(i * tm, tm)
        _, conc = _pool_block(_with_halo(ub_ref, u_ref, ua_ref, i, n), u_ref[...], w_ref, r0, tm, T)
        o_ref[...] = _rms_fwd(conc * sc_ref[...], g_ref[...]).astype(o_ref.dtype)

    return pl.pallas_call(
        kern, name="pool_fwd", grid=(n,),
        in_specs=_halo_specs(tm, C, T) + [pl.BlockSpec(pool_w_bf.shape, lambda i: (0, 0, 0)),
                                          pl.BlockSpec((1, C), lambda i: (0, 0)), pl.BlockSpec((1, C), lambda i: (0, 0))],
        out_specs=pl.BlockSpec((tm, C), lambda i: (i, 0)),
        out_shape=jax.ShapeDtypeStruct((T, C), BF16),
        compiler_params=_params(("arbitrary",)),
    )(u, u, u, pool_w_bf, pool_scale, g_pool)


def _pool_bwd_rows(u, d_y, pool_w_bf, pool_scale, g_pool, tm):
    T = u.shape[0]
    nw = len(POOL_WINDOWS)
    C = nw * POOL_GROUP
    n = T // tm

    def kern(ub_ref, u_ref, ua_ref, dy_ref, w_ref, sc_ref, g_ref, q_ref, dp_ref, dw_ref, dsc_ref, dg_ref):
        i = pl.program_id(0)
        r0 = pl.multiple_of(i * tm, tm)
        pooled, conc = _pool_block(_with_halo(ub_ref, u_ref, ua_ref, i, n), u_ref[...], w_ref, r0, tm, T)
        sc = sc_ref[...]
        dyp, dg = _rms_bwd(conc * sc, g_ref[...], dy_ref[...])
        dsc = _colsum(dyp * conc)
        dconc = (dyp * sc).astype(BF16)
        dws = []
        for gi, w in enumerate(POOL_WINDOWS):
            cols = slice(gi * POOL_GROUP, (gi + 1) * POOL_GROUP)
            dc = dconc[:, cols]
            dp = lax.dot_general(dc, w_ref[gi], (((1,), (1,)), ((), ())), preferred_element_type=F32)
            dp_ref[:, cols] = dp
            q_ref[:, cols] = dp / _window_count(r0, tm, w // 2, T)
            dws.append(lax.dot_general(pooled[gi].astype(BF16), dc, (((0,), (0,)), ((), ())), preferred_element_type=F32))

        @pl.when(i == 0)
        def _():
            for gi in range(nw):
                dw_ref[gi] = dws[gi]
            dsc_ref[...] = dsc
            dg_ref[...] = dg

        @pl.when(i != 0)
        def _():
            for gi in range(nw):
                dw_ref[gi] += dws[gi]
            dsc_ref[...] += dsc
            dg_ref[...] += dg

    full2 = pl.BlockSpec((1, C), lambda i: (0, 0))
    row_blk = pl.BlockSpec((tm, C), lambda i: (i, 0))
    return pl.pallas_call(
        kern, name="pool_bwd_rows", grid=(n,),
        in_specs=_halo_specs(tm, C, T) + [row_blk, pl.BlockSpec(pool_w_bf.shape, lambda i: (0, 0, 0)), full2, full2],
        out_specs=[row_blk, row_blk, pl.BlockSpec((nw, POOL_GROUP, POOL_GROUP), lambda i: (0, 0, 0)), full2, full2],
        out_shape=[jax.ShapeDtypeStruct((T, C), F32), jax.ShapeDtypeStruct((T, C), F32),
                   jax.ShapeDtypeStruct((nw, POOL_GROUP, POOL_GROUP), F32),
                   jax.ShapeDtypeStruct((1, C), F32), jax.ShapeDtypeStruct((1, C), F32)],
        compiler_params=_params(("arbitrary",)),
    )(u, u, u, d_y, pool_w_bf, pool_scale, g_pool)


def _pool_bwd_band(q, dpooled, tm):
    T, C = dpooled.shape
    n = T // tm

    def kern(qb_ref, q_ref, qa_ref, dp_ref, o_ref):
        ext = _with_halo(qb_ref, q_ref, qa_ref, pl.program_id(0), n)
        for gi, w in enumerate(POOL_WINDOWS):
            half = w // 2
            cols = slice(gi * POOL_GROUP, (gi + 1) * POOL_GROUP)
            o_ref[:, cols] = _band_sum(_band(tm, -half + 1, half + 1), ext[:, cols]) - dp_ref[:, cols]

    return pl.pallas_call(
        kern, name="pool_bwd_band", grid=(n,),
        in_specs=_halo_specs(tm, C, T) + [pl.BlockSpec((tm, C), lambda i: (i, 0))],
        out_specs=pl.BlockSpec((tm, C), lambda i: (i, 0)),
        out_shape=jax.ShapeDtypeStruct((T, C), F32),
        compiler_params=_params(("arbitrary",)),
    )(q, q, q, dpooled)


def _cmul(ar, ai, br, bi):
    return ar * br - ai * bi, ar * bi + ai * br


def _ssm_discretise(log_neg_a_re, a_im, log_dt):
    dt = jnp.exp(log_dt)
    a_re = -jnp.exp(log_neg_a_re)
    mag = jnp.exp(a_re * dt)
    ang = a_im * dt
    lam_re, lam_im = mag * jnp.cos(ang), mag * jnp.sin(ang)
    den = a_re * a_re + a_im * a_im
    f_re = ((lam_re - 1.0) * a_re + lam_im * a_im) / den
    f_im = (lam_im * a_re - (lam_re - 1.0) * a_im) / den
    return lam_re, lam_im, f_re, f_im


def _ssm_params_fwd(log_neg_a_re, a_im, log_dt):
    rows, n = log_neg_a_re.shape

    def kern(a_ref, b_ref, c_ref, o1, o2, o3, o4):
        for o, v in zip((o1, o2, o3, o4), _ssm_discretise(a_ref[...], b_ref[...], c_ref[...])):
            o[...] = v

    return pl.pallas_call(kern, name="ssm_params_fwd", out_shape=[jax.ShapeDtypeStruct((rows, n), F32)] * 4)(
        log_neg_a_re, a_im, log_dt)


def _ssm_params_bwd(log_neg_a_re, a_im, log_dt, cots):
    rows, n = log_neg_a_re.shape

    def kern(a_ref, b_ref, c_ref, g1, g2, g3, g4, o1, o2, o3):
        _, vjp = jax.vjp(_ssm_discretise, a_ref[...], b_ref[...], c_ref[...])
        d1, d2, d3 = vjp((g1[...], g2[...], g3[...], g4[...]))
        o1[...] = d1
        o2[...] = d2
        o3[...] = d3

    return pl.pallas_call(
        kern, name="ssm_params_bwd",
        out_shape=[jax.ShapeDtypeStruct((rows, n), F32), jax.ShapeDtypeStruct((rows, n), F32),
                   jax.ShapeDtypeStruct((rows, 1), F32)])(log_neg_a_re, a_im, log_dt, *cots)


def _cpow(lr, li, n):
    out = None
    br, bi = lr, li
    while n:
        if n & 1:
            out = (br, bi) if out is None else _cmul(out[0], out[1], br, bi)
        n >>= 1
        if n:
            br, bi = _cmul(br, bi, br, bi)
    return out


def _slab(t):
    return pl.ds(pl.multiple_of(t * N_CHUNK, N_CHUNK), N_CHUNK)


def _steps(n, step, carry):
    main = n // SCAN_UNROLL

    def body(i, c):
        for r in range(SCAN_UNROLL):
            c = step(i * SCAN_UNROLL + r, c)
        return c

    carry = lax.fori_loop(0, main, body, carry)
    for k in range(main * SCAN_UNROLL, n):
        carry = step(jnp.int32(k), carry)
    return carry


def _shift_chunks(vr, vi, reverse):
    sub = lax.broadcasted_iota(jnp.int32, vr.shape, 0)
    if reverse:
        keep = sub != N_CHUNK - 1
        return jnp.where(keep, pltpu.roll(vr, N_CHUNK - 1, 0), 0.0), jnp.where(keep, pltpu.roll(vi, N_CHUNK - 1, 0), 0.0)
    keep = sub != 0
    return jnp.where(keep, pltpu.roll(vr, 1, 0), 0.0), jnp.where(keep, pltpu.roll(vi, 1, 0), 0.0)


def _chunk_scan(xr, xi, lr, li, Lc, reverse, prev=None):
    C = xr.shape[1]
    lrb, lib = jnp.broadcast_to(lr, (N_CHUNK, C)), jnp.broadcast_to(li, (N_CHUNK, C))
    zero = jnp.zeros((N_CHUNK, C), F32)

    def step_of(k):
        return (Lc - 1 - k) if reverse else k

    def advance(sr, si, t):
        return lrb * sr - lib * si + xr[_slab(t), :], lrb * si + lib * sr + xi[_slab(t), :]

    def pass1(k, c):
        return advance(c[0], c[1], step_of(k))

    er, ei = _steps(Lc, pass1, (zero, zero))
    pr, pi = _cpow(lr, li, Lc)
    prb, pib = jnp.broadcast_to(pr, (N_CHUNK, C)), jnp.broadcast_to(pi, (N_CHUNK, C))
    cr, ci = zero, zero
    for _ in range(N_CHUNK - 1):
        cr, ci = _shift_chunks(er + prb * cr - pib * ci, ei + prb * ci + pib * cr, reverse)

    if prev is None:
        def pass2(k, c):
            t = step_of(k)
            nr, ni = advance(c[0], c[1], t)
            xr[_slab(t), :] = nr
            xi[_slab(t), :] = ni
            return nr, ni

        _steps(Lc, pass2, (cr, ci))
        return None

    qr, qi = prev
    def pair(ar, ai, wr, wi, acc):
        return acc[0] + ar * wr + ai * wi, acc[1] + ai * wr - ar * wi

    def pass2(k, c):
        sr, si, accr, acci = c
        t = step_of(k)
        nr, ni = advance(sr, si, t)
        xr[_slab(t), :] = nr
        xi[_slab(t), :] = ni
        tp = t - 1 if reverse else t + 1
        accr, acci = pair(nr, ni, qr[_slab(tp), :], qi[_slab(tp), :], (accr, acci))
        return nr, ni, accr, acci

    sr, si, accr, acci = _steps(Lc - 1, pass2, (cr, ci, zero, zero))
    t = step_of(Lc - 1)
    nr, ni = advance(sr, si, t)
    xr[_slab(t), :] = nr
    xi[_slab(t), :] = ni
    edge = step_of(0)
    wr, wi = _shift_chunks(qr[_slab(edge), :], qi[_slab(edge), :], not reverse)
    return pair(nr, ni, wr, wi, (accr, acci))


def _ssm_dims(T):
    assert T % ROW_BLOCK == 0
    return T // N_CHUNK, T // ROW_BLOCK


def _group_mask():
    shape = (SSM_BLOCK_GROUPS * SSM_GROUP, SSM_BLOCK_GROUPS * SSM_STATE)
    rows = lax.broadcasted_iota(jnp.int32, shape, 0) >> GROUP_SHIFT
    cols = lax.broadcasted_iota(jnp.int32, shape, 1) >> STATE_SHIFT
    return rows == cols


def _expand_groups(compact):
    return jnp.where(_group_mask(), jnp.tile(compact, (SSM_BLOCK_GROUPS, 1)), 0.0).astype(BF16)


def _collect_groups(block):
    kept = jnp.where(_group_mask(), block, 0.0)
    out = kept[0:SSM_GROUP]
    for g in range(1, SSM_BLOCK_GROUPS):
        out = out + kept[g * SSM_GROUP:(g + 1) * SSM_GROUP]
    return out


def _ssm_specs(T, direction):
    SC = SSM_BLOCK_GROUPS * SSM_STATE
    col = pl.BlockSpec((T, LANES), lambda m: (0, m))
    mat = pl.BlockSpec((None, None, SSM_GROUP, SC), lambda m: (direction, m, 0, 0))
    vec = pl.BlockSpec((None, None, 1, SC), lambda m: (direction, m, 0, 0))
    return col, mat, vec


def _ssm_project_in(u_ref, bre, bim, xr, xi, nblk):
    for blk in range(nblk):
        rows = pl.ds(blk * ROW_BLOCK, ROW_BLOCK)
        ub = u_ref[rows, :]
        xr[rows, :] = jnp.dot(ub, bre, preferred_element_type=F32)
        xi[rows, :] = jnp.dot(ub, bim, preferred_element_type=F32)


def _ssm_fwd(name, u_bf, mats, vecs, direction):
    T, W = u_bf.shape
    SC = SSM_BLOCK_GROUPS * SSM_STATE
    nb = W // LANES
    Lc, nblk = _ssm_dims(T)
    reverse = bool(direction)

    def kern(u_ref, bre_ref, bim_ref, cre_ref, cim_ref, lr_ref, li_ref, fr_ref, fi_ref, y_ref, xr, xi):
        _ssm_project_in(u_ref, _expand_groups(bre_ref[...]), _expand_groups(bim_ref[...]), xr, xi, nblk)
        _chunk_scan(xr, xi, lr_ref[...], li_ref[...], Lc, reverse)
        fr, fi = fr_ref[...], fi_ref[...]
        cre, cim = _expand_groups(cre_ref[...]), _expand_groups(cim_ref[...])
        for blk in range(nblk):
            rows = pl.ds(blk * ROW_BLOCK, ROW_BLOCK)
            sr, si = _cmul(fr, fi, xr[rows, :], xi[rows, :])
            y_ref[rows, :] = (lax.dot_general(sr.astype(BF16), cre, _NT, preferred_element_type=F32)
                              - lax.dot_general(si.astype(BF16), cim, _NT, preferred_element_type=F32))

    col, mat, vec = _ssm_specs(T, direction)
    return pl.pallas_call(
        kern, name=name, grid=(nb,),
        in_specs=[col, mat, mat, mat, mat, vec, vec, vec, vec],
        out_specs=col, out_shape=jax.ShapeDtypeStruct((T, W), F32),
        scratch_shapes=[pltpu.VMEM((T, SC), F32), pltpu.VMEM((T, SC), F32)],
        compiler_params=_params(("arbitrary",)),
    )(u_bf, *mats, *vecs)


def _ssm_bwd(name, u_bf, dy_bf, du_in, mats, vecs, direction):
    T, W = u_bf.shape
    SC = SSM_BLOCK_GROUPS * SSM_STATE
    nb = W // LANES
    Lc, nblk = _ssm_dims(T)
    reverse = bool(direction)

    def kern(u_ref, dy_ref, dui_ref, bre_ref, bim_ref, cre_ref, cim_ref, lr_ref, li_ref, fr_ref, fi_ref,
             du_ref, dbre_ref, dbim_ref, dcre_ref, dcim_ref, dlr_ref, dli_ref, dfr_ref, dfi_ref,
             xr, xi, gr, gi):
        lr, li, fr, fi = lr_ref[...], li_ref[...], fr_ref[...], fi_ref[...]
        bre, bim = _expand_groups(bre_ref[...]), _expand_groups(bim_ref[...])
        cre, cim = _expand_groups(cre_ref[...]), _expand_groups(cim_ref[...])
        _ssm_project_in(u_ref, bre, bim, xr, xi, nblk)
        _chunk_scan(xr, xi, lr, li, Lc, reverse)
        mre = jnp.zeros((LANES, SC), F32)
        mim = jnp.zeros((LANES, SC), F32)
        dfr = jnp.zeros((1, SC), F32)
        dfi = jnp.zeros((1, SC), F32)
        for blk in range(nblk):
            rows = pl.ds(blk * ROW_BLOCK, ROW_BLOCK)
            dy = dy_ref[rows, :]
            x_r, x_i = xr[rows, :], xi[rows, :]
            g_r = jnp.dot(dy, cre, preferred_element_type=F32)
            g_i = -jnp.dot(dy, cim, preferred_element_type=F32)
            mre += lax.dot_general(dy, x_r.astype(BF16), _TN, preferred_element_type=F32)
            mim += lax.dot_general(dy, x_i.astype(BF16), _TN, preferred_element_type=F32)
            dfr += _colsum(g_r * x_r + g_i * x_i)
            dfi += _colsum(g_i * x_r - g_r * x_i)
            gr[rows, :] = fr * g_r + fi * g_i
            gi[rows, :] = fr * g_i - fi * g_r
        dcre_ref[...] = _collect_groups(fr * mre - fi * mim)
        dcim_ref[...] = _collect_groups(-(fr * mim + fi * mre))
        dfr_ref[...] = dfr
        dfi_ref[...] = dfi
        accr, acci = _chunk_scan(gr, gi, lr, -li, Lc, not reverse, prev=(xr, xi))
        dlr_ref[...] = _colsum(accr)
        dli_ref[...] = _colsum(acci)
        dbre = jnp.zeros((LANES, SC), F32)
        dbim = jnp.zeros((LANES, SC), F32)
        for blk in range(nblk):
            rows = pl.ds(blk * ROW_BLOCK, ROW_BLOCK)
            ub = u_ref[rows, :]
            a_r, a_i = gr[rows, :].astype(BF16), gi[rows, :].astype(BF16)
            dbre += lax.dot_general(ub, a_r, _TN, preferred_element_type=F32)
            dbim += lax.dot_general(ub, a_i, _TN, preferred_element_type=F32)
            du_ref[rows, :] = (dui_ref[rows, :] + lax.dot_general(a_r, bre, _NT, preferred_element_type=F32)
                               + lax.dot_general(a_i, bim, _NT, preferred_element_type=F32))
        dbre_ref[...] = _collect_groups(dbre)
        dbim_ref[...] = _collect_groups(dbim)

    col, mat, vec = _ssm_specs(T, direction)
    mat_out = pl.BlockSpec((None, SSM_GROUP, SC), lambda m: (m, 0, 0))
    vec_out = pl.BlockSpec((None, 1, SC), lambda m: (m, 0, 0))
    mat_shape = jax.ShapeDtypeStruct((nb, SSM_GROUP, SC), F32)
    vec_shape = jax.ShapeDtypeStruct((nb, 1, SC), F32)
    return pl.pallas_call(
        kern, name=name, grid=(nb,),
        in_specs=[col, col, col, mat, mat, mat, mat, vec, vec, vec, vec],
        out_specs=[col, mat_out, mat_out, mat_out, mat_out, vec_out, vec_out, vec_out, vec_out],
        out_shape=[jax.ShapeDtypeStruct((T, W), F32), mat_shape, mat_shape, mat_shape, mat_shape,
                   vec_shape, vec_shape, vec_shape, vec_shape],
        scratch_shapes=[pltpu.VMEM((T, SC), F32)] * 4,
        compiler_params=_params(("arbitrary",)),
    )(u_bf, dy_bf, du_in, *mats, *vecs)


def _to_chunk_rows(a):
    T, C = a.shape
    return a.reshape(N_CHUNK, T // N_CHUNK, C).transpose(1, 0, 2).reshape(T, C)


def _from_chunk_rows(a):
    T, C = a.shape
    return a.reshape(T // N_CHUNK, N_CHUNK, C).transpose(1, 0, 2).reshape(T, C)


def _groups_side_by_side(p, channel_axis):
    d, G = p.shape[:2]
    nb = G // SSM_BLOCK_GROUPS
    v = p.reshape(d, nb, SSM_BLOCK_GROUPS, p.shape[2], p.shape[3])
    v = v.transpose(0, 1, 4, 2, 3) if channel_axis == 3 else v.transpose(0, 1, 3, 2, 4)
    return v.reshape(d, nb, SSM_GROUP, SSM_BLOCK_GROUPS * SSM_STATE)


def _groups_apart(m, channel_axis):
    d, nb = m.shape[:2]
    v = m.reshape(d, nb, SSM_GROUP, SSM_BLOCK_GROUPS, SSM_STATE)
    v = v.transpose(0, 1, 3, 4, 2) if channel_axis == 3 else v.transpose(0, 1, 3, 2, 4)
    return v.reshape((d, nb * SSM_BLOCK_GROUPS) + v.shape[3:])


def _conv_taps(ref, c, R, T):
    n = T // R
    r0 = pl.multiple_of(c * R, R)
    main = ref[pl.ds(r0, R), :]
    before = ref[pl.ds(pl.multiple_of(jnp.maximum(r0 - 8, 0), 8), 8), :]
    after = ref[pl.ds(pl.multiple_of(jnp.minimum(r0 + R, T - 8), 8), 8), :]
    ext = jnp.concatenate([jnp.where(c > 0, before, 0.0), main, jnp.where(c < n - 1, after, 0.0)], axis=0)
    m1 = pltpu.roll(ext, 1, 0)[8:8 + R]
    p1 = pltpu.roll(ext, R + 15, 0)[8:8 + R]
    return m1, main, p1, r0


def _conv_specs(T, F, cb):
    nj = F // cb
    lo = lambda rows: pl.BlockSpec((rows, cb), lambda j: (0, j))
    hi = lambda rows: pl.BlockSpec((rows, cb), lambda j: (0, j + nj))
    return nj, lo, hi


def _convact_fwd(up, conv_w, conv_b, F):
    T = up.shape[0]
    cb = _pick(F, (256, 128))
    R = _pick(T, (256, 128, 64))
    nj, lo, hi = _conv_specs(T, F, cb)

    def kern(uv, ug, wv, wg, bv, bg, o_ref):
        def body(c, carry):
            v1, v0, v2, r0 = _conv_taps(uv, c, R, T)
            g1, g0, g2, _ = _conv_taps(ug, c, R, T)
            val = v1 * wv[0:1, :] + v0 * wv[1:2, :] + v2 * wv[2:3, :] + bv[...]
            gate = g1 * wg[0:1, :] + g0 * wg[1:2, :] + g2 * wg[2:3, :] + bg[...]
            o_ref[pl.ds(r0, R), :] = (val * (gate * _sigmoid(gate))).astype(o_ref.dtype)
            return carry

        lax.fori_loop(0, T // R, body, 0)

    return pl.pallas_call(
        kern, name="convact_fwd", grid=(nj,),
        in_specs=[lo(T), hi(T), lo(3), hi(3), lo(1), hi(1)],
        out_specs=lo(T), out_shape=jax.ShapeDtypeStruct((T, F), BF16),
        compiler_params=_params(("arbitrary",)),
    )(up, up, conv_w, conv_w, conv_b, conv_b)


def _convact_bwd(up, dact, conv_w, conv_b, F):
    T = up.shape[0]
    cb = _pick(F, (256, 128))
    R = _pick(T, (256, 128, 64))
    nj, lo, hi = _conv_specs(T, F, cb)

    def kern(uv, ug, da, wv, wg, bv, bg, dup, dwv, dwg, dbv, dbg, sv, sg):
        zero = jnp.zeros((1, cb), F32)

        def pass_a(c, acc):
            v1, v0, v2, r0 = _conv_taps(uv, c, R, T)
            g1, g0, g2, _ = _conv_taps(ug, c, R, T)
            val = v1 * wv[0:1, :] + v0 * wv[1:2, :] + v2 * wv[2:3, :] + bv[...]
            gate = g1 * wg[0:1, :] + g0 * wg[1:2, :] + g2 * wg[2:3, :] + bg[...]
            sig = _sigmoid(gate)
            d = da[pl.ds(r0, R), :]
            dval = d * (gate * sig)
            dgate = d * val * (sig * (1.0 + gate * (1.0 - sig)))
            sv[pl.ds(r0, R), :] = dval
            sg[pl.ds(r0, R), :] = dgate
            terms = (dval * v1, dval * v0, dval * v2, dval, dgate * g1, dgate * g0, dgate * g2, dgate)
            return tuple(a + _colsum(t) for a, t in zip(acc, terms))

        acc = lax.fori_loop(0, T // R, pass_a, (zero,) * 8)
        for k in range(3):
            dwv[k:k + 1, :] = acc[k]
            dwg[k:k + 1, :] = acc[4 + k]
        dbv[...] = acc[3]
        dbg[...] = acc[7]

        def pass_b(c, carry):
            v1, v0, v2, r0 = _conv_taps(sv, c, R, T)
            g1, g0, g2, _ = _conv_taps(sg, c, R, T)
            dup[0, pl.ds(r0, R), :] = (v2 * wv[0:1, :] + v0 * wv[1:2, :] + v1 * wv[2:3, :]).astype(dup.dtype)
            dup[1, pl.ds(r0, R), :] = (g2 * wg[0:1, :] + g0 * wg[1:2, :] + g1 * wg[2:3, :]).astype(dup.dtype)
            return carry

        lax.fori_loop(0, T // R, pass_b, 0)

    dup, dwv, dwg, dbv, dbg = pl.pallas_call(
        kern, name="convact_bwd", grid=(nj,),
        in_specs=[lo(T), hi(T), lo(T), lo(3), hi(3), lo(1), hi(1)],
        out_specs=[pl.BlockSpec((2, T, cb), lambda j: (0, 0, j)), lo(3), lo(3), lo(1), lo(1)],
        out_shape=[jax.ShapeDtypeStruct((2, T, F), BF16),
                   jax.ShapeDtypeStruct((3, F), F32), jax.ShapeDtypeStruct((3, F), F32),
                   jax.ShapeDtypeStruct((1, F), F32), jax.ShapeDtypeStruct((1, F), F32)],
        scratch_shapes=[pltpu.VMEM((T, cb), F32), pltpu.VMEM((T, cb), F32)],
        compiler_params=_params(("arbitrary",)),
    )(up, up, dact, conv_w, conv_w, conv_b, conv_b)
    return dup, (dwv, dwg), jnp.concatenate([dbv, dbg], axis=1)


def _peers(x, y, c, with_self=False):
    out = []
    for k in range(0 if with_self else 1, N_DEV):
        px = 1 - x if k & 4 else x
        py = 1 - y if k & 2 else y
        pc = 1 - c if k & 1 else c
        out.append((k, (px, py, pc), 4 * px + 2 * py + pc))
    return out


def _exchange_start(name, bufs, modes, after):
    n = len(bufs)
    ncopy = n * N_DEV
    lands = [lax.empty((N_DEV,) + tuple(b.shape[-2:]), b.dtype) for b in bufs]

    def body(*refs):
        ins, land_in = refs[:n], refs[n:2 * n]
        send_sems, recv_sems = refs[2 * n + 1], refs[2 * n + 2]
        token = refs[-1]
        x, y, c = lax.axis_index("x"), lax.axis_index("y"), lax.axis_index("c")
        me = 4 * x + 2 * y + c
        for b in range(n):
            for k, peer, slot in _peers(x, y, c, with_self=True):
                sem = b * N_DEV + k
                pltpu.make_async_remote_copy(
                    src_ref=ins[b] if modes[b] == 'gather' else ins[b].at[slot], dst_ref=land_in[b].at[me],
                    send_sem=send_sems.at[sem], recv_sem=recv_sems.at[sem],
                    device_id=peer, device_id_type=MESH_ID).start()
        token[...] = jnp.zeros_like(token)

    hbm = pl.BlockSpec(memory_space=pltpu.HBM)
    sem_spec = pl.BlockSpec(memory_space=pltpu.SEMAPHORE)
    operands = [pltpu.with_memory_space_constraint(a, pltpu.HBM) for a in list(bufs) + lands]
    res = pl.pallas_call(
        body, name=name,
        out_shape=(pltpu.SemaphoreType.DMA((ncopy,)), pltpu.SemaphoreType.DMA((ncopy,)),
                   *[pltpu.HBM(a.shape, a.dtype) for a in operands], jax.ShapeDtypeStruct((8, LANES), F32)),
        in_specs=[hbm] * (2 * n) + [pl.BlockSpec(memory_space=pl.ANY)],
        out_specs=(sem_spec, sem_spec, *([hbm] * (2 * n)), pl.BlockSpec(memory_space=pltpu.VMEM)),
        input_output_aliases={i: 2 + i for i in range(2 * n)},
        compiler_params=pltpu.CompilerParams(has_side_effects=pltpu.SideEffectType.DATAFLOW_SIDE_EFFECTING),
    )(*operands, after)
    return (res[0], res[1], list(res[2:2 + n]), list(res[2 + n:2 + 2 * n]), tuple(modes)), res[-1]


def _exchange_wait(name, handle, after):
    send_sems, recv_sems, srcs, lands, modes = handle
    n = len(srcs)

    def body(*refs):
        src_in, land_in = refs[:n], refs[n:2 * n]
        send_ref, recv_ref = refs[2 * n], refs[2 * n + 1]
        x, y, c = lax.axis_index("x"), lax.axis_index("y"), lax.axis_index("c")
        for b in range(n):
            for k, peer, slot in _peers(x, y, c, with_self=True):
                sem = b * N_DEV + k
                copy = pltpu.make_async_remote_copy(
                    src_ref=src_in[b] if modes[b] == 'gather' else src_in[b].at[slot], dst_ref=land_in[b].at[slot],
                    send_sem=send_ref.at[sem], recv_sem=recv_ref.at[sem],
                    device_id=peer, device_id_type=MESH_ID)
                copy.wait_send()
                copy.wait_recv()

    hbm = pl.BlockSpec(memory_space=pltpu.HBM)
    sem_spec = pl.BlockSpec(memory_space=pltpu.SEMAPHORE)
    res = pl.pallas_call(
        body, name=name,
        out_shape=tuple(pltpu.HBM(a.shape, a.dtype) for a in srcs + lands),
        in_specs=[hbm] * (2 * n) + [sem_spec, sem_spec, pl.BlockSpec(memory_space=pl.ANY)],
        out_specs=tuple([hbm] * (2 * n)),
        input_output_aliases={i: i for i in range(2 * n)},
        compiler_params=pltpu.CompilerParams(has_side_effects=pltpu.SideEffectType.DATAFLOW_SIDE_EFFECTING),
    )(*srcs, *lands, send_sems, recv_sems, after)
    return list(res[n:])


def _adamw(w, g, m, v):
    m2 = ADAM_B1 * m + (1.0 - ADAM_B1) * g
    v2 = ADAM_B2 * v + (1.0 - ADAM_B2) * (g * g)
    m_hat = m2 / (1.0 - ADAM_B1 ** ADAM_STEP)
    v_hat = v2 / (1.0 - ADAM_B2 ** ADAM_STEP)
    return -ADAM_LR * (m_hat / (jnp.sqrt(v_hat) + ADAM_EPS) + ADAM_WD * w), m2, v2


def _sum_adamw(name, landed, w, m, v):
    R, C = w.shape
    row_bytes = N_DEV * (-(-C // LANES) * LANES) * landed.dtype.itemsize
    tiles = [d for d in range(16, R, 16) if R % d == 0 and d * row_bytes <= ADAM_TILE_BYTES]
    tr = max(tiles) if tiles and R * row_bytes > ADAM_TILE_BYTES else R

    def kern(x_ref, w_ref, m_ref, v_ref, g_out, d_out, m_out, v_out):
        g = x_ref[0].astype(F32)
        for q in range(1, N_DEV):
            g = g + x_ref[q].astype(F32)
        g_out[...] = g
        d_out[...], m_out[...], v_out[...] = _adamw(w_ref[...], g, m_ref[...], v_ref[...])

    blk = pl.BlockSpec((tr, C), lambda i: (i, 0))
    return pl.pallas_call(
        kern, name=name, grid=(R // tr,),
        in_specs=[pl.BlockSpec((N_DEV, tr, C), lambda i: (0, i, 0)), blk, blk, blk],
        out_specs=[blk] * 4, out_shape=[jax.ShapeDtypeStruct((R, C), F32)] * 4,
        compiler_params=_params(("arbitrary",)),
    )(landed, w, m, v)


def _sum_adamw_many(name, items):
    n = len(items)

    def kern(*refs):
        ins, outs = refs[:4 * n], refs[4 * n:]
        for k in range(n):
            x_ref, w_ref, m_ref, v_ref = ins[4 * k:4 * k + 4]
            g = x_ref[0].astype(F32)
            for q in range(1, N_DEV):
                g = g + x_ref[q].astype(F32)
            outs[4 * k][...] = g
            outs[4 * k + 1][...], outs[4 * k + 2][...], outs[4 * k + 3][...] = _adamw(w_ref[...], g, m_ref[...], v_ref[...])

    out_shape = [jax.ShapeDtypeStruct(w.shape, F32) for _, w, _, _ in items for _ in range(4)]
    res = pl.pallas_call(kern, name=name, out_shape=out_shape)(*[a for item in items for a in item])
    return [res[4 * k:4 * k + 4] for k in range(n)]


def _sum_slots(name, landed):
    def kern(x_ref, o_ref):
        g = x_ref[0].astype(F32)
        for q in range(1, N_DEV):
            g = g + x_ref[q].astype(F32)
        o_ref[...] = g

    return pl.pallas_call(kern, name=name, out_shape=jax.ShapeDtypeStruct(landed.shape[1:], F32))(landed)


def _adamw_whole(name, g, w, m, v):
    def kern(g_ref, w_ref, m_ref, v_ref, d_out, m_out, v_out):
        d_out[...], m_out[...], v_out[...] = _adamw(w_ref[...], g_ref[...], m_ref[...], v_ref[...])

    return pl.pallas_call(kern, name=name, out_shape=[jax.ShapeDtypeStruct(w.shape, F32)] * 3)(g, w, m, v)


def _view2d(a):
    if a.ndim == 1:
        return a.reshape(1, -1)
    return a.reshape(-1, a.shape[-1])


def kernel(x, norm_mix_g, w_in, pool_w, pool_scale, ssm_log_neg_a_re, ssm_a_im, ssm_log_dt, ssm_b_re, ssm_b_im, ssm_c_re, ssm_c_im, ssm_d, glu_w, glu_b, out_norm_pool_g, out_norm_ssm_g, w_out, norm_ffn_g, w_up, conv_w, conv_b, w_down, final_norm_g, loss_target, m_norm_mix_g, m_w_in, m_pool_w, m_pool_scale, m_ssm_log_neg_a_re, m_ssm_a_im, m_ssm_log_dt, m_ssm_b_re, m_ssm_b_im, m_ssm_c_re, m_ssm_c_im, m_ssm_d, m_glu_w, m_glu_b, m_out_norm_pool_g, m_out_norm_ssm_g, m_w_out, m_norm_ffn_g, m_w_up, m_conv_w, m_conv_b, m_w_down, m_final_norm_g, v_norm_mix_g, v_w_in, v_pool_w, v_pool_scale, v_ssm_log_neg_a_re, v_ssm_a_im, v_ssm_log_dt, v_ssm_b_re, v_ssm_b_im, v_ssm_c_re, v_ssm_c_im, v_ssm_d, v_glu_w, v_glu_b, v_out_norm_pool_g, v_out_norm_ssm_g, v_w_out, v_norm_ffn_g, v_w_up, v_conv_w, v_conv_b, v_w_down, v_final_norm_g):
    given = dict(locals())
    weights = {n: given[n] for n in WEIGHTS}
    mom1 = {n: given["m_" + n] for n in WEIGHTS}
    mom2 = {n: given["v_" + n] for n in WEIGHTS}

    xs = x[0]
    tgt = loss_target[0]
    T, D = xs.shape
    DP = len(POOL_WINDOWS) * POOL_GROUP
    DS = D - DP
    G = DS // SSM_GROUP
    N, H = SSM_STATE, SSM_GROUP
    F2 = w_up.shape[1] * N_DEV
    F = F2 // 2
    fs = w_up.shape[1]
    row = lambda a: a.reshape(1, -1)

    early_handle, token = _exchange_start("gather_early_start", [w_in.astype(BF16), glu_w.astype(BF16)],
                                          ['gather'] * 2, after=w_in)
    late_src = [w_out.astype(BF16), w_down.astype(BF16), w_up.T.astype(BF16), conv_w]
    late_handle, token = _exchange_start("gather_late_start", late_src, ['gather'] * 4, after=token)
    CB = row(conv_b)

    g1, g2, g3 = row(norm_mix_g), row(norm_ffn_g), row(final_norm_g)
    gp, gs = row(out_norm_pool_g), row(out_norm_ssm_g)
    (xn,) = _rowmap("norm_mix", _rms_fwd, [xs], [g1 + token[0:1, 0:1]], [(D, BF16)])
    early = _exchange_wait("gather_early_wait", early_handle, xn)
    W_in = early[0].reshape(D, D)
    W_glu = early[1].reshape(DS, DS)
    u = _mm("proj_in", xn, W_in, 'nn', F32)
    tmp = _pick(T, (256, 128))
    pool_w_bf = pool_w.astype(BF16)
    ypn = _pool_fwd(u, pool_w_bf, row(pool_scale), gp, tmp)

    u_ssm = _to_chunk_rows(u[:, DP:])
    u_ssm_bf = u_ssm.astype(BF16)
    a_rows = (ssm_log_neg_a_re.reshape(2 * G, N), ssm_a_im.reshape(2 * G, N), ssm_log_dt.reshape(2 * G, 1))
    disc = _ssm_params_fwd(*a_rows)
    nb = G // SSM_BLOCK_GROUPS
    vecs = [d.reshape(2, nb, 1, SSM_BLOCK_GROUPS * N) for d in disc]
    mats = [_groups_side_by_side(ssm_b_re, 3), _groups_side_by_side(ssm_b_im, 3),
            _groups_side_by_side(ssm_c_re, 2), _groups_side_by_side(ssm_c_im, 2)]
    y_dir = [_ssm_fwd("ssm_fwd_%d" % d, u_ssm_bf, mats, vecs, d) for d in range(2)]

    def mix_post(yf, yb, us, d, gw, gb, g):
        y = yf + yb + d * us
        z = _gelu(y)
        gate = _sigmoid(jnp.dot(z.astype(BF16), gw, preferred_element_type=F32) + gb)
        return _rms_fwd(z * gate, g), y

    ysn, y_ssm = _rowmap("ssm_post", mix_post, [y_dir[0], y_dir[1], u_ssm], [row(ssm_d), W_glu, row(glu_b), gs],
                         [(DS, BF16), (DS, F32)])
    ycat = jnp.concatenate([ypn, _from_chunk_rows(ysn)], axis=1)
    late = _exchange_wait("gather_late_wait", late_handle, ysn)
    W_out = late[0].reshape(D, D)
    W_down = late[1].reshape(F, D)
    W_up_t = late[2].reshape(F2, D)
    CW = late[3].transpose(1, 0, 2).reshape(3, F2)
    h1 = _mm("proj_out", ycat, W_out, 'nn', F32, add=xs)
    (hn,) = _rowmap("norm_ffn", _rms_fwd, [h1], [g2], [(D, BF16)])
    up = _mm("ffn_up", hn, W_up_t, 'nt', F32)
    act = _convact_fwd(up, CW, CB, F)
    h2 = _mm("ffn_down", act, W_down, 'nn', F32, add=h1)

    def head(h, t, g):
        r = lax.rsqrt(_rowmean(h * h) + EPS)
        hh = h * r
        e = hh * g - t
        loss = 0.5 * jnp.sum(_rowmean(e * e), keepdims=True)
        dy = e * (1.0 / D)
        dxh = dy * g
        dh = r * (dxh - hh * _rowmean(dxh * hh))
        return dh, dh, jnp.broadcast_to(loss, (1, LANES)), _colsum(dy * hh)

    dh2, dh2_bf, loss_acc, dg3 = _rowmap("loss_head", head, [h2, tgt], [g3], [(D, F32), (D, BF16)], [(1, LANES), (1, D)])

    dact = _mm("ffn_down_dx", dh2_bf, W_down, 'nt', F32)
    dW_down = _mm("ffn_down_dw", act, dh2_bf, 'tn', BF16)
    dup, dCW, dCB = _convact_bwd(up, dact, CW, CB, F)
    dhn = _mm("ffn_up_dx_val", dup, W_up_t, 'nn', F32, b_rows=(2, 0), a_plane=0)
    dhn = _mm("ffn_up_dx_gate", dup, W_up_t, 'nn', F32, b_rows=(2, 1), a_plane=1, add=dhn)
    dW_up_t = _mm("ffn_up_dw", dup, hn, 'tn', BF16).reshape(F2, D)

    def shards(full_grad):
        return full_grad.reshape((N_DEV, full_grad.shape[0] // N_DEV) + full_grad.shape[1:])

    half = N_DEV // 2
    dCW_sh = jnp.concatenate([h.reshape(3, half, fs).transpose(1, 0, 2) for h in dCW], axis=0)
    ffn_sent = [shards(dW_down), shards(dW_up_t), dCW_sh]
    ffn_handle, token = _exchange_start("scatter_ffn_start", ffn_sent, ['scatter'] * 3, after=dhn)
    g2_late = g2 + token[0:1, 0:1]

    def norm_bwd_add(hx, dy, res, g):
        dx, dg = _rms_bwd(hx, g, dy)
        return dx + res, dg

    def norm_bwd_add2(hx, dy, res, g):
        dx, dg = _rms_bwd(hx, g, dy)
        return dx + res, dx + res, dg

    dh1, dh1_bf, dg2 = _rowmap("norm_ffn_bwd", norm_bwd_add2, [h1, dhn, dh2], [g2_late], [(D, F32), (D, BF16)], [(1, D)])
    dycat = _mm("proj_out_dx", dh1_bf, W_out, 'nt', F32)
    dW_out = _mm("proj_out_dw", ycat, dh1_bf, 'tn', BF16)

    out_sent = [shards(dW_out)]
    out_handle, token = _exchange_start("scatter_out_start", out_sent, ['scatter'], after=dycat)
    q, dpooled, dpool_w, dpool_scale, dgp = _pool_bwd_rows(u, dycat, pool_w_bf, row(pool_scale),
                                                           gp + token[0:1, 0:1], tmp)
    du_pool = _pool_bwd_band(q, dpooled, tmp)

    def mix_post_bwd(y, us, dyn, d, gw, gb, g):
        z = _gelu(y)
        gz = z.astype(BF16)
        sig = _sigmoid(jnp.dot(gz, gw, preferred_element_type=F32) + gb)
        dys, dg = _rms_bwd(z * sig, g, dyn)
        dgl = dys * z * sig * (1.0 - sig)
        dgl_bf = dgl.astype(BF16)
        dz = dys * sig + lax.dot_general(dgl_bf, gw, (((1,), (1,)), ((), ())), preferred_element_type=F32)
        dgw = lax.dot_general(gz, dgl_bf, (((0,), (0,)), ((), ())), preferred_element_type=F32)
        dyv = dz * _gelu_grad(y)
        return dyv, dyv * d, dgw, _colsum(dgl), _colsum(dyv * us), dg

    dyv, du_dir, dglu_w, dglu_b, dssm_d, dgs = _rowmap(
        "ssm_post_bwd", mix_post_bwd, [y_ssm, u_ssm, _to_chunk_rows(dycat[:, DP:])], [row(ssm_d), W_glu, row(glu_b), gs],
        [(DS, BF16), (DS, F32)], [(DS, DS), (1, DS), (1, DS), (1, DS)])

    du_ssm = du_dir
    dB, dC, dvec = [], [], []
    for d in range(2):
        res = _ssm_bwd("ssm_bwd_%d" % d, u_ssm_bf, dyv, du_ssm, mats, vecs, d)
        du_ssm = res[0]
        dB.append(res[1:3])
        dC.append(res[3:5])
        dvec.append(res[5:9])
    grad_b = [_groups_apart(jnp.stack([dB[0][k], dB[1][k]]), 3) for k in range(2)]
    grad_c = [_groups_apart(jnp.stack([dC[0][k], dC[1][k]]), 2) for k in range(2)]
    cots = [jnp.stack([dvec[0][k], dvec[1][k]]).reshape(2 * G, N) for k in range(4)]
    d_a_re, d_a_im, d_log_dt = _ssm_params_bwd(*a_rows, cots)

    rep_grads = {
        'pool_w': dpool_w, 'pool_scale': dpool_scale, 'ssm_log_neg_a_re': d_a_re, 'ssm_a_im': d_a_im,
        'ssm_log_dt': d_log_dt, 'ssm_b_re': grad_b[0], 'ssm_b_im': grad_b[1], 'ssm_c_re': grad_c[0], 'ssm_c_im': grad_c[1],
        'ssm_d': dssm_d, 'glu_b': dglu_b, 'out_norm_pool_g': dgp, 'out_norm_ssm_g': dgs, 'norm_ffn_g': dg2,
        'conv_b': dCB, 'final_norm_g': dg3}
    wide = ('pool_w', 'ssm_b_re', 'ssm_b_im', 'ssm_c_re', 'ssm_c_im')

    def travel(n):
        g = rep_grads[n]
        if n in wide:
            return g.reshape(-1, PACK_W if g.size % PACK_W == 0 else LANES).astype(BF16)
        return _view2d(g.reshape(weights[n].shape))

    small = tuple(n for n in REPLICATED if n != 'norm_mix_g')
    small_sent = [shards(dglu_w.astype(BF16))] + [travel(n) for n in small] + [loss_acc]
    small_handle, token = _exchange_start("reduce_small_start", small_sent, ['scatter'] + ['gather'] * (len(small) + 1),
                                          after=d_a_re)

    du = (jnp.concatenate([du_pool, _from_chunk_rows(du_ssm)], axis=1) + token[0:1, 0:1]).astype(BF16)
    dW_in = _mm("proj_in_dw", xn, du, 'tn', BF16)
    in_handle, token = _exchange_start("scatter_in_start", [shards(dW_in)], ['scatter'], after=dW_in)
    dxn = _mm("proj_in_dx", du, W_in + token[0, 0].astype(BF16), 'nt', F32)
    dx, dg1 = _rowmap("norm_mix_bwd", norm_bwd_add, [xs, dxn, dh1], [g1], [(D, F32)], [(1, D)])
    rep_grads['norm_mix_g'] = dg1
    last_handle, token = _exchange_start("reduce_last_start", [travel('norm_mix_g')], ['gather'], after=dx)

    grads, delta, new_m, new_v = {}, {}, {}, {}

    def update(n, landed_n):
        shape = weights[n].shape
        if len(shape) > 2 and shape[-1] < LANES // 2:
            grads[n] = _sum_slots("sum_" + n, landed_n).reshape(shape)
            delta[n], new_m[n], new_v[n] = _adamw_whole("adamw_" + n, grads[n], weights[n], mom1[n], mom2[n])
            return
        if n == 'w_up':
            res = _sum_adamw("adamw_" + n, landed_n, weights[n].T, mom1[n].T, mom2[n].T)
            grads[n], delta[n], new_m[n], new_v[n] = [r.T for r in res]
            return
        landed_n = landed_n.reshape((N_DEV,) + _view2d(weights[n]).shape)
        res = _sum_adamw("adamw_" + n, landed_n, _view2d(weights[n]), _view2d(mom1[n]), _view2d(mom2[n]))
        grads[n], delta[n], new_m[n], new_v[n] = [r.reshape(shape) for r in res]

    def update_many(name, named_landed):
        items = [(l.reshape((N_DEV,) + _view2d(weights[n]).shape), _view2d(weights[n]), _view2d(mom1[n]), _view2d(mom2[n]))
                 for n, l in named_landed]
        for (n, _), res in zip(named_landed, _sum_adamw_many(name, items)):
            grads[n], delta[n], new_m[n], new_v[n] = [r.reshape(weights[n].shape) for r in res]

    ffn_landed = _exchange_wait("scatter_ffn_wait", ffn_handle, token)
    for n, l in zip(('w_down', 'w_up'), ffn_landed):
        update(n, l)
    (out_landed,) = _exchange_wait("scatter_out_wait", out_handle, token)
    update('w_out', out_landed)
    small_landed = _exchange_wait("reduce_small_wait", small_handle, token)
    tiny = [('conv_w', ffn_landed[2])]
    for n, l in zip(('glu_w',) + small, small_landed):
        if weights[n].size <= ADAM_TINY:
            tiny.append((n, l))
        else:
            update(n, l)
    update_many("adamw_tiny", tiny)

    def loss_sum(parts_ref, o_ref):
        s = parts_ref[0]
        for q in range(1, N_DEV):
            s = s + parts_ref[q]
        o_ref[...] = s

    loss = pl.pallas_call(loss_sum, name="loss_sum", out_shape=jax.ShapeDtypeStruct((1, LANES), F32))(small_landed[-1])[0, 0]

    update('w_in', _exchange_wait("scatter_in_wait", in_handle, grads[small[-1]])[0])
    update('norm_mix_g', _exchange_wait("reduce_last_wait", last_handle, grads['w_in'])[0])

    return (loss, dx[None], *[grads[n] for n in WEIGHTS], *[delta[n] for n in WEIGHTS],
            *[new_m[n] for n in WEIGHTS], *[new_v[n] for n in WEIGHTS])
```

```python
import functools

import jax
import jax.numpy as jnp
import numpy as np
from jax import lax
from jax.experimental import pallas as pl
from jax.experimental.pallas import tpu as pltpu

F32 = jnp.float32
BF16 = jnp.bfloat16
MESH_ID = pl.DeviceIdType.MESH

N_DEV = 8
EPS = 1e-6
POOL_WINDOWS = (2, 4, 8, 16)
POOL_GROUP = 128
POOL_PAD = 64
SSM_GROUP = 16
SSM_STATE = 64
SSM_BLOCK_GROUPS = 8
N_CHUNK = 8
ROW_BLOCK = 512
SCAN_UNROLL = 4
GROUP_SHIFT = 4
STATE_SHIFT = 6
_NT = (((1,), (1,)), ((), ()))
_TN = (((0,), (0,)), ((), ()))
LANES = 128
PACK_W = 1024
VMEM_LIMIT = 56 * 1024 * 1024
MM_VMEM_BUDGET = 40 * 1024 * 1024
MM_TILE_CAP = 1408
MM_ROWS_TILE = 512
ADAM_TILE_BYTES = 4 * 1024 * 1024
ADAM_TINY = 8192

ADAM_LR = 0.001
ADAM_B1 = 0.9
ADAM_B2 = 0.999
ADAM_EPS = 1e-08
ADAM_WD = 0.01
ADAM_STEP = 10

WEIGHTS = ['norm_mix_g', 'w_in', 'pool_w', 'pool_scale', 'ssm_log_neg_a_re', 'ssm_a_im', 'ssm_log_dt',
           'ssm_b_re', 'ssm_b_im', 'ssm_c_re', 'ssm_c_im', 'ssm_d', 'glu_w', 'glu_b', 'out_norm_pool_g',
           'out_norm_ssm_g', 'w_out', 'norm_ffn_g', 'w_up', 'conv_w', 'conv_b', 'w_down', 'final_norm_g']
SHARDED = ('w_in', 'w_out', 'w_down', 'glu_w', 'w_up', 'conv_w')
REPLICATED = tuple(n for n in WEIGHTS if n not in SHARDED)


def _pick(n, prefs):
    for p in prefs:
        if n % p == 0:
            return p
    return n


def _params(sem, vmem=None):
    return pltpu.CompilerParams(dimension_semantics=sem, vmem_limit_bytes=vmem or VMEM_LIMIT)


def _tiles(n, cap):
    return [d for d in range(LANES, min(n, cap) + 1, LANES) if n % d == 0] or [n]


def _mm(name, a, b, mode, out_dtype, add=None, b_rows=(1, 0), a_plane=None):
    batch = a.shape[0] if a.ndim == 3 and a_plane is None else None
    if mode == 'nn':
        (M, K), N = a.shape[-2:], b.shape[1]
        assert b.shape[0] == K * b_rows[0]
    elif mode == 'nt':
        (M, K), (N, _) = a.shape[-2:], b.shape
    else:
        (K, M), (_, N) = a.shape[-2:], b.shape
    dims = {'nn': (((1,), (0,)), ((), ())), 'nt': (((1,), (1,)), ((), ())), 'tn': (((0,), (0,)), ((), ()))}[mode]
    sa, sb, so = a.dtype.itemsize, b.dtype.itemsize, jnp.dtype(out_dtype).itemsize
    best = None
    for tm in _tiles(M, MM_TILE_CAP):
        for tn in _tiles(N, MM_TILE_CAP):
            need = 2 * (tm * K * sa + tn * K * sb + tm * tn * (so + (4 if add is not None else 0)))
            if need <= MM_VMEM_BUDGET:
                key = (tm * tn / (tm + tn), tm * tn)
                if best is None or key > best[0]:
                    best = (key, tm, tn)
    _, tm, tn = best
    rows_inner = a.size * sa * (N // tn) + b.size * sb < a.size * sa + b.size * sb * (M // tm)

    def kern(*refs):
        a_ref, b_ref = refs[:2]
        o_ref = refs[-1]
        r = lax.dot_general(a_ref[...].astype(BF16), b_ref[...].astype(BF16), dims, preferred_element_type=F32)
        if add is not None:
            r = r + refs[2][...]
        o_ref[...] = r.astype(o_ref.dtype)

    if rows_inner:
        grid = (N // tn, M // tm)
        ij = lambda g0, g1: (g1, g0)
    else:
        grid = (M // tm, N // tn)
        ij = lambda g0, g1: (g0, g1)

    def spec(block, index, batched=False):
        if batch is None:
            return pl.BlockSpec(block, lambda g0, g1: index(*ij(g0, g1)))
        if batched:
            return pl.BlockSpec((None,) + block, lambda p, g0, g1: (p,) + index(*ij(g0, g1)))
        return pl.BlockSpec(block, lambda p, g0, g1: index(*ij(g0, g1)))

    if a_plane is not None:
        assert mode != 'tn'
        a_spec = pl.BlockSpec((None, tm, K), lambda g0, g1: (a_plane, ij(g0, g1)[0], 0))
    else:
        a_spec = (spec((K, tm), lambda i, j: (0, i), True) if mode == 'tn' else spec((tm, K), lambda i, j: (i, 0), True))
    b_spec = (spec((tn, K), lambda i, j: (j, 0)) if mode == 'nt' else spec((K, tn), lambda i, j: (b_rows[1], j)))
    in_specs = [a_spec, b_spec]
    args = [a, b]
    if add is not None:
        in_specs.append(spec((tm, tn), lambda i, j: (i, j), True))
        args.append(add)
    lead = () if batch is None else (batch,)
    return pl.pallas_call(
        kern, name=name, grid=lead + grid, in_specs=in_specs,
        out_specs=spec((tm, tn), lambda i, j: (i, j), True),
        out_shape=jax.ShapeDtypeStruct(lead + (M, N), out_dtype),
        compiler_params=_params(("parallel",) * (len(lead) + 2)),
    )(*args)


def _store_rows_and_sums(i, outs, res, n_row):
    for o, v in zip(outs[:n_row], res[:n_row]):
        o[...] = v.astype(o.dtype)
    for o, v in zip(outs[n_row:], res[n_row:]):
        @pl.when(i == 0)
        def _(o=o, v=v):
            o[...] = v

        @pl.when(i != 0)
        def _(o=o, v=v):
            o[...] += v


def _mm_rows(name, a, b, mode, fn, rows, fulls, out_rows, out_accs=(), add=None, b_rows=(1, 0), a_plane=None):
    M, K = a.shape[-2:]
    N = b.shape[1] if mode == 'nn' else b.shape[0]
    assert b.shape[0] == (K * b_rows[0] if mode == 'nn' else N)
    dims = _NT if mode == 'nt' else (((1,), (0,)), ((), ()))
    tm = min(MM_ROWS_TILE, M)
    n_extra, n_row = len(rows) + len(fulls), len(out_rows)

    def kern(*refs):
        i = pl.program_id(0)
        r = lax.dot_general(refs[0][...].astype(BF16), refs[1][...].astype(BF16), dims, preferred_element_type=F32)
        pos = 2
        if add is not None:
            r = r + refs[pos][...]
            pos += 1
        res = fn(r, *[x[...] for x in refs[pos:pos + n_extra]])
        _store_rows_and_sums(i, refs[pos + n_extra:], res, n_row)

    def full_spec(shape):
        nd = len(shape)
        return pl.BlockSpec(tuple(shape), lambda i: (0,) * nd)

    a_spec = (pl.BlockSpec((tm, K), lambda i: (i, 0)) if a_plane is None
              else pl.BlockSpec((None, tm, K), lambda i: (a_plane, i, 0)))
    b_spec = pl.BlockSpec((K, N), lambda i: (b_rows[1], 0)) if mode == 'nn' else pl.BlockSpec((N, K), lambda i: (0, 0))
    row_spec = lambda c: pl.BlockSpec((tm, c), lambda i: (i, 0))
    in_specs = [a_spec, b_spec] + ([row_spec(N)] if add is not None else []) + [row_spec(r.shape[1]) for r in rows] \
        + [full_spec(f.shape) for f in fulls]
    args = [a, b] + ([add] if add is not None else []) + list(rows) + list(fulls)
    return pl.pallas_call(
        kern, name=name, grid=(M // tm,), in_specs=in_specs,
        out_specs=[row_spec(c) for c, _ in out_rows] + [full_spec(s) for s in out_accs],
        out_shape=[jax.ShapeDtypeStruct((M, c), dt) for c, dt in out_rows] + [jax.ShapeDtypeStruct(tuple(s), F32) for s in out_accs],
        compiler_params=_params(("arbitrary",)),
    )(*args)


def _rowmap(name, fn, rows, fulls, out_rows, out_accs=(), tm=512):
    rows = [r if isinstance(r, tuple) else (r, r.shape[1], 0) for r in rows]
    T = rows[0][0].shape[0]
    tm = min(tm, T)
    assert T % tm == 0
    n_in, n_row = len(rows) + len(fulls), len(out_rows)

    def kern(*refs):
        i = pl.program_id(0)
        res = fn(*[r[...] for r in refs[:n_in]])
        res = res if isinstance(res, (tuple, list)) else (res,)
        _store_rows_and_sums(i, refs[n_in:], res, n_row)

    def full_spec(shape):
        nd = len(shape)
        return pl.BlockSpec(tuple(shape), lambda i: (0,) * nd)

    in_specs = [pl.BlockSpec((tm, w), lambda i, k=k: (i, k)) for _, w, k in rows] + [full_spec(f.shape) for f in fulls]
    out_specs = [pl.BlockSpec((tm, c), lambda i: (i, 0)) for c, _ in out_rows] + [full_spec(s) for s in out_accs]
    out_shape = [jax.ShapeDtypeStruct((T, c), dt) for c, dt in out_rows] + [jax.ShapeDtypeStruct(tuple(s), F32) for s in out_accs]
    return pl.pallas_call(
        kern, name=name, grid=(T // tm,), in_specs=in_specs, out_specs=out_specs, out_shape=out_shape,
        compiler_params=_params(("arbitrary",)),
    )(*[r for r, _, _ in rows], *fulls)


def _colsum(v):
    return jnp.sum(v, axis=0, keepdims=True)


def _rowmean(v):
    return jnp.mean(v, axis=-1, keepdims=True)


def _rms_fwd(x, g):
    r = lax.rsqrt(_rowmean(x * x) + EPS)
    return x * r * g


def _rms_bwd(x, g, dy):
    r = lax.rsqrt(_rowmean(x * x) + EPS)
    xh = x * r
    dxh = dy * g
    return r * (dxh - xh * _rowmean(dxh * xh)), _colsum(dy * xh)


def _sigmoid(v):
    return 0.5 * jnp.tanh(0.5 * v) + 0.5


def _gelu(y):
    c = np.sqrt(2.0 / np.pi).astype(np.float32)
    return 0.5 * y * (1.0 + jnp.tanh(c * (y + 0.044715 * (y * y * y))))


def _gelu_grad(y):
    c = np.sqrt(2.0 / np.pi).astype(np.float32)
    th = jnp.tanh(c * (y + 0.044715 * (y * y * y)))
    return 0.5 * (1.0 + th) + 0.5 * y * (1.0 - th * th) * c * (1.0 + 3.0 * 0.044715 * (y * y))


def _split3(v):
    hi = v.astype(BF16)
    r1 = v - hi.astype(F32)
    mid = r1.astype(BF16)
    lo = (r1 - mid.astype(F32)).astype(BF16)
    return hi, mid, lo


def _band(tm, lo_off, hi_off):
    shape = (tm, tm + 2 * POOL_PAD)
    d = lax.broadcasted_iota(jnp.int32, shape, 1) - lax.broadcasted_iota(jnp.int32, shape, 0) - POOL_PAD
    return ((d >= lo_off) & (d < hi_off)).astype(BF16)


def _band_sum(band, ext):
    hi, mid, lo = _split3(ext)
    dot = functools.partial(jnp.dot, preferred_element_type=F32)
    return dot(band, hi) + dot(band, mid) + dot(band, lo)


def _window_count(r0, tm, half, T):
    t = r0 + lax.broadcasted_iota(jnp.int32, (tm, 1), 0)
    return (jnp.minimum(t + half, T) - jnp.maximum(t - half, 0)).astype(F32)


def _halo_specs(tm, C, T):
    per = tm // POOL_PAD
    last = T // POOL_PAD - 1
    return [pl.BlockSpec((POOL_PAD, C), lambda i: (jnp.maximum(i * per - 1, 0), 0)),
            pl.BlockSpec((tm, C), lambda i: (i, 0)),
            pl.BlockSpec((POOL_PAD, C), lambda i: (jnp.minimum((i + 1) * per, last), 0))]


def _with_halo(before_ref, main_ref, after_ref, i, n):
    before = jnp.where(i > 0, before_ref[...], 0.0)
    after = jnp.where(i < n - 1, after_ref[...], 0.0)
    return jnp.concatenate([before, main_ref[...], after], axis=0)


def _pool_block(ext, ctr, w_ref, r0, tm, T):
    pooled, conc = [], []
    for gi, w in enumerate(POOL_WINDOWS):
        half = w // 2
        cols = slice(gi * POOL_GROUP, (gi + 1) * POOL_GROUP)
        ws = _band_sum(_band(tm, -half, half), ext[:, cols])
        p = ws / _window_count(r0, tm, half, T) - ctr[:, cols]
        pooled.append(p)
        conc.append(jnp.dot(p.astype(BF16), w_ref[gi], preferred_element_type=F32))
    return pooled, jnp.concatenate(conc, axis=1)


def _pool_fwd(u, pool_w_bf, pool_scale, g_pool, tm):
    T = u.shape[0]
    nw = len(POOL_WINDOWS)
    C = nw * POOL_GROUP
    n = T // tm

    def kern(ub_ref, u_ref, ua_ref, w_ref, sc_ref, g_ref, o_ref):
        i = pl.program_id(0)
        r0 = pl.multiple_of(i * tm, tm)
        _, conc = _pool_block(_with_halo(ub_ref, u_ref, ua_ref, i, n), u_ref[...], w_ref, r0, tm, T)
        o_ref[...] = _rms_fwd(conc * sc_ref[...], g_ref[...]).astype(o_ref.dtype)

    return pl.pallas_call(
        kern, name="pool_fwd", grid=(n,),
        in_specs=_halo_specs(tm, C, T) + [pl.BlockSpec(pool_w_bf.shape, lambda i: (0, 0, 0)),
                                          pl.BlockSpec((1, C), lambda i: (0, 0)), pl.BlockSpec((1, C), lambda i: (0, 0))],
        out_specs=pl.BlockSpec((tm, C), lambda i: (i, 0)),
        out_shape=jax.ShapeDtypeStruct((T, C), BF16),
        compiler_params=_params(("arbitrary",)),
    )(u, u, u, pool_w_bf, pool_scale, g_pool)


def _pool_bwd_rows(u, d_y, pool_w_bf, pool_scale, g_pool, tm):
    T = u.shape[0]
    nw = len(POOL_WINDOWS)
    C = nw * POOL_GROUP
    n = T // tm

    def kern(ub_ref, u_ref, ua_ref, dy_ref, w_ref, sc_ref, g_ref, q_ref, dp_ref, dw_ref, dsc_ref, dg_ref):
        i = pl.program_id(0)
        r0 = pl.multiple_of(i * tm, tm)
        pooled, conc = _pool_block(_with_halo(ub_ref, u_ref, ua_ref, i, n), u_ref[...], w_ref, r0, tm, T)
        sc = sc_ref[...]
        dyp, dg = _rms_bwd(conc * sc, g_ref[...], dy_ref[...])
        dsc = _colsum(dyp * conc)
        dconc = (dyp * sc).astype(BF16)
        dws = []
        for gi, w in enumerate(POOL_WINDOWS):
            cols = slice(gi * POOL_GROUP, (gi + 1) * POOL_GROUP)
            dc = dconc[:, cols]
            dp = lax.dot_general(dc, w_ref[gi], (((1,), (1,)), ((), ())), preferred_element_type=F32)
            dp_ref[:, cols] = dp
            q_ref[:, cols] = dp / _window_count(r0, tm, w // 2, T)
            dws.append(lax.dot_general(pooled[gi].astype(BF16), dc, (((0,), (0,)), ((), ())), preferred_element_type=F32))

        @pl.when(i == 0)
        def _():
            for gi in range(nw):
                dw_ref[gi] = dws[gi]
            dsc_ref[...] = dsc
            dg_ref[...] = dg

        @pl.when(i != 0)
        def _():
            for gi in range(nw):
                dw_ref[gi] += dws[gi]
            dsc_ref[...] += dsc
            dg_ref[...] += dg

    full2 = pl.BlockSpec((1, C), lambda i: (0, 0))
    row_blk = pl.BlockSpec((tm, C), lambda i: (i, 0))
    return pl.pallas_call(
        kern, name="pool_bwd_rows", grid=(n,),
        in_specs=_halo_specs(tm, C, T) + [row_blk, pl.BlockSpec(pool_w_bf.shape, lambda i: (0, 0, 0)), full2, full2],
        out_specs=[row_blk, row_blk, pl.BlockSpec((nw, POOL_GROUP, POOL_GROUP), lambda i: (0, 0, 0)), full2, full2],
        out_shape=[jax.ShapeDtypeStruct((T, C), F32), jax.ShapeDtypeStruct((T, C), F32),
                   jax.ShapeDtypeStruct((nw, POOL_GROUP, POOL_GROUP), F32),
                   jax.ShapeDtypeStruct((1, C), F32), jax.ShapeDtypeStruct((1, C), F32)],
        compiler_params=_params(("arbitrary",)),
    )(u, u, u, d_y, pool_w_bf, pool_scale, g_pool)


def _pool_bwd_band(q, dpooled, tm):
    T, C = dpooled.shape
    n = T // tm

    def kern(qb_ref, q_ref, qa_ref, dp_ref, o_ref):
        ext = _with_halo(qb_ref, q_ref, qa_ref, pl.program_id(0), n)
        for gi, w in enumerate(POOL_WINDOWS):
            half = w // 2
            cols = slice(gi * POOL_GROUP, (gi + 1) * POOL_GROUP)
            o_ref[:, cols] = _band_sum(_band(tm, -half + 1, half + 1), ext[:, cols]) - dp_ref[:, cols]

    return pl.pallas_call(
        kern, name="pool_bwd_band", grid=(n,),
        in_specs=_halo_specs(tm, C, T) + [pl.BlockSpec((tm, C), lambda i: (i, 0))],
        out_specs=pl.BlockSpec((tm, C), lambda i: (i, 0)),
        out_shape=jax.ShapeDtypeStruct((T, C), F32),
        compiler_params=_params(("arbitrary",)),
    )(q, q, q, dpooled)


def _cmul(ar, ai, br, bi):
    return ar * br - ai * bi, ar * bi + ai * br


def _ssm_discretise(log_neg_a_re, a_im, log_dt):
    dt = jnp.exp(log_dt)
    a_re = -jnp.exp(log_neg_a_re)
    mag = jnp.exp(a_re * dt)
    ang = a_im * dt
    lam_re, lam_im = mag * jnp.cos(ang), mag * jnp.sin(ang)
    den = a_re * a_re + a_im * a_im
    f_re = ((lam_re - 1.0) * a_re + lam_im * a_im) / den
    f_im = (lam_im * a_re - (lam_re - 1.0) * a_im) / den
    return lam_re, lam_im, f_re, f_im


def _ssm_params_fwd(log_neg_a_re, a_im, log_dt):
    rows, n = log_neg_a_re.shape

    def kern(a_ref, b_ref, c_ref, o1, o2, o3, o4):
        for o, v in zip((o1, o2, o3, o4), _ssm_discretise(a_ref[...], b_ref[...], c_ref[...])):
            o[...] = v

    return pl.pallas_call(kern, name="ssm_params_fwd", out_shape=[jax.ShapeDtypeStruct((rows, n), F32)] * 4)(
        log_neg_a_re, a_im, log_dt)


def _ssm_params_bwd(log_neg_a_re, a_im, log_dt, cots):
    rows, n = log_neg_a_re.shape

    def kern(a_ref, b_ref, c_ref, g1, g2, g3, g4, o1, o2, o3):
        _, vjp = jax.vjp(_ssm_discretise, a_ref[...], b_ref[...], c_ref[...])
        d1, d2, d3 = vjp((g1[...], g2[...], g3[...], g4[...]))
        o1[...] = d1
        o2[...] = d2
        o3[...] = d3

    return pl.pallas_call(
        kern, name="ssm_params_bwd",
        out_shape=[jax.ShapeDtypeStruct((rows, n), F32), jax.ShapeDtypeStruct((rows, n), F32),
                   jax.ShapeDtypeStruct((rows, 1), F32)])(log_neg_a_re, a_im, log_dt, *cots)


def _cpow(lr, li, n):
    out = None
    br, bi = lr, li
    while n:
        if n & 1:
            out = (br, bi) if out is None else _cmul(out[0], out[1], br, bi)
        n >>= 1
        if n:
            br, bi = _cmul(br, bi, br, bi)
    return out


def _slab(t):
    return pl.ds(pl.multiple_of(t * N_CHUNK, N_CHUNK), N_CHUNK)


def _steps(n, step, carry):
    main = n // SCAN_UNROLL

    def body(i, c):
        for r in range(SCAN_UNROLL):
            c = step(i * SCAN_UNROLL + r, c)
        return c

    carry = lax.fori_loop(0, main, body, carry)
    for k in range(main * SCAN_UNROLL, n):
        carry = step(jnp.int32(k), carry)
    return carry


def _shift_chunks(vr, vi, reverse):
    sub = lax.broadcasted_iota(jnp.int32, vr.shape, 0)
    if reverse:
        keep = sub != N_CHUNK - 1
        return jnp.where(keep, pltpu.roll(vr, N_CHUNK - 1, 0), 0.0), jnp.where(keep, pltpu.roll(vi, N_CHUNK - 1, 0), 0.0)
    keep = sub != 0
    return jnp.where(keep, pltpu.roll(vr, 1, 0), 0.0), jnp.where(keep, pltpu.roll(vi, 1, 0), 0.0)


def _chunk_scan(xr, xi, lr, li, Lc, reverse, prev=None):
    C = xr.shape[1]
    lrb, lib = jnp.broadcast_to(lr, (N_CHUNK, C)), jnp.broadcast_to(li, (N_CHUNK, C))
    zero = jnp.zeros((N_CHUNK, C), F32)

    def step_of(k):
        return (Lc - 1 - k) if reverse else k

    def advance(sr, si, t):
        return lrb * sr - lib * si + xr[_slab(t), :], lrb * si + lib * sr + xi[_slab(t), :]

    def pass1(k, c):
        return advance(c[0], c[1], step_of(k))

    er, ei = _steps(Lc, pass1, (zero, zero))
    pr, pi = _cpow(lr, li, Lc)
    prb, pib = jnp.broadcast_to(pr, (N_CHUNK, C)), jnp.broadcast_to(pi, (N_CHUNK, C))
    cr, ci = zero, zero
    for _ in range(N_CHUNK - 1):
        cr, ci = _shift_chunks(er + prb * cr - pib * ci, ei + prb * ci + pib * cr, reverse)

    if prev is None:
        def pass2(k, c):
            t = step_of(k)
            nr, ni = advance(c[0], c[1], t)
            xr[_slab(t), :] = nr
            xi[_slab(t), :] = ni
            return nr, ni

        _steps(Lc, pass2, (cr, ci))
        return None

    qr, qi = prev
    def pair(ar, ai, wr, wi, acc):
        return acc[0] + ar * wr + ai * wi, acc[1] + ai * wr - ar * wi

    def pass2(k, c):
        sr, si, accr, acci = c
        t = step_of(k)
        nr, ni = advance(sr, si, t)
        xr[_slab(t), :] = nr
        xi[_slab(t), :] = ni
        tp = t - 1 if reverse else t + 1
        accr, acci = pair(nr, ni, qr[_slab(tp), :], qi[_slab(tp), :], (accr, acci))
        return nr, ni, accr, acci

    sr, si, accr, acci = _steps(Lc - 1, pass2, (cr, ci, zero, zero))
    t = step_of(Lc - 1)
    nr, ni = advance(sr, si, t)
    xr[_slab(t), :] = nr
    xi[_slab(t), :] = ni
    edge = step_of(0)
    wr, wi = _shift_chunks(qr[_slab(edge), :], qi[_slab(edge), :], not reverse)
    return pair(nr, ni, wr, wi, (accr, acci))


def _ssm_dims(T):
    assert T % ROW_BLOCK == 0
    return T // N_CHUNK, T // ROW_BLOCK


def _group_mask():
    shape = (SSM_BLOCK_GROUPS * SSM_GROUP, SSM_BLOCK_GROUPS * SSM_STATE)
    rows = lax.broadcasted_iota(jnp.int32, shape, 0) >> GROUP_SHIFT
    cols = lax.broadcasted_iota(jnp.int32, shape, 1) >> STATE_SHIFT
    return rows == cols


def _expand_groups(compact):
    return jnp.where(_group_mask(), jnp.tile(compact, (SSM_BLOCK_GROUPS, 1)), 0.0).astype(BF16)


def _collect_groups(block):
    kept = jnp.where(_group_mask(), block, 0.0)
    out = kept[0:SSM_GROUP]
    for g in range(1, SSM_BLOCK_GROUPS):
        out = out + kept[g * SSM_GROUP:(g + 1) * SSM_GROUP]
    return out


def _ssm_specs(T, direction):
    SC = SSM_BLOCK_GROUPS * SSM_STATE
    col = pl.BlockSpec((T, LANES), lambda m: (0, m))
    mat = pl.BlockSpec((None, None, SSM_GROUP, SC), lambda m: (direction, m, 0, 0))
    vec = pl.BlockSpec((None, None, 1, SC), lambda m: (direction, m, 0, 0))
    return col, mat, vec


def _ssm_project_in(u_ref, bre, bim, xr, xi, nblk):
    for blk in range(nblk):
        rows = pl.ds(blk * ROW_BLOCK, ROW_BLOCK)
        ub = u_ref[rows, :]
        xr[rows, :] = jnp.dot(ub, bre, preferred_element_type=F32)
        xi[rows, :] = jnp.dot(ub, bim, preferred_element_type=F32)


def _ssm_fwd(name, u_bf, mats, vecs, direction):
    T, W = u_bf.shape
    SC = SSM_BLOCK_GROUPS * SSM_STATE
    nb = W // LANES
    Lc, nblk = _ssm_dims(T)
    reverse = bool(direction)

    def kern(u_ref, bre_ref, bim_ref, cre_ref, cim_ref, lr_ref, li_ref, fr_ref, fi_ref, y_ref, xr, xi):
        _ssm_project_in(u_ref, _expand_groups(bre_ref[...]), _expand_groups(bim_ref[...]), xr, xi, nblk)
        _chunk_scan(xr, xi, lr_ref[...], li_ref[...], Lc, reverse)
        fr, fi = fr_ref[...], fi_ref[...]
        cre, cim = _expand_groups(cre_ref[...]), _expand_groups(cim_ref[...])
        for blk in range(nblk):
            rows = pl.ds(blk * ROW_BLOCK, ROW_BLOCK)
            sr, si = _cmul(fr, fi, xr[rows, :], xi[rows, :])
            y_ref[rows, :] = (lax.dot_general(sr.astype(BF16), cre, _NT, preferred_element_type=F32)
                              - lax.dot_general(si.astype(BF16), cim, _NT, preferred_element_type=F32))

    col, mat, vec = _ssm_specs(T, direction)
    return pl.pallas_call(
        kern, name=name, grid=(nb,),
        in_specs=[col, mat, mat, mat, mat, vec, vec, vec, vec],
        out_specs=col, out_shape=jax.ShapeDtypeStruct((T, W), F32),
        scratch_shapes=[pltpu.VMEM((T, SC), F32), pltpu.VMEM((T, SC), F32)],
        compiler_params=_params(("arbitrary",)),
    )(u_bf, *mats, *vecs)


def _ssm_bwd(name, u_bf, dy_bf, du_in, mats, vecs, direction):
    T, W = u_bf.shape
    SC = SSM_BLOCK_GROUPS * SSM_STATE
    nb = W // LANES
    Lc, nblk = _ssm_dims(T)
    reverse = bool(direction)

    def kern(u_ref, dy_ref, dui_ref, bre_ref, bim_ref, cre_ref, cim_ref, lr_ref, li_ref, fr_ref, fi_ref,
             du_ref, dbre_ref, dbim_ref, dcre_ref, dcim_ref, dlr_ref, dli_ref, dfr_ref, dfi_ref,
             xr, xi, gr, gi):
        lr, li, fr, fi = lr_ref[...], li_ref[...], fr_ref[...], fi_ref[...]
        bre, bim = _expand_groups(bre_ref[...]), _expand_groups(bim_ref[...])
        cre, cim = _expand_groups(cre_ref[...]), _expand_groups(cim_ref[...])
        _ssm_project_in(u_ref, bre, bim, xr, xi, nblk)
        _chunk_scan(xr, xi, lr, li, Lc, reverse)
        mre = jnp.zeros((LANES, SC), F32)
        mim = jnp.zeros((LANES, SC), F32)
        dfr = jnp.zeros((1, SC), F32)
        dfi = jnp.zeros((1, SC), F32)
        for blk in range(nblk):
            rows = pl.ds(blk * ROW_BLOCK, ROW_BLOCK)
            dy = dy_ref[rows, :]
            x_r, x_i = xr[rows, :], xi[rows, :]
            g_r = jnp.dot(dy, cre, preferred_element_type=F32)
            g_i = -jnp.dot(dy, cim, preferred_element_type=F32)
            mre += lax.dot_general(dy, x_r.astype(BF16), _TN, preferred_element_type=F32)
            mim += lax.dot_general(dy, x_i.astype(BF16), _TN, preferred_element_type=F32)
            dfr += _colsum(g_r * x_r + g_i * x_i)
            dfi += _colsum(g_i * x_r - g_r * x_i)
            gr[rows, :] = fr * g_r + fi * g_i
            gi[rows, :] = fr * g_i - fi * g_r
        dcre_ref[...] = _collect_groups(fr * mre - fi * mim)
        dcim_ref[...] = _collect_groups(-(fr * mim + fi * mre))
        dfr_ref[...] = dfr
        dfi_ref[...] = dfi
        accr, acci = _chunk_scan(gr, gi, lr, -li, Lc, not reverse, prev=(xr, xi))
        dlr_ref[...] = _colsum(accr)
        dli_ref[...] = _colsum(acci)
        dbre = jnp.zeros((LANES, SC), F32)
        dbim = jnp.zeros((LANES, SC), F32)
        for blk in range(nblk):
            rows = pl.ds(blk * ROW_BLOCK, ROW_BLOCK)
            ub = u_ref[rows, :]
            a_r, a_i = gr[rows, :].astype(BF16), gi[rows, :].astype(BF16)
            dbre += lax.dot_general(ub, a_r, _TN, preferred_element_type=F32)
            dbim += lax.dot_general(ub, a_i, _TN, preferred_element_type=F32)
            du_ref[rows, :] = (dui_ref[rows, :] + lax.dot_general(a_r, bre, _NT, preferred_element_type=F32)
                               + lax.dot_general(a_i, bim, _NT, preferred_element_type=F32))
        dbre_ref[...] = _collect_groups(dbre)
        dbim_ref[...] = _collect_groups(dbim)

    col, mat, vec = _ssm_specs(T, direction)
    mat_out = pl.BlockSpec((None, SSM_GROUP, SC), lambda m: (m, 0, 0))
    vec_out = pl.BlockSpec((None, 1, SC), lambda m: (m, 0, 0))
    mat_shape = jax.ShapeDtypeStruct((nb, SSM_GROUP, SC), F32)
    vec_shape = jax.ShapeDtypeStruct((nb, 1, SC), F32)
    return pl.pallas_call(
        kern, name=name, grid=(nb,),
        in_specs=[col, col, col, mat, mat, mat, mat, vec, vec, vec, vec],
        out_specs=[col, mat_out, mat_out, mat_out, mat_out, vec_out, vec_out, vec_out, vec_out],
        out_shape=[jax.ShapeDtypeStruct((T, W), F32), mat_shape, mat_shape, mat_shape, mat_shape,
                   vec_shape, vec_shape, vec_shape, vec_shape],
        scratch_shapes=[pltpu.VMEM((T, SC), F32)] * 4,
        compiler_params=_params(("arbitrary",)),
    )(u_bf, dy_bf, du_in, *mats, *vecs)


def _to_chunk_rows(a):
    T, C = a.shape
    return a.reshape(N_CHUNK, T // N_CHUNK, C).transpose(1, 0, 2).reshape(T, C)


def _from_chunk_rows(a):
    T, C = a.shape
    return a.reshape(T // N_CHUNK, N_CHUNK, C).transpose(1, 0, 2).reshape(T, C)


def _groups_side_by_side(p, channel_axis):
    d, G = p.shape[:2]
    nb = G // SSM_BLOCK_GROUPS
    v = p.reshape(d, nb, SSM_BLOCK_GROUPS, p.shape[2], p.shape[3])
    v = v.transpose(0, 1, 4, 2, 3) if channel_axis == 3 else v.transpose(0, 1, 3, 2, 4)
    return v.reshape(d, nb, SSM_GROUP, SSM_BLOCK_GROUPS * SSM_STATE)


def _groups_apart(m, channel_axis):
    d, nb = m.shape[:2]
    v = m.reshape(d, nb, SSM_GROUP, SSM_BLOCK_GROUPS, SSM_STATE)
    v = v.transpose(0, 1, 3, 4, 2) if channel_axis == 3 else v.transpose(0, 1, 3, 2, 4)
    return v.reshape((d, nb * SSM_BLOCK_GROUPS) + v.shape[3:])


def _conv_taps(ref, c, R, T):
    n = T // R
    r0 = pl.multiple_of(c * R, R)
    main = ref[pl.ds(r0, R), :]
    before = ref[pl.ds(pl.multiple_of(jnp.maximum(r0 - 8, 0), 8), 8), :]
    after = ref[pl.ds(pl.multiple_of(jnp.minimum(r0 + R, T - 8), 8), 8), :]
    ext = jnp.concatenate([jnp.where(c > 0, before, 0.0), main, jnp.where(c < n - 1, after, 0.0)], axis=0)
    m1 = pltpu.roll(ext, 1, 0)[8:8 + R]
    p1 = pltpu.roll(ext, R + 15, 0)[8:8 + R]
    return m1, main, p1, r0


def _conv_specs(T, F, cb):
    nj = F // cb
    lo = lambda rows: pl.BlockSpec((rows, cb), lambda j: (0, j))
    hi = lambda rows: pl.BlockSpec((rows, cb), lambda j: (0, j + nj))
    return nj, lo, hi


def _convact_fwd(up, conv_w, conv_b, F):
    T = up.shape[0]
    cb = _pick(F, (256, 128))
    R = _pick(T, (256, 128, 64))
    nj, lo, hi = _conv_specs(T, F, cb)

    def kern(uv, ug, wv, wg, bv, bg, o_ref):
        def body(c, carry):
            v1, v0, v2, r0 = _conv_taps(uv, c, R, T)
            g1, g0, g2, _ = _conv_taps(ug, c, R, T)
            val = v1 * wv[0:1, :] + v0 * wv[1:2, :] + v2 * wv[2:3, :] + bv[...]
            gate = g1 * wg[0:1, :] + g0 * wg[1:2, :] + g2 * wg[2:3, :] + bg[...]
            o_ref[pl.ds(r0, R), :] = (val * (gate * _sigmoid(gate))).astype(o_ref.dtype)
            return carry

        lax.fori_loop(0, T // R, body, 0)

    return pl.pallas_call(
        kern, name="convact_fwd", grid=(nj,),
        in_specs=[lo(T), hi(T), lo(3), hi(3), lo(1), hi(1)],
        out_specs=lo(T), out_shape=jax.ShapeDtypeStruct((T, F), BF16),
        compiler_params=_params(("arbitrary",)),
    )(up, up, conv_w, conv_w, conv_b, conv_b)


def _convact_bwd(up, dact, conv_w, conv_b, F):
    T = up.shape[0]
    cb = _pick(F, (256, 128))
    R = _pick(T, (256, 128, 64))
    nj, lo, hi = _conv_specs(T, F, cb)

    def kern(uv, ug, da, wv, wg, bv, bg, dup, dwv, dwg, dbv, dbg, sv, sg):
        zero = jnp.zeros((1, cb), F32)

        def pass_a(c, acc):
            v1, v0, v2, r0 = _conv_taps(uv, c, R, T)
            g1, g0, g2, _ = _conv_taps(ug, c, R, T)
            val = v1 * wv[0:1, :] + v0 * wv[1:2, :] + v2 * wv[2:3, :] + bv[...]
            gate = g1 * wg[0:1, :] + g0 * wg[1:2, :] + g2 * wg[2:3, :] + bg[...]
            sig = _sigmoid(gate)
            d = da[pl.ds(r0, R), :]
            dval = d * (gate * sig)
            dgate = d * val * (sig * (1.0 + gate * (1.0 - sig)))
            sv[pl.ds(r0, R), :] = dval
            sg[pl.ds(r0, R), :] = dgate
            terms = (dval * v1, dval * v0, dval * v2, dval, dgate * g1, dgate * g0, dgate * g2, dgate)
            return tuple(a + _colsum(t) for a, t in zip(acc, terms))

        acc = lax.fori_loop(0, T // R, pass_a, (zero,) * 8)
        for k in range(3):
            dwv[k:k + 1, :] = acc[k]
            dwg[k:k + 1, :] = acc[4 + k]
        dbv[...] = acc[3]
        dbg[...] = acc[7]

        def pass_b(c, carry):
            v1, v0, v2, r0 = _conv_taps(sv, c, R, T)
            g1, g0, g2, _ = _conv_taps(sg, c, R, T)
            dup[0, pl.ds(r0, R), :] = (v2 * wv[0:1, :] + v0 * wv[1:2, :] + v1 * wv[2:3, :]).astype(dup.dtype)
            dup[1, pl.ds(r0, R), :] = (g2 * wg[0:1, :] + g0 * wg[1:2, :] + g1 * wg[2:3, :]).astype(dup.dtype)
            return carry

        lax.fori_loop(0, T // R, pass_b, 0)

    dup, dwv, dwg, dbv, dbg = pl.pallas_call(
        kern, name="convact_bwd", grid=(nj,),
        in_specs=[lo(T), hi(T), lo(T), lo(3), hi(3), lo(1), hi(1)],
        out_specs=[pl.BlockSpec((2, T, cb), lambda j: (0, 0, j)), lo(3), lo(3), lo(1), lo(1)],
        out_shape=[jax.ShapeDtypeStruct((2, T, F), BF16),
                   jax.ShapeDtypeStruct((3, F), F32), jax.ShapeDtypeStruct((3, F), F32),
                   jax.ShapeDtypeStruct((1, F), F32), jax.ShapeDtypeStruct((1, F), F32)],
        scratch_shapes=[pltpu.VMEM((T, cb), F32), pltpu.VMEM((T, cb), F32)],
        compiler_params=_params(("arbitrary",)),
    )(up, up, dact, conv_w, conv_w, conv_b, conv_b)
    return dup, (dwv, dwg), jnp.concatenate([dbv, dbg], axis=1)


def _peers(x, y, c, with_self=False):
    out = []
    for k in range(0 if with_self else 1, N_DEV):
        px = 1 - x if k & 4 else x
        py = 1 - y if k & 2 else y
        pc = 1 - c if k & 1 else c
        out.append((k, (px, py, pc), 4 * px + 2 * py + pc))
    return out


def _exchange_start(name, bufs, modes, after):
    n = len(bufs)
    ncopy = n * N_DEV
    lands = [lax.empty((N_DEV,) + tuple(b.shape[-2:]), b.dtype) for b in bufs]

    def body(*refs):
        ins, land_in = refs[:n], refs[n:2 * n]
        send_sems, recv_sems = refs[2 * n + 1], refs[2 * n + 2]
        token = refs[-1]
        x, y, c = lax.axis_index("x"), lax.axis_index("y"), lax.axis_index("c")
        me = 4 * x + 2 * y + c
        for b in range(n):
            for k, peer, slot in _peers(x, y, c, with_self=True):
                sem = b * N_DEV + k
                pltpu.make_async_remote_copy(
                    src_ref=ins[b] if modes[b] == 'gather' else ins[b].at[slot], dst_ref=land_in[b].at[me],
                    send_sem=send_sems.at[sem], recv_sem=recv_sems.at[sem],
                    device_id=peer, device_id_type=MESH_ID).start()
        token[...] = jnp.zeros_like(token)

    hbm = pl.BlockSpec(memory_space=pltpu.HBM)
    sem_spec = pl.BlockSpec(memory_space=pltpu.SEMAPHORE)
    operands = [pltpu.with_memory_space_constraint(a, pltpu.HBM) for a in list(bufs) + lands]
    res = pl.pallas_call(
        body, name=name,
        out_shape=(pltpu.SemaphoreType.DMA((ncopy,)), pltpu.SemaphoreType.DMA((ncopy,)),
                   *[pltpu.HBM(a.shape, a.dtype) for a in operands], jax.ShapeDtypeStruct((8, LANES), F32)),
        in_specs=[hbm] * (2 * n) + [pl.BlockSpec(memory_space=pl.ANY)],
        out_specs=(sem_spec, sem_spec, *([hbm] * (2 * n)), pl.BlockSpec(memory_space=pltpu.VMEM)),
        input_output_aliases={i: 2 + i for i in range(2 * n)},
        compiler_params=pltpu.CompilerParams(has_side_effects=pltpu.SideEffectType.DATAFLOW_SIDE_EFFECTING),
    )(*operands, after)
    return (res[0], res[1], list(res[2:2 + n]), list(res[2 + n:2 + 2 * n]), tuple(modes)), res[-1]


def _exchange_wait(name, handle, after):
    send_sems, recv_sems, srcs, lands, modes = handle
    n = len(srcs)

    def body(*refs):
        src_in, land_in = refs[:n], refs[n:2 * n]
        send_ref, recv_ref = refs[2 * n], refs[2 * n + 1]
        x, y, c = lax.axis_index("x"), lax.axis_index("y"), lax.axis_index("c")
        for b in range(n):
            for k, peer, slot in _peers(x, y, c, with_self=True):
                sem = b * N_DEV + k
                copy = pltpu.make_async_remote_copy(
                    src_ref=src_in[b] if modes[b] == 'gather' else src_in[b].at[slot], dst_ref=land_in[b].at[slot],
                    send_sem=send_ref.at[sem], recv_sem=recv_ref.at[sem],
                    device_id=peer, device_id_type=MESH_ID)
                copy.wait_send()
                copy.wait_recv()

    hbm = pl.BlockSpec(memory_space=pltpu.HBM)
    sem_spec = pl.BlockSpec(memory_space=pltpu.SEMAPHORE)
    res = pl.pallas_call(
        body, name=name,
        out_shape=tuple(pltpu.HBM(a.shape, a.dtype) for a in srcs + lands),
        in_specs=[hbm] * (2 * n) + [sem_spec, sem_spec, pl.BlockSpec(memory_space=pl.ANY)],
        out_specs=tuple([hbm] * (2 * n)),
        input_output_aliases={i: i for i in range(2 * n)},
        compiler_params=pltpu.CompilerParams(has_side_effects=pltpu.SideEffectType.DATAFLOW_SIDE_EFFECTING),
    )(*srcs, *lands, send_sems, recv_sems, after)
    return list(res[n:])


def _adamw(w, g, m, v):
    m2 = ADAM_B1 * m + (1.0 - ADAM_B1) * g
    v2 = ADAM_B2 * v + (1.0 - ADAM_B2) * (g * g)
    m_hat = m2 / (1.0 - ADAM_B1 ** ADAM_STEP)
    v_hat = v2 / (1.0 - ADAM_B2 ** ADAM_STEP)
    return -ADAM_LR * (m_hat / (jnp.sqrt(v_hat) + ADAM_EPS) + ADAM_WD * w), m2, v2


def _sum_adamw(name, landed, w, m, v):
    R, C = w.shape
    row_bytes = N_DEV * (-(-C // LANES) * LANES) * landed.dtype.itemsize
    tiles = [d for d in range(16, R, 16) if R % d == 0 and d * row_bytes <= ADAM_TILE_BYTES]
    tr = max(tiles) if tiles and R * row_bytes > ADAM_TILE_BYTES else R

    def kern(x_ref, w_ref, m_ref, v_ref, g_out, d_out, m_out, v_out):
        g = x_ref[0].astype(F32)
        for q in range(1, N_DEV):
            g = g + x_ref[q].astype(F32)
        g_out[...] = g
        d_out[...], m_out[...], v_out[...] = _adamw(w_ref[...], g, m_ref[...], v_ref[...])

    blk = pl.BlockSpec((tr, C), lambda i: (i, 0))
    return pl.pallas_call(
        kern, name=name, grid=(R // tr,),
        in_specs=[pl.BlockSpec((N_DEV, tr, C), lambda i: (0, i, 0)), blk, blk, blk],
        out_specs=[blk] * 4, out_shape=[jax.ShapeDtypeStruct((R, C), F32)] * 4,
        compiler_params=_params(("arbitrary",)),
    )(landed, w, m, v)


def _sum_adamw_many(name, items):
    n = len(items)

    def kern(*refs):
        ins, outs = refs[:4 * n], refs[4 * n:]
        for k in range(n):
            x_ref, w_ref, m_ref, v_ref = ins[4 * k:4 * k + 4]
            g = x_ref[0].astype(F32)
            for q in range(1, N_DEV):
                g = g + x_ref[q].astype(F32)
            outs[4 * k][...] = g
            outs[4 * k + 1][...], outs[4 * k + 2][...], outs[4 * k + 3][...] = _adamw(w_ref[...], g, m_ref[...], v_ref[...])

    out_shape = [jax.ShapeDtypeStruct(w.shape, F32) for _, w, _, _ in items for _ in range(4)]
    res = pl.pallas_call(kern, name=name, out_shape=out_shape)(*[a for item in items for a in item])
    return [res[4 * k:4 * k + 4] for k in range(n)]


def _sum_slots(name, landed):
    def kern(x_ref, o_ref):
        g = x_ref[0].astype(F32)
        for q in range(1, N_DEV):
            g = g + x_ref[q].astype(F32)
        o_ref[...] = g

    return pl.pallas_call(kern, name=name, out_shape=jax.ShapeDtypeStruct(landed.shape[1:], F32))(landed)


def _adamw_whole(name, g, w, m, v):
    def kern(g_ref, w_ref, m_ref, v_ref, d_out, m_out, v_out):
        d_out[...], m_out[...], v_out[...] = _adamw(w_ref[...], g_ref[...], m_ref[...], v_ref[...])

    return pl.pallas_call(kern, name=name, out_shape=[jax.ShapeDtypeStruct(w.shape, F32)] * 3)(g, w, m, v)


def _view2d(a):
    if a.ndim == 1:
        return a.reshape(1, -1)
    return a.reshape(-1, a.shape[-1])


def kernel(x, norm_mix_g, w_in, pool_w, pool_scale, ssm_log_neg_a_re, ssm_a_im, ssm_log_dt, ssm_b_re, ssm_b_im, ssm_c_re, ssm_c_im, ssm_d, glu_w, glu_b, out_norm_pool_g, out_norm_ssm_g, w_out, norm_ffn_g, w_up, conv_w, conv_b, w_down, final_norm_g, loss_target, m_norm_mix_g, m_w_in, m_pool_w, m_pool_scale, m_ssm_log_neg_a_re, m_ssm_a_im, m_ssm_log_dt, m_ssm_b_re, m_ssm_b_im, m_ssm_c_re, m_ssm_c_im, m_ssm_d, m_glu_w, m_glu_b, m_out_norm_pool_g, m_out_norm_ssm_g, m_w_out, m_norm_ffn_g, m_w_up, m_conv_w, m_conv_b, m_w_down, m_final_norm_g, v_norm_mix_g, v_w_in, v_pool_w, v_pool_scale, v_ssm_log_neg_a_re, v_ssm_a_im, v_ssm_log_dt, v_ssm_b_re, v_ssm_b_im, v_ssm_c_re, v_ssm_c_im, v_ssm_d, v_glu_w, v_glu_b, v_out_norm_pool_g, v_out_norm_ssm_g, v_w_out, v_norm_ffn_g, v_w_up, v_conv_w, v_conv_b, v_w_down, v_final_norm_g):
    given = dict(locals())
    weights = {n: given[n] for n in WEIGHTS}
    mom1 = {n: given["m_" + n] for n in WEIGHTS}
    mom2 = {n: given["v_" + n] for n in WEIGHTS}

    xs = x[0]
    tgt = loss_target[0]
    T, D = xs.shape
    DP = len(POOL_WINDOWS) * POOL_GROUP
    DS = D - DP
    G = DS // SSM_GROUP
    N, H = SSM_STATE, SSM_GROUP
    F2 = w_up.shape[1] * N_DEV
    F = F2 // 2
    fs = w_up.shape[1]
    row = lambda a: a.reshape(1, -1)

    early_handle, token = _exchange_start("gather_early_start", [w_in.astype(BF16), glu_w.astype(BF16)],
                                          ['gather'] * 2, after=w_in)
    late_src = [w_out.astype(BF16), w_down.astype(BF16), w_up.T.astype(BF16), conv_w]
    late_handle, token = _exchange_start("gather_late_start", late_src, ['gather'] * 4, after=token)
    CB = row(conv_b)

    g1, g2, g3 = row(norm_mix_g), row(norm_ffn_g), row(final_norm_g)
    gp, gs = row(out_norm_pool_g), row(out_norm_ssm_g)
    (xn,) = _rowmap("norm_mix", _rms_fwd, [xs], [g1 + token[0:1, 0:1]], [(D, BF16)])
    early = _exchange_wait("gather_early_wait", early_handle, xn)
    W_in = early[0].reshape(D, D)
    W_glu = early[1].reshape(DS, DS)
    u = _mm("proj_in", xn, W_in, 'nn', F32)
    tmp = _pick(T, (256, 128))
    pool_w_bf = pool_w.astype(BF16)
    ypn = _pool_fwd(u, pool_w_bf, row(pool_scale), gp, tmp)

    u_ssm = _to_chunk_rows(u[:, DP:])
    u_ssm_bf = u_ssm.astype(BF16)
    a_rows = (ssm_log_neg_a_re.reshape(2 * G, N), ssm_a_im.reshape(2 * G, N), ssm_log_dt.reshape(2 * G, 1))
    disc = _ssm_params_fwd(*a_rows)
    nb = G // SSM_BLOCK_GROUPS
    vecs = [d.reshape(2, nb, 1, SSM_BLOCK_GROUPS * N) for d in disc]
    mats = [_groups_side_by_side(ssm_b_re, 3), _groups_side_by_side(ssm_b_im, 3),
            _groups_side_by_side(ssm_c_re, 2), _groups_side_by_side(ssm_c_im, 2)]
    y_dir = [_ssm_fwd("ssm_fwd_%d" % d, u_ssm_bf, mats, vecs, d) for d in range(2)]

    def mix_post(yf, yb, us, d, gw, gb, g):
        y = yf + yb + d * us
        z = _gelu(y)
        gate = _sigmoid(jnp.dot(z.astype(BF16), gw, preferred_element_type=F32) + gb)
        return _rms_fwd(z * gate, g), y

    ysn, y_ssm = _rowmap("ssm_post", mix_post, [y_dir[0], y_dir[1], u_ssm], [row(ssm_d), W_glu, row(glu_b), gs],
                         [(DS, BF16), (DS, F32)])
    ycat = jnp.concatenate([ypn, _from_chunk_rows(ysn)], axis=1)
    late = _exchange_wait("gather_late_wait", late_handle, ysn)
    W_out = late[0].reshape(D, D)
    W_down = late[1].reshape(F, D)
    W_up_t = late[2].reshape(F2, D)
    CW = late[3].transpose(1, 0, 2).reshape(3, F2)
    h1, hn = _mm_rows("proj_out", ycat, W_out, 'nn', lambda h, g: (h, _rms_fwd(h, g)), [], [g2],
                      [(D, F32), (D, BF16)], add=xs)
    up = _mm("ffn_up", hn, W_up_t, 'nt', F32)
    act = _convact_fwd(up, CW, CB, F)

    def head(h, t, g):
        r = lax.rsqrt(_rowmean(h * h) + EPS)
        hh = h * r
        e = hh * g - t
        loss = 0.5 * jnp.sum(_rowmean(e * e), keepdims=True)
        dy = e * (1.0 / D)
        dxh = dy * g
        dh = r * (dxh - hh * _rowmean(dxh * hh))
        return dh, dh, jnp.broadcast_to(loss, (1, LANES)), _colsum(dy * hh)

    dh2, dh2_bf, loss_acc, dg3 = _mm_rows("ffn_down", act, W_down, 'nn', head, [tgt], [g3],
                                          [(D, F32), (D, BF16)], [(1, LANES), (1, D)], add=h1)

    dact = _mm("ffn_down_dx", dh2_bf, W_down, 'nt', F32)
    dW_down = _mm("ffn_down_dw", act, dh2_bf, 'tn', BF16)
    dup, dCW, dCB = _convact_bwd(up, dact, CW, CB, F)
    dhn_val = _mm("ffn_up_dx_val", dup, W_up_t, 'nn', F32, b_rows=(2, 0), a_plane=0)
    dW_up_t = _mm("ffn_up_dw", dup, hn, 'tn', BF16).reshape(F2, D)

    def shards(full_grad):
        return full_grad.reshape((N_DEV, full_grad.shape[0] // N_DEV) + full_grad.shape[1:])

    half = N_DEV // 2
    dCW_sh = jnp.concatenate([h.reshape(3, half, fs).transpose(1, 0, 2) for h in dCW], axis=0)
    ffn_sent = [shards(dW_down), shards(dW_up_t), dCW_sh]
    ffn_handle, token = _exchange_start("scatter_ffn_start", ffn_sent, ['scatter'] * 3, after=dhn_val)
    g2_late = g2 + token[0:1, 0:1]

    def norm_bwd_add(dy, hx, res, g):
        dx, dg = _rms_bwd(hx, g, dy)
        return dx + res, dg

    def norm_bwd_add2(dy, hx, res, g):
        dx, dg = _rms_bwd(hx, g, dy)
        return dx + res, dx + res, dg

    dh1, dh1_bf, dg2 = _mm_rows("ffn_up_dx_gate", dup, W_up_t, 'nn', norm_bwd_add2, [h1, dh2], [g2_late],
                                [(D, F32), (D, BF16)], [(1, D)], add=dhn_val, b_rows=(2, 1), a_plane=1)
    dycat = _mm("proj_out_dx", dh1_bf, W_out, 'nt', F32)
    dW_out = _mm("proj_out_dw", ycat, dh1_bf, 'tn', BF16)

    out_sent = [shards(dW_out)]
    out_handle, token = _exchange_start("scatter_out_start", out_sent, ['scatter'], after=dycat)
    q, dpooled, dpool_w, dpool_scale, dgp = _pool_bwd_rows(u, dycat, pool_w_bf, row(pool_scale),
                                                           gp + token[0:1, 0:1], tmp)
    du_pool = _pool_bwd_band(q, dpooled, tmp)

    def mix_post_bwd(y, us, dyn, d, gw, gb, g):
        z = _gelu(y)
        gz = z.astype(BF16)
        sig = _sigmoid(jnp.dot(gz, gw, preferred_element_type=F32) + gb)
        dys, dg = _rms_bwd(z * sig, g, dyn)
        dgl = dys * z * sig * (1.0 - sig)
        dgl_bf = dgl.astype(BF16)
        dz = dys * sig + lax.dot_general(dgl_bf, gw, (((1,), (1,)), ((), ())), preferred_element_type=F32)
        dgw = lax.dot_general(gz, dgl_bf, (((0,), (0,)), ((), ())), preferred_element_type=F32)
        dyv = dz * _gelu_grad(y)
        return dyv, dyv * d, dgw, _colsum(dgl), _colsum(dyv * us), dg

    dyv, du_dir, dglu_w, dglu_b, dssm_d, dgs = _rowmap(
        "ssm_post_bwd", mix_post_bwd, [y_ssm, u_ssm, _to_chunk_rows(dycat[:, DP:])], [row(ssm_d), W_glu, row(glu_b), gs],
        [(DS, BF16), (DS, F32)], [(DS, DS), (1, DS), (1, DS), (1, DS)])

    du_ssm = du_dir
    dB, dC, dvec = [], [], []
    for d in range(2):
        res = _ssm_bwd("ssm_bwd_%d" % d, u_ssm_bf, dyv, du_ssm, mats, vecs, d)
        du_ssm = res[0]
        dB.append(res[1:3])
        dC.append(res[3:5])
        dvec.append(res[5:9])
    grad_b = [_groups_apart(jnp.stack([dB[0][k], dB[1][k]]), 3) for k in range(2)]
    grad_c = [_groups_apart(jnp.stack([dC[0][k], dC[1][k]]), 2) for k in range(2)]
    cots = [jnp.stack([dvec[0][k], dvec[1][k]]).reshape(2 * G, N) for k in range(4)]
    d_a_re, d_a_im, d_log_dt = _ssm_params_bwd(*a_rows, cots)

    rep_grads = {
        'pool_w': dpool_w, 'pool_scale': dpool_scale, 'ssm_log_neg_a_re': d_a_re, 'ssm_a_im': d_a_im,
        'ssm_log_dt': d_log_dt, 'ssm_b_re': grad_b[0], 'ssm_b_im': grad_b[1], 'ssm_c_re': grad_c[0], 'ssm_c_im': grad_c[1],
        'ssm_d': dssm_d, 'glu_b': dglu_b, 'out_norm_pool_g': dgp, 'out_norm_ssm_g': dgs, 'norm_ffn_g': dg2,
        'conv_b': dCB, 'final_norm_g': dg3}
    wide = ('pool_w', 'ssm_b_re', 'ssm_b_im', 'ssm_c_re', 'ssm_c_im')

    def travel(n):
        g = rep_grads[n]
        if n in wide:
            return g.reshape(-1, PACK_W if g.size % PACK_W == 0 else LANES).astype(BF16)
        return _view2d(g.reshape(weights[n].shape))

    small = tuple(n for n in REPLICATED if n != 'norm_mix_g')
    small_sent = [shards(dglu_w.astype(BF16))] + [travel(n) for n in small] + [loss_acc]
    small_handle, token = _exchange_start("reduce_small_start", small_sent, ['scatter'] + ['gather'] * (len(small) + 1),
                                          after=d_a_re)

    du = (jnp.concatenate([du_pool, _from_chunk_rows(du_ssm)], axis=1) + token[0:1, 0:1]).astype(BF16)
    dW_in = _mm("proj_in_dw", xn, du, 'tn', BF16)
    in_handle, token = _exchange_start("scatter_in_start", [shards(dW_in)], ['scatter'], after=dW_in)
    dx, dg1 = _mm_rows("proj_in_dx", du, W_in + token[0, 0].astype(BF16), 'nt', norm_bwd_add, [xs, dh1], [g1],
                       [(D, F32)], [(1, D)])
    rep_grads['norm_mix_g'] = dg1
    last_handle, token = _exchange_start("reduce_last_start", [travel('norm_mix_g')], ['gather'], after=dx)

    grads, delta, new_m, new_v = {}, {}, {}, {}

    def update(n, landed_n):
        shape = weights[n].shape
        if len(shape) > 2 and shape[-1] < LANES // 2:
            grads[n] = _sum_slots("sum_" + n, landed_n).reshape(shape)
            delta[n], new_m[n], new_v[n] = _adamw_whole("adamw_" + n, grads[n], weights[n], mom1[n], mom2[n])
            return
        if n == 'w_up':
            res = _sum_adamw("adamw_" + n, landed_n, weights[n].T, mom1[n].T, mom2[n].T)
            grads[n], delta[n], new_m[n], new_v[n] = [r.T for r in res]
            return
        landed_n = landed_n.reshape((N_DEV,) + _view2d(weights[n]).shape)
        res = _sum_adamw("adamw_" + n, landed_n, _view2d(weights[n]), _view2d(mom1[n]), _view2d(mom2[n]))
        grads[n], delta[n], new_m[n], new_v[n] = [r.reshape(shape) for r in res]

    def update_many(name, named_landed):
        items = [(l.reshape((N_DEV,) + _view2d(weights[n]).shape), _view2d(weights[n]), _view2d(mom1[n]), _view2d(mom2[n]))
                 for n, l in named_landed]
        for (n, _), res in zip(named_landed, _sum_adamw_many(name, items)):
            grads[n], delta[n], new_m[n], new_v[n] = [r.reshape(weights[n].shape) for r in res]

    ffn_landed = _exchange_wait("scatter_ffn_wait", ffn_handle, token)
    for n, l in zip(('w_down', 'w_up'), ffn_landed):
        update(n, l)
    (out_landed,) = _exchange_wait("scatter_out_wait", out_handle, token)
    update('w_out', out_landed)
    small_landed = _exchange_wait("reduce_small_wait", small_handle, token)
    tiny = [('conv_w', ffn_landed[2])]
    for n, l in zip(('glu_w',) + small, small_landed):
        if weights[n].size <= ADAM_TINY:
            tiny.append((n, l))
        else:
            update(n, l)
    update_many("adamw_tiny", tiny)

    def loss_sum(parts_ref, o_ref):
        s = parts_ref[0]
        for q in range(1, N_DEV):
            s = s + parts_ref[q]
        o_ref[...] = s

    loss = pl.pallas_call(loss_sum, name="loss_sum", out_shape=jax.ShapeDtypeStruct((1, LANES), F32))(small_landed[-1])[0, 0]

    update('w_in', _exchange_wait("scatter_in_wait", in_handle, grads[small[-1]])[0])
    update('norm_mix_g', _exchange_wait("reduce_last_wait", last_handle, grads['w_in'])[0])

    return (loss, dx[None], *[grads[n] for n in WEIGHTS], *[delta[n] for n in WEIGHTS],
            *[new_m[n] for n in WEIGHTS], *[new_v[n] for n in WEIGHTS])
```

```python
import functools

import jax
import jax.numpy as jnp
import numpy as np
from jax import lax
from jax.experimental import pallas as pl
from jax.experimental.pallas import tpu as pltpu

F32 = jnp.float32
BF16 = jnp.bfloat16
MESH_ID = pl.DeviceIdType.MESH

N_DEV = 8
EPS = 1e-6
POOL_WINDOWS = (2, 4, 8, 16)
POOL_GROUP = 128
POOL_PAD = 64
SSM_GROUP = 16
SSM_STATE = 64
SSM_BLOCK_GROUPS = 8
N_CHUNK = 8
ROW_BLOCK = 512
SCAN_UNROLL = 4
GROUP_SHIFT = 4
STATE_SHIFT = 6
_NT = (((1,), (1,)), ((), ()))
_TN = (((0,), (0,)), ((), ()))
LANES = 128
PACK_W = 1024
VMEM_LIMIT = 56 * 1024 * 1024
MM_VMEM_BUDGET = 40 * 1024 * 1024
MM_TILE_CAP = 1408
MM_ROWS_TILE = 512
ADAM_TILE_BYTES = 4 * 1024 * 1024
ADAM_TINY = 8192

ADAM_LR = 0.001
ADAM_B1 = 0.9
ADAM_B2 = 0.999
ADAM_EPS = 1e-08
ADAM_WD = 0.01
ADAM_STEP = 10

WEIGHTS = ['norm_mix_g', 'w_in', 'pool_w', 'pool_scale', 'ssm_log_neg_a_re', 'ssm_a_im', 'ssm_log_dt',
           'ssm_b_re', 'ssm_b_im', 'ssm_c_re', 'ssm_c_im', 'ssm_d', 'glu_w', 'glu_b', 'out_norm_pool_g',
           'out_norm_ssm_g', 'w_out', 'norm_ffn_g', 'w_up', 'conv_w', 'conv_b', 'w_down', 'final_norm_g']
SHARDED = ('w_in', 'w_out', 'w_down', 'glu_w', 'w_up', 'conv_w')
REPLICATED = tuple(n for n in WEIGHTS if n not in SHARDED)


def _pick(n, prefs):
    for p in prefs:
        if n % p == 0:
            return p
    return n


def _params(sem, vmem=None):
    return pltpu.CompilerParams(dimension_semantics=sem, vmem_limit_bytes=vmem or VMEM_LIMIT)


def _tiles(n, cap):
    return [d for d in range(LANES, min(n, cap) + 1, LANES) if n % d == 0] or [n]


def _mm(name, a, b, mode, out_dtype, add=None, b_rows=(1, 0), a_plane=None):
    batch = a.shape[0] if a.ndim == 3 and a_plane is None else None
    if mode == 'nn':
        (M, K), N = a.shape[-2:], b.shape[1]
        assert b.shape[0] == K * b_rows[0]
    elif mode == 'nt':
        (M, K), (N, _) = a.shape[-2:], b.shape
    else:
        (K, M), (_, N) = a.shape[-2:], b.shape
    dims = {'nn': (((1,), (0,)), ((), ())), 'nt': (((1,), (1,)), ((), ())), 'tn': (((0,), (0,)), ((), ()))}[mode]
    sa, sb, so = a.dtype.itemsize, b.dtype.itemsize, jnp.dtype(out_dtype).itemsize
    best = None
    for tm in _tiles(M, MM_TILE_CAP):
        for tn in _tiles(N, MM_TILE_CAP):
            need = 2 * (tm * K * sa + tn * K * sb + tm * tn * (so + (4 if add is not None else 0)))
            if need <= MM_VMEM_BUDGET:
                key = (tm * tn / (tm + tn), tm * tn)
                if best is None or key > best[0]:
                    best = (key, tm, tn)
    _, tm, tn = best
    rows_inner = a.size * sa * (N // tn) + b.size * sb < a.size * sa + b.size * sb * (M // tm)

    def kern(*refs):
        a_ref, b_ref = refs[:2]
        o_ref = refs[-1]
        r = lax.dot_general(a_ref[...].astype(BF16), b_ref[...].astype(BF16), dims, preferred_element_type=F32)
        if add is not None:
            r = r + refs[2][...]
        o_ref[...] = r.astype(o_ref.dtype)

    if rows_inner:
        grid = (N // tn, M // tm)
        ij = lambda g0, g1: (g1, g0)
    else:
        grid = (M // tm, N // tn)
        ij = lambda g0, g1: (g0, g1)

    def spec(block, index, batched=False):
        if batch is None:
            return pl.BlockSpec(block, lambda g0, g1: index(*ij(g0, g1)))
        if batched:
            return pl.BlockSpec((None,) + block, lambda p, g0, g1: (p,) + index(*ij(g0, g1)))
        return pl.BlockSpec(block, lambda p, g0, g1: index(*ij(g0, g1)))

    if a_plane is not None:
        assert mode != 'tn'
        a_spec = pl.BlockSpec((None, tm, K), lambda g0, g1: (a_plane, ij(g0, g1)[0], 0))
    else:
        a_spec = (spec((K, tm), lambda i, j: (0, i), True) if mode == 'tn' else spec((tm, K), lambda i, j: (i, 0), True))
    b_spec = (spec((tn, K), lambda i, j: (j, 0)) if mode == 'nt' else spec((K, tn), lambda i, j: (b_rows[1], j)))
    in_specs = [a_spec, b_spec]
    args = [a, b]
    if add is not None:
        in_specs.append(spec((tm, tn), lambda i, j: (i, j), True))
        args.append(add)
    lead = () if batch is None else (batch,)
    return pl.pallas_call(
        kern, name=name, grid=lead + grid, in_specs=in_specs,
        out_specs=spec((tm, tn), lambda i, j: (i, j), True),
        out_shape=jax.ShapeDtypeStruct(lead + (M, N), out_dtype),
        compiler_params=_params(("parallel",) * (len(lead) + 2)),
    )(*args)


def _store_rows_and_sums(i, outs, res, n_row):
    for o, v in zip(outs[:n_row], res[:n_row]):
        o[...] = v.astype(o.dtype)
    for o, v in zip(outs[n_row:], res[n_row:]):
        @pl.when(i == 0)
        def _(o=o, v=v):
            o[...] = v

        @pl.when(i != 0)
        def _(o=o, v=v):
            o[...] += v


def _mm_rows(name, a, b, mode, fn, rows, fulls, out_rows, out_accs=(), add=None, b_rows=(1, 0), a_plane=None):
    M, K = a.shape[-2:]
    N = b.shape[1] if mode == 'nn' else b.shape[0]
    assert b.shape[0] == (K * b_rows[0] if mode == 'nn' else N)
    dims = _NT if mode == 'nt' else (((1,), (0,)), ((), ()))
    tm = min(MM_ROWS_TILE, M)
    n_extra, n_row = len(rows) + len(fulls), len(out_rows)

    def kern(*refs):
        i = pl.program_id(0)
        r = lax.dot_general(refs[0][...].astype(BF16), refs[1][...].astype(BF16), dims, preferred_element_type=F32)
        pos = 2
        if add is not None:
            r = r + refs[pos][...]
            pos += 1
        res = fn(r, *[x[...] for x in refs[pos:pos + n_extra]])
        _store_rows_and_sums(i, refs[pos + n_extra:], res, n_row)

    def full_spec(shape):
        nd = len(shape)
        return pl.BlockSpec(tuple(shape), lambda i: (0,) * nd)

    a_spec = (pl.BlockSpec((tm, K), lambda i: (i, 0)) if a_plane is None
              else pl.BlockSpec((None, tm, K), lambda i: (a_plane, i, 0)))
    b_spec = pl.BlockSpec((K, N), lambda i: (b_rows[1], 0)) if mode == 'nn' else pl.BlockSpec((N, K), lambda i: (0, 0))
    ins = [_row_in(r, tm, M) for r in ([add] if add is not None else []) + list(rows)]
    outs = [_row_out(o, tm, M) for o in out_rows]
    res = pl.pallas_call(
        kern, name=name, grid=(M // tm,),
        in_specs=[a_spec, b_spec] + [s for s, _ in ins] + [full_spec(f.shape) for f in fulls],
        out_specs=[s for s, _ in outs] + [full_spec(s) for s in out_accs],
        out_shape=[sh for _, sh in outs] + [jax.ShapeDtypeStruct(tuple(s), F32) for s in out_accs],
        compiler_params=_params(("arbitrary",)),
    )(a, b, *[x for _, x in ins], *fulls)
    return res


def _row_in(r, tm, T):
    arr, w, k = r if isinstance(r, tuple) else (r, r.shape[1], 0)
    return pl.BlockSpec((tm, w), lambda i: (i, k)), arr


def _row_out(o, tm, T):
    c, dt = o
    return pl.BlockSpec((tm, c), lambda i: (i, 0)), jax.ShapeDtypeStruct((T, c), dt)


def _rowmap(name, fn, rows, fulls, out_rows, out_accs=(), tm=512):
    T = (rows[0][0] if isinstance(rows[0], tuple) else rows[0]).shape[0]
    tm = min(tm, T)
    assert T % tm == 0
    n_in, n_row = len(rows) + len(fulls), len(out_rows)

    def kern(*refs):
        i = pl.program_id(0)
        res = fn(*[r[...] for r in refs[:n_in]])
        res = res if isinstance(res, (tuple, list)) else (res,)
        _store_rows_and_sums(i, refs[n_in:], res, n_row)

    def full_spec(shape):
        nd = len(shape)
        return pl.BlockSpec(tuple(shape), lambda i: (0,) * nd)

    ins = [_row_in(r, tm, T) for r in rows]
    outs = [_row_out(o, tm, T) for o in out_rows]
    res = pl.pallas_call(
        kern, name=name, grid=(T // tm,),
        in_specs=[s for s, _ in ins] + [full_spec(f.shape) for f in fulls],
        out_specs=[s for s, _ in outs] + [full_spec(s) for s in out_accs],
        out_shape=[sh for _, sh in outs] + [jax.ShapeDtypeStruct(tuple(s), F32) for s in out_accs],
        compiler_params=_params(("arbitrary",)),
    )(*[x for _, x in ins], *fulls)
    return res


def _colsum(v):
    return jnp.sum(v, axis=0, keepdims=True)


def _rowmean(v):
    return jnp.mean(v, axis=-1, keepdims=True)


def _rms_fwd(x, g):
    r = lax.rsqrt(_rowmean(x * x) + EPS)
    return x * r * g


def _rms_bwd(x, g, dy):
    r = lax.rsqrt(_rowmean(x * x) + EPS)
    xh = x * r
    dxh = dy * g
    return r * (dxh - xh * _rowmean(dxh * xh)), _colsum(dy * xh)


def _sigmoid(v):
    return 0.5 * jnp.tanh(0.5 * v) + 0.5


def _gelu(y):
    c = np.sqrt(2.0 / np.pi).astype(np.float32)
    return 0.5 * y * (1.0 + jnp.tanh(c * (y + 0.044715 * (y * y * y))))


def _gelu_grad(y):
    c = np.sqrt(2.0 / np.pi).astype(np.float32)
    th = jnp.tanh(c * (y + 0.044715 * (y * y * y)))
    return 0.5 * (1.0 + th) + 0.5 * y * (1.0 - th * th) * c * (1.0 + 3.0 * 0.044715 * (y * y))


def _split3(v):
    hi = v.astype(BF16)
    r1 = v - hi.astype(F32)
    mid = r1.astype(BF16)
    lo = (r1 - mid.astype(F32)).astype(BF16)
    return hi, mid, lo


def _band(tm, lo_off, hi_off):
    shape = (tm, tm + 2 * POOL_PAD)
    d = lax.broadcasted_iota(jnp.int32, shape, 1) - lax.broadcasted_iota(jnp.int32, shape, 0) - POOL_PAD
    return ((d >= lo_off) & (d < hi_off)).astype(BF16)


def _band_sum(band, ext):
    hi, mid, lo = _split3(ext)
    dot = functools.partial(jnp.dot, preferred_element_type=F32)
    return dot(band, hi) + dot(band, mid) + dot(band, lo)


def _window_count(r0, tm, half, T):
    t = r0 + lax.broadcasted_iota(jnp.int32, (tm, 1), 0)
    return (jnp.minimum(t + half, T) - jnp.maximum(t - half, 0)).astype(F32)


def _halo_specs(tm, C, T):
    per = tm // POOL_PAD
    last = T // POOL_PAD - 1
    return [pl.BlockSpec((POOL_PAD, C), lambda i: (jnp.maximum(i * per - 1, 0), 0)),
            pl.BlockSpec((tm, C), lambda i: (i, 0)),
            pl.BlockSpec((POOL_PAD, C), lambda i: (jnp.minimum((i + 1) * per, last), 0))]


def _with_halo(before_ref, main_ref, after_ref, i, n):
    before = jnp.where(i > 0, before_ref[...], 0.0)
    after = jnp.where(i < n - 1, after_ref[...], 0.0)
    return jnp.concatenate([before, main_ref[...], after], axis=0)


def _pool_block(ext, ctr, w_ref, r0, tm, T):
    pooled, conc = [], []
    for gi, w in enumerate(POOL_WINDOWS):
        half = w // 2
        cols = slice(gi * POOL_GROUP, (gi + 1) * POOL_GROUP)
        ws = _band_sum(_band(tm, -half, half), ext[:, cols])
        p = ws / _window_count(r0, tm, half, T) - ctr[:, cols]
        pooled.append(p)
        conc.append(jnp.dot(p.astype(BF16), w_ref[gi], preferred_element_type=F32))
    return pooled, jnp.concatenate(conc, axis=1)


def _pool_fwd(u, pool_w_bf, pool_scale, g_pool, tm):
    T = u.shape[0]
    nw = len(POOL_WINDOWS)
    C = nw * POOL_GROUP
    n = T // tm

    def kern(ub_ref, u_ref, ua_ref, w_ref, sc_ref, g_ref, o_ref):
        i = pl.program_id(0)
        r0 = pl.multiple_of(i * tm, tm)
        _, conc = _pool_block(_with_halo(ub_ref, u_ref, ua_ref, i, n), u_ref[...], w_ref, r0, tm, T)
        o_ref[...] = _rms_fwd(conc * sc_ref[...], g_ref[...]).astype(o_ref.dtype)

    return pl.pallas_call(
        kern, name="pool_fwd", grid=(n,),
        in_specs=_halo_specs(tm, C, T) + [pl.BlockSpec(pool_w_bf.shape, lambda i: (0, 0, 0)),
                                          pl.BlockSpec((1, C), lambda i: (0, 0)), pl.BlockSpec((1, C), lambda i: (0, 0))],
        out_specs=pl.BlockSpec((tm, C), lambda i: (i, 0)),
        out_shape=jax.ShapeDtypeStruct((T, C), BF16),
        compiler_params=_params(("arbitrary",)),
    )(u, u, u, pool_w_bf, pool_scale, g_pool)


def _pool_bwd_rows(u, d_y, pool_w_bf, pool_scale, g_pool, tm):
    T = u.shape[0]
    nw = len(POOL_WINDOWS)
    C = nw * POOL_GROUP
    n = T // tm

    def kern(ub_ref, u_ref, ua_ref, dy_ref, w_ref, sc_ref, g_ref, q_ref, dp_ref, dw_ref, dsc_ref, dg_ref):
        i = pl.program_id(0)
        r0 = pl.multiple_of(i * tm, tm)
        pooled, conc = _pool_block(_with_halo(ub_ref, u_ref, ua_ref, i, n), u_ref[...], w_ref, r0, tm, T)
        sc = sc_ref[...]
        dyp, dg = _rms_bwd(conc * sc, g_ref[...], dy_ref[...])
        dsc = _colsum(dyp * conc)
        dconc = (dyp * sc).astype(BF16)
        dws = []
        for gi, w in enumerate(POOL_WINDOWS):
            cols = slice(gi * POOL_GROUP, (gi + 1) * POOL_GROUP)
            dc = dconc[:, cols]
            dp = lax.dot_general(dc, w_ref[gi], (((1,), (1,)), ((), ())), preferred_element_type=F32)
            dp_ref[:, cols] = dp
            q_ref[:, cols] = dp / _window_count(r0, tm, w // 2, T)
            dws.append(lax.dot_general(pooled[gi].astype(BF16), dc, (((0,), (0,)), ((), ())), preferred_element_type=F32))

        @pl.when(i == 0)
        def _():
            for gi in range(nw):
                dw_ref[gi] = dws[gi]
            dsc_ref[...] = dsc
            dg_ref[...] = dg

        @pl.when(i != 0)
        def _():
            for gi in range(nw):
                dw_ref[gi] += dws[gi]
            dsc_ref[...] += dsc
            dg_ref[...] += dg

    full2 = pl.BlockSpec((1, C), lambda i: (0, 0))
    row_blk = pl.BlockSpec((tm, C), lambda i: (i, 0))
    return pl.pallas_call(
        kern, name="pool_bwd_rows", grid=(n,),
        in_specs=_halo_specs(tm, C, T) + [row_blk, pl.BlockSpec(pool_w_bf.shape, lambda i: (0, 0, 0)), full2, full2],
        out_specs=[row_blk, row_blk, pl.BlockSpec((nw, POOL_GROUP, POOL_GROUP), lambda i: (0, 0, 0)), full2, full2],
        out_shape=[jax.ShapeDtypeStruct((T, C), F32), jax.ShapeDtypeStruct((T, C), F32),
                   jax.ShapeDtypeStruct((nw, POOL_GROUP, POOL_GROUP), F32),
                   jax.ShapeDtypeStruct((1, C), F32), jax.ShapeDtypeStruct((1, C), F32)],
        compiler_params=_params(("arbitrary",)),
    )(u, u, u, d_y, pool_w_bf, pool_scale, g_pool)


def _pool_bwd_band(q, dpooled, du_other, after, tm):
    T, C = dpooled.shape
    C2 = du_other.shape[1]
    n = T // tm

    def kern(qb_ref, q_ref, qa_ref, dp_ref, other_ref, after_ref, o_ref):
        ext = _with_halo(qb_ref, q_ref, qa_ref, pl.program_id(0), n)
        for gi, w in enumerate(POOL_WINDOWS):
            half = w // 2
            cols = slice(gi * POOL_GROUP, (gi + 1) * POOL_GROUP)
            du = _band_sum(_band(tm, -half + 1, half + 1), ext[:, cols]) - dp_ref[:, cols]
            o_ref[:, cols] = du.astype(o_ref.dtype)
        o_ref[:, C:] = other_ref[...].astype(o_ref.dtype)

    return pl.pallas_call(
        kern, name="pool_bwd_band", grid=(n,),
        in_specs=_halo_specs(tm, C, T) + [pl.BlockSpec((tm, C), lambda i: (i, 0)), pl.BlockSpec((tm, C2), lambda i: (i, 0)),
                                          pl.BlockSpec(after.shape, lambda i: (0, 0))],
        out_specs=pl.BlockSpec((tm, C + C2), lambda i: (i, 0)),
        out_shape=jax.ShapeDtypeStruct((T, C + C2), BF16),
        compiler_params=_params(("arbitrary",)),
    )(q, q, q, dpooled, du_other, after)


def _to_chunk_rows(a):
    T, C = a.shape
    return a.reshape(N_CHUNK, T // N_CHUNK, C).transpose(1, 0, 2).reshape(T, C)


def _from_chunk_rows(a):
    T, C = a.shape
    return a.reshape(T // N_CHUNK, N_CHUNK, C).transpose(1, 0, 2).reshape(T, C)


def _cmul(ar, ai, br, bi):
    return ar * br - ai * bi, ar * bi + ai * br


def _ssm_discretise(log_neg_a_re, a_im, log_dt):
    dt = jnp.exp(log_dt)
    a_re = -jnp.exp(log_neg_a_re)
    mag = jnp.exp(a_re * dt)
    ang = a_im * dt
    lam_re, lam_im = mag * jnp.cos(ang), mag * jnp.sin(ang)
    den = a_re * a_re + a_im * a_im
    f_re = ((lam_re - 1.0) * a_re + lam_im * a_im) / den
    f_im = (lam_im * a_re - (lam_re - 1.0) * a_im) / den
    return lam_re, lam_im, f_re, f_im


def _ssm_params_fwd(log_neg_a_re, a_im, log_dt):
    rows, n = log_neg_a_re.shape

    def kern(a_ref, b_ref, c_ref, o1, o2, o3, o4):
        for o, v in zip((o1, o2, o3, o4), _ssm_discretise(a_ref[...], b_ref[...], c_ref[...])):
            o[...] = v

    return pl.pallas_call(kern, name="ssm_params_fwd", out_shape=[jax.ShapeDtypeStruct((rows, n), F32)] * 4)(
        log_neg_a_re, a_im, log_dt)


def _ssm_params_bwd(log_neg_a_re, a_im, log_dt, cots):
    rows, n = log_neg_a_re.shape

    def kern(a_ref, b_ref, c_ref, g1, g2, g3, g4, o1, o2, o3):
        _, vjp = jax.vjp(_ssm_discretise, a_ref[...], b_ref[...], c_ref[...])
        d1, d2, d3 = vjp((g1[...], g2[...], g3[...], g4[...]))
        o1[...] = d1
        o2[...] = d2
        o3[...] = d3

    return pl.pallas_call(
        kern, name="ssm_params_bwd",
        out_shape=[jax.ShapeDtypeStruct((rows, n), F32), jax.ShapeDtypeStruct((rows, n), F32),
                   jax.ShapeDtypeStruct((rows, 1), F32)])(log_neg_a_re, a_im, log_dt, *cots)


def _cpow(lr, li, n):
    out = None
    br, bi = lr, li
    while n:
        if n & 1:
            out = (br, bi) if out is None else _cmul(out[0], out[1], br, bi)
        n >>= 1
        if n:
            br, bi = _cmul(br, bi, br, bi)
    return out


def _slab(t):
    return pl.ds(pl.multiple_of(t * N_CHUNK, N_CHUNK), N_CHUNK)


def _steps(n, step, carry):
    main = n // SCAN_UNROLL

    def body(i, c):
        for r in range(SCAN_UNROLL):
            c = step(i * SCAN_UNROLL + r, c)
        return c

    carry = lax.fori_loop(0, main, body, carry)
    for k in range(main * SCAN_UNROLL, n):
        carry = step(jnp.int32(k), carry)
    return carry


def _shift_chunks(vr, vi, reverse):
    sub = lax.broadcasted_iota(jnp.int32, vr.shape, 0)
    if reverse:
        keep = sub != N_CHUNK - 1
        return jnp.where(keep, pltpu.roll(vr, N_CHUNK - 1, 0), 0.0), jnp.where(keep, pltpu.roll(vi, N_CHUNK - 1, 0), 0.0)
    keep = sub != 0
    return jnp.where(keep, pltpu.roll(vr, 1, 0), 0.0), jnp.where(keep, pltpu.roll(vi, 1, 0), 0.0)


def _chunk_scan(xr, xi, lr, li, Lc, reverse, prev=None):
    C = xr.shape[1]
    lrb, lib = jnp.broadcast_to(lr, (N_CHUNK, C)), jnp.broadcast_to(li, (N_CHUNK, C))
    zero = jnp.zeros((N_CHUNK, C), F32)

    def step_of(k):
        return (Lc - 1 - k) if reverse else k

    def advance(sr, si, t):
        return lrb * sr - lib * si + xr[_slab(t), :], lrb * si + lib * sr + xi[_slab(t), :]

    def pass1(k, c):
        return advance(c[0], c[1], step_of(k))

    er, ei = _steps(Lc, pass1, (zero, zero))
    pr, pi = _cpow(lr, li, Lc)
    prb, pib = jnp.broadcast_to(pr, (N_CHUNK, C)), jnp.broadcast_to(pi, (N_CHUNK, C))
    cr, ci = zero, zero
    for _ in range(N_CHUNK - 1):
        cr, ci = _shift_chunks(er + prb * cr - pib * ci, ei + prb * ci + pib * cr, reverse)

    if prev is None:
        def pass2(k, c):
            t = step_of(k)
            nr, ni = advance(c[0], c[1], t)
            xr[_slab(t), :] = nr
            xi[_slab(t), :] = ni
            return nr, ni

        _steps(Lc, pass2, (cr, ci))
        return None

    qr, qi = prev
    def pair(ar, ai, wr, wi, acc):
        return acc[0] + ar * wr + ai * wi, acc[1] + ai * wr - ar * wi

    def pass2(k, c):
        sr, si, accr, acci = c
        t = step_of(k)
        nr, ni = advance(sr, si, t)
        xr[_slab(t), :] = nr
        xi[_slab(t), :] = ni
        tp = t - 1 if reverse else t + 1
        accr, acci = pair(nr, ni, qr[_slab(tp), :], qi[_slab(tp), :], (accr, acci))
        return nr, ni, accr, acci

    sr, si, accr, acci = _steps(Lc - 1, pass2, (cr, ci, zero, zero))
    t = step_of(Lc - 1)
    nr, ni = advance(sr, si, t)
    xr[_slab(t), :] = nr
    xi[_slab(t), :] = ni
    edge = step_of(0)
    wr, wi = _shift_chunks(qr[_slab(edge), :], qi[_slab(edge), :], not reverse)
    return pair(nr, ni, wr, wi, (accr, acci))


def _ssm_dims(T):
    assert T % ROW_BLOCK == 0
    return T // N_CHUNK, T // ROW_BLOCK


def _group_mask():
    shape = (SSM_BLOCK_GROUPS * SSM_GROUP, SSM_BLOCK_GROUPS * SSM_STATE)
    rows = lax.broadcasted_iota(jnp.int32, shape, 0) >> GROUP_SHIFT
    cols = lax.broadcasted_iota(jnp.int32, shape, 1) >> STATE_SHIFT
    return rows == cols


def _expand_groups(compact):
    return jnp.where(_group_mask(), jnp.tile(compact, (SSM_BLOCK_GROUPS, 1)), 0.0).astype(BF16)


def _collect_groups(block):
    kept = jnp.where(_group_mask(), block, 0.0)
    out = kept[0:SSM_GROUP]
    for g in range(1, SSM_BLOCK_GROUPS):
        out = out + kept[g * SSM_GROUP:(g + 1) * SSM_GROUP]
    return out


def _ssm_specs(T, direction):
    SC = SSM_BLOCK_GROUPS * SSM_STATE
    col = pl.BlockSpec((T, LANES), lambda m: (0, m))
    mat = pl.BlockSpec((None, None, SSM_GROUP, SC), lambda m: (direction, m, 0, 0))
    vec = pl.BlockSpec((None, None, 1, SC), lambda m: (direction, m, 0, 0))
    return col, mat, vec


def _ssm_project_in(u_ref, bre, bim, xr, xi, nblk):
    for blk in range(nblk):
        rows = pl.ds(blk * ROW_BLOCK, ROW_BLOCK)
        ub = u_ref[rows, :].astype(BF16)
        xr[rows, :] = jnp.dot(ub, bre, preferred_element_type=F32)
        xi[rows, :] = jnp.dot(ub, bim, preferred_element_type=F32)


def _ssm_fwd(name, u_bf, mats, vecs, direction):
    T, W = u_bf.shape
    SC = SSM_BLOCK_GROUPS * SSM_STATE
    nb = W // LANES
    Lc, nblk = _ssm_dims(T)
    reverse = bool(direction)

    def kern(u_ref, bre_ref, bim_ref, cre_ref, cim_ref, lr_ref, li_ref, fr_ref, fi_ref, y_ref, xr, xi):
        _ssm_project_in(u_ref, _expand_groups(bre_ref[...]), _expand_groups(bim_ref[...]), xr, xi, nblk)
        _chunk_scan(xr, xi, lr_ref[...], li_ref[...], Lc, reverse)
        fr, fi = fr_ref[...], fi_ref[...]
        cre, cim = _expand_groups(cre_ref[...]), _expand_groups(cim_ref[...])
        for blk in range(nblk):
            rows = pl.ds(blk * ROW_BLOCK, ROW_BLOCK)
            sr, si = _cmul(fr, fi, xr[rows, :], xi[rows, :])
            y_ref[rows, :] = (lax.dot_general(sr.astype(BF16), cre, _NT, preferred_element_type=F32)
                              - lax.dot_general(si.astype(BF16), cim, _NT, preferred_element_type=F32))

    col, mat, vec = _ssm_specs(T, direction)
    return pl.pallas_call(
        kern, name=name, grid=(nb,),
        in_specs=[col, mat, mat, mat, mat, vec, vec, vec, vec],
        out_specs=col, out_shape=jax.ShapeDtypeStruct((T, W), F32),
        scratch_shapes=[pltpu.VMEM((T, SC), F32), pltpu.VMEM((T, SC), F32)],
        compiler_params=_params(("arbitrary",)),
    )(u_bf, *mats, *vecs)


def _ssm_bwd(name, u_bf, dy_bf, du_in, mats, vecs, direction):
    T, W = u_bf.shape
    SC = SSM_BLOCK_GROUPS * SSM_STATE
    nb = W // LANES
    Lc, nblk = _ssm_dims(T)
    reverse = bool(direction)

    def kern(u_ref, dy_ref, dui_ref, bre_ref, bim_ref, cre_ref, cim_ref, lr_ref, li_ref, fr_ref, fi_ref,
             du_ref, dbre_ref, dbim_ref, dcre_ref, dcim_ref, dlr_ref, dli_ref, dfr_ref, dfi_ref,
             xr, xi, gr, gi):
        lr, li, fr, fi = lr_ref[...], li_ref[...], fr_ref[...], fi_ref[...]
        bre, bim = _expand_groups(bre_ref[...]), _expand_groups(bim_ref[...])
        cre, cim = _expand_groups(cre_ref[...]), _expand_groups(cim_ref[...])
        _ssm_project_in(u_ref, bre, bim, xr, xi, nblk)
        _chunk_scan(xr, xi, lr, li, Lc, reverse)
        mre = jnp.zeros((LANES, SC), F32)
        mim = jnp.zeros((LANES, SC), F32)
        dfr = jnp.zeros((1, SC), F32)
        dfi = jnp.zeros((1, SC), F32)
        for blk in range(nblk):
            rows = pl.ds(blk * ROW_BLOCK, ROW_BLOCK)
            dy = dy_ref[rows, :].astype(BF16)
            x_r, x_i = xr[rows, :], xi[rows, :]
            g_r = jnp.dot(dy, cre, preferred_element_type=F32)
            g_i = -jnp.dot(dy, cim, preferred_element_type=F32)
            mre += lax.dot_general(dy, x_r.astype(BF16), _TN, preferred_element_type=F32)
            mim += lax.dot_general(dy, x_i.astype(BF16), _TN, preferred_element_type=F32)
            dfr += _colsum(g_r * x_r + g_i * x_i)
            dfi += _colsum(g_i * x_r - g_r * x_i)
            gr[rows, :] = fr * g_r + fi * g_i
            gi[rows, :] = fr * g_i - fi * g_r
        dcre_ref[...] = _collect_groups(fr * mre - fi * mim)
        dcim_ref[...] = _collect_groups(-(fr * mim + fi * mre))
        dfr_ref[...] = dfr
        dfi_ref[...] = dfi
        accr, acci = _chunk_scan(gr, gi, lr, -li, Lc, not reverse, prev=(xr, xi))
        dlr_ref[...] = _colsum(accr)
        dli_ref[...] = _colsum(acci)
        dbre = jnp.zeros((LANES, SC), F32)
        dbim = jnp.zeros((LANES, SC), F32)
        for blk in range(nblk):
            rows = pl.ds(blk * ROW_BLOCK, ROW_BLOCK)
            ub = u_ref[rows, :].astype(BF16)
            a_r, a_i = gr[rows, :].astype(BF16), gi[rows, :].astype(BF16)
            dbre += lax.dot_general(ub, a_r, _TN, preferred_element_type=F32)
            dbim += lax.dot_general(ub, a_i, _TN, preferred_element_type=F32)
            du_ref[rows, :] = (dui_ref[rows, :] + lax.dot_general(a_r, bre, _NT, preferred_element_type=F32)
                               + lax.dot_general(a_i, bim, _NT, preferred_element_type=F32))
        dbre_ref[...] = _collect_groups(dbre)
        dbim_ref[...] = _collect_groups(dbim)

    col, mat, vec = _ssm_specs(T, direction)
    mat_out = pl.BlockSpec((None, SSM_GROUP, SC), lambda m: (m, 0, 0))
    vec_out = pl.BlockSpec((None, 1, SC), lambda m: (m, 0, 0))
    mat_shape = jax.ShapeDtypeStruct((nb, SSM_GROUP, SC), F32)
    vec_shape = jax.ShapeDtypeStruct((nb, 1, SC), F32)
    return pl.pallas_call(
        kern, name=name, grid=(nb,),
        in_specs=[col, col, col, mat, mat, mat, mat, vec, vec, vec, vec],
        out_specs=[col, mat_out, mat_out, mat_out, mat_out, vec_out, vec_out, vec_out, vec_out],
        out_shape=[jax.ShapeDtypeStruct((T, W), F32), mat_shape, mat_shape, mat_shape, mat_shape,
                   vec_shape, vec_shape, vec_shape, vec_shape],
        scratch_shapes=[pltpu.VMEM((T, SC), F32)] * 4,
        compiler_params=_params(("arbitrary",)),
    )(u_bf, dy_bf, du_in, *mats, *vecs)


def _groups_side_by_side(p, channel_axis):
    d, G = p.shape[:2]
    nb = G // SSM_BLOCK_GROUPS
    v = p.reshape(d, nb, SSM_BLOCK_GROUPS, p.shape[2], p.shape[3])
    v = v.transpose(0, 1, 4, 2, 3) if channel_axis == 3 else v.transpose(0, 1, 3, 2, 4)
    return v.reshape(d, nb, SSM_GROUP, SSM_BLOCK_GROUPS * SSM_STATE)


def _groups_apart(m, channel_axis):
    d, nb = m.shape[:2]
    v = m.reshape(d, nb, SSM_GROUP, SSM_BLOCK_GROUPS, SSM_STATE)
    v = v.transpose(0, 1, 3, 4, 2) if channel_axis == 3 else v.transpose(0, 1, 3, 2, 4)
    return v.reshape((d, nb * SSM_BLOCK_GROUPS) + v.shape[3:])


def _conv_taps(ref, c, R, T):
    n = T // R
    r0 = pl.multiple_of(c * R, R)
    main = ref[pl.ds(r0, R), :]
    before = ref[pl.ds(pl.multiple_of(jnp.maximum(r0 - 8, 0), 8), 8), :]
    after = ref[pl.ds(pl.multiple_of(jnp.minimum(r0 + R, T - 8), 8), 8), :]
    ext = jnp.concatenate([jnp.where(c > 0, before, 0.0), main, jnp.where(c < n - 1, after, 0.0)], axis=0)
    m1 = pltpu.roll(ext, 1, 0)[8:8 + R]
    p1 = pltpu.roll(ext, R + 15, 0)[8:8 + R]
    return m1, main, p1, r0


def _conv_specs(T, F, cb):
    nj = F // cb
    lo = lambda rows: pl.BlockSpec((rows, cb), lambda j: (0, j))
    hi = lambda rows: pl.BlockSpec((rows, cb), lambda j: (0, j + nj))
    return nj, lo, hi


def _convact_fwd(up, conv_w, conv_b, F):
    T = up.shape[0]
    cb = _pick(F, (256, 128))
    R = _pick(T, (256, 128, 64))
    nj, lo, hi = _conv_specs(T, F, cb)

    def kern(uv, ug, wv, wg, bv, bg, o_ref):
        def body(c, carry):
            v1, v0, v2, r0 = _conv_taps(uv, c, R, T)
            g1, g0, g2, _ = _conv_taps(ug, c, R, T)
            val = v1 * wv[0:1, :] + v0 * wv[1:2, :] + v2 * wv[2:3, :] + bv[...]
            gate = g1 * wg[0:1, :] + g0 * wg[1:2, :] + g2 * wg[2:3, :] + bg[...]
            o_ref[pl.ds(r0, R), :] = (val * (gate * _sigmoid(gate))).astype(o_ref.dtype)
            return carry

        lax.fori_loop(0, T // R, body, 0)

    return pl.pallas_call(
        kern, name="convact_fwd", grid=(nj,),
        in_specs=[lo(T), hi(T), lo(3), hi(3), lo(1), hi(1)],
        out_specs=lo(T), out_shape=jax.ShapeDtypeStruct((T, F), BF16),
        compiler_params=_params(("arbitrary",)),
    )(up, up, conv_w, conv_w, conv_b, conv_b)


def _convact_bwd(up, dact, conv_w, conv_b, F):
    T = up.shape[0]
    cb = _pick(F, (256, 128))
    R = _pick(T, (256, 128, 64))
    nj, lo, hi = _conv_specs(T, F, cb)

    def kern(uv, ug, da, wv, wg, bv, bg, dup, dwv, dwg, dbv, dbg, sv, sg):
        zero = jnp.zeros((1, cb), F32)

        def pass_a(c, acc):
            v1, v0, v2, r0 = _conv_taps(uv, c, R, T)
            g1, g0, g2, _ = _conv_taps(ug, c, R, T)
            val = v1 * wv[0:1, :] + v0 * wv[1:2, :] + v2 * wv[2:3, :] + bv[...]
            gate = g1 * wg[0:1, :] + g0 * wg[1:2, :] + g2 * wg[2:3, :] + bg[...]
            sig = _sigmoid(gate)
            d = da[pl.ds(r0, R), :]
            dval = d * (gate * sig)
            dgate = d * val * (sig * (1.0 + gate * (1.0 - sig)))
            sv[pl.ds(r0, R), :] = dval
            sg[pl.ds(r0, R), :] = dgate
            terms = (dval * v1, dval * v0, dval * v2, dval, dgate * g1, dgate * g0, dgate * g2, dgate)
            return tuple(a + _colsum(t) for a, t in zip(acc, terms))

        acc = lax.fori_loop(0, T // R, pass_a, (zero,) * 8)
        for k in range(3):
            dwv[k:k + 1, :] = acc[k]
            dwg[k:k + 1, :] = acc[4 + k]
        dbv[...] = acc[3]
        dbg[...] = acc[7]

        def pass_b(c, carry):
            v1, v0, v2, r0 = _conv_taps(sv, c, R, T)
            g1, g0, g2, _ = _conv_taps(sg, c, R, T)
            dup[0, pl.ds(r0, R), :] = (v2 * wv[0:1, :] + v0 * wv[1:2, :] + v1 * wv[2:3, :]).astype(dup.dtype)
            dup[1, pl.ds(r0, R), :] = (g2 * wg[0:1, :] + g0 * wg[1:2, :] + g1 * wg[2:3, :]).astype(dup.dtype)
            return carry

        lax.fori_loop(0, T // R, pass_b, 0)

    dup, dwv, dwg, dbv, dbg = pl.pallas_call(
        kern, name="convact_bwd", grid=(nj,),
        in_specs=[lo(T), hi(T), lo(T), lo(3), hi(3), lo(1), hi(1)],
        out_specs=[pl.BlockSpec((2, T, cb), lambda j: (0, 0, j)), lo(3), lo(3), lo(1), lo(1)],
        out_shape=[jax.ShapeDtypeStruct((2, T, F), BF16),
                   jax.ShapeDtypeStruct((3, F), F32), jax.ShapeDtypeStruct((3, F), F32),
                   jax.ShapeDtypeStruct((1, F), F32), jax.ShapeDtypeStruct((1, F), F32)],
        scratch_shapes=[pltpu.VMEM((T, cb), F32), pltpu.VMEM((T, cb), F32)],
        compiler_params=_params(("arbitrary",)),
    )(up, up, dact, conv_w, conv_w, conv_b, conv_b)
    return dup, (dwv, dwg), jnp.concatenate([dbv, dbg], axis=1)


def _peers(x, y, c, with_self=False):
    out = []
    for k in range(0 if with_self else 1, N_DEV):
        px = 1 - x if k & 4 else x
        py = 1 - y if k & 2 else y
        pc = 1 - c if k & 1 else c
        out.append((k, (px, py, pc), 4 * px + 2 * py + pc))
    return out


def _exchange_start(name, bufs, modes, after):
    n = len(bufs)
    ncopy = n * N_DEV
    lands = [lax.empty((N_DEV,) + tuple(b.shape[-2:]), b.dtype) for b in bufs]

    def body(*refs):
        ins, land_in = refs[:n], refs[n:2 * n]
        send_sems, recv_sems = refs[2 * n + 1], refs[2 * n + 2]
        token = refs[-1]
        x, y, c = lax.axis_index("x"), lax.axis_index("y"), lax.axis_index("c")
        me = 4 * x + 2 * y + c
        for b in range(n):
            for k, peer, slot in _peers(x, y, c, with_self=True):
                sem = b * N_DEV + k
                pltpu.make_async_remote_copy(
                    src_ref=ins[b] if modes[b] == 'gather' else ins[b].at[slot], dst_ref=land_in[b].at[me],
                    send_sem=send_sems.at[sem], recv_sem=recv_sems.at[sem],
                    device_id=peer, device_id_type=MESH_ID).start()
        token[...] = jnp.zeros_like(token)

    hbm = pl.BlockSpec(memory_space=pltpu.HBM)
    sem_spec = pl.BlockSpec(memory_space=pltpu.SEMAPHORE)
    operands = [pltpu.with_memory_space_constraint(a, pltpu.HBM) for a in list(bufs) + lands]
    res = pl.pallas_call(
        body, name=name,
        out_shape=(pltpu.SemaphoreType.DMA((ncopy,)), pltpu.SemaphoreType.DMA((ncopy,)),
                   *[pltpu.HBM(a.shape, a.dtype) for a in operands], jax.ShapeDtypeStruct((8, LANES), F32)),
        in_specs=[hbm] * (2 * n) + [pl.BlockSpec(memory_space=pl.ANY)],
        out_specs=(sem_spec, sem_spec, *([hbm] * (2 * n)), pl.BlockSpec(memory_space=pltpu.VMEM)),
        input_output_aliases={i: 2 + i for i in range(2 * n)},
        compiler_params=pltpu.CompilerParams(has_side_effects=pltpu.SideEffectType.DATAFLOW_SIDE_EFFECTING),
    )(*operands, after)
    return (res[0], res[1], list(res[2:2 + n]), list(res[2 + n:2 + 2 * n]), tuple(modes)), res[-1]


def _exchange_wait(name, handle, after):
    send_sems, recv_sems, srcs, lands, modes = handle
    n = len(srcs)

    def body(*refs):
        src_in, land_in = refs[:n], refs[n:2 * n]
        send_ref, recv_ref = refs[2 * n], refs[2 * n + 1]
        x, y, c = lax.axis_index("x"), lax.axis_index("y"), lax.axis_index("c")
        for b in range(n):
            for k, peer, slot in _peers(x, y, c, with_self=True):
                sem = b * N_DEV + k
                copy = pltpu.make_async_remote_copy(
                    src_ref=src_in[b] if modes[b] == 'gather' else src_in[b].at[slot], dst_ref=land_in[b].at[slot],
                    send_sem=send_ref.at[sem], recv_sem=recv_ref.at[sem],
                    device_id=peer, device_id_type=MESH_ID)
                copy.wait_send()
                copy.wait_recv()

    hbm = pl.BlockSpec(memory_space=pltpu.HBM)
    sem_spec = pl.BlockSpec(memory_space=pltpu.SEMAPHORE)
    res = pl.pallas_call(
        body, name=name,
        out_shape=tuple(pltpu.HBM(a.shape, a.dtype) for a in srcs + lands),
        in_specs=[hbm] * (2 * n) + [sem_spec, sem_spec, pl.BlockSpec(memory_space=pl.ANY)],
        out_specs=tuple([hbm] * (2 * n)),
        input_output_aliases={i: i for i in range(2 * n)},
        compiler_params=pltpu.CompilerParams(has_side_effects=pltpu.SideEffectType.DATAFLOW_SIDE_EFFECTING),
    )(*srcs, *lands, send_sems, recv_sems, after)
    return list(res[n:])


def _adamw(w, g, m, v):
    m2 = ADAM_B1 * m + (1.0 - ADAM_B1) * g
    v2 = ADAM_B2 * v + (1.0 - ADAM_B2) * (g * g)
    m_hat = m2 / (1.0 - ADAM_B1 ** ADAM_STEP)
    v_hat = v2 / (1.0 - ADAM_B2 ** ADAM_STEP)
    return -ADAM_LR * (m_hat / (jnp.sqrt(v_hat) + ADAM_EPS) + ADAM_WD * w), m2, v2


def _sum_adamw(name, landed, w, m, v):
    R, C = w.shape
    row_bytes = N_DEV * (-(-C // LANES) * LANES) * landed.dtype.itemsize
    tiles = [d for d in range(16, R, 16) if R % d == 0 and d * row_bytes <= ADAM_TILE_BYTES]
    tr = max(tiles) if tiles and R * row_bytes > ADAM_TILE_BYTES else R

    def kern(x_ref, w_ref, m_ref, v_ref, g_out, d_out, m_out, v_out):
        g = x_ref[0].astype(F32)
        for q in range(1, N_DEV):
            g = g + x_ref[q].astype(F32)
        g_out[...] = g
        d_out[...], m_out[...], v_out[...] = _adamw(w_ref[...], g, m_ref[...], v_ref[...])

    blk = pl.BlockSpec((tr, C), lambda i: (i, 0))
    return pl.pallas_call(
        kern, name=name, grid=(R // tr,),
        in_specs=[pl.BlockSpec((N_DEV, tr, C), lambda i: (0, i, 0)), blk, blk, blk],
        out_specs=[blk] * 4, out_shape=[jax.ShapeDtypeStruct((R, C), F32)] * 4,
        compiler_params=_params(("arbitrary",)),
    )(landed, w, m, v)


def _sum_adamw_many(name, items):
    n = len(items)

    def kern(*refs):
        ins, outs = refs[:4 * n], refs[4 * n:]
        for k in range(n):
            x_ref, w_ref, m_ref, v_ref = ins[4 * k:4 * k + 4]
            g = x_ref[0].astype(F32)
            for q in range(1, N_DEV):
                g = g + x_ref[q].astype(F32)
            outs[4 * k][...] = g
            outs[4 * k + 1][...], outs[4 * k + 2][...], outs[4 * k + 3][...] = _adamw(w_ref[...], g, m_ref[...], v_ref[...])

    out_shape = [jax.ShapeDtypeStruct(w.shape, F32) for _, w, _, _ in items for _ in range(4)]
    res = pl.pallas_call(kern, name=name, out_shape=out_shape)(*[a for item in items for a in item])
    return [res[4 * k:4 * k + 4] for k in range(n)]


def _view2d(a):
    if a.ndim == 1:
        return a.reshape(1, -1)
    return a.reshape(-1, a.shape[-1])


def kernel(x, norm_mix_g, w_in, pool_w, pool_scale, ssm_log_neg_a_re, ssm_a_im, ssm_log_dt, ssm_b_re, ssm_b_im, ssm_c_re, ssm_c_im, ssm_d, glu_w, glu_b, out_norm_pool_g, out_norm_ssm_g, w_out, norm_ffn_g, w_up, conv_w, conv_b, w_down, final_norm_g, loss_target, m_norm_mix_g, m_w_in, m_pool_w, m_pool_scale, m_ssm_log_neg_a_re, m_ssm_a_im, m_ssm_log_dt, m_ssm_b_re, m_ssm_b_im, m_ssm_c_re, m_ssm_c_im, m_ssm_d, m_glu_w, m_glu_b, m_out_norm_pool_g, m_out_norm_ssm_g, m_w_out, m_norm_ffn_g, m_w_up, m_conv_w, m_conv_b, m_w_down, m_final_norm_g, v_norm_mix_g, v_w_in, v_pool_w, v_pool_scale, v_ssm_log_neg_a_re, v_ssm_a_im, v_ssm_log_dt, v_ssm_b_re, v_ssm_b_im, v_ssm_c_re, v_ssm_c_im, v_ssm_d, v_glu_w, v_glu_b, v_out_norm_pool_g, v_out_norm_ssm_g, v_w_out, v_norm_ffn_g, v_w_up, v_conv_w, v_conv_b, v_w_down, v_final_norm_g):
    given = dict(locals())
    weights = {n: given[n] for n in WEIGHTS}
    mom1 = {n: given["m_" + n] for n in WEIGHTS}
    mom2 = {n: given["v_" + n] for n in WEIGHTS}

    xs = x[0]
    tgt = loss_target[0]
    T, D = xs.shape
    DP = len(POOL_WINDOWS) * POOL_GROUP
    DS = D - DP
    G = DS // SSM_GROUP
    N, H = SSM_STATE, SSM_GROUP
    F2 = w_up.shape[1] * N_DEV
    F = F2 // 2
    fs = w_up.shape[1]
    row = lambda a: a.reshape(1, -1)

    early_handle, token = _exchange_start("gather_early_start", [w_in.astype(BF16), glu_w.astype(BF16)],
                                          ['gather'] * 2, after=w_in)
    late_src = [w_out.astype(BF16), w_down.astype(BF16), w_up.T.astype(BF16), conv_w]
    late_handle, token = _exchange_start("gather_late_start", late_src, ['gather'] * 4, after=token)
    CB = row(conv_b)

    g1, g2, g3 = row(norm_mix_g), row(norm_ffn_g), row(final_norm_g)
    gp, gs = row(out_norm_pool_g), row(out_norm_ssm_g)
    (xn,) = _rowmap("norm_mix", _rms_fwd, [xs], [g1 + token[0:1, 0:1]], [(D, BF16)])
    early = _exchange_wait("gather_early_wait", early_handle, xn)
    W_in = early[0].reshape(D, D)
    W_glu = early[1].reshape(DS, DS)
    u = _mm("proj_in", xn, W_in, 'nn', F32)
    tmp = _pick(T, (256, 128))
    pool_w_bf = pool_w.astype(BF16)
    ypn = _pool_fwd(u, pool_w_bf, row(pool_scale), gp, tmp)

    u_ssm = _to_chunk_rows(u[:, DP:])
    u_ssm_bf = u_ssm.astype(BF16)
    a_rows = (ssm_log_neg_a_re.reshape(2 * G, N), ssm_a_im.reshape(2 * G, N), ssm_log_dt.reshape(2 * G, 1))
    disc = _ssm_params_fwd(*a_rows)
    nb = G // SSM_BLOCK_GROUPS
    vecs = [d.reshape(2, nb, 1, SSM_BLOCK_GROUPS * N) for d in disc]
    mats = [_groups_side_by_side(ssm_b_re, 3), _groups_side_by_side(ssm_b_im, 3),
            _groups_side_by_side(ssm_c_re, 2), _groups_side_by_side(ssm_c_im, 2)]
    y_dir = [_ssm_fwd("ssm_fwd_%d" % d, u_ssm_bf, mats, vecs, d) for d in range(2)]

    def mix_post(yf, yb, us, d, gw, gb, g):
        y = yf + yb + d * us
        z = _gelu(y)
        gate = _sigmoid(jnp.dot(z.astype(BF16), gw, preferred_element_type=F32) + gb)
        return _rms_fwd(z * gate, g), y

    ysn, y_ssm = _rowmap("ssm_post", mix_post, [y_dir[0], y_dir[1], u_ssm], [row(ssm_d), W_glu, row(glu_b), gs],
                         [(DS, BF16), (DS, F32)])
    ycat = jnp.concatenate([ypn, _from_chunk_rows(ysn)], axis=1)
    late = _exchange_wait("gather_late_wait", late_handle, ysn)
    W_out = late[0].reshape(D, D)
    W_down = late[1].reshape(F, D)
    W_up_t = late[2].reshape(F2, D)
    CW = late[3].transpose(1, 0, 2).reshape(3, F2)
    h1, hn = _mm_rows("proj_out", ycat, W_out, 'nn', lambda h, g: (h, _rms_fwd(h, g)), [], [g2],
                      [(D, F32), (D, BF16)], add=xs)
    up = _mm("ffn_up", hn, W_up_t, 'nt', F32)
    act = _convact_fwd(up, CW, CB, F)

    def head(h, t, g):
        r = lax.rsqrt(_rowmean(h * h) + EPS)
        hh = h * r
        e = hh * g - t
        loss = 0.5 * jnp.sum(_rowmean(e * e), keepdims=True)
        dy = e * (1.0 / D)
        dxh = dy * g
        dh = r * (dxh - hh * _rowmean(dxh * hh))
        return dh, dh, jnp.broadcast_to(loss, (1, LANES)), _colsum(dy * hh)

    dh2, dh2_bf, loss_acc, dg3 = _mm_rows("ffn_down", act, W_down, 'nn', head, [tgt], [g3],
                                          [(D, F32), (D, BF16)], [(1, LANES), (1, D)], add=h1)

    dact = _mm("ffn_down_dx", dh2_bf, W_down, 'nt', F32)
    dW_down = _mm("ffn_down_dw", act, dh2_bf, 'tn', BF16)
    dup, dCW, dCB = _convact_bwd(up, dact, CW, CB, F)
    dhn_val = _mm("ffn_up_dx_val", dup, W_up_t, 'nn', F32, b_rows=(2, 0), a_plane=0)
    dW_up_t = _mm("ffn_up_dw", dup, hn, 'tn', BF16).reshape(F2, D)

    def shards(full_grad):
        return full_grad.reshape((N_DEV, full_grad.shape[0] // N_DEV) + full_grad.shape[1:])

    half = N_DEV // 2
    dCW_sh = jnp.concatenate([h.reshape(3, half, fs).transpose(1, 0, 2) for h in dCW], axis=0)
    ffn_sent = [shards(dW_down), shards(dW_up_t), dCW_sh]
    ffn_handle, token = _exchange_start("scatter_ffn_start", ffn_sent, ['scatter'] * 3, after=dhn_val)
    g2_late = g2 + token[0:1, 0:1]

    def norm_bwd_add(dy, hx, res, g):
        dx, dg = _rms_bwd(hx, g, dy)
        return dx + res, dg

    def norm_bwd_add2(dy, hx, res, g):
        dx, dg = _rms_bwd(hx, g, dy)
        return dx + res, dx + res, dg

    dh1, dh1_bf, dg2 = _mm_rows("ffn_up_dx_gate", dup, W_up_t, 'nn', norm_bwd_add2, [h1, dh2], [g2_late],
                                [(D, F32), (D, BF16)], [(1, D)], add=dhn_val, b_rows=(2, 1), a_plane=1)
    dycat = _mm("proj_out_dx", dh1_bf, W_out, 'nt', F32)
    dW_out = _mm("proj_out_dw", ycat, dh1_bf, 'tn', BF16)

    out_sent = [shards(dW_out)]
    out_handle, token = _exchange_start("scatter_out_start", out_sent, ['scatter'], after=dycat)
    q, dpooled, dpool_w, dpool_scale, dgp = _pool_bwd_rows(u, dycat, pool_w_bf, row(pool_scale),
                                                           gp + token[0:1, 0:1], tmp)

    def mix_post_bwd(y, us, dyn, d, gw, gb, g):
        z = _gelu(y)
        gz = z.astype(BF16)
        sig = _sigmoid(jnp.dot(gz, gw, preferred_element_type=F32) + gb)
        dys, dg = _rms_bwd(z * sig, g, dyn)
        dgl = dys * z * sig * (1.0 - sig)
        dgl_bf = dgl.astype(BF16)
        dz = dys * sig + lax.dot_general(dgl_bf, gw, (((1,), (1,)), ((), ())), preferred_element_type=F32)
        dgw = lax.dot_general(gz, dgl_bf, (((0,), (0,)), ((), ())), preferred_element_type=F32)
        dyv = dz * _gelu_grad(y)
        return dyv, dyv * d, dgw, _colsum(dgl), _colsum(dyv * us), dg

    dyv, du_dir, dglu_w, dglu_b, dssm_d, dgs = _rowmap(
        "ssm_post_bwd", mix_post_bwd, [y_ssm, u_ssm, _to_chunk_rows(dycat[:, DP:])], [row(ssm_d), W_glu, row(glu_b), gs],
        [(DS, BF16), (DS, F32)], [(DS, DS), (1, DS), (1, DS), (1, DS)])

    du_ssm = du_dir
    dB, dC, dvec = [], [], []
    for d in range(2):
        res = _ssm_bwd("ssm_bwd_%d" % d, u_ssm_bf, dyv, du_ssm, mats, vecs, d)
        du_ssm = res[0]
        dB.append(res[1:3])
        dC.append(res[3:5])
        dvec.append(res[5:9])
    grad_b = [_groups_apart(jnp.stack([dB[0][k], dB[1][k]]), 3) for k in range(2)]
    grad_c = [_groups_apart(jnp.stack([dC[0][k], dC[1][k]]), 2) for k in range(2)]
    cots = [jnp.stack([dvec[0][k], dvec[1][k]]).reshape(2 * G, N) for k in range(4)]
    d_a_re, d_a_im, d_log_dt = _ssm_params_bwd(*a_rows, cots)

    rep_grads = {
        'pool_w': dpool_w, 'pool_scale': dpool_scale, 'ssm_log_neg_a_re': d_a_re, 'ssm_a_im': d_a_im,
        'ssm_log_dt': d_log_dt, 'ssm_b_re': grad_b[0], 'ssm_b_im': grad_b[1], 'ssm_c_re': grad_c[0], 'ssm_c_im': grad_c[1],
        'ssm_d': dssm_d, 'glu_b': dglu_b, 'out_norm_pool_g': dgp, 'out_norm_ssm_g': dgs, 'norm_ffn_g': dg2,
        'conv_b': dCB, 'final_norm_g': dg3}
    wide = ('pool_w', 'ssm_b_re', 'ssm_b_im', 'ssm_c_re', 'ssm_c_im')

    def narrow(n):
        return weights[n].ndim == 4 and weights[n].shape[-1] < LANES // 2

    def travel(n):
        g = rep_grads[n]
        if narrow(n):
            g = jnp.swapaxes(g, 2, 3)
        if n in wide:
            return g.reshape(-1, PACK_W if g.size % PACK_W == 0 else LANES).astype(BF16)
        return _view2d(g.reshape(weights[n].shape))

    small = tuple(n for n in REPLICATED if n != 'norm_mix_g')
    small_sent = [shards(dglu_w.astype(BF16))] + [travel(n) for n in small] + [loss_acc]
    small_handle, token = _exchange_start("reduce_small_start", small_sent, ['scatter'] + ['gather'] * (len(small) + 1),
                                          after=d_a_re)

    du = _pool_bwd_band(q, dpooled, _from_chunk_rows(du_ssm), token, tmp)
    dW_in = _mm("proj_in_dw", xn, du, 'tn', BF16)
    in_handle, token = _exchange_start("scatter_in_start", [shards(dW_in)], ['scatter'], after=dW_in)
    dx, dg1 = _mm_rows("proj_in_dx", du, W_in + token[0, 0].astype(BF16), 'nt', norm_bwd_add, [xs, dh1], [g1],
                       [(D, F32)], [(1, D)])
    rep_grads['norm_mix_g'] = dg1
    last_handle, token = _exchange_start("reduce_last_start", [travel('norm_mix_g')], ['gather'], after=dx)

    grads, delta, new_m, new_v = {}, {}, {}, {}

    def update(n, landed_n):
        shape = weights[n].shape
        if narrow(n):
            swapped = lambda a: _view2d(jnp.swapaxes(a, 2, 3))
            landed_n = landed_n.reshape((N_DEV,) + swapped(weights[n]).shape)
            res = _sum_adamw("adamw_" + n, landed_n, swapped(weights[n]), swapped(mom1[n]), swapped(mom2[n]))
            back = lambda r: jnp.swapaxes(r.reshape(shape[:2] + (shape[3], shape[2])), 2, 3)
            grads[n], delta[n], new_m[n], new_v[n] = [back(r) for r in res]
            return
        if n == 'w_up':
            res = _sum_adamw("adamw_" + n, landed_n, weights[n].T, mom1[n].T, mom2[n].T)
            grads[n], delta[n], new_m[n], new_v[n] = [r.T for r in res]
            return
        landed_n = landed_n.reshape((N_DEV,) + _view2d(weights[n]).shape)
        res = _sum_adamw("adamw_" + n, landed_n, _view2d(weights[n]), _view2d(mom1[n]), _view2d(mom2[n]))
        grads[n], delta[n], new_m[n], new_v[n] = [r.reshape(shape) for r in res]

    def update_many(name, named_landed):
        items = [(l.reshape((N_DEV,) + _view2d(weights[n]).shape), _view2d(weights[n]), _view2d(mom1[n]), _view2d(mom2[n]))
                 for n, l in named_landed]
        for (n, _), res in zip(named_landed, _sum_adamw_many(name, items)):
            grads[n], delta[n], new_m[n], new_v[n] = [r.reshape(weights[n].shape) for r in res]

    ffn_landed = _exchange_wait("scatter_ffn_wait", ffn_handle, token)
    for n, l in zip(('w_down', 'w_up'), ffn_landed):
        update(n, l)
    (out_landed,) = _exchange_wait("scatter_out_wait", out_handle, token)
    update('w_out', out_landed)
    small_landed = _exchange_wait("reduce_small_wait", small_handle, token)
    tiny = [('conv_w', ffn_landed[2])]
    for n, l in zip(('glu_w',) + small, small_landed):
        if weights[n].size <= ADAM_TINY:
            tiny.append((n, l))
        else:
            update(n, l)
    update_many("adamw_tiny", tiny)

    def loss_sum(parts_ref, o_ref):
        s = parts_ref[0]
        for q in range(1, N_DEV):
            s = s + parts_ref[q]
        o_ref[...] = s

    loss = pl.pallas_call(loss_sum, name="loss_sum", out_shape=jax.ShapeDtypeStruct((1, LANES), F32))(small_landed[-1])[0, 0]

    update('w_in', _exchange_wait("scatter_in_wait", in_handle, grads[small[-1]])[0])
    update('norm_mix_g', _exchange_wait("reduce_last_wait", last_handle, grads['w_in'])[0])

    return (loss, dx[None], *[grads[n] for n in WEIGHTS], *[delta[n] for n in WEIGHTS],
            *[new_m[n] for n in WEIGHTS], *[new_v[n] for n in WEIGHTS])
```

```python
import functools

import jax
import jax.numpy as jnp
import numpy as np
from jax import lax
from jax.experimental import pallas as pl
from jax.experimental.pallas import tpu as pltpu

F32 = jnp.float32
BF16 = jnp.bfloat16
MESH_ID = pl.DeviceIdType.MESH

N_DEV = 8
EPS = 1e-6
POOL_WINDOWS = (2, 4, 8, 16)
POOL_GROUP = 128
POOL_PAD = 64
SSM_GROUP = 16
SSM_STATE = 64
SSM_BLOCK_GROUPS = 8
N_CHUNK = 8
ROW_BLOCK = 512
SCAN_UNROLL = 4
GROUP_SHIFT = 4
STATE_SHIFT = 6
_NT = (((1,), (1,)), ((), ()))
_TN = (((0,), (0,)), ((), ()))
LANES = 128
PACK_W = 1024
VMEM_LIMIT = 56 * 1024 * 1024
MM_VMEM_BUDGET = 40 * 1024 * 1024
MM_TILE_CAP = 1408
MM_ROWS_TILE = 512
ADAM_TILE_BYTES = 4 * 1024 * 1024
ADAM_TINY = 8192

ADAM_LR = 0.001
ADAM_B1 = 0.9
ADAM_B2 = 0.999
ADAM_EPS = 1e-08
ADAM_WD = 0.01
ADAM_STEP = 10

WEIGHTS = ['norm_mix_g', 'w_in', 'pool_w', 'pool_scale', 'ssm_log_neg_a_re', 'ssm_a_im', 'ssm_log_dt',
           'ssm_b_re', 'ssm_b_im', 'ssm_c_re', 'ssm_c_im', 'ssm_d', 'glu_w', 'glu_b', 'out_norm_pool_g',
           'out_norm_ssm_g', 'w_out', 'norm_ffn_g', 'w_up', 'conv_w', 'conv_b', 'w_down', 'final_norm_g']
SHARDED = ('w_in', 'w_out', 'w_down', 'glu_w', 'w_up', 'conv_w')
REPLICATED = tuple(n for n in WEIGHTS if n not in SHARDED)


def _pick(n, prefs):
    for p in prefs:
        if n % p == 0:
            return p
    return n


def _params(sem, vmem=None):
    return pltpu.CompilerParams(dimension_semantics=sem, vmem_limit_bytes=vmem or VMEM_LIMIT)


def _tiles(n, cap):
    return [d for d in range(LANES, min(n, cap) + 1, LANES) if n % d == 0] or [n]


def _mm(name, a, b, mode, out_dtype, add=None, b_rows=(1, 0), a_plane=None):
    batch = a.shape[0] if a.ndim == 3 and a_plane is None else None
    if mode == 'nn':
        (M, K), N = a.shape[-2:], b.shape[1]
        assert b.shape[0] == K * b_rows[0]
    elif mode == 'nt':
        (M, K), (N, _) = a.shape[-2:], b.shape
    else:
        (K, M), (_, N) = a.shape[-2:], b.shape
    dims = {'nn': (((1,), (0,)), ((), ())), 'nt': (((1,), (1,)), ((), ())), 'tn': (((0,), (0,)), ((), ()))}[mode]
    sa, sb, so = a.dtype.itemsize, b.dtype.itemsize, jnp.dtype(out_dtype).itemsize
    best = None
    for tm in _tiles(M, MM_TILE_CAP):
        for tn in _tiles(N, MM_TILE_CAP):
            need = 2 * (tm * K * sa + tn * K * sb + tm * tn * (so + (4 if add is not None else 0)))
            if need <= MM_VMEM_BUDGET:
                key = (tm * tn / (tm + tn), tm * tn)
                if best is None or key > best[0]:
                    best = (key, tm, tn)
    _, tm, tn = best
    rows_inner = a.size * sa * (N // tn) + b.size * sb < a.size * sa + b.size * sb * (M // tm)

    def kern(*refs):
        a_ref, b_ref = refs[:2]
        o_ref = refs[-1]
        r = lax.dot_general(a_ref[...].astype(BF16), b_ref[...].astype(BF16), dims, preferred_element_type=F32)
        if add is not None:
            r = r + refs[2][...]
        o_ref[...] = r.astype(o_ref.dtype)

    if rows_inner:
        grid = (N // tn, M // tm)
        ij = lambda g0, g1: (g1, g0)
    else:
        grid = (M // tm, N // tn)
        ij = lambda g0, g1: (g0, g1)

    def spec(block, index, batched=False):
        if batch is None:
            return pl.BlockSpec(block, lambda g0, g1: index(*ij(g0, g1)))
        if batched:
            return pl.BlockSpec((None,) + block, lambda p, g0, g1: (p,) + index(*ij(g0, g1)))
        return pl.BlockSpec(block, lambda p, g0, g1: index(*ij(g0, g1)))

    if a_plane is not None:
        assert mode != 'tn'
        a_spec = pl.BlockSpec((None, tm, K), lambda g0, g1: (a_plane, ij(g0, g1)[0], 0))
    else:
        a_spec = (spec((K, tm), lambda i, j: (0, i), True) if mode == 'tn' else spec((tm, K), lambda i, j: (i, 0), True))
    b_spec = (spec((tn, K), lambda i, j: (j, 0)) if mode == 'nt' else spec((K, tn), lambda i, j: (b_rows[1], j)))
    in_specs = [a_spec, b_spec]
    args = [a, b]
    if add is not None:
        in_specs.append(spec((tm, tn), lambda i, j: (i, j), True))
        args.append(add)
    lead = () if batch is None else (batch,)
    return pl.pallas_call(
        kern, name=name, grid=lead + grid, in_specs=in_specs,
        out_specs=spec((tm, tn), lambda i, j: (i, j), True),
        out_shape=jax.ShapeDtypeStruct(lead + (M, N), out_dtype),
        compiler_params=_params(("parallel",) * (len(lead) + 2)),
    )(*args)


def _store_rows_and_sums(i, outs, res, n_row):
    for o, v in zip(outs[:n_row], res[:n_row]):
        o[...] = v.astype(o.dtype)
    for o, v in zip(outs[n_row:], res[n_row:]):
        @pl.when(i == 0)
        def _(o=o, v=v):
            o[...] = v

        @pl.when(i != 0)
        def _(o=o, v=v):
            o[...] += v


def _mm_rows(name, a, b, mode, fn, rows, fulls, out_rows, out_accs=(), add=None, b_rows=(1, 0), a_plane=None):
    M, K = a.shape[-2:]
    N = b.shape[1] if mode == 'nn' else b.shape[0]
    assert b.shape[0] == (K * b_rows[0] if mode == 'nn' else N)
    dims = _NT if mode == 'nt' else (((1,), (0,)), ((), ()))
    tm = min(MM_ROWS_TILE, M)
    n_extra, n_row = len(rows) + len(fulls), len(out_rows)

    def kern(*refs):
        i = pl.program_id(0)
        r = lax.dot_general(refs[0][...].astype(BF16), refs[1][...].astype(BF16), dims, preferred_element_type=F32)
        pos = 2
        if add is not None:
            r = r + refs[pos][...]
            pos += 1
        res = fn(r, *[x[...] for x in refs[pos:pos + n_extra]])
        _store_rows_and_sums(i, refs[pos + n_extra:], res, n_row)

    def full_spec(shape):
        nd = len(shape)
        return pl.BlockSpec(tuple(shape), lambda i: (0,) * nd)

    a_spec = (pl.BlockSpec((tm, K), lambda i: (i, 0)) if a_plane is None
              else pl.BlockSpec((None, tm, K), lambda i: (a_plane, i, 0)))
    b_spec = pl.BlockSpec((K, N), lambda i: (b_rows[1], 0)) if mode == 'nn' else pl.BlockSpec((N, K), lambda i: (0, 0))
    ins = [_row_in(r, tm, M) for r in ([add] if add is not None else []) + list(rows)]
    outs = [_row_out(o, tm, M) for o in out_rows]
    res = pl.pallas_call(
        kern, name=name, grid=(M // tm,),
        in_specs=[a_spec, b_spec] + [s for s, _ in ins] + [full_spec(f.shape) for f in fulls],
        out_specs=[s for s, _ in outs] + [full_spec(s) for s in out_accs],
        out_shape=[sh for _, sh in outs] + [jax.ShapeDtypeStruct(tuple(s), F32) for s in out_accs],
        compiler_params=_params(("arbitrary",)),
    )(a, b, *[x for _, x in ins], *fulls)
    return res


def _row_in(r, tm, T):
    arr, w, k = r if isinstance(r, tuple) else (r, r.shape[1], 0)
    return pl.BlockSpec((tm, w), lambda i: (i, k)), arr


def _row_out(o, tm, T):
    c, dt = o
    return pl.BlockSpec((tm, c), lambda i: (i, 0)), jax.ShapeDtypeStruct((T, c), dt)


def _rowmap(name, fn, rows, fulls, out_rows, out_accs=(), tm=512):
    T = (rows[0][0] if isinstance(rows[0], tuple) else rows[0]).shape[0]
    tm = min(tm, T)
    assert T % tm == 0
    n_in, n_row = len(rows) + len(fulls), len(out_rows)

    def kern(*refs):
        i = pl.program_id(0)
        res = fn(*[r[...] for r in refs[:n_in]])
        res = res if isinstance(res, (tuple, list)) else (res,)
        _store_rows_and_sums(i, refs[n_in:], res, n_row)

    def full_spec(shape):
        nd = len(shape)
        return pl.BlockSpec(tuple(shape), lambda i: (0,) * nd)

    ins = [_row_in(r, tm, T) for r in rows]
    outs = [_row_out(o, tm, T) for o in out_rows]
    res = pl.pallas_call(
        kern, name=name, grid=(T // tm,),
        in_specs=[s for s, _ in ins] + [full_spec(f.shape) for f in fulls],
        out_specs=[s for s, _ in outs] + [full_spec(s) for s in out_accs],
        out_shape=[sh for _, sh in outs] + [jax.ShapeDtypeStruct(tuple(s), F32) for s in out_accs],
        compiler_params=_params(("arbitrary",)),
    )(*[x for _, x in ins], *fulls)
    return res


def _colsum(v):
    return jnp.sum(v, axis=0, keepdims=True)


def _rowmean(v):
    return jnp.mean(v, axis=-1, keepdims=True)


def _rms_fwd(x, g):
    r = lax.rsqrt(_rowmean(x * x) + EPS)
    return x * r * g


def _rms_bwd(x, g, dy):
    r = lax.rsqrt(_rowmean(x * x) + EPS)
    xh = x * r
    dxh = dy * g
    return r * (dxh - xh * _rowmean(dxh * xh)), _colsum(dy * xh)


def _sigmoid(v):
    return 0.5 * jnp.tanh(0.5 * v) + 0.5


def _gelu(y):
    c = np.sqrt(2.0 / np.pi).astype(np.float32)
    return 0.5 * y * (1.0 + jnp.tanh(c * (y + 0.044715 * (y * y * y))))


def _gelu_grad(y):
    c = np.sqrt(2.0 / np.pi).astype(np.float32)
    th = jnp.tanh(c * (y + 0.044715 * (y * y * y)))
    return 0.5 * (1.0 + th) + 0.5 * y * (1.0 - th * th) * c * (1.0 + 3.0 * 0.044715 * (y * y))


def _split3(v):
    hi = v.astype(BF16)
    r1 = v - hi.astype(F32)
    mid = r1.astype(BF16)
    lo = (r1 - mid.astype(F32)).astype(BF16)
    return hi, mid, lo


def _band(tm, lo_off, hi_off):
    shape = (tm, tm + 2 * POOL_PAD)
    d = lax.broadcasted_iota(jnp.int32, shape, 1) - lax.broadcasted_iota(jnp.int32, shape, 0) - POOL_PAD
    return ((d >= lo_off) & (d < hi_off)).astype(BF16)


def _band_sum(band, ext):
    hi, mid, lo = _split3(ext)
    dot = functools.partial(jnp.dot, preferred_element_type=F32)
    return dot(band, hi) + dot(band, mid) + dot(band, lo)


def _window_count(r0, tm, half, T):
    t = r0 + lax.broadcasted_iota(jnp.int32, (tm, 1), 0)
    return (jnp.minimum(t + half, T) - jnp.maximum(t - half, 0)).astype(F32)


def _halo_specs(tm, C, T):
    per = tm // POOL_PAD
    last = T // POOL_PAD - 1
    return [pl.BlockSpec((POOL_PAD, C), lambda i: (jnp.maximum(i * per - 1, 0), 0)),
            pl.BlockSpec((tm, C), lambda i: (i, 0)),
            pl.BlockSpec((POOL_PAD, C), lambda i: (jnp.minimum((i + 1) * per, last), 0))]


def _with_halo(before_ref, main_ref, after_ref, i, n):
    before = jnp.where(i > 0, before_ref[...], 0.0)
    after = jnp.where(i < n - 1, after_ref[...], 0.0)
    return jnp.concatenate([before, main_ref[...], after], axis=0)


def _pool_block(ext, ctr, w_ref, r0, tm, T):
    pooled, conc = [], []
    for gi, w in enumerate(POOL_WINDOWS):
        half = w // 2
        cols = slice(gi * POOL_GROUP, (gi + 1) * POOL_GROUP)
        ws = _band_sum(_band(tm, -half, half), ext[:, cols])
        p = ws / _window_count(r0, tm, half, T) - ctr[:, cols]
        pooled.append(p)
        conc.append(jnp.dot(p.astype(BF16), w_ref[gi], preferred_element_type=F32))
    return pooled, jnp.concatenate(conc, axis=1)


def _pool_fwd(u, pool_w_bf, pool_scale, g_pool, tm):
    T = u.shape[0]
    nw = len(POOL_WINDOWS)
    C = nw * POOL_GROUP
    n = T // tm

    def kern(ub_ref, u_ref, ua_ref, w_ref, sc_ref, g_ref, o_ref):
        i = pl.program_id(0)
        r0 = pl.multiple_of(i * tm, tm)
        _, conc = _pool_block(_with_halo(ub_ref, u_ref, ua_ref, i, n), u_ref[...], w_ref, r0, tm, T)
        o_ref[...] = _rms_fwd(conc * sc_ref[...], g_ref[...]).astype(o_ref.dtype)

    return pl.pallas_call(
        kern, name="pool_fwd", grid=(n,),
        in_specs=_halo_specs(tm, C, T) + [pl.BlockSpec(pool_w_bf.shape, lambda i: (0, 0, 0)),
                                          pl.BlockSpec((1, C), lambda i: (0, 0)), pl.BlockSpec((1, C), lambda i: (0, 0))],
        out_specs=pl.BlockSpec((tm, C), lambda i: (i, 0)),
        out_shape=jax.ShapeDtypeStruct((T, C), BF16),
        compiler_params=_params(("arbitrary",)),
    )(u, u, u, pool_w_bf, pool_scale, g_pool)


def _pool_bwd_rows(u, d_y, pool_w_bf, pool_scale, g_pool, tm):
    T = u.shape[0]
    nw = len(POOL_WINDOWS)
    C = nw * POOL_GROUP
    n = T // tm

    def kern(ub_ref, u_ref, ua_ref, dy_ref, w_ref, sc_ref, g_ref, q_ref, dp_ref, dw_ref, dsc_ref, dg_ref):
        i = pl.program_id(0)
        r0 = pl.multiple_of(i * tm, tm)
        pooled, conc = _pool_block(_with_halo(ub_ref, u_ref, ua_ref, i, n), u_ref[...], w_ref, r0, tm, T)
        sc = sc_ref[...]
        dyp, dg = _rms_bwd(conc * sc, g_ref[...], dy_ref[...])
        dsc = _colsum(dyp * conc)
        dconc = (dyp * sc).astype(BF16)
        dws = []
        for gi, w in enumerate(POOL_WINDOWS):
            cols = slice(gi * POOL_GROUP, (gi + 1) * POOL_GROUP)
            dc = dconc[:, cols]
            dp = lax.dot_general(dc, w_ref[gi], (((1,), (1,)), ((), ())), preferred_element_type=F32)
            dp_ref[:, cols] = dp
            q_ref[:, cols] = dp / _window_count(r0, tm, w // 2, T)
            dws.append(lax.dot_general(pooled[gi].astype(BF16), dc, (((0,), (0,)), ((), ())), preferred_element_type=F32))

        @pl.when(i == 0)
        def _():
            for gi in range(nw):
                dw_ref[gi] = dws[gi]
            dsc_ref[...] = dsc
            dg_ref[...] = dg

        @pl.when(i != 0)
        def _():
            for gi in range(nw):
                dw_ref[gi] += dws[gi]
            dsc_ref[...] += dsc
            dg_ref[...] += dg

    full2 = pl.BlockSpec((1, C), lambda i: (0, 0))
    row_blk = pl.BlockSpec((tm, C), lambda i: (i, 0))
    return pl.pallas_call(
        kern, name="pool_bwd_rows", grid=(n,),
        in_specs=_halo_specs(tm, C, T) + [row_blk, pl.BlockSpec(pool_w_bf.shape, lambda i: (0, 0, 0)), full2, full2],
        out_specs=[row_blk, row_blk, pl.BlockSpec((nw, POOL_GROUP, POOL_GROUP), lambda i: (0, 0, 0)), full2, full2],
        out_shape=[jax.ShapeDtypeStruct((T, C), F32), jax.ShapeDtypeStruct((T, C), F32),
                   jax.ShapeDtypeStruct((nw, POOL_GROUP, POOL_GROUP), F32),
                   jax.ShapeDtypeStruct((1, C), F32), jax.ShapeDtypeStruct((1, C), F32)],
        compiler_params=_params(("arbitrary",)),
    )(u, u, u, d_y, pool_w_bf, pool_scale, g_pool)


def _pool_bwd_band(q, dpooled, du_other, after, tm):
    T, C = dpooled.shape
    C2 = du_other.shape[1]
    n = T // tm

    def kern(qb_ref, q_ref, qa_ref, dp_ref, other_ref, after_ref, o_ref):
        ext = _with_halo(qb_ref, q_ref, qa_ref, pl.program_id(0), n)
        for gi, w in enumerate(POOL_WINDOWS):
            half = w // 2
            cols = slice(gi * POOL_GROUP, (gi + 1) * POOL_GROUP)
            du = _band_sum(_band(tm, -half + 1, half + 1), ext[:, cols]) - dp_ref[:, cols]
            o_ref[:, cols] = du.astype(o_ref.dtype)
        o_ref[:, C:] = other_ref[...].astype(o_ref.dtype)

    return pl.pallas_call(
        kern, name="pool_bwd_band", grid=(n,),
        in_specs=_halo_specs(tm, C, T) + [pl.BlockSpec((tm, C), lambda i: (i, 0)), pl.BlockSpec((tm, C2), lambda i: (i, 0)),
                                          pl.BlockSpec(after.shape, lambda i: (0, 0))],
        out_specs=pl.BlockSpec((tm, C + C2), lambda i: (i, 0)),
        out_shape=jax.ShapeDtypeStruct((T, C + C2), BF16),
        compiler_params=_params(("arbitrary",)),
    )(q, q, q, dpooled, du_other, after)


def _to_chunk_rows(a):
    T, C = a.shape
    return a.reshape(N_CHUNK, T // N_CHUNK, C).transpose(1, 0, 2).reshape(T, C)


def _from_chunk_rows(a):
    T, C = a.shape
    return a.reshape(T // N_CHUNK, N_CHUNK, C).transpose(1, 0, 2).reshape(T, C)


def _cmul(ar, ai, br, bi):
    return ar * br - ai * bi, ar * bi + ai * br


def _ssm_discretise(log_neg_a_re, a_im, log_dt):
    dt = jnp.exp(log_dt)
    a_re = -jnp.exp(log_neg_a_re)
    mag = jnp.exp(a_re * dt)
    ang = a_im * dt
    lam_re, lam_im = mag * jnp.cos(ang), mag * jnp.sin(ang)
    den = a_re * a_re + a_im * a_im
    f_re = ((lam_re - 1.0) * a_re + lam_im * a_im) / den
    f_im = (lam_im * a_re - (lam_re - 1.0) * a_im) / den
    return lam_re, lam_im, f_re, f_im


def _ssm_params_fwd(log_neg_a_re, a_im, log_dt):
    rows, n = log_neg_a_re.shape

    def kern(a_ref, b_ref, c_ref, o1, o2, o3, o4):
        for o, v in zip((o1, o2, o3, o4), _ssm_discretise(a_ref[...], b_ref[...], c_ref[...])):
            o[...] = v

    return pl.pallas_call(kern, name="ssm_params_fwd", out_shape=[jax.ShapeDtypeStruct((rows, n), F32)] * 4)(
        log_neg_a_re, a_im, log_dt)


def _ssm_params_bwd(log_neg_a_re, a_im, log_dt, cots):
    rows, n = log_neg_a_re.shape

    def kern(a_ref, b_ref, c_ref, g1, g2, g3, g4, o1, o2, o3):
        _, vjp = jax.vjp(_ssm_discretise, a_ref[...], b_ref[...], c_ref[...])
        d1, d2, d3 = vjp((g1[...], g2[...], g3[...], g4[...]))
        o1[...] = d1
        o2[...] = d2
        o3[...] = d3

    return pl.pallas_call(
        kern, name="ssm_params_bwd",
        out_shape=[jax.ShapeDtypeStruct((rows, n), F32), jax.ShapeDtypeStruct((rows, n), F32),
                   jax.ShapeDtypeStruct((rows, 1), F32)])(log_neg_a_re, a_im, log_dt, *cots)


def _cpow(lr, li, n):
    out = None
    br, bi = lr, li
    while n:
        if n & 1:
            out = (br, bi) if out is None else _cmul(out[0], out[1], br, bi)
        n >>= 1
        if n:
            br, bi = _cmul(br, bi, br, bi)
    return out


def _slab(t):
    return pl.ds(pl.multiple_of(t * N_CHUNK, N_CHUNK), N_CHUNK)


def _steps(n, step, carry):
    main = n // SCAN_UNROLL

    def body(i, c):
        for r in range(SCAN_UNROLL):
            c = step(i * SCAN_UNROLL + r, c)
        return c

    carry = lax.fori_loop(0, main, body, carry)
    for k in range(main * SCAN_UNROLL, n):
        carry = step(jnp.int32(k), carry)
    return carry


def _shift_chunks(vr, vi, reverse):
    sub = lax.broadcasted_iota(jnp.int32, vr.shape, 0)
    if reverse:
        keep = sub != N_CHUNK - 1
        return jnp.where(keep, pltpu.roll(vr, N_CHUNK - 1, 0), 0.0), jnp.where(keep, pltpu.roll(vi, N_CHUNK - 1, 0), 0.0)
    keep = sub != 0
    return jnp.where(keep, pltpu.roll(vr, 1, 0), 0.0), jnp.where(keep, pltpu.roll(vi, 1, 0), 0.0)


def _chunk_scan(xr, xi, lr, li, Lc, reverse, prev=None):
    C = xr.shape[1]
    lrb, lib = jnp.broadcast_to(lr, (N_CHUNK, C)), jnp.broadcast_to(li, (N_CHUNK, C))
    zero = jnp.zeros((N_CHUNK, C), F32)

    def step_of(k):
        return (Lc - 1 - k) if reverse else k

    def advance(sr, si, t):
        return lrb * sr - lib * si + xr[_slab(t), :], lrb * si + lib * sr + xi[_slab(t), :]

    def pass1(k, c):
        return advance(c[0], c[1], step_of(k))

    er, ei = _steps(Lc, pass1, (zero, zero))
    pr, pi = _cpow(lr, li, Lc)
    prb, pib = jnp.broadcast_to(pr, (N_CHUNK, C)), jnp.broadcast_to(pi, (N_CHUNK, C))
    cr, ci = zero, zero
    for _ in range(N_CHUNK - 1):
        cr, ci = _shift_chunks(er + prb * cr - pib * ci, ei + prb * ci + pib * cr, reverse)

    if prev is None:
        def pass2(k, c):
            t = step_of(k)
            nr, ni = advance(c[0], c[1], t)
            xr[_slab(t), :] = nr
            xi[_slab(t), :] = ni
            return nr, ni

        _steps(Lc, pass2, (cr, ci))
        return None

    qr, qi = prev
    def pair(ar, ai, wr, wi, acc):
        return acc[0] + ar * wr + ai * wi, acc[1] + ai * wr - ar * wi

    def pass2(k, c):
        sr, si, accr, acci = c
        t = step_of(k)
        nr, ni = advance(sr, si, t)
        xr[_slab(t), :] = nr
        xi[_slab(t), :] = ni
        tp = t - 1 if reverse else t + 1
        accr, acci = pair(nr, ni, qr[_slab(tp), :], qi[_slab(tp), :], (accr, acci))
        return nr, ni, accr, acci

    sr, si, accr, acci = _steps(Lc - 1, pass2, (cr, ci, zero, zero))
    t = step_of(Lc - 1)
    nr, ni = advance(sr, si, t)
    xr[_slab(t), :] = nr
    xi[_slab(t), :] = ni
    edge = step_of(0)
    wr, wi = _shift_chunks(qr[_slab(edge), :], qi[_slab(edge), :], not reverse)
    return pair(nr, ni, wr, wi, (accr, acci))


def _ssm_dims(T):
    assert T % ROW_BLOCK == 0
    return T // N_CHUNK, T // ROW_BLOCK


def _group_mask():
    shape = (SSM_BLOCK_GROUPS * SSM_GROUP, SSM_BLOCK_GROUPS * SSM_STATE)
    rows = lax.broadcasted_iota(jnp.int32, shape, 0) >> GROUP_SHIFT
    cols = lax.broadcasted_iota(jnp.int32, shape, 1) >> STATE_SHIFT
    return rows == cols


def _expand_groups(compact):
    return jnp.where(_group_mask(), jnp.tile(compact, (SSM_BLOCK_GROUPS, 1)), 0.0).astype(BF16)


def _collect_groups(block):
    kept = jnp.where(_group_mask(), block, 0.0)
    out = kept[0:SSM_GROUP]
    for g in range(1, SSM_BLOCK_GROUPS):
        out = out + kept[g * SSM_GROUP:(g + 1) * SSM_GROUP]
    return out


def _ssm_specs(T, direction):
    SC = SSM_BLOCK_GROUPS * SSM_STATE
    col = pl.BlockSpec((T, LANES), lambda m: (0, m))
    mat = pl.BlockSpec((None, None, SSM_GROUP, SC), lambda m: (direction, m, 0, 0))
    vec = pl.BlockSpec((None, None, 1, SC), lambda m: (direction, m, 0, 0))
    return col, mat, vec


def _ssm_project_in(u_ref, bre, bim, xr, xi, nblk):
    for blk in range(nblk):
        rows = pl.ds(blk * ROW_BLOCK, ROW_BLOCK)
        ub = u_ref[rows, :].astype(BF16)
        xr[rows, :] = jnp.dot(ub, bre, preferred_element_type=F32)
        xi[rows, :] = jnp.dot(ub, bim, preferred_element_type=F32)


def _ssm_fwd(name, u_bf, mats, vecs, direction):
    T, W = u_bf.shape
    SC = SSM_BLOCK_GROUPS * SSM_STATE
    nb = W // LANES
    Lc, nblk = _ssm_dims(T)
    reverse = bool(direction)

    def kern(u_ref, bre_ref, bim_ref, cre_ref, cim_ref, lr_ref, li_ref, fr_ref, fi_ref, y_ref, xr, xi):
        _ssm_project_in(u_ref, _expand_groups(bre_ref[...]), _expand_groups(bim_ref[...]), xr, xi, nblk)
        _chunk_scan(xr, xi, lr_ref[...], li_ref[...], Lc, reverse)
        fr, fi = fr_ref[...], fi_ref[...]
        cre, cim = _expand_groups(cre_ref[...]), _expand_groups(cim_ref[...])
        for blk in range(nblk):
            rows = pl.ds(blk * ROW_BLOCK, ROW_BLOCK)
            sr, si = _cmul(fr, fi, xr[rows, :], xi[rows, :])
            y_ref[rows, :] = (lax.dot_general(sr.astype(BF16), cre, _NT, preferred_element_type=F32)
                              - lax.dot_general(si.astype(BF16), cim, _NT, preferred_element_type=F32))

    col, mat, vec = _ssm_specs(T, direction)
    return pl.pallas_call(
        kern, name=name, grid=(nb,),
        in_specs=[col, mat, mat, mat, mat, vec, vec, vec, vec],
        out_specs=col, out_shape=jax.ShapeDtypeStruct((T, W), F32),
        scratch_shapes=[pltpu.VMEM((T, SC), F32), pltpu.VMEM((T, SC), F32)],
        compiler_params=_params(("arbitrary",)),
    )(u_bf, *mats, *vecs)


def _ssm_bwd(name, u_bf, dy_bf, du_in, mats, vecs, direction):
    T, W = u_bf.shape
    SC = SSM_BLOCK_GROUPS * SSM_STATE
    nb = W // LANES
    Lc, nblk = _ssm_dims(T)
    reverse = bool(direction)

    def kern(u_ref, dy_ref, dui_ref, bre_ref, bim_ref, cre_ref, cim_ref, lr_ref, li_ref, fr_ref, fi_ref,
             du_ref, dbre_ref, dbim_ref, dcre_ref, dcim_ref, dlr_ref, dli_ref, dfr_ref, dfi_ref,
             xr, xi, gr, gi):
        lr, li, fr, fi = lr_ref[...], li_ref[...], fr_ref[...], fi_ref[...]
        bre, bim = _expand_groups(bre_ref[...]), _expand_groups(bim_ref[...])
        cre, cim = _expand_groups(cre_ref[...]), _expand_groups(cim_ref[...])
        _ssm_project_in(u_ref, bre, bim, xr, xi, nblk)
        _chunk_scan(xr, xi, lr, li, Lc, reverse)
        mre = jnp.zeros((LANES, SC), F32)
        mim = jnp.zeros((LANES, SC), F32)
        dfr = jnp.zeros((1, SC), F32)
        dfi = jnp.zeros((1, SC), F32)
        for blk in range(nblk):
            rows = pl.ds(blk * ROW_BLOCK, ROW_BLOCK)
            dy = dy_ref[rows, :].astype(BF16)
            x_r, x_i = xr[rows, :], xi[rows, :]
            g_r = jnp.dot(dy, cre, preferred_element_type=F32)
            g_i = -jnp.dot(dy, cim, preferred_element_type=F32)
            mre += lax.dot_general(dy, x_r.astype(BF16), _TN, preferred_element_type=F32)
            mim += lax.dot_general(dy, x_i.astype(BF16), _TN, preferred_element_type=F32)
            dfr += _colsum(g_r * x_r + g_i * x_i)
            dfi += _colsum(g_i * x_r - g_r * x_i)
            gr[rows, :] = fr * g_r + fi * g_i
            gi[rows, :] = fr * g_i - fi * g_r
        dcre_ref[...] = _collect_groups(fr * mre - fi * mim)
        dcim_ref[...] = _collect_groups(-(fr * mim + fi * mre))
        dfr_ref[...] = dfr
        dfi_ref[...] = dfi
        accr, acci = _chunk_scan(gr, gi, lr, -li, Lc, not reverse, prev=(xr, xi))
        dlr_ref[...] = _colsum(accr)
        dli_ref[...] = _colsum(acci)
        dbre = jnp.zeros((LANES, SC), F32)
        dbim = jnp.zeros((LANES, SC), F32)
        for blk in range(nblk):
            rows = pl.ds(blk * ROW_BLOCK, ROW_BLOCK)
            ub = u_ref[rows, :].astype(BF16)
            a_r, a_i = gr[rows, :].astype(BF16), gi[rows, :].astype(BF16)
            dbre += lax.dot_general(ub, a_r, _TN, preferred_element_type=F32)
            dbim += lax.dot_general(ub, a_i, _TN, preferred_element_type=F32)
            du_ref[rows, :] = (dui_ref[rows, :] + lax.dot_general(a_r, bre, _NT, preferred_element_type=F32)
                               + lax.dot_general(a_i, bim, _NT, preferred_element_type=F32))
        dbre_ref[...] = _collect_groups(dbre)
        dbim_ref[...] = _collect_groups(dbim)

    col, mat, vec = _ssm_specs(T, direction)
    mat_out = pl.BlockSpec((None, SSM_GROUP, SC), lambda m: (m, 0, 0))
    vec_out = pl.BlockSpec((None, 1, SC), lambda m: (m, 0, 0))
    mat_shape = jax.ShapeDtypeStruct((nb, SSM_GROUP, SC), F32)
    vec_shape = jax.ShapeDtypeStruct((nb, 1, SC), F32)
    return pl.pallas_call(
        kern, name=name, grid=(nb,),
        in_specs=[col, col, col, mat, mat, mat, mat, vec, vec, vec, vec],
        out_specs=[col, mat_out, mat_out, mat_out, mat_out, vec_out, vec_out, vec_out, vec_out],
        out_shape=[jax.ShapeDtypeStruct((T, W), F32), mat_shape, mat_shape, mat_shape, mat_shape,
                   vec_shape, vec_shape, vec_shape, vec_shape],
        scratch_shapes=[pltpu.VMEM((T, SC), F32)] * 4,
        compiler_params=_params(("arbitrary",)),
    )(u_bf, dy_bf, du_in, *mats, *vecs)


def _groups_side_by_side(p, channel_axis):
    d, G = p.shape[:2]
    nb = G // SSM_BLOCK_GROUPS
    v = p.reshape(d, nb, SSM_BLOCK_GROUPS, p.shape[2], p.shape[3])
    v = v.transpose(0, 1, 4, 2, 3) if channel_axis == 3 else v.transpose(0, 1, 3, 2, 4)
    return v.reshape(d, nb, SSM_GROUP, SSM_BLOCK_GROUPS * SSM_STATE)


def _groups_apart(m, channel_axis):
    d, nb = m.shape[:2]
    v = m.reshape(d, nb, SSM_GROUP, SSM_BLOCK_GROUPS, SSM_STATE)
    v = v.transpose(0, 1, 3, 4, 2) if channel_axis == 3 else v.transpose(0, 1, 3, 2, 4)
    return v.reshape((d, nb * SSM_BLOCK_GROUPS) + v.shape[3:])


def _conv_taps(ref, c, R, T):
    n = T // R
    r0 = pl.multiple_of(c * R, R)
    main = ref[pl.ds(r0, R), :]
    before = ref[pl.ds(pl.multiple_of(jnp.maximum(r0 - 8, 0), 8), 8), :]
    after = ref[pl.ds(pl.multiple_of(jnp.minimum(r0 + R, T - 8), 8), 8), :]
    ext = jnp.concatenate([jnp.where(c > 0, before, 0.0), main, jnp.where(c < n - 1, after, 0.0)], axis=0)
    m1 = pltpu.roll(ext, 1, 0)[8:8 + R]
    p1 = pltpu.roll(ext, R + 15, 0)[8:8 + R]
    return m1, main, p1, r0


def _conv_specs(T, F, cb):
    nj = F // cb
    lo = lambda rows: pl.BlockSpec((rows, cb), lambda j: (0, j))
    hi = lambda rows: pl.BlockSpec((rows, cb), lambda j: (0, j + nj))
    return nj, lo, hi


def _convact_fwd(up, conv_w, conv_b, F):
    T = up.shape[0]
    cb = _pick(F, (256, 128))
    R = _pick(T, (256, 128, 64))
    nj, lo, hi = _conv_specs(T, F, cb)

    def kern(uv, ug, wv, wg, bv, bg, o_ref):
        def body(c, carry):
            v1, v0, v2, r0 = _conv_taps(uv, c, R, T)
            g1, g0, g2, _ = _conv_taps(ug, c, R, T)
            val = v1 * wv[0:1, :] + v0 * wv[1:2, :] + v2 * wv[2:3, :] + bv[...]
            gate = g1 * wg[0:1, :] + g0 * wg[1:2, :] + g2 * wg[2:3, :] + bg[...]
            o_ref[pl.ds(r0, R), :] = (val * (gate * _sigmoid(gate))).astype(o_ref.dtype)
            return carry

        lax.fori_loop(0, T // R, body, 0)

    return pl.pallas_call(
        kern, name="convact_fwd", grid=(nj,),
        in_specs=[lo(T), hi(T), lo(3), hi(3), lo(1), hi(1)],
        out_specs=lo(T), out_shape=jax.ShapeDtypeStruct((T, F), BF16),
        compiler_params=_params(("arbitrary",)),
    )(up, up, conv_w, conv_w, conv_b, conv_b)


def _convact_bwd(up, dact, conv_w, conv_b, F):
    T = up.shape[0]
    cb = _pick(F, (256, 128))
    R = _pick(T, (256, 128, 64))
    nj, lo, hi = _conv_specs(T, F, cb)

    def kern(uv, ug, da, wv, wg, bv, bg, dup, dwv, dwg, dbv, dbg, sv, sg):
        zero = jnp.zeros((1, cb), F32)

        def pass_a(c, acc):
            v1, v0, v2, r0 = _conv_taps(uv, c, R, T)
            g1, g0, g2, _ = _conv_taps(ug, c, R, T)
            val = v1 * wv[0:1, :] + v0 * wv[1:2, :] + v2 * wv[2:3, :] + bv[...]
            gate = g1 * wg[0:1, :] + g0 * wg[1:2, :] + g2 * wg[2:3, :] + bg[...]
            sig = _sigmoid(gate)
            d = da[pl.ds(r0, R), :]
            dval = d * (gate * sig)
            dgate = d * val * (sig * (1.0 + gate * (1.0 - sig)))
            sv[pl.ds(r0, R), :] = dval
            sg[pl.ds(r0, R), :] = dgate
            terms = (dval * v1, dval * v0, dval * v2, dval, dgate * g1, dgate * g0, dgate * g2, dgate)
            return tuple(a + _colsum(t) for a, t in zip(acc, terms))

        acc = lax.fori_loop(0, T // R, pass_a, (zero,) * 8)
        for k in range(3):
            dwv[k:k + 1, :] = acc[k]
            dwg[k:k + 1, :] = acc[4 + k]
        dbv[...] = acc[3]
        dbg[...] = acc[7]

        def pass_b(c, carry):
            v1, v0, v2, r0 = _conv_taps(sv, c, R, T)
            g1, g0, g2, _ = _conv_taps(sg, c, R, T)
            dup[0, pl.ds(r0, R), :] = (v2 * wv[0:1, :] + v0 * wv[1:2, :] + v1 * wv[2:3, :]).astype(dup.dtype)
            dup[1, pl.ds(r0, R), :] = (g2 * wg[0:1, :] + g0 * wg[1:2, :] + g1 * wg[2:3, :]).astype(dup.dtype)
            return carry

        lax.fori_loop(0, T // R, pass_b, 0)

    dup, dwv, dwg, dbv, dbg = pl.pallas_call(
        kern, name="convact_bwd", grid=(nj,),
        in_specs=[lo(T), hi(T), lo(T), lo(3), hi(3), lo(1), hi(1)],
        out_specs=[pl.BlockSpec((2, T, cb), lambda j: (0, 0, j)), lo(3), lo(3), lo(1), lo(1)],
        out_shape=[jax.ShapeDtypeStruct((2, T, F), BF16),
                   jax.ShapeDtypeStruct((3, F), F32), jax.ShapeDtypeStruct((3, F), F32),
                   jax.ShapeDtypeStruct((1, F), F32), jax.ShapeDtypeStruct((1, F), F32)],
        scratch_shapes=[pltpu.VMEM((T, cb), F32), pltpu.VMEM((T, cb), F32)],
        compiler_params=_params(("arbitrary",)),
    )(up, up, dact, conv_w, conv_w, conv_b, conv_b)
    return dup, (dwv, dwg), jnp.concatenate([dbv, dbg], axis=1)


def _peers(x, y, c, with_self=False):
    out = []
    for k in range(0 if with_self else 1, N_DEV):
        px = 1 - x if k & 4 else x
        py = 1 - y if k & 2 else y
        pc = 1 - c if k & 1 else c
        out.append((k, (px, py, pc), 4 * px + 2 * py + pc))
    return out


def _exchange_start(name, bufs, modes, after):
    n = len(bufs)
    ncopy = n * N_DEV
    lands = [lax.empty((N_DEV,) + tuple(b.shape[-2:]), b.dtype) for b in bufs]

    def body(*refs):
        ins, land_in = refs[:n], refs[n:2 * n]
        send_sems, recv_sems = refs[2 * n + 1], refs[2 * n + 2]
        token = refs[-1]
        x, y, c = lax.axis_index("x"), lax.axis_index("y"), lax.axis_index("c")
        me = 4 * x + 2 * y + c
        for b in range(n):
            for k, peer, slot in _peers(x, y, c, with_self=True):
                sem = b * N_DEV + k
                pltpu.make_async_remote_copy(
                    src_ref=ins[b] if modes[b] == 'gather' else ins[b].at[slot], dst_ref=land_in[b].at[me],
                    send_sem=send_sems.at[sem], recv_sem=recv_sems.at[sem],
                    device_id=peer, device_id_type=MESH_ID).start()
        token[...] = jnp.zeros_like(token)

    hbm = pl.BlockSpec(memory_space=pltpu.HBM)
    sem_spec = pl.BlockSpec(memory_space=pltpu.SEMAPHORE)
    operands = [pltpu.with_memory_space_constraint(a, pltpu.HBM) for a in list(bufs) + lands]
    res = pl.pallas_call(
        body, name=name,
        out_shape=(pltpu.SemaphoreType.DMA((ncopy,)), pltpu.SemaphoreType.DMA((ncopy,)),
                   *[pltpu.HBM(a.shape, a.dtype) for a in operands], jax.ShapeDtypeStruct((8, LANES), F32)),
        in_specs=[hbm] * (2 * n) + [pl.BlockSpec(memory_space=pl.ANY)],
        out_specs=(sem_spec, sem_spec, *([hbm] * (2 * n)), pl.BlockSpec(memory_space=pltpu.VMEM)),
        input_output_aliases={i: 2 + i for i in range(2 * n)},
        compiler_params=pltpu.CompilerParams(has_side_effects=pltpu.SideEffectType.DATAFLOW_SIDE_EFFECTING),
    )(*operands, after)
    return (res[0], res[1], list(res[2:2 + n]), list(res[2 + n:2 + 2 * n]), tuple(modes)), res[-1]


def _exchange_wait(name, handle, after):
    send_sems, recv_sems, srcs, lands, modes = handle
    n = len(srcs)

    def body(*refs):
        src_in, land_in = refs[:n], refs[n:2 * n]
        send_ref, recv_ref = refs[2 * n], refs[2 * n + 1]
        x, y, c = lax.axis_index("x"), lax.axis_index("y"), lax.axis_index("c")
        for b in range(n):
            for k, peer, slot in _peers(x, y, c, with_self=True):
                sem = b * N_DEV + k
                copy = pltpu.make_async_remote_copy(
                    src_ref=src_in[b] if modes[b] == 'gather' else src_in[b].at[slot], dst_ref=land_in[b].at[slot],
                    send_sem=send_ref.at[sem], recv_sem=recv_ref.at[sem],
                    device_id=peer, device_id_type=MESH_ID)
                copy.wait_send()
                copy.wait_recv()

    hbm = pl.BlockSpec(memory_space=pltpu.HBM)
    sem_spec = pl.BlockSpec(memory_space=pltpu.SEMAPHORE)
    res = pl.pallas_call(
        body, name=name,
        out_shape=tuple(pltpu.HBM(a.shape, a.dtype) for a in srcs + lands),
        in_specs=[hbm] * (2 * n) + [sem_spec, sem_spec, pl.BlockSpec(memory_space=pl.ANY)],
        out_specs=tuple([hbm] * (2 * n)),
        input_output_aliases={i: i for i in range(2 * n)},
        compiler_params=pltpu.CompilerParams(has_side_effects=pltpu.SideEffectType.DATAFLOW_SIDE_EFFECTING),
    )(*srcs, *lands, send_sems, recv_sems, after)
    return list(res[n:])


def _adamw(w, g, m, v):
    m2 = ADAM_B1 * m + (1.0 - ADAM_B1) * g
    v2 = ADAM_B2 * v + (1.0 - ADAM_B2) * (g * g)
    m_hat = m2 / (1.0 - ADAM_B1 ** ADAM_STEP)
    v_hat = v2 / (1.0 - ADAM_B2 ** ADAM_STEP)
    return -ADAM_LR * (m_hat / (jnp.sqrt(v_hat) + ADAM_EPS) + ADAM_WD * w), m2, v2


def _sum_adamw(name, landed, w, m, v):
    R, C = w.shape
    row_bytes = N_DEV * (-(-C // LANES) * LANES) * landed.dtype.itemsize
    tiles = [d for d in range(16, R, 16) if R % d == 0 and d * row_bytes <= ADAM_TILE_BYTES]
    tr = max(tiles) if tiles and R * row_bytes > ADAM_TILE_BYTES else R

    def kern(x_ref, w_ref, m_ref, v_ref, g_out, d_out, m_out, v_out):
        g = x_ref[0].astype(F32)
        for q in range(1, N_DEV):
            g = g + x_ref[q].astype(F32)
        g_out[...] = g
        d_out[...], m_out[...], v_out[...] = _adamw(w_ref[...], g, m_ref[...], v_ref[...])

    blk = pl.BlockSpec((tr, C), lambda i: (i, 0))
    return pl.pallas_call(
        kern, name=name, grid=(R // tr,),
        in_specs=[pl.BlockSpec((N_DEV, tr, C), lambda i: (0, i, 0)), blk, blk, blk],
        out_specs=[blk] * 4, out_shape=[jax.ShapeDtypeStruct((R, C), F32)] * 4,
        compiler_params=_params(("arbitrary",)),
    )(landed, w, m, v)


def _sum_adamw_many(name, items):
    n = len(items)

    def kern(*refs):
        ins, outs = refs[:4 * n], refs[4 * n:]
        for k in range(n):
            x_ref, w_ref, m_ref, v_ref = ins[4 * k:4 * k + 4]
            g = x_ref[0].astype(F32)
            for q in range(1, N_DEV):
                g = g + x_ref[q].astype(F32)
            outs[4 * k][...] = g
            outs[4 * k + 1][...], outs[4 * k + 2][...], outs[4 * k + 3][...] = _adamw(w_ref[...], g, m_ref[...], v_ref[...])

    out_shape = [jax.ShapeDtypeStruct(w.shape, F32) for _, w, _, _ in items for _ in range(4)]
    res = pl.pallas_call(kern, name=name, out_shape=out_shape)(*[a for item in items for a in item])
    return [res[4 * k:4 * k + 4] for k in range(n)]


def _view2d(a):
    if a.ndim == 1:
        return a.reshape(1, -1)
    return a.reshape(-1, a.shape[-1])


def kernel(x, norm_mix_g, w_in, pool_w, pool_scale, ssm_log_neg_a_re, ssm_a_im, ssm_log_dt, ssm_b_re, ssm_b_im, ssm_c_re, ssm_c_im, ssm_d, glu_w, glu_b, out_norm_pool_g, out_norm_ssm_g, w_out, norm_ffn_g, w_up, conv_w, conv_b, w_down, final_norm_g, loss_target, m_norm_mix_g, m_w_in, m_pool_w, m_pool_scale, m_ssm_log_neg_a_re, m_ssm_a_im, m_ssm_log_dt, m_ssm_b_re, m_ssm_b_im, m_ssm_c_re, m_ssm_c_im, m_ssm_d, m_glu_w, m_glu_b, m_out_norm_pool_g, m_out_norm_ssm_g, m_w_out, m_norm_ffn_g, m_w_up, m_conv_w, m_conv_b, m_w_down, m_final_norm_g, v_norm_mix_g, v_w_in, v_pool_w, v_pool_scale, v_ssm_log_neg_a_re, v_ssm_a_im, v_ssm_log_dt, v_ssm_b_re, v_ssm_b_im, v_ssm_c_re, v_ssm_c_im, v_ssm_d, v_glu_w, v_glu_b, v_out_norm_pool_g, v_out_norm_ssm_g, v_w_out, v_norm_ffn_g, v_w_up, v_conv_w, v_conv_b, v_w_down, v_final_norm_g):
    given = dict(locals())
    weights = {n: given[n] for n in WEIGHTS}
    mom1 = {n: given["m_" + n] for n in WEIGHTS}
    mom2 = {n: given["v_" + n] for n in WEIGHTS}

    xs = x[0]
    tgt = loss_target[0]
    T, D = xs.shape
    DP = len(POOL_WINDOWS) * POOL_GROUP
    DS = D - DP
    G = DS // SSM_GROUP
    N, H = SSM_STATE, SSM_GROUP
    F2 = w_up.shape[1] * N_DEV
    F = F2 // 2
    fs = w_up.shape[1]
    row = lambda a: a.reshape(1, -1)

    in_handle, token = _exchange_start("gather_in_start", [w_in.astype(BF16)], ['gather'], after=w_in)
    mix_handle, token = _exchange_start("gather_mix_start", [glu_w.astype(BF16), w_out.astype(BF16)], ['gather'] * 2,
                                        after=token)
    up_handle, token = _exchange_start("gather_up_start", [w_up.T.astype(BF16), conv_w], ['gather'] * 2, after=token)
    down_handle, token = _exchange_start("gather_down_start", [w_down.astype(BF16)], ['gather'], after=token)
    CB = row(conv_b)

    g1, g2, g3 = row(norm_mix_g), row(norm_ffn_g), row(final_norm_g)
    gp, gs = row(out_norm_pool_g), row(out_norm_ssm_g)
    (xn,) = _rowmap("norm_mix", _rms_fwd, [xs], [g1 + token[0:1, 0:1]], [(D, BF16)])
    W_in = _exchange_wait("gather_in_wait", in_handle, xn)[0].reshape(D, D)
    u = _mm("proj_in", xn, W_in, 'nn', F32)
    tmp = _pick(T, (256, 128))
    pool_w_bf = pool_w.astype(BF16)
    ypn = _pool_fwd(u, pool_w_bf, row(pool_scale), gp, tmp)

    u_ssm = _to_chunk_rows(u[:, DP:])
    u_ssm_bf = u_ssm.astype(BF16)
    a_rows = (ssm_log_neg_a_re.reshape(2 * G, N), ssm_a_im.reshape(2 * G, N), ssm_log_dt.reshape(2 * G, 1))
    disc = _ssm_params_fwd(*a_rows)
    nb = G // SSM_BLOCK_GROUPS
    vecs = [d.reshape(2, nb, 1, SSM_BLOCK_GROUPS * N) for d in disc]
    mats = [_groups_side_by_side(ssm_b_re, 3), _groups_side_by_side(ssm_b_im, 3),
            _groups_side_by_side(ssm_c_re, 2), _groups_side_by_side(ssm_c_im, 2)]
    y_dir = [_ssm_fwd("ssm_fwd_%d" % d, u_ssm_bf, mats, vecs, d) for d in range(2)]

    def mix_post(yf, yb, us, d, gw, gb, g):
        y = yf + yb + d * us
        z = _gelu(y)
        gate = _sigmoid(jnp.dot(z.astype(BF16), gw, preferred_element_type=F32) + gb)
        return _rms_fwd(z * gate, g), y

    mix_landed = _exchange_wait("gather_mix_wait", mix_handle, y_dir[1])
    W_glu = mix_landed[0].reshape(DS, DS)
    W_out = mix_landed[1].reshape(D, D)
    ysn, y_ssm = _rowmap("ssm_post", mix_post, [y_dir[0], y_dir[1], u_ssm], [row(ssm_d), W_glu, row(glu_b), gs],
                         [(DS, BF16), (DS, F32)])
    ycat = jnp.concatenate([ypn, _from_chunk_rows(ysn)], axis=1)
    h1, hn = _mm_rows("proj_out", ycat, W_out, 'nn', lambda h, g: (h, _rms_fwd(h, g)), [], [g2],
                      [(D, F32), (D, BF16)], add=xs)
    up_landed = _exchange_wait("gather_up_wait", up_handle, hn)
    W_up_t = up_landed[0].reshape(F2, D)
    CW = up_landed[1].transpose(1, 0, 2).reshape(3, F2)
    up = _mm("ffn_up", hn, W_up_t, 'nt', F32)
    act = _convact_fwd(up, CW, CB, F)
    W_down = _exchange_wait("gather_down_wait", down_handle, act)[0].reshape(F, D)

    def head(h, t, g):
        r = lax.rsqrt(_rowmean(h * h) + EPS)
        hh = h * r
        e = hh * g - t
        loss = 0.5 * jnp.sum(_rowmean(e * e), keepdims=True)
        dy = e * (1.0 / D)
        dxh = dy * g
        dh = r * (dxh - hh * _rowmean(dxh * hh))
        return dh, dh, jnp.broadcast_to(loss, (1, LANES)), _colsum(dy * hh)

    dh2, dh2_bf, loss_acc, dg3 = _mm_rows("ffn_down", act, W_down, 'nn', head, [tgt], [g3],
                                          [(D, F32), (D, BF16)], [(1, LANES), (1, D)], add=h1)

    dact = _mm("ffn_down_dx", dh2_bf, W_down, 'nt', F32)
    dW_down = _mm("ffn_down_dw", act, dh2_bf, 'tn', BF16)
    dup, dCW, dCB = _convact_bwd(up, dact, CW, CB, F)
    dhn_val = _mm("ffn_up_dx_val", dup, W_up_t, 'nn', F32, b_rows=(2, 0), a_plane=0)
    dW_up_t = _mm("ffn_up_dw", dup, hn, 'tn', BF16).reshape(F2, D)

    def shards(full_grad):
        return full_grad.reshape((N_DEV, full_grad.shape[0] // N_DEV) + full_grad.shape[1:])

    half = N_DEV // 2
    dCW_sh = jnp.concatenate([h.reshape(3, half, fs).transpose(1, 0, 2) for h in dCW], axis=0)
    ffn_sent = [shards(dW_down), shards(dW_up_t), dCW_sh]
    ffn_handle, token = _exchange_start("scatter_ffn_start", ffn_sent, ['scatter'] * 3, after=dhn_val)
    g2_late = g2 + token[0:1, 0:1]

    def norm_bwd_add(dy, hx, res, g):
        dx, dg = _rms_bwd(hx, g, dy)
        return dx + res, dg

    def norm_bwd_add2(dy, hx, res, g):
        dx, dg = _rms_bwd(hx, g, dy)
        return dx + res, dx + res, dg

    dh1, dh1_bf, dg2 = _mm_rows("ffn_up_dx_gate", dup, W_up_t, 'nn', norm_bwd_add2, [h1, dh2], [g2_late],
                                [(D, F32), (D, BF16)], [(1, D)], add=dhn_val, b_rows=(2, 1), a_plane=1)
    dycat = _mm("proj_out_dx", dh1_bf, W_out, 'nt', F32)
    dW_out = _mm("proj_out_dw", ycat, dh1_bf, 'tn', BF16)

    out_sent = [shards(dW_out)]
    out_handle, token = _exchange_start("scatter_out_start", out_sent, ['scatter'], after=dycat)
    q, dpooled, dpool_w, dpool_scale, dgp = _pool_bwd_rows(u, dycat, pool_w_bf, row(pool_scale),
                                                           gp + token[0:1, 0:1], tmp)

    def mix_post_bwd(y, us, dyn, d, gw, gb, g):
        z = _gelu(y)
        gz = z.astype(BF16)
        sig = _sigmoid(jnp.dot(gz, gw, preferred_element_type=F32) + gb)
        dys, dg = _rms_bwd(z * sig, g, dyn)
        dgl = dys * z * sig * (1.0 - sig)
        dgl_bf = dgl.astype(BF16)
        dz = dys * sig + lax.dot_general(dgl_bf, gw, (((1,), (1,)), ((), ())), preferred_element_type=F32)
        dgw = lax.dot_general(gz, dgl_bf, (((0,), (0,)), ((), ())), preferred_element_type=F32)
        dyv = dz * _gelu_grad(y)
        return dyv, dyv * d, dgw, _colsum(dgl), _colsum(dyv * us), dg

    dyv, du_dir, dglu_w, dglu_b, dssm_d, dgs = _rowmap(
        "ssm_post_bwd", mix_post_bwd, [y_ssm, u_ssm, _to_chunk_rows(dycat[:, DP:])], [row(ssm_d), W_glu, row(glu_b), gs],
        [(DS, BF16), (DS, F32)], [(DS, DS), (1, DS), (1, DS), (1, DS)])

    du_ssm = du_dir
    dB, dC, dvec = [], [], []
    for d in range(2):
        res = _ssm_bwd("ssm_bwd_%d" % d, u_ssm_bf, dyv, du_ssm, mats, vecs, d)
        du_ssm = res[0]
        dB.append(res[1:3])
        dC.append(res[3:5])
        dvec.append(res[5:9])
    grad_b = [_groups_apart(jnp.stack([dB[0][k], dB[1][k]]), 3) for k in range(2)]
    grad_c = [_groups_apart(jnp.stack([dC[0][k], dC[1][k]]), 2) for k in range(2)]
    cots = [jnp.stack([dvec[0][k], dvec[1][k]]).reshape(2 * G, N) for k in range(4)]
    d_a_re, d_a_im, d_log_dt = _ssm_params_bwd(*a_rows, cots)

    rep_grads = {
        'pool_w': dpool_w, 'pool_scale': dpool_scale, 'ssm_log_neg_a_re': d_a_re, 'ssm_a_im': d_a_im,
        'ssm_log_dt': d_log_dt, 'ssm_b_re': grad_b[0], 'ssm_b_im': grad_b[1], 'ssm_c_re': grad_c[0], 'ssm_c_im': grad_c[1],
        'ssm_d': dssm_d, 'glu_b': dglu_b, 'out_norm_pool_g': dgp, 'out_norm_ssm_g': dgs, 'norm_ffn_g': dg2,
        'conv_b': dCB, 'final_norm_g': dg3}
    wide = ('pool_w', 'ssm_b_re', 'ssm_b_im', 'ssm_c_re', 'ssm_c_im')

    def narrow(n):
        return weights[n].ndim == 4 and weights[n].shape[-1] < LANES // 2

    def travel(n):
        g = rep_grads[n]
        if narrow(n):
            g = jnp.swapaxes(g, 2, 3)
        if n in wide:
            return g.reshape(-1, PACK_W if g.size % PACK_W == 0 else LANES).astype(BF16)
        return _view2d(g.reshape(weights[n].shape))

    small = tuple(n for n in REPLICATED if n != 'norm_mix_g')
    small_sent = [shards(dglu_w.astype(BF16))] + [travel(n) for n in small] + [loss_acc]
    small_handle, token = _exchange_start("reduce_small_start", small_sent, ['scatter'] + ['gather'] * (len(small) + 1),
                                          after=d_a_re)

    du = _pool_bwd_band(q, dpooled, _from_chunk_rows(du_ssm), token, tmp)
    dW_in = _mm("proj_in_dw", xn, du, 'tn', BF16)
    in_handle, token = _exchange_start("scatter_in_start", [shards(dW_in)], ['scatter'], after=dW_in)
    dx, dg1 = _mm_rows("proj_in_dx", du, W_in + token[0, 0].astype(BF16), 'nt', norm_bwd_add, [xs, dh1], [g1],
                       [(D, F32)], [(1, D)])
    rep_grads['norm_mix_g'] = dg1
    last_handle, token = _exchange_start("reduce_last_start", [travel('norm_mix_g')], ['gather'], after=dx)

    grads, delta, new_m, new_v = {}, {}, {}, {}

    def update(n, landed_n):
        shape = weights[n].shape
        if narrow(n):
            swapped = lambda a: _view2d(jnp.swapaxes(a, 2, 3))
            landed_n = landed_n.reshape((N_DEV,) + swapped(weights[n]).shape)
            res = _sum_adamw("adamw_" + n, landed_n, swapped(weights[n]), swapped(mom1[n]), swapped(mom2[n]))
            back = lambda r: jnp.swapaxes(r.reshape(shape[:2] + (shape[3], shape[2])), 2, 3)
            grads[n], delta[n], new_m[n], new_v[n] = [back(r) for r in res]
            return
        if n == 'w_up':
            res = _sum_adamw("adamw_" + n, landed_n, weights[n].T, mom1[n].T, mom2[n].T)
            grads[n], delta[n], new_m[n], new_v[n] = [r.T for r in res]
            return
        landed_n = landed_n.reshape((N_DEV,) + _view2d(weights[n]).shape)
        res = _sum_adamw("adamw_" + n, landed_n, _view2d(weights[n]), _view2d(mom1[n]), _view2d(mom2[n]))
        grads[n], delta[n], new_m[n], new_v[n] = [r.reshape(shape) for r in res]

    def update_many(name, named_landed):
        items = [(l.reshape((N_DEV,) + _view2d(weights[n]).shape), _view2d(weights[n]), _view2d(mom1[n]), _view2d(mom2[n]))
                 for n, l in named_landed]
        for (n, _), res in zip(named_landed, _sum_adamw_many(name, items)):
            grads[n], delta[n], new_m[n], new_v[n] = [r.reshape(weights[n].shape) for r in res]

    ffn_landed = _exchange_wait("scatter_ffn_wait", ffn_handle, token)
    for n, l in zip(('w_down', 'w_up'), ffn_landed):
        update(n, l)
    (out_landed,) = _exchange_wait("scatter_out_wait", out_handle, token)
    update('w_out', out_landed)
    small_landed = _exchange_wait("reduce_small_wait", small_handle, token)
    tiny = [('conv_w', ffn_landed[2])]
    for n, l in zip(('glu_w',) + small, small_landed):
        if weights[n].size <= ADAM_TINY:
            tiny.append((n, l))
        else:
            update(n, l)
    update_many("adamw_tiny", tiny)

    def loss_sum(parts_ref, o_ref):
        s = parts_ref[0]
        for q in range(1, N_DEV):
            s = s + parts_ref[q]
        o_ref[...] = s

    loss = pl.pallas_call(loss_sum, name="loss_sum", out_shape=jax.ShapeDtypeStruct((1, LANES), F32))(small_landed[-1])[0, 0]

    update('w_in', _exchange_wait("scatter_in_wait", in_handle, grads[small[-1]])[0])
    update('norm_mix_g', _exchange_wait("reduce_last_wait", last_handle, grads['w_in'])[0])

    return (loss, dx[None], *[grads[n] for n in WEIGHTS], *[delta[n] for n in WEIGHTS],
            *[new_m[n] for n in WEIGHTS], *[new_v[n] for n in WEIGHTS])
```

```python
import functools

import jax
import jax.numpy as jnp
import numpy as np
from jax import lax
from jax.experimental import pallas as pl
from jax.experimental.pallas import tpu as pltpu

F32 = jnp.float32
BF16 = jnp.bfloat16
MESH_ID = pl.DeviceIdType.MESH

N_DEV = 8
EPS = 1e-6
POOL_WINDOWS = (2, 4, 8, 16)
POOL_GROUP = 128
POOL_PAD = 64
SSM_GROUP = 16
SSM_STATE = 64
SSM_BLOCK_GROUPS = 8
N_CHUNK = 8
ROW_BLOCK = 512
SCAN_UNROLL = 8
GROUP_SHIFT = 4
STATE_SHIFT = 6
_NT = (((1,), (1,)), ((), ()))
_TN = (((0,), (0,)), ((), ()))
LANES = 128
PACK_W = 1024
VMEM_LIMIT = 56 * 1024 * 1024
MM_VMEM_BUDGET = 40 * 1024 * 1024
MM_TILE_CAP = 1408
MM_ROWS_TILE = 512
ADAM_TILE_BYTES = 4 * 1024 * 1024
ADAM_TINY = 8192

ADAM_LR = 0.001
ADAM_B1 = 0.9
ADAM_B2 = 0.999
ADAM_EPS = 1e-08
ADAM_WD = 0.01
ADAM_STEP = 10

WEIGHTS = ['norm_mix_g', 'w_in', 'pool_w', 'pool_scale', 'ssm_log_neg_a_re', 'ssm_a_im', 'ssm_log_dt',
           'ssm_b_re', 'ssm_b_im', 'ssm_c_re', 'ssm_c_im', 'ssm_d', 'glu_w', 'glu_b', 'out_norm_pool_g',
           'out_norm_ssm_g', 'w_out', 'norm_ffn_g', 'w_up', 'conv_w', 'conv_b', 'w_down', 'final_norm_g']
SHARDED = ('w_in', 'w_out', 'w_down', 'glu_w', 'w_up', 'conv_w')
REPLICATED = tuple(n for n in WEIGHTS if n not in SHARDED)


def _pick(n, prefs):
    for p in prefs:
        if n % p == 0:
            return p
    return n


def _params(sem, vmem=None):
    return pltpu.CompilerParams(dimension_semantics=sem, vmem_limit_bytes=vmem or VMEM_LIMIT)


def _tiles(n, cap):
    return [d for d in range(LANES, min(n, cap) + 1, LANES) if n % d == 0] or [n]


def _mm(name, a, b, mode, out_dtype, add=None, b_rows=(1, 0), a_plane=None):
    batch = a.shape[0] if a.ndim == 3 and a_plane is None else None
    if mode == 'nn':
        (M, K), N = a.shape[-2:], b.shape[1]
        assert b.shape[0] == K * b_rows[0]
    elif mode == 'nt':
        (M, K), (N, _) = a.shape[-2:], b.shape
    else:
        (K, M), (_, N) = a.shape[-2:], b.shape
    dims = {'nn': (((1,), (0,)), ((), ())), 'nt': (((1,), (1,)), ((), ())), 'tn': (((0,), (0,)), ((), ()))}[mode]
    sa, sb, so = a.dtype.itemsize, b.dtype.itemsize, jnp.dtype(out_dtype).itemsize
    best = None
    for tm in _tiles(M, MM_TILE_CAP):
        for tn in _tiles(N, MM_TILE_CAP):
            need = 2 * (tm * K * sa + tn * K * sb + tm * tn * (so + (4 if add is not None else 0)))
            if need <= MM_VMEM_BUDGET:
                key = (tm * tn / (tm + tn), tm * tn)
                if best is None or key > best[0]:
                    best = (key, tm, tn)
    _, tm, tn = best
    rows_inner = a.size * sa * (N // tn) + b.size * sb < a.size * sa + b.size * sb * (M // tm)

    def kern(*refs):
        a_ref, b_ref = refs[:2]
        o_ref = refs[-1]
        r = lax.dot_general(a_ref[...].astype(BF16), b_ref[...].astype(BF16), dims, preferred_element_type=F32)
        if add is not None:
            r = r + refs[2][...]
        o_ref[...] = r.astype(o_ref.dtype)

    if rows_inner:
        grid = (N // tn, M // tm)
        ij = lambda g0, g1: (g1, g0)
    else:
        grid = (M // tm, N // tn)
        ij = lambda g0, g1: (g0, g1)

    def spec(block, index, batched=False):
        if batch is None:
            return pl.BlockSpec(block, lambda g0, g1: index(*ij(g0, g1)))
        if batched:
            return pl.BlockSpec((None,) + block, lambda p, g0, g1: (p,) + index(*ij(g0, g1)))
        return pl.BlockSpec(block, lambda p, g0, g1: index(*ij(g0, g1)))

    if a_plane is not None:
        assert mode != 'tn'
        a_spec = pl.BlockSpec((None, tm, K), lambda g0, g1: (a_plane, ij(g0, g1)[0], 0))
    else:
        a_spec = (spec((K, tm), lambda i, j: (0, i), True) if mode == 'tn' else spec((tm, K), lambda i, j: (i, 0), True))
    b_spec = (spec((tn, K), lambda i, j: (j, 0)) if mode == 'nt' else spec((K, tn), lambda i, j: (b_rows[1], j)))
    in_specs = [a_spec, b_spec]
    args = [a, b]
    if add is not None:
        in_specs.append(spec((tm, tn), lambda i, j: (i, j), True))
        args.append(add)
    lead = () if batch is None else (batch,)
    return pl.pallas_call(
        kern, name=name, grid=lead + grid, in_specs=in_specs,
        out_specs=spec((tm, tn), lambda i, j: (i, j), True),
        out_shape=jax.ShapeDtypeStruct(lead + (M, N), out_dtype),
        compiler_params=_params(("parallel",) * (len(lead) + 2)),
    )(*args)


def _store_rows_and_sums(i, outs, res, n_row):
    for o, v in zip(outs[:n_row], res[:n_row]):
        o[...] = v.astype(o.dtype)
    for o, v in zip(outs[n_row:], res[n_row:]):
        @pl.when(i == 0)
        def _(o=o, v=v):
            o[...] = v

        @pl.when(i != 0)
        def _(o=o, v=v):
            o[...] += v


def _mm_rows(name, a, b, mode, fn, rows, fulls, out_rows, out_accs=(), add=None, b_rows=(1, 0), a_plane=None):
    M, K = a.shape[-2:]
    N = b.shape[1] if mode == 'nn' else b.shape[0]
    assert b.shape[0] == (K * b_rows[0] if mode == 'nn' else N)
    dims = _NT if mode == 'nt' else (((1,), (0,)), ((), ()))
    tm = min(MM_ROWS_TILE, M)
    n_extra, n_row = len(rows) + len(fulls), len(out_rows)

    def kern(*refs):
        i = pl.program_id(0)
        r = lax.dot_general(refs[0][...].astype(BF16), refs[1][...].astype(BF16), dims, preferred_element_type=F32)
        pos = 2
        if add is not None:
            r = r + refs[pos][...]
            pos += 1
        res = fn(r, *[x[...] for x in refs[pos:pos + n_extra]])
        _store_rows_and_sums(i, refs[pos + n_extra:], res, n_row)

    def full_spec(shape):
        nd = len(shape)
        return pl.BlockSpec(tuple(shape), lambda i: (0,) * nd)

    a_spec = (pl.BlockSpec((tm, K), lambda i: (i, 0)) if a_plane is None
              else pl.BlockSpec((None, tm, K), lambda i: (a_plane, i, 0)))
    b_spec = pl.BlockSpec((K, N), lambda i: (b_rows[1], 0)) if mode == 'nn' else pl.BlockSpec((N, K), lambda i: (0, 0))
    ins = [_row_in(r, tm, M) for r in ([add] if add is not None else []) + list(rows)]
    outs = [_row_out(o, tm, M) for o in out_rows]
    res = pl.pallas_call(
        kern, name=name, grid=(M // tm,),
        in_specs=[a_spec, b_spec] + [s for s, _ in ins] + [full_spec(f.shape) for f in fulls],
        out_specs=[s for s, _ in outs] + [full_spec(s) for s in out_accs],
        out_shape=[sh for _, sh in outs] + [jax.ShapeDtypeStruct(tuple(s), F32) for s in out_accs],
        compiler_params=_params(("arbitrary",)),
    )(a, b, *[x for _, x in ins], *fulls)
    return res


def _row_in(r, tm, T):
    arr, w, k = r if isinstance(r, tuple) else (r, r.shape[1], 0)
    return pl.BlockSpec((tm, w), lambda i: (i, k)), arr


def _row_out(o, tm, T):
    c, dt = o
    return pl.BlockSpec((tm, c), lambda i: (i, 0)), jax.ShapeDtypeStruct((T, c), dt)


def _rowmap(name, fn, rows, fulls, out_rows, out_accs=(), tm=512):
    T = (rows[0][0] if isinstance(rows[0], tuple) else rows[0]).shape[0]
    tm = min(tm, T)
    assert T % tm == 0
    n_in, n_row = len(rows) + len(fulls), len(out_rows)

    def kern(*refs):
        i = pl.program_id(0)
        res = fn(*[r[...] for r in refs[:n_in]])
        res = res if isinstance(res, (tuple, list)) else (res,)
        _store_rows_and_sums(i, refs[n_in:], res, n_row)

    def full_spec(shape):
        nd = len(shape)
        return pl.BlockSpec(tuple(shape), lambda i: (0,) * nd)

    ins = [_row_in(r, tm, T) for r in rows]
    outs = [_row_out(o, tm, T) for o in out_rows]
    res = pl.pallas_call(
        kern, name=name, grid=(T // tm,),
        in_specs=[s for s, _ in ins] + [full_spec(f.shape) for f in fulls],
        out_specs=[s for s, _ in outs] + [full_spec(s) for s in out_accs],
        out_shape=[sh for _, sh in outs] + [jax.ShapeDtypeStruct(tuple(s), F32) for s in out_accs],
        compiler_params=_params(("arbitrary",)),
    )(*[x for _, x in ins], *fulls)
    return res


def _colsum(v):
    return jnp.sum(v, axis=0, keepdims=True)


def _rowmean(v):
    return jnp.mean(v, axis=-1, keepdims=True)


def _rms_fwd(x, g):
    r = lax.rsqrt(_rowmean(x * x) + EPS)
    return x * r * g


def _rms_bwd(x, g, dy):
    r = lax.rsqrt(_rowmean(x * x) + EPS)
    xh = x * r
    dxh = dy * g
    return r * (dxh - xh * _rowmean(dxh * xh)), _colsum(dy * xh)


def _sigmoid(v):
    return 0.5 * jnp.tanh(0.5 * v) + 0.5


def _gelu(y):
    c = np.sqrt(2.0 / np.pi).astype(np.float32)
    return 0.5 * y * (1.0 + jnp.tanh(c * (y + 0.044715 * (y * y * y))))


def _gelu_grad(y):
    c = np.sqrt(2.0 / np.pi).astype(np.float32)
    th = jnp.tanh(c * (y + 0.044715 * (y * y * y)))
    return 0.5 * (1.0 + th) + 0.5 * y * (1.0 - th * th) * c * (1.0 + 3.0 * 0.044715 * (y * y))


def _split3(v):
    hi = v.astype(BF16)
    r1 = v - hi.astype(F32)
    mid = r1.astype(BF16)
    lo = (r1 - mid.astype(F32)).astype(BF16)
    return hi, mid, lo


def _band(tm, lo_off, hi_off):
    shape = (tm, tm + 2 * POOL_PAD)
    d = lax.broadcasted_iota(jnp.int32, shape, 1) - lax.broadcasted_iota(jnp.int32, shape, 0) - POOL_PAD
    return ((d >= lo_off) & (d < hi_off)).astype(BF16)


def _band_sum(band, ext):
    hi, mid, lo = _split3(ext)
    dot = functools.partial(jnp.dot, preferred_element_type=F32)
    return dot(band, hi) + dot(band, mid) + dot(band, lo)


def _window_count(r0, tm, half, T):
    t = r0 + lax.broadcasted_iota(jnp.int32, (tm, 1), 0)
    return (jnp.minimum(t + half, T) - jnp.maximum(t - half, 0)).astype(F32)


def _halo_specs(tm, C, T):
    per = tm // POOL_PAD
    last = T // POOL_PAD - 1
    return [pl.BlockSpec((POOL_PAD, C), lambda i: (jnp.maximum(i * per - 1, 0), 0)),
            pl.BlockSpec((tm, C), lambda i: (i, 0)),
            pl.BlockSpec((POOL_PAD, C), lambda i: (jnp.minimum((i + 1) * per, last), 0))]


def _with_halo(before_ref, main_ref, after_ref, i, n):
    before = jnp.where(i > 0, before_ref[...], 0.0)
    after = jnp.where(i < n - 1, after_ref[...], 0.0)
    return jnp.concatenate([before, main_ref[...], after], axis=0)


def _pool_block(ext, ctr, w_ref, r0, tm, T):
    pooled, conc = [], []
    for gi, w in enumerate(POOL_WINDOWS):
        half = w // 2
        cols = slice(gi * POOL_GROUP, (gi + 1) * POOL_GROUP)
        ws = _band_sum(_band(tm, -half, half), ext[:, cols])
        p = ws / _window_count(r0, tm, half, T) - ctr[:, cols]
        pooled.append(p)
        conc.append(jnp.dot(p.astype(BF16), w_ref[gi], preferred_element_type=F32))
    return pooled, jnp.concatenate(conc, axis=1)


def _pool_fwd(u, pool_w_bf, pool_scale, g_pool, tm):
    T = u.shape[0]
    nw = len(POOL_WINDOWS)
    C = nw * POOL_GROUP
    n = T // tm

    def kern(ub_ref, u_ref, ua_ref, w_ref, sc_ref, g_ref, o_ref):
        i = pl.program_id(0)
        r0 = pl.multiple_of(i * tm, tm)
        _, conc = _pool_block(_with_halo(ub_ref, u_ref, ua_ref, i, n), u_ref[...], w_ref, r0, tm, T)
        o_ref[...] = _rms_fwd(conc * sc_ref[...], g_ref[...]).astype(o_ref.dtype)

    return pl.pallas_call(
        kern, name="pool_fwd", grid=(n,),
        in_specs=_halo_specs(tm, C, T) + [pl.BlockSpec(pool_w_bf.shape, lambda i: (0, 0, 0)),
                                          pl.BlockSpec((1, C), lambda i: (0, 0)), pl.BlockSpec((1, C), lambda i: (0, 0))],
        out_specs=pl.BlockSpec((tm, C), lambda i: (i, 0)),
        out_shape=jax.ShapeDtypeStruct((T, C), BF16),
        compiler_params=_params(("arbitrary",)),
    )(u, u, u, pool_w_bf, pool_scale, g_pool)


def _pool_bwd_rows(u, d_y, pool_w_bf, pool_scale, g_pool, tm):
    T = u.shape[0]
    nw = len(POOL_WINDOWS)
    C = nw * POOL_GROUP
    n = T // tm

    def kern(ub_ref, u_ref, ua_ref, dy_ref, w_ref, sc_ref, g_ref, q_ref, dp_ref, dw_ref, dsc_ref, dg_ref):
        i = pl.program_id(0)
        r0 = pl.multiple_of(i * tm, tm)
        pooled, conc = _pool_block(_with_halo(ub_ref, u_ref, ua_ref, i, n), u_ref[...], w_ref, r0, tm, T)
        sc = sc_ref[...]
        dyp, dg = _rms_bwd(conc * sc, g_ref[...], dy_ref[...])
        dsc = _colsum(dyp * conc)
        dconc = (dyp * sc).astype(BF16)
        dws = []
        for gi, w in enumerate(POOL_WINDOWS):
            cols = slice(gi * POOL_GROUP, (gi + 1) * POOL_GROUP)
            dc = dconc[:, cols]
            dp = lax.dot_general(dc, w_ref[gi], (((1,), (1,)), ((), ())), preferred_element_type=F32)
            dp_ref[:, cols] = dp
            q_ref[:, cols] = dp / _window_count(r0, tm, w // 2, T)
            dws.append(lax.dot_general(pooled[gi].astype(BF16), dc, (((0,), (0,)), ((), ())), preferred_element_type=F32))

        @pl.when(i == 0)
        def _():
            for gi in range(nw):
                dw_ref[gi] = dws[gi]
            dsc_ref[...] = dsc
            dg_ref[...] = dg

        @pl.when(i != 0)
        def _():
            for gi in range(nw):
                dw_ref[gi] += dws[gi]
            dsc_ref[...] += dsc
            dg_ref[...] += dg

    full2 = pl.BlockSpec((1, C), lambda i: (0, 0))
    row_blk = pl.BlockSpec((tm, C), lambda i: (i, 0))
    return pl.pallas_call(
        kern, name="pool_bwd_rows", grid=(n,),
        in_specs=_halo_specs(tm, C, T) + [row_blk, pl.BlockSpec(pool_w_bf.shape, lambda i: (0, 0, 0)), full2, full2],
        out_specs=[row_blk, row_blk, pl.BlockSpec((nw, POOL_GROUP, POOL_GROUP), lambda i: (0, 0, 0)), full2, full2],
        out_shape=[jax.ShapeDtypeStruct((T, C), F32), jax.ShapeDtypeStruct((T, C), F32),
                   jax.ShapeDtypeStruct((nw, POOL_GROUP, POOL_GROUP), F32),
                   jax.ShapeDtypeStruct((1, C), F32), jax.ShapeDtypeStruct((1, C), F32)],
        compiler_params=_params(("arbitrary",)),
    )(u, u, u, d_y, pool_w_bf, pool_scale, g_pool)


def _pool_bwd_band(q, dpooled, du_other, after, tm):
    T, C = dpooled.shape
    C2 = du_other.shape[1]
    n = T // tm

    def kern(qb_ref, q_ref, qa_ref, dp_ref, other_ref, after_ref, o_ref):
        ext = _with_halo(qb_ref, q_ref, qa_ref, pl.program_id(0), n)
        for gi, w in enumerate(POOL_WINDOWS):
            half = w // 2
            cols = slice(gi * POOL_GROUP, (gi + 1) * POOL_GROUP)
            du = _band_sum(_band(tm, -half + 1, half + 1), ext[:, cols]) - dp_ref[:, cols]
            o_ref[:, cols] = du.astype(o_ref.dtype)
        o_ref[:, C:] = other_ref[...].astype(o_ref.dtype)

    return pl.pallas_call(
        kern, name="pool_bwd_band", grid=(n,),
        in_specs=_halo_specs(tm, C, T) + [pl.BlockSpec((tm, C), lambda i: (i, 0)), pl.BlockSpec((tm, C2), lambda i: (i, 0)),
                                          pl.BlockSpec(after.shape, lambda i: (0, 0))],
        out_specs=pl.BlockSpec((tm, C + C2), lambda i: (i, 0)),
        out_shape=jax.ShapeDtypeStruct((T, C + C2), BF16),
        compiler_params=_params(("arbitrary",)),
    )(q, q, q, dpooled, du_other, after)


def _to_chunk_rows(a):
    T, C = a.shape
    return a.reshape(N_CHUNK, T // N_CHUNK, C).transpose(1, 0, 2).reshape(T, C)


def _from_chunk_rows(a):
    T, C = a.shape
    return a.reshape(T // N_CHUNK, N_CHUNK, C).transpose(1, 0, 2).reshape(T, C)


def _cmul(ar, ai, br, bi):
    return ar * br - ai * bi, ar * bi + ai * br


def _ssm_discretise(log_neg_a_re, a_im, log_dt):
    dt = jnp.exp(log_dt)
    a_re = -jnp.exp(log_neg_a_re)
    mag = jnp.exp(a_re * dt)
    ang = a_im * dt
    lam_re, lam_im = mag * jnp.cos(ang), mag * jnp.sin(ang)
    den = a_re * a_re + a_im * a_im
    f_re = ((lam_re - 1.0) * a_re + lam_im * a_im) / den
    f_im = (lam_im * a_re - (lam_re - 1.0) * a_im) / den
    return lam_re, lam_im, f_re, f_im


def _ssm_params_fwd(log_neg_a_re, a_im, log_dt):
    rows, n = log_neg_a_re.shape

    def kern(a_ref, b_ref, c_ref, o1, o2, o3, o4):
        for o, v in zip((o1, o2, o3, o4), _ssm_discretise(a_ref[...], b_ref[...], c_ref[...])):
            o[...] = v

    return pl.pallas_call(kern, name="ssm_params_fwd", out_shape=[jax.ShapeDtypeStruct((rows, n), F32)] * 4)(
        log_neg_a_re, a_im, log_dt)


def _ssm_params_bwd(log_neg_a_re, a_im, log_dt, cots):
    rows, n = log_neg_a_re.shape

    def kern(a_ref, b_ref, c_ref, g1, g2, g3, g4, o1, o2, o3):
        _, vjp = jax.vjp(_ssm_discretise, a_ref[...], b_ref[...], c_ref[...])
        d1, d2, d3 = vjp((g1[...], g2[...], g3[...], g4[...]))
        o1[...] = d1
        o2[...] = d2
        o3[...] = d3

    return pl.pallas_call(
        kern, name="ssm_params_bwd",
        out_shape=[jax.ShapeDtypeStruct((rows, n), F32), jax.ShapeDtypeStruct((rows, n), F32),
                   jax.ShapeDtypeStruct((rows, 1), F32)])(log_neg_a_re, a_im, log_dt, *cots)


def _cpow(lr, li, n):
    out = None
    br, bi = lr, li
    while n:
        if n & 1:
            out = (br, bi) if out is None else _cmul(out[0], out[1], br, bi)
        n >>= 1
        if n:
            br, bi = _cmul(br, bi, br, bi)
    return out


def _slab(t):
    return pl.ds(pl.multiple_of(t * N_CHUNK, N_CHUNK), N_CHUNK)


def _steps(n, step, carry):
    main = n // SCAN_UNROLL

    def body(i, c):
        for r in range(SCAN_UNROLL):
            c = step(i * SCAN_UNROLL + r, c)
        return c

    carry = lax.fori_loop(0, main, body, carry)
    for k in range(main * SCAN_UNROLL, n):
        carry = step(jnp.int32(k), carry)
    return carry


def _shift_chunks(vr, vi, reverse):
    sub = lax.broadcasted_iota(jnp.int32, vr.shape, 0)
    if reverse:
        keep = sub != N_CHUNK - 1
        return jnp.where(keep, pltpu.roll(vr, N_CHUNK - 1, 0), 0.0), jnp.where(keep, pltpu.roll(vi, N_CHUNK - 1, 0), 0.0)
    keep = sub != 0
    return jnp.where(keep, pltpu.roll(vr, 1, 0), 0.0), jnp.where(keep, pltpu.roll(vi, 1, 0), 0.0)


def _chunk_scan(xr, xi, lr, li, Lc, reverse, prev=None):
    C = xr.shape[1]
    lrb, lib = jnp.broadcast_to(lr, (N_CHUNK, C)), jnp.broadcast_to(li, (N_CHUNK, C))
    zero = jnp.zeros((N_CHUNK, C), F32)

    def step_of(k):
        return (Lc - 1 - k) if reverse else k

    def advance(sr, si, t):
        return lrb * sr - lib * si + xr[_slab(t), :], lrb * si + lib * sr + xi[_slab(t), :]

    def pass1(k, c):
        return advance(c[0], c[1], step_of(k))

    er, ei = _steps(Lc, pass1, (zero, zero))
    pr, pi = _cpow(lr, li, Lc)
    prb, pib = jnp.broadcast_to(pr, (N_CHUNK, C)), jnp.broadcast_to(pi, (N_CHUNK, C))
    cr, ci = zero, zero
    for _ in range(N_CHUNK - 1):
        cr, ci = _shift_chunks(er + prb * cr - pib * ci, ei + prb * ci + pib * cr, reverse)

    if prev is None:
        def pass2(k, c):
            t = step_of(k)
            nr, ni = advance(c[0], c[1], t)
            xr[_slab(t), :] = nr
            xi[_slab(t), :] = ni
            return nr, ni

        _steps(Lc, pass2, (cr, ci))
        return None

    qr, qi = prev
    def pair(ar, ai, wr, wi, acc):
        return acc[0] + ar * wr + ai * wi, acc[1] + ai * wr - ar * wi

    def pass2(k, c):
        sr, si, accr, acci = c
        t = step_of(k)
        nr, ni = advance(sr, si, t)
        xr[_slab(t), :] = nr
        xi[_slab(t), :] = ni
        tp = t - 1 if reverse else t + 1
        accr, acci = pair(nr, ni, qr[_slab(tp), :], qi[_slab(tp), :], (accr, acci))
        return nr, ni, accr, acci

    sr, si, accr, acci = _steps(Lc - 1, pass2, (cr, ci, zero, zero))
    t = step_of(Lc - 1)
    nr, ni = advance(sr, si, t)
    xr[_slab(t), :] = nr
    xi[_slab(t), :] = ni
    edge = step_of(0)
    wr, wi = _shift_chunks(qr[_slab(edge), :], qi[_slab(edge), :], not reverse)
    return pair(nr, ni, wr, wi, (accr, acci))


def _ssm_dims(T):
    assert T % ROW_BLOCK == 0
    return T // N_CHUNK, T // ROW_BLOCK


def _group_mask():
    shape = (SSM_BLOCK_GROUPS * SSM_GROUP, SSM_BLOCK_GROUPS * SSM_STATE)
    rows = lax.broadcasted_iota(jnp.int32, shape, 0) >> GROUP_SHIFT
    cols = lax.broadcasted_iota(jnp.int32, shape, 1) >> STATE_SHIFT
    return rows == cols


def _expand_groups(compact):
    return jnp.where(_group_mask(), jnp.tile(compact, (SSM_BLOCK_GROUPS, 1)), 0.0).astype(BF16)


def _collect_groups(block):
    kept = jnp.where(_group_mask(), block, 0.0)
    out = kept[0:SSM_GROUP]
    for g in range(1, SSM_BLOCK_GROUPS):
        out = out + kept[g * SSM_GROUP:(g + 1) * SSM_GROUP]
    return out


def _ssm_specs(T, direction):
    SC = SSM_BLOCK_GROUPS * SSM_STATE
    col = pl.BlockSpec((T, LANES), lambda m: (0, m))
    mat = pl.BlockSpec((None, None, SSM_GROUP, SC), lambda m: (direction, m, 0, 0))
    vec = pl.BlockSpec((None, None, 1, SC), lambda m: (direction, m, 0, 0))
    return col, mat, vec


def _ssm_project_in(u_ref, bre, bim, xr, xi, nblk):
    for blk in range(nblk):
        rows = pl.ds(blk * ROW_BLOCK, ROW_BLOCK)
        ub = u_ref[rows, :].astype(BF16)
        xr[rows, :] = jnp.dot(ub, bre, preferred_element_type=F32)
        xi[rows, :] = jnp.dot(ub, bim, preferred_element_type=F32)


def _ssm_fwd(name, u_bf, mats, vecs, direction):
    T, W = u_bf.shape
    SC = SSM_BLOCK_GROUPS * SSM_STATE
    nb = W // LANES
    Lc, nblk = _ssm_dims(T)
    reverse = bool(direction)

    def kern(u_ref, bre_ref, bim_ref, cre_ref, cim_ref, lr_ref, li_ref, fr_ref, fi_ref, y_ref, xr, xi):
        _ssm_project_in(u_ref, _expand_groups(bre_ref[...]), _expand_groups(bim_ref[...]), xr, xi, nblk)
        _chunk_scan(xr, xi, lr_ref[...], li_ref[...], Lc, reverse)
        fr, fi = fr_ref[...], fi_ref[...]
        cre, cim = _expand_groups(cre_ref[...]), _expand_groups(cim_ref[...])
        for blk in range(nblk):
            rows = pl.ds(blk * ROW_BLOCK, ROW_BLOCK)
            sr, si = _cmul(fr, fi, xr[rows, :], xi[rows, :])
            y_ref[rows, :] = (lax.dot_general(sr.astype(BF16), cre, _NT, preferred_element_type=F32)
                              - lax.dot_general(si.astype(BF16), cim, _NT, preferred_element_type=F32))

    col, mat, vec = _ssm_specs(T, direction)
    return pl.pallas_call(
        kern, name=name, grid=(nb,),
        in_specs=[col, mat, mat, mat, mat, vec, vec, vec, vec],
        out_specs=col, out_shape=jax.ShapeDtypeStruct((T, W), F32),
        scratch_shapes=[pltpu.VMEM((T, SC), F32), pltpu.VMEM((T, SC), F32)],
        compiler_params=_params(("arbitrary",)),
    )(u_bf, *mats, *vecs)


def _ssm_bwd(name, u_bf, dy_bf, du_in, mats, vecs, direction):
    T, W = u_bf.shape
    SC = SSM_BLOCK_GROUPS * SSM_STATE
    nb = W // LANES
    Lc, nblk = _ssm_dims(T)
    reverse = bool(direction)

    def kern(u_ref, dy_ref, dui_ref, bre_ref, bim_ref, cre_ref, cim_ref, lr_ref, li_ref, fr_ref, fi_ref,
             du_ref, dbre_ref, dbim_ref, dcre_ref, dcim_ref, dlr_ref, dli_ref, dfr_ref, dfi_ref,
             xr, xi, gr, gi):
        lr, li, fr, fi = lr_ref[...], li_ref[...], fr_ref[...], fi_ref[...]
        bre, bim = _expand_groups(bre_ref[...]), _expand_groups(bim_ref[...])
        cre, cim = _expand_groups(cre_ref[...]), _expand_groups(cim_ref[...])
        _ssm_project_in(u_ref, bre, bim, xr, xi, nblk)
        _chunk_scan(xr, xi, lr, li, Lc, reverse)
        mre = jnp.zeros((LANES, SC), F32)
        mim = jnp.zeros((LANES, SC), F32)
        dfr = jnp.zeros((1, SC), F32)
        dfi = jnp.zeros((1, SC), F32)
        for blk in range(nblk):
            rows = pl.ds(blk * ROW_BLOCK, ROW_BLOCK)
            dy = dy_ref[rows, :].astype(BF16)
            x_r, x_i = xr[rows, :], xi[rows, :]
            g_r = jnp.dot(dy, cre, preferred_element_type=F32)
            g_i = -jnp.dot(dy, cim, preferred_element_type=F32)
            mre += lax.dot_general(dy, x_r.astype(BF16), _TN, preferred_element_type=F32)
            mim += lax.dot_general(dy, x_i.astype(BF16), _TN, preferred_element_type=F32)
            dfr += _colsum(g_r * x_r + g_i * x_i)
            dfi += _colsum(g_i * x_r - g_r * x_i)
            gr[rows, :] = fr * g_r + fi * g_i
            gi[rows, :] = fr * g_i - fi * g_r
        dcre_ref[...] = _collect_groups(fr * mre - fi * mim)
        dcim_ref[...] = _collect_groups(-(fr * mim + fi * mre))
        dfr_ref[...] = dfr
        dfi_ref[...] = dfi
        accr, acci = _chunk_scan(gr, gi, lr, -li, Lc, not reverse, prev=(xr, xi))
        dlr_ref[...] = _colsum(accr)
        dli_ref[...] = _colsum(acci)
        dbre = jnp.zeros((LANES, SC), F32)
        dbim = jnp.zeros((LANES, SC), F32)
        for blk in range(nblk):
            rows = pl.ds(blk * ROW_BLOCK, ROW_BLOCK)
            ub = u_ref[rows, :].astype(BF16)
            a_r, a_i = gr[rows, :].astype(BF16), gi[rows, :].astype(BF16)
            dbre += lax.dot_general(ub, a_r, _TN, preferred_element_type=F32)
            dbim += lax.dot_general(ub, a_i, _TN, preferred_element_type=F32)
            du_ref[rows, :] = (dui_ref[rows, :] + lax.dot_general(a_r, bre, _NT, preferred_element_type=F32)
                               + lax.dot_general(a_i, bim, _NT, preferred_element_type=F32))
        dbre_ref[...] = _collect_groups(dbre)
        dbim_ref[...] = _collect_groups(dbim)

    col, mat, vec = _ssm_specs(T, direction)
    mat_out = pl.BlockSpec((None, SSM_GROUP, SC), lambda m: (m, 0, 0))
    vec_out = pl.BlockSpec((None, 1, SC), lambda m: (m, 0, 0))
    mat_shape = jax.ShapeDtypeStruct((nb, SSM_GROUP, SC), F32)
    vec_shape = jax.ShapeDtypeStruct((nb, 1, SC), F32)
    return pl.pallas_call(
        kern, name=name, grid=(nb,),
        in_specs=[col, col, col, mat, mat, mat, mat, vec, vec, vec, vec],
        out_specs=[col, mat_out, mat_out, mat_out, mat_out, vec_out, vec_out, vec_out, vec_out],
        out_shape=[jax.ShapeDtypeStruct((T, W), F32), mat_shape, mat_shape, mat_shape, mat_shape,
                   vec_shape, vec_shape, vec_shape, vec_shape],
        scratch_shapes=[pltpu.VMEM((T, SC), F32)] * 4,
        compiler_params=_params(("arbitrary",)),
    )(u_bf, dy_bf, du_in, *mats, *vecs)


def _groups_side_by_side(p, channel_axis):
    d, G = p.shape[:2]
    nb = G // SSM_BLOCK_GROUPS
    v = p.reshape(d, nb, SSM_BLOCK_GROUPS, p.shape[2], p.shape[3])
    v = v.transpose(0, 1, 4, 2, 3) if channel_axis == 3 else v.transpose(0, 1, 3, 2, 4)
    return v.reshape(d, nb, SSM_GROUP, SSM_BLOCK_GROUPS * SSM_STATE)


def _groups_apart(m, channel_axis):
    d, nb = m.shape[:2]
    v = m.reshape(d, nb, SSM_GROUP, SSM_BLOCK_GROUPS, SSM_STATE)
    v = v.transpose(0, 1, 3, 4, 2) if channel_axis == 3 else v.transpose(0, 1, 3, 2, 4)
    return v.reshape((d, nb * SSM_BLOCK_GROUPS) + v.shape[3:])


def _conv_taps(ref, c, R, T):
    n = T // R
    r0 = pl.multiple_of(c * R, R)
    main = ref[pl.ds(r0, R), :]
    before = ref[pl.ds(pl.multiple_of(jnp.maximum(r0 - 8, 0), 8), 8), :]
    after = ref[pl.ds(pl.multiple_of(jnp.minimum(r0 + R, T - 8), 8), 8), :]
    ext = jnp.concatenate([jnp.where(c > 0, before, 0.0), main, jnp.where(c < n - 1, after, 0.0)], axis=0)
    m1 = pltpu.roll(ext, 1, 0)[8:8 + R]
    p1 = pltpu.roll(ext, R + 15, 0)[8:8 + R]
    return m1, main, p1, r0


def _conv_specs(T, F, cb):
    nj = F // cb
    lo = lambda rows: pl.BlockSpec((rows, cb), lambda j: (0, j))
    hi = lambda rows: pl.BlockSpec((rows, cb), lambda j: (0, j + nj))
    return nj, lo, hi


def _convact_fwd(up, conv_w, conv_b, F):
    T = up.shape[0]
    cb = _pick(F, (256, 128))
    R = _pick(T, (256, 128, 64))
    nj, lo, hi = _conv_specs(T, F, cb)

    def kern(uv, ug, wv, wg, bv, bg, o_ref):
        def body(c, carry):
            v1, v0, v2, r0 = _conv_taps(uv, c, R, T)
            g1, g0, g2, _ = _conv_taps(ug, c, R, T)
            val = v1 * wv[0:1, :] + v0 * wv[1:2, :] + v2 * wv[2:3, :] + bv[...]
            gate = g1 * wg[0:1, :] + g0 * wg[1:2, :] + g2 * wg[2:3, :] + bg[...]
            o_ref[pl.ds(r0, R), :] = (val * (gate * _sigmoid(gate))).astype(o_ref.dtype)
            return carry

        lax.fori_loop(0, T // R, body, 0)

    return pl.pallas_call(
        kern, name="convact_fwd", grid=(nj,),
        in_specs=[lo(T), hi(T), lo(3), hi(3), lo(1), hi(1)],
        out_specs=lo(T), out_shape=jax.ShapeDtypeStruct((T, F), BF16),
        compiler_params=_params(("arbitrary",)),
    )(up, up, conv_w, conv_w, conv_b, conv_b)


def _convact_bwd(up, dact, conv_w, conv_b, F):
    T = up.shape[0]
    cb = _pick(F, (256, 128))
    R = _pick(T, (256, 128, 64))
    nj, lo, hi = _conv_specs(T, F, cb)

    def kern(uv, ug, da, wv, wg, bv, bg, dup, dwv, dwg, dbv, dbg, sv, sg):
        zero = jnp.zeros((1, cb), F32)

        def pass_a(c, acc):
            v1, v0, v2, r0 = _conv_taps(uv, c, R, T)
            g1, g0, g2, _ = _conv_taps(ug, c, R, T)
            val = v1 * wv[0:1, :] + v0 * wv[1:2, :] + v2 * wv[2:3, :] + bv[...]
            gate = g1 * wg[0:1, :] + g0 * wg[1:2, :] + g2 * wg[2:3, :] + bg[...]
            sig = _sigmoid(gate)
            d = da[pl.ds(r0, R), :]
            dval = d * (gate * sig)
            dgate = d * val * (sig * (1.0 + gate * (1.0 - sig)))
            sv[pl.ds(r0, R), :] = dval
            sg[pl.ds(r0, R), :] = dgate
            terms = (dval * v1, dval * v0, dval * v2, dval, dgate * g1, dgate * g0, dgate * g2, dgate)
            return tuple(a + _colsum(t) for a, t in zip(acc, terms))

        acc = lax.fori_loop(0, T // R, pass_a, (zero,) * 8)
        for k in range(3):
            dwv[k:k + 1, :] = acc[k]
            dwg[k:k + 1, :] = acc[4 + k]
        dbv[...] = acc[3]
        dbg[...] = acc[7]

        def pass_b(c, carry):
            v1, v0, v2, r0 = _conv_taps(sv, c, R, T)
            g1, g0, g2, _ = _conv_taps(sg, c, R, T)
            dup[0, pl.ds(r0, R), :] = (v2 * wv[0:1, :] + v0 * wv[1:2, :] + v1 * wv[2:3, :]).astype(dup.dtype)
            dup[1, pl.ds(r0, R), :] = (g2 * wg[0:1, :] + g0 * wg[1:2, :] + g1 * wg[2:3, :]).astype(dup.dtype)
            return carry

        lax.fori_loop(0, T // R, pass_b, 0)

    dup, dwv, dwg, dbv, dbg = pl.pallas_call(
        kern, name="convact_bwd", grid=(nj,),
        in_specs=[lo(T), hi(T), lo(T), lo(3), hi(3), lo(1), hi(1)],
        out_specs=[pl.BlockSpec((2, T, cb), lambda j: (0, 0, j)), lo(3), lo(3), lo(1), lo(1)],
        out_shape=[jax.ShapeDtypeStruct((2, T, F), BF16),
                   jax.ShapeDtypeStruct((3, F), F32), jax.ShapeDtypeStruct((3, F), F32),
                   jax.ShapeDtypeStruct((1, F), F32), jax.ShapeDtypeStruct((1, F), F32)],
        scratch_shapes=[pltpu.VMEM((T, cb), F32), pltpu.VMEM((T, cb), F32)],
        compiler_params=_params(("arbitrary",)),
    )(up, up, dact, conv_w, conv_w, conv_b, conv_b)
    return dup, (dwv, dwg), jnp.concatenate([dbv, dbg], axis=1)


def _peers(x, y, c, with_self=False):
    out = []
    for k in range(0 if with_self else 1, N_DEV):
        px = 1 - x if k & 4 else x
        py = 1 - y if k & 2 else y
        pc = 1 - c if k & 1 else c
        out.append((k, (px, py, pc), 4 * px + 2 * py + pc))
    return out


def _exchange_start(name, bufs, modes, after):
    handles, token = _exchange_start_groups(name, [(bufs, modes)], after)
    return handles[0], token


def _exchange_start_groups(name, groups, after):
    bufs = [b for g, _ in groups for b in g]
    modes = [m for _, ms in groups for m in ms]
    first = [int(v) for v in np.cumsum([0] + [len(g) for g, _ in groups])]
    n, ng = len(bufs), len(groups)
    lands = [lax.empty((N_DEV,) + tuple(b.shape[-2:]), b.dtype) for b in bufs]

    def body(*refs):
        ins, land_in = refs[:n], refs[n:2 * n]
        sems = refs[2 * n + 1:2 * n + 1 + 2 * ng]
        token = refs[-1]
        x, y, c = lax.axis_index("x"), lax.axis_index("y"), lax.axis_index("c")
        me = 4 * x + 2 * y + c
        for g in range(ng):
            for b in range(first[g], first[g + 1]):
                for k, peer, slot in _peers(x, y, c, with_self=True):
                    sem = (b - first[g]) * N_DEV + k
                    pltpu.make_async_remote_copy(
                        src_ref=ins[b] if modes[b] == 'gather' else ins[b].at[slot], dst_ref=land_in[b].at[me],
                        send_sem=sems[2 * g].at[sem], recv_sem=sems[2 * g + 1].at[sem],
                        device_id=peer, device_id_type=MESH_ID).start()
        token[...] = jnp.zeros_like(token)

    hbm = pl.BlockSpec(memory_space=pltpu.HBM)
    sem_spec = pl.BlockSpec(memory_space=pltpu.SEMAPHORE)
    operands = [pltpu.with_memory_space_constraint(a, pltpu.HBM) for a in bufs + lands]
    sem_shapes = [pltpu.SemaphoreType.DMA((len(g) * N_DEV,)) for g, _ in groups for _ in range(2)]
    res = pl.pallas_call(
        body, name=name,
        out_shape=(*sem_shapes, *[pltpu.HBM(a.shape, a.dtype) for a in operands], jax.ShapeDtypeStruct((8, LANES), F32)),
        in_specs=[hbm] * (2 * n) + [pl.BlockSpec(memory_space=pl.ANY)],
        out_specs=(*([sem_spec] * (2 * ng)), *([hbm] * (2 * n)), pl.BlockSpec(memory_space=pltpu.VMEM)),
        input_output_aliases={i: 2 * ng + i for i in range(2 * n)},
        compiler_params=pltpu.CompilerParams(has_side_effects=pltpu.SideEffectType.DATAFLOW_SIDE_EFFECTING),
    )(*operands, after)
    srcs, landed = res[2 * ng:2 * ng + n], res[2 * ng + n:2 * ng + 2 * n]
    handles = [(res[2 * g], res[2 * g + 1], list(srcs[first[g]:first[g + 1]]), list(landed[first[g]:first[g + 1]]),
                tuple(modes[first[g]:first[g + 1]])) for g in range(ng)]
    return handles, res[-1]


def _exchange_wait(name, handle, after):
    send_sems, recv_sems, srcs, lands, modes = handle
    n = len(srcs)

    def body(*refs):
        src_in, land_in = refs[:n], refs[n:2 * n]
        send_ref, recv_ref = refs[2 * n], refs[2 * n + 1]
        x, y, c = lax.axis_index("x"), lax.axis_index("y"), lax.axis_index("c")
        for b in range(n):
            for k, peer, slot in _peers(x, y, c, with_self=True):
                sem = b * N_DEV + k
                copy = pltpu.make_async_remote_copy(
                    src_ref=src_in[b] if modes[b] == 'gather' else src_in[b].at[slot], dst_ref=land_in[b].at[slot],
                    send_sem=send_ref.at[sem], recv_sem=recv_ref.at[sem],
                    device_id=peer, device_id_type=MESH_ID)
                copy.wait_send()
                copy.wait_recv()

    hbm = pl.BlockSpec(memory_space=pltpu.HBM)
    sem_spec = pl.BlockSpec(memory_space=pltpu.SEMAPHORE)
    res = pl.pallas_call(
        body, name=name,
        out_shape=tuple(pltpu.HBM(a.shape, a.dtype) for a in srcs + lands),
        in_specs=[hbm] * (2 * n) + [sem_spec, sem_spec, pl.BlockSpec(memory_space=pl.ANY)],
        out_specs=tuple([hbm] * (2 * n)),
        input_output_aliases={i: i for i in range(2 * n)},
        compiler_params=pltpu.CompilerParams(has_side_effects=pltpu.SideEffectType.DATAFLOW_SIDE_EFFECTING),
    )(*srcs, *lands, send_sems, recv_sems, after)
    return list(res[n:])


def _adamw(w, g, m, v):
    m2 = ADAM_B1 * m + (1.0 - ADAM_B1) * g
    v2 = ADAM_B2 * v + (1.0 - ADAM_B2) * (g * g)
    m_hat = m2 / (1.0 - ADAM_B1 ** ADAM_STEP)
    v_hat = v2 / (1.0 - ADAM_B2 ** ADAM_STEP)
    return -ADAM_LR * (m_hat / (jnp.sqrt(v_hat) + ADAM_EPS) + ADAM_WD * w), m2, v2


def _sum_adamw(name, landed, w, m, v):
    R, C = w.shape
    row_bytes = N_DEV * (-(-C // LANES) * LANES) * landed.dtype.itemsize
    tiles = [d for d in range(16, R, 16) if R % d == 0 and d * row_bytes <= ADAM_TILE_BYTES]
    tr = max(tiles) if tiles and R * row_bytes > ADAM_TILE_BYTES else R

    def kern(x_ref, w_ref, m_ref, v_ref, g_out, d_out, m_out, v_out):
        g = x_ref[0].astype(F32)
        for q in range(1, N_DEV):
            g = g + x_ref[q].astype(F32)
        g_out[...] = g
        d_out[...], m_out[...], v_out[...] = _adamw(w_ref[...], g, m_ref[...], v_ref[...])

    blk = pl.BlockSpec((tr, C), lambda i: (i, 0))
    return pl.pallas_call(
        kern, name=name, grid=(R // tr,),
        in_specs=[pl.BlockSpec((N_DEV, tr, C), lambda i: (0, i, 0)), blk, blk, blk],
        out_specs=[blk] * 4, out_shape=[jax.ShapeDtypeStruct((R, C), F32)] * 4,
        compiler_params=_params(("arbitrary",)),
    )(landed, w, m, v)


def _sum_adamw_many(name, items):
    n = len(items)

    def kern(*refs):
        ins, outs = refs[:4 * n], refs[4 * n:]
        for k in range(n):
            x_ref, w_ref, m_ref, v_ref = ins[4 * k:4 * k + 4]
            g = x_ref[0].astype(F32)
            for q in range(1, N_DEV):
                g = g + x_ref[q].astype(F32)
            outs[4 * k][...] = g
            outs[4 * k + 1][...], outs[4 * k + 2][...], outs[4 * k + 3][...] = _adamw(w_ref[...], g, m_ref[...], v_ref[...])

    out_shape = [jax.ShapeDtypeStruct(w.shape, F32) for _, w, _, _ in items for _ in range(4)]
    res = pl.pallas_call(kern, name=name, out_shape=out_shape)(*[a for item in items for a in item])
    return [res[4 * k:4 * k + 4] for k in range(n)]


def _view2d(a):
    if a.ndim == 1:
        return a.reshape(1, -1)
    return a.reshape(-1, a.shape[-1])


def kernel(x, norm_mix_g, w_in, pool_w, pool_scale, ssm_log_neg_a_re, ssm_a_im, ssm_log_dt, ssm_b_re, ssm_b_im, ssm_c_re, ssm_c_im, ssm_d, glu_w, glu_b, out_norm_pool_g, out_norm_ssm_g, w_out, norm_ffn_g, w_up, conv_w, conv_b, w_down, final_norm_g, loss_target, m_norm_mix_g, m_w_in, m_pool_w, m_pool_scale, m_ssm_log_neg_a_re, m_ssm_a_im, m_ssm_log_dt, m_ssm_b_re, m_ssm_b_im, m_ssm_c_re, m_ssm_c_im, m_ssm_d, m_glu_w, m_glu_b, m_out_norm_pool_g, m_out_norm_ssm_g, m_w_out, m_norm_ffn_g, m_w_up, m_conv_w, m_conv_b, m_w_down, m_final_norm_g, v_norm_mix_g, v_w_in, v_pool_w, v_pool_scale, v_ssm_log_neg_a_re, v_ssm_a_im, v_ssm_log_dt, v_ssm_b_re, v_ssm_b_im, v_ssm_c_re, v_ssm_c_im, v_ssm_d, v_glu_w, v_glu_b, v_out_norm_pool_g, v_out_norm_ssm_g, v_w_out, v_norm_ffn_g, v_w_up, v_conv_w, v_conv_b, v_w_down, v_final_norm_g):
    given = dict(locals())
    weights = {n: given[n] for n in WEIGHTS}
    mom1 = {n: given["m_" + n] for n in WEIGHTS}
    mom2 = {n: given["v_" + n] for n in WEIGHTS}

    xs = x[0]
    tgt = loss_target[0]
    T, D = xs.shape
    DP = len(POOL_WINDOWS) * POOL_GROUP
    DS = D - DP
    G = DS // SSM_GROUP
    N, H = SSM_STATE, SSM_GROUP
    F2 = w_up.shape[1] * N_DEV
    F = F2 // 2
    fs = w_up.shape[1]
    row = lambda a: a.reshape(1, -1)

    gather_groups = [([w_in.astype(BF16)], ['gather']), ([glu_w.astype(BF16), w_out.astype(BF16)], ['gather'] * 2),
                     ([w_up.T.astype(BF16), conv_w], ['gather'] * 2), ([w_down.astype(BF16)], ['gather'])]
    (in_handle, mix_handle, up_handle, down_handle), token = _exchange_start_groups("gather_start", gather_groups, w_in)
    CB = row(conv_b)

    g1, g2, g3 = row(norm_mix_g), row(norm_ffn_g), row(final_norm_g)
    gp, gs = row(out_norm_pool_g), row(out_norm_ssm_g)
    (xn,) = _rowmap("norm_mix", _rms_fwd, [xs], [g1 + token[0:1, 0:1]], [(D, BF16)])
    W_in = _exchange_wait("gather_in_wait", in_handle, xn)[0].reshape(D, D)
    u = _mm("proj_in", xn, W_in, 'nn', F32)
    tmp = _pick(T, (256, 128))
    pool_w_bf = pool_w.astype(BF16)
    ypn = _pool_fwd(u, pool_w_bf, row(pool_scale), gp, tmp)

    u_ssm = _to_chunk_rows(u[:, DP:])
    u_ssm_bf = u_ssm.astype(BF16)
    a_rows = (ssm_log_neg_a_re.reshape(2 * G, N), ssm_a_im.reshape(2 * G, N), ssm_log_dt.reshape(2 * G, 1))
    disc = _ssm_params_fwd(*a_rows)
    nb = G // SSM_BLOCK_GROUPS
    vecs = [d.reshape(2, nb, 1, SSM_BLOCK_GROUPS * N) for d in disc]
    mats = [_groups_side_by_side(ssm_b_re, 3), _groups_side_by_side(ssm_b_im, 3),
            _groups_side_by_side(ssm_c_re, 2), _groups_side_by_side(ssm_c_im, 2)]
    y_dir = [_ssm_fwd("ssm_fwd_%d" % d, u_ssm_bf, mats, vecs, d) for d in range(2)]

    def mix_post(yf, yb, us, d, gw, gb, g):
        y = yf + yb + d * us
        z = _gelu(y)
        gate = _sigmoid(jnp.dot(z.astype(BF16), gw, preferred_element_type=F32) + gb)
        return _rms_fwd(z * gate, g), y

    mix_landed = _exchange_wait("gather_mix_wait", mix_handle, y_dir[1])
    W_glu = mix_landed[0].reshape(DS, DS)
    W_out = mix_landed[1].reshape(D, D)
    ysn, y_ssm = _rowmap("ssm_post", mix_post, [y_dir[0], y_dir[1], u_ssm], [row(ssm_d), W_glu, row(glu_b), gs],
                         [(DS, BF16), (DS, F32)])
    ycat = jnp.concatenate([ypn, _from_chunk_rows(ysn)], axis=1)
    h1, hn = _mm_rows("proj_out", ycat, W_out, 'nn', lambda h, g: (h, _rms_fwd(h, g)), [], [g2],
                      [(D, F32), (D, BF16)], add=xs)
    up_landed = _exchange_wait("gather_up_wait", up_handle, hn)
    W_up_t = up_landed[0].reshape(F2, D)
    CW = up_landed[1].transpose(1, 0, 2).reshape(3, F2)
    up = _mm("ffn_up", hn, W_up_t, 'nt', F32)
    act = _convact_fwd(up, CW, CB, F)
    W_down = _exchange_wait("gather_down_wait", down_handle, act)[0].reshape(F, D)

    def head(h, t, g):
        r = lax.rsqrt(_rowmean(h * h) + EPS)
        hh = h * r
        e = hh * g - t
        loss = 0.5 * jnp.sum(_rowmean(e * e), keepdims=True)
        dy = e * (1.0 / D)
        dxh = dy * g
        dh = r * (dxh - hh * _rowmean(dxh * hh))
        return dh, dh, jnp.broadcast_to(loss, (1, LANES)), _colsum(dy * hh)

    dh2, dh2_bf, loss_acc, dg3 = _mm_rows("ffn_down", act, W_down, 'nn', head, [tgt], [g3],
                                          [(D, F32), (D, BF16)], [(1, LANES), (1, D)], add=h1)

    dact = _mm("ffn_down_dx", dh2_bf, W_down, 'nt', F32)
    dW_down = _mm("ffn_down_dw", act, dh2_bf, 'tn', BF16)
    dup, dCW, dCB = _convact_bwd(up, dact, CW, CB, F)
    dhn_val = _mm("ffn_up_dx_val", dup, W_up_t, 'nn', F32, b_rows=(2, 0), a_plane=0)
    dW_up_t = _mm("ffn_up_dw", dup, hn, 'tn', BF16).reshape(F2, D)

    def shards(full_grad):
        return full_grad.reshape((N_DEV, full_grad.shape[0] // N_DEV) + full_grad.shape[1:])

    half = N_DEV // 2
    dCW_sh = jnp.concatenate([h.reshape(3, half, fs).transpose(1, 0, 2) for h in dCW], axis=0)
    ffn_sent = [shards(dW_down), shards(dW_up_t), dCW_sh]
    ffn_handle, token = _exchange_start("scatter_ffn_start", ffn_sent, ['scatter'] * 3, after=dhn_val)
    g2_late = g2 + token[0:1, 0:1]

    def norm_bwd_add(dy, hx, res, g):
        dx, dg = _rms_bwd(hx, g, dy)
        return dx + res, dg

    def norm_bwd_add2(dy, hx, res, g):
        dx, dg = _rms_bwd(hx, g, dy)
        return dx + res, dx + res, dg

    dh1, dh1_bf, dg2 = _mm_rows("ffn_up_dx_gate", dup, W_up_t, 'nn', norm_bwd_add2, [h1, dh2], [g2_late],
                                [(D, F32), (D, BF16)], [(1, D)], add=dhn_val, b_rows=(2, 1), a_plane=1)
    dycat = _mm("proj_out_dx", dh1_bf, W_out, 'nt', F32)
    dW_out = _mm("proj_out_dw", ycat, dh1_bf, 'tn', BF16)

    out_sent = [shards(dW_out)]
    out_handle, token = _exchange_start("scatter_out_start", out_sent, ['scatter'], after=dycat)
    q, dpooled, dpool_w, dpool_scale, dgp = _pool_bwd_rows(u, dycat, pool_w_bf, row(pool_scale),
                                                           gp + token[0:1, 0:1], tmp)

    def mix_post_bwd(y, us, dyn, d, gw, gb, g):
        z = _gelu(y)
        gz = z.astype(BF16)
        sig = _sigmoid(jnp.dot(gz, gw, preferred_element_type=F32) + gb)
        dys, dg = _rms_bwd(z * sig, g, dyn)
        dgl = dys * z * sig * (1.0 - sig)
        dgl_bf = dgl.astype(BF16)
        dz = dys * sig + lax.dot_general(dgl_bf, gw, (((1,), (1,)), ((), ())), preferred_element_type=F32)
        dgw = lax.dot_general(gz, dgl_bf, (((0,), (0,)), ((), ())), preferred_element_type=F32)
        dyv = dz * _gelu_grad(y)
        return dyv, dyv * d, dgw, _colsum(dgl), _colsum(dyv * us), dg

    dyv, du_dir, dglu_w, dglu_b, dssm_d, dgs = _rowmap(
        "ssm_post_bwd", mix_post_bwd, [y_ssm, u_ssm, _to_chunk_rows(dycat[:, DP:])], [row(ssm_d), W_glu, row(glu_b), gs],
        [(DS, BF16), (DS, F32)], [(DS, DS), (1, DS), (1, DS), (1, DS)])

    du_ssm = du_dir
    dB, dC, dvec = [], [], []
    for d in range(2):
        res = _ssm_bwd("ssm_bwd_%d" % d, u_ssm_bf, dyv, du_ssm, mats, vecs, d)
        du_ssm = res[0]
        dB.append(res[1:3])
        dC.append(res[3:5])
        dvec.append(res[5:9])
    grad_b = [_groups_apart(jnp.stack([dB[0][k], dB[1][k]]), 3) for k in range(2)]
    grad_c = [_groups_apart(jnp.stack([dC[0][k], dC[1][k]]), 2) for k in range(2)]
    cots = [jnp.stack([dvec[0][k], dvec[1][k]]).reshape(2 * G, N) for k in range(4)]
    d_a_re, d_a_im, d_log_dt = _ssm_params_bwd(*a_rows, cots)

    rep_grads = {
        'pool_w': dpool_w, 'pool_scale': dpool_scale, 'ssm_log_neg_a_re': d_a_re, 'ssm_a_im': d_a_im,
        'ssm_log_dt': d_log_dt, 'ssm_b_re': grad_b[0], 'ssm_b_im': grad_b[1], 'ssm_c_re': grad_c[0], 'ssm_c_im': grad_c[1],
        'ssm_d': dssm_d, 'glu_b': dglu_b, 'out_norm_pool_g': dgp, 'out_norm_ssm_g': dgs, 'norm_ffn_g': dg2,
        'conv_b': dCB, 'final_norm_g': dg3}
    wide = ('pool_w', 'ssm_b_re', 'ssm_b_im', 'ssm_c_re', 'ssm_c_im')

    def narrow(n):
        return weights[n].ndim == 4 and weights[n].shape[-1] < LANES // 2

    def travel(n):
        g = rep_grads[n]
        if narrow(n):
            g = jnp.swapaxes(g, 2, 3)
        if n in wide:
            return g.reshape(-1, PACK_W if g.size % PACK_W == 0 else LANES).astype(BF16)
        return _view2d(g.reshape(weights[n].shape))

    small = tuple(n for n in REPLICATED if n != 'norm_mix_g')
    small_sent = [shards(dglu_w.astype(BF16))] + [travel(n) for n in small] + [loss_acc]
    small_handle, token = _exchange_start("reduce_small_start", small_sent, ['scatter'] + ['gather'] * (len(small) + 1),
                                          after=d_a_re)

    du = _pool_bwd_band(q, dpooled, _from_chunk_rows(du_ssm), token, tmp)
    dW_in = _mm("proj_in_dw", xn, du, 'tn', BF16)
    in_handle, token = _exchange_start("scatter_in_start", [shards(dW_in)], ['scatter'], after=dW_in)
    dx, dg1 = _mm_rows("proj_in_dx", du, W_in + token[0, 0].astype(BF16), 'nt', norm_bwd_add, [xs, dh1], [g1],
                       [(D, F32)], [(1, D)])
    rep_grads['norm_mix_g'] = dg1
    last_handle, token = _exchange_start("reduce_last_start", [travel('norm_mix_g')], ['gather'], after=dx)

    grads, delta, new_m, new_v = {}, {}, {}, {}

    def update(n, landed_n):
        shape = weights[n].shape
        if narrow(n):
            swapped = lambda a: _view2d(jnp.swapaxes(a, 2, 3))
            landed_n = landed_n.reshape((N_DEV,) + swapped(weights[n]).shape)
            res = _sum_adamw("adamw_" + n, landed_n, swapped(weights[n]), swapped(mom1[n]), swapped(mom2[n]))
            back = lambda r: jnp.swapaxes(r.reshape(shape[:2] + (shape[3], shape[2])), 2, 3)
            grads[n], delta[n], new_m[n], new_v[n] = [back(r) for r in res]
            return
        if n == 'w_up':
            res = _sum_adamw("adamw_" + n, landed_n, weights[n].T, mom1[n].T, mom2[n].T)
            grads[n], delta[n], new_m[n], new_v[n] = [r.T for r in res]
            return
        landed_n = landed_n.reshape((N_DEV,) + _view2d(weights[n]).shape)
        res = _sum_adamw("adamw_" + n, landed_n, _view2d(weights[n]), _view2d(mom1[n]), _view2d(mom2[n]))
        grads[n], delta[n], new_m[n], new_v[n] = [r.reshape(shape) for r in res]

    def update_many(name, named_landed):
        items = [(l.reshape((N_DEV,) + _view2d(weights[n]).shape), _view2d(weights[n]), _view2d(mom1[n]), _view2d(mom2[n]))
                 for n, l in named_landed]
        for (n, _), res in zip(named_landed, _sum_adamw_many(name, items)):
            grads[n], delta[n], new_m[n], new_v[n] = [r.reshape(weights[n].shape) for r in res]

    ffn_landed = _exchange_wait("scatter_ffn_wait", ffn_handle, token)
    for n, l in zip(('w_down', 'w_up'), ffn_landed):
        update(n, l)
    (out_landed,) = _exchange_wait("scatter_out_wait", out_handle, token)
    update('w_out', out_landed)
    small_landed = _exchange_wait("reduce_small_wait", small_handle, token)
    tiny = [('conv_w', ffn_landed[2])]
    for n, l in zip(('glu_w',) + small, small_landed):
        if weights[n].size <= ADAM_TINY:
            tiny.append((n, l))
        else:
            update(n, l)
    update_many("adamw_tiny", tiny)

    def loss_sum(parts_ref, o_ref):
        s = parts_ref[0]
        for q in range(1, N_DEV):
            s = s + parts_ref[q]
        o_ref[...] = s

    loss = pl.pallas_call(loss_sum, name="loss_sum", out_shape=jax.ShapeDtypeStruct((1, LANES), F32))(small_landed[-1])[0, 0]

    update('w_in', _exchange_wait("scatter_in_wait", in_handle, grads[small[-1]])[0])
    update('norm_mix_g', _exchange_wait("reduce_last_wait", last_handle, grads['w_in'])[0])

    return (loss, dx[None], *[grads[n] for n in WEIGHTS], *[delta[n] for n in WEIGHTS],
            *[new_m[n] for n in WEIGHTS], *[new_v[n] for n in WEIGHTS])
```

```python
import functools

import jax
import jax.numpy as jnp
import numpy as np
from jax import lax
from jax.experimental import pallas as pl
from jax.experimental.pallas import tpu as pltpu

F32 = jnp.float32
BF16 = jnp.bfloat16
MESH_ID = pl.DeviceIdType.MESH

N_DEV = 8
EPS = 1e-6
POOL_WINDOWS = (2, 4, 8, 16)
POOL_GROUP = 128
POOL_PAD = 64
SSM_GROUP = 16
SSM_STATE = 64
SSM_BLOCK_GROUPS = 8
N_CHUNK = 8
ROW_BLOCK = 512
SCAN_UNROLL = 8
GROUP_SHIFT = 4
STATE_SHIFT = 6
_NT = (((1,), (1,)), ((), ()))
_TN = (((0,), (0,)), ((), ()))
LANES = 128
PACK_W = 1024
VMEM_LIMIT = 56 * 1024 * 1024
MM_VMEM_BUDGET = 40 * 1024 * 1024
MM_TILE_CAP = 1408
MM_ROWS_TILE = 512
ADAM_TILE_BYTES = 4 * 1024 * 1024
ADAM_TINY = 8192

ADAM_LR = 0.001
ADAM_B1 = 0.9
ADAM_B2 = 0.999
ADAM_EPS = 1e-08
ADAM_WD = 0.01
ADAM_STEP = 10

WEIGHTS = ['norm_mix_g', 'w_in', 'pool_w', 'pool_scale', 'ssm_log_neg_a_re', 'ssm_a_im', 'ssm_log_dt',
           'ssm_b_re', 'ssm_b_im', 'ssm_c_re', 'ssm_c_im', 'ssm_d', 'glu_w', 'glu_b', 'out_norm_pool_g',
           'out_norm_ssm_g', 'w_out', 'norm_ffn_g', 'w_up', 'conv_w', 'conv_b', 'w_down', 'final_norm_g']
SHARDED = ('w_in', 'w_out', 'w_down', 'glu_w', 'w_up', 'conv_w')
REPLICATED = tuple(n for n in WEIGHTS if n not in SHARDED)


def _pick(n, prefs):
    for p in prefs:
        if n % p == 0:
            return p
    return n


def _params(sem, vmem=None):
    return pltpu.CompilerParams(dimension_semantics=sem, vmem_limit_bytes=vmem or VMEM_LIMIT)


def _tiles(n, cap):
    return [d for d in range(LANES, min(n, cap) + 1, LANES) if n % d == 0] or [n]


def _mm(name, a, b, mode, out_dtype, add=None, b_rows=(1, 0), a_plane=None):
    batch = a.shape[0] if a.ndim == 3 and a_plane is None else None
    if mode == 'nn':
        (M, K), N = a.shape[-2:], b.shape[1]
        assert b.shape[0] == K * b_rows[0]
    elif mode == 'nt':
        (M, K), (N, _) = a.shape[-2:], b.shape
    else:
        (K, M), (_, N) = a.shape[-2:], b.shape
    dims = {'nn': (((1,), (0,)), ((), ())), 'nt': (((1,), (1,)), ((), ())), 'tn': (((0,), (0,)), ((), ()))}[mode]
    sa, sb, so = a.dtype.itemsize, b.dtype.itemsize, jnp.dtype(out_dtype).itemsize
    best = None
    for tm in _tiles(M, MM_TILE_CAP):
        for tn in _tiles(N, MM_TILE_CAP):
            need = 2 * (tm * K * sa + tn * K * sb + tm * tn * (so + (4 if add is not None else 0)))
            if need <= MM_VMEM_BUDGET:
                key = (tm * tn / (tm + tn), tm * tn)
                if best is None or key > best[0]:
                    best = (key, tm, tn)
    _, tm, tn = best
    rows_inner = a.size * sa * (N // tn) + b.size * sb < a.size * sa + b.size * sb * (M // tm)

    def kern(*refs):
        a_ref, b_ref = refs[:2]
        o_ref = refs[-1]
        r = lax.dot_general(a_ref[...].astype(BF16), b_ref[...].astype(BF16), dims, preferred_element_type=F32)
        if add is not None:
            r = r + refs[2][...]
        o_ref[...] = r.astype(o_ref.dtype)

    if rows_inner:
        grid = (N // tn, M // tm)
        ij = lambda g0, g1: (g1, g0)
    else:
        grid = (M // tm, N // tn)
        ij = lambda g0, g1: (g0, g1)

    def spec(block, index, batched=False):
        if batch is None:
            return pl.BlockSpec(block, lambda g0, g1: index(*ij(g0, g1)))
        if batched:
            return pl.BlockSpec((None,) + block, lambda p, g0, g1: (p,) + index(*ij(g0, g1)))
        return pl.BlockSpec(block, lambda p, g0, g1: index(*ij(g0, g1)))

    if a_plane is not None:
        assert mode != 'tn'
        a_spec = pl.BlockSpec((None, tm, K), lambda g0, g1: (a_plane, ij(g0, g1)[0], 0))
    else:
        a_spec = (spec((K, tm), lambda i, j: (0, i), True) if mode == 'tn' else spec((tm, K), lambda i, j: (i, 0), True))
    b_spec = (spec((tn, K), lambda i, j: (j, 0)) if mode == 'nt' else spec((K, tn), lambda i, j: (b_rows[1], j)))
    in_specs = [a_spec, b_spec]
    args = [a, b]
    if add is not None:
        in_specs.append(spec((tm, tn), lambda i, j: (i, j), True))
        args.append(add)
    lead = () if batch is None else (batch,)
    return pl.pallas_call(
        kern, name=name, grid=lead + grid, in_specs=in_specs,
        out_specs=spec((tm, tn), lambda i, j: (i, j), True),
        out_shape=jax.ShapeDtypeStruct(lead + (M, N), out_dtype),
        compiler_params=_params(("parallel",) * (len(lead) + 2)),
    )(*args)


def _store_rows_and_sums(i, outs, res, n_row):
    for o, v in zip(outs[:n_row], res[:n_row]):
        o[...] = v.astype(o.dtype)
    for o, v in zip(outs[n_row:], res[n_row:]):
        @pl.when(i == 0)
        def _(o=o, v=v):
            o[...] = v

        @pl.when(i != 0)
        def _(o=o, v=v):
            o[...] += v


def _mm_rows(name, a, b, mode, fn, rows, fulls, out_rows, out_accs=(), add=None, b_rows=(1, 0), a_plane=None):
    M, K = a.shape[-2:]
    N = b.shape[1] if mode == 'nn' else b.shape[0]
    assert b.shape[0] == (K * b_rows[0] if mode == 'nn' else N)
    dims = _NT if mode == 'nt' else (((1,), (0,)), ((), ()))
    tm = min(MM_ROWS_TILE, M)
    n_extra, n_row = len(rows) + len(fulls), len(out_rows)

    def kern(*refs):
        i = pl.program_id(0)
        r = lax.dot_general(refs[0][...].astype(BF16), refs[1][...].astype(BF16), dims, preferred_element_type=F32)
        pos = 2
        if add is not None:
            r = r + refs[pos][...]
            pos += 1
        res = fn(r, *[x[...] for x in refs[pos:pos + n_extra]])
        _store_rows_and_sums(i, refs[pos + n_extra:], res, n_row)

    def full_spec(shape):
        nd = len(shape)
        return pl.BlockSpec(tuple(shape), lambda i: (0,) * nd)

    a_spec = (pl.BlockSpec((tm, K), lambda i: (i, 0)) if a_plane is None
              else pl.BlockSpec((None, tm, K), lambda i: (a_plane, i, 0)))
    b_spec = pl.BlockSpec((K, N), lambda i: (b_rows[1], 0)) if mode == 'nn' else pl.BlockSpec((N, K), lambda i: (0, 0))
    ins = [_row_in(r, tm, M) for r in ([add] if add is not None else []) + list(rows)]
    outs = [_row_out(o, tm, M) for o in out_rows]
    res = pl.pallas_call(
        kern, name=name, grid=(M // tm,),
        in_specs=[a_spec, b_spec] + [s for s, _ in ins] + [full_spec(f.shape) for f in fulls],
        out_specs=[s for s, _ in outs] + [full_spec(s) for s in out_accs],
        out_shape=[sh for _, sh in outs] + [jax.ShapeDtypeStruct(tuple(s), F32) for s in out_accs],
        compiler_params=_params(("arbitrary",)),
    )(a, b, *[x for _, x in ins], *fulls)
    return res


def _row_in(r, tm, T):
    arr, w, k = r if isinstance(r, tuple) else (r, r.shape[1], 0)
    return pl.BlockSpec((tm, w), lambda i: (i, k)), arr


def _row_out(o, tm, T):
    c, dt = o
    return pl.BlockSpec((tm, c), lambda i: (i, 0)), jax.ShapeDtypeStruct((T, c), dt)


def _rowmap(name, fn, rows, fulls, out_rows, out_accs=(), tm=512):
    T = (rows[0][0] if isinstance(rows[0], tuple) else rows[0]).shape[0]
    tm = min(tm, T)
    assert T % tm == 0
    n_in, n_row = len(rows) + len(fulls), len(out_rows)

    def kern(*refs):
        i = pl.program_id(0)
        res = fn(*[r[...] for r in refs[:n_in]])
        res = res if isinstance(res, (tuple, list)) else (res,)
        _store_rows_and_sums(i, refs[n_in:], res, n_row)

    def full_spec(shape):
        nd = len(shape)
        return pl.BlockSpec(tuple(shape), lambda i: (0,) * nd)

    ins = [_row_in(r, tm, T) for r in rows]
    outs = [_row_out(o, tm, T) for o in out_rows]
    res = pl.pallas_call(
        kern, name=name, grid=(T // tm,),
        in_specs=[s for s, _ in ins] + [full_spec(f.shape) for f in fulls],
        out_specs=[s for s, _ in outs] + [full_spec(s) for s in out_accs],
        out_shape=[sh for _, sh in outs] + [jax.ShapeDtypeStruct(tuple(s), F32) for s in out_accs],
        compiler_params=_params(("arbitrary",)),
    )(*[x for _, x in ins], *fulls)
    return res


def _colsum(v):
    return jnp.sum(v, axis=0, keepdims=True)


def _rowmean(v):
    return jnp.mean(v, axis=-1, keepdims=True)


def _rms_fwd(x, g):
    r = lax.rsqrt(_rowmean(x * x) + EPS)
    return x * r * g


def _rms_bwd(x, g, dy):
    r = lax.rsqrt(_rowmean(x * x) + EPS)
    xh = x * r
    dxh = dy * g
    return r * (dxh - xh * _rowmean(dxh * xh)), _colsum(dy * xh)


def _sigmoid(v):
    return 0.5 * jnp.tanh(0.5 * v) + 0.5


def _gelu(y):
    c = np.sqrt(2.0 / np.pi).astype(np.float32)
    return 0.5 * y * (1.0 + jnp.tanh(c * (y + 0.044715 * (y * y * y))))


def _gelu_grad(y):
    c = np.sqrt(2.0 / np.pi).astype(np.float32)
    th = jnp.tanh(c * (y + 0.044715 * (y * y * y)))
    return 0.5 * (1.0 + th) + 0.5 * y * (1.0 - th * th) * c * (1.0 + 3.0 * 0.044715 * (y * y))


def _split3(v):
    hi = v.astype(BF16)
    r1 = v - hi.astype(F32)
    mid = r1.astype(BF16)
    lo = (r1 - mid.astype(F32)).astype(BF16)
    return hi, mid, lo


def _band(tm, lo_off, hi_off):
    shape = (tm, tm + 2 * POOL_PAD)
    d = lax.broadcasted_iota(jnp.int32, shape, 1) - lax.broadcasted_iota(jnp.int32, shape, 0) - POOL_PAD
    return ((d >= lo_off) & (d < hi_off)).astype(BF16)


def _band_sum(band, ext):
    hi, mid, lo = _split3(ext)
    dot = functools.partial(jnp.dot, preferred_element_type=F32)
    return dot(band, hi) + dot(band, mid) + dot(band, lo)


def _window_count(r0, tm, half, T):
    t = r0 + lax.broadcasted_iota(jnp.int32, (tm, 1), 0)
    return (jnp.minimum(t + half, T) - jnp.maximum(t - half, 0)).astype(F32)


def _halo_specs(tm, C, T):
    per = tm // POOL_PAD
    last = T // POOL_PAD - 1
    return [pl.BlockSpec((POOL_PAD, C), lambda i: (jnp.maximum(i * per - 1, 0), 0)),
            pl.BlockSpec((tm, C), lambda i: (i, 0)),
            pl.BlockSpec((POOL_PAD, C), lambda i: (jnp.minimum((i + 1) * per, last), 0))]


def _with_halo(before_ref, main_ref, after_ref, i, n):
    before = jnp.where(i > 0, before_ref[...], 0.0)
    after = jnp.where(i < n - 1, after_ref[...], 0.0)
    return jnp.concatenate([before, main_ref[...], after], axis=0)


def _pool_block(ext, ctr, w_ref, r0, tm, T):
    pooled, conc = [], []
    for gi, w in enumerate(POOL_WINDOWS):
        half = w // 2
        cols = slice(gi * POOL_GROUP, (gi + 1) * POOL_GROUP)
        ws = _band_sum(_band(tm, -half, half), ext[:, cols])
        p = ws / _window_count(r0, tm, half, T) - ctr[:, cols]
        pooled.append(p)
        conc.append(jnp.dot(p.astype(BF16), w_ref[gi], preferred_element_type=F32))
    return pooled, jnp.concatenate(conc, axis=1)


def _pool_fwd(u, pool_w_bf, pool_scale, g_pool, tm):
    T = u.shape[0]
    nw = len(POOL_WINDOWS)
    C = nw * POOL_GROUP
    n = T // tm

    def kern(ub_ref, u_ref, ua_ref, w_ref, sc_ref, g_ref, o_ref):
        i = pl.program_id(0)
        r0 = pl.multiple_of(i * tm, tm)
        _, conc = _pool_block(_with_halo(ub_ref, u_ref, ua_ref, i, n), u_ref[...], w_ref, r0, tm, T)
        o_ref[...] = _rms_fwd(conc * sc_ref[...], g_ref[...]).astype(o_ref.dtype)

    return pl.pallas_call(
        kern, name="pool_fwd", grid=(n,),
        in_specs=_halo_specs(tm, C, T) + [pl.BlockSpec(pool_w_bf.shape, lambda i: (0, 0, 0)),
                                          pl.BlockSpec((1, C), lambda i: (0, 0)), pl.BlockSpec((1, C), lambda i: (0, 0))],
        out_specs=pl.BlockSpec((tm, C), lambda i: (i, 0)),
        out_shape=jax.ShapeDtypeStruct((T, C), BF16),
        compiler_params=_params(("arbitrary",)),
    )(u, u, u, pool_w_bf, pool_scale, g_pool)


def _pool_bwd_rows(u, d_y, pool_w_bf, pool_scale, g_pool, tm):
    T = u.shape[0]
    nw = len(POOL_WINDOWS)
    C = nw * POOL_GROUP
    n = T // tm

    def kern(ub_ref, u_ref, ua_ref, dy_ref, w_ref, sc_ref, g_ref, q_ref, dp_ref, dw_ref, dsc_ref, dg_ref):
        i = pl.program_id(0)
        r0 = pl.multiple_of(i * tm, tm)
        pooled, conc = _pool_block(_with_halo(ub_ref, u_ref, ua_ref, i, n), u_ref[...], w_ref, r0, tm, T)
        sc = sc_ref[...]
        dyp, dg = _rms_bwd(conc * sc, g_ref[...], dy_ref[...])
        dsc = _colsum(dyp * conc)
        dconc = (dyp * sc).astype(BF16)
        dws = []
        for gi, w in enumerate(POOL_WINDOWS):
            cols = slice(gi * POOL_GROUP, (gi + 1) * POOL_GROUP)
            dc = dconc[:, cols]
            dp = lax.dot_general(dc, w_ref[gi], (((1,), (1,)), ((), ())), preferred_element_type=F32)
            dp_ref[:, cols] = dp
            q_ref[:, cols] = dp / _window_count(r0, tm, w // 2, T)
            dws.append(lax.dot_general(pooled[gi].astype(BF16), dc, (((0,), (0,)), ((), ())), preferred_element_type=F32))

        @pl.when(i == 0)
        def _():
            for gi in range(nw):
                dw_ref[gi] = dws[gi]
            dsc_ref[...] = dsc
            dg_ref[...] = dg

        @pl.when(i != 0)
        def _():
            for gi in range(nw):
                dw_ref[gi] += dws[gi]
            dsc_ref[...] += dsc
            dg_ref[...] += dg

    full2 = pl.BlockSpec((1, C), lambda i: (0, 0))
    row_blk = pl.BlockSpec((tm, C), lambda i: (i, 0))
    return pl.pallas_call(
        kern, name="pool_bwd_rows", grid=(n,),
        in_specs=_halo_specs(tm, C, T) + [row_blk, pl.BlockSpec(pool_w_bf.shape, lambda i: (0, 0, 0)), full2, full2],
        out_specs=[row_blk, row_blk, pl.BlockSpec((nw, POOL_GROUP, POOL_GROUP), lambda i: (0, 0, 0)), full2, full2],
        out_shape=[jax.ShapeDtypeStruct((T, C), F32), jax.ShapeDtypeStruct((T, C), F32),
                   jax.ShapeDtypeStruct((nw, POOL_GROUP, POOL_GROUP), F32),
                   jax.ShapeDtypeStruct((1, C), F32), jax.ShapeDtypeStruct((1, C), F32)],
        compiler_params=_params(("arbitrary",)),
    )(u, u, u, d_y, pool_w_bf, pool_scale, g_pool)


def _pool_bwd_band(q, dpooled, du_other, after, tm):
    T, C = dpooled.shape
    C2 = du_other.shape[1]
    n = T // tm

    def kern(qb_ref, q_ref, qa_ref, dp_ref, other_ref, after_ref, o_ref):
        ext = _with_halo(qb_ref, q_ref, qa_ref, pl.program_id(0), n)
        for gi, w in enumerate(POOL_WINDOWS):
            half = w // 2
            cols = slice(gi * POOL_GROUP, (gi + 1) * POOL_GROUP)
            du = _band_sum(_band(tm, -half + 1, half + 1), ext[:, cols]) - dp_ref[:, cols]
            o_ref[:, cols] = du.astype(o_ref.dtype)
        o_ref[:, C:] = other_ref[...].astype(o_ref.dtype)

    return pl.pallas_call(
        kern, name="pool_bwd_band", grid=(n,),
        in_specs=_halo_specs(tm, C, T) + [pl.BlockSpec((tm, C), lambda i: (i, 0)), pl.BlockSpec((tm, C2), lambda i: (i, 0)),
                                          pl.BlockSpec(after.shape, lambda i: (0, 0))],
        out_specs=pl.BlockSpec((tm, C + C2), lambda i: (i, 0)),
        out_shape=jax.ShapeDtypeStruct((T, C + C2), BF16),
        compiler_params=_params(("arbitrary",)),
    )(q, q, q, dpooled, du_other, after)


def _to_chunk_rows(a):
    T, C = a.shape
    return a.reshape(N_CHUNK, T // N_CHUNK, C).transpose(1, 0, 2).reshape(T, C)


def _from_chunk_rows(a):
    T, C = a.shape
    return a.reshape(T // N_CHUNK, N_CHUNK, C).transpose(1, 0, 2).reshape(T, C)


def _cmul(ar, ai, br, bi):
    return ar * br - ai * bi, ar * bi + ai * br


def _ssm_discretise(log_neg_a_re, a_im, log_dt):
    dt = jnp.exp(log_dt)
    a_re = -jnp.exp(log_neg_a_re)
    mag = jnp.exp(a_re * dt)
    ang = a_im * dt
    lam_re, lam_im = mag * jnp.cos(ang), mag * jnp.sin(ang)
    den = a_re * a_re + a_im * a_im
    f_re = ((lam_re - 1.0) * a_re + lam_im * a_im) / den
    f_im = (lam_im * a_re - (lam_re - 1.0) * a_im) / den
    return lam_re, lam_im, f_re, f_im


def _ssm_params_fwd(log_neg_a_re, a_im, log_dt):
    rows, n = log_neg_a_re.shape

    def kern(a_ref, b_ref, c_ref, o1, o2, o3, o4):
        for o, v in zip((o1, o2, o3, o4), _ssm_discretise(a_ref[...], b_ref[...], c_ref[...])):
            o[...] = v

    return pl.pallas_call(kern, name="ssm_params_fwd", out_shape=[jax.ShapeDtypeStruct((rows, n), F32)] * 4)(
        log_neg_a_re, a_im, log_dt)


def _ssm_params_bwd(log_neg_a_re, a_im, log_dt, cots):
    rows, n = log_neg_a_re.shape

    def kern(a_ref, b_ref, c_ref, g1, g2, g3, g4, o1, o2, o3):
        _, vjp = jax.vjp(_ssm_discretise, a_ref[...], b_ref[...], c_ref[...])
        d1, d2, d3 = vjp((g1[...], g2[...], g3[...], g4[...]))
        o1[...] = d1
        o2[...] = d2
        o3[...] = d3

    return pl.pallas_call(
        kern, name="ssm_params_bwd",
        out_shape=[jax.ShapeDtypeStruct((rows, n), F32), jax.ShapeDtypeStruct((rows, n), F32),
                   jax.ShapeDtypeStruct((rows, 1), F32)])(log_neg_a_re, a_im, log_dt, *cots)


def _cpow(lr, li, n):
    out = None
    br, bi = lr, li
    while n:
        if n & 1:
            out = (br, bi) if out is None else _cmul(out[0], out[1], br, bi)
        n >>= 1
        if n:
            br, bi = _cmul(br, bi, br, bi)
    return out


def _slab(t):
    return pl.ds(pl.multiple_of(t * N_CHUNK, N_CHUNK), N_CHUNK)


def _steps(n, step, carry):
    main = n // SCAN_UNROLL

    def body(i, c):
        for r in range(SCAN_UNROLL):
            c = step(i * SCAN_UNROLL + r, c)
        return c

    carry = lax.fori_loop(0, main, body, carry)
    for k in range(main * SCAN_UNROLL, n):
        carry = step(jnp.int32(k), carry)
    return carry


def _shift_chunks(vr, vi, reverse):
    sub = lax.broadcasted_iota(jnp.int32, vr.shape, 0)
    if reverse:
        keep = sub != N_CHUNK - 1
        return jnp.where(keep, pltpu.roll(vr, N_CHUNK - 1, 0), 0.0), jnp.where(keep, pltpu.roll(vi, N_CHUNK - 1, 0), 0.0)
    keep = sub != 0
    return jnp.where(keep, pltpu.roll(vr, 1, 0), 0.0), jnp.where(keep, pltpu.roll(vi, 1, 0), 0.0)


def _chunk_scan(xr, xi, lr, li, Lc, reverse, prev=None):
    C = xr.shape[1]
    lrb, lib = jnp.broadcast_to(lr, (N_CHUNK, C)), jnp.broadcast_to(li, (N_CHUNK, C))
    zero = jnp.zeros((N_CHUNK, C), F32)

    def step_of(k):
        return (Lc - 1 - k) if reverse else k

    def advance(sr, si, t):
        return lrb * sr - lib * si + xr[_slab(t), :], lrb * si + lib * sr + xi[_slab(t), :]

    def pass1(k, c):
        return advance(c[0], c[1], step_of(k))

    er, ei = _steps(Lc, pass1, (zero, zero))
    pr, pi = _cpow(lr, li, Lc)
    prb, pib = jnp.broadcast_to(pr, (N_CHUNK, C)), jnp.broadcast_to(pi, (N_CHUNK, C))
    cr, ci = zero, zero
    for _ in range(N_CHUNK - 1):
        cr, ci = _shift_chunks(er + prb * cr - pib * ci, ei + prb * ci + pib * cr, reverse)

    if prev is None:
        def pass2(k, c):
            t = step_of(k)
            nr, ni = advance(c[0], c[1], t)
            xr[_slab(t), :] = nr
            xi[_slab(t), :] = ni
            return nr, ni

        _steps(Lc, pass2, (cr, ci))
        return None

    qr, qi = prev
    def pair(ar, ai, wr, wi, acc):
        return acc[0] + ar * wr + ai * wi, acc[1] + ai * wr - ar * wi

    def pass2(k, c):
        sr, si, accr, acci = c
        t = step_of(k)
        nr, ni = advance(sr, si, t)
        xr[_slab(t), :] = nr
        xi[_slab(t), :] = ni
        tp = t - 1 if reverse else t + 1
        accr, acci = pair(nr, ni, qr[_slab(tp), :], qi[_slab(tp), :], (accr, acci))
        return nr, ni, accr, acci

    sr, si, accr, acci = _steps(Lc - 1, pass2, (cr, ci, zero, zero))
    t = step_of(Lc - 1)
    nr, ni = advance(sr, si, t)
    xr[_slab(t), :] = nr
    xi[_slab(t), :] = ni
    edge = step_of(0)
    wr, wi = _shift_chunks(qr[_slab(edge), :], qi[_slab(edge), :], not reverse)
    return pair(nr, ni, wr, wi, (accr, acci))


def _ssm_dims(T):
    assert T % ROW_BLOCK == 0
    return T // N_CHUNK, T // ROW_BLOCK


def _group_mask():
    shape = (SSM_BLOCK_GROUPS * SSM_GROUP, SSM_BLOCK_GROUPS * SSM_STATE)
    rows = lax.broadcasted_iota(jnp.int32, shape, 0) >> GROUP_SHIFT
    cols = lax.broadcasted_iota(jnp.int32, shape, 1) >> STATE_SHIFT
    return rows == cols


def _expand_groups(compact):
    return jnp.where(_group_mask(), jnp.tile(compact, (SSM_BLOCK_GROUPS, 1)), 0.0).astype(BF16)


def _collect_groups(block):
    kept = jnp.where(_group_mask(), block, 0.0)
    out = kept[0:SSM_GROUP]
    for g in range(1, SSM_BLOCK_GROUPS):
        out = out + kept[g * SSM_GROUP:(g + 1) * SSM_GROUP]
    return out


def _ssm_specs(T, direction):
    SC = SSM_BLOCK_GROUPS * SSM_STATE
    col = pl.BlockSpec((T, LANES), lambda m: (0, m))
    mat = pl.BlockSpec((None, None, SSM_GROUP, SC), lambda m: (direction, m, 0, 0))
    vec = pl.BlockSpec((None, None, 1, SC), lambda m: (direction, m, 0, 0))
    return col, mat, vec


def _ssm_project_in(u_ref, bre, bim, xr, xi, nblk):
    for blk in range(nblk):
        rows = pl.ds(blk * ROW_BLOCK, ROW_BLOCK)
        ub = u_ref[rows, :].astype(BF16)
        xr[rows, :] = jnp.dot(ub, bre, preferred_element_type=F32)
        xi[rows, :] = jnp.dot(ub, bim, preferred_element_type=F32)


def _ssm_fwd(name, u_bf, mats, vecs, direction):
    T, W = u_bf.shape
    SC = SSM_BLOCK_GROUPS * SSM_STATE
    nb = W // LANES
    Lc, nblk = _ssm_dims(T)
    reverse = bool(direction)

    def kern(u_ref, bre_ref, bim_ref, cre_ref, cim_ref, lr_ref, li_ref, fr_ref, fi_ref, y_ref, xr, xi):
        _ssm_project_in(u_ref, _expand_groups(bre_ref[...]), _expand_groups(bim_ref[...]), xr, xi, nblk)
        _chunk_scan(xr, xi, lr_ref[...], li_ref[...], Lc, reverse)
        fr, fi = fr_ref[...], fi_ref[...]
        cre, cim = _expand_groups(cre_ref[...]), _expand_groups(cim_ref[...])
        for blk in range(nblk):
            rows = pl.ds(blk * ROW_BLOCK, ROW_BLOCK)
            sr, si = _cmul(fr, fi, xr[rows, :], xi[rows, :])
            y_ref[rows, :] = (lax.dot_general(sr.astype(BF16), cre, _NT, preferred_element_type=F32)
                              - lax.dot_general(si.astype(BF16), cim, _NT, preferred_element_type=F32))

    col, mat, vec = _ssm_specs(T, direction)
    return pl.pallas_call(
        kern, name=name, grid=(nb,),
        in_specs=[col, mat, mat, mat, mat, vec, vec, vec, vec],
        out_specs=col, out_shape=jax.ShapeDtypeStruct((T, W), F32),
        scratch_shapes=[pltpu.VMEM((T, SC), F32), pltpu.VMEM((T, SC), F32)],
        compiler_params=_params(("arbitrary",)),
    )(u_bf, *mats, *vecs)


def _ssm_bwd(name, u_bf, dy_bf, du_in, mats, vecs, direction, du_dtype):
    T, W = u_bf.shape
    SC = SSM_BLOCK_GROUPS * SSM_STATE
    nb = W // LANES
    Lc, nblk = _ssm_dims(T)
    reverse = bool(direction)

    def kern(u_ref, dy_ref, dui_ref, bre_ref, bim_ref, cre_ref, cim_ref, lr_ref, li_ref, fr_ref, fi_ref,
             du_ref, dbre_ref, dbim_ref, dcre_ref, dcim_ref, dlr_ref, dli_ref, dfr_ref, dfi_ref,
             xr, xi, gr, gi):
        lr, li, fr, fi = lr_ref[...], li_ref[...], fr_ref[...], fi_ref[...]
        bre, bim = _expand_groups(bre_ref[...]), _expand_groups(bim_ref[...])
        cre, cim = _expand_groups(cre_ref[...]), _expand_groups(cim_ref[...])
        _ssm_project_in(u_ref, bre, bim, xr, xi, nblk)
        _chunk_scan(xr, xi, lr, li, Lc, reverse)
        mre = jnp.zeros((LANES, SC), F32)
        mim = jnp.zeros((LANES, SC), F32)
        dfr = jnp.zeros((1, SC), F32)
        dfi = jnp.zeros((1, SC), F32)
        for blk in range(nblk):
            rows = pl.ds(blk * ROW_BLOCK, ROW_BLOCK)
            dy = dy_ref[rows, :].astype(BF16)
            x_r, x_i = xr[rows, :], xi[rows, :]
            g_r = jnp.dot(dy, cre, preferred_element_type=F32)
            g_i = -jnp.dot(dy, cim, preferred_element_type=F32)
            mre += lax.dot_general(dy, x_r.astype(BF16), _TN, preferred_element_type=F32)
            mim += lax.dot_general(dy, x_i.astype(BF16), _TN, preferred_element_type=F32)
            dfr += _colsum(g_r * x_r + g_i * x_i)
            dfi += _colsum(g_i * x_r - g_r * x_i)
            gr[rows, :] = fr * g_r + fi * g_i
            gi[rows, :] = fr * g_i - fi * g_r
        dcre_ref[...] = _collect_groups(fr * mre - fi * mim)
        dcim_ref[...] = _collect_groups(-(fr * mim + fi * mre))
        dfr_ref[...] = dfr
        dfi_ref[...] = dfi
        accr, acci = _chunk_scan(gr, gi, lr, -li, Lc, not reverse, prev=(xr, xi))
        dlr_ref[...] = _colsum(accr)
        dli_ref[...] = _colsum(acci)
        dbre = jnp.zeros((LANES, SC), F32)
        dbim = jnp.zeros((LANES, SC), F32)
        for blk in range(nblk):
            rows = pl.ds(blk * ROW_BLOCK, ROW_BLOCK)
            ub = u_ref[rows, :].astype(BF16)
            a_r, a_i = gr[rows, :].astype(BF16), gi[rows, :].astype(BF16)
            dbre += lax.dot_general(ub, a_r, _TN, preferred_element_type=F32)
            dbim += lax.dot_general(ub, a_i, _TN, preferred_element_type=F32)
            du_ref[rows, :] = (dui_ref[rows, :] + lax.dot_general(a_r, bre, _NT, preferred_element_type=F32)
                               + lax.dot_general(a_i, bim, _NT, preferred_element_type=F32)).astype(du_ref.dtype)
        dbre_ref[...] = _collect_groups(dbre)
        dbim_ref[...] = _collect_groups(dbim)

    col, mat, vec = _ssm_specs(T, direction)
    mat_out = pl.BlockSpec((None, SSM_GROUP, SC), lambda m: (m, 0, 0))
    vec_out = pl.BlockSpec((None, 1, SC), lambda m: (m, 0, 0))
    mat_shape = jax.ShapeDtypeStruct((nb, SSM_GROUP, SC), F32)
    vec_shape = jax.ShapeDtypeStruct((nb, 1, SC), F32)
    return pl.pallas_call(
        kern, name=name, grid=(nb,),
        in_specs=[col, col, col, mat, mat, mat, mat, vec, vec, vec, vec],
        out_specs=[col, mat_out, mat_out, mat_out, mat_out, vec_out, vec_out, vec_out, vec_out],
        out_shape=[jax.ShapeDtypeStruct((T, W), du_dtype), mat_shape, mat_shape, mat_shape, mat_shape,
                   vec_shape, vec_shape, vec_shape, vec_shape],
        scratch_shapes=[pltpu.VMEM((T, SC), F32)] * 4,
        compiler_params=_params(("arbitrary",)),
    )(u_bf, dy_bf, du_in, *mats, *vecs)


def _groups_side_by_side(p, channel_axis):
    d, G = p.shape[:2]
    nb = G // SSM_BLOCK_GROUPS
    v = p.reshape(d, nb, SSM_BLOCK_GROUPS, p.shape[2], p.shape[3])
    v = v.transpose(0, 1, 4, 2, 3) if channel_axis == 3 else v.transpose(0, 1, 3, 2, 4)
    return v.reshape(d, nb, SSM_GROUP, SSM_BLOCK_GROUPS * SSM_STATE)


def _groups_apart(m, channel_axis):
    d, nb = m.shape[:2]
    v = m.reshape(d, nb, SSM_GROUP, SSM_BLOCK_GROUPS, SSM_STATE)
    v = v.transpose(0, 1, 3, 4, 2) if channel_axis == 3 else v.transpose(0, 1, 3, 2, 4)
    return v.reshape((d, nb * SSM_BLOCK_GROUPS) + v.shape[3:])


def _conv_taps(ref, c, R, T):
    n = T // R
    r0 = pl.multiple_of(c * R, R)
    main = ref[pl.ds(r0, R), :]
    before = ref[pl.ds(pl.multiple_of(jnp.maximum(r0 - 8, 0), 8), 8), :]
    after = ref[pl.ds(pl.multiple_of(jnp.minimum(r0 + R, T - 8), 8), 8), :]
    ext = jnp.concatenate([jnp.where(c > 0, before, 0.0), main, jnp.where(c < n - 1, after, 0.0)], axis=0)
    m1 = pltpu.roll(ext, 1, 0)[8:8 + R]
    p1 = pltpu.roll(ext, R + 15, 0)[8:8 + R]
    return m1, main, p1, r0


def _conv_specs(T, F, cb):
    nj = F // cb
    lo = lambda rows: pl.BlockSpec((rows, cb), lambda j: (0, j))
    hi = lambda rows: pl.BlockSpec((rows, cb), lambda j: (0, j + nj))
    return nj, lo, hi


def _convact_fwd(up, conv_w, conv_b, F):
    T = up.shape[0]
    cb = _pick(F, (256, 128))
    R = _pick(T, (256, 128, 64))
    nj, lo, hi = _conv_specs(T, F, cb)

    def kern(uv, ug, wv, wg, bv, bg, o_ref):
        def body(c, carry):
            v1, v0, v2, r0 = _conv_taps(uv, c, R, T)
            g1, g0, g2, _ = _conv_taps(ug, c, R, T)
            val = v1 * wv[0:1, :] + v0 * wv[1:2, :] + v2 * wv[2:3, :] + bv[...]
            gate = g1 * wg[0:1, :] + g0 * wg[1:2, :] + g2 * wg[2:3, :] + bg[...]
            o_ref[pl.ds(r0, R), :] = (val * (gate * _sigmoid(gate))).astype(o_ref.dtype)
            return carry

        lax.fori_loop(0, T // R, body, 0)

    return pl.pallas_call(
        kern, name="convact_fwd", grid=(nj,),
        in_specs=[lo(T), hi(T), lo(3), hi(3), lo(1), hi(1)],
        out_specs=lo(T), out_shape=jax.ShapeDtypeStruct((T, F), BF16),
        compiler_params=_params(("arbitrary",)),
    )(up, up, conv_w, conv_w, conv_b, conv_b)


def _convact_bwd(up, dact, conv_w, conv_b, F):
    T = up.shape[0]
    cb = _pick(F, (256, 128))
    R = _pick(T, (256, 128, 64))
    nj, lo, hi = _conv_specs(T, F, cb)

    def kern(uv, ug, da, wv, wg, bv, bg, dup, dwv, dwg, dbv, dbg, sv, sg):
        zero = jnp.zeros((1, cb), F32)

        def pass_a(c, acc):
            v1, v0, v2, r0 = _conv_taps(uv, c, R, T)
            g1, g0, g2, _ = _conv_taps(ug, c, R, T)
            val = v1 * wv[0:1, :] + v0 * wv[1:2, :] + v2 * wv[2:3, :] + bv[...]
            gate = g1 * wg[0:1, :] + g0 * wg[1:2, :] + g2 * wg[2:3, :] + bg[...]
            sig = _sigmoid(gate)
            d = da[pl.ds(r0, R), :]
            dval = d * (gate * sig)
            dgate = d * val * (sig * (1.0 + gate * (1.0 - sig)))
            sv[pl.ds(r0, R), :] = dval
            sg[pl.ds(r0, R), :] = dgate
            terms = (dval * v1, dval * v0, dval * v2, dval, dgate * g1, dgate * g0, dgate * g2, dgate)
            return tuple(a + _colsum(t) for a, t in zip(acc, terms))

        acc = lax.fori_loop(0, T // R, pass_a, (zero,) * 8)
        for k in range(3):
            dwv[k:k + 1, :] = acc[k]
            dwg[k:k + 1, :] = acc[4 + k]
        dbv[...] = acc[3]
        dbg[...] = acc[7]

        def pass_b(c, carry):
            v1, v0, v2, r0 = _conv_taps(sv, c, R, T)
            g1, g0, g2, _ = _conv_taps(sg, c, R, T)
            dup[0, pl.ds(r0, R), :] = (v2 * wv[0:1, :] + v0 * wv[1:2, :] + v1 * wv[2:3, :]).astype(dup.dtype)
            dup[1, pl.ds(r0, R), :] = (g2 * wg[0:1, :] + g0 * wg[1:2, :] + g1 * wg[2:3, :]).astype(dup.dtype)
            return carry

        lax.fori_loop(0, T // R, pass_b, 0)

    dup, dwv, dwg, dbv, dbg = pl.pallas_call(
        kern, name="convact_bwd", grid=(nj,),
        in_specs=[lo(T), hi(T), lo(T), lo(3), hi(3), lo(1), hi(1)],
        out_specs=[pl.BlockSpec((2, T, cb), lambda j: (0, 0, j)), lo(3), lo(3), lo(1), lo(1)],
        out_shape=[jax.ShapeDtypeStruct((2, T, F), BF16),
                   jax.ShapeDtypeStruct((3, F), F32), jax.ShapeDtypeStruct((3, F), F32),
                   jax.ShapeDtypeStruct((1, F), F32), jax.ShapeDtypeStruct((1, F), F32)],
        scratch_shapes=[pltpu.VMEM((T, cb), F32), pltpu.VMEM((T, cb), F32)],
        compiler_params=_params(("arbitrary",)),
    )(up, up, dact, conv_w, conv_w, conv_b, conv_b)
    return dup, (dwv, dwg), jnp.concatenate([dbv, dbg], axis=1)


def _peers(x, y, c, with_self=False):
    out = []
    for k in range(0 if with_self else 1, N_DEV):
        px = 1 - x if k & 4 else x
        py = 1 - y if k & 2 else y
        pc = 1 - c if k & 1 else c
        out.append((k, (px, py, pc), 4 * px + 2 * py + pc))
    return out


def _exchange_start(name, bufs, modes, after):
    handles, token = _exchange_start_groups(name, [(bufs, modes)], after)
    return handles[0], token


def _exchange_start_groups(name, groups, after):
    bufs = [b for g, _ in groups for b in g]
    modes = [m for _, ms in groups for m in ms]
    first = [int(v) for v in np.cumsum([0] + [len(g) for g, _ in groups])]
    n, ng = len(bufs), len(groups)
    lands = [lax.empty((N_DEV,) + tuple(b.shape[-2:]), b.dtype) for b in bufs]

    def body(*refs):
        ins, land_in = refs[:n], refs[n:2 * n]
        sems = refs[2 * n + 1:2 * n + 1 + 2 * ng]
        token = refs[-1]
        x, y, c = lax.axis_index("x"), lax.axis_index("y"), lax.axis_index("c")
        me = 4 * x + 2 * y + c
        for g in range(ng):
            for b in range(first[g], first[g + 1]):
                for k, peer, slot in _peers(x, y, c, with_self=True):
                    sem = (b - first[g]) * N_DEV + k
                    pltpu.make_async_remote_copy(
                        src_ref=ins[b] if modes[b] == 'gather' else ins[b].at[slot], dst_ref=land_in[b].at[me],
                        send_sem=sems[2 * g].at[sem], recv_sem=sems[2 * g + 1].at[sem],
                        device_id=peer, device_id_type=MESH_ID).start()
        token[...] = jnp.zeros_like(token)

    hbm = pl.BlockSpec(memory_space=pltpu.HBM)
    sem_spec = pl.BlockSpec(memory_space=pltpu.SEMAPHORE)
    operands = [pltpu.with_memory_space_constraint(a, pltpu.HBM) for a in bufs + lands]
    sem_shapes = [pltpu.SemaphoreType.DMA((len(g) * N_DEV,)) for g, _ in groups for _ in range(2)]
    res = pl.pallas_call(
        body, name=name,
        out_shape=(*sem_shapes, *[pltpu.HBM(a.shape, a.dtype) for a in operands], jax.ShapeDtypeStruct((8, LANES), F32)),
        in_specs=[hbm] * (2 * n) + [pl.BlockSpec(memory_space=pl.ANY)],
        out_specs=(*([sem_spec] * (2 * ng)), *([hbm] * (2 * n)), pl.BlockSpec(memory_space=pltpu.VMEM)),
        input_output_aliases={i: 2 * ng + i for i in range(2 * n)},
        compiler_params=pltpu.CompilerParams(has_side_effects=pltpu.SideEffectType.DATAFLOW_SIDE_EFFECTING),
    )(*operands, after)
    srcs, landed = res[2 * ng:2 * ng + n], res[2 * ng + n:2 * ng + 2 * n]
    handles = [(res[2 * g], res[2 * g + 1], list(srcs[first[g]:first[g + 1]]), list(landed[first[g]:first[g + 1]]),
                tuple(modes[first[g]:first[g + 1]])) for g in range(ng)]
    return handles, res[-1]


def _exchange_wait(name, handle, after):
    send_sems, recv_sems, srcs, lands, modes = handle
    n = len(srcs)

    def body(*refs):
        src_in, land_in = refs[:n], refs[n:2 * n]
        send_ref, recv_ref = refs[2 * n], refs[2 * n + 1]
        x, y, c = lax.axis_index("x"), lax.axis_index("y"), lax.axis_index("c")
        for b in range(n):
            for k, peer, slot in _peers(x, y, c, with_self=True):
                sem = b * N_DEV + k
                copy = pltpu.make_async_remote_copy(
                    src_ref=src_in[b] if modes[b] == 'gather' else src_in[b].at[slot], dst_ref=land_in[b].at[slot],
                    send_sem=send_ref.at[sem], recv_sem=recv_ref.at[sem],
                    device_id=peer, device_id_type=MESH_ID)
                copy.wait_send()
                copy.wait_recv()

    hbm = pl.BlockSpec(memory_space=pltpu.HBM)
    sem_spec = pl.BlockSpec(memory_space=pltpu.SEMAPHORE)
    res = pl.pallas_call(
        body, name=name,
        out_shape=tuple(pltpu.HBM(a.shape, a.dtype) for a in srcs + lands),
        in_specs=[hbm] * (2 * n) + [sem_spec, sem_spec, pl.BlockSpec(memory_space=pl.ANY)],
        out_specs=tuple([hbm] * (2 * n)),
        input_output_aliases={i: i for i in range(2 * n)},
        compiler_params=pltpu.CompilerParams(has_side_effects=pltpu.SideEffectType.DATAFLOW_SIDE_EFFECTING),
    )(*srcs, *lands, send_sems, recv_sems, after)
    return list(res[n:])


def _adamw(w, g, m, v):
    m2 = ADAM_B1 * m + (1.0 - ADAM_B1) * g
    v2 = ADAM_B2 * v + (1.0 - ADAM_B2) * (g * g)
    m_hat = m2 / (1.0 - ADAM_B1 ** ADAM_STEP)
    v_hat = v2 / (1.0 - ADAM_B2 ** ADAM_STEP)
    return -ADAM_LR * (m_hat / (jnp.sqrt(v_hat) + ADAM_EPS) + ADAM_WD * w), m2, v2


def _sum_adamw(name, landed, w, m, v):
    R, C = w.shape
    row_bytes = N_DEV * (-(-C // LANES) * LANES) * landed.dtype.itemsize
    tiles = [d for d in range(16, R, 16) if R % d == 0 and d * row_bytes <= ADAM_TILE_BYTES]
    tr = max(tiles) if tiles and R * row_bytes > ADAM_TILE_BYTES else R

    def kern(x_ref, w_ref, m_ref, v_ref, g_out, d_out, m_out, v_out):
        g = x_ref[0].astype(F32)
        for q in range(1, N_DEV):
            g = g + x_ref[q].astype(F32)
        g_out[...] = g
        d_out[...], m_out[...], v_out[...] = _adamw(w_ref[...], g, m_ref[...], v_ref[...])

    blk = pl.BlockSpec((tr, C), lambda i: (i, 0))
    return pl.pallas_call(
        kern, name=name, grid=(R // tr,),
        in_specs=[pl.BlockSpec((N_DEV, tr, C), lambda i: (0, i, 0)), blk, blk, blk],
        out_specs=[blk] * 4, out_shape=[jax.ShapeDtypeStruct((R, C), F32)] * 4,
        compiler_params=_params(("arbitrary",)),
    )(landed, w, m, v)


def _sum_adamw_many(name, items):
    n = len(items)

    def kern(*refs):
        ins, outs = refs[:4 * n], refs[4 * n:]
        for k in range(n):
            x_ref, w_ref, m_ref, v_ref = ins[4 * k:4 * k + 4]
            g = x_ref[0].astype(F32)
            for q in range(1, N_DEV):
                g = g + x_ref[q].astype(F32)
            outs[4 * k][...] = g
            outs[4 * k + 1][...], outs[4 * k + 2][...], outs[4 * k + 3][...] = _adamw(w_ref[...], g, m_ref[...], v_ref[...])

    out_shape = [jax.ShapeDtypeStruct(w.shape, F32) for _, w, _, _ in items for _ in range(4)]
    res = pl.pallas_call(kern, name=name, out_shape=out_shape)(*[a for item in items for a in item])
    return [res[4 * k:4 * k + 4] for k in range(n)]


def _view2d(a):
    if a.ndim == 1:
        return a.reshape(1, -1)
    return a.reshape(-1, a.shape[-1])


def kernel(x, norm_mix_g, w_in, pool_w, pool_scale, ssm_log_neg_a_re, ssm_a_im, ssm_log_dt, ssm_b_re, ssm_b_im, ssm_c_re, ssm_c_im, ssm_d, glu_w, glu_b, out_norm_pool_g, out_norm_ssm_g, w_out, norm_ffn_g, w_up, conv_w, conv_b, w_down, final_norm_g, loss_target, m_norm_mix_g, m_w_in, m_pool_w, m_pool_scale, m_ssm_log_neg_a_re, m_ssm_a_im, m_ssm_log_dt, m_ssm_b_re, m_ssm_b_im, m_ssm_c_re, m_ssm_c_im, m_ssm_d, m_glu_w, m_glu_b, m_out_norm_pool_g, m_out_norm_ssm_g, m_w_out, m_norm_ffn_g, m_w_up, m_conv_w, m_conv_b, m_w_down, m_final_norm_g, v_norm_mix_g, v_w_in, v_pool_w, v_pool_scale, v_ssm_log_neg_a_re, v_ssm_a_im, v_ssm_log_dt, v_ssm_b_re, v_ssm_b_im, v_ssm_c_re, v_ssm_c_im, v_ssm_d, v_glu_w, v_glu_b, v_out_norm_pool_g, v_out_norm_ssm_g, v_w_out, v_norm_ffn_g, v_w_up, v_conv_w, v_conv_b, v_w_down, v_final_norm_g):
    given = dict(locals())
    weights = {n: given[n] for n in WEIGHTS}
    mom1 = {n: given["m_" + n] for n in WEIGHTS}
    mom2 = {n: given["v_" + n] for n in WEIGHTS}

    xs = x[0]
    tgt = loss_target[0]
    T, D = xs.shape
    DP = len(POOL_WINDOWS) * POOL_GROUP
    DS = D - DP
    G = DS // SSM_GROUP
    N, H = SSM_STATE, SSM_GROUP
    F2 = w_up.shape[1] * N_DEV
    F = F2 // 2
    fs = w_up.shape[1]
    row = lambda a: a.reshape(1, -1)

    gather_groups = [([w_in.astype(BF16)], ['gather']), ([glu_w.astype(BF16), w_out.astype(BF16)], ['gather'] * 2),
                     ([w_up.T.astype(BF16), conv_w], ['gather'] * 2), ([w_down.astype(BF16)], ['gather'])]
    (in_handle, mix_handle, up_handle, down_handle), token = _exchange_start_groups("gather_start", gather_groups, w_in)
    CB = row(conv_b)

    g1, g2, g3 = row(norm_mix_g), row(norm_ffn_g), row(final_norm_g)
    gp, gs = row(out_norm_pool_g), row(out_norm_ssm_g)
    (xn,) = _rowmap("norm_mix", _rms_fwd, [xs], [g1 + token[0:1, 0:1]], [(D, BF16)])
    W_in = _exchange_wait("gather_in_wait", in_handle, xn)[0].reshape(D, D)
    halves = lambda r: (r[:, :DP], r[:, DP:])
    u, u_s5 = _mm_rows("proj_in", xn, W_in, 'nn', halves, [], [], [(DP, F32), (DS, F32)])
    tmp = _pick(T, (256, 128))
    pool_w_bf = pool_w.astype(BF16)
    ypn = _pool_fwd(u, pool_w_bf, row(pool_scale), gp, tmp)

    u_ssm = _to_chunk_rows(u_s5)
    u_ssm_bf = u_ssm.astype(BF16)
    a_rows = (ssm_log_neg_a_re.reshape(2 * G, N), ssm_a_im.reshape(2 * G, N), ssm_log_dt.reshape(2 * G, 1))
    disc = _ssm_params_fwd(*a_rows)
    nb = G // SSM_BLOCK_GROUPS
    vecs = [d.reshape(2, nb, 1, SSM_BLOCK_GROUPS * N) for d in disc]
    mats = [_groups_side_by_side(ssm_b_re, 3), _groups_side_by_side(ssm_b_im, 3),
            _groups_side_by_side(ssm_c_re, 2), _groups_side_by_side(ssm_c_im, 2)]
    y_dir = [_ssm_fwd("ssm_fwd_%d" % d, u_ssm_bf, mats, vecs, d) for d in range(2)]

    def mix_post(yf, yb, us, d, gw, gb, g):
        y = yf + yb + d * us
        z = _gelu(y)
        gate = _sigmoid(jnp.dot(z.astype(BF16), gw, preferred_element_type=F32) + gb)
        return _rms_fwd(z * gate, g), y

    mix_landed = _exchange_wait("gather_mix_wait", mix_handle, y_dir[1])
    W_glu = mix_landed[0].reshape(DS, DS)
    W_out = mix_landed[1].reshape(D, D)
    ysn, y_ssm = _rowmap("ssm_post", mix_post, [y_dir[0], y_dir[1], u_ssm], [row(ssm_d), W_glu, row(glu_b), gs],
                         [(DS, BF16), (DS, F32)])
    ycat = jnp.concatenate([ypn, _from_chunk_rows(ysn)], axis=1)
    h1, hn = _mm_rows("proj_out", ycat, W_out, 'nn', lambda h, g: (h, _rms_fwd(h, g)), [], [g2],
                      [(D, F32), (D, BF16)], add=xs)
    up_landed = _exchange_wait("gather_up_wait", up_handle, hn)
    W_up_t = up_landed[0].reshape(F2, D)
    CW = up_landed[1].transpose(1, 0, 2).reshape(3, F2)
    up = _mm("ffn_up", hn, W_up_t, 'nt', F32)
    act = _convact_fwd(up, CW, CB, F)
    W_down = _exchange_wait("gather_down_wait", down_handle, act)[0].reshape(F, D)

    def head(h, t, g):
        r = lax.rsqrt(_rowmean(h * h) + EPS)
        hh = h * r
        e = hh * g - t
        loss = 0.5 * jnp.sum(_rowmean(e * e), keepdims=True)
        dy = e * (1.0 / D)
        dxh = dy * g
        dh = r * (dxh - hh * _rowmean(dxh * hh))
        return dh, dh, jnp.broadcast_to(loss, (1, LANES)), _colsum(dy * hh)

    dh2, dh2_bf, loss_acc, dg3 = _mm_rows("ffn_down", act, W_down, 'nn', head, [tgt], [g3],
                                          [(D, F32), (D, BF16)], [(1, LANES), (1, D)], add=h1)

    dact = _mm("ffn_down_dx", dh2_bf, W_down, 'nt', F32)
    dW_down = _mm("ffn_down_dw", act, dh2_bf, 'tn', BF16)
    dup, dCW, dCB = _convact_bwd(up, dact, CW, CB, F)
    dhn_val = _mm("ffn_up_dx_val", dup, W_up_t, 'nn', F32, b_rows=(2, 0), a_plane=0)
    dW_up_t = _mm("ffn_up_dw", dup, hn, 'tn', BF16).reshape(F2, D)

    def shards(full_grad):
        return full_grad.reshape((N_DEV, full_grad.shape[0] // N_DEV) + full_grad.shape[1:])

    half = N_DEV // 2
    dCW_sh = jnp.concatenate([h.reshape(3, half, fs).transpose(1, 0, 2) for h in dCW], axis=0)
    ffn_sent = [shards(dW_down), shards(dW_up_t), dCW_sh]
    ffn_handle, token = _exchange_start("scatter_ffn_start", ffn_sent, ['scatter'] * 3, after=dhn_val)
    g2_late = g2 + token[0:1, 0:1]

    def norm_bwd_add(dy, hx, res, g):
        dx, dg = _rms_bwd(hx, g, dy)
        return dx + res, dg

    def norm_bwd_add2(dy, hx, res, g):
        dx, dg = _rms_bwd(hx, g, dy)
        return dx + res, dx + res, dg

    dh1, dh1_bf, dg2 = _mm_rows("ffn_up_dx_gate", dup, W_up_t, 'nn', norm_bwd_add2, [h1, dh2], [g2_late],
                                [(D, F32), (D, BF16)], [(1, D)], add=dhn_val, b_rows=(2, 1), a_plane=1)
    dycat, d_ysn = _mm_rows("proj_out_dx", dh1_bf, W_out, 'nt', halves, [], [], [(DP, F32), (DS, F32)])
    dW_out = _mm("proj_out_dw", ycat, dh1_bf, 'tn', BF16)

    out_sent = [shards(dW_out)]
    out_handle, token = _exchange_start("scatter_out_start", out_sent, ['scatter'], after=dycat)
    q, dpooled, dpool_w, dpool_scale, dgp = _pool_bwd_rows(u, dycat, pool_w_bf, row(pool_scale),
                                                           gp + token[0:1, 0:1], tmp)

    def mix_post_bwd(y, us, dyn, d, gw, gb, g):
        z = _gelu(y)
        gz = z.astype(BF16)
        sig = _sigmoid(jnp.dot(gz, gw, preferred_element_type=F32) + gb)
        dys, dg = _rms_bwd(z * sig, g, dyn)
        dgl = dys * z * sig * (1.0 - sig)
        dgl_bf = dgl.astype(BF16)
        dz = dys * sig + lax.dot_general(dgl_bf, gw, (((1,), (1,)), ((), ())), preferred_element_type=F32)
        dgw = lax.dot_general(gz, dgl_bf, (((0,), (0,)), ((), ())), preferred_element_type=F32)
        dyv = dz * _gelu_grad(y)
        return dyv, dyv * d, dgw, _colsum(dgl), _colsum(dyv * us), dg

    dyv, du_dir, dglu_w, dglu_b, dssm_d, dgs = _rowmap(
        "ssm_post_bwd", mix_post_bwd, [y_ssm, u_ssm, _to_chunk_rows(d_ysn)], [row(ssm_d), W_glu, row(glu_b), gs],
        [(DS, BF16), (DS, F32)], [(DS, DS), (1, DS), (1, DS), (1, DS)])

    du_ssm = du_dir
    dB, dC, dvec = [], [], []
    for d in range(2):
        res = _ssm_bwd("ssm_bwd_%d" % d, u_ssm_bf, dyv, du_ssm, mats, vecs, d, BF16 if d == 1 else F32)
        du_ssm = res[0]
        dB.append(res[1:3])
        dC.append(res[3:5])
        dvec.append(res[5:9])
    grad_b = [_groups_apart(jnp.stack([dB[0][k], dB[1][k]]), 3) for k in range(2)]
    grad_c = [_groups_apart(jnp.stack([dC[0][k], dC[1][k]]), 2) for k in range(2)]
    cots = [jnp.stack([dvec[0][k], dvec[1][k]]).reshape(2 * G, N) for k in range(4)]
    d_a_re, d_a_im, d_log_dt = _ssm_params_bwd(*a_rows, cots)

    rep_grads = {
        'pool_w': dpool_w, 'pool_scale': dpool_scale, 'ssm_log_neg_a_re': d_a_re, 'ssm_a_im': d_a_im,
        'ssm_log_dt': d_log_dt, 'ssm_b_re': grad_b[0], 'ssm_b_im': grad_b[1], 'ssm_c_re': grad_c[0], 'ssm_c_im': grad_c[1],
        'ssm_d': dssm_d, 'glu_b': dglu_b, 'out_norm_pool_g': dgp, 'out_norm_ssm_g': dgs, 'norm_ffn_g': dg2,
        'conv_b': dCB, 'final_norm_g': dg3}
    wide = ('pool_w', 'ssm_b_re', 'ssm_b_im', 'ssm_c_re', 'ssm_c_im')

    def narrow(n):
        return weights[n].ndim == 4 and weights[n].shape[-1] < LANES // 2

    def travel(n):
        g = rep_grads[n]
        if narrow(n):
            g = jnp.swapaxes(g, 2, 3)
        if n in wide:
            return g.reshape(-1, PACK_W if g.size % PACK_W == 0 else LANES).astype(BF16)
        return _view2d(g.reshape(weights[n].shape))

    small = tuple(n for n in REPLICATED if n != 'norm_mix_g')
    small_sent = [shards(dglu_w.astype(BF16))] + [travel(n) for n in small] + [loss_acc]
    small_handle, token = _exchange_start("reduce_small_start", small_sent, ['scatter'] + ['gather'] * (len(small) + 1),
                                          after=d_a_re)

    du = _pool_bwd_band(q, dpooled, _from_chunk_rows(du_ssm), token, tmp)
    dW_in = _mm("proj_in_dw", xn, du, 'tn', BF16)
    in_handle, token = _exchange_start("scatter_in_start", [shards(dW_in)], ['scatter'], after=dW_in)
    dx, dg1 = _mm_rows("proj_in_dx", du, W_in + token[0, 0].astype(BF16), 'nt', norm_bwd_add, [xs, dh1], [g1],
                       [(D, F32)], [(1, D)])
    rep_grads['norm_mix_g'] = dg1
    last_handle, token = _exchange_start("reduce_last_start", [travel('norm_mix_g')], ['gather'], after=dx)

    grads, delta, new_m, new_v = {}, {}, {}, {}

    def update(n, landed_n):
        shape = weights[n].shape
        if narrow(n):
            swapped = lambda a: _view2d(jnp.swapaxes(a, 2, 3))
            landed_n = landed_n.reshape((N_DEV,) + swapped(weights[n]).shape)
            res = _sum_adamw("adamw_" + n, landed_n, swapped(weights[n]), swapped(mom1[n]), swapped(mom2[n]))
            back = lambda r: jnp.swapaxes(r.reshape(shape[:2] + (shape[3], shape[2])), 2, 3)
            grads[n], delta[n], new_m[n], new_v[n] = [back(r) for r in res]
            return
        if n == 'w_up':
            res = _sum_adamw("adamw_" + n, landed_n, weights[n].T, mom1[n].T, mom2[n].T)
            grads[n], delta[n], new_m[n], new_v[n] = [r.T for r in res]
            return
        landed_n = landed_n.reshape((N_DEV,) + _view2d(weights[n]).shape)
        res = _sum_adamw("adamw_" + n, landed_n, _view2d(weights[n]), _view2d(mom1[n]), _view2d(mom2[n]))
        grads[n], delta[n], new_m[n], new_v[n] = [r.reshape(shape) for r in res]

    def update_many(name, named_landed):
        items = [(l.reshape((N_DEV,) + _view2d(weights[n]).shape), _view2d(weights[n]), _view2d(mom1[n]), _view2d(mom2[n]))
                 for n, l in named_landed]
        for (n, _), res in zip(named_landed, _sum_adamw_many(name, items)):
            grads[n], delta[n], new_m[n], new_v[n] = [r.reshape(weights[n].shape) for r in res]

    ffn_landed = _exchange_wait("scatter_ffn_wait", ffn_handle, token)
    for n, l in zip(('w_down', 'w_up'), ffn_landed):
        update(n, l)
    (out_landed,) = _exchange_wait("scatter_out_wait", out_handle, token)
    update('w_out', out_landed)
    small_landed = _exchange_wait("reduce_small_wait", small_handle, token)
    tiny = [('conv_w', ffn_landed[2])]
    for n, l in zip(('glu_w',) + small, small_landed):
        if weights[n].size <= ADAM_TINY:
            tiny.append((n, l))
        else:
            update(n, l)
    update_many("adamw_tiny", tiny)

    def loss_sum(parts_ref, o_ref):
        s = parts_ref[0]
        for q in range(1, N_DEV):
            s = s + parts_ref[q]
        o_ref[...] = s

    loss = pl.pallas_call(loss_sum, name="loss_sum", out_shape=jax.ShapeDtypeStruct((1, LANES), F32))(small_landed[-1])[0, 0]

    update('w_in', _exchange_wait("scatter_in_wait", in_handle, grads[small[-1]])[0])
    update('norm_mix_g', _exchange_wait("reduce_last_wait", last_handle, grads['w_in'])[0])

    return (loss, dx[None], *[grads[n] for n in WEIGHTS], *[delta[n] for n in WEIGHTS],
            *[new_m[n] for n in WEIGHTS], *[new_v[n] for n in WEIGHTS])
```

```python
import functools

import jax
import jax.numpy as jnp
import numpy as np
from jax import lax
from jax.experimental import pallas as pl
from jax.experimental.pallas import tpu as pltpu

F32 = jnp.float32
BF16 = jnp.bfloat16
MESH_ID = pl.DeviceIdType.MESH

N_DEV = 8
EPS = 1e-6
POOL_WINDOWS = (2, 4, 8, 16)
POOL_GROUP = 128
POOL_PAD = 64
SSM_GROUP = 16
SSM_STATE = 64
SSM_BLOCK_GROUPS = 8
N_CHUNK = 8
ROW_BLOCK = 512
SCAN_UNROLL = 8
GROUP_SHIFT = 4
STATE_SHIFT = 6
_NT = (((1,), (1,)), ((), ()))
_TN = (((0,), (0,)), ((), ()))
LANES = 128
PACK_W = 1024
VMEM_LIMIT = 56 * 1024 * 1024
MM_VMEM_BUDGET = 40 * 1024 * 1024
MM_TILE_CAP = 1408
MM_ROWS_TILE = 512
ADAM_TILE_BYTES = 4 * 1024 * 1024
ADAM_TINY = 8192

ADAM_LR = 0.001
ADAM_B1 = 0.9
ADAM_B2 = 0.999
ADAM_EPS = 1e-08
ADAM_WD = 0.01
ADAM_STEP = 10

WEIGHTS = ['norm_mix_g', 'w_in', 'pool_w', 'pool_scale', 'ssm_log_neg_a_re', 'ssm_a_im', 'ssm_log_dt',
           'ssm_b_re', 'ssm_b_im', 'ssm_c_re', 'ssm_c_im', 'ssm_d', 'glu_w', 'glu_b', 'out_norm_pool_g',
           'out_norm_ssm_g', 'w_out', 'norm_ffn_g', 'w_up', 'conv_w', 'conv_b', 'w_down', 'final_norm_g']
SHARDED = ('w_in', 'w_out', 'w_down', 'glu_w', 'w_up', 'conv_w')
REPLICATED = tuple(n for n in WEIGHTS if n not in SHARDED)


def _pick(n, prefs):
    for p in prefs:
        if n % p == 0:
            return p
    return n


def _params(sem, vmem=None):
    return pltpu.CompilerParams(dimension_semantics=sem, vmem_limit_bytes=vmem or VMEM_LIMIT)


def _tiles(n, cap):
    return [d for d in range(LANES, min(n, cap) + 1, LANES) if n % d == 0] or [n]


def _mm(name, a, b, mode, out_dtype, add=None, b_rows=(1, 0), a_plane=None):
    batch = a.shape[0] if a.ndim == 3 and a_plane is None else None
    if mode == 'nn':
        (M, K), N = a.shape[-2:], b.shape[1]
        assert b.shape[0] == K * b_rows[0]
    elif mode == 'nt':
        (M, K), (N, _) = a.shape[-2:], b.shape
    else:
        (K, M), (_, N) = a.shape[-2:], b.shape
    dims = {'nn': (((1,), (0,)), ((), ())), 'nt': (((1,), (1,)), ((), ())), 'tn': (((0,), (0,)), ((), ()))}[mode]
    sa, sb, so = a.dtype.itemsize, b.dtype.itemsize, jnp.dtype(out_dtype).itemsize
    best = None
    for tm in _tiles(M, MM_TILE_CAP):
        for tn in _tiles(N, MM_TILE_CAP):
            need = 2 * (tm * K * sa + tn * K * sb + tm * tn * (so + (4 if add is not None else 0)))
            if need <= MM_VMEM_BUDGET:
                key = (tm * tn / (tm + tn), tm * tn)
                if best is None or key > best[0]:
                    best = (key, tm, tn)
    _, tm, tn = best
    rows_inner = a.size * sa * (N // tn) + b.size * sb < a.size * sa + b.size * sb * (M // tm)

    def kern(*refs):
        a_ref, b_ref = refs[:2]
        o_ref = refs[-1]
        r = lax.dot_general(a_ref[...].astype(BF16), b_ref[...].astype(BF16), dims, preferred_element_type=F32)
        if add is not None:
            r = r + refs[2][...]
        o_ref[...] = r.astype(o_ref.dtype)

    if rows_inner:
        grid = (N // tn, M // tm)
        ij = lambda g0, g1: (g1, g0)
    else:
        grid = (M // tm, N // tn)
        ij = lambda g0, g1: (g0, g1)

    def spec(block, index, batched=False):
        if batch is None:
            return pl.BlockSpec(block, lambda g0, g1: index(*ij(g0, g1)))
        if batched:
            return pl.BlockSpec((None,) + block, lambda p, g0, g1: (p,) + index(*ij(g0, g1)))
        return pl.BlockSpec(block, lambda p, g0, g1: index(*ij(g0, g1)))

    if a_plane is not None:
        assert mode != 'tn'
        a_spec = pl.BlockSpec((None, tm, K), lambda g0, g1: (a_plane, ij(g0, g1)[0], 0))
    else:
        a_spec = (spec((K, tm), lambda i, j: (0, i), True) if mode == 'tn' else spec((tm, K), lambda i, j: (i, 0), True))
    b_spec = (spec((tn, K), lambda i, j: (j, 0)) if mode == 'nt' else spec((K, tn), lambda i, j: (b_rows[1], j)))
    in_specs = [a_spec, b_spec]
    args = [a, b]
    if add is not None:
        in_specs.append(spec((tm, tn), lambda i, j: (i, j), True))
        args.append(add)
    lead = () if batch is None else (batch,)
    return pl.pallas_call(
        kern, name=name, grid=lead + grid, in_specs=in_specs,
        out_specs=spec((tm, tn), lambda i, j: (i, j), True),
        out_shape=jax.ShapeDtypeStruct(lead + (M, N), out_dtype),
        compiler_params=_params(("parallel",) * (len(lead) + 2)),
    )(*args)


def _store_rows_and_sums(i, outs, res, n_row):
    for o, v in zip(outs[:n_row], res[:n_row]):
        o[...] = v.astype(o.dtype)
    for o, v in zip(outs[n_row:], res[n_row:]):
        @pl.when(i == 0)
        def _(o=o, v=v):
            o[...] = v

        @pl.when(i != 0)
        def _(o=o, v=v):
            o[...] += v


def _mm_rows(name, a, b, mode, fn, rows, fulls, out_rows, out_accs=(), add=None, b_rows=(1, 0), a_plane=None):
    M, K = a.shape[-2:]
    N = b.shape[1] if mode == 'nn' else b.shape[0]
    assert b.shape[0] == (K * b_rows[0] if mode == 'nn' else N)
    dims = _NT if mode == 'nt' else (((1,), (0,)), ((), ()))
    tm = min(MM_ROWS_TILE, M)
    n_extra, n_row = len(rows) + len(fulls), len(out_rows)

    def kern(*refs):
        i = pl.program_id(0)
        r = lax.dot_general(refs[0][...].astype(BF16), refs[1][...].astype(BF16), dims, preferred_element_type=F32)
        pos = 2
        if add is not None:
            r = r + refs[pos][...]
            pos += 1
        res = fn(r, *[x[...] for x in refs[pos:pos + n_extra]])
        _store_rows_and_sums(i, refs[pos + n_extra:], res, n_row)

    def full_spec(shape):
        nd = len(shape)
        return pl.BlockSpec(tuple(shape), lambda i: (0,) * nd)

    a_spec = (pl.BlockSpec((tm, K), lambda i: (i, 0)) if a_plane is None
              else pl.BlockSpec((None, tm, K), lambda i: (a_plane, i, 0)))
    b_spec = pl.BlockSpec((K, N), lambda i: (b_rows[1], 0)) if mode == 'nn' else pl.BlockSpec((N, K), lambda i: (0, 0))
    ins = [_row_in(r, tm, M) for r in ([add] if add is not None else []) + list(rows)]
    outs = [_row_out(o, tm, M) for o in out_rows]
    res = pl.pallas_call(
        kern, name=name, grid=(M // tm,),
        in_specs=[a_spec, b_spec] + [s for s, _ in ins] + [full_spec(f.shape) for f in fulls],
        out_specs=[s for s, _ in outs] + [full_spec(s) for s in out_accs],
        out_shape=[sh for _, sh in outs] + [jax.ShapeDtypeStruct(tuple(s), F32) for s in out_accs],
        compiler_params=_params(("arbitrary",)),
    )(a, b, *[x for _, x in ins], *fulls)
    return res


def _row_in(r, tm, T):
    arr, w, k = r if isinstance(r, tuple) else (r, r.shape[1], 0)
    return pl.BlockSpec((tm, w), lambda i: (i, k)), arr


def _row_out(o, tm, T):
    c, dt = o
    return pl.BlockSpec((tm, c), lambda i: (i, 0)), jax.ShapeDtypeStruct((T, c), dt)


def _rowmap(name, fn, rows, fulls, out_rows, out_accs=(), tm=512):
    T = (rows[0][0] if isinstance(rows[0], tuple) else rows[0]).shape[0]
    tm = min(tm, T)
    assert T % tm == 0
    n_in, n_row = len(rows) + len(fulls), len(out_rows)

    def kern(*refs):
        i = pl.program_id(0)
        res = fn(*[r[...] for r in refs[:n_in]])
        res = res if isinstance(res, (tuple, list)) else (res,)
        _store_rows_and_sums(i, refs[n_in:], res, n_row)

    def full_spec(shape):
        nd = len(shape)
        return pl.BlockSpec(tuple(shape), lambda i: (0,) * nd)

    ins = [_row_in(r, tm, T) for r in rows]
    outs = [_row_out(o, tm, T) for o in out_rows]
    res = pl.pallas_call(
        kern, name=name, grid=(T // tm,),
        in_specs=[s for s, _ in ins] + [full_spec(f.shape) for f in fulls],
        out_specs=[s for s, _ in outs] + [full_spec(s) for s in out_accs],
        out_shape=[sh for _, sh in outs] + [jax.ShapeDtypeStruct(tuple(s), F32) for s in out_accs],
        compiler_params=_params(("arbitrary",)),
    )(*[x for _, x in ins], *fulls)
    return res


def _colsum(v):
    return jnp.sum(v, axis=0, keepdims=True)


def _rowmean(v):
    return jnp.mean(v, axis=-1, keepdims=True)


def _rms_fwd(x, g):
    r = lax.rsqrt(_rowmean(x * x) + EPS)
    return x * r * g


def _rms_bwd(x, g, dy):
    r = lax.rsqrt(_rowmean(x * x) + EPS)
    xh = x * r
    dxh = dy * g
    return r * (dxh - xh * _rowmean(dxh * xh)), _colsum(dy * xh)


def _sigmoid(v):
    return 0.5 * jnp.tanh(0.5 * v) + 0.5


def _gelu(y):
    c = np.sqrt(2.0 / np.pi).astype(np.float32)
    return 0.5 * y * (1.0 + jnp.tanh(c * (y + 0.044715 * (y * y * y))))


def _gelu_grad(y):
    c = np.sqrt(2.0 / np.pi).astype(np.float32)
    th = jnp.tanh(c * (y + 0.044715 * (y * y * y)))
    return 0.5 * (1.0 + th) + 0.5 * y * (1.0 - th * th) * c * (1.0 + 3.0 * 0.044715 * (y * y))


def _split3(v):
    hi = v.astype(BF16)
    r1 = v - hi.astype(F32)
    mid = r1.astype(BF16)
    lo = (r1 - mid.astype(F32)).astype(BF16)
    return hi, mid, lo


def _band(tm, lo_off, hi_off):
    shape = (tm, tm + 2 * POOL_PAD)
    d = lax.broadcasted_iota(jnp.int32, shape, 1) - lax.broadcasted_iota(jnp.int32, shape, 0) - POOL_PAD
    return ((d >= lo_off) & (d < hi_off)).astype(BF16)


def _window_sums(ext, tm, lo_off, hi_off):
    hi, mid, lo = _split3(ext)
    band = _band(LANES, lo_off, hi_off)
    dot = functools.partial(jnp.dot, preferred_element_type=F32)
    out = []
    for s in range(tm // LANES):
        rows = slice(s * LANES, (s + 2) * LANES)
        out.append(dot(band, hi[rows]) + dot(band, mid[rows]) + dot(band, lo[rows]))
    return jnp.concatenate(out, axis=0)


def _window_count(r0, tm, half, T):
    t = r0 + lax.broadcasted_iota(jnp.int32, (tm, 1), 0)
    return (jnp.minimum(t + half, T) - jnp.maximum(t - half, 0)).astype(F32)


def _halo_specs(tm, C, T):
    per = tm // POOL_PAD
    last = T // POOL_PAD - 1
    return [pl.BlockSpec((POOL_PAD, C), lambda i: (jnp.maximum(i * per - 1, 0), 0)),
            pl.BlockSpec((tm, C), lambda i: (i, 0)),
            pl.BlockSpec((POOL_PAD, C), lambda i: (jnp.minimum((i + 1) * per, last), 0))]


def _with_halo(before_ref, main_ref, after_ref, i, n):
    before = jnp.where(i > 0, before_ref[...], 0.0)
    after = jnp.where(i < n - 1, after_ref[...], 0.0)
    return jnp.concatenate([before, main_ref[...], after], axis=0)


def _pool_block(ext, ctr, w_ref, r0, tm, T):
    pooled, conc = [], []
    for gi, w in enumerate(POOL_WINDOWS):
        half = w // 2
        cols = slice(gi * POOL_GROUP, (gi + 1) * POOL_GROUP)
        ws = _window_sums(ext[:, cols], tm, -half, half)
        p = ws / _window_count(r0, tm, half, T) - ctr[:, cols]
        pooled.append(p)
        conc.append(jnp.dot(p.astype(BF16), w_ref[gi], preferred_element_type=F32))
    return pooled, jnp.concatenate(conc, axis=1)


def _pool_fwd(u, pool_w_bf, pool_scale, g_pool, tm):
    T = u.shape[0]
    nw = len(POOL_WINDOWS)
    C = nw * POOL_GROUP
    n = T // tm

    def kern(ub_ref, u_ref, ua_ref, w_ref, sc_ref, g_ref, o_ref):
        i = pl.program_id(0)
        r0 = pl.multiple_of(i * tm, tm)
        _, conc = _pool_block(_with_halo(ub_ref, u_ref, ua_ref, i, n), u_ref[...], w_ref, r0, tm, T)
        o_ref[...] = _rms_fwd(conc * sc_ref[...], g_ref[...]).astype(o_ref.dtype)

    return pl.pallas_call(
        kern, name="pool_fwd", grid=(n,),
        in_specs=_halo_specs(tm, C, T) + [pl.BlockSpec(pool_w_bf.shape, lambda i: (0, 0, 0)),
                                          pl.BlockSpec((1, C), lambda i: (0, 0)), pl.BlockSpec((1, C), lambda i: (0, 0))],
        out_specs=pl.BlockSpec((tm, C), lambda i: (i, 0)),
        out_shape=jax.ShapeDtypeStruct((T, C), BF16),
        compiler_params=_params(("arbitrary",)),
    )(u, u, u, pool_w_bf, pool_scale, g_pool)


def _pool_bwd_rows(u, d_y, pool_w_bf, pool_scale, g_pool, tm):
    T = u.shape[0]
    nw = len(POOL_WINDOWS)
    C = nw * POOL_GROUP
    n = T // tm

    def kern(ub_ref, u_ref, ua_ref, dy_ref, w_ref, sc_ref, g_ref, q_ref, dp_ref, dw_ref, dsc_ref, dg_ref):
        i = pl.program_id(0)
        r0 = pl.multiple_of(i * tm, tm)
        pooled, conc = _pool_block(_with_halo(ub_ref, u_ref, ua_ref, i, n), u_ref[...], w_ref, r0, tm, T)
        sc = sc_ref[...]
        dyp, dg = _rms_bwd(conc * sc, g_ref[...], dy_ref[...])
        dsc = _colsum(dyp * conc)
        dconc = (dyp * sc).astype(BF16)
        dws = []
        for gi, w in enumerate(POOL_WINDOWS):
            cols = slice(gi * POOL_GROUP, (gi + 1) * POOL_GROUP)
            dc = dconc[:, cols]
            dp = lax.dot_general(dc, w_ref[gi], (((1,), (1,)), ((), ())), preferred_element_type=F32)
            dp_ref[:, cols] = dp
            q_ref[:, cols] = dp / _window_count(r0, tm, w // 2, T)
            dws.append(lax.dot_general(pooled[gi].astype(BF16), dc, (((0,), (0,)), ((), ())), preferred_element_type=F32))

        @pl.when(i == 0)
        def _():
            for gi in range(nw):
                dw_ref[gi] = dws[gi]
            dsc_ref[...] = dsc
            dg_ref[...] = dg

        @pl.when(i != 0)
        def _():
            for gi in range(nw):
                dw_ref[gi] += dws[gi]
            dsc_ref[...] += dsc
            dg_ref[...] += dg

    full2 = pl.BlockSpec((1, C), lambda i: (0, 0))
    row_blk = pl.BlockSpec((tm, C), lambda i: (i, 0))
    return pl.pallas_call(
        kern, name="pool_bwd_rows", grid=(n,),
        in_specs=_halo_specs(tm, C, T) + [row_blk, pl.BlockSpec(pool_w_bf.shape, lambda i: (0, 0, 0)), full2, full2],
        out_specs=[row_blk, row_blk, pl.BlockSpec((nw, POOL_GROUP, POOL_GROUP), lambda i: (0, 0, 0)), full2, full2],
        out_shape=[jax.ShapeDtypeStruct((T, C), F32), jax.ShapeDtypeStruct((T, C), F32),
                   jax.ShapeDtypeStruct((nw, POOL_GROUP, POOL_GROUP), F32),
                   jax.ShapeDtypeStruct((1, C), F32), jax.ShapeDtypeStruct((1, C), F32)],
        compiler_params=_params(("arbitrary",)),
    )(u, u, u, d_y, pool_w_bf, pool_scale, g_pool)


def _pool_bwd_band(q, dpooled, du_other, after, tm):
    T, C = dpooled.shape
    C2 = du_other.shape[1]
    n = T // tm

    def kern(qb_ref, q_ref, qa_ref, dp_ref, other_ref, after_ref, o_ref):
        ext = _with_halo(qb_ref, q_ref, qa_ref, pl.program_id(0), n)
        for gi, w in enumerate(POOL_WINDOWS):
            half = w // 2
            cols = slice(gi * POOL_GROUP, (gi + 1) * POOL_GROUP)
            du = _window_sums(ext[:, cols], tm, -half + 1, half + 1) - dp_ref[:, cols]
            o_ref[:, cols] = du.astype(o_ref.dtype)
        o_ref[:, C:] = other_ref[...].astype(o_ref.dtype)

    return pl.pallas_call(
        kern, name="pool_bwd_band", grid=(n,),
        in_specs=_halo_specs(tm, C, T) + [pl.BlockSpec((tm, C), lambda i: (i, 0)), pl.BlockSpec((tm, C2), lambda i: (i, 0)),
                                          pl.BlockSpec(after.shape, lambda i: (0, 0))],
        out_specs=pl.BlockSpec((tm, C + C2), lambda i: (i, 0)),
        out_shape=jax.ShapeDtypeStruct((T, C + C2), BF16),
        compiler_params=_params(("arbitrary",)),
    )(q, q, q, dpooled, du_other, after)


def _to_chunk_rows(a):
    T, C = a.shape
    return a.reshape(N_CHUNK, T // N_CHUNK, C).transpose(1, 0, 2).reshape(T, C)


def _from_chunk_rows(a):
    T, C = a.shape
    return a.reshape(T // N_CHUNK, N_CHUNK, C).transpose(1, 0, 2).reshape(T, C)


def _cmul(ar, ai, br, bi):
    return ar * br - ai * bi, ar * bi + ai * br


def _ssm_discretise(log_neg_a_re, a_im, log_dt):
    dt = jnp.exp(log_dt)
    a_re = -jnp.exp(log_neg_a_re)
    mag = jnp.exp(a_re * dt)
    ang = a_im * dt
    lam_re, lam_im = mag * jnp.cos(ang), mag * jnp.sin(ang)
    den = a_re * a_re + a_im * a_im
    f_re = ((lam_re - 1.0) * a_re + lam_im * a_im) / den
    f_im = (lam_im * a_re - (lam_re - 1.0) * a_im) / den
    return lam_re, lam_im, f_re, f_im


def _ssm_params_fwd(log_neg_a_re, a_im, log_dt):
    rows, n = log_neg_a_re.shape

    def kern(a_ref, b_ref, c_ref, o1, o2, o3, o4):
        for o, v in zip((o1, o2, o3, o4), _ssm_discretise(a_ref[...], b_ref[...], c_ref[...])):
            o[...] = v

    return pl.pallas_call(kern, name="ssm_params_fwd", out_shape=[jax.ShapeDtypeStruct((rows, n), F32)] * 4)(
        log_neg_a_re, a_im, log_dt)


def _ssm_params_bwd(log_neg_a_re, a_im, log_dt, cots):
    rows, n = log_neg_a_re.shape

    def kern(a_ref, b_ref, c_ref, g1, g2, g3, g4, o1, o2, o3):
        _, vjp = jax.vjp(_ssm_discretise, a_ref[...], b_ref[...], c_ref[...])
        d1, d2, d3 = vjp((g1[...], g2[...], g3[...], g4[...]))
        o1[...] = d1
        o2[...] = d2
        o3[...] = d3

    return pl.pallas_call(
        kern, name="ssm_params_bwd",
        out_shape=[jax.ShapeDtypeStruct((rows, n), F32), jax.ShapeDtypeStruct((rows, n), F32),
                   jax.ShapeDtypeStruct((rows, 1), F32)])(log_neg_a_re, a_im, log_dt, *cots)


def _cpow(lr, li, n):
    out = None
    br, bi = lr, li
    while n:
        if n & 1:
            out = (br, bi) if out is None else _cmul(out[0], out[1], br, bi)
        n >>= 1
        if n:
            br, bi = _cmul(br, bi, br, bi)
    return out


def _slab(t):
    return pl.ds(pl.multiple_of(t * N_CHUNK, N_CHUNK), N_CHUNK)


def _steps(n, step, carry):
    main = n // SCAN_UNROLL

    def body(i, c):
        for r in range(SCAN_UNROLL):
            c = step(i * SCAN_UNROLL + r, c)
        return c

    carry = lax.fori_loop(0, main, body, carry)
    for k in range(main * SCAN_UNROLL, n):
        carry = step(jnp.int32(k), carry)
    return carry


def _shift_chunks(vr, vi, reverse):
    sub = lax.broadcasted_iota(jnp.int32, vr.shape, 0)
    if reverse:
        keep = sub != N_CHUNK - 1
        return jnp.where(keep, pltpu.roll(vr, N_CHUNK - 1, 0), 0.0), jnp.where(keep, pltpu.roll(vi, N_CHUNK - 1, 0), 0.0)
    keep = sub != 0
    return jnp.where(keep, pltpu.roll(vr, 1, 0), 0.0), jnp.where(keep, pltpu.roll(vi, 1, 0), 0.0)


def _chunk_scan(xr, xi, lr, li, Lc, reverse, prev=None):
    C = xr.shape[1]
    lrb, lib = jnp.broadcast_to(lr, (N_CHUNK, C)), jnp.broadcast_to(li, (N_CHUNK, C))
    zero = jnp.zeros((N_CHUNK, C), F32)

    def step_of(k):
        return (Lc - 1 - k) if reverse else k

    def advance(sr, si, t):
        return lrb * sr - lib * si + xr[_slab(t), :], lrb * si + lib * sr + xi[_slab(t), :]

    def pass1(k, c):
        return advance(c[0], c[1], step_of(k))

    er, ei = _steps(Lc, pass1, (zero, zero))
    pr, pi = _cpow(lr, li, Lc)
    prb, pib = jnp.broadcast_to(pr, (N_CHUNK, C)), jnp.broadcast_to(pi, (N_CHUNK, C))
    cr, ci = zero, zero
    for _ in range(N_CHUNK - 1):
        cr, ci = _shift_chunks(er + prb * cr - pib * ci, ei + prb * ci + pib * cr, reverse)

    if prev is None:
        def pass2(k, c):
            t = step_of(k)
            nr, ni = advance(c[0], c[1], t)
            xr[_slab(t), :] = nr
            xi[_slab(t), :] = ni
            return nr, ni

        _steps(Lc, pass2, (cr, ci))
        return None

    qr, qi = prev
    def pair(ar, ai, wr, wi, acc):
        return acc[0] + ar * wr + ai * wi, acc[1] + ai * wr - ar * wi

    def pass2(k, c):
        sr, si, accr, acci = c
        t = step_of(k)
        nr, ni = advance(sr, si, t)
        xr[_slab(t), :] = nr
        xi[_slab(t), :] = ni
        tp = t - 1 if reverse else t + 1
        accr, acci = pair(nr, ni, qr[_slab(tp), :], qi[_slab(tp), :], (accr, acci))
        return nr, ni, accr, acci

    sr, si, accr, acci = _steps(Lc - 1, pass2, (cr, ci, zero, zero))
    t = step_of(Lc - 1)
    nr, ni = advance(sr, si, t)
    xr[_slab(t), :] = nr
    xi[_slab(t), :] = ni
    edge = step_of(0)
    wr, wi = _shift_chunks(qr[_slab(edge), :], qi[_slab(edge), :], not reverse)
    return pair(nr, ni, wr, wi, (accr, acci))


def _ssm_dims(T):
    assert T % ROW_BLOCK == 0
    return T // N_CHUNK, T // ROW_BLOCK


def _group_mask():
    shape = (SSM_BLOCK_GROUPS * SSM_GROUP, SSM_BLOCK_GROUPS * SSM_STATE)
    rows = lax.broadcasted_iota(jnp.int32, shape, 0) >> GROUP_SHIFT
    cols = lax.broadcasted_iota(jnp.int32, shape, 1) >> STATE_SHIFT
    return rows == cols


def _expand_groups(compact):
    return jnp.where(_group_mask(), jnp.tile(compact, (SSM_BLOCK_GROUPS, 1)), 0.0).astype(BF16)


def _collect_groups(block):
    kept = jnp.where(_group_mask(), block, 0.0)
    out = kept[0:SSM_GROUP]
    for g in range(1, SSM_BLOCK_GROUPS):
        out = out + kept[g * SSM_GROUP:(g + 1) * SSM_GROUP]
    return out


def _ssm_specs(T, direction):
    SC = SSM_BLOCK_GROUPS * SSM_STATE
    col = pl.BlockSpec((T, LANES), lambda m: (0, m))
    mat = pl.BlockSpec((None, None, SSM_GROUP, SC), lambda m: (direction, m, 0, 0))
    vec = pl.BlockSpec((None, None, 1, SC), lambda m: (direction, m, 0, 0))
    return col, mat, vec


def _ssm_project_in(u_ref, bre, bim, xr, xi, nblk):
    for blk in range(nblk):
        rows = pl.ds(blk * ROW_BLOCK, ROW_BLOCK)
        ub = u_ref[rows, :].astype(BF16)
        xr[rows, :] = jnp.dot(ub, bre, preferred_element_type=F32)
        xi[rows, :] = jnp.dot(ub, bim, preferred_element_type=F32)


def _ssm_fwd(name, u_bf, mats, vecs, direction):
    T, W = u_bf.shape
    SC = SSM_BLOCK_GROUPS * SSM_STATE
    nb = W // LANES
    Lc, nblk = _ssm_dims(T)
    reverse = bool(direction)

    def kern(u_ref, bre_ref, bim_ref, cre_ref, cim_ref, lr_ref, li_ref, fr_ref, fi_ref, y_ref, xr, xi):
        _ssm_project_in(u_ref, _expand_groups(bre_ref[...]), _expand_groups(bim_ref[...]), xr, xi, nblk)
        _chunk_scan(xr, xi, lr_ref[...], li_ref[...], Lc, reverse)
        fr, fi = fr_ref[...], fi_ref[...]
        cre, cim = _expand_groups(cre_ref[...]), _expand_groups(cim_ref[...])
        for blk in range(nblk):
            rows = pl.ds(blk * ROW_BLOCK, ROW_BLOCK)
            sr, si = _cmul(fr, fi, xr[rows, :], xi[rows, :])
            y_ref[rows, :] = (lax.dot_general(sr.astype(BF16), cre, _NT, preferred_element_type=F32)
                              - lax.dot_general(si.astype(BF16), cim, _NT, preferred_element_type=F32))

    col, mat, vec = _ssm_specs(T, direction)
    return pl.pallas_call(
        kern, name=name, grid=(nb,),
        in_specs=[col, mat, mat, mat, mat, vec, vec, vec, vec],
        out_specs=col, out_shape=jax.ShapeDtypeStruct((T, W), F32),
        scratch_shapes=[pltpu.VMEM((T, SC), F32), pltpu.VMEM((T, SC), F32)],
        compiler_params=_params(("arbitrary",)),
    )(u_bf, *mats, *vecs)


def _ssm_bwd(name, u_bf, dy_bf, du_in, mats, vecs, direction, du_dtype):
    T, W = u_bf.shape
    SC = SSM_BLOCK_GROUPS * SSM_STATE
    nb = W // LANES
    Lc, nblk = _ssm_dims(T)
    reverse = bool(direction)

    def kern(u_ref, dy_ref, dui_ref, bre_ref, bim_ref, cre_ref, cim_ref, lr_ref, li_ref, fr_ref, fi_ref,
             du_ref, dbre_ref, dbim_ref, dcre_ref, dcim_ref, dlr_ref, dli_ref, dfr_ref, dfi_ref,
             xr, xi, gr, gi):
        lr, li, fr, fi = lr_ref[...], li_ref[...], fr_ref[...], fi_ref[...]
        bre, bim = _expand_groups(bre_ref[...]), _expand_groups(bim_ref[...])
        cre, cim = _expand_groups(cre_ref[...]), _expand_groups(cim_ref[...])
        _ssm_project_in(u_ref, bre, bim, xr, xi, nblk)
        _chunk_scan(xr, xi, lr, li, Lc, reverse)
        mre = jnp.zeros((LANES, SC), F32)
        mim = jnp.zeros((LANES, SC), F32)
        dfr = jnp.zeros((1, SC), F32)
        dfi = jnp.zeros((1, SC), F32)
        for blk in range(nblk):
            rows = pl.ds(blk * ROW_BLOCK, ROW_BLOCK)
            dy = dy_ref[rows, :].astype(BF16)
            x_r, x_i = xr[rows, :], xi[rows, :]
            g_r = jnp.dot(dy, cre, preferred_element_type=F32)
            g_i = -jnp.dot(dy, cim, preferred_element_type=F32)
            mre += lax.dot_general(dy, x_r.astype(BF16), _TN, preferred_element_type=F32)
            mim += lax.dot_general(dy, x_i.astype(BF16), _TN, preferred_element_type=F32)
            dfr += _colsum(g_r * x_r + g_i * x_i)
            dfi += _colsum(g_i * x_r - g_r * x_i)
            gr[rows, :] = fr * g_r + fi * g_i
            gi[rows, :] = fr * g_i - fi * g_r
        dcre_ref[...] = _collect_groups(fr * mre - fi * mim)
        dcim_ref[...] = _collect_groups(-(fr * mim + fi * mre))
        dfr_ref[...] = dfr
        dfi_ref[...] = dfi
        accr, acci = _chunk_scan(gr, gi, lr, -li, Lc, not reverse, prev=(xr, xi))
        dlr_ref[...] = _colsum(accr)
        dli_ref[...] = _colsum(acci)
        dbre = jnp.zeros((LANES, SC), F32)
        dbim = jnp.zeros((LANES, SC), F32)
        for blk in range(nblk):
            rows = pl.ds(blk * ROW_BLOCK, ROW_BLOCK)
            ub = u_ref[rows, :].astype(BF16)
            a_r, a_i = gr[rows, :].astype(BF16), gi[rows, :].astype(BF16)
            dbre += lax.dot_general(ub, a_r, _TN, preferred_element_type=F32)
            dbim += lax.dot_general(ub, a_i, _TN, preferred_element_type=F32)
            du_ref[rows, :] = (dui_ref[rows, :] + lax.dot_general(a_r, bre, _NT, preferred_element_type=F32)
                               + lax.dot_general(a_i, bim, _NT, preferred_element_type=F32)).astype(du_ref.dtype)
        dbre_ref[...] = _collect_groups(dbre)
        dbim_ref[...] = _collect_groups(dbim)

    col, mat, vec = _ssm_specs(T, direction)
    mat_out = pl.BlockSpec((None, SSM_GROUP, SC), lambda m: (m, 0, 0))
    vec_out = pl.BlockSpec((None, 1, SC), lambda m: (m, 0, 0))
    mat_shape = jax.ShapeDtypeStruct((nb, SSM_GROUP, SC), F32)
    vec_shape = jax.ShapeDtypeStruct((nb, 1, SC), F32)
    return pl.pallas_call(
        kern, name=name, grid=(nb,),
        in_specs=[col, col, col, mat, mat, mat, mat, vec, vec, vec, vec],
        out_specs=[col, mat_out, mat_out, mat_out, mat_out, vec_out, vec_out, vec_out, vec_out],
        out_shape=[jax.ShapeDtypeStruct((T, W), du_dtype), mat_shape, mat_shape, mat_shape, mat_shape,
                   vec_shape, vec_shape, vec_shape, vec_shape],
        scratch_shapes=[pltpu.VMEM((T, SC), F32)] * 4,
        compiler_params=_params(("arbitrary",)),
    )(u_bf, dy_bf, du_in, *mats, *vecs)


def _groups_side_by_side(p, channel_axis):
    d, G = p.shape[:2]
    nb = G // SSM_BLOCK_GROUPS
    v = p.reshape(d, nb, SSM_BLOCK_GROUPS, p.shape[2], p.shape[3])
    v = v.transpose(0, 1, 4, 2, 3) if channel_axis == 3 else v.transpose(0, 1, 3, 2, 4)
    return v.reshape(d, nb, SSM_GROUP, SSM_BLOCK_GROUPS * SSM_STATE)


def _groups_apart(m, channel_axis):
    d, nb = m.shape[:2]
    v = m.reshape(d, nb, SSM_GROUP, SSM_BLOCK_GROUPS, SSM_STATE)
    v = v.transpose(0, 1, 3, 4, 2) if channel_axis == 3 else v.transpose(0, 1, 3, 2, 4)
    return v.reshape((d, nb * SSM_BLOCK_GROUPS) + v.shape[3:])


def _conv_taps(ref, c, R, T):
    n = T // R
    r0 = pl.multiple_of(c * R, R)
    main = ref[pl.ds(r0, R), :]
    before = ref[pl.ds(pl.multiple_of(jnp.maximum(r0 - 8, 0), 8), 8), :]
    after = ref[pl.ds(pl.multiple_of(jnp.minimum(r0 + R, T - 8), 8), 8), :]
    ext = jnp.concatenate([jnp.where(c > 0, before, 0.0), main, jnp.where(c < n - 1, after, 0.0)], axis=0)
    m1 = pltpu.roll(ext, 1, 0)[8:8 + R]
    p1 = pltpu.roll(ext, R + 15, 0)[8:8 + R]
    return m1, main, p1, r0


def _conv_specs(T, F, cb):
    nj = F // cb
    lo = lambda rows: pl.BlockSpec((rows, cb), lambda j: (0, j))
    hi = lambda rows: pl.BlockSpec((rows, cb), lambda j: (0, j + nj))
    return nj, lo, hi


def _convact_fwd(up, conv_w, conv_b, F):
    T = up.shape[0]
    cb = _pick(F, (256, 128))
    R = _pick(T, (256, 128, 64))
    nj, lo, hi = _conv_specs(T, F, cb)

    def kern(uv, ug, wv, wg, bv, bg, o_ref):
        def body(c, carry):
            v1, v0, v2, r0 = _conv_taps(uv, c, R, T)
            g1, g0, g2, _ = _conv_taps(ug, c, R, T)
            val = v1 * wv[0:1, :] + v0 * wv[1:2, :] + v2 * wv[2:3, :] + bv[...]
            gate = g1 * wg[0:1, :] + g0 * wg[1:2, :] + g2 * wg[2:3, :] + bg[...]
            o_ref[pl.ds(r0, R), :] = (val * (gate * _sigmoid(gate))).astype(o_ref.dtype)
            return carry

        lax.fori_loop(0, T // R, body, 0)

    return pl.pallas_call(
        kern, name="convact_fwd", grid=(nj,),
        in_specs=[lo(T), hi(T), lo(3), hi(3), lo(1), hi(1)],
        out_specs=lo(T), out_shape=jax.ShapeDtypeStruct((T, F), BF16),
        compiler_params=_params(("arbitrary",)),
    )(up, up, conv_w, conv_w, conv_b, conv_b)


def _convact_bwd(up, dact, conv_w, conv_b, F):
    T = up.shape[0]
    cb = _pick(F, (256, 128))
    R = _pick(T, (256, 128, 64))
    nj, lo, hi = _conv_specs(T, F, cb)

    def kern(uv, ug, da, wv, wg, bv, bg, dup, dwv, dwg, dbv, dbg, sv, sg):
        zero = jnp.zeros((1, cb), F32)

        def pass_a(c, acc):
            v1, v0, v2, r0 = _conv_taps(uv, c, R, T)
            g1, g0, g2, _ = _conv_taps(ug, c, R, T)
            val = v1 * wv[0:1, :] + v0 * wv[1:2, :] + v2 * wv[2:3, :] + bv[...]
            gate = g1 * wg[0:1, :] + g0 * wg[1:2, :] + g2 * wg[2:3, :] + bg[...]
            sig = _sigmoid(gate)
            d = da[pl.ds(r0, R), :]
            dval = d * (gate * sig)
            dgate = d * val * (sig * (1.0 + gate * (1.0 - sig)))
            sv[pl.ds(r0, R), :] = dval
            sg[pl.ds(r0, R), :] = dgate
            terms = (dval * v1, dval * v0, dval * v2, dval, dgate * g1, dgate * g0, dgate * g2, dgate)
            return tuple(a + _colsum(t) for a, t in zip(acc, terms))

        acc = lax.fori_loop(0, T // R, pass_a, (zero,) * 8)
        for k in range(3):
            dwv[k:k + 1, :] = acc[k]
            dwg[k:k + 1, :] = acc[4 + k]
        dbv[...] = acc[3]
        dbg[...] = acc[7]

        def pass_b(c, carry):
            v1, v0, v2, r0 = _conv_taps(sv, c, R, T)
            g1, g0, g2, _ = _conv_taps(sg, c, R, T)
            dup[0, pl.ds(r0, R), :] = (v2 * wv[0:1, :] + v0 * wv[1:2, :] + v1 * wv[2:3, :]).astype(dup.dtype)
            dup[1, pl.ds(r0, R), :] = (g2 * wg[0:1, :] + g0 * wg[1:2, :] + g1 * wg[2:3, :]).astype(dup.dtype)
            return carry

        lax.fori_loop(0, T // R, pass_b, 0)

    dup, dwv, dwg, dbv, dbg = pl.pallas_call(
        kern, name="convact_bwd", grid=(nj,),
        in_specs=[lo(T), hi(T), lo(T), lo(3), hi(3), lo(1), hi(1)],
        out_specs=[pl.BlockSpec((2, T, cb), lambda j: (0, 0, j)), lo(3), lo(3), lo(1), lo(1)],
        out_shape=[jax.ShapeDtypeStruct((2, T, F), BF16),
                   jax.ShapeDtypeStruct((3, F), F32), jax.ShapeDtypeStruct((3, F), F32),
                   jax.ShapeDtypeStruct((1, F), F32), jax.ShapeDtypeStruct((1, F), F32)],
        scratch_shapes=[pltpu.VMEM((T, cb), F32), pltpu.VMEM((T, cb), F32)],
        compiler_params=_params(("arbitrary",)),
    )(up, up, dact, conv_w, conv_w, conv_b, conv_b)
    return dup, (dwv, dwg), jnp.concatenate([dbv, dbg], axis=1)


def _peers(x, y, c, with_self=False):
    out = []
    for k in range(0 if with_self else 1, N_DEV):
        px = 1 - x if k & 4 else x
        py = 1 - y if k & 2 else y
        pc = 1 - c if k & 1 else c
        out.append((k, (px, py, pc), 4 * px + 2 * py + pc))
    return out


def _exchange_start(name, bufs, modes, after):
    handles, token = _exchange_start_groups(name, [(bufs, modes)], after)
    return handles[0], token


def _exchange_start_groups(name, groups, after):
    bufs = [b for g, _ in groups for b in g]
    modes = [m for _, ms in groups for m in ms]
    first = [int(v) for v in np.cumsum([0] + [len(g) for g, _ in groups])]
    n, ng = len(bufs), len(groups)
    lands = [lax.empty((N_DEV,) + tuple(b.shape[-2:]), b.dtype) for b in bufs]

    def body(*refs):
        ins, land_in = refs[:n], refs[n:2 * n]
        sems = refs[2 * n + 1:2 * n + 1 + 2 * ng]
        token = refs[-1]
        x, y, c = lax.axis_index("x"), lax.axis_index("y"), lax.axis_index("c")
        me = 4 * x + 2 * y + c
        for g in range(ng):
            for b in range(first[g], first[g + 1]):
                for k, peer, slot in _peers(x, y, c, with_self=True):
                    sem = (b - first[g]) * N_DEV + k
                    pltpu.make_async_remote_copy(
                        src_ref=ins[b] if modes[b] == 'gather' else ins[b].at[slot], dst_ref=land_in[b].at[me],
                        send_sem=sems[2 * g].at[sem], recv_sem=sems[2 * g + 1].at[sem],
                        device_id=peer, device_id_type=MESH_ID).start()
        token[...] = jnp.zeros_like(token)

    hbm = pl.BlockSpec(memory_space=pltpu.HBM)
    sem_spec = pl.BlockSpec(memory_space=pltpu.SEMAPHORE)
    operands = [pltpu.with_memory_space_constraint(a, pltpu.HBM) for a in bufs + lands]
    sem_shapes = [pltpu.SemaphoreType.DMA((len(g) * N_DEV,)) for g, _ in groups for _ in range(2)]
    res = pl.pallas_call(
        body, name=name,
        out_shape=(*sem_shapes, *[pltpu.HBM(a.shape, a.dtype) for a in operands], jax.ShapeDtypeStruct((8, LANES), F32)),
        in_specs=[hbm] * (2 * n) + [pl.BlockSpec(memory_space=pl.ANY)],
        out_specs=(*([sem_spec] * (2 * ng)), *([hbm] * (2 * n)), pl.BlockSpec(memory_space=pltpu.VMEM)),
        input_output_aliases={i: 2 * ng + i for i in range(2 * n)},
        compiler_params=pltpu.CompilerParams(has_side_effects=pltpu.SideEffectType.DATAFLOW_SIDE_EFFECTING),
    )(*operands, after)
    srcs, landed = res[2 * ng:2 * ng + n], res[2 * ng + n:2 * ng + 2 * n]
    handles = [(res[2 * g], res[2 * g + 1], list(srcs[first[g]:first[g + 1]]), list(landed[first[g]:first[g + 1]]),
                tuple(modes[first[g]:first[g + 1]])) for g in range(ng)]
    return handles, res[-1]


def _exchange_wait(name, handle, after):
    send_sems, recv_sems, srcs, lands, modes = handle
    n = len(srcs)

    def body(*refs):
        src_in, land_in = refs[:n], refs[n:2 * n]
        send_ref, recv_ref = refs[2 * n], refs[2 * n + 1]
        x, y, c = lax.axis_index("x"), lax.axis_index("y"), lax.axis_index("c")
        for b in range(n):
            for k, peer, slot in _peers(x, y, c, with_self=True):
                sem = b * N_DEV + k
                copy = pltpu.make_async_remote_copy(
                    src_ref=src_in[b] if modes[b] == 'gather' else src_in[b].at[slot], dst_ref=land_in[b].at[slot],
                    send_sem=send_ref.at[sem], recv_sem=recv_ref.at[sem],
                    device_id=peer, device_id_type=MESH_ID)
                copy.wait_send()
                copy.wait_recv()

    hbm = pl.BlockSpec(memory_space=pltpu.HBM)
    sem_spec = pl.BlockSpec(memory_space=pltpu.SEMAPHORE)
    res = pl.pallas_call(
        body, name=name,
        out_shape=tuple(pltpu.HBM(a.shape, a.dtype) for a in srcs + lands),
        in_specs=[hbm] * (2 * n) + [sem_spec, sem_spec, pl.BlockSpec(memory_space=pl.ANY)],
        out_specs=tuple([hbm] * (2 * n)),
        input_output_aliases={i: i for i in range(2 * n)},
        compiler_params=pltpu.CompilerParams(has_side_effects=pltpu.SideEffectType.DATAFLOW_SIDE_EFFECTING),
    )(*srcs, *lands, send_sems, recv_sems, after)
    return list(res[n:])


def _adamw(w, g, m, v):
    m2 = ADAM_B1 * m + (1.0 - ADAM_B1) * g
    v2 = ADAM_B2 * v + (1.0 - ADAM_B2) * (g * g)
    m_hat = m2 / (1.0 - ADAM_B1 ** ADAM_STEP)
    v_hat = v2 / (1.0 - ADAM_B2 ** ADAM_STEP)
    return -ADAM_LR * (m_hat / (jnp.sqrt(v_hat) + ADAM_EPS) + ADAM_WD * w), m2, v2


def _sum_adamw(name, landed, w, m, v):
    R, C = w.shape
    row_bytes = N_DEV * (-(-C // LANES) * LANES) * landed.dtype.itemsize
    tiles = [d for d in range(16, R, 16) if R % d == 0 and d * row_bytes <= ADAM_TILE_BYTES]
    tr = max(tiles) if tiles and R * row_bytes > ADAM_TILE_BYTES else R

    def kern(x_ref, w_ref, m_ref, v_ref, g_out, d_out, m_out, v_out):
        g = x_ref[0].astype(F32)
        for q in range(1, N_DEV):
            g = g + x_ref[q].astype(F32)
        g_out[...] = g
        d_out[...], m_out[...], v_out[...] = _adamw(w_ref[...], g, m_ref[...], v_ref[...])

    blk = pl.BlockSpec((tr, C), lambda i: (i, 0))
    return pl.pallas_call(
        kern, name=name, grid=(R // tr,),
        in_specs=[pl.BlockSpec((N_DEV, tr, C), lambda i: (0, i, 0)), blk, blk, blk],
        out_specs=[blk] * 4, out_shape=[jax.ShapeDtypeStruct((R, C), F32)] * 4,
        compiler_params=_params(("arbitrary",)),
    )(landed, w, m, v)


def _sum_adamw_many(name, items):
    n = len(items)

    def kern(*refs):
        ins, outs = refs[:4 * n], refs[4 * n:]
        for k in range(n):
            x_ref, w_ref, m_ref, v_ref = ins[4 * k:4 * k + 4]
            g = x_ref[0].astype(F32)
            for q in range(1, N_DEV):
                g = g + x_ref[q].astype(F32)
            outs[4 * k][...] = g
            outs[4 * k + 1][...], outs[4 * k + 2][...], outs[4 * k + 3][...] = _adamw(w_ref[...], g, m_ref[...], v_ref[...])

    out_shape = [jax.ShapeDtypeStruct(w.shape, F32) for _, w, _, _ in items for _ in range(4)]
    res = pl.pallas_call(kern, name=name, out_shape=out_shape)(*[a for item in items for a in item])
    return [res[4 * k:4 * k + 4] for k in range(n)]


def _view2d(a):
    if a.ndim == 1:
        return a.reshape(1, -1)
    return a.reshape(-1, a.shape[-1])


def kernel(x, norm_mix_g, w_in, pool_w, pool_scale, ssm_log_neg_a_re, ssm_a_im, ssm_log_dt, ssm_b_re, ssm_b_im, ssm_c_re, ssm_c_im, ssm_d, glu_w, glu_b, out_norm_pool_g, out_norm_ssm_g, w_out, norm_ffn_g, w_up, conv_w, conv_b, w_down, final_norm_g, loss_target, m_norm_mix_g, m_w_in, m_pool_w, m_pool_scale, m_ssm_log_neg_a_re, m_ssm_a_im, m_ssm_log_dt, m_ssm_b_re, m_ssm_b_im, m_ssm_c_re, m_ssm_c_im, m_ssm_d, m_glu_w, m_glu_b, m_out_norm_pool_g, m_out_norm_ssm_g, m_w_out, m_norm_ffn_g, m_w_up, m_conv_w, m_conv_b, m_w_down, m_final_norm_g, v_norm_mix_g, v_w_in, v_pool_w, v_pool_scale, v_ssm_log_neg_a_re, v_ssm_a_im, v_ssm_log_dt, v_ssm_b_re, v_ssm_b_im, v_ssm_c_re, v_ssm_c_im, v_ssm_d, v_glu_w, v_glu_b, v_out_norm_pool_g, v_out_norm_ssm_g, v_w_out, v_norm_ffn_g, v_w_up, v_conv_w, v_conv_b, v_w_down, v_final_norm_g):
    given = dict(locals())
    weights = {n: given[n] for n in WEIGHTS}
    mom1 = {n: given["m_" + n] for n in WEIGHTS}
    mom2 = {n: given["v_" + n] for n in WEIGHTS}

    xs = x[0]
    tgt = loss_target[0]
    T, D = xs.shape
    DP = len(POOL_WINDOWS) * POOL_GROUP
    DS = D - DP
    G = DS // SSM_GROUP
    N, H = SSM_STATE, SSM_GROUP
    F2 = w_up.shape[1] * N_DEV
    F = F2 // 2
    fs = w_up.shape[1]
    row = lambda a: a.reshape(1, -1)

    gather_groups = [([w_in.astype(BF16)], ['gather']), ([glu_w.astype(BF16), w_out.astype(BF16)], ['gather'] * 2),
                     ([w_up.T.astype(BF16), conv_w], ['gather'] * 2), ([w_down.astype(BF16)], ['gather'])]
    (in_handle, mix_handle, up_handle, down_handle), token = _exchange_start_groups("gather_start", gather_groups, w_in)
    CB = row(conv_b)

    g1, g2, g3 = row(norm_mix_g), row(norm_ffn_g), row(final_norm_g)
    gp, gs = row(out_norm_pool_g), row(out_norm_ssm_g)
    (xn,) = _rowmap("norm_mix", _rms_fwd, [xs], [g1 + token[0:1, 0:1]], [(D, BF16)])
    W_in = _exchange_wait("gather_in_wait", in_handle, xn)[0].reshape(D, D)
    halves = lambda r: (r[:, :DP], r[:, DP:])
    u, u_s5 = _mm_rows("proj_in", xn, W_in, 'nn', halves, [], [], [(DP, F32), (DS, F32)])
    tmp = _pick(T, (256, 128))
    pool_w_bf = pool_w.astype(BF16)
    ypn = _pool_fwd(u, pool_w_bf, row(pool_scale), gp, tmp)

    u_ssm = _to_chunk_rows(u_s5)
    u_ssm_bf = u_ssm.astype(BF16)
    a_rows = (ssm_log_neg_a_re.reshape(2 * G, N), ssm_a_im.reshape(2 * G, N), ssm_log_dt.reshape(2 * G, 1))
    disc = _ssm_params_fwd(*a_rows)
    nb = G // SSM_BLOCK_GROUPS
    vecs = [d.reshape(2, nb, 1, SSM_BLOCK_GROUPS * N) for d in disc]
    mats = [_groups_side_by_side(ssm_b_re, 3), _groups_side_by_side(ssm_b_im, 3),
            _groups_side_by_side(ssm_c_re, 2), _groups_side_by_side(ssm_c_im, 2)]
    y_dir = [_ssm_fwd("ssm_fwd_%d" % d, u_ssm_bf, mats, vecs, d) for d in range(2)]

    def mix_post(yf, yb, us, d, gw, gb, g):
        y = yf + yb + d * us
        z = _gelu(y)
        gate = _sigmoid(jnp.dot(z.astype(BF16), gw, preferred_element_type=F32) + gb)
        return _rms_fwd(z * gate, g), y

    mix_landed = _exchange_wait("gather_mix_wait", mix_handle, y_dir[1])
    W_glu = mix_landed[0].reshape(DS, DS)
    W_out = mix_landed[1].reshape(D, D)
    ysn, y_ssm = _rowmap("ssm_post", mix_post, [y_dir[0], y_dir[1], u_ssm], [row(ssm_d), W_glu, row(glu_b), gs],
                         [(DS, BF16), (DS, F32)])
    ycat = jnp.concatenate([ypn, _from_chunk_rows(ysn)], axis=1)
    h1, hn = _mm_rows("proj_out", ycat, W_out, 'nn', lambda h, g: (h, _rms_fwd(h, g)), [], [g2],
                      [(D, F32), (D, BF16)], add=xs)
    up_landed = _exchange_wait("gather_up_wait", up_handle, hn)
    W_up_t = up_landed[0].reshape(F2, D)
    CW = up_landed[1].transpose(1, 0, 2).reshape(3, F2)
    up = _mm("ffn_up", hn, W_up_t, 'nt', F32)
    act = _convact_fwd(up, CW, CB, F)
    W_down = _exchange_wait("gather_down_wait", down_handle, act)[0].reshape(F, D)

    def head(h, t, g):
        r = lax.rsqrt(_rowmean(h * h) + EPS)
        hh = h * r
        e = hh * g - t
        loss = 0.5 * jnp.sum(_rowmean(e * e), keepdims=True)
        dy = e * (1.0 / D)
        dxh = dy * g
        dh = r * (dxh - hh * _rowmean(dxh * hh))
        return dh, dh, jnp.broadcast_to(loss, (1, LANES)), _colsum(dy * hh)

    dh2, dh2_bf, loss_acc, dg3 = _mm_rows("ffn_down", act, W_down, 'nn', head, [tgt], [g3],
                                          [(D, F32), (D, BF16)], [(1, LANES), (1, D)], add=h1)

    dact = _mm("ffn_down_dx", dh2_bf, W_down, 'nt', F32)
    dW_down = _mm("ffn_down_dw", act, dh2_bf, 'tn', BF16)
    dup, dCW, dCB = _convact_bwd(up, dact, CW, CB, F)
    dhn_val = _mm("ffn_up_dx_val", dup, W_up_t, 'nn', F32, b_rows=(2, 0), a_plane=0)
    dW_up_t = _mm("ffn_up_dw", dup, hn, 'tn', BF16).reshape(F2, D)

    def shards(full_grad):
        return full_grad.reshape((N_DEV, full_grad.shape[0] // N_DEV) + full_grad.shape[1:])

    half = N_DEV // 2
    dCW_sh = jnp.concatenate([h.reshape(3, half, fs).transpose(1, 0, 2) for h in dCW], axis=0)
    ffn_sent = [shards(dW_down), shards(dW_up_t), dCW_sh]
    ffn_handle, token = _exchange_start("scatter_ffn_start", ffn_sent, ['scatter'] * 3, after=dhn_val)
    g2_late = g2 + token[0:1, 0:1]

    def norm_bwd_add(dy, hx, res, g):
        dx, dg = _rms_bwd(hx, g, dy)
        return dx + res, dg

    def norm_bwd_add2(dy, hx, res, g):
        dx, dg = _rms_bwd(hx, g, dy)
        return dx + res, dx + res, dg

    dh1, dh1_bf, dg2 = _mm_rows("ffn_up_dx_gate", dup, W_up_t, 'nn', norm_bwd_add2, [h1, dh2], [g2_late],
                                [(D, F32), (D, BF16)], [(1, D)], add=dhn_val, b_rows=(2, 1), a_plane=1)
    dycat, d_ysn = _mm_rows("proj_out_dx", dh1_bf, W_out, 'nt', halves, [], [], [(DP, F32), (DS, F32)])
    dW_out = _mm("proj_out_dw", ycat, dh1_bf, 'tn', BF16)

    out_sent = [shards(dW_out)]
    out_handle, token = _exchange_start("scatter_out_start", out_sent, ['scatter'], after=dycat)
    q, dpooled, dpool_w, dpool_scale, dgp = _pool_bwd_rows(u, dycat, pool_w_bf, row(pool_scale),
                                                           gp + token[0:1, 0:1], tmp)

    def mix_post_bwd(y, us, dyn, d, gw, gb, g):
        z = _gelu(y)
        gz = z.astype(BF16)
        sig = _sigmoid(jnp.dot(gz, gw, preferred_element_type=F32) + gb)
        dys, dg = _rms_bwd(z * sig, g, dyn)
        dgl = dys * z * sig * (1.0 - sig)
        dgl_bf = dgl.astype(BF16)
        dz = dys * sig + lax.dot_general(dgl_bf, gw, (((1,), (1,)), ((), ())), preferred_element_type=F32)
        dgw = lax.dot_general(gz, dgl_bf, (((0,), (0,)), ((), ())), preferred_element_type=F32)
        dyv = dz * _gelu_grad(y)
        return dyv, dyv * d, dgw, _colsum(dgl), _colsum(dyv * us), dg

    dyv, du_dir, dglu_w, dglu_b, dssm_d, dgs = _rowmap(
        "ssm_post_bwd", mix_post_bwd, [y_ssm, u_ssm, _to_chunk_rows(d_ysn)], [row(ssm_d), W_glu, row(glu_b), gs],
        [(DS, BF16), (DS, F32)], [(DS, DS), (1, DS), (1, DS), (1, DS)])

    du_ssm = du_dir
    dB, dC, dvec = [], [], []
    for d in range(2):
        res = _ssm_bwd("ssm_bwd_%d" % d, u_ssm_bf, dyv, du_ssm, mats, vecs, d, BF16 if d == 1 else F32)
        du_ssm = res[0]
        dB.append(res[1:3])
        dC.append(res[3:5])
        dvec.append(res[5:9])
    grad_b = [_groups_apart(jnp.stack([dB[0][k], dB[1][k]]), 3) for k in range(2)]
    grad_c = [_groups_apart(jnp.stack([dC[0][k], dC[1][k]]), 2) for k in range(2)]
    cots = [jnp.stack([dvec[0][k], dvec[1][k]]).reshape(2 * G, N) for k in range(4)]
    d_a_re, d_a_im, d_log_dt = _ssm_params_bwd(*a_rows, cots)

    rep_grads = {
        'pool_w': dpool_w, 'pool_scale': dpool_scale, 'ssm_log_neg_a_re': d_a_re, 'ssm_a_im': d_a_im,
        'ssm_log_dt': d_log_dt, 'ssm_b_re': grad_b[0], 'ssm_b_im': grad_b[1], 'ssm_c_re': grad_c[0], 'ssm_c_im': grad_c[1],
        'ssm_d': dssm_d, 'glu_b': dglu_b, 'out_norm_pool_g': dgp, 'out_norm_ssm_g': dgs, 'norm_ffn_g': dg2,
        'conv_b': dCB, 'final_norm_g': dg3}
    wide = ('pool_w', 'ssm_b_re', 'ssm_b_im', 'ssm_c_re', 'ssm_c_im')

    def narrow(n):
        return weights[n].ndim == 4 and weights[n].shape[-1] < LANES // 2

    def travel(n):
        g = rep_grads[n]
        if narrow(n):
            g = jnp.swapaxes(g, 2, 3)
        if n in wide:
            return g.reshape(-1, PACK_W if g.size % PACK_W == 0 else LANES).astype(BF16)
        return _view2d(g.reshape(weights[n].shape))

    small = tuple(n for n in REPLICATED if n != 'norm_mix_g')
    small_sent = [shards(dglu_w.astype(BF16))] + [travel(n) for n in small] + [loss_acc]
    small_handle, token = _exchange_start("reduce_small_start", small_sent, ['scatter'] + ['gather'] * (len(small) + 1),
                                          after=d_a_re)

    du = _pool_bwd_band(q, dpooled, _from_chunk_rows(du_ssm), token, tmp)
    dW_in = _mm("proj_in_dw", xn, du, 'tn', BF16)
    in_handle, token = _exchange_start("scatter_in_start", [shards(dW_in)], ['scatter'], after=dW_in)
    dx, dg1 = _mm_rows("proj_in_dx", du, W_in + token[0, 0].astype(BF16), 'nt', norm_bwd_add, [xs, dh1], [g1],
                       [(D, F32)], [(1, D)])
    rep_grads['norm_mix_g'] = dg1
    last_handle, token = _exchange_start("reduce_last_start", [travel('norm_mix_g')], ['gather'], after=dx)

    grads, delta, new_m, new_v = {}, {}, {}, {}

    def update(n, landed_n):
        shape = weights[n].shape
        if narrow(n):
            swapped = lambda a: _view2d(jnp.swapaxes(a, 2, 3))
            landed_n = landed_n.reshape((N_DEV,) + swapped(weights[n]).shape)
            res = _sum_adamw("adamw_" + n, landed_n, swapped(weights[n]), swapped(mom1[n]), swapped(mom2[n]))
            back = lambda r: jnp.swapaxes(r.reshape(shape[:2] + (shape[3], shape[2])), 2, 3)
            grads[n], delta[n], new_m[n], new_v[n] = [back(r) for r in res]
            return
        if n == 'w_up':
            res = _sum_adamw("adamw_" + n, landed_n, weights[n].T, mom1[n].T, mom2[n].T)
            grads[n], delta[n], new_m[n], new_v[n] = [r.T for r in res]
            return
        landed_n = landed_n.reshape((N_DEV,) + _view2d(weights[n]).shape)
        res = _sum_adamw("adamw_" + n, landed_n, _view2d(weights[n]), _view2d(mom1[n]), _view2d(mom2[n]))
        grads[n], delta[n], new_m[n], new_v[n] = [r.reshape(shape) for r in res]

    def update_many(name, named_landed):
        items = [(l.reshape((N_DEV,) + _view2d(weights[n]).shape), _view2d(weights[n]), _view2d(mom1[n]), _view2d(mom2[n]))
                 for n, l in named_landed]
        for (n, _), res in zip(named_landed, _sum_adamw_many(name, items)):
            grads[n], delta[n], new_m[n], new_v[n] = [r.reshape(weights[n].shape) for r in res]

    ffn_landed = _exchange_wait("scatter_ffn_wait", ffn_handle, token)
    for n, l in zip(('w_down', 'w_up'), ffn_landed):
        update(n, l)
    (out_landed,) = _exchange_wait("scatter_out_wait", out_handle, token)
    update('w_out', out_landed)
    small_landed = _exchange_wait("reduce_small_wait", small_handle, token)
    tiny = [('conv_w', ffn_landed[2])]
    for n, l in zip(('glu_w',) + small, small_landed):
        if weights[n].size <= ADAM_TINY:
            tiny.append((n, l))
        else:
            update(n, l)
    update_many("adamw_tiny", tiny)

    def loss_sum(parts_ref, o_ref):
        s = parts_ref[0]
        for q in range(1, N_DEV):
            s = s + parts_ref[q]
        o_ref[...] = s

    loss = pl.pallas_call(loss_sum, name="loss_sum", out_shape=jax.ShapeDtypeStruct((1, LANES), F32))(small_landed[-1])[0, 0]

    update('w_in', _exchange_wait("scatter_in_wait", in_handle, grads[small[-1]])[0])
    update('norm_mix_g', _exchange_wait("reduce_last_wait", last_handle, grads['w_in'])[0])

    return (loss, dx[None], *[grads[n] for n in WEIGHTS], *[delta[n] for n in WEIGHTS],
            *[new_m[n] for n in WEIGHTS], *[new_v[n] for n in WEIGHTS])
```

```python
import functools

import jax
import jax.numpy as jnp
import numpy as np
from jax import lax
from jax.experimental import pallas as pl
from jax.experimental.pallas import tpu as pltpu

F32 = jnp.float32
BF16 = jnp.bfloat16
MESH_ID = pl.DeviceIdType.MESH

N_DEV = 8
EPS = 1e-6
POOL_WINDOWS = (2, 4, 8, 16)
POOL_GROUP = 128
POOL_PAD = 64
SSM_GROUP = 16
SSM_STATE = 64
SSM_BLOCK_GROUPS = 8
N_CHUNK = 8
ROW_BLOCK = 512
SCAN_UNROLL = 8
GROUP_SHIFT = 4
STATE_SHIFT = 6
_NT = (((1,), (1,)), ((), ()))
_TN = (((0,), (0,)), ((), ()))
LANES = 128
PACK_W = 1024
VMEM_LIMIT = 56 * 1024 * 1024
MM_VMEM_BUDGET = 40 * 1024 * 1024
MM_TILE_CAP = 1408
MM_ROWS_TILE = 512
ADAM_TILE_BYTES = 4 * 1024 * 1024
ADAM_TINY = 8192

ADAM_LR = 0.001
ADAM_B1 = 0.9
ADAM_B2 = 0.999
ADAM_EPS = 1e-08
ADAM_WD = 0.01
ADAM_STEP = 10

WEIGHTS = ['norm_mix_g', 'w_in', 'pool_w', 'pool_scale', 'ssm_log_neg_a_re', 'ssm_a_im', 'ssm_log_dt',
           'ssm_b_re', 'ssm_b_im', 'ssm_c_re', 'ssm_c_im', 'ssm_d', 'glu_w', 'glu_b', 'out_norm_pool_g',
           'out_norm_ssm_g', 'w_out', 'norm_ffn_g', 'w_up', 'conv_w', 'conv_b', 'w_down', 'final_norm_g']
SHARDED = ('w_in', 'w_out', 'w_down', 'glu_w', 'w_up', 'conv_w')
REPLICATED = tuple(n for n in WEIGHTS if n not in SHARDED)


def _pick(n, prefs):
    for p in prefs:
        if n % p == 0:
            return p
    return n


def _params(sem, vmem=None):
    return pltpu.CompilerParams(dimension_semantics=sem, vmem_limit_bytes=vmem or VMEM_LIMIT)


def _tiles(n, cap):
    return [d for d in range(LANES, min(n, cap) + 1, LANES) if n % d == 0] or [n]


def _mm(name, a, b, mode, out_dtype, add=None, b_rows=(1, 0), a_plane=None):
    batch = a.shape[0] if a.ndim == 3 and a_plane is None else None
    if mode == 'nn':
        (M, K), N = a.shape[-2:], b.shape[1]
        assert b.shape[0] == K * b_rows[0]
    elif mode == 'nt':
        (M, K), (N, _) = a.shape[-2:], b.shape
    else:
        (K, M), (_, N) = a.shape[-2:], b.shape
    dims = {'nn': (((1,), (0,)), ((), ())), 'nt': (((1,), (1,)), ((), ())), 'tn': (((0,), (0,)), ((), ()))}[mode]
    sa, sb, so = a.dtype.itemsize, b.dtype.itemsize, jnp.dtype(out_dtype).itemsize
    best = None
    for tm in _tiles(M, MM_TILE_CAP):
        for tn in _tiles(N, MM_TILE_CAP):
            need = 2 * (tm * K * sa + tn * K * sb + tm * tn * (so + (4 if add is not None else 0)))
            if need <= MM_VMEM_BUDGET:
                key = (tm * tn / (tm + tn), tm * tn)
                if best is None or key > best[0]:
                    best = (key, tm, tn)
    _, tm, tn = best
    rows_inner = a.size * sa * (N // tn) + b.size * sb < a.size * sa + b.size * sb * (M // tm)

    def kern(*refs):
        a_ref, b_ref = refs[:2]
        o_ref = refs[-1]
        r = lax.dot_general(a_ref[...].astype(BF16), b_ref[...].astype(BF16), dims, preferred_element_type=F32)
        if add is not None:
            r = r + refs[2][...]
        o_ref[...] = r.astype(o_ref.dtype)

    if rows_inner:
        grid = (N // tn, M // tm)
        ij = lambda g0, g1: (g1, g0)
    else:
        grid = (M // tm, N // tn)
        ij = lambda g0, g1: (g0, g1)

    def spec(block, index, batched=False):
        if batch is None:
            return pl.BlockSpec(block, lambda g0, g1: index(*ij(g0, g1)))
        if batched:
            return pl.BlockSpec((None,) + block, lambda p, g0, g1: (p,) + index(*ij(g0, g1)))
        return pl.BlockSpec(block, lambda p, g0, g1: index(*ij(g0, g1)))

    if a_plane is not None:
        assert mode != 'tn'
        a_spec = pl.BlockSpec((None, tm, K), lambda g0, g1: (a_plane, ij(g0, g1)[0], 0))
    else:
        a_spec = (spec((K, tm), lambda i, j: (0, i), True) if mode == 'tn' else spec((tm, K), lambda i, j: (i, 0), True))
    b_spec = (spec((tn, K), lambda i, j: (j, 0)) if mode == 'nt' else spec((K, tn), lambda i, j: (b_rows[1], j)))
    in_specs = [a_spec, b_spec]
    args = [a, b]
    if add is not None:
        in_specs.append(spec((tm, tn), lambda i, j: (i, j), True))
        args.append(add)
    lead = () if batch is None else (batch,)
    return pl.pallas_call(
        kern, name=name, grid=lead + grid, in_specs=in_specs,
        out_specs=spec((tm, tn), lambda i, j: (i, j), True),
        out_shape=jax.ShapeDtypeStruct(lead + (M, N), out_dtype),
        compiler_params=_params(("parallel",) * (len(lead) + 2)),
    )(*args)


def _store_rows_and_sums(i, outs, res, n_row):
    for o, v in zip(outs[:n_row], res[:n_row]):
        o[...] = v.astype(o.dtype)
    for o, v in zip(outs[n_row:], res[n_row:]):
        @pl.when(i == 0)
        def _(o=o, v=v):
            o[...] = v

        @pl.when(i != 0)
        def _(o=o, v=v):
            o[...] += v


def _mm_rows(name, a, b, mode, fn, rows, fulls, out_rows, out_accs=(), add=None, b_rows=(1, 0), a_plane=None):
    M, K = a.shape[-2:]
    N = b.shape[1] if mode == 'nn' else b.shape[0]
    assert b.shape[0] == (K * b_rows[0] if mode == 'nn' else N)
    dims = _NT if mode == 'nt' else (((1,), (0,)), ((), ()))
    tm = min(MM_ROWS_TILE, M)
    n_extra, n_row = len(rows) + len(fulls), len(out_rows)

    def kern(*refs):
        i = pl.program_id(0)
        r = lax.dot_general(refs[0][...].astype(BF16), refs[1][...].astype(BF16), dims, preferred_element_type=F32)
        pos = 2
        if add is not None:
            r = r + refs[pos][...]
            pos += 1
        res = fn(r, *[x[...] for x in refs[pos:pos + n_extra]])
        _store_rows_and_sums(i, refs[pos + n_extra:], res, n_row)

    def full_spec(shape):
        nd = len(shape)
        return pl.BlockSpec(tuple(shape), lambda i: (0,) * nd)

    a_spec = (pl.BlockSpec((tm, K), lambda i: (i, 0)) if a_plane is None
              else pl.BlockSpec((None, tm, K), lambda i: (a_plane, i, 0)))
    b_spec = pl.BlockSpec((K, N), lambda i: (b_rows[1], 0)) if mode == 'nn' else pl.BlockSpec((N, K), lambda i: (0, 0))
    ins = [_row_in(r, tm, M) for r in ([add] if add is not None else []) + list(rows)]
    outs = [_row_out(o, tm, M) for o in out_rows]
    res = pl.pallas_call(
        kern, name=name, grid=(M // tm,),
        in_specs=[a_spec, b_spec] + [s for s, _ in ins] + [full_spec(f.shape) for f in fulls],
        out_specs=[s for s, _ in outs] + [full_spec(s) for s in out_accs],
        out_shape=[sh for _, sh in outs] + [jax.ShapeDtypeStruct(tuple(s), F32) for s in out_accs],
        compiler_params=_params(("arbitrary",)),
    )(a, b, *[x for _, x in ins], *fulls)
    return res


def _row_in(r, tm, T):
    arr, w, k = r if isinstance(r, tuple) else (r, r.shape[1], 0)
    return pl.BlockSpec((tm, w), lambda i: (i, k)), arr


def _row_out(o, tm, T):
    c, dt = o
    return pl.BlockSpec((tm, c), lambda i: (i, 0)), jax.ShapeDtypeStruct((T, c), dt)


def _rowmap(name, fn, rows, fulls, out_rows, out_accs=(), tm=512):
    T = (rows[0][0] if isinstance(rows[0], tuple) else rows[0]).shape[0]
    tm = min(tm, T)
    assert T % tm == 0
    n_in, n_row = len(rows) + len(fulls), len(out_rows)

    def kern(*refs):
        i = pl.program_id(0)
        res = fn(*[r[...] for r in refs[:n_in]])
        res = res if isinstance(res, (tuple, list)) else (res,)
        _store_rows_and_sums(i, refs[n_in:], res, n_row)

    def full_spec(shape):
        nd = len(shape)
        return pl.BlockSpec(tuple(shape), lambda i: (0,) * nd)

    ins = [_row_in(r, tm, T) for r in rows]
    outs = [_row_out(o, tm, T) for o in out_rows]
    res = pl.pallas_call(
        kern, name=name, grid=(T // tm,),
        in_specs=[s for s, _ in ins] + [full_spec(f.shape) for f in fulls],
        out_specs=[s for s, _ in outs] + [full_spec(s) for s in out_accs],
        out_shape=[sh for _, sh in outs] + [jax.ShapeDtypeStruct(tuple(s), F32) for s in out_accs],
        compiler_params=_params(("arbitrary",)),
    )(*[x for _, x in ins], *fulls)
    return res


def _colsum(v):
    return jnp.sum(v, axis=0, keepdims=True)


def _rowmean(v):
    return jnp.mean(v, axis=-1, keepdims=True)


def _rms_fwd(x, g):
    r = lax.rsqrt(_rowmean(x * x) + EPS)
    return x * r * g


def _rms_bwd(x, g, dy):
    r = lax.rsqrt(_rowmean(x * x) + EPS)
    xh = x * r
    dxh = dy * g
    return r * (dxh - xh * _rowmean(dxh * xh)), _colsum(dy * xh)


def _sigmoid(v):
    return 0.5 * jnp.tanh(0.5 * v) + 0.5


def _gelu(y):
    c = np.sqrt(2.0 / np.pi).astype(np.float32)
    return 0.5 * y * (1.0 + jnp.tanh(c * (y + 0.044715 * (y * y * y))))


def _gelu_grad(y):
    c = np.sqrt(2.0 / np.pi).astype(np.float32)
    th = jnp.tanh(c * (y + 0.044715 * (y * y * y)))
    return 0.5 * (1.0 + th) + 0.5 * y * (1.0 - th * th) * c * (1.0 + 3.0 * 0.044715 * (y * y))


def _split3(v):
    hi = v.astype(BF16)
    r1 = v - hi.astype(F32)
    mid = r1.astype(BF16)
    lo = (r1 - mid.astype(F32)).astype(BF16)
    return hi, mid, lo


def _band(tm, lo_off, hi_off):
    shape = (tm, tm + 2 * POOL_PAD)
    d = lax.broadcasted_iota(jnp.int32, shape, 1) - lax.broadcasted_iota(jnp.int32, shape, 0) - POOL_PAD
    return ((d >= lo_off) & (d < hi_off)).astype(BF16)


def _window_sums(ext, tm, lo_off, hi_off):
    hi, mid, lo = _split3(ext)
    band = _band(LANES, lo_off, hi_off)
    dot = functools.partial(jnp.dot, preferred_element_type=F32)
    out = []
    for s in range(tm // LANES):
        rows = slice(s * LANES, (s + 2) * LANES)
        out.append(dot(band, hi[rows]) + dot(band, mid[rows]) + dot(band, lo[rows]))
    return jnp.concatenate(out, axis=0)


def _window_count(r0, tm, half, T):
    t = r0 + lax.broadcasted_iota(jnp.int32, (tm, 1), 0)
    return (jnp.minimum(t + half, T) - jnp.maximum(t - half, 0)).astype(F32)


def _halo_specs(tm, C, T):
    per = tm // POOL_PAD
    last = T // POOL_PAD - 1
    return [pl.BlockSpec((POOL_PAD, C), lambda i: (jnp.maximum(i * per - 1, 0), 0)),
            pl.BlockSpec((tm, C), lambda i: (i, 0)),
            pl.BlockSpec((POOL_PAD, C), lambda i: (jnp.minimum((i + 1) * per, last), 0))]


def _with_halo(before_ref, main_ref, after_ref, i, n):
    before = jnp.where(i > 0, before_ref[...], 0.0)
    after = jnp.where(i < n - 1, after_ref[...], 0.0)
    return jnp.concatenate([before, main_ref[...], after], axis=0)


def _pool_block(ext, ctr, w_ref, r0, tm, T):
    pooled, conc = [], []
    for gi, w in enumerate(POOL_WINDOWS):
        half = w // 2
        cols = slice(gi * POOL_GROUP, (gi + 1) * POOL_GROUP)
        ws = _window_sums(ext[:, cols], tm, -half, half)
        p = ws / _window_count(r0, tm, half, T) - ctr[:, cols]
        pooled.append(p)
        conc.append(jnp.dot(p.astype(BF16), w_ref[gi], preferred_element_type=F32))
    return pooled, jnp.concatenate(conc, axis=1)


def _pool_fwd(u, pool_w_bf, pool_scale, g_pool, tm):
    T = u.shape[0]
    nw = len(POOL_WINDOWS)
    C = nw * POOL_GROUP
    n = T // tm

    def kern(ub_ref, u_ref, ua_ref, w_ref, sc_ref, g_ref, o_ref):
        i = pl.program_id(0)
        r0 = pl.multiple_of(i * tm, tm)
        _, conc = _pool_block(_with_halo(ub_ref, u_ref, ua_ref, i, n), u_ref[...], w_ref, r0, tm, T)
        o_ref[...] = _rms_fwd(conc * sc_ref[...], g_ref[...]).astype(o_ref.dtype)

    return pl.pallas_call(
        kern, name="pool_fwd", grid=(n,),
        in_specs=_halo_specs(tm, C, T) + [pl.BlockSpec(pool_w_bf.shape, lambda i: (0, 0, 0)),
                                          pl.BlockSpec((1, C), lambda i: (0, 0)), pl.BlockSpec((1, C), lambda i: (0, 0))],
        out_specs=pl.BlockSpec((tm, C), lambda i: (i, 0)),
        out_shape=jax.ShapeDtypeStruct((T, C), BF16),
        compiler_params=_params(("arbitrary",)),
    )(u, u, u, pool_w_bf, pool_scale, g_pool)


def _pool_bwd_rows(u, d_y, pool_w_bf, pool_scale, g_pool, tm):
    T = u.shape[0]
    nw = len(POOL_WINDOWS)
    C = nw * POOL_GROUP
    n = T // tm

    def kern(ub_ref, u_ref, ua_ref, dy_ref, w_ref, sc_ref, g_ref, q_ref, dp_ref, dw_ref, dsc_ref, dg_ref):
        i = pl.program_id(0)
        r0 = pl.multiple_of(i * tm, tm)
        pooled, conc = _pool_block(_with_halo(ub_ref, u_ref, ua_ref, i, n), u_ref[...], w_ref, r0, tm, T)
        sc = sc_ref[...]
        dyp, dg = _rms_bwd(conc * sc, g_ref[...], dy_ref[...])
        dsc = _colsum(dyp * conc)
        dconc = (dyp * sc).astype(BF16)
        dws = []
        for gi, w in enumerate(POOL_WINDOWS):
            cols = slice(gi * POOL_GROUP, (gi + 1) * POOL_GROUP)
            dc = dconc[:, cols]
            dp = lax.dot_general(dc, w_ref[gi], (((1,), (1,)), ((), ())), preferred_element_type=F32)
            dp_ref[:, cols] = dp
            q_ref[:, cols] = dp / _window_count(r0, tm, w // 2, T)
            dws.append(lax.dot_general(pooled[gi].astype(BF16), dc, (((0,), (0,)), ((), ())), preferred_element_type=F32))

        @pl.when(i == 0)
        def _():
            for gi in range(nw):
                dw_ref[gi] = dws[gi]
            dsc_ref[...] = dsc
            dg_ref[...] = dg

        @pl.when(i != 0)
        def _():
            for gi in range(nw):
                dw_ref[gi] += dws[gi]
            dsc_ref[...] += dsc
            dg_ref[...] += dg

    full2 = pl.BlockSpec((1, C), lambda i: (0, 0))
    row_blk = pl.BlockSpec((tm, C), lambda i: (i, 0))
    return pl.pallas_call(
        kern, name="pool_bwd_rows", grid=(n,),
        in_specs=_halo_specs(tm, C, T) + [row_blk, pl.BlockSpec(pool_w_bf.shape, lambda i: (0, 0, 0)), full2, full2],
        out_specs=[row_blk, row_blk, pl.BlockSpec((nw, POOL_GROUP, POOL_GROUP), lambda i: (0, 0, 0)), full2, full2],
        out_shape=[jax.ShapeDtypeStruct((T, C), F32), jax.ShapeDtypeStruct((T, C), F32),
                   jax.ShapeDtypeStruct((nw, POOL_GROUP, POOL_GROUP), F32),
                   jax.ShapeDtypeStruct((1, C), F32), jax.ShapeDtypeStruct((1, C), F32)],
        compiler_params=_params(("arbitrary",)),
    )(u, u, u, d_y, pool_w_bf, pool_scale, g_pool)


def _pool_bwd_band(q, dpooled, du_other, after, tm):
    T, C = dpooled.shape
    C2 = du_other.shape[1]
    n = T // tm

    def kern(qb_ref, q_ref, qa_ref, dp_ref, other_ref, after_ref, o_ref):
        ext = _with_halo(qb_ref, q_ref, qa_ref, pl.program_id(0), n)
        for gi, w in enumerate(POOL_WINDOWS):
            half = w // 2
            cols = slice(gi * POOL_GROUP, (gi + 1) * POOL_GROUP)
            du = _window_sums(ext[:, cols], tm, -half + 1, half + 1) - dp_ref[:, cols]
            o_ref[:, cols] = du.astype(o_ref.dtype)
        o_ref[:, C:] = other_ref[...].astype(o_ref.dtype)

    return pl.pallas_call(
        kern, name="pool_bwd_band", grid=(n,),
        in_specs=_halo_specs(tm, C, T) + [pl.BlockSpec((tm, C), lambda i: (i, 0)), pl.BlockSpec((tm, C2), lambda i: (i, 0)),
                                          pl.BlockSpec(after.shape, lambda i: (0, 0))],
        out_specs=pl.BlockSpec((tm, C + C2), lambda i: (i, 0)),
        out_shape=jax.ShapeDtypeStruct((T, C + C2), BF16),
        compiler_params=_params(("arbitrary",)),
    )(q, q, q, dpooled, du_other, after)


def _to_chunk_rows(a):
    T, C = a.shape
    return a.reshape(N_CHUNK, T // N_CHUNK, C).transpose(1, 0, 2).reshape(T, C)


def _from_chunk_rows(a):
    T, C = a.shape
    return a.reshape(T // N_CHUNK, N_CHUNK, C).transpose(1, 0, 2).reshape(T, C)


def _cmul(ar, ai, br, bi):
    return ar * br - ai * bi, ar * bi + ai * br


def _ssm_discretise(log_neg_a_re, a_im, log_dt):
    dt = jnp.exp(log_dt)
    a_re = -jnp.exp(log_neg_a_re)
    mag = jnp.exp(a_re * dt)
    ang = a_im * dt
    lam_re, lam_im = mag * jnp.cos(ang), mag * jnp.sin(ang)
    den = a_re * a_re + a_im * a_im
    f_re = ((lam_re - 1.0) * a_re + lam_im * a_im) / den
    f_im = (lam_im * a_re - (lam_re - 1.0) * a_im) / den
    return lam_re, lam_im, f_re, f_im


def _ssm_params_fwd(log_neg_a_re, a_im, log_dt):
    rows, n = log_neg_a_re.shape

    def kern(a_ref, b_ref, c_ref, o1, o2, o3, o4):
        for o, v in zip((o1, o2, o3, o4), _ssm_discretise(a_ref[...], b_ref[...], c_ref[...])):
            o[...] = v

    return pl.pallas_call(kern, name="ssm_params_fwd", out_shape=[jax.ShapeDtypeStruct((rows, n), F32)] * 4)(
        log_neg_a_re, a_im, log_dt)


def _ssm_params_bwd(log_neg_a_re, a_im, log_dt, cots):
    rows, n = log_neg_a_re.shape

    def kern(a_ref, b_ref, c_ref, g1, g2, g3, g4, o1, o2, o3):
        _, vjp = jax.vjp(_ssm_discretise, a_ref[...], b_ref[...], c_ref[...])
        d1, d2, d3 = vjp((g1[...], g2[...], g3[...], g4[...]))
        o1[...] = d1
        o2[...] = d2
        o3[...] = d3

    return pl.pallas_call(
        kern, name="ssm_params_bwd",
        out_shape=[jax.ShapeDtypeStruct((rows, n), F32), jax.ShapeDtypeStruct((rows, n), F32),
                   jax.ShapeDtypeStruct((rows, 1), F32)])(log_neg_a_re, a_im, log_dt, *cots)


def _cpow(lr, li, n):
    out = None
    br, bi = lr, li
    while n:
        if n & 1:
            out = (br, bi) if out is None else _cmul(out[0], out[1], br, bi)
        n >>= 1
        if n:
            br, bi = _cmul(br, bi, br, bi)
    return out


def _slab(t):
    return pl.ds(pl.multiple_of(t * N_CHUNK, N_CHUNK), N_CHUNK)


def _steps(n, step, carry):
    main = n // SCAN_UNROLL

    def body(i, c):
        for r in range(SCAN_UNROLL):
            c = step(i * SCAN_UNROLL + r, c)
        return c

    carry = lax.fori_loop(0, main, body, carry)
    for k in range(main * SCAN_UNROLL, n):
        carry = step(jnp.int32(k), carry)
    return carry


def _shift_chunks(vr, vi, reverse):
    sub = lax.broadcasted_iota(jnp.int32, vr.shape, 0)
    if reverse:
        keep = sub != N_CHUNK - 1
        return jnp.where(keep, pltpu.roll(vr, N_CHUNK - 1, 0), 0.0), jnp.where(keep, pltpu.roll(vi, N_CHUNK - 1, 0), 0.0)
    keep = sub != 0
    return jnp.where(keep, pltpu.roll(vr, 1, 0), 0.0), jnp.where(keep, pltpu.roll(vi, 1, 0), 0.0)


def _chunk_scan(xr, xi, lr, li, Lc, reverse, prev=None):
    C = xr.shape[1]
    lrb, lib = jnp.broadcast_to(lr, (N_CHUNK, C)), jnp.broadcast_to(li, (N_CHUNK, C))
    zero = jnp.zeros((N_CHUNK, C), F32)

    def step_of(k):
        return (Lc - 1 - k) if reverse else k

    def advance(sr, si, t):
        return lrb * sr - lib * si + xr[_slab(t), :], lrb * si + lib * sr + xi[_slab(t), :]

    def pass1(k, c):
        return advance(c[0], c[1], step_of(k))

    er, ei = _steps(Lc, pass1, (zero, zero))
    pr, pi = _cpow(lr, li, Lc)
    prb, pib = jnp.broadcast_to(pr, (N_CHUNK, C)), jnp.broadcast_to(pi, (N_CHUNK, C))
    cr, ci = zero, zero
    for _ in range(N_CHUNK - 1):
        cr, ci = _shift_chunks(er + prb * cr - pib * ci, ei + prb * ci + pib * cr, reverse)

    if prev is None:
        def pass2(k, c):
            t = step_of(k)
            nr, ni = advance(c[0], c[1], t)
            xr[_slab(t), :] = nr
            xi[_slab(t), :] = ni
            return nr, ni

        _steps(Lc, pass2, (cr, ci))
        return None

    qr, qi = prev
    def pair(ar, ai, wr, wi, acc):
        return acc[0] + ar * wr + ai * wi, acc[1] + ai * wr - ar * wi

    def pass2(k, c):
        sr, si, accr, acci = c
        t = step_of(k)
        nr, ni = advance(sr, si, t)
        xr[_slab(t), :] = nr
        xi[_slab(t), :] = ni
        tp = t - 1 if reverse else t + 1
        accr, acci = pair(nr, ni, qr[_slab(tp), :], qi[_slab(tp), :], (accr, acci))
        return nr, ni, accr, acci

    sr, si, accr, acci = _steps(Lc - 1, pass2, (cr, ci, zero, zero))
    t = step_of(Lc - 1)
    nr, ni = advance(sr, si, t)
    xr[_slab(t), :] = nr
    xi[_slab(t), :] = ni
    edge = step_of(0)
    wr, wi = _shift_chunks(qr[_slab(edge), :], qi[_slab(edge), :], not reverse)
    return pair(nr, ni, wr, wi, (accr, acci))


def _ssm_dims(T):
    assert T % ROW_BLOCK == 0
    return T // N_CHUNK, T // ROW_BLOCK


def _group_mask():
    shape = (SSM_BLOCK_GROUPS * SSM_GROUP, SSM_BLOCK_GROUPS * SSM_STATE)
    rows = lax.broadcasted_iota(jnp.int32, shape, 0) >> GROUP_SHIFT
    cols = lax.broadcasted_iota(jnp.int32, shape, 1) >> STATE_SHIFT
    return rows == cols


def _expand_groups(compact):
    return jnp.where(_group_mask(), jnp.tile(compact, (SSM_BLOCK_GROUPS, 1)), 0.0).astype(BF16)


def _collect_groups(block):
    kept = jnp.where(_group_mask(), block, 0.0)
    out = kept[0:SSM_GROUP]
    for g in range(1, SSM_BLOCK_GROUPS):
        out = out + kept[g * SSM_GROUP:(g + 1) * SSM_GROUP]
    return out


def _ssm_specs(T, direction):
    SC = SSM_BLOCK_GROUPS * SSM_STATE
    col = pl.BlockSpec((T, LANES), lambda m: (0, m))
    mat = pl.BlockSpec((None, None, SSM_GROUP, SC), lambda m: (direction, m, 0, 0))
    vec = pl.BlockSpec((None, None, 1, SC), lambda m: (direction, m, 0, 0))
    return col, mat, vec


def _ssm_project_in(u_ref, bre, bim, xr, xi, nblk):
    for blk in range(nblk):
        rows = pl.ds(blk * ROW_BLOCK, ROW_BLOCK)
        ub = u_ref[rows, :].astype(BF16)
        xr[rows, :] = jnp.dot(ub, bre, preferred_element_type=F32)
        xi[rows, :] = jnp.dot(ub, bim, preferred_element_type=F32)


def _ssm_fwd(name, u_bf, mats, vecs, direction):
    T, W = u_bf.shape
    SC = SSM_BLOCK_GROUPS * SSM_STATE
    nb = W // LANES
    Lc, nblk = _ssm_dims(T)
    reverse = bool(direction)

    def kern(u_ref, bre_ref, bim_ref, cre_ref, cim_ref, lr_ref, li_ref, fr_ref, fi_ref, y_ref, xr, xi):
        _ssm_project_in(u_ref, _expand_groups(bre_ref[...]), _expand_groups(bim_ref[...]), xr, xi, nblk)
        _chunk_scan(xr, xi, lr_ref[...], li_ref[...], Lc, reverse)
        fr, fi = fr_ref[...], fi_ref[...]
        cre, cim = _expand_groups(cre_ref[...]), _expand_groups(cim_ref[...])
        for blk in range(nblk):
            rows = pl.ds(blk * ROW_BLOCK, ROW_BLOCK)
            sr, si = _cmul(fr, fi, xr[rows, :], xi[rows, :])
            y_ref[rows, :] = (lax.dot_general(sr.astype(BF16), cre, _NT, preferred_element_type=F32)
                              - lax.dot_general(si.astype(BF16), cim, _NT, preferred_element_type=F32))

    col, mat, vec = _ssm_specs(T, direction)
    return pl.pallas_call(
        kern, name=name, grid=(nb,),
        in_specs=[col, mat, mat, mat, mat, vec, vec, vec, vec],
        out_specs=col, out_shape=jax.ShapeDtypeStruct((T, W), F32),
        scratch_shapes=[pltpu.VMEM((T, SC), F32), pltpu.VMEM((T, SC), F32)],
        compiler_params=_params(("arbitrary",)),
    )(u_bf, *mats, *vecs)


def _ssm_bwd(name, u_bf, dy_bf, du_in, mats, vecs, direction, du_dtype):
    T, W = u_bf.shape
    SC = SSM_BLOCK_GROUPS * SSM_STATE
    nb = W // LANES
    Lc, nblk = _ssm_dims(T)
    reverse = bool(direction)

    def kern(u_ref, dy_ref, dui_ref, bre_ref, bim_ref, cre_ref, cim_ref, lr_ref, li_ref, fr_ref, fi_ref,
             du_ref, dbre_ref, dbim_ref, dcre_ref, dcim_ref, dlr_ref, dli_ref, dfr_ref, dfi_ref,
             xr, xi, gr, gi):
        lr, li, fr, fi = lr_ref[...], li_ref[...], fr_ref[...], fi_ref[...]
        bre, bim = _expand_groups(bre_ref[...]), _expand_groups(bim_ref[...])
        cre, cim = _expand_groups(cre_ref[...]), _expand_groups(cim_ref[...])
        _ssm_project_in(u_ref, bre, bim, xr, xi, nblk)
        _chunk_scan(xr, xi, lr, li, Lc, reverse)
        mre = jnp.zeros((LANES, SC), F32)
        mim = jnp.zeros((LANES, SC), F32)
        dfr = jnp.zeros((1, SC), F32)
        dfi = jnp.zeros((1, SC), F32)
        for blk in range(nblk):
            rows = pl.ds(blk * ROW_BLOCK, ROW_BLOCK)
            dy = dy_ref[rows, :].astype(BF16)
            x_r, x_i = xr[rows, :], xi[rows, :]
            g_r = jnp.dot(dy, cre, preferred_element_type=F32)
            g_i = -jnp.dot(dy, cim, preferred_element_type=F32)
            mre += lax.dot_general(dy, x_r.astype(BF16), _TN, preferred_element_type=F32)
            mim += lax.dot_general(dy, x_i.astype(BF16), _TN, preferred_element_type=F32)
            dfr += _colsum(g_r * x_r + g_i * x_i)
            dfi += _colsum(g_i * x_r - g_r * x_i)
            gr[rows, :] = fr * g_r + fi * g_i
            gi[rows, :] = fr * g_i - fi * g_r
        dcre_ref[...] = _collect_groups(fr * mre - fi * mim)
        dcim_ref[...] = _collect_groups(-(fr * mim + fi * mre))
        dfr_ref[...] = dfr
        dfi_ref[...] = dfi
        accr, acci = _chunk_scan(gr, gi, lr, -li, Lc, not reverse, prev=(xr, xi))
        dlr_ref[...] = _colsum(accr)
        dli_ref[...] = _colsum(acci)
        dbre = jnp.zeros((LANES, SC), F32)
        dbim = jnp.zeros((LANES, SC), F32)
        for blk in range(nblk):
            rows = pl.ds(blk * ROW_BLOCK, ROW_BLOCK)
            ub = u_ref[rows, :].astype(BF16)
            a_r, a_i = gr[rows, :].astype(BF16), gi[rows, :].astype(BF16)
            dbre += lax.dot_general(ub, a_r, _TN, preferred_element_type=F32)
            dbim += lax.dot_general(ub, a_i, _TN, preferred_element_type=F32)
            du_ref[rows, :] = (dui_ref[rows, :] + lax.dot_general(a_r, bre, _NT, preferred_element_type=F32)
                               + lax.dot_general(a_i, bim, _NT, preferred_element_type=F32)).astype(du_ref.dtype)
        dbre_ref[...] = _collect_groups(dbre)
        dbim_ref[...] = _collect_groups(dbim)

    col, mat, vec = _ssm_specs(T, direction)
    mat_out = pl.BlockSpec((None, SSM_GROUP, SC), lambda m: (m, 0, 0))
    vec_out = pl.BlockSpec((None, 1, SC), lambda m: (m, 0, 0))
    mat_shape = jax.ShapeDtypeStruct((nb, SSM_GROUP, SC), F32)
    vec_shape = jax.ShapeDtypeStruct((nb, 1, SC), F32)
    return pl.pallas_call(
        kern, name=name, grid=(nb,),
        in_specs=[col, col, col, mat, mat, mat, mat, vec, vec, vec, vec],
        out_specs=[col, mat_out, mat_out, mat_out, mat_out, vec_out, vec_out, vec_out, vec_out],
        out_shape=[jax.ShapeDtypeStruct((T, W), du_dtype), mat_shape, mat_shape, mat_shape, mat_shape,
                   vec_shape, vec_shape, vec_shape, vec_shape],
        scratch_shapes=[pltpu.VMEM((T, SC), F32)] * 4,
        compiler_params=_params(("arbitrary",)),
    )(u_bf, dy_bf, du_in, *mats, *vecs)


def _groups_side_by_side(p, channel_axis):
    d, G = p.shape[:2]
    nb = G // SSM_BLOCK_GROUPS
    v = p.reshape(d, nb, SSM_BLOCK_GROUPS, p.shape[2], p.shape[3])
    v = v.transpose(0, 1, 4, 2, 3) if channel_axis == 3 else v.transpose(0, 1, 3, 2, 4)
    return v.reshape(d, nb, SSM_GROUP, SSM_BLOCK_GROUPS * SSM_STATE)


def _groups_apart(m, channel_axis):
    d, nb = m.shape[:2]
    v = m.reshape(d, nb, SSM_GROUP, SSM_BLOCK_GROUPS, SSM_STATE)
    v = v.transpose(0, 1, 3, 4, 2) if channel_axis == 3 else v.transpose(0, 1, 3, 2, 4)
    return v.reshape((d, nb * SSM_BLOCK_GROUPS) + v.shape[3:])


def _conv_taps(ref, c, R, T):
    n = T // R
    r0 = pl.multiple_of(c * R, R)
    main = ref[pl.ds(r0, R), :]
    before = ref[pl.ds(pl.multiple_of(jnp.maximum(r0 - 8, 0), 8), 8), :]
    after = ref[pl.ds(pl.multiple_of(jnp.minimum(r0 + R, T - 8), 8), 8), :]
    ext = jnp.concatenate([jnp.where(c > 0, before, 0.0), main, jnp.where(c < n - 1, after, 0.0)], axis=0)
    m1 = pltpu.roll(ext, 1, 0)[8:8 + R]
    p1 = pltpu.roll(ext, R + 15, 0)[8:8 + R]
    return m1, main, p1, r0


def _conv_specs(T, F, cb):
    nj = F // cb
    lo = lambda rows: pl.BlockSpec((rows, cb), lambda j: (0, j))
    hi = lambda rows: pl.BlockSpec((rows, cb), lambda j: (0, j + nj))
    return nj, lo, hi


def _convact_fwd(up, conv_w, conv_b, F):
    T = up.shape[0]
    cb = _pick(F, (256, 128))
    R = _pick(T, (256, 128, 64))
    nj, lo, hi = _conv_specs(T, F, cb)

    def kern(uv, ug, wv, wg, bv, bg, o_ref):
        def body(c, carry):
            v1, v0, v2, r0 = _conv_taps(uv, c, R, T)
            g1, g0, g2, _ = _conv_taps(ug, c, R, T)
            val = v1 * wv[0:1, :] + v0 * wv[1:2, :] + v2 * wv[2:3, :] + bv[...]
            gate = g1 * wg[0:1, :] + g0 * wg[1:2, :] + g2 * wg[2:3, :] + bg[...]
            o_ref[pl.ds(r0, R), :] = (val * (gate * _sigmoid(gate))).astype(o_ref.dtype)
            return carry

        lax.fori_loop(0, T // R, body, 0)

    return pl.pallas_call(
        kern, name="convact_fwd", grid=(nj,),
        in_specs=[lo(T), hi(T), lo(3), hi(3), lo(1), hi(1)],
        out_specs=lo(T), out_shape=jax.ShapeDtypeStruct((T, F), BF16),
        compiler_params=_params(("arbitrary",)),
    )(up, up, conv_w, conv_w, conv_b, conv_b)


def _convact_bwd(up, dact, conv_w, conv_b, F):
    T = up.shape[0]
    cb = _pick(F, (256, 128))
    R = _pick(T, (256, 128, 64))
    nj, lo, hi = _conv_specs(T, F, cb)

    def kern(uv, ug, da, wv, wg, bv, bg, dup, dwv, dwg, dbv, dbg, sv, sg):
        zero = jnp.zeros((1, cb), F32)

        def pass_a(c, acc):
            v1, v0, v2, r0 = _conv_taps(uv, c, R, T)
            g1, g0, g2, _ = _conv_taps(ug, c, R, T)
            val = v1 * wv[0:1, :] + v0 * wv[1:2, :] + v2 * wv[2:3, :] + bv[...]
            gate = g1 * wg[0:1, :] + g0 * wg[1:2, :] + g2 * wg[2:3, :] + bg[...]
            sig = _sigmoid(gate)
            d = da[pl.ds(r0, R), :]
            dval = d * (gate * sig)
            dgate = d * val * (sig * (1.0 + gate * (1.0 - sig)))
            sv[pl.ds(r0, R), :] = dval
            sg[pl.ds(r0, R), :] = dgate
            terms = (dval * v1, dval * v0, dval * v2, dval, dgate * g1, dgate * g0, dgate * g2, dgate)
            return tuple(a + _colsum(t) for a, t in zip(acc, terms))

        acc = lax.fori_loop(0, T // R, pass_a, (zero,) * 8)
        for k in range(3):
            dwv[k:k + 1, :] = acc[k]
            dwg[k:k + 1, :] = acc[4 + k]
        dbv[...] = acc[3]
        dbg[...] = acc[7]

        def pass_b(c, carry):
            v1, v0, v2, r0 = _conv_taps(sv, c, R, T)
            g1, g0, g2, _ = _conv_taps(sg, c, R, T)
            dup[0, pl.ds(r0, R), :] = (v2 * wv[0:1, :] + v0 * wv[1:2, :] + v1 * wv[2:3, :]).astype(dup.dtype)
            dup[1, pl.ds(r0, R), :] = (g2 * wg[0:1, :] + g0 * wg[1:2, :] + g1 * wg[2:3, :]).astype(dup.dtype)
            return carry

        lax.fori_loop(0, T // R, pass_b, 0)

    dup, dwv, dwg, dbv, dbg = pl.pallas_call(
        kern, name="convact_bwd", grid=(nj,),
        in_specs=[lo(T), hi(T), lo(T), lo(3), hi(3), lo(1), hi(1)],
        out_specs=[pl.BlockSpec((2, T, cb), lambda j: (0, 0, j)), lo(3), lo(3), lo(1), lo(1)],
        out_shape=[jax.ShapeDtypeStruct((2, T, F), BF16),
                   jax.ShapeDtypeStruct((3, F), F32), jax.ShapeDtypeStruct((3, F), F32),
                   jax.ShapeDtypeStruct((1, F), F32), jax.ShapeDtypeStruct((1, F), F32)],
        scratch_shapes=[pltpu.VMEM((T, cb), F32), pltpu.VMEM((T, cb), F32)],
        compiler_params=_params(("arbitrary",)),
    )(up, up, dact, conv_w, conv_w, conv_b, conv_b)
    return dup, (dwv, dwg), jnp.concatenate([dbv, dbg], axis=1)


def _peers(x, y, c, with_self=False):
    out = []
    for k in range(0 if with_self else 1, N_DEV):
        px = 1 - x if k & 4 else x
        py = 1 - y if k & 2 else y
        pc = 1 - c if k & 1 else c
        out.append((k, (px, py, pc), 4 * px + 2 * py + pc))
    return out


def _exchange_start(name, bufs, modes, after):
    handles, token = _exchange_start_groups(name, [(bufs, modes)], after)
    return handles[0], token


def _exchange_start_groups(name, groups, after):
    bufs = [b for g, _ in groups for b in g]
    modes = [m for _, ms in groups for m in ms]
    first = [int(v) for v in np.cumsum([0] + [len(g) for g, _ in groups])]
    n, ng = len(bufs), len(groups)
    lands = [lax.empty((N_DEV,) + tuple(b.shape[-2:]), b.dtype) for b in bufs]

    def body(*refs):
        ins, land_in = refs[:n], refs[n:2 * n]
        sems = refs[2 * n + 1:2 * n + 1 + 2 * ng]
        token = refs[-1]
        x, y, c = lax.axis_index("x"), lax.axis_index("y"), lax.axis_index("c")
        me = 4 * x + 2 * y + c
        for g in range(ng):
            for b in range(first[g], first[g + 1]):
                for k, peer, slot in _peers(x, y, c, with_self=True):
                    sem = (b - first[g]) * N_DEV + k
                    pltpu.make_async_remote_copy(
                        src_ref=ins[b] if modes[b] == 'gather' else ins[b].at[slot], dst_ref=land_in[b].at[me],
                        send_sem=sems[2 * g].at[sem], recv_sem=sems[2 * g + 1].at[sem],
                        device_id=peer, device_id_type=MESH_ID).start()
        token[...] = jnp.zeros_like(token)

    hbm = pl.BlockSpec(memory_space=pltpu.HBM)
    sem_spec = pl.BlockSpec(memory_space=pltpu.SEMAPHORE)
    operands = [pltpu.with_memory_space_constraint(a, pltpu.HBM) for a in bufs + lands]
    sem_shapes = [pltpu.SemaphoreType.DMA((len(g) * N_DEV,)) for g, _ in groups for _ in range(2)]
    res = pl.pallas_call(
        body, name=name,
        out_shape=(*sem_shapes, *[pltpu.HBM(a.shape, a.dtype) for a in operands], jax.ShapeDtypeStruct((8, LANES), F32)),
        in_specs=[hbm] * (2 * n) + [pl.BlockSpec(memory_space=pl.ANY)],
        out_specs=(*([sem_spec] * (2 * ng)), *([hbm] * (2 * n)), pl.BlockSpec(memory_space=pltpu.VMEM)),
        input_output_aliases={i: 2 * ng + i for i in range(2 * n)},
        compiler_params=pltpu.CompilerParams(has_side_effects=pltpu.SideEffectType.DATAFLOW_SIDE_EFFECTING),
    )(*operands, after)
    srcs, landed = res[2 * ng:2 * ng + n], res[2 * ng + n:2 * ng + 2 * n]
    handles = [(res[2 * g], res[2 * g + 1], list(srcs[first[g]:first[g + 1]]), list(landed[first[g]:first[g + 1]]),
                tuple(modes[first[g]:first[g + 1]])) for g in range(ng)]
    return handles, res[-1]


def _exchange_wait(name, handle, after):
    send_sems, recv_sems, srcs, lands, modes = handle
    n = len(srcs)

    def body(*refs):
        src_in, land_in = refs[:n], refs[n:2 * n]
        send_ref, recv_ref = refs[2 * n], refs[2 * n + 1]
        x, y, c = lax.axis_index("x"), lax.axis_index("y"), lax.axis_index("c")
        for b in range(n):
            for k, peer, slot in _peers(x, y, c, with_self=True):
                sem = b * N_DEV + k
                copy = pltpu.make_async_remote_copy(
                    src_ref=src_in[b] if modes[b] == 'gather' else src_in[b].at[slot], dst_ref=land_in[b].at[slot],
                    send_sem=send_ref.at[sem], recv_sem=recv_ref.at[sem],
                    device_id=peer, device_id_type=MESH_ID)
                copy.wait_send()
                copy.wait_recv()

    hbm = pl.BlockSpec(memory_space=pltpu.HBM)
    sem_spec = pl.BlockSpec(memory_space=pltpu.SEMAPHORE)
    res = pl.pallas_call(
        body, name=name,
        out_shape=tuple(pltpu.HBM(a.shape, a.dtype) for a in srcs + lands),
        in_specs=[hbm] * (2 * n) + [sem_spec, sem_spec, pl.BlockSpec(memory_space=pl.ANY)],
        out_specs=tuple([hbm] * (2 * n)),
        input_output_aliases={i: i for i in range(2 * n)},
        compiler_params=pltpu.CompilerParams(has_side_effects=pltpu.SideEffectType.DATAFLOW_SIDE_EFFECTING),
    )(*srcs, *lands, send_sems, recv_sems, after)
    return list(res[n:])


def _adamw(w, g, m, v):
    m2 = ADAM_B1 * m + (1.0 - ADAM_B1) * g
    v2 = ADAM_B2 * v + (1.0 - ADAM_B2) * (g * g)
    m_hat = m2 / (1.0 - ADAM_B1 ** ADAM_STEP)
    v_hat = v2 / (1.0 - ADAM_B2 ** ADAM_STEP)
    return -ADAM_LR * (m_hat / (jnp.sqrt(v_hat) + ADAM_EPS) + ADAM_WD * w), m2, v2


def _sum_adamw(name, landed, w, m, v):
    R, C = w.shape
    row_bytes = N_DEV * (-(-C // LANES) * LANES) * landed.dtype.itemsize
    tiles = [d for d in range(16, R, 16) if R % d == 0 and d * row_bytes <= ADAM_TILE_BYTES]
    tr = max(tiles) if tiles and R * row_bytes > ADAM_TILE_BYTES else R

    def kern(x_ref, w_ref, m_ref, v_ref, g_out, d_out, m_out, v_out):
        g = x_ref[0].astype(F32)
        for q in range(1, N_DEV):
            g = g + x_ref[q].astype(F32)
        g_out[...] = g
        d_out[...], m_out[...], v_out[...] = _adamw(w_ref[...], g, m_ref[...], v_ref[...])

    blk = pl.BlockSpec((tr, C), lambda i: (i, 0))
    return pl.pallas_call(
        kern, name=name, grid=(R // tr,),
        in_specs=[pl.BlockSpec((N_DEV, tr, C), lambda i: (0, i, 0)), blk, blk, blk],
        out_specs=[blk] * 4, out_shape=[jax.ShapeDtypeStruct((R, C), F32)] * 4,
        compiler_params=_params(("arbitrary",)),
    )(landed, w, m, v)


def _sum_adamw_many(name, items):
    n = len(items)

    def kern(*refs):
        ins, outs = refs[:4 * n], refs[4 * n:]
        for k in range(n):
            x_ref, w_ref, m_ref, v_ref = ins[4 * k:4 * k + 4]
            g = x_ref[0].astype(F32)
            for q in range(1, N_DEV):
                g = g + x_ref[q].astype(F32)
            outs[4 * k][...] = g
            outs[4 * k + 1][...], outs[4 * k + 2][...], outs[4 * k + 3][...] = _adamw(w_ref[...], g, m_ref[...], v_ref[...])

    out_shape = [jax.ShapeDtypeStruct(w.shape, F32) for _, w, _, _ in items for _ in range(4)]
    res = pl.pallas_call(kern, name=name, out_shape=out_shape)(*[a for item in items for a in item])
    return [res[4 * k:4 * k + 4] for k in range(n)]


def _view2d(a):
    if a.ndim == 1:
        return a.reshape(1, -1)
    return a.reshape(-1, a.shape[-1])


def kernel(x, norm_mix_g, w_in, pool_w, pool_scale, ssm_log_neg_a_re, ssm_a_im, ssm_log_dt, ssm_b_re, ssm_b_im, ssm_c_re, ssm_c_im, ssm_d, glu_w, glu_b, out_norm_pool_g, out_norm_ssm_g, w_out, norm_ffn_g, w_up, conv_w, conv_b, w_down, final_norm_g, loss_target, m_norm_mix_g, m_w_in, m_pool_w, m_pool_scale, m_ssm_log_neg_a_re, m_ssm_a_im, m_ssm_log_dt, m_ssm_b_re, m_ssm_b_im, m_ssm_c_re, m_ssm_c_im, m_ssm_d, m_glu_w, m_glu_b, m_out_norm_pool_g, m_out_norm_ssm_g, m_w_out, m_norm_ffn_g, m_w_up, m_conv_w, m_conv_b, m_w_down, m_final_norm_g, v_norm_mix_g, v_w_in, v_pool_w, v_pool_scale, v_ssm_log_neg_a_re, v_ssm_a_im, v_ssm_log_dt, v_ssm_b_re, v_ssm_b_im, v_ssm_c_re, v_ssm_c_im, v_ssm_d, v_glu_w, v_glu_b, v_out_norm_pool_g, v_out_norm_ssm_g, v_w_out, v_norm_ffn_g, v_w_up, v_conv_w, v_conv_b, v_w_down, v_final_norm_g):
    given = dict(locals())
    weights = {n: given[n] for n in WEIGHTS}
    mom1 = {n: given["m_" + n] for n in WEIGHTS}
    mom2 = {n: given["v_" + n] for n in WEIGHTS}

    xs = x[0]
    tgt = loss_target[0]
    T, D = xs.shape
    DP = len(POOL_WINDOWS) * POOL_GROUP
    DS = D - DP
    G = DS // SSM_GROUP
    N, H = SSM_STATE, SSM_GROUP
    F2 = w_up.shape[1] * N_DEV
    F = F2 // 2
    fs = w_up.shape[1]
    row = lambda a: a.reshape(1, -1)

    in_handle, token = _exchange_start("gather_in_start", [w_in.astype(BF16)], ['gather'], after=w_in)
    later = lambda w: (w + token[0, 0]).astype(BF16)
    gather_groups = [([later(glu_w), later(w_out)], ['gather'] * 2),
                     ([later(w_up.T), conv_w], ['gather'] * 2), ([later(w_down)], ['gather'])]
    (mix_handle, up_handle, down_handle), token = _exchange_start_groups("gather_rest_start", gather_groups, token)
    CB = row(conv_b)

    g1, g2, g3 = row(norm_mix_g), row(norm_ffn_g), row(final_norm_g)
    gp, gs = row(out_norm_pool_g), row(out_norm_ssm_g)
    (xn,) = _rowmap("norm_mix", _rms_fwd, [xs], [g1 + token[0:1, 0:1]], [(D, BF16)])
    W_in = _exchange_wait("gather_in_wait", in_handle, xn)[0].reshape(D, D)
    halves = lambda r: (r[:, :DP], r[:, DP:])
    u, u_s5 = _mm_rows("proj_in", xn, W_in, 'nn', halves, [], [], [(DP, F32), (DS, F32)])
    tmp = _pick(T, (256, 128))
    pool_w_bf = pool_w.astype(BF16)
    ypn = _pool_fwd(u, pool_w_bf, row(pool_scale), gp, tmp)

    u_ssm = _to_chunk_rows(u_s5)
    u_ssm_bf = u_ssm.astype(BF16)
    a_rows = (ssm_log_neg_a_re.reshape(2 * G, N), ssm_a_im.reshape(2 * G, N), ssm_log_dt.reshape(2 * G, 1))
    disc = _ssm_params_fwd(*a_rows)
    nb = G // SSM_BLOCK_GROUPS
    vecs = [d.reshape(2, nb, 1, SSM_BLOCK_GROUPS * N) for d in disc]
    mats = [_groups_side_by_side(ssm_b_re, 3), _groups_side_by_side(ssm_b_im, 3),
            _groups_side_by_side(ssm_c_re, 2), _groups_side_by_side(ssm_c_im, 2)]
    y_dir = [_ssm_fwd("ssm_fwd_%d" % d, u_ssm_bf, mats, vecs, d) for d in range(2)]

    def mix_post(yf, yb, us, d, gw, gb, g):
        y = yf + yb + d * us
        z = _gelu(y)
        gate = _sigmoid(jnp.dot(z.astype(BF16), gw, preferred_element_type=F32) + gb)
        return _rms_fwd(z * gate, g), y

    mix_landed = _exchange_wait("gather_mix_wait", mix_handle, y_dir[1])
    W_glu = mix_landed[0].reshape(DS, DS)
    W_out = mix_landed[1].reshape(D, D)
    ysn, y_ssm = _rowmap("ssm_post", mix_post, [y_dir[0], y_dir[1], u_ssm], [row(ssm_d), W_glu, row(glu_b), gs],
                         [(DS, BF16), (DS, F32)])
    ycat = jnp.concatenate([ypn, _from_chunk_rows(ysn)], axis=1)
    h1, hn = _mm_rows("proj_out", ycat, W_out, 'nn', lambda h, g: (h, _rms_fwd(h, g)), [], [g2],
                      [(D, F32), (D, BF16)], add=xs)
    up_landed = _exchange_wait("gather_up_wait", up_handle, hn)
    W_up_t = up_landed[0].reshape(F2, D)
    CW = up_landed[1].transpose(1, 0, 2).reshape(3, F2)
    up = _mm("ffn_up", hn, W_up_t, 'nt', F32)
    act = _convact_fwd(up, CW, CB, F)
    W_down = _exchange_wait("gather_down_wait", down_handle, act)[0].reshape(F, D)

    def head(h, t, g):
        r = lax.rsqrt(_rowmean(h * h) + EPS)
        hh = h * r
        e = hh * g - t
        loss = 0.5 * jnp.sum(_rowmean(e * e), keepdims=True)
        dy = e * (1.0 / D)
        dxh = dy * g
        dh = r * (dxh - hh * _rowmean(dxh * hh))
        return dh, dh, jnp.broadcast_to(loss, (1, LANES)), _colsum(dy * hh)

    dh2, dh2_bf, loss_acc, dg3 = _mm_rows("ffn_down", act, W_down, 'nn', head, [tgt], [g3],
                                          [(D, F32), (D, BF16)], [(1, LANES), (1, D)], add=h1)

    dact = _mm("ffn_down_dx", dh2_bf, W_down, 'nt', F32)
    dW_down = _mm("ffn_down_dw", act, dh2_bf, 'tn', BF16)
    dup, dCW, dCB = _convact_bwd(up, dact, CW, CB, F)
    dhn_val = _mm("ffn_up_dx_val", dup, W_up_t, 'nn', F32, b_rows=(2, 0), a_plane=0)
    dW_up_t = _mm("ffn_up_dw", dup, hn, 'tn', BF16).reshape(F2, D)

    def shards(full_grad):
        return full_grad.reshape((N_DEV, full_grad.shape[0] // N_DEV) + full_grad.shape[1:])

    half = N_DEV // 2
    dCW_sh = jnp.concatenate([h.reshape(3, half, fs).transpose(1, 0, 2) for h in dCW], axis=0)
    ffn_sent = [shards(dW_down), shards(dW_up_t), dCW_sh]
    ffn_handle, token = _exchange_start("scatter_ffn_start", ffn_sent, ['scatter'] * 3, after=dhn_val)
    g2_late = g2 + token[0:1, 0:1]

    def norm_bwd_add(dy, hx, res, g):
        dx, dg = _rms_bwd(hx, g, dy)
        return dx + res, dg

    def norm_bwd_add2(dy, hx, res, g):
        dx, dg = _rms_bwd(hx, g, dy)
        return dx + res, dx + res, dg

    dh1, dh1_bf, dg2 = _mm_rows("ffn_up_dx_gate", dup, W_up_t, 'nn', norm_bwd_add2, [h1, dh2], [g2_late],
                                [(D, F32), (D, BF16)], [(1, D)], add=dhn_val, b_rows=(2, 1), a_plane=1)
    dycat, d_ysn = _mm_rows("proj_out_dx", dh1_bf, W_out, 'nt', halves, [], [], [(DP, F32), (DS, F32)])
    dW_out = _mm("proj_out_dw", ycat, dh1_bf, 'tn', BF16)

    out_sent = [shards(dW_out)]
    out_handle, token = _exchange_start("scatter_out_start", out_sent, ['scatter'], after=dycat)
    q, dpooled, dpool_w, dpool_scale, dgp = _pool_bwd_rows(u, dycat, pool_w_bf, row(pool_scale),
                                                           gp + token[0:1, 0:1], tmp)

    def mix_post_bwd(y, us, dyn, d, gw, gb, g):
        z = _gelu(y)
        gz = z.astype(BF16)
        sig = _sigmoid(jnp.dot(gz, gw, preferred_element_type=F32) + gb)
        dys, dg = _rms_bwd(z * sig, g, dyn)
        dgl = dys * z * sig * (1.0 - sig)
        dgl_bf = dgl.astype(BF16)
        dz = dys * sig + lax.dot_general(dgl_bf, gw, (((1,), (1,)), ((), ())), preferred_element_type=F32)
        dgw = lax.dot_general(gz, dgl_bf, (((0,), (0,)), ((), ())), preferred_element_type=F32)
        dyv = dz * _gelu_grad(y)
        return dyv, dyv * d, dgw, _colsum(dgl), _colsum(dyv * us), dg

    dyv, du_dir, dglu_w, dglu_b, dssm_d, dgs = _rowmap(
        "ssm_post_bwd", mix_post_bwd, [y_ssm, u_ssm, _to_chunk_rows(d_ysn)], [row(ssm_d), W_glu, row(glu_b), gs],
        [(DS, BF16), (DS, F32)], [(DS, DS), (1, DS), (1, DS), (1, DS)])

    du_ssm = du_dir
    dB, dC, dvec = [], [], []
    for d in range(2):
        res = _ssm_bwd("ssm_bwd_%d" % d, u_ssm_bf, dyv, du_ssm, mats, vecs, d, BF16 if d == 1 else F32)
        du_ssm = res[0]
        dB.append(res[1:3])
        dC.append(res[3:5])
        dvec.append(res[5:9])
    grad_b = [_groups_apart(jnp.stack([dB[0][k], dB[1][k]]), 3) for k in range(2)]
    grad_c = [_groups_apart(jnp.stack([dC[0][k], dC[1][k]]), 2) for k in range(2)]
    cots = [jnp.stack([dvec[0][k], dvec[1][k]]).reshape(2 * G, N) for k in range(4)]
    d_a_re, d_a_im, d_log_dt = _ssm_params_bwd(*a_rows, cots)

    rep_grads = {
        'pool_w': dpool_w, 'pool_scale': dpool_scale, 'ssm_log_neg_a_re': d_a_re, 'ssm_a_im': d_a_im,
        'ssm_log_dt': d_log_dt, 'ssm_b_re': grad_b[0], 'ssm_b_im': grad_b[1], 'ssm_c_re': grad_c[0], 'ssm_c_im': grad_c[1],
        'ssm_d': dssm_d, 'glu_b': dglu_b, 'out_norm_pool_g': dgp, 'out_norm_ssm_g': dgs, 'norm_ffn_g': dg2,
        'conv_b': dCB, 'final_norm_g': dg3}
    wide = ('pool_w', 'ssm_b_re', 'ssm_b_im', 'ssm_c_re', 'ssm_c_im')

    def narrow(n):
        return weights[n].ndim == 4 and weights[n].shape[-1] < LANES // 2

    def travel(n):
        g = rep_grads[n]
        if narrow(n):
            g = jnp.swapaxes(g, 2, 3)
        if n in wide:
            return g.reshape(-1, PACK_W if g.size % PACK_W == 0 else LANES).astype(BF16)
        return _view2d(g.reshape(weights[n].shape))

    small = tuple(n for n in REPLICATED if n != 'norm_mix_g')
    small_sent = [shards(dglu_w.astype(BF16))] + [travel(n) for n in small] + [loss_acc]
    small_handle, token = _exchange_start("reduce_small_start", small_sent, ['scatter'] + ['gather'] * (len(small) + 1),
                                          after=d_a_re)

    du = _pool_bwd_band(q, dpooled, _from_chunk_rows(du_ssm), token, tmp)
    dW_in = _mm("proj_in_dw", xn, du, 'tn', BF16)
    in_handle, token = _exchange_start("scatter_in_start", [shards(dW_in)], ['scatter'], after=dW_in)
    dx, dg1 = _mm_rows("proj_in_dx", du, W_in + token[0, 0].astype(BF16), 'nt', norm_bwd_add, [xs, dh1], [g1],
                       [(D, F32)], [(1, D)])
    rep_grads['norm_mix_g'] = dg1
    last_handle, token = _exchange_start("reduce_last_start", [travel('norm_mix_g')], ['gather'], after=dx)

    grads, delta, new_m, new_v = {}, {}, {}, {}

    def update(n, landed_n):
        shape = weights[n].shape
        if narrow(n):
            swapped = lambda a: _view2d(jnp.swapaxes(a, 2, 3))
            landed_n = landed_n.reshape((N_DEV,) + swapped(weights[n]).shape)
            res = _sum_adamw("adamw_" + n, landed_n, swapped(weights[n]), swapped(mom1[n]), swapped(mom2[n]))
            back = lambda r: jnp.swapaxes(r.reshape(shape[:2] + (shape[3], shape[2])), 2, 3)
            grads[n], delta[n], new_m[n], new_v[n] = [back(r) for r in res]
            return
        if n == 'w_up':
            res = _sum_adamw("adamw_" + n, landed_n, weights[n].T, mom1[n].T, mom2[n].T)
            grads[n], delta[n], new_m[n], new_v[n] = [r.T for r in res]
            return
        landed_n = landed_n.reshape((N_DEV,) + _view2d(weights[n]).shape)
        res = _sum_adamw("adamw_" + n, landed_n, _view2d(weights[n]), _view2d(mom1[n]), _view2d(mom2[n]))
        grads[n], delta[n], new_m[n], new_v[n] = [r.reshape(shape) for r in res]

    def update_many(name, named_landed):
        items = [(l.reshape((N_DEV,) + _view2d(weights[n]).shape), _view2d(weights[n]), _view2d(mom1[n]), _view2d(mom2[n]))
                 for n, l in named_landed]
        for (n, _), res in zip(named_landed, _sum_adamw_many(name, items)):
            grads[n], delta[n], new_m[n], new_v[n] = [r.reshape(weights[n].shape) for r in res]

    ffn_landed = _exchange_wait("scatter_ffn_wait", ffn_handle, token)
    for n, l in zip(('w_down', 'w_up'), ffn_landed):
        update(n, l)
    (out_landed,) = _exchange_wait("scatter_out_wait", out_handle, token)
    update('w_out', out_landed)
    small_landed = _exchange_wait("reduce_small_wait", small_handle, token)
    tiny = [('conv_w', ffn_landed[2])]
    for n, l in zip(('glu_w',) + small, small_landed):
        if weights[n].size <= ADAM_TINY:
            tiny.append((n, l))
        else:
            update(n, l)
    update_many("adamw_tiny", tiny)

    def loss_sum(parts_ref, o_ref):
        s = parts_ref[0]
        for q in range(1, N_DEV):
            s = s + parts_ref[q]
        o_ref[...] = s

    loss = pl.pallas_call(loss_sum, name="loss_sum", out_shape=jax.ShapeDtypeStruct((1, LANES), F32))(small_landed[-1])[0, 0]

    update('w_in', _exchange_wait("scatter_in_wait", in_handle, grads[small[-1]])[0])
    update('norm_mix_g', _exchange_wait("reduce_last_wait", last_handle, grads['w_in'])[0])

    return (loss, dx[None], *[grads[n] for n in WEIGHTS], *[delta[n] for n in WEIGHTS],
            *[new_m[n] for n in WEIGHTS], *[new_v[n] for n in WEIGHTS])
```

```python
import functools

import jax
import jax.numpy as jnp
import numpy as np
from jax import lax
from jax.experimental import pallas as pl
from jax.experimental.pallas import tpu as pltpu

F32 = jnp.float32
BF16 = jnp.bfloat16
MESH_ID = pl.DeviceIdType.MESH

N_DEV = 8
EPS = 1e-6
POOL_WINDOWS = (2, 4, 8, 16)
POOL_GROUP = 128
POOL_PAD = 64
SSM_GROUP = 16
SSM_STATE = 64
SSM_BLOCK_GROUPS = 8
N_CHUNK = 8
ROW_BLOCK = 512
SCAN_UNROLL = 8
GROUP_SHIFT = 4
STATE_SHIFT = 6
_NT = (((1,), (1,)), ((), ()))
_TN = (((0,), (0,)), ((), ()))
LANES = 128
PACK_W = 1024
VMEM_LIMIT = 56 * 1024 * 1024
MM_VMEM_BUDGET = 40 * 1024 * 1024
MM_TILE_CAP = 1408
MM_ROWS_TILES = (512, 1024)
ADAM_TILE_BYTES = 4 * 1024 * 1024
ADAM_TINY = 8192

ADAM_LR = 0.001
ADAM_B1 = 0.9
ADAM_B2 = 0.999
ADAM_EPS = 1e-08
ADAM_WD = 0.01
ADAM_STEP = 10

WEIGHTS = ['norm_mix_g', 'w_in', 'pool_w', 'pool_scale', 'ssm_log_neg_a_re', 'ssm_a_im', 'ssm_log_dt',
           'ssm_b_re', 'ssm_b_im', 'ssm_c_re', 'ssm_c_im', 'ssm_d', 'glu_w', 'glu_b', 'out_norm_pool_g',
           'out_norm_ssm_g', 'w_out', 'norm_ffn_g', 'w_up', 'conv_w', 'conv_b', 'w_down', 'final_norm_g']
SHARDED = ('w_in', 'w_out', 'w_down', 'glu_w', 'w_up', 'conv_w')
REPLICATED = tuple(n for n in WEIGHTS if n not in SHARDED)


def _pick(n, prefs):
    for p in prefs:
        if n % p == 0:
            return p
    return n


def _params(sem, vmem=None):
    return pltpu.CompilerParams(dimension_semantics=sem, vmem_limit_bytes=vmem or VMEM_LIMIT)


def _tiles(n, cap):
    return [d for d in range(LANES, min(n, cap) + 1, LANES) if n % d == 0] or [n]


def _mm(name, a, b, mode, out_dtype, add=None, b_rows=(1, 0), a_plane=None):
    batch = a.shape[0] if a.ndim == 3 and a_plane is None else None
    if mode == 'nn':
        (M, K), N = a.shape[-2:], b.shape[1]
        assert b.shape[0] == K * b_rows[0]
    elif mode == 'nt':
        (M, K), (N, _) = a.shape[-2:], b.shape
    else:
        (K, M), (_, N) = a.shape[-2:], b.shape
    dims = {'nn': (((1,), (0,)), ((), ())), 'nt': (((1,), (1,)), ((), ())), 'tn': (((0,), (0,)), ((), ()))}[mode]
    sa, sb, so = a.dtype.itemsize, b.dtype.itemsize, jnp.dtype(out_dtype).itemsize
    best = None
    for tm in _tiles(M, MM_TILE_CAP):
        for tn in _tiles(N, MM_TILE_CAP):
            need = 2 * (tm * K * sa + tn * K * sb + tm * tn * (so + (4 if add is not None else 0)))
            if need <= MM_VMEM_BUDGET:
                key = (tm * tn / (tm + tn), tm * tn)
                if best is None or key > best[0]:
                    best = (key, tm, tn)
    _, tm, tn = best
    rows_inner = a.size * sa * (N // tn) + b.size * sb < a.size * sa + b.size * sb * (M // tm)

    def kern(*refs):
        a_ref, b_ref = refs[:2]
        o_ref = refs[-1]
        r = lax.dot_general(a_ref[...].astype(BF16), b_ref[...].astype(BF16), dims, preferred_element_type=F32)
        if add is not None:
            r = r + refs[2][...]
        o_ref[...] = r.astype(o_ref.dtype)

    if rows_inner:
        grid = (N // tn, M // tm)
        ij = lambda g0, g1: (g1, g0)
    else:
        grid = (M // tm, N // tn)
        ij = lambda g0, g1: (g0, g1)

    def spec(block, index, batched=False):
        if batch is None:
            return pl.BlockSpec(block, lambda g0, g1: index(*ij(g0, g1)))
        if batched:
            return pl.BlockSpec((None,) + block, lambda p, g0, g1: (p,) + index(*ij(g0, g1)))
        return pl.BlockSpec(block, lambda p, g0, g1: index(*ij(g0, g1)))

    if a_plane is not None:
        assert mode != 'tn'
        a_spec = pl.BlockSpec((None, tm, K), lambda g0, g1: (a_plane, ij(g0, g1)[0], 0))
    else:
        a_spec = (spec((K, tm), lambda i, j: (0, i), True) if mode == 'tn' else spec((tm, K), lambda i, j: (i, 0), True))
    b_spec = (spec((tn, K), lambda i, j: (j, 0)) if mode == 'nt' else spec((K, tn), lambda i, j: (b_rows[1], j)))
    in_specs = [a_spec, b_spec]
    args = [a, b]
    if add is not None:
        in_specs.append(spec((tm, tn), lambda i, j: (i, j), True))
        args.append(add)
    lead = () if batch is None else (batch,)
    return pl.pallas_call(
        kern, name=name, grid=lead + grid, in_specs=in_specs,
        out_specs=spec((tm, tn), lambda i, j: (i, j), True),
        out_shape=jax.ShapeDtypeStruct(lead + (M, N), out_dtype),
        compiler_params=_params(("parallel",) * (len(lead) + 2)),
    )(*args)


def _store_rows_and_sums(i, outs, res, n_row):
    for o, v in zip(outs[:n_row], res[:n_row]):
        o[...] = v.astype(o.dtype)
    for o, v in zip(outs[n_row:], res[n_row:]):
        @pl.when(i == 0)
        def _(o=o, v=v):
            o[...] = v

        @pl.when(i != 0)
        def _(o=o, v=v):
            o[...] += v


def _mm_rows(name, a, b, mode, fn, rows, fulls, out_rows, out_accs=(), add=None, b_rows=(1, 0), a_plane=None):
    M, K = a.shape[-2:]
    N = b.shape[1] if mode == 'nn' else b.shape[0]
    assert b.shape[0] == (K * b_rows[0] if mode == 'nn' else N)
    dims = _NT if mode == 'nt' else (((1,), (0,)), ((), ()))
    n_extra, n_row = len(rows) + len(fulls), len(out_rows)
    row_bytes = (K * a.dtype.itemsize + (4 * N if add is not None else 0) + sum(4 * r.shape[1] for r in rows)
                 + sum(c * jnp.dtype(dt).itemsize for c, dt in out_rows))
    fits = [t for t in MM_ROWS_TILES if M % t == 0 and 2 * (t * row_bytes + K * N * b.dtype.itemsize) <= MM_VMEM_BUDGET]
    tm = max(fits) if fits else min(MM_ROWS_TILES)

    def kern(*refs):
        i = pl.program_id(0)
        r = lax.dot_general(refs[0][...].astype(BF16), refs[1][...].astype(BF16), dims, preferred_element_type=F32)
        pos = 2
        if add is not None:
            r = r + refs[pos][...]
            pos += 1
        res = fn(r, *[x[...] for x in refs[pos:pos + n_extra]])
        _store_rows_and_sums(i, refs[pos + n_extra:], res, n_row)

    def full_spec(shape):
        nd = len(shape)
        return pl.BlockSpec(tuple(shape), lambda i: (0,) * nd)

    a_spec = (pl.BlockSpec((tm, K), lambda i: (i, 0)) if a_plane is None
              else pl.BlockSpec((None, tm, K), lambda i: (a_plane, i, 0)))
    b_spec = pl.BlockSpec((K, N), lambda i: (b_rows[1], 0)) if mode == 'nn' else pl.BlockSpec((N, K), lambda i: (0, 0))
    ins = [_row_in(r, tm, M) for r in ([add] if add is not None else []) + list(rows)]
    outs = [_row_out(o, tm, M) for o in out_rows]
    res = pl.pallas_call(
        kern, name=name, grid=(M // tm,),
        in_specs=[a_spec, b_spec] + [s for s, _ in ins] + [full_spec(f.shape) for f in fulls],
        out_specs=[s for s, _ in outs] + [full_spec(s) for s in out_accs],
        out_shape=[sh for _, sh in outs] + [jax.ShapeDtypeStruct(tuple(s), F32) for s in out_accs],
        compiler_params=_params(("arbitrary",)),
    )(a, b, *[x for _, x in ins], *fulls)
    return res


def _row_in(r, tm, T):
    arr, w, k = r if isinstance(r, tuple) else (r, r.shape[1], 0)
    return pl.BlockSpec((tm, w), lambda i: (i, k)), arr


def _row_out(o, tm, T):
    c, dt = o
    return pl.BlockSpec((tm, c), lambda i: (i, 0)), jax.ShapeDtypeStruct((T, c), dt)


def _rowmap(name, fn, rows, fulls, out_rows, out_accs=(), tm=512):
    T = (rows[0][0] if isinstance(rows[0], tuple) else rows[0]).shape[0]
    tm = min(tm, T)
    assert T % tm == 0
    n_in, n_row = len(rows) + len(fulls), len(out_rows)

    def kern(*refs):
        i = pl.program_id(0)
        res = fn(*[r[...] for r in refs[:n_in]])
        res = res if isinstance(res, (tuple, list)) else (res,)
        _store_rows_and_sums(i, refs[n_in:], res, n_row)

    def full_spec(shape):
        nd = len(shape)
        return pl.BlockSpec(tuple(shape), lambda i: (0,) * nd)

    ins = [_row_in(r, tm, T) for r in rows]
    outs = [_row_out(o, tm, T) for o in out_rows]
    res = pl.pallas_call(
        kern, name=name, grid=(T // tm,),
        in_specs=[s for s, _ in ins] + [full_spec(f.shape) for f in fulls],
        out_specs=[s for s, _ in outs] + [full_spec(s) for s in out_accs],
        out_shape=[sh for _, sh in outs] + [jax.ShapeDtypeStruct(tuple(s), F32) for s in out_accs],
        compiler_params=_params(("arbitrary",)),
    )(*[x for _, x in ins], *fulls)
    return res


def _colsum(v):
    return jnp.sum(v, axis=0, keepdims=True)


def _rowmean(v):
    return jnp.mean(v, axis=-1, keepdims=True)


def _rms_fwd(x, g):
    r = lax.rsqrt(_rowmean(x * x) + EPS)
    return x * r * g


def _rms_bwd(x, g, dy):
    r = lax.rsqrt(_rowmean(x * x) + EPS)
    xh = x * r
    dxh = dy * g
    return r * (dxh - xh * _rowmean(dxh * xh)), _colsum(dy * xh)


def _sigmoid(v):
    return 0.5 * jnp.tanh(0.5 * v) + 0.5


def _gelu(y):
    c = np.sqrt(2.0 / np.pi).astype(np.float32)
    return 0.5 * y * (1.0 + jnp.tanh(c * (y + 0.044715 * (y * y * y))))


def _gelu_and_grad(y):
    c = np.sqrt(2.0 / np.pi).astype(np.float32)
    th = jnp.tanh(c * (y + 0.044715 * (y * y * y)))
    return (0.5 * y * (1.0 + th),
            0.5 * (1.0 + th) + 0.5 * y * (1.0 - th * th) * c * (1.0 + 3.0 * 0.044715 * (y * y)))


def _split3(v):
    hi = v.astype(BF16)
    r1 = v - hi.astype(F32)
    mid = r1.astype(BF16)
    lo = (r1 - mid.astype(F32)).astype(BF16)
    return hi, mid, lo


def _band(tm, lo_off, hi_off):
    shape = (tm, tm + 2 * POOL_PAD)
    d = lax.broadcasted_iota(jnp.int32, shape, 1) - lax.broadcasted_iota(jnp.int32, shape, 0) - POOL_PAD
    return ((d >= lo_off) & (d < hi_off)).astype(BF16)


def _window_sums(ext, tm, lo_off, hi_off):
    hi, mid, lo = _split3(ext)
    band = _band(LANES, lo_off, hi_off)
    dot = functools.partial(jnp.dot, preferred_element_type=F32)
    out = []
    for s in range(tm // LANES):
        rows = slice(s * LANES, (s + 2) * LANES)
        out.append(dot(band, hi[rows]) + dot(band, mid[rows]) + dot(band, lo[rows]))
    return jnp.concatenate(out, axis=0)


def _window_count(r0, tm, half, T):
    t = r0 + lax.broadcasted_iota(jnp.int32, (tm, 1), 0)
    return (jnp.minimum(t + half, T) - jnp.maximum(t - half, 0)).astype(F32)


def _halo_specs(tm, C, T):
    per = tm // POOL_PAD
    last = T // POOL_PAD - 1
    return [pl.BlockSpec((POOL_PAD, C), lambda i: (jnp.maximum(i * per - 1, 0), 0)),
            pl.BlockSpec((tm, C), lambda i: (i, 0)),
            pl.BlockSpec((POOL_PAD, C), lambda i: (jnp.minimum((i + 1) * per, last), 0))]


def _with_halo(before_ref, main_ref, after_ref, i, n):
    before = jnp.where(i > 0, before_ref[...], 0.0)
    after = jnp.where(i < n - 1, after_ref[...], 0.0)
    return jnp.concatenate([before, main_ref[...], after], axis=0)


def _pool_block(ext, ctr, w_ref, r0, tm, T):
    pooled, conc = [], []
    for gi, w in enumerate(POOL_WINDOWS):
        half = w // 2
        cols = slice(gi * POOL_GROUP, (gi + 1) * POOL_GROUP)
        ws = _window_sums(ext[:, cols], tm, -half, half)
        p = ws / _window_count(r0, tm, half, T) - ctr[:, cols]
        pooled.append(p)
        conc.append(jnp.dot(p.astype(BF16), w_ref[gi], preferred_element_type=F32))
    return pooled, jnp.concatenate(conc, axis=1)


def _pool_fwd(u, pool_w_bf, pool_scale, g_pool, tm):
    T = u.shape[0]
    nw = len(POOL_WINDOWS)
    C = nw * POOL_GROUP
    n = T // tm

    def kern(ub_ref, u_ref, ua_ref, w_ref, sc_ref, g_ref, o_ref):
        i = pl.program_id(0)
        r0 = pl.multiple_of(i * tm, tm)
        _, conc = _pool_block(_with_halo(ub_ref, u_ref, ua_ref, i, n), u_ref[...], w_ref, r0, tm, T)
        o_ref[...] = _rms_fwd(conc * sc_ref[...], g_ref[...]).astype(o_ref.dtype)

    return pl.pallas_call(
        kern, name="pool_fwd", grid=(n,),
        in_specs=_halo_specs(tm, C, T) + [pl.BlockSpec(pool_w_bf.shape, lambda i: (0, 0, 0)),
                                          pl.BlockSpec((1, C), lambda i: (0, 0)), pl.BlockSpec((1, C), lambda i: (0, 0))],
        out_specs=pl.BlockSpec((tm, C), lambda i: (i, 0)),
        out_shape=jax.ShapeDtypeStruct((T, C), BF16),
        compiler_params=_params(("arbitrary",)),
    )(u, u, u, pool_w_bf, pool_scale, g_pool)


def _pool_bwd_rows(u, d_y, pool_w_bf, pool_scale, g_pool, tm):
    T = u.shape[0]
    nw = len(POOL_WINDOWS)
    C = nw * POOL_GROUP
    n = T // tm

    def kern(ub_ref, u_ref, ua_ref, dy_ref, w_ref, sc_ref, g_ref, q_ref, dp_ref, dw_ref, dsc_ref, dg_ref):
        i = pl.program_id(0)
        r0 = pl.multiple_of(i * tm, tm)
        pooled, conc = _pool_block(_with_halo(ub_ref, u_ref, ua_ref, i, n), u_ref[...], w_ref, r0, tm, T)
        sc = sc_ref[...]
        dyp, dg = _rms_bwd(conc * sc, g_ref[...], dy_ref[...])
        dsc = _colsum(dyp * conc)
        dconc = (dyp * sc).astype(BF16)
        dws = []
        for gi, w in enumerate(POOL_WINDOWS):
            cols = slice(gi * POOL_GROUP, (gi + 1) * POOL_GROUP)
            dc = dconc[:, cols]
            dp = lax.dot_general(dc, w_ref[gi], (((1,), (1,)), ((), ())), preferred_element_type=F32)
            dp_ref[:, cols] = dp
            q_ref[:, cols] = dp / _window_count(r0, tm, w // 2, T)
            dws.append(lax.dot_general(pooled[gi].astype(BF16), dc, (((0,), (0,)), ((), ())), preferred_element_type=F32))

        @pl.when(i == 0)
        def _():
            for gi in range(nw):
                dw_ref[gi] = dws[gi]
            dsc_ref[...] = dsc
            dg_ref[...] = dg

        @pl.when(i != 0)
        def _():
            for gi in range(nw):
                dw_ref[gi] += dws[gi]
            dsc_ref[...] += dsc
            dg_ref[...] += dg

    full2 = pl.BlockSpec((1, C), lambda i: (0, 0))
    row_blk = pl.BlockSpec((tm, C), lambda i: (i, 0))
    return pl.pallas_call(
        kern, name="pool_bwd_rows", grid=(n,),
        in_specs=_halo_specs(tm, C, T) + [row_blk, pl.BlockSpec(pool_w_bf.shape, lambda i: (0, 0, 0)), full2, full2],
        out_specs=[row_blk, row_blk, pl.BlockSpec((nw, POOL_GROUP, POOL_GROUP), lambda i: (0, 0, 0)), full2, full2],
        out_shape=[jax.ShapeDtypeStruct((T, C), F32), jax.ShapeDtypeStruct((T, C), F32),
                   jax.ShapeDtypeStruct((nw, POOL_GROUP, POOL_GROUP), F32),
                   jax.ShapeDtypeStruct((1, C), F32), jax.ShapeDtypeStruct((1, C), F32)],
        compiler_params=_params(("arbitrary",)),
    )(u, u, u, d_y, pool_w_bf, pool_scale, g_pool)


def _pool_bwd_band(q, dpooled, du_other, after, tm):
    T, C = dpooled.shape
    C2 = du_other.shape[1]
    n = T // tm

    def kern(qb_ref, q_ref, qa_ref, dp_ref, other_ref, after_ref, o_ref):
        ext = _with_halo(qb_ref, q_ref, qa_ref, pl.program_id(0), n)
        for gi, w in enumerate(POOL_WINDOWS):
            half = w // 2
            cols = slice(gi * POOL_GROUP, (gi + 1) * POOL_GROUP)
            du = _window_sums(ext[:, cols], tm, -half + 1, half + 1) - dp_ref[:, cols]
            o_ref[:, cols] = du.astype(o_ref.dtype)
        o_ref[:, C:] = other_ref[...].astype(o_ref.dtype)

    return pl.pallas_call(
        kern, name="pool_bwd_band", grid=(n,),
        in_specs=_halo_specs(tm, C, T) + [pl.BlockSpec((tm, C), lambda i: (i, 0)), pl.BlockSpec((tm, C2), lambda i: (i, 0)),
                                          pl.BlockSpec(after.shape, lambda i: (0, 0))],
        out_specs=pl.BlockSpec((tm, C + C2), lambda i: (i, 0)),
        out_shape=jax.ShapeDtypeStruct((T, C + C2), BF16),
        compiler_params=_params(("arbitrary",)),
    )(q, q, q, dpooled, du_other, after)


def _to_chunk_rows(a):
    T, C = a.shape
    return a.reshape(N_CHUNK, T // N_CHUNK, C).transpose(1, 0, 2).reshape(T, C)


def _from_chunk_rows(a):
    T, C = a.shape
    return a.reshape(T // N_CHUNK, N_CHUNK, C).transpose(1, 0, 2).reshape(T, C)


def _cmul(ar, ai, br, bi):
    return ar * br - ai * bi, ar * bi + ai * br


def _ssm_discretise(log_neg_a_re, a_im, log_dt):
    dt = jnp.exp(log_dt)
    a_re = -jnp.exp(log_neg_a_re)
    mag = jnp.exp(a_re * dt)
    ang = a_im * dt
    lam_re, lam_im = mag * jnp.cos(ang), mag * jnp.sin(ang)
    den = a_re * a_re + a_im * a_im
    f_re = ((lam_re - 1.0) * a_re + lam_im * a_im) / den
    f_im = (lam_im * a_re - (lam_re - 1.0) * a_im) / den
    return lam_re, lam_im, f_re, f_im


def _ssm_params_fwd(log_neg_a_re, a_im, log_dt):
    rows, n = log_neg_a_re.shape

    def kern(a_ref, b_ref, c_ref, o1, o2, o3, o4):
        for o, v in zip((o1, o2, o3, o4), _ssm_discretise(a_ref[...], b_ref[...], c_ref[...])):
            o[...] = v

    return pl.pallas_call(kern, name="ssm_params_fwd", out_shape=[jax.ShapeDtypeStruct((rows, n), F32)] * 4)(
        log_neg_a_re, a_im, log_dt)


def _ssm_params_bwd(log_neg_a_re, a_im, log_dt, cots):
    rows, n = log_neg_a_re.shape

    def kern(a_ref, b_ref, c_ref, g1, g2, g3, g4, o1, o2, o3):
        _, vjp = jax.vjp(_ssm_discretise, a_ref[...], b_ref[...], c_ref[...])
        d1, d2, d3 = vjp((g1[...], g2[...], g3[...], g4[...]))
        o1[...] = d1
        o2[...] = d2
        o3[...] = d3

    return pl.pallas_call(
        kern, name="ssm_params_bwd",
        out_shape=[jax.ShapeDtypeStruct((rows, n), F32), jax.ShapeDtypeStruct((rows, n), F32),
                   jax.ShapeDtypeStruct((rows, 1), F32)])(log_neg_a_re, a_im, log_dt, *cots)


def _cpow(lr, li, n):
    out = None
    br, bi = lr, li
    while n:
        if n & 1:
            out = (br, bi) if out is None else _cmul(out[0], out[1], br, bi)
        n >>= 1
        if n:
            br, bi = _cmul(br, bi, br, bi)
    return out


def _slab(t):
    return pl.ds(pl.multiple_of(t * N_CHUNK, N_CHUNK), N_CHUNK)


def _steps(n, step, carry):
    main = n // SCAN_UNROLL

    def body(i, c):
        for r in range(SCAN_UNROLL):
            c = step(i * SCAN_UNROLL + r, c)
        return c

    carry = lax.fori_loop(0, main, body, carry)
    for k in range(main * SCAN_UNROLL, n):
        carry = step(jnp.int32(k), carry)
    return carry


def _shift_chunks(vr, vi, reverse):
    sub = lax.broadcasted_iota(jnp.int32, vr.shape, 0)
    if reverse:
        keep = sub != N_CHUNK - 1
        return jnp.where(keep, pltpu.roll(vr, N_CHUNK - 1, 0), 0.0), jnp.where(keep, pltpu.roll(vi, N_CHUNK - 1, 0), 0.0)
    keep = sub != 0
    return jnp.where(keep, pltpu.roll(vr, 1, 0), 0.0), jnp.where(keep, pltpu.roll(vi, 1, 0), 0.0)


def _chunk_scan(xr, xi, lr, li, Lc, reverse, prev=None):
    C = xr.shape[1]
    lrb, lib = jnp.broadcast_to(lr, (N_CHUNK, C)), jnp.broadcast_to(li, (N_CHUNK, C))
    zero = jnp.zeros((N_CHUNK, C), F32)

    def step_of(k):
        return (Lc - 1 - k) if reverse else k

    def advance(sr, si, t):
        return lrb * sr - lib * si + xr[_slab(t), :], lrb * si + lib * sr + xi[_slab(t), :]

    def pass1(k, c):
        return advance(c[0], c[1], step_of(k))

    er, ei = _steps(Lc, pass1, (zero, zero))
    pr, pi = _cpow(lr, li, Lc)
    prb, pib = jnp.broadcast_to(pr, (N_CHUNK, C)), jnp.broadcast_to(pi, (N_CHUNK, C))
    cr, ci = zero, zero
    for _ in range(N_CHUNK - 1):
        cr, ci = _shift_chunks(er + prb * cr - pib * ci, ei + prb * ci + pib * cr, reverse)

    if prev is None:
        def pass2(k, c):
            t = step_of(k)
            nr, ni = advance(c[0], c[1], t)
            xr[_slab(t), :] = nr
            xi[_slab(t), :] = ni
            return nr, ni

        _steps(Lc, pass2, (cr, ci))
        return None

    qr, qi = prev
    def pair(ar, ai, wr, wi, acc):
        return acc[0] + ar * wr + ai * wi, acc[1] + ai * wr - ar * wi

    def pass2(k, c):
        sr, si, accr, acci = c
        t = step_of(k)
        nr, ni = advance(sr, si, t)
        xr[_slab(t), :] = nr
        xi[_slab(t), :] = ni
        tp = t - 1 if reverse else t + 1
        accr, acci = pair(nr, ni, qr[_slab(tp), :], qi[_slab(tp), :], (accr, acci))
        return nr, ni, accr, acci

    sr, si, accr, acci = _steps(Lc - 1, pass2, (cr, ci, zero, zero))
    t = step_of(Lc - 1)
    nr, ni = advance(sr, si, t)
    xr[_slab(t), :] = nr
    xi[_slab(t), :] = ni
    edge = step_of(0)
    wr, wi = _shift_chunks(qr[_slab(edge), :], qi[_slab(edge), :], not reverse)
    return pair(nr, ni, wr, wi, (accr, acci))


def _ssm_dims(T):
    assert T % ROW_BLOCK == 0
    return T // N_CHUNK, T // ROW_BLOCK


def _group_mask():
    shape = (SSM_BLOCK_GROUPS * SSM_GROUP, SSM_BLOCK_GROUPS * SSM_STATE)
    rows = lax.broadcasted_iota(jnp.int32, shape, 0) >> GROUP_SHIFT
    cols = lax.broadcasted_iota(jnp.int32, shape, 1) >> STATE_SHIFT
    return rows == cols


def _expand_groups(compact):
    return jnp.where(_group_mask(), jnp.tile(compact, (SSM_BLOCK_GROUPS, 1)), 0.0).astype(BF16)


def _collect_groups(block):
    kept = jnp.where(_group_mask(), block, 0.0)
    out = kept[0:SSM_GROUP]
    for g in range(1, SSM_BLOCK_GROUPS):
        out = out + kept[g * SSM_GROUP:(g + 1) * SSM_GROUP]
    return out


def _ssm_specs(T, direction):
    SC = SSM_BLOCK_GROUPS * SSM_STATE
    col = pl.BlockSpec((T, LANES), lambda m: (0, m))
    mat = pl.BlockSpec((None, None, SSM_GROUP, SC), lambda m: (direction, m, 0, 0))
    vec = pl.BlockSpec((None, None, 1, SC), lambda m: (direction, m, 0, 0))
    return col, mat, vec


def _ssm_project_in(u_ref, bre, bim, xr, xi, nblk):
    for blk in range(nblk):
        rows = pl.ds(blk * ROW_BLOCK, ROW_BLOCK)
        ub = u_ref[rows, :].astype(BF16)
        xr[rows, :] = jnp.dot(ub, bre, preferred_element_type=F32)
        xi[rows, :] = jnp.dot(ub, bim, preferred_element_type=F32)


def _ssm_fwd(name, u_bf, mats, vecs, direction):
    T, W = u_bf.shape
    SC = SSM_BLOCK_GROUPS * SSM_STATE
    nb = W // LANES
    Lc, nblk = _ssm_dims(T)
    reverse = bool(direction)

    def kern(u_ref, bre_ref, bim_ref, cre_ref, cim_ref, lr_ref, li_ref, fr_ref, fi_ref, y_ref, xr, xi):
        _ssm_project_in(u_ref, _expand_groups(bre_ref[...]), _expand_groups(bim_ref[...]), xr, xi, nblk)
        _chunk_scan(xr, xi, lr_ref[...], li_ref[...], Lc, reverse)
        fr, fi = fr_ref[...], fi_ref[...]
        cre, cim = _expand_groups(cre_ref[...]), _expand_groups(cim_ref[...])
        for blk in range(nblk):
            rows = pl.ds(blk * ROW_BLOCK, ROW_BLOCK)
            sr, si = _cmul(fr, fi, xr[rows, :], xi[rows, :])
            y_ref[rows, :] = (lax.dot_general(sr.astype(BF16), cre, _NT, preferred_element_type=F32)
                              - lax.dot_general(si.astype(BF16), cim, _NT, preferred_element_type=F32))

    col, mat, vec = _ssm_specs(T, direction)
    return pl.pallas_call(
        kern, name=name, grid=(nb,),
        in_specs=[col, mat, mat, mat, mat, vec, vec, vec, vec],
        out_specs=col, out_shape=jax.ShapeDtypeStruct((T, W), F32),
        scratch_shapes=[pltpu.VMEM((T, SC), F32), pltpu.VMEM((T, SC), F32)],
        compiler_params=_params(("arbitrary",)),
    )(u_bf, *mats, *vecs)


def _ssm_bwd(name, u_bf, dy_bf, du_in, mats, vecs, direction, du_dtype):
    T, W = u_bf.shape
    SC = SSM_BLOCK_GROUPS * SSM_STATE
    nb = W // LANES
    Lc, nblk = _ssm_dims(T)
    reverse = bool(direction)

    def kern(u_ref, dy_ref, dui_ref, bre_ref, bim_ref, cre_ref, cim_ref, lr_ref, li_ref, fr_ref, fi_ref,
             du_ref, dbre_ref, dbim_ref, dcre_ref, dcim_ref, dlr_ref, dli_ref, dfr_ref, dfi_ref,
             xr, xi, gr, gi):
        lr, li, fr, fi = lr_ref[...], li_ref[...], fr_ref[...], fi_ref[...]
        bre, bim = _expand_groups(bre_ref[...]), _expand_groups(bim_ref[...])
        cre, cim = _expand_groups(cre_ref[...]), _expand_groups(cim_ref[...])
        _ssm_project_in(u_ref, bre, bim, xr, xi, nblk)
        _chunk_scan(xr, xi, lr, li, Lc, reverse)
        mre = jnp.zeros((LANES, SC), F32)
        mim = jnp.zeros((LANES, SC), F32)
        dfr = jnp.zeros((1, SC), F32)
        dfi = jnp.zeros((1, SC), F32)
        for blk in range(nblk):
            rows = pl.ds(blk * ROW_BLOCK, ROW_BLOCK)
            dy = dy_ref[rows, :].astype(BF16)
            x_r, x_i = xr[rows, :], xi[rows, :]
            g_r = jnp.dot(dy, cre, preferred_element_type=F32)
            g_i = -jnp.dot(dy, cim, preferred_element_type=F32)
            mre += lax.dot_general(dy, x_r.astype(BF16), _TN, preferred_element_type=F32)
            mim += lax.dot_general(dy, x_i.astype(BF16), _TN, preferred_element_type=F32)
            dfr += _colsum(g_r * x_r + g_i * x_i)
            dfi += _colsum(g_i * x_r - g_r * x_i)
            gr[rows, :] = fr * g_r + fi * g_i
            gi[rows, :] = fr * g_i - fi * g_r
        dcre_ref[...] = _collect_groups(fr * mre - fi * mim)
        dcim_ref[...] = _collect_groups(-(fr * mim + fi * mre))
        dfr_ref[...] = dfr
        dfi_ref[...] = dfi
        accr, acci = _chunk_scan(gr, gi, lr, -li, Lc, not reverse, prev=(xr, xi))
        dlr_ref[...] = _colsum(accr)
        dli_ref[...] = _colsum(acci)
        dbre = jnp.zeros((LANES, SC), F32)
        dbim = jnp.zeros((LANES, SC), F32)
        for blk in range(nblk):
            rows = pl.ds(blk * ROW_BLOCK, ROW_BLOCK)
            ub = u_ref[rows, :].astype(BF16)
            a_r, a_i = gr[rows, :].astype(BF16), gi[rows, :].astype(BF16)
            dbre += lax.dot_general(ub, a_r, _TN, preferred_element_type=F32)
            dbim += lax.dot_general(ub, a_i, _TN, preferred_element_type=F32)
            du_ref[rows, :] = (dui_ref[rows, :] + lax.dot_general(a_r, bre, _NT, preferred_element_type=F32)
                               + lax.dot_general(a_i, bim, _NT, preferred_element_type=F32)).astype(du_ref.dtype)
        dbre_ref[...] = _collect_groups(dbre)
        dbim_ref[...] = _collect_groups(dbim)

    col, mat, vec = _ssm_specs(T, direction)
    mat_out = pl.BlockSpec((None, SSM_GROUP, SC), lambda m: (m, 0, 0))
    vec_out = pl.BlockSpec((None, 1, SC), lambda m: (m, 0, 0))
    mat_shape = jax.ShapeDtypeStruct((nb, SSM_GROUP, SC), F32)
    vec_shape = jax.ShapeDtypeStruct((nb, 1, SC), F32)
    return pl.pallas_call(
        kern, name=name, grid=(nb,),
        in_specs=[col, col, col, mat, mat, mat, mat, vec, vec, vec, vec],
        out_specs=[col, mat_out, mat_out, mat_out, mat_out, vec_out, vec_out, vec_out, vec_out],
        out_shape=[jax.ShapeDtypeStruct((T, W), du_dtype), mat_shape, mat_shape, mat_shape, mat_shape,
                   vec_shape, vec_shape, vec_shape, vec_shape],
        scratch_shapes=[pltpu.VMEM((T, SC), F32)] * 4,
        compiler_params=_params(("arbitrary",)),
    )(u_bf, dy_bf, du_in, *mats, *vecs)


def _groups_side_by_side(p, channel_axis):
    d, G = p.shape[:2]
    nb = G // SSM_BLOCK_GROUPS
    v = p.reshape(d, nb, SSM_BLOCK_GROUPS, p.shape[2], p.shape[3])
    v = v.transpose(0, 1, 4, 2, 3) if channel_axis == 3 else v.transpose(0, 1, 3, 2, 4)
    return v.reshape(d, nb, SSM_GROUP, SSM_BLOCK_GROUPS * SSM_STATE)


def _groups_apart(m, channel_axis):
    d, nb = m.shape[:2]
    v = m.reshape(d, nb, SSM_GROUP, SSM_BLOCK_GROUPS, SSM_STATE)
    v = v.transpose(0, 1, 3, 4, 2) if channel_axis == 3 else v.transpose(0, 1, 3, 2, 4)
    return v.reshape((d, nb * SSM_BLOCK_GROUPS) + v.shape[3:])


def _conv_taps(ref, c, R, T):
    n = T // R
    r0 = pl.multiple_of(c * R, R)
    main = ref[pl.ds(r0, R), :]
    before = ref[pl.ds(pl.multiple_of(jnp.maximum(r0 - 8, 0), 8), 8), :]
    after = ref[pl.ds(pl.multiple_of(jnp.minimum(r0 + R, T - 8), 8), 8), :]
    ext = jnp.concatenate([jnp.where(c > 0, before, 0.0), main, jnp.where(c < n - 1, after, 0.0)], axis=0)
    m1 = pltpu.roll(ext, 1, 0)[8:8 + R]
    p1 = pltpu.roll(ext, R + 15, 0)[8:8 + R]
    return m1, main, p1, r0


def _conv_specs(T, F, cb):
    nj = F // cb
    lo = lambda rows: pl.BlockSpec((rows, cb), lambda j: (0, j))
    hi = lambda rows: pl.BlockSpec((rows, cb), lambda j: (0, j + nj))
    return nj, lo, hi


def _convact_fwd(up, conv_w, conv_b, F):
    T = up.shape[0]
    cb = _pick(F, (256, 128))
    R = _pick(T, (256, 128, 64))
    nj, lo, hi = _conv_specs(T, F, cb)

    def kern(uv, ug, wv, wg, bv, bg, o_ref):
        def body(c, carry):
            v1, v0, v2, r0 = _conv_taps(uv, c, R, T)
            g1, g0, g2, _ = _conv_taps(ug, c, R, T)
            val = v1 * wv[0:1, :] + v0 * wv[1:2, :] + v2 * wv[2:3, :] + bv[...]
            gate = g1 * wg[0:1, :] + g0 * wg[1:2, :] + g2 * wg[2:3, :] + bg[...]
            half = 0.5 * gate
            o_ref[pl.ds(r0, R), :] = (val * (half * jnp.tanh(half) + half)).astype(o_ref.dtype)
            return carry

        lax.fori_loop(0, T // R, body, 0)

    return pl.pallas_call(
        kern, name="convact_fwd", grid=(nj,),
        in_specs=[lo(T), hi(T), lo(3), hi(3), lo(1), hi(1)],
        out_specs=lo(T), out_shape=jax.ShapeDtypeStruct((T, F), BF16),
        compiler_params=_params(("arbitrary",)),
    )(up, up, conv_w, conv_w, conv_b, conv_b)


def _convact_bwd(up, dact, conv_w, conv_b, F):
    T = up.shape[0]
    cb = _pick(F, (256, 128))
    R = _pick(T, (256, 128, 64))
    nj, lo, hi = _conv_specs(T, F, cb)

    def kern(uv, ug, da, wv, wg, bv, bg, dup, dwv, dwg, dbv, dbg, sv, sg):
        zero = jnp.zeros((1, cb), F32)

        def pass_a(c, acc):
            v1, v0, v2, r0 = _conv_taps(uv, c, R, T)
            g1, g0, g2, _ = _conv_taps(ug, c, R, T)
            val = v1 * wv[0:1, :] + v0 * wv[1:2, :] + v2 * wv[2:3, :] + bv[...]
            gate = g1 * wg[0:1, :] + g0 * wg[1:2, :] + g2 * wg[2:3, :] + bg[...]
            sig = _sigmoid(gate)
            d = da[pl.ds(r0, R), :]
            silu = gate * sig
            dval = d * silu
            dgate = d * val * (sig + silu - silu * sig)
            sv[pl.ds(r0, R), :] = dval
            sg[pl.ds(r0, R), :] = dgate
            terms = (dval * v1, dval * v0, dval * v2, dval, dgate * g1, dgate * g0, dgate * g2, dgate)
            return tuple(a + _colsum(t) for a, t in zip(acc, terms))

        acc = lax.fori_loop(0, T // R, pass_a, (zero,) * 8)
        for k in range(3):
            dwv[k:k + 1, :] = acc[k]
            dwg[k:k + 1, :] = acc[4 + k]
        dbv[...] = acc[3]
        dbg[...] = acc[7]

        def pass_b(c, carry):
            v1, v0, v2, r0 = _conv_taps(sv, c, R, T)
            g1, g0, g2, _ = _conv_taps(sg, c, R, T)
            dup[0, pl.ds(r0, R), :] = (v2 * wv[0:1, :] + v0 * wv[1:2, :] + v1 * wv[2:3, :]).astype(dup.dtype)
            dup[1, pl.ds(r0, R), :] = (g2 * wg[0:1, :] + g0 * wg[1:2, :] + g1 * wg[2:3, :]).astype(dup.dtype)
            return carry

        lax.fori_loop(0, T // R, pass_b, 0)

    dup, dwv, dwg, dbv, dbg = pl.pallas_call(
        kern, name="convact_bwd", grid=(nj,),
        in_specs=[lo(T), hi(T), lo(T), lo(3), hi(3), lo(1), hi(1)],
        out_specs=[pl.BlockSpec((2, T, cb), lambda j: (0, 0, j)), lo(3), lo(3), lo(1), lo(1)],
        out_shape=[jax.ShapeDtypeStruct((2, T, F), BF16),
                   jax.ShapeDtypeStruct((3, F), F32), jax.ShapeDtypeStruct((3, F), F32),
                   jax.ShapeDtypeStruct((1, F), F32), jax.ShapeDtypeStruct((1, F), F32)],
        scratch_shapes=[pltpu.VMEM((T, cb), F32), pltpu.VMEM((T, cb), F32)],
        compiler_params=_params(("arbitrary",)),
    )(up, up, dact, conv_w, conv_w, conv_b, conv_b)
    return dup, (dwv, dwg), jnp.concatenate([dbv, dbg], axis=1)


def _peers(x, y, c, with_self=False):
    out = []
    for k in range(0 if with_self else 1, N_DEV):
        px = 1 - x if k & 4 else x
        py = 1 - y if k & 2 else y
        pc = 1 - c if k & 1 else c
        out.append((k, (px, py, pc), 4 * px + 2 * py + pc))
    return out


def _exchange_start(name, bufs, modes, after):
    handles, token = _exchange_start_groups(name, [(bufs, modes)], after)
    return handles[0], token


def _exchange_start_groups(name, groups, after):
    bufs = [b for g, _ in groups for b in g]
    modes = [m for _, ms in groups for m in ms]
    first = [int(v) for v in np.cumsum([0] + [len(g) for g, _ in groups])]
    n, ng = len(bufs), len(groups)
    lands = [lax.empty((N_DEV,) + tuple(b.shape[-2:]), b.dtype) for b in bufs]

    def body(*refs):
        ins, land_in = refs[:n], refs[n:2 * n]
        sems = refs[2 * n + 1:2 * n + 1 + 2 * ng]
        token = refs[-1]
        x, y, c = lax.axis_index("x"), lax.axis_index("y"), lax.axis_index("c")
        me = 4 * x + 2 * y + c
        for g in range(ng):
            for b in range(first[g], first[g + 1]):
                for k, peer, slot in _peers(x, y, c, with_self=True):
                    sem = (b - first[g]) * N_DEV + k
                    pltpu.make_async_remote_copy(
                        src_ref=ins[b] if modes[b] == 'gather' else ins[b].at[slot], dst_ref=land_in[b].at[me],
                        send_sem=sems[2 * g].at[sem], recv_sem=sems[2 * g + 1].at[sem],
                        device_id=peer, device_id_type=MESH_ID).start()
        token[...] = jnp.zeros_like(token)

    hbm = pl.BlockSpec(memory_space=pltpu.HBM)
    sem_spec = pl.BlockSpec(memory_space=pltpu.SEMAPHORE)
    operands = [pltpu.with_memory_space_constraint(a, pltpu.HBM) for a in bufs + lands]
    sem_shapes = [pltpu.SemaphoreType.DMA((len(g) * N_DEV,)) for g, _ in groups for _ in range(2)]
    res = pl.pallas_call(
        body, name=name,
        out_shape=(*sem_shapes, *[pltpu.HBM(a.shape, a.dtype) for a in operands], jax.ShapeDtypeStruct((8, LANES), F32)),
        in_specs=[hbm] * (2 * n) + [pl.BlockSpec(memory_space=pl.ANY)],
        out_specs=(*([sem_spec] * (2 * ng)), *([hbm] * (2 * n)), pl.BlockSpec(memory_space=pltpu.VMEM)),
        input_output_aliases={i: 2 * ng + i for i in range(2 * n)},
        compiler_params=pltpu.CompilerParams(has_side_effects=pltpu.SideEffectType.DATAFLOW_SIDE_EFFECTING),
    )(*operands, after)
    srcs, landed = res[2 * ng:2 * ng + n], res[2 * ng + n:2 * ng + 2 * n]
    handles = [(res[2 * g], res[2 * g + 1], list(srcs[first[g]:first[g + 1]]), list(landed[first[g]:first[g + 1]]),
                tuple(modes[first[g]:first[g + 1]])) for g in range(ng)]
    return handles, res[-1]


def _exchange_wait(name, handle, after):
    send_sems, recv_sems, srcs, lands, modes = handle
    n = len(srcs)

    def body(*refs):
        src_in, land_in = refs[:n], refs[n:2 * n]
        send_ref, recv_ref = refs[2 * n], refs[2 * n + 1]
        x, y, c = lax.axis_index("x"), lax.axis_index("y"), lax.axis_index("c")
        for b in range(n):
            for k, peer, slot in _peers(x, y, c, with_self=True):
                sem = b * N_DEV + k
                copy = pltpu.make_async_remote_copy(
                    src_ref=src_in[b] if modes[b] == 'gather' else src_in[b].at[slot], dst_ref=land_in[b].at[slot],
                    send_sem=send_ref.at[sem], recv_sem=recv_ref.at[sem],
                    device_id=peer, device_id_type=MESH_ID)
                copy.wait_send()
                copy.wait_recv()

    hbm = pl.BlockSpec(memory_space=pltpu.HBM)
    sem_spec = pl.BlockSpec(memory_space=pltpu.SEMAPHORE)
    res = pl.pallas_call(
        body, name=name,
        out_shape=tuple(pltpu.HBM(a.shape, a.dtype) for a in srcs + lands),
        in_specs=[hbm] * (2 * n) + [sem_spec, sem_spec, pl.BlockSpec(memory_space=pl.ANY)],
        out_specs=tuple([hbm] * (2 * n)),
        input_output_aliases={i: i for i in range(2 * n)},
        compiler_params=pltpu.CompilerParams(has_side_effects=pltpu.SideEffectType.DATAFLOW_SIDE_EFFECTING),
    )(*srcs, *lands, send_sems, recv_sems, after)
    return list(res[n:])


def _adamw(w, g, m, v):
    m2 = ADAM_B1 * m + (1.0 - ADAM_B1) * g
    v2 = ADAM_B2 * v + (1.0 - ADAM_B2) * (g * g)
    m_hat = m2 / (1.0 - ADAM_B1 ** ADAM_STEP)
    v_hat = v2 / (1.0 - ADAM_B2 ** ADAM_STEP)
    return -ADAM_LR * (m_hat / (jnp.sqrt(v_hat) + ADAM_EPS) + ADAM_WD * w), m2, v2


def _sum_adamw(name, landed, w, m, v):
    R, C = w.shape
    row_bytes = N_DEV * (-(-C // LANES) * LANES) * landed.dtype.itemsize
    tiles = [d for d in range(16, R, 16) if R % d == 0 and d * row_bytes <= ADAM_TILE_BYTES]
    tr = max(tiles) if tiles and R * row_bytes > ADAM_TILE_BYTES else R

    def kern(x_ref, w_ref, m_ref, v_ref, g_out, d_out, m_out, v_out):
        g = x_ref[0].astype(F32)
        for q in range(1, N_DEV):
            g = g + x_ref[q].astype(F32)
        g_out[...] = g
        d_out[...], m_out[...], v_out[...] = _adamw(w_ref[...], g, m_ref[...], v_ref[...])

    blk = pl.BlockSpec((tr, C), lambda i: (i, 0))
    return pl.pallas_call(
        kern, name=name, grid=(R // tr,),
        in_specs=[pl.BlockSpec((N_DEV, tr, C), lambda i: (0, i, 0)), blk, blk, blk],
        out_specs=[blk] * 4, out_shape=[jax.ShapeDtypeStruct((R, C), F32)] * 4,
        compiler_params=_params(("arbitrary",)),
    )(landed, w, m, v)


def _sum_adamw_many(name, items):
    n = len(items)

    def kern(*refs):
        ins, outs = refs[:4 * n], refs[4 * n:]
        for k in range(n):
            x_ref, w_ref, m_ref, v_ref = ins[4 * k:4 * k + 4]
            g = x_ref[0].astype(F32)
            for q in range(1, N_DEV):
                g = g + x_ref[q].astype(F32)
            outs[4 * k][...] = g
            outs[4 * k + 1][...], outs[4 * k + 2][...], outs[4 * k + 3][...] = _adamw(w_ref[...], g, m_ref[...], v_ref[...])

    out_shape = [jax.ShapeDtypeStruct(w.shape, F32) for _, w, _, _ in items for _ in range(4)]
    res = pl.pallas_call(kern, name=name, out_shape=out_shape)(*[a for item in items for a in item])
    return [res[4 * k:4 * k + 4] for k in range(n)]


def _view2d(a):
    if a.ndim == 1:
        return a.reshape(1, -1)
    return a.reshape(-1, a.shape[-1])


def kernel(x, norm_mix_g, w_in, pool_w, pool_scale, ssm_log_neg_a_re, ssm_a_im, ssm_log_dt, ssm_b_re, ssm_b_im, ssm_c_re, ssm_c_im, ssm_d, glu_w, glu_b, out_norm_pool_g, out_norm_ssm_g, w_out, norm_ffn_g, w_up, conv_w, conv_b, w_down, final_norm_g, loss_target, m_norm_mix_g, m_w_in, m_pool_w, m_pool_scale, m_ssm_log_neg_a_re, m_ssm_a_im, m_ssm_log_dt, m_ssm_b_re, m_ssm_b_im, m_ssm_c_re, m_ssm_c_im, m_ssm_d, m_glu_w, m_glu_b, m_out_norm_pool_g, m_out_norm_ssm_g, m_w_out, m_norm_ffn_g, m_w_up, m_conv_w, m_conv_b, m_w_down, m_final_norm_g, v_norm_mix_g, v_w_in, v_pool_w, v_pool_scale, v_ssm_log_neg_a_re, v_ssm_a_im, v_ssm_log_dt, v_ssm_b_re, v_ssm_b_im, v_ssm_c_re, v_ssm_c_im, v_ssm_d, v_glu_w, v_glu_b, v_out_norm_pool_g, v_out_norm_ssm_g, v_w_out, v_norm_ffn_g, v_w_up, v_conv_w, v_conv_b, v_w_down, v_final_norm_g):
    given = dict(locals())
    weights = {n: given[n] for n in WEIGHTS}
    mom1 = {n: given["m_" + n] for n in WEIGHTS}
    mom2 = {n: given["v_" + n] for n in WEIGHTS}

    xs = x[0]
    tgt = loss_target[0]
    T, D = xs.shape
    DP = len(POOL_WINDOWS) * POOL_GROUP
    DS = D - DP
    G = DS // SSM_GROUP
    N, H = SSM_STATE, SSM_GROUP
    F2 = w_up.shape[1] * N_DEV
    F = F2 // 2
    fs = w_up.shape[1]
    row = lambda a: a.reshape(1, -1)

    in_handle, token = _exchange_start("gather_in_start", [w_in.astype(BF16)], ['gather'], after=w_in)
    later = lambda w: (w + token[0, 0]).astype(BF16)
    gather_groups = [([later(glu_w), later(w_out)], ['gather'] * 2),
                     ([later(w_up.T), conv_w], ['gather'] * 2), ([later(w_down)], ['gather'])]
    (mix_handle, up_handle, down_handle), token = _exchange_start_groups("gather_rest_start", gather_groups, token)
    CB = row(conv_b)

    g1, g2, g3 = row(norm_mix_g), row(norm_ffn_g), row(final_norm_g)
    gp, gs = row(out_norm_pool_g), row(out_norm_ssm_g)
    (xn,) = _rowmap("norm_mix", _rms_fwd, [xs], [g1 + token[0:1, 0:1]], [(D, BF16)])
    W_in = _exchange_wait("gather_in_wait", in_handle, xn)[0].reshape(D, D)
    halves = lambda r: (r[:, :DP], r[:, DP:])
    u, u_s5 = _mm_rows("proj_in", xn, W_in, 'nn', halves, [], [], [(DP, F32), (DS, F32)])
    tmp = _pick(T, (256, 128))
    pool_w_bf = pool_w.astype(BF16)
    ypn = _pool_fwd(u, pool_w_bf, row(pool_scale), gp, tmp)

    u_ssm = _to_chunk_rows(u_s5)
    u_ssm_bf = u_ssm.astype(BF16)
    a_rows = (ssm_log_neg_a_re.reshape(2 * G, N), ssm_a_im.reshape(2 * G, N), ssm_log_dt.reshape(2 * G, 1))
    disc = _ssm_params_fwd(*a_rows)
    nb = G // SSM_BLOCK_GROUPS
    vecs = [d.reshape(2, nb, 1, SSM_BLOCK_GROUPS * N) for d in disc]
    mats = [_groups_side_by_side(ssm_b_re, 3), _groups_side_by_side(ssm_b_im, 3),
            _groups_side_by_side(ssm_c_re, 2), _groups_side_by_side(ssm_c_im, 2)]
    y_dir = [_ssm_fwd("ssm_fwd_%d" % d, u_ssm_bf, mats, vecs, d) for d in range(2)]

    def mix_post(yf, yb, us, d, gw, gb, g):
        y = yf + yb + d * us
        z = _gelu(y)
        gate = _sigmoid(jnp.dot(z.astype(BF16), gw, preferred_element_type=F32) + gb)
        return _rms_fwd(z * gate, g), y

    mix_landed = _exchange_wait("gather_mix_wait", mix_handle, y_dir[1])
    W_glu = mix_landed[0].reshape(DS, DS)
    W_out = mix_landed[1].reshape(D, D)
    ysn, y_ssm = _rowmap("ssm_post", mix_post, [y_dir[0], y_dir[1], u_ssm], [row(ssm_d), W_glu, row(glu_b), gs],
                         [(DS, BF16), (DS, F32)])
    ycat = jnp.concatenate([ypn, _from_chunk_rows(ysn)], axis=1)
    h1, hn = _mm_rows("proj_out", ycat, W_out, 'nn', lambda h, g: (h, _rms_fwd(h, g)), [], [g2],
                      [(D, F32), (D, BF16)], add=xs)
    up_landed = _exchange_wait("gather_up_wait", up_handle, hn)
    W_up_t = up_landed[0].reshape(F2, D)
    CW = up_landed[1].transpose(1, 0, 2).reshape(3, F2)
    up = _mm("ffn_up", hn, W_up_t, 'nt', F32)
    act = _convact_fwd(up, CW, CB, F)
    W_down = _exchange_wait("gather_down_wait", down_handle, act)[0].reshape(F, D)

    def head(h, t, g):
        r = lax.rsqrt(_rowmean(h * h) + EPS)
        hh = h * r
        e = hh * g - t
        loss = 0.5 * jnp.sum(_rowmean(e * e), keepdims=True)
        dy = e * (1.0 / D)
        dxh = dy * g
        dh = r * (dxh - hh * _rowmean(dxh * hh))
        return dh, dh, jnp.broadcast_to(loss, (1, LANES)), _colsum(dy * hh)

    dh2, dh2_bf, loss_acc, dg3 = _mm_rows("ffn_down", act, W_down, 'nn', head, [tgt], [g3],
                                          [(D, F32), (D, BF16)], [(1, LANES), (1, D)], add=h1)

    dact = _mm("ffn_down_dx", dh2_bf, W_down, 'nt', F32)
    dW_down = _mm("ffn_down_dw", act, dh2_bf, 'tn', BF16)
    dup, dCW, dCB = _convact_bwd(up, dact, CW, CB, F)
    dhn_val = _mm("ffn_up_dx_val", dup, W_up_t, 'nn', F32, b_rows=(2, 0), a_plane=0)
    dW_up_t = _mm("ffn_up_dw", dup, hn, 'tn', BF16).reshape(F2, D)

    def shards(full_grad):
        return full_grad.reshape((N_DEV, full_grad.shape[0] // N_DEV) + full_grad.shape[1:])

    half = N_DEV // 2
    dCW_sh = jnp.concatenate([h.reshape(3, half, fs).transpose(1, 0, 2) for h in dCW], axis=0)
    ffn_sent = [shards(dW_down), shards(dW_up_t), dCW_sh]
    ffn_handle, token = _exchange_start("scatter_ffn_start", ffn_sent, ['scatter'] * 3, after=dhn_val)
    g2_late = g2 + token[0:1, 0:1]

    def norm_bwd_add(dy, hx, res, g):
        dx, dg = _rms_bwd(hx, g, dy)
        return dx + res, dg

    def norm_bwd_add2(dy, hx, res, g):
        dx, dg = _rms_bwd(hx, g, dy)
        return dx + res, dx + res, dg

    dh1, dh1_bf, dg2 = _mm_rows("ffn_up_dx_gate", dup, W_up_t, 'nn', norm_bwd_add2, [h1, dh2], [g2_late],
                                [(D, F32), (D, BF16)], [(1, D)], add=dhn_val, b_rows=(2, 1), a_plane=1)
    dycat, d_ysn = _mm_rows("proj_out_dx", dh1_bf, W_out, 'nt', halves, [], [], [(DP, F32), (DS, F32)])
    dW_out = _mm("proj_out_dw", ycat, dh1_bf, 'tn', BF16)

    out_sent = [shards(dW_out)]
    out_handle, token = _exchange_start("scatter_out_start", out_sent, ['scatter'], after=dycat)
    q, dpooled, dpool_w, dpool_scale, dgp = _pool_bwd_rows(u, dycat, pool_w_bf, row(pool_scale),
                                                           gp + token[0:1, 0:1], tmp)

    def mix_post_bwd(y, us, dyn, d, gw, gb, g):
        z, dz_dy = _gelu_and_grad(y)
        gz = z.astype(BF16)
        sig = _sigmoid(jnp.dot(gz, gw, preferred_element_type=F32) + gb)
        dys, dg = _rms_bwd(z * sig, g, dyn)
        dgl = dys * z * sig * (1.0 - sig)
        dgl_bf = dgl.astype(BF16)
        dz = dys * sig + lax.dot_general(dgl_bf, gw, (((1,), (1,)), ((), ())), preferred_element_type=F32)
        dgw = lax.dot_general(gz, dgl_bf, (((0,), (0,)), ((), ())), preferred_element_type=F32)
        dyv = dz * dz_dy
        return dyv, dyv * d, dgw, _colsum(dgl), _colsum(dyv * us), dg

    dyv, du_dir, dglu_w, dglu_b, dssm_d, dgs = _rowmap(
        "ssm_post_bwd", mix_post_bwd, [y_ssm, u_ssm, _to_chunk_rows(d_ysn)], [row(ssm_d), W_glu, row(glu_b), gs],
        [(DS, BF16), (DS, F32)], [(DS, DS), (1, DS), (1, DS), (1, DS)])

    du_ssm = du_dir
    dB, dC, dvec = [], [], []
    for d in range(2):
        res = _ssm_bwd("ssm_bwd_%d" % d, u_ssm_bf, dyv, du_ssm, mats, vecs, d, BF16 if d == 1 else F32)
        du_ssm = res[0]
        dB.append(res[1:3])
        dC.append(res[3:5])
        dvec.append(res[5:9])
    grad_b = [_groups_apart(jnp.stack([dB[0][k], dB[1][k]]), 3) for k in range(2)]
    grad_c = [_groups_apart(jnp.stack([dC[0][k], dC[1][k]]), 2) for k in range(2)]
    cots = [jnp.stack([dvec[0][k], dvec[1][k]]).reshape(2 * G, N) for k in range(4)]
    d_a_re, d_a_im, d_log_dt = _ssm_params_bwd(*a_rows, cots)

    rep_grads = {
        'pool_w': dpool_w, 'pool_scale': dpool_scale, 'ssm_log_neg_a_re': d_a_re, 'ssm_a_im': d_a_im,
        'ssm_log_dt': d_log_dt, 'ssm_b_re': grad_b[0], 'ssm_b_im': grad_b[1], 'ssm_c_re': grad_c[0], 'ssm_c_im': grad_c[1],
        'ssm_d': dssm_d, 'glu_b': dglu_b, 'out_norm_pool_g': dgp, 'out_norm_ssm_g': dgs, 'norm_ffn_g': dg2,
        'conv_b': dCB, 'final_norm_g': dg3}
    wide = ('pool_w', 'ssm_b_re', 'ssm_b_im', 'ssm_c_re', 'ssm_c_im')

    def narrow(n):
        return weights[n].ndim == 4 and weights[n].shape[-1] < LANES // 2

    def travel(n):
        g = rep_grads[n]
        if narrow(n):
            g = jnp.swapaxes(g, 2, 3)
        if n in wide:
            return g.reshape(-1, PACK_W if g.size % PACK_W == 0 else LANES).astype(BF16)
        return _view2d(g.reshape(weights[n].shape))

    small = tuple(n for n in REPLICATED if n != 'norm_mix_g')
    small_sent = [shards(dglu_w.astype(BF16))] + [travel(n) for n in small] + [loss_acc]
    small_handle, token = _exchange_start("reduce_small_start", small_sent, ['scatter'] + ['gather'] * (len(small) + 1),
                                          after=d_a_re)

    du = _pool_bwd_band(q, dpooled, _from_chunk_rows(du_ssm), token, tmp)
    dW_in = _mm("proj_in_dw", xn, du, 'tn', BF16)
    in_handle, token = _exchange_start("scatter_in_start", [shards(dW_in)], ['scatter'], after=dW_in)
    dx, dg1 = _mm_rows("proj_in_dx", du, W_in + token[0, 0].astype(BF16), 'nt', norm_bwd_add, [xs, dh1], [g1],
                       [(D, F32)], [(1, D)])
    rep_grads['norm_mix_g'] = dg1
    last_handle, token = _exchange_start("reduce_last_start", [travel('norm_mix_g')], ['gather'], after=dx)

    grads, delta, new_m, new_v = {}, {}, {}, {}

    def update(n, landed_n):
        shape = weights[n].shape
        if narrow(n):
            swapped = lambda a: _view2d(jnp.swapaxes(a, 2, 3))
            landed_n = landed_n.reshape((N_DEV,) + swapped(weights[n]).shape)
            res = _sum_adamw("adamw_" + n, landed_n, swapped(weights[n]), swapped(mom1[n]), swapped(mom2[n]))
            back = lambda r: jnp.swapaxes(r.reshape(shape[:2] + (shape[3], shape[2])), 2, 3)
            grads[n], delta[n], new_m[n], new_v[n] = [back(r) for r in res]
            return
        if n == 'w_up':
            res = _sum_adamw("adamw_" + n, landed_n, weights[n].T, mom1[n].T, mom2[n].T)
            grads[n], delta[n], new_m[n], new_v[n] = [r.T for r in res]
            return
        landed_n = landed_n.reshape((N_DEV,) + _view2d(weights[n]).shape)
        res = _sum_adamw("adamw_" + n, landed_n, _view2d(weights[n]), _view2d(mom1[n]), _view2d(mom2[n]))
        grads[n], delta[n], new_m[n], new_v[n] = [r.reshape(shape) for r in res]

    def update_many(name, named_landed):
        items = [(l.reshape((N_DEV,) + _view2d(weights[n]).shape), _view2d(weights[n]), _view2d(mom1[n]), _view2d(mom2[n]))
                 for n, l in named_landed]
        for (n, _), res in zip(named_landed, _sum_adamw_many(name, items)):
            grads[n], delta[n], new_m[n], new_v[n] = [r.reshape(weights[n].shape) for r in res]

    ffn_landed = _exchange_wait("scatter_ffn_wait", ffn_handle, token)
    for n, l in zip(('w_down', 'w_up'), ffn_landed):
        update(n, l)
    (out_landed,) = _exchange_wait("scatter_out_wait", out_handle, token)
    update('w_out', out_landed)
    small_landed = _exchange_wait("reduce_small_wait", small_handle, token)
    tiny = [('conv_w', ffn_landed[2])]
    for n, l in zip(('glu_w',) + small, small_landed):
        if weights[n].size <= ADAM_TINY:
            tiny.append((n, l))
        else:
            update(n, l)
    update_many("adamw_tiny", tiny)

    def loss_sum(parts_ref, o_ref):
        s = parts_ref[0]
        for q in range(1, N_DEV):
            s = s + parts_ref[q]
        o_ref[...] = s

    loss = pl.pallas_call(loss_sum, name="loss_sum", out_shape=jax.ShapeDtypeStruct((1, LANES), F32))(small_landed[-1])[0, 0]

    update('w_in', _exchange_wait("scatter_in_wait", in_handle, grads[small[-1]])[0])
    update('norm_mix_g', _exchange_wait("reduce_last_wait", last_handle, grads['w_in'])[0])

    return (loss, dx[None], *[grads[n] for n in WEIGHTS], *[delta[n] for n in WEIGHTS],
            *[new_m[n] for n in WEIGHTS], *[new_v[n] for n in WEIGHTS])
```

```python
import functools

import jax
import jax.numpy as jnp
import numpy as np
from jax import lax
from jax.experimental import pallas as pl
from jax.experimental.pallas import tpu as pltpu

F32 = jnp.float32
BF16 = jnp.bfloat16
MESH_ID = pl.DeviceIdType.MESH

N_DEV = 8
EPS = 1e-6
POOL_WINDOWS = (2, 4, 8, 16)
POOL_GROUP = 128
POOL_PAD = 64
SSM_GROUP = 16
SSM_STATE = 64
SSM_BLOCK_GROUPS = 8
N_CHUNK = 8
ROW_BLOCK = 512
SCAN_UNROLL = 8
GROUP_SHIFT = 4
STATE_SHIFT = 6
_NT = (((1,), (1,)), ((), ()))
_TN = (((0,), (0,)), ((), ()))
LANES = 128
PACK_W = 1024
VMEM_LIMIT = 56 * 1024 * 1024
MM_VMEM_BUDGET = 40 * 1024 * 1024
MM_TILE_CAP = 1408
MM_ROWS_TILES = (512, 1024)
ADAM_TILE_BYTES = 4 * 1024 * 1024
ADAM_TINY = 8192

ADAM_LR = 0.001
ADAM_B1 = 0.9
ADAM_B2 = 0.999
ADAM_EPS = 1e-08
ADAM_WD = 0.01
ADAM_STEP = 10

WEIGHTS = ['norm_mix_g', 'w_in', 'pool_w', 'pool_scale', 'ssm_log_neg_a_re', 'ssm_a_im', 'ssm_log_dt',
           'ssm_b_re', 'ssm_b_im', 'ssm_c_re', 'ssm_c_im', 'ssm_d', 'glu_w', 'glu_b', 'out_norm_pool_g',
           'out_norm_ssm_g', 'w_out', 'norm_ffn_g', 'w_up', 'conv_w', 'conv_b', 'w_down', 'final_norm_g']
SHARDED = ('w_in', 'w_out', 'w_down', 'glu_w', 'w_up', 'conv_w')
REPLICATED = tuple(n for n in WEIGHTS if n not in SHARDED)


def _pick(n, prefs):
    for p in prefs:
        if n % p == 0:
            return p
    return n


def _params(sem, vmem=None):
    return pltpu.CompilerParams(dimension_semantics=sem, vmem_limit_bytes=vmem or VMEM_LIMIT)


def _tiles(n, cap):
    return [d for d in range(LANES, min(n, cap) + 1, LANES) if n % d == 0] or [n]


def _mm(name, a, b, mode, out_dtype, add=None, b_rows=(1, 0), a_plane=None):
    batch = a.shape[0] if a.ndim == 3 and a_plane is None else None
    if mode == 'nn':
        (M, K), N = a.shape[-2:], b.shape[1]
        assert b.shape[0] == K * b_rows[0]
    elif mode == 'nt':
        (M, K), (N, _) = a.shape[-2:], b.shape
    else:
        (K, M), (_, N) = a.shape[-2:], b.shape
    dims = {'nn': (((1,), (0,)), ((), ())), 'nt': (((1,), (1,)), ((), ())), 'tn': (((0,), (0,)), ((), ()))}[mode]
    sa, sb, so = a.dtype.itemsize, b.dtype.itemsize, jnp.dtype(out_dtype).itemsize
    best = None
    for tm in _tiles(M, MM_TILE_CAP):
        for tn in _tiles(N, MM_TILE_CAP):
            need = 2 * (tm * K * sa + tn * K * sb + tm * tn * (so + (4 if add is not None else 0)))
            if need <= MM_VMEM_BUDGET:
                key = (tm * tn / (tm + tn), tm * tn)
                if best is None or key > best[0]:
                    best = (key, tm, tn)
    _, tm, tn = best
    rows_inner = a.size * sa * (N // tn) + b.size * sb < a.size * sa + b.size * sb * (M // tm)

    def kern(*refs):
        a_ref, b_ref = refs[:2]
        o_ref = refs[-1]
        r = lax.dot_general(a_ref[...].astype(BF16), b_ref[...].astype(BF16), dims, preferred_element_type=F32)
        if add is not None:
            r = r + refs[2][...]
        o_ref[...] = r.astype(o_ref.dtype)

    if rows_inner:
        grid = (N // tn, M // tm)
        ij = lambda g0, g1: (g1, g0)
    else:
        grid = (M // tm, N // tn)
        ij = lambda g0, g1: (g0, g1)

    def spec(block, index, batched=False):
        if batch is None:
            return pl.BlockSpec(block, lambda g0, g1: index(*ij(g0, g1)))
        if batched:
            return pl.BlockSpec((None,) + block, lambda p, g0, g1: (p,) + index(*ij(g0, g1)))
        return pl.BlockSpec(block, lambda p, g0, g1: index(*ij(g0, g1)))

    if a_plane is not None:
        assert mode != 'tn'
        a_spec = pl.BlockSpec((None, tm, K), lambda g0, g1: (a_plane, ij(g0, g1)[0], 0))
    else:
        a_spec = (spec((K, tm), lambda i, j: (0, i), True) if mode == 'tn' else spec((tm, K), lambda i, j: (i, 0), True))
    b_spec = (spec((tn, K), lambda i, j: (j, 0)) if mode == 'nt' else spec((K, tn), lambda i, j: (b_rows[1], j)))
    in_specs = [a_spec, b_spec]
    args = [a, b]
    if add is not None:
        in_specs.append(spec((tm, tn), lambda i, j: (i, j), True))
        args.append(add)
    lead = () if batch is None else (batch,)
    return pl.pallas_call(
        kern, name=name, grid=lead + grid, in_specs=in_specs,
        out_specs=spec((tm, tn), lambda i, j: (i, j), True),
        out_shape=jax.ShapeDtypeStruct(lead + (M, N), out_dtype),
        compiler_params=_params(("parallel",) * (len(lead) + 2)),
    )(*args)


def _store_rows_and_sums(i, outs, res, n_row):
    for o, v in zip(outs[:n_row], res[:n_row]):
        o[...] = v.astype(o.dtype)
    for o, v in zip(outs[n_row:], res[n_row:]):
        @pl.when(i == 0)
        def _(o=o, v=v):
            o[...] = v

        @pl.when(i != 0)
        def _(o=o, v=v):
            o[...] += v


def _mm_rows(name, a, b, mode, fn, rows, fulls, out_rows, out_accs=(), add=None, b_rows=(1, 0), a_plane=None):
    M, K = a.shape[-2:]
    N = b.shape[1] if mode == 'nn' else b.shape[0]
    assert b.shape[0] == (K * b_rows[0] if mode == 'nn' else N)
    dims = _NT if mode == 'nt' else (((1,), (0,)), ((), ()))
    n_extra, n_row = len(rows) + len(fulls), len(out_rows)
    row_bytes = (K * a.dtype.itemsize + (4 * N if add is not None else 0) + sum(4 * r.shape[1] for r in rows)
                 + sum(c * jnp.dtype(dt).itemsize for c, dt in out_rows))
    fits = [t for t in MM_ROWS_TILES if M % t == 0 and 2 * (t * row_bytes + K * N * b.dtype.itemsize) <= MM_VMEM_BUDGET]
    tm = max(fits) if fits else min(MM_ROWS_TILES)

    def kern(*refs):
        i = pl.program_id(0)
        r = lax.dot_general(refs[0][...].astype(BF16), refs[1][...].astype(BF16), dims, preferred_element_type=F32)
        pos = 2
        if add is not None:
            r = r + refs[pos][...]
            pos += 1
        res = fn(r, *[x[...] for x in refs[pos:pos + n_extra]])
        _store_rows_and_sums(i, refs[pos + n_extra:], res, n_row)

    def full_spec(shape):
        nd = len(shape)
        return pl.BlockSpec(tuple(shape), lambda i: (0,) * nd)

    a_spec = (pl.BlockSpec((tm, K), lambda i: (i, 0)) if a_plane is None
              else pl.BlockSpec((None, tm, K), lambda i: (a_plane, i, 0)))
    b_spec = pl.BlockSpec((K, N), lambda i: (b_rows[1], 0)) if mode == 'nn' else pl.BlockSpec((N, K), lambda i: (0, 0))
    ins = [_row_in(r, tm, M) for r in ([add] if add is not None else []) + list(rows)]
    outs = [_row_out(o, tm, M) for o in out_rows]
    res = pl.pallas_call(
        kern, name=name, grid=(M // tm,),
        in_specs=[a_spec, b_spec] + [s for s, _ in ins] + [full_spec(f.shape) for f in fulls],
        out_specs=[s for s, _ in outs] + [full_spec(s) for s in out_accs],
        out_shape=[sh for _, sh in outs] + [jax.ShapeDtypeStruct(tuple(s), F32) for s in out_accs],
        compiler_params=_params(("arbitrary",)),
    )(a, b, *[x for _, x in ins], *fulls)
    return res


def _row_in(r, tm, T):
    arr, w, k = r if isinstance(r, tuple) else (r, r.shape[1], 0)
    return pl.BlockSpec((tm, w), lambda i: (i, k)), arr


def _row_out(o, tm, T):
    c, dt = o
    return pl.BlockSpec((tm, c), lambda i: (i, 0)), jax.ShapeDtypeStruct((T, c), dt)


def _rowmap(name, fn, rows, fulls, out_rows, out_accs=(), tm=512):
    T = (rows[0][0] if isinstance(rows[0], tuple) else rows[0]).shape[0]
    tm = min(tm, T)
    assert T % tm == 0
    n_in, n_row = len(rows) + len(fulls), len(out_rows)

    def kern(*refs):
        i = pl.program_id(0)
        res = fn(*[r[...] for r in refs[:n_in]])
        res = res if isinstance(res, (tuple, list)) else (res,)
        _store_rows_and_sums(i, refs[n_in:], res, n_row)

    def full_spec(shape):
        nd = len(shape)
        return pl.BlockSpec(tuple(shape), lambda i: (0,) * nd)

    ins = [_row_in(r, tm, T) for r in rows]
    outs = [_row_out(o, tm, T) for o in out_rows]
    res = pl.pallas_call(
        kern, name=name, grid=(T // tm,),
        in_specs=[s for s, _ in ins] + [full_spec(f.shape) for f in fulls],
        out_specs=[s for s, _ in outs] + [full_spec(s) for s in out_accs],
        out_shape=[sh for _, sh in outs] + [jax.ShapeDtypeStruct(tuple(s), F32) for s in out_accs],
        compiler_params=_params(("arbitrary",)),
    )(*[x for _, x in ins], *fulls)
    return res


def _colsum(v):
    return jnp.sum(v, axis=0, keepdims=True)


def _rowmean(v):
    return jnp.mean(v, axis=-1, keepdims=True)


def _rms_fwd(x, g):
    r = lax.rsqrt(_rowmean(x * x) + EPS)
    return x * r * g


def _rms_bwd(x, g, dy):
    r = lax.rsqrt(_rowmean(x * x) + EPS)
    xh = x * r
    dxh = dy * g
    return r * (dxh - xh * _rowmean(dxh * xh)), _colsum(dy * xh)


def _sigmoid(v):
    return 0.5 * jnp.tanh(0.5 * v) + 0.5


def _gelu(y):
    c = np.sqrt(2.0 / np.pi).astype(np.float32)
    return 0.5 * y * (1.0 + jnp.tanh(c * (y + 0.044715 * (y * y * y))))


def _gelu_and_grad(y):
    c = np.sqrt(2.0 / np.pi).astype(np.float32)
    th = jnp.tanh(c * (y + 0.044715 * (y * y * y)))
    return (0.5 * y * (1.0 + th),
            0.5 * (1.0 + th) + 0.5 * y * (1.0 - th * th) * c * (1.0 + 3.0 * 0.044715 * (y * y)))


def _split3(v):
    hi = v.astype(BF16)
    r1 = v - hi.astype(F32)
    mid = r1.astype(BF16)
    lo = (r1 - mid.astype(F32)).astype(BF16)
    return hi, mid, lo


def _band(tm, lo_off, hi_off):
    shape = (tm, tm + 2 * POOL_PAD)
    d = lax.broadcasted_iota(jnp.int32, shape, 1) - lax.broadcasted_iota(jnp.int32, shape, 0) - POOL_PAD
    return ((d >= lo_off) & (d < hi_off)).astype(BF16)


def _window_sums(ext, tm, lo_off, hi_off):
    hi, mid, lo = _split3(ext)
    band = _band(LANES, lo_off, hi_off)
    dot = functools.partial(jnp.dot, preferred_element_type=F32)
    out = []
    for s in range(tm // LANES):
        rows = slice(s * LANES, (s + 2) * LANES)
        out.append(dot(band, hi[rows]) + dot(band, mid[rows]) + dot(band, lo[rows]))
    return jnp.concatenate(out, axis=0)


def _window_count(r0, tm, half, T):
    t = r0 + lax.broadcasted_iota(jnp.int32, (tm, 1), 0)
    return (jnp.minimum(t + half, T) - jnp.maximum(t - half, 0)).astype(F32)


def _halo_specs(tm, C, T):
    per = tm // POOL_PAD
    last = T // POOL_PAD - 1
    return [pl.BlockSpec((POOL_PAD, C), lambda i: (jnp.maximum(i * per - 1, 0), 0)),
            pl.BlockSpec((tm, C), lambda i: (i, 0)),
            pl.BlockSpec((POOL_PAD, C), lambda i: (jnp.minimum((i + 1) * per, last), 0))]


def _with_halo(before_ref, main_ref, after_ref, i, n):
    before = jnp.where(i > 0, before_ref[...], 0.0)
    after = jnp.where(i < n - 1, after_ref[...], 0.0)
    return jnp.concatenate([before, main_ref[...], after], axis=0)


def _pool_block(ext, ctr, w_ref, r0, tm, T):
    pooled, conc = [], []
    for gi, w in enumerate(POOL_WINDOWS):
        half = w // 2
        cols = slice(gi * POOL_GROUP, (gi + 1) * POOL_GROUP)
        ws = _window_sums(ext[:, cols], tm, -half, half)
        p = ws / _window_count(r0, tm, half, T) - ctr[:, cols]
        pooled.append(p)
        conc.append(jnp.dot(p.astype(BF16), w_ref[gi], preferred_element_type=F32))
    return pooled, jnp.concatenate(conc, axis=1)


def _pool_fwd(u, pool_w_bf, pool_scale, g_pool, tm):
    T = u.shape[0]
    nw = len(POOL_WINDOWS)
    C = nw * POOL_GROUP
    n = T // tm

    def kern(ub_ref, u_ref, ua_ref, w_ref, sc_ref, g_ref, o_ref):
        i = pl.program_id(0)
        r0 = pl.multiple_of(i * tm, tm)
        _, conc = _pool_block(_with_halo(ub_ref, u_ref, ua_ref, i, n), u_ref[...], w_ref, r0, tm, T)
        o_ref[...] = _rms_fwd(conc * sc_ref[...], g_ref[...]).astype(o_ref.dtype)

    return pl.pallas_call(
        kern, name="pool_fwd", grid=(n,),
        in_specs=_halo_specs(tm, C, T) + [pl.BlockSpec(pool_w_bf.shape, lambda i: (0, 0, 0)),
                                          pl.BlockSpec((1, C), lambda i: (0, 0)), pl.BlockSpec((1, C), lambda i: (0, 0))],
        out_specs=pl.BlockSpec((tm, C), lambda i: (i, 0)),
        out_shape=jax.ShapeDtypeStruct((T, C), BF16),
        compiler_params=_params(("arbitrary",)),
    )(u, u, u, pool_w_bf, pool_scale, g_pool)


def _pool_bwd_rows(u, d_y, pool_w_bf, pool_scale, g_pool, tm):
    T = u.shape[0]
    nw = len(POOL_WINDOWS)
    C = nw * POOL_GROUP
    n = T // tm

    def kern(ub_ref, u_ref, ua_ref, dy_ref, w_ref, sc_ref, g_ref, q_ref, dp_ref, dw_ref, dsc_ref, dg_ref):
        i = pl.program_id(0)
        r0 = pl.multiple_of(i * tm, tm)
        pooled, conc = _pool_block(_with_halo(ub_ref, u_ref, ua_ref, i, n), u_ref[...], w_ref, r0, tm, T)
        sc = sc_ref[...]
        dyp, dg = _rms_bwd(conc * sc, g_ref[...], dy_ref[...])
        dsc = _colsum(dyp * conc)
        dconc = (dyp * sc).astype(BF16)
        dws = []
        for gi, w in enumerate(POOL_WINDOWS):
            cols = slice(gi * POOL_GROUP, (gi + 1) * POOL_GROUP)
            dc = dconc[:, cols]
            dp = lax.dot_general(dc, w_ref[gi], (((1,), (1,)), ((), ())), preferred_element_type=F32)
            dp_ref[:, cols] = dp
            q_ref[:, cols] = dp / _window_count(r0, tm, w // 2, T)
            dws.append(lax.dot_general(pooled[gi].astype(BF16), dc, (((0,), (0,)), ((), ())), preferred_element_type=F32))

        @pl.when(i == 0)
        def _():
            for gi in range(nw):
                dw_ref[gi] = dws[gi]
            dsc_ref[...] = dsc
            dg_ref[...] = dg

        @pl.when(i != 0)
        def _():
            for gi in range(nw):
                dw_ref[gi] += dws[gi]
            dsc_ref[...] += dsc
            dg_ref[...] += dg

    full2 = pl.BlockSpec((1, C), lambda i: (0, 0))
    row_blk = pl.BlockSpec((tm, C), lambda i: (i, 0))
    return pl.pallas_call(
        kern, name="pool_bwd_rows", grid=(n,),
        in_specs=_halo_specs(tm, C, T) + [row_blk, pl.BlockSpec(pool_w_bf.shape, lambda i: (0, 0, 0)), full2, full2],
        out_specs=[row_blk, row_blk, pl.BlockSpec((nw, POOL_GROUP, POOL_GROUP), lambda i: (0, 0, 0)), full2, full2],
        out_shape=[jax.ShapeDtypeStruct((T, C), F32), jax.ShapeDtypeStruct((T, C), F32),
                   jax.ShapeDtypeStruct((nw, POOL_GROUP, POOL_GROUP), F32),
                   jax.ShapeDtypeStruct((1, C), F32), jax.ShapeDtypeStruct((1, C), F32)],
        compiler_params=_params(("arbitrary",)),
    )(u, u, u, d_y, pool_w_bf, pool_scale, g_pool)


def _pool_bwd_band(q, dpooled, du_other, after, tm):
    T, C = dpooled.shape
    C2 = du_other.shape[1]
    n = T // tm

    def kern(qb_ref, q_ref, qa_ref, dp_ref, other_ref, after_ref, o_ref):
        ext = _with_halo(qb_ref, q_ref, qa_ref, pl.program_id(0), n)
        for gi, w in enumerate(POOL_WINDOWS):
            half = w // 2
            cols = slice(gi * POOL_GROUP, (gi + 1) * POOL_GROUP)
            du = _window_sums(ext[:, cols], tm, -half + 1, half + 1) - dp_ref[:, cols]
            o_ref[:, cols] = du.astype(o_ref.dtype)
        o_ref[:, C:] = other_ref[...].astype(o_ref.dtype)

    return pl.pallas_call(
        kern, name="pool_bwd_band", grid=(n,),
        in_specs=_halo_specs(tm, C, T) + [pl.BlockSpec((tm, C), lambda i: (i, 0)), pl.BlockSpec((tm, C2), lambda i: (i, 0)),
                                          pl.BlockSpec(after.shape, lambda i: (0, 0))],
        out_specs=pl.BlockSpec((tm, C + C2), lambda i: (i, 0)),
        out_shape=jax.ShapeDtypeStruct((T, C + C2), BF16),
        compiler_params=_params(("arbitrary",)),
    )(q, q, q, dpooled, du_other, after)


def _to_chunk_rows(a):
    T, C = a.shape
    return a.reshape(N_CHUNK, T // N_CHUNK, C).transpose(1, 0, 2).reshape(T, C)


def _from_chunk_rows(a):
    T, C = a.shape
    return a.reshape(T // N_CHUNK, N_CHUNK, C).transpose(1, 0, 2).reshape(T, C)


def _cmul(ar, ai, br, bi):
    return ar * br - ai * bi, ar * bi + ai * br


def _ssm_discretise(log_neg_a_re, a_im, log_dt):
    dt = jnp.exp(log_dt)
    a_re = -jnp.exp(log_neg_a_re)
    mag = jnp.exp(a_re * dt)
    ang = a_im * dt
    lam_re, lam_im = mag * jnp.cos(ang), mag * jnp.sin(ang)
    den = a_re * a_re + a_im * a_im
    f_re = ((lam_re - 1.0) * a_re + lam_im * a_im) / den
    f_im = (lam_im * a_re - (lam_re - 1.0) * a_im) / den
    return lam_re, lam_im, f_re, f_im


def _ssm_params_fwd(log_neg_a_re, a_im, log_dt):
    rows, n = log_neg_a_re.shape

    def kern(a_ref, b_ref, c_ref, o1, o2, o3, o4):
        for o, v in zip((o1, o2, o3, o4), _ssm_discretise(a_ref[...], b_ref[...], c_ref[...])):
            o[...] = v

    return pl.pallas_call(kern, name="ssm_params_fwd", out_shape=[jax.ShapeDtypeStruct((rows, n), F32)] * 4)(
        log_neg_a_re, a_im, log_dt)


def _ssm_params_bwd(log_neg_a_re, a_im, log_dt, cots):
    rows, n = log_neg_a_re.shape

    def kern(a_ref, b_ref, c_ref, g1, g2, g3, g4, o1, o2, o3):
        _, vjp = jax.vjp(_ssm_discretise, a_ref[...], b_ref[...], c_ref[...])
        d1, d2, d3 = vjp((g1[...], g2[...], g3[...], g4[...]))
        o1[...] = d1
        o2[...] = d2
        o3[...] = d3

    return pl.pallas_call(
        kern, name="ssm_params_bwd",
        out_shape=[jax.ShapeDtypeStruct((rows, n), F32), jax.ShapeDtypeStruct((rows, n), F32),
                   jax.ShapeDtypeStruct((rows, 1), F32)])(log_neg_a_re, a_im, log_dt, *cots)


def _cpow(lr, li, n):
    out = None
    br, bi = lr, li
    while n:
        if n & 1:
            out = (br, bi) if out is None else _cmul(out[0], out[1], br, bi)
        n >>= 1
        if n:
            br, bi = _cmul(br, bi, br, bi)
    return out


def _slab(t):
    return pl.ds(pl.multiple_of(t * N_CHUNK, N_CHUNK), N_CHUNK)


def _steps(n, step, carry):
    main = n // SCAN_UNROLL

    def body(i, c):
        for r in range(SCAN_UNROLL):
            c = step(i * SCAN_UNROLL + r, c)
        return c

    carry = lax.fori_loop(0, main, body, carry)
    for k in range(main * SCAN_UNROLL, n):
        carry = step(jnp.int32(k), carry)
    return carry


def _shift_chunks(vr, vi, reverse):
    sub = lax.broadcasted_iota(jnp.int32, vr.shape, 0)
    if reverse:
        keep = sub != N_CHUNK - 1
        return jnp.where(keep, pltpu.roll(vr, N_CHUNK - 1, 0), 0.0), jnp.where(keep, pltpu.roll(vi, N_CHUNK - 1, 0), 0.0)
    keep = sub != 0
    return jnp.where(keep, pltpu.roll(vr, 1, 0), 0.0), jnp.where(keep, pltpu.roll(vi, 1, 0), 0.0)


def _chunk_scan(xr, xi, lr, li, Lc, reverse, prev=None):
    C = xr.shape[1]
    lrb, lib = jnp.broadcast_to(lr, (N_CHUNK, C)), jnp.broadcast_to(li, (N_CHUNK, C))
    zero = jnp.zeros((N_CHUNK, C), F32)

    def step_of(k):
        return (Lc - 1 - k) if reverse else k

    def advance(sr, si, t):
        return lrb * sr - lib * si + xr[_slab(t), :], lrb * si + lib * sr + xi[_slab(t), :]

    def pass1(k, c):
        return advance(c[0], c[1], step_of(k))

    er, ei = _steps(Lc, pass1, (zero, zero))
    pr, pi = _cpow(lr, li, Lc)
    prb, pib = jnp.broadcast_to(pr, (N_CHUNK, C)), jnp.broadcast_to(pi, (N_CHUNK, C))
    cr, ci = zero, zero
    for _ in range(N_CHUNK - 1):
        cr, ci = _shift_chunks(er + prb * cr - pib * ci, ei + prb * ci + pib * cr, reverse)

    if prev is None:
        def pass2(k, c):
            t = step_of(k)
            nr, ni = advance(c[0], c[1], t)
            xr[_slab(t), :] = nr
            xi[_slab(t), :] = ni
            return nr, ni

        _steps(Lc, pass2, (cr, ci))
        return None

    qr, qi = prev
    def pair(ar, ai, wr, wi, acc):
        return acc[0] + ar * wr + ai * wi, acc[1] + ai * wr - ar * wi

    def pass2(k, c):
        sr, si, accr, acci = c
        t = step_of(k)
        nr, ni = advance(sr, si, t)
        xr[_slab(t), :] = nr
        xi[_slab(t), :] = ni
        tp = t - 1 if reverse else t + 1
        accr, acci = pair(nr, ni, qr[_slab(tp), :], qi[_slab(tp), :], (accr, acci))
        return nr, ni, accr, acci

    sr, si, accr, acci = _steps(Lc - 1, pass2, (cr, ci, zero, zero))
    t = step_of(Lc - 1)
    nr, ni = advance(sr, si, t)
    xr[_slab(t), :] = nr
    xi[_slab(t), :] = ni
    edge = step_of(0)
    wr, wi = _shift_chunks(qr[_slab(edge), :], qi[_slab(edge), :], not reverse)
    return pair(nr, ni, wr, wi, (accr, acci))


def _ssm_dims(T):
    assert T % ROW_BLOCK == 0
    return T // N_CHUNK, T // ROW_BLOCK


def _group_mask():
    shape = (SSM_BLOCK_GROUPS * SSM_GROUP, SSM_BLOCK_GROUPS * SSM_STATE)
    rows = lax.broadcasted_iota(jnp.int32, shape, 0) >> GROUP_SHIFT
    cols = lax.broadcasted_iota(jnp.int32, shape, 1) >> STATE_SHIFT
    return rows == cols


def _expand_groups(compact):
    return jnp.where(_group_mask(), jnp.tile(compact, (SSM_BLOCK_GROUPS, 1)), 0.0).astype(BF16)


def _collect_groups(block):
    kept = jnp.where(_group_mask(), block, 0.0)
    out = kept[0:SSM_GROUP]
    for g in range(1, SSM_BLOCK_GROUPS):
        out = out + kept[g * SSM_GROUP:(g + 1) * SSM_GROUP]
    return out


def _ssm_specs(T, direction):
    SC = SSM_BLOCK_GROUPS * SSM_STATE
    col = pl.BlockSpec((T, LANES), lambda m: (0, m))
    mat = pl.BlockSpec((None, None, SSM_GROUP, SC), lambda m: (direction, m, 0, 0))
    vec = pl.BlockSpec((None, None, 1, SC), lambda m: (direction, m, 0, 0))
    return col, mat, vec


def _ssm_project_in(u_ref, bre, bim, xr, xi, nblk):
    for blk in range(nblk):
        rows = pl.ds(blk * ROW_BLOCK, ROW_BLOCK)
        ub = u_ref[rows, :].astype(BF16)
        xr[rows, :] = jnp.dot(ub, bre, preferred_element_type=F32)
        xi[rows, :] = jnp.dot(ub, bim, preferred_element_type=F32)


def _ssm_fwd(name, u_bf, mats, vecs, direction):
    T, W = u_bf.shape
    SC = SSM_BLOCK_GROUPS * SSM_STATE
    nb = W // LANES
    Lc, nblk = _ssm_dims(T)
    reverse = bool(direction)

    def kern(u_ref, bre_ref, bim_ref, cre_ref, cim_ref, lr_ref, li_ref, fr_ref, fi_ref, y_ref, xr, xi):
        _ssm_project_in(u_ref, _expand_groups(bre_ref[...]), _expand_groups(bim_ref[...]), xr, xi, nblk)
        _chunk_scan(xr, xi, lr_ref[...], li_ref[...], Lc, reverse)
        fr, fi = fr_ref[...], fi_ref[...]
        cre, cim = _expand_groups(cre_ref[...]), _expand_groups(cim_ref[...])
        for blk in range(nblk):
            rows = pl.ds(blk * ROW_BLOCK, ROW_BLOCK)
            sr, si = _cmul(fr, fi, xr[rows, :], xi[rows, :])
            y_ref[rows, :] = (lax.dot_general(sr.astype(BF16), cre, _NT, preferred_element_type=F32)
                              - lax.dot_general(si.astype(BF16), cim, _NT, preferred_element_type=F32))

    col, mat, vec = _ssm_specs(T, direction)
    return pl.pallas_call(
        kern, name=name, grid=(nb,),
        in_specs=[col, mat, mat, mat, mat, vec, vec, vec, vec],
        out_specs=col, out_shape=jax.ShapeDtypeStruct((T, W), F32),
        scratch_shapes=[pltpu.VMEM((T, SC), F32), pltpu.VMEM((T, SC), F32)],
        compiler_params=_params(("arbitrary",)),
    )(u_bf, *mats, *vecs)


def _ssm_bwd(name, u_bf, dy_bf, du_in, mats, vecs, direction, du_dtype):
    T, W = u_bf.shape
    SC = SSM_BLOCK_GROUPS * SSM_STATE
    nb = W // LANES
    Lc, nblk = _ssm_dims(T)
    reverse = bool(direction)

    def kern(u_ref, dy_ref, dui_ref, bre_ref, bim_ref, cre_ref, cim_ref, lr_ref, li_ref, fr_ref, fi_ref,
             du_ref, dbre_ref, dbim_ref, dcre_ref, dcim_ref, dlr_ref, dli_ref, dfr_ref, dfi_ref,
             xr, xi, gr, gi):
        lr, li, fr, fi = lr_ref[...], li_ref[...], fr_ref[...], fi_ref[...]
        bre, bim = _expand_groups(bre_ref[...]), _expand_groups(bim_ref[...])
        cre, cim = _expand_groups(cre_ref[...]), _expand_groups(cim_ref[...])
        _ssm_project_in(u_ref, bre, bim, xr, xi, nblk)
        _chunk_scan(xr, xi, lr, li, Lc, reverse)
        mre = jnp.zeros((LANES, SC), F32)
        mim = jnp.zeros((LANES, SC), F32)
        dfr = jnp.zeros((1, SC), F32)
        dfi = jnp.zeros((1, SC), F32)
        for blk in range(nblk):
            rows = pl.ds(blk * ROW_BLOCK, ROW_BLOCK)
            dy = dy_ref[rows, :].astype(BF16)
            x_r, x_i = xr[rows, :], xi[rows, :]
            g_r = jnp.dot(dy, cre, preferred_element_type=F32)
            g_i = -jnp.dot(dy, cim, preferred_element_type=F32)
            mre += lax.dot_general(dy, x_r.astype(BF16), _TN, preferred_element_type=F32)
            mim += lax.dot_general(dy, x_i.astype(BF16), _TN, preferred_element_type=F32)
            dfr += _colsum(g_r * x_r + g_i * x_i)
            dfi += _colsum(g_i * x_r - g_r * x_i)
            gr[rows, :] = fr * g_r + fi * g_i
            gi[rows, :] = fr * g_i - fi * g_r
        dcre_ref[...] = _collect_groups(fr * mre - fi * mim)
        dcim_ref[...] = _collect_groups(-(fr * mim + fi * mre))
        dfr_ref[...] = dfr
        dfi_ref[...] = dfi
        accr, acci = _chunk_scan(gr, gi, lr, -li, Lc, not reverse, prev=(xr, xi))
        dlr_ref[...] = _colsum(accr)
        dli_ref[...] = _colsum(acci)
        dbre = jnp.zeros((LANES, SC), F32)
        dbim = jnp.zeros((LANES, SC), F32)
        for blk in range(nblk):
            rows = pl.ds(blk * ROW_BLOCK, ROW_BLOCK)
            ub = u_ref[rows, :].astype(BF16)
            a_r, a_i = gr[rows, :].astype(BF16), gi[rows, :].astype(BF16)
            dbre += lax.dot_general(ub, a_r, _TN, preferred_element_type=F32)
            dbim += lax.dot_general(ub, a_i, _TN, preferred_element_type=F32)
            du_ref[rows, :] = (dui_ref[rows, :] + lax.dot_general(a_r, bre, _NT, preferred_element_type=F32)
                               + lax.dot_general(a_i, bim, _NT, preferred_element_type=F32)).astype(du_ref.dtype)
        dbre_ref[...] = _collect_groups(dbre)
        dbim_ref[...] = _collect_groups(dbim)

    col, mat, vec = _ssm_specs(T, direction)
    mat_out = pl.BlockSpec((None, SSM_GROUP, SC), lambda m: (m, 0, 0))
    vec_out = pl.BlockSpec((None, 1, SC), lambda m: (m, 0, 0))
    mat_shape = jax.ShapeDtypeStruct((nb, SSM_GROUP, SC), F32)
    vec_shape = jax.ShapeDtypeStruct((nb, 1, SC), F32)
    return pl.pallas_call(
        kern, name=name, grid=(nb,),
        in_specs=[col, col, col, mat, mat, mat, mat, vec, vec, vec, vec],
        out_specs=[col, mat_out, mat_out, mat_out, mat_out, vec_out, vec_out, vec_out, vec_out],
        out_shape=[jax.ShapeDtypeStruct((T, W), du_dtype), mat_shape, mat_shape, mat_shape, mat_shape,
                   vec_shape, vec_shape, vec_shape, vec_shape],
        scratch_shapes=[pltpu.VMEM((T, SC), F32)] * 4,
        compiler_params=_params(("arbitrary",)),
    )(u_bf, dy_bf, du_in, *mats, *vecs)


def _groups_side_by_side(p, channel_axis):
    d, G = p.shape[:2]
    nb = G // SSM_BLOCK_GROUPS
    v = p.reshape(d, nb, SSM_BLOCK_GROUPS, p.shape[2], p.shape[3])
    v = v.transpose(0, 1, 4, 2, 3) if channel_axis == 3 else v.transpose(0, 1, 3, 2, 4)
    return v.reshape(d, nb, SSM_GROUP, SSM_BLOCK_GROUPS * SSM_STATE)


def _groups_apart(m, channel_axis):
    d, nb = m.shape[:2]
    v = m.reshape(d, nb, SSM_GROUP, SSM_BLOCK_GROUPS, SSM_STATE)
    v = v.transpose(0, 1, 3, 4, 2) if channel_axis == 3 else v.transpose(0, 1, 3, 2, 4)
    return v.reshape((d, nb * SSM_BLOCK_GROUPS) + v.shape[3:])


def _conv_taps(ref, c, R, T):
    n = T // R
    r0 = pl.multiple_of(c * R, R)
    main = ref[pl.ds(r0, R), :]
    before = ref[pl.ds(pl.multiple_of(jnp.maximum(r0 - 8, 0), 8), 8), :]
    after = ref[pl.ds(pl.multiple_of(jnp.minimum(r0 + R, T - 8), 8), 8), :]
    ext = jnp.concatenate([jnp.where(c > 0, before, 0.0), main, jnp.where(c < n - 1, after, 0.0)], axis=0)
    m1 = pltpu.roll(ext, 1, 0)[8:8 + R]
    p1 = pltpu.roll(ext, R + 15, 0)[8:8 + R]
    return m1, main, p1, r0


def _conv_specs(T, F, cb):
    nj = F // cb
    lo = lambda rows: pl.BlockSpec((rows, cb), lambda j: (0, j))
    hi = lambda rows: pl.BlockSpec((rows, cb), lambda j: (0, j + nj))
    return nj, lo, hi


def _convact_fwd(up, conv_w, conv_b, F):
    T = up.shape[0]
    cb = _pick(F, (256, 128))
    R = _pick(T, (256, 128, 64))
    nj, lo, hi = _conv_specs(T, F, cb)

    def kern(uv, ug, wv, wg, bv, bg, o_ref):
        def body(c, carry):
            v1, v0, v2, r0 = _conv_taps(uv, c, R, T)
            g1, g0, g2, _ = _conv_taps(ug, c, R, T)
            val = v1 * wv[0:1, :] + v0 * wv[1:2, :] + v2 * wv[2:3, :] + bv[...]
            gate = g1 * wg[0:1, :] + g0 * wg[1:2, :] + g2 * wg[2:3, :] + bg[...]
            half = 0.5 * gate
            o_ref[pl.ds(r0, R), :] = (val * (half * jnp.tanh(half) + half)).astype(o_ref.dtype)
            return carry

        lax.fori_loop(0, T // R, body, 0)

    return pl.pallas_call(
        kern, name="convact_fwd", grid=(nj,),
        in_specs=[lo(T), hi(T), lo(3), hi(3), lo(1), hi(1)],
        out_specs=lo(T), out_shape=jax.ShapeDtypeStruct((T, F), BF16),
        compiler_params=_params(("arbitrary",)),
    )(up, up, conv_w, conv_w, conv_b, conv_b)


def _convact_bwd(up, dact, conv_w, conv_b, F):
    T = up.shape[0]
    cb = _pick(F, (256, 128))
    R = _pick(T, (256, 128, 64))
    nj, lo, hi = _conv_specs(T, F, cb)

    def kern(uv, ug, da, wv, wg, bv, bg, dup, dwv, dwg, dbv, dbg, sv, sg):
        zero = jnp.zeros((1, cb), F32)

        def pass_a(c, acc):
            v1, v0, v2, r0 = _conv_taps(uv, c, R, T)
            g1, g0, g2, _ = _conv_taps(ug, c, R, T)
            val = v1 * wv[0:1, :] + v0 * wv[1:2, :] + v2 * wv[2:3, :] + bv[...]
            gate = g1 * wg[0:1, :] + g0 * wg[1:2, :] + g2 * wg[2:3, :] + bg[...]
            sig = _sigmoid(gate)
            d = da[pl.ds(r0, R), :]
            silu = gate * sig
            dval = d * silu
            dgate = d * val * (sig + silu - silu * sig)
            sv[pl.ds(r0, R), :] = dval
            sg[pl.ds(r0, R), :] = dgate
            terms = (dval * v1, dval * v0, dval * v2, dval, dgate * g1, dgate * g0, dgate * g2, dgate)
            return tuple(a + _colsum(t) for a, t in zip(acc, terms))

        acc = lax.fori_loop(0, T // R, pass_a, (zero,) * 8)
        for k in range(3):
            dwv[k:k + 1, :] = acc[k]
            dwg[k:k + 1, :] = acc[4 + k]
        dbv[...] = acc[3]
        dbg[...] = acc[7]

        def pass_b(c, carry):
            v1, v0, v2, r0 = _conv_taps(sv, c, R, T)
            g1, g0, g2, _ = _conv_taps(sg, c, R, T)
            dup[0, pl.ds(r0, R), :] = (v2 * wv[0:1, :] + v0 * wv[1:2, :] + v1 * wv[2:3, :]).astype(dup.dtype)
            dup[1, pl.ds(r0, R), :] = (g2 * wg[0:1, :] + g0 * wg[1:2, :] + g1 * wg[2:3, :]).astype(dup.dtype)
            return carry

        lax.fori_loop(0, T // R, pass_b, 0)

    dup, dwv, dwg, dbv, dbg = pl.pallas_call(
        kern, name="convact_bwd", grid=(nj,),
        in_specs=[lo(T), hi(T), lo(T), lo(3), hi(3), lo(1), hi(1)],
        out_specs=[pl.BlockSpec((2, T, cb), lambda j: (0, 0, j)), lo(3), lo(3), lo(1), lo(1)],
        out_shape=[jax.ShapeDtypeStruct((2, T, F), BF16),
                   jax.ShapeDtypeStruct((3, F), F32), jax.ShapeDtypeStruct((3, F), F32),
                   jax.ShapeDtypeStruct((1, F), F32), jax.ShapeDtypeStruct((1, F), F32)],
        scratch_shapes=[pltpu.VMEM((T, cb), F32), pltpu.VMEM((T, cb), F32)],
        compiler_params=_params(("arbitrary",)),
    )(up, up, dact, conv_w, conv_w, conv_b, conv_b)
    return dup, (dwv, dwg), jnp.concatenate([dbv, dbg], axis=1)


def _peers(x, y, c, with_self=False):
    out = []
    for k in range(0 if with_self else 1, N_DEV):
        px = 1 - x if k & 4 else x
        py = 1 - y if k & 2 else y
        pc = 1 - c if k & 1 else c
        out.append((k, (px, py, pc), 4 * px + 2 * py + pc))
    return out


def _exchange_start(name, bufs, modes, after):
    handles, token = _exchange_start_groups(name, [(bufs, modes)], after)
    return handles[0], token


def _exchange_start_groups(name, groups, after):
    bufs = [b for g, _ in groups for b in g]
    modes = [m for _, ms in groups for m in ms]
    first = [int(v) for v in np.cumsum([0] + [len(g) for g, _ in groups])]
    n, ng = len(bufs), len(groups)
    lands = [lax.empty((N_DEV,) + tuple(b.shape[-2:]), b.dtype) for b in bufs]

    def body(*refs):
        ins, land_in = refs[:n], refs[n:2 * n]
        sems = refs[2 * n + 1:2 * n + 1 + 2 * ng]
        token = refs[-1]
        x, y, c = lax.axis_index("x"), lax.axis_index("y"), lax.axis_index("c")
        me = 4 * x + 2 * y + c
        for g in range(ng):
            for b in range(first[g], first[g + 1]):
                for k, peer, slot in _peers(x, y, c, with_self=True):
                    sem = (b - first[g]) * N_DEV + k
                    pltpu.make_async_remote_copy(
                        src_ref=ins[b] if modes[b] == 'gather' else ins[b].at[slot], dst_ref=land_in[b].at[me],
                        send_sem=sems[2 * g].at[sem], recv_sem=sems[2 * g + 1].at[sem],
                        device_id=peer, device_id_type=MESH_ID).start()
        token[...] = jnp.zeros_like(token)

    hbm = pl.BlockSpec(memory_space=pltpu.HBM)
    sem_spec = pl.BlockSpec(memory_space=pltpu.SEMAPHORE)
    operands = [pltpu.with_memory_space_constraint(a, pltpu.HBM) for a in bufs + lands]
    sem_shapes = [pltpu.SemaphoreType.DMA((len(g) * N_DEV,)) for g, _ in groups for _ in range(2)]
    res = pl.pallas_call(
        body, name=name,
        out_shape=(*sem_shapes, *[pltpu.HBM(a.shape, a.dtype) for a in operands], jax.ShapeDtypeStruct((8, LANES), F32)),
        in_specs=[hbm] * (2 * n) + [pl.BlockSpec(memory_space=pl.ANY)],
        out_specs=(*([sem_spec] * (2 * ng)), *([hbm] * (2 * n)), pl.BlockSpec(memory_space=pltpu.VMEM)),
        input_output_aliases={i: 2 * ng + i for i in range(2 * n)},
        compiler_params=pltpu.CompilerParams(has_side_effects=pltpu.SideEffectType.DATAFLOW_SIDE_EFFECTING),
    )(*operands, after)
    srcs, landed = res[2 * ng:2 * ng + n], res[2 * ng + n:2 * ng + 2 * n]
    handles = [(res[2 * g], res[2 * g + 1], list(srcs[first[g]:first[g + 1]]), list(landed[first[g]:first[g + 1]]),
                tuple(modes[first[g]:first[g + 1]])) for g in range(ng)]
    return handles, res[-1]


def _exchange_wait(name, handle, after):
    send_sems, recv_sems, srcs, lands, modes = handle
    n = len(srcs)

    def body(*refs):
        src_in, land_in = refs[:n], refs[n:2 * n]
        send_ref, recv_ref = refs[2 * n], refs[2 * n + 1]
        x, y, c = lax.axis_index("x"), lax.axis_index("y"), lax.axis_index("c")
        for b in range(n):
            for k, peer, slot in _peers(x, y, c, with_self=True):
                sem = b * N_DEV + k
                copy = pltpu.make_async_remote_copy(
                    src_ref=src_in[b] if modes[b] == 'gather' else src_in[b].at[slot], dst_ref=land_in[b].at[slot],
                    send_sem=send_ref.at[sem], recv_sem=recv_ref.at[sem],
                    device_id=peer, device_id_type=MESH_ID)
                copy.wait_send()
                copy.wait_recv()

    hbm = pl.BlockSpec(memory_space=pltpu.HBM)
    sem_spec = pl.BlockSpec(memory_space=pltpu.SEMAPHORE)
    res = pl.pallas_call(
        body, name=name,
        out_shape=tuple(pltpu.HBM(a.shape, a.dtype) for a in srcs + lands),
        in_specs=[hbm] * (2 * n) + [sem_spec, sem_spec, pl.BlockSpec(memory_space=pl.ANY)],
        out_specs=tuple([hbm] * (2 * n)),
        input_output_aliases={i: i for i in range(2 * n)},
        compiler_params=pltpu.CompilerParams(has_side_effects=pltpu.SideEffectType.DATAFLOW_SIDE_EFFECTING),
    )(*srcs, *lands, send_sems, recv_sems, after)
    return list(res[n:])


def _adamw(w, g, m, v):
    m2 = ADAM_B1 * m + (1.0 - ADAM_B1) * g
    v2 = ADAM_B2 * v + (1.0 - ADAM_B2) * (g * g)
    m_hat = m2 / (1.0 - ADAM_B1 ** ADAM_STEP)
    v_hat = v2 / (1.0 - ADAM_B2 ** ADAM_STEP)
    return -ADAM_LR * (m_hat / (jnp.sqrt(v_hat) + ADAM_EPS) + ADAM_WD * w), m2, v2


def _sum_adamw(name, landed, w, m, v):
    R, C = w.shape
    row_bytes = N_DEV * (-(-C // LANES) * LANES) * landed.dtype.itemsize
    tiles = [d for d in range(16, R, 16) if R % d == 0 and d * row_bytes <= ADAM_TILE_BYTES]
    tr = max(tiles) if tiles and R * row_bytes > ADAM_TILE_BYTES else R

    def kern(x_ref, w_ref, m_ref, v_ref, g_out, d_out, m_out, v_out):
        g = x_ref[0].astype(F32)
        for q in range(1, N_DEV):
            g = g + x_ref[q].astype(F32)
        g_out[...] = g
        d_out[...], m_out[...], v_out[...] = _adamw(w_ref[...], g, m_ref[...], v_ref[...])

    blk = pl.BlockSpec((tr, C), lambda i: (i, 0))
    return pl.pallas_call(
        kern, name=name, grid=(R // tr,),
        in_specs=[pl.BlockSpec((N_DEV, tr, C), lambda i: (0, i, 0)), blk, blk, blk],
        out_specs=[blk] * 4, out_shape=[jax.ShapeDtypeStruct((R, C), F32)] * 4,
        compiler_params=_params(("arbitrary",)),
    )(landed, w, m, v)


def _sum_adamw_many(name, items):
    n = len(items)

    def kern(*refs):
        ins, outs = refs[:4 * n], refs[4 * n:]
        for k in range(n):
            x_ref, w_ref, m_ref, v_ref = ins[4 * k:4 * k + 4]
            g = x_ref[0].astype(F32)
            for q in range(1, N_DEV):
                g = g + x_ref[q].astype(F32)
            outs[4 * k][...] = g
            outs[4 * k + 1][...], outs[4 * k + 2][...], outs[4 * k + 3][...] = _adamw(w_ref[...], g, m_ref[...], v_ref[...])

    out_shape = [jax.ShapeDtypeStruct(w.shape, F32) for _, w, _, _ in items for _ in range(4)]
    res = pl.pallas_call(kern, name=name, out_shape=out_shape)(*[a for item in items for a in item])
    return [res[4 * k:4 * k + 4] for k in range(n)]


def _view2d(a):
    if a.ndim == 1:
        return a.reshape(1, -1)
    return a.reshape(-1, a.shape[-1])


def kernel(x, norm_mix_g, w_in, pool_w, pool_scale, ssm_log_neg_a_re, ssm_a_im, ssm_log_dt, ssm_b_re, ssm_b_im, ssm_c_re, ssm_c_im, ssm_d, glu_w, glu_b, out_norm_pool_g, out_norm_ssm_g, w_out, norm_ffn_g, w_up, conv_w, conv_b, w_down, final_norm_g, loss_target, m_norm_mix_g, m_w_in, m_pool_w, m_pool_scale, m_ssm_log_neg_a_re, m_ssm_a_im, m_ssm_log_dt, m_ssm_b_re, m_ssm_b_im, m_ssm_c_re, m_ssm_c_im, m_ssm_d, m_glu_w, m_glu_b, m_out_norm_pool_g, m_out_norm_ssm_g, m_w_out, m_norm_ffn_g, m_w_up, m_conv_w, m_conv_b, m_w_down, m_final_norm_g, v_norm_mix_g, v_w_in, v_pool_w, v_pool_scale, v_ssm_log_neg_a_re, v_ssm_a_im, v_ssm_log_dt, v_ssm_b_re, v_ssm_b_im, v_ssm_c_re, v_ssm_c_im, v_ssm_d, v_glu_w, v_glu_b, v_out_norm_pool_g, v_out_norm_ssm_g, v_w_out, v_norm_ffn_g, v_w_up, v_conv_w, v_conv_b, v_w_down, v_final_norm_g):
    given = dict(locals())
    weights = {n: given[n] for n in WEIGHTS}
    mom1 = {n: given["m_" + n] for n in WEIGHTS}
    mom2 = {n: given["v_" + n] for n in WEIGHTS}

    xs = x[0]
    tgt = loss_target[0]
    T, D = xs.shape
    DP = len(POOL_WINDOWS) * POOL_GROUP
    DS = D - DP
    G = DS // SSM_GROUP
    N, H = SSM_STATE, SSM_GROUP
    F2 = w_up.shape[1] * N_DEV
    F = F2 // 2
    fs = w_up.shape[1]
    row = lambda a: a.reshape(1, -1)

    in_handle, token = _exchange_start("gather_in_start", [w_in.astype(BF16)], ['gather'], after=w_in)
    later = lambda w: (w + token[0, 0]).astype(BF16)
    gather_groups = [([later(glu_w), later(w_out)], ['gather'] * 2),
                     ([later(w_up.T), conv_w], ['gather'] * 2), ([later(w_down)], ['gather'])]
    (mix_handle, up_handle, down_handle), token = _exchange_start_groups("gather_rest_start", gather_groups, token)
    CB = row(conv_b)

    g1, g2, g3 = row(norm_mix_g), row(norm_ffn_g), row(final_norm_g)
    gp, gs = row(out_norm_pool_g), row(out_norm_ssm_g)
    (xn,) = _rowmap("norm_mix", _rms_fwd, [xs], [g1 + token[0:1, 0:1]], [(D, BF16)])
    W_in = _exchange_wait("gather_in_wait", in_handle, xn)[0].reshape(D, D)
    halves = lambda r: (r[:, :DP], r[:, DP:])
    u, u_s5 = _mm_rows("proj_in", xn, W_in, 'nn', halves, [], [], [(DP, F32), (DS, F32)])
    tmp = _pick(T, (256, 128))
    pool_w_bf = pool_w.astype(BF16)
    ypn = _pool_fwd(u, pool_w_bf, row(pool_scale), gp, tmp)

    u_ssm = _to_chunk_rows(u_s5)
    u_ssm_bf = u_ssm.astype(BF16)
    a_rows = (ssm_log_neg_a_re.reshape(2 * G, N), ssm_a_im.reshape(2 * G, N), ssm_log_dt.reshape(2 * G, 1))
    disc = _ssm_params_fwd(*a_rows)
    nb = G // SSM_BLOCK_GROUPS
    vecs = [d.reshape(2, nb, 1, SSM_BLOCK_GROUPS * N) for d in disc]
    mats = [_groups_side_by_side(ssm_b_re, 3), _groups_side_by_side(ssm_b_im, 3),
            _groups_side_by_side(ssm_c_re, 2), _groups_side_by_side(ssm_c_im, 2)]
    y_dir = [_ssm_fwd("ssm_fwd_%d" % d, u_ssm_bf, mats, vecs, d) for d in range(2)]

    def mix_post(yf, yb, us, d, gw, gb, g):
        y = yf + yb + d * us
        z = _gelu(y)
        gate = _sigmoid(jnp.dot(z.astype(BF16), gw, preferred_element_type=F32) + gb)
        return _rms_fwd(z * gate, g), y

    mix_landed = _exchange_wait("gather_mix_wait", mix_handle, y_dir[1])
    W_glu = mix_landed[0].reshape(DS, DS)
    W_out = mix_landed[1].reshape(D, D)
    ysn, y_ssm = _rowmap("ssm_post", mix_post, [y_dir[0], y_dir[1], u_ssm], [row(ssm_d), W_glu, row(glu_b), gs],
                         [(DS, F32), (DS, F32)])
    ycat = jnp.concatenate([ypn, _from_chunk_rows(ysn).astype(BF16)], axis=1)
    h1, hn = _mm_rows("proj_out", ycat, W_out, 'nn', lambda h, g: (h, _rms_fwd(h, g)), [], [g2],
                      [(D, F32), (D, BF16)], add=xs)
    up_landed = _exchange_wait("gather_up_wait", up_handle, hn)
    W_up_t = up_landed[0].reshape(F2, D)
    CW = up_landed[1].transpose(1, 0, 2).reshape(3, F2)
    up = _mm("ffn_up", hn, W_up_t, 'nt', F32)
    act = _convact_fwd(up, CW, CB, F)
    W_down = _exchange_wait("gather_down_wait", down_handle, act)[0].reshape(F, D)

    def head(h, t, g):
        r = lax.rsqrt(_rowmean(h * h) + EPS)
        hh = h * r
        e = hh * g - t
        loss = 0.5 * jnp.sum(_rowmean(e * e), keepdims=True)
        dy = e * (1.0 / D)
        dxh = dy * g
        dh = r * (dxh - hh * _rowmean(dxh * hh))
        return dh, dh, jnp.broadcast_to(loss, (1, LANES)), _colsum(dy * hh)

    dh2, dh2_bf, loss_acc, dg3 = _mm_rows("ffn_down", act, W_down, 'nn', head, [tgt], [g3],
                                          [(D, F32), (D, BF16)], [(1, LANES), (1, D)], add=h1)

    dact = _mm("ffn_down_dx", dh2_bf, W_down, 'nt', F32)
    dW_down = _mm("ffn_down_dw", act, dh2_bf, 'tn', BF16)
    dup, dCW, dCB = _convact_bwd(up, dact, CW, CB, F)
    dhn_val = _mm("ffn_up_dx_val", dup, W_up_t, 'nn', F32, b_rows=(2, 0), a_plane=0)
    dW_up_t = _mm("ffn_up_dw", dup, hn, 'tn', BF16).reshape(F2, D)

    def shards(full_grad):
        return full_grad.reshape((N_DEV, full_grad.shape[0] // N_DEV) + full_grad.shape[1:])

    half = N_DEV // 2
    dCW_sh = jnp.concatenate([h.reshape(3, half, fs).transpose(1, 0, 2) for h in dCW], axis=0)
    ffn_sent = [shards(dW_down), shards(dW_up_t), dCW_sh]
    ffn_handle, token = _exchange_start("scatter_ffn_start", ffn_sent, ['scatter'] * 3, after=dhn_val)
    g2_late = g2 + token[0:1, 0:1]

    def norm_bwd_add(dy, hx, res, g):
        dx, dg = _rms_bwd(hx, g, dy)
        return dx + res, dg

    def norm_bwd_add2(dy, hx, res, g):
        dx, dg = _rms_bwd(hx, g, dy)
        return dx + res, dx + res, dg

    dh1, dh1_bf, dg2 = _mm_rows("ffn_up_dx_gate", dup, W_up_t, 'nn', norm_bwd_add2, [h1, dh2], [g2_late],
                                [(D, F32), (D, BF16)], [(1, D)], add=dhn_val, b_rows=(2, 1), a_plane=1)
    dycat, d_ysn = _mm_rows("proj_out_dx", dh1_bf, W_out, 'nt', halves, [], [], [(DP, F32), (DS, F32)])
    dW_out = _mm("proj_out_dw", ycat, dh1_bf, 'tn', BF16)

    out_sent = [shards(dW_out)]
    out_handle, token = _exchange_start("scatter_out_start", out_sent, ['scatter'], after=dycat)
    q, dpooled, dpool_w, dpool_scale, dgp = _pool_bwd_rows(u, dycat, pool_w_bf, row(pool_scale),
                                                           gp + token[0:1, 0:1], tmp)

    def mix_post_bwd(y, us, dyn, d, gw, gb, g):
        z, dz_dy = _gelu_and_grad(y)
        gz = z.astype(BF16)
        sig = _sigmoid(jnp.dot(gz, gw, preferred_element_type=F32) + gb)
        dys, dg = _rms_bwd(z * sig, g, dyn)
        dgl = dys * z * sig * (1.0 - sig)
        dgl_bf = dgl.astype(BF16)
        dz = dys * sig + lax.dot_general(dgl_bf, gw, (((1,), (1,)), ((), ())), preferred_element_type=F32)
        dgw = lax.dot_general(gz, dgl_bf, (((0,), (0,)), ((), ())), preferred_element_type=F32)
        dyv = dz * dz_dy
        return dyv, dyv * d, dgw, _colsum(dgl), _colsum(dyv * us), dg

    dyv, du_dir, dglu_w, dglu_b, dssm_d, dgs = _rowmap(
        "ssm_post_bwd", mix_post_bwd, [y_ssm, u_ssm, _to_chunk_rows(d_ysn)], [row(ssm_d), W_glu, row(glu_b), gs],
        [(DS, BF16), (DS, F32)], [(DS, DS), (1, DS), (1, DS), (1, DS)])

    du_ssm = du_dir
    dB, dC, dvec = [], [], []
    for d in range(2):
        res = _ssm_bwd("ssm_bwd_%d" % d, u_ssm_bf, dyv, du_ssm, mats, vecs, d, BF16 if d == 1 else F32)
        du_ssm = res[0]
        dB.append(res[1:3])
        dC.append(res[3:5])
        dvec.append(res[5:9])
    grad_b = [_groups_apart(jnp.stack([dB[0][k], dB[1][k]]), 3) for k in range(2)]
    grad_c = [_groups_apart(jnp.stack([dC[0][k], dC[1][k]]), 2) for k in range(2)]
    cots = [jnp.stack([dvec[0][k], dvec[1][k]]).reshape(2 * G, N) for k in range(4)]
    d_a_re, d_a_im, d_log_dt = _ssm_params_bwd(*a_rows, cots)

    rep_grads = {
        'pool_w': dpool_w, 'pool_scale': dpool_scale, 'ssm_log_neg_a_re': d_a_re, 'ssm_a_im': d_a_im,
        'ssm_log_dt': d_log_dt, 'ssm_b_re': grad_b[0], 'ssm_b_im': grad_b[1], 'ssm_c_re': grad_c[0], 'ssm_c_im': grad_c[1],
        'ssm_d': dssm_d, 'glu_b': dglu_b, 'out_norm_pool_g': dgp, 'out_norm_ssm_g': dgs, 'norm_ffn_g': dg2,
        'conv_b': dCB, 'final_norm_g': dg3}
    wide = ('pool_w', 'ssm_b_re', 'ssm_b_im', 'ssm_c_re', 'ssm_c_im')

    def narrow(n):
        return weights[n].ndim == 4 and weights[n].shape[-1] < LANES // 2

    def travel(n):
        g = rep_grads[n]
        if narrow(n):
            g = jnp.swapaxes(g, 2, 3)
        if n in wide:
            return g.reshape(-1, PACK_W if g.size % PACK_W == 0 else LANES).astype(BF16)
        return _view2d(g.reshape(weights[n].shape))

    small = tuple(n for n in REPLICATED if n != 'norm_mix_g')
    small_sent = [shards(dglu_w.astype(BF16))] + [travel(n) for n in small] + [loss_acc]
    small_handle, token = _exchange_start("reduce_small_start", small_sent, ['scatter'] + ['gather'] * (len(small) + 1),
                                          after=d_a_re)

    du = _pool_bwd_band(q, dpooled, _from_chunk_rows(du_ssm), token, tmp)
    dW_in = _mm("proj_in_dw", xn, du, 'tn', BF16)
    in_handle, token = _exchange_start("scatter_in_start", [shards(dW_in)], ['scatter'], after=dW_in)
    dx, dg1 = _mm_rows("proj_in_dx", du, W_in + token[0, 0].astype(BF16), 'nt', norm_bwd_add, [xs, dh1], [g1],
                       [(D, F32)], [(1, D)])
    rep_grads['norm_mix_g'] = dg1
    last_handle, token = _exchange_start("reduce_last_start", [travel('norm_mix_g')], ['gather'], after=dx)

    grads, delta, new_m, new_v = {}, {}, {}, {}

    def update(n, landed_n):
        shape = weights[n].shape
        if narrow(n):
            swapped = lambda a: _view2d(jnp.swapaxes(a, 2, 3))
            landed_n = landed_n.reshape((N_DEV,) + swapped(weights[n]).shape)
            res = _sum_adamw("adamw_" + n, landed_n, swapped(weights[n]), swapped(mom1[n]), swapped(mom2[n]))
            back = lambda r: jnp.swapaxes(r.reshape(shape[:2] + (shape[3], shape[2])), 2, 3)
            grads[n], delta[n], new_m[n], new_v[n] = [back(r) for r in res]
            return
        if n == 'w_up':
            res = _sum_adamw("adamw_" + n, landed_n, weights[n].T, mom1[n].T, mom2[n].T)
            grads[n], delta[n], new_m[n], new_v[n] = [r.T for r in res]
            return
        landed_n = landed_n.reshape((N_DEV,) + _view2d(weights[n]).shape)
        res = _sum_adamw("adamw_" + n, landed_n, _view2d(weights[n]), _view2d(mom1[n]), _view2d(mom2[n]))
        grads[n], delta[n], new_m[n], new_v[n] = [r.reshape(shape) for r in res]

    def update_many(name, named_landed):
        items = [(l.reshape((N_DEV,) + _view2d(weights[n]).shape), _view2d(weights[n]), _view2d(mom1[n]), _view2d(mom2[n]))
                 for n, l in named_landed]
        for (n, _), res in zip(named_landed, _sum_adamw_many(name, items)):
            grads[n], delta[n], new_m[n], new_v[n] = [r.reshape(weights[n].shape) for r in res]

    ffn_landed = _exchange_wait("scatter_ffn_wait", ffn_handle, token)
    for n, l in zip(('w_down', 'w_up'), ffn_landed):
        update(n, l)
    (out_landed,) = _exchange_wait("scatter_out_wait", out_handle, token)
    update('w_out', out_landed)
    small_landed = _exchange_wait("reduce_small_wait", small_handle, token)
    tiny = [('conv_w', ffn_landed[2])]
    for n, l in zip(('glu_w',) + small, small_landed):
        if weights[n].size <= ADAM_TINY:
            tiny.append((n, l))
        else:
            update(n, l)
    update_many("adamw_tiny", tiny)

    def loss_sum(parts_ref, o_ref):
        s = parts_ref[0]
        for q in range(1, N_DEV):
            s = s + parts_ref[q]
        o_ref[...] = s

    loss = pl.pallas_call(loss_sum, name="loss_sum", out_shape=jax.ShapeDtypeStruct((1, LANES), F32))(small_landed[-1])[0, 0]

    update('w_in', _exchange_wait("scatter_in_wait", in_handle, grads[small[-1]])[0])
    update('norm_mix_g', _exchange_wait("reduce_last_wait", last_handle, grads['w_in'])[0])

    return (loss, dx[None], *[grads[n] for n in WEIGHTS], *[delta[n] for n in WEIGHTS],
            *[new_m[n] for n in WEIGHTS], *[new_v[n] for n in WEIGHTS])
```

```python
import functools

import jax
import jax.numpy as jnp
import numpy as np
from jax import lax
from jax.experimental import pallas as pl
from jax.experimental.pallas import tpu as pltpu

F32 = jnp.float32
BF16 = jnp.bfloat16
MESH_ID = pl.DeviceIdType.MESH

N_DEV = 8
EPS = 1e-6
POOL_WINDOWS = (2, 4, 8, 16)
POOL_GROUP = 128
POOL_PAD = 64
SSM_GROUP = 16
SSM_STATE = 64
SSM_BLOCK_GROUPS = 8
N_CHUNK = 8
ROW_BLOCK = 512
SCAN_UNROLL = 8
GROUP_SHIFT = 4
STATE_SHIFT = 6
_NT = (((1,), (1,)), ((), ()))
_TN = (((0,), (0,)), ((), ()))
LANES = 128
PACK_W = 1024
VMEM_LIMIT = 56 * 1024 * 1024
MM_VMEM_BUDGET = 40 * 1024 * 1024
MM_TILE_CAP = 1408
MM_ROWS_TILES = (512, 1024)
ADAM_TILE_BYTES = 4 * 1024 * 1024
ADAM_TINY = 8192

ADAM_LR = 0.001
ADAM_B1 = 0.9
ADAM_B2 = 0.999
ADAM_EPS = 1e-08
ADAM_WD = 0.01
ADAM_STEP = 10

WEIGHTS = ['norm_mix_g', 'w_in', 'pool_w', 'pool_scale', 'ssm_log_neg_a_re', 'ssm_a_im', 'ssm_log_dt',
           'ssm_b_re', 'ssm_b_im', 'ssm_c_re', 'ssm_c_im', 'ssm_d', 'glu_w', 'glu_b', 'out_norm_pool_g',
           'out_norm_ssm_g', 'w_out', 'norm_ffn_g', 'w_up', 'conv_w', 'conv_b', 'w_down', 'final_norm_g']
SHARDED = ('w_in', 'w_out', 'w_down', 'glu_w', 'w_up', 'conv_w')
REPLICATED = tuple(n for n in WEIGHTS if n not in SHARDED)


def _pick(n, prefs):
    for p in prefs:
        if n % p == 0:
            return p
    return n


def _params(sem, vmem=None):
    return pltpu.CompilerParams(dimension_semantics=sem, vmem_limit_bytes=vmem or VMEM_LIMIT)


def _tiles(n, cap):
    return [d for d in range(LANES, min(n, cap) + 1, LANES) if n % d == 0] or [n]


def _mm(name, a, b, mode, out_dtype, add=None, b_rows=(1, 0), a_plane=None):
    batch = a.shape[0] if a.ndim == 3 and a_plane is None else None
    if mode == 'nn':
        (M, K), N = a.shape[-2:], b.shape[1]
        assert b.shape[0] == K * b_rows[0]
    elif mode == 'nt':
        (M, K), (N, _) = a.shape[-2:], b.shape
    else:
        (K, M), (_, N) = a.shape[-2:], b.shape
    dims = {'nn': (((1,), (0,)), ((), ())), 'nt': (((1,), (1,)), ((), ())), 'tn': (((0,), (0,)), ((), ()))}[mode]
    sa, sb, so = a.dtype.itemsize, b.dtype.itemsize, jnp.dtype(out_dtype).itemsize
    best = None
    for tm in _tiles(M, MM_TILE_CAP):
        for tn in _tiles(N, MM_TILE_CAP):
            need = 2 * (tm * K * sa + tn * K * sb + tm * tn * (so + (4 if add is not None else 0)))
            if need <= MM_VMEM_BUDGET:
                key = (tm * tn / (tm + tn), tm * tn)
                if best is None or key > best[0]:
                    best = (key, tm, tn)
    _, tm, tn = best
    rows_inner = a.size * sa * (N // tn) + b.size * sb < a.size * sa + b.size * sb * (M // tm)

    def kern(*refs):
        a_ref, b_ref = refs[:2]
        o_ref = refs[-1]
        r = lax.dot_general(a_ref[...].astype(BF16), b_ref[...].astype(BF16), dims, preferred_element_type=F32)
        if add is not None:
            r = r + refs[2][...]
        o_ref[...] = r.astype(o_ref.dtype)

    if rows_inner:
        grid = (N // tn, M // tm)
        ij = lambda g0, g1: (g1, g0)
    else:
        grid = (M // tm, N // tn)
        ij = lambda g0, g1: (g0, g1)

    def spec(block, index, batched=False):
        if batch is None:
            return pl.BlockSpec(block, lambda g0, g1: index(*ij(g0, g1)))
        if batched:
            return pl.BlockSpec((None,) + block, lambda p, g0, g1: (p,) + index(*ij(g0, g1)))
        return pl.BlockSpec(block, lambda p, g0, g1: index(*ij(g0, g1)))

    if a_plane is not None:
        assert mode != 'tn'
        a_spec = pl.BlockSpec((None, tm, K), lambda g0, g1: (a_plane, ij(g0, g1)[0], 0))
    else:
        a_spec = (spec((K, tm), lambda i, j: (0, i), True) if mode == 'tn' else spec((tm, K), lambda i, j: (i, 0), True))
    b_spec = (spec((tn, K), lambda i, j: (j, 0)) if mode == 'nt' else spec((K, tn), lambda i, j: (b_rows[1], j)))
    in_specs = [a_spec, b_spec]
    args = [a, b]
    if add is not None:
        in_specs.append(spec((tm, tn), lambda i, j: (i, j), True))
        args.append(add)
    lead = () if batch is None else (batch,)
    return pl.pallas_call(
        kern, name=name, grid=lead + grid, in_specs=in_specs,
        out_specs=spec((tm, tn), lambda i, j: (i, j), True),
        out_shape=jax.ShapeDtypeStruct(lead + (M, N), out_dtype),
        compiler_params=_params(("parallel",) * (len(lead) + 2)),
    )(*args)


def _store_rows_and_sums(i, outs, res, n_row):
    for o, v in zip(outs[:n_row], res[:n_row]):
        o[...] = v.astype(o.dtype)
    for o, v in zip(outs[n_row:], res[n_row:]):
        @pl.when(i == 0)
        def _(o=o, v=v):
            o[...] = v

        @pl.when(i != 0)
        def _(o=o, v=v):
            o[...] += v


def _mm_rows(name, a, b, mode, fn, rows, fulls, out_rows, out_accs=(), add=None, b_rows=(1, 0), a_plane=None):
    M, K = a.shape[-2:]
    N = b.shape[1] if mode == 'nn' else b.shape[0]
    assert b.shape[0] == (K * b_rows[0] if mode == 'nn' else N)
    dims = _NT if mode == 'nt' else (((1,), (0,)), ((), ()))
    n_extra, n_row = len(rows) + len(fulls), len(out_rows)
    row_bytes = (K * a.dtype.itemsize + (4 * N if add is not None else 0) + sum(4 * r.shape[1] for r in rows)
                 + sum(c * jnp.dtype(dt).itemsize for c, dt in out_rows))
    fits = [t for t in MM_ROWS_TILES if M % t == 0 and 2 * (t * row_bytes + K * N * b.dtype.itemsize) <= MM_VMEM_BUDGET]
    tm = max(fits) if fits else min(MM_ROWS_TILES)

    def kern(*refs):
        i = pl.program_id(0)
        r = lax.dot_general(refs[0][...].astype(BF16), refs[1][...].astype(BF16), dims, preferred_element_type=F32)
        pos = 2
        if add is not None:
            r = r + refs[pos][...]
            pos += 1
        res = fn(r, *[x[...] for x in refs[pos:pos + n_extra]])
        _store_rows_and_sums(i, refs[pos + n_extra:], res, n_row)

    def full_spec(shape):
        nd = len(shape)
        return pl.BlockSpec(tuple(shape), lambda i: (0,) * nd)

    a_spec = (pl.BlockSpec((tm, K), lambda i: (i, 0)) if a_plane is None
              else pl.BlockSpec((None, tm, K), lambda i: (a_plane, i, 0)))
    b_spec = pl.BlockSpec((K, N), lambda i: (b_rows[1], 0)) if mode == 'nn' else pl.BlockSpec((N, K), lambda i: (0, 0))
    ins = [_row_in(r, tm, M) for r in ([add] if add is not None else []) + list(rows)]
    outs = [_row_out(o, tm, M) for o in out_rows]
    res = pl.pallas_call(
        kern, name=name, grid=(M // tm,),
        in_specs=[a_spec, b_spec] + [s for s, _ in ins] + [full_spec(f.shape) for f in fulls],
        out_specs=[s for s, _ in outs] + [full_spec(s) for s in out_accs],
        out_shape=[sh for _, sh in outs] + [jax.ShapeDtypeStruct(tuple(s), F32) for s in out_accs],
        compiler_params=_params(("arbitrary",)),
    )(a, b, *[x for _, x in ins], *fulls)
    return res


def _row_in(r, tm, T):
    arr, w, k = r if isinstance(r, tuple) else (r, r.shape[1], 0)
    return pl.BlockSpec((tm, w), lambda i: (i, k)), arr


def _row_out(o, tm, T):
    c, dt = o
    return pl.BlockSpec((tm, c), lambda i: (i, 0)), jax.ShapeDtypeStruct((T, c), dt)


def _rowmap(name, fn, rows, fulls, out_rows, out_accs=(), tm=512):
    T = (rows[0][0] if isinstance(rows[0], tuple) else rows[0]).shape[0]
    tm = min(tm, T)
    assert T % tm == 0
    n_in, n_row = len(rows) + len(fulls), len(out_rows)

    def kern(*refs):
        i = pl.program_id(0)
        res = fn(*[r[...] for r in refs[:n_in]])
        res = res if isinstance(res, (tuple, list)) else (res,)
        _store_rows_and_sums(i, refs[n_in:], res, n_row)

    def full_spec(shape):
        nd = len(shape)
        return pl.BlockSpec(tuple(shape), lambda i: (0,) * nd)

    ins = [_row_in(r, tm, T) for r in rows]
    outs = [_row_out(o, tm, T) for o in out_rows]
    res = pl.pallas_call(
        kern, name=name, grid=(T // tm,),
        in_specs=[s for s, _ in ins] + [full_spec(f.shape) for f in fulls],
        out_specs=[s for s, _ in outs] + [full_spec(s) for s in out_accs],
        out_shape=[sh for _, sh in outs] + [jax.ShapeDtypeStruct(tuple(s), F32) for s in out_accs],
        compiler_params=_params(("arbitrary",)),
    )(*[x for _, x in ins], *fulls)
    return res


def _colsum(v):
    return jnp.sum(v, axis=0, keepdims=True)


def _rowmean(v):
    return jnp.mean(v, axis=-1, keepdims=True)


def _rms_fwd(x, g):
    r = lax.rsqrt(_rowmean(x * x) + EPS)
    return x * r * g


def _rms_bwd(x, g, dy):
    r = lax.rsqrt(_rowmean(x * x) + EPS)
    xh = x * r
    dxh = dy * g
    return r * (dxh - xh * _rowmean(dxh * xh)), _colsum(dy * xh)


def _sigmoid(v):
    return 0.5 * jnp.tanh(0.5 * v) + 0.5


def _gelu(y):
    c = np.sqrt(2.0 / np.pi).astype(np.float32)
    return 0.5 * y * (1.0 + jnp.tanh(c * (y + 0.044715 * (y * y * y))))


def _gelu_and_grad(y):
    c = np.sqrt(2.0 / np.pi).astype(np.float32)
    th = jnp.tanh(c * (y + 0.044715 * (y * y * y)))
    return (0.5 * y * (1.0 + th),
            0.5 * (1.0 + th) + 0.5 * y * (1.0 - th * th) * c * (1.0 + 3.0 * 0.044715 * (y * y)))


def _split3(v):
    hi = v.astype(BF16)
    r1 = v - hi.astype(F32)
    mid = r1.astype(BF16)
    lo = (r1 - mid.astype(F32)).astype(BF16)
    return hi, mid, lo


def _band(tm, lo_off, hi_off):
    shape = (tm, tm + 2 * POOL_PAD)
    d = lax.broadcasted_iota(jnp.int32, shape, 1) - lax.broadcasted_iota(jnp.int32, shape, 0) - POOL_PAD
    return ((d >= lo_off) & (d < hi_off)).astype(BF16)


def _window_sums(ext, tm, lo_off, hi_off):
    hi, mid, lo = _split3(ext)
    band = _band(LANES, lo_off, hi_off)
    dot = functools.partial(jnp.dot, preferred_element_type=F32)
    out = []
    for s in range(tm // LANES):
        rows = slice(s * LANES, (s + 2) * LANES)
        out.append(dot(band, hi[rows]) + dot(band, mid[rows]) + dot(band, lo[rows]))
    return jnp.concatenate(out, axis=0)


def _window_count(r0, tm, half, T):
    t = r0 + lax.broadcasted_iota(jnp.int32, (tm, 1), 0)
    return (jnp.minimum(t + half, T) - jnp.maximum(t - half, 0)).astype(F32)


def _halo_specs(tm, C, T):
    per = tm // POOL_PAD
    last = T // POOL_PAD - 1
    return [pl.BlockSpec((POOL_PAD, C), lambda i: (jnp.maximum(i * per - 1, 0), 0)),
            pl.BlockSpec((tm, C), lambda i: (i, 0)),
            pl.BlockSpec((POOL_PAD, C), lambda i: (jnp.minimum((i + 1) * per, last), 0))]


def _with_halo(before_ref, main_ref, after_ref, i, n):
    before = jnp.where(i > 0, before_ref[...], 0.0)
    after = jnp.where(i < n - 1, after_ref[...], 0.0)
    return jnp.concatenate([before, main_ref[...], after], axis=0)


def _pool_block(ext, ctr, w_ref, r0, tm, T):
    pooled, conc = [], []
    for gi, w in enumerate(POOL_WINDOWS):
        half = w // 2
        cols = slice(gi * POOL_GROUP, (gi + 1) * POOL_GROUP)
        ws = _window_sums(ext[:, cols], tm, -half, half)
        p = ws / _window_count(r0, tm, half, T) - ctr[:, cols]
        pooled.append(p)
        conc.append(jnp.dot(p.astype(BF16), w_ref[gi], preferred_element_type=F32))
    return pooled, jnp.concatenate(conc, axis=1)


def _pool_fwd(u, pool_w_bf, pool_scale, g_pool, tm):
    T = u.shape[0]
    nw = len(POOL_WINDOWS)
    C = nw * POOL_GROUP
    n = T // tm

    def kern(ub_ref, u_ref, ua_ref, w_ref, sc_ref, g_ref, o_ref):
        i = pl.program_id(0)
        r0 = pl.multiple_of(i * tm, tm)
        _, conc = _pool_block(_with_halo(ub_ref, u_ref, ua_ref, i, n), u_ref[...], w_ref, r0, tm, T)
        o_ref[...] = _rms_fwd(conc * sc_ref[...], g_ref[...]).astype(o_ref.dtype)

    return pl.pallas_call(
        kern, name="pool_fwd", grid=(n,),
        in_specs=_halo_specs(tm, C, T) + [pl.BlockSpec(pool_w_bf.shape, lambda i: (0, 0, 0)),
                                          pl.BlockSpec((1, C), lambda i: (0, 0)), pl.BlockSpec((1, C), lambda i: (0, 0))],
        out_specs=pl.BlockSpec((tm, C), lambda i: (i, 0)),
        out_shape=jax.ShapeDtypeStruct((T, C), BF16),
        compiler_params=_params(("arbitrary",)),
    )(u, u, u, pool_w_bf, pool_scale, g_pool)


def _pool_bwd_rows(u, d_y, pool_w_bf, pool_scale, g_pool, tm):
    T = u.shape[0]
    nw = len(POOL_WINDOWS)
    C = nw * POOL_GROUP
    n = T // tm

    def kern(ub_ref, u_ref, ua_ref, dy_ref, w_ref, sc_ref, g_ref, q_ref, dp_ref, dw_ref, dsc_ref, dg_ref):
        i = pl.program_id(0)
        r0 = pl.multiple_of(i * tm, tm)
        pooled, conc = _pool_block(_with_halo(ub_ref, u_ref, ua_ref, i, n), u_ref[...], w_ref, r0, tm, T)
        sc = sc_ref[...]
        dyp, dg = _rms_bwd(conc * sc, g_ref[...], dy_ref[...])
        dsc = _colsum(dyp * conc)
        dconc = (dyp * sc).astype(BF16)
        dws = []
        for gi, w in enumerate(POOL_WINDOWS):
            cols = slice(gi * POOL_GROUP, (gi + 1) * POOL_GROUP)
            dc = dconc[:, cols]
            dp = lax.dot_general(dc, w_ref[gi], (((1,), (1,)), ((), ())), preferred_element_type=F32)
            dp_ref[:, cols] = dp
            q_ref[:, cols] = dp / _window_count(r0, tm, w // 2, T)
            dws.append(lax.dot_general(pooled[gi].astype(BF16), dc, (((0,), (0,)), ((), ())), preferred_element_type=F32))

        @pl.when(i == 0)
        def _():
            for gi in range(nw):
                dw_ref[gi] = dws[gi]
            dsc_ref[...] = dsc
            dg_ref[...] = dg

        @pl.when(i != 0)
        def _():
            for gi in range(nw):
                dw_ref[gi] += dws[gi]
            dsc_ref[...] += dsc
            dg_ref[...] += dg

    full2 = pl.BlockSpec((1, C), lambda i: (0, 0))
    row_blk = pl.BlockSpec((tm, C), lambda i: (i, 0))
    return pl.pallas_call(
        kern, name="pool_bwd_rows", grid=(n,),
        in_specs=_halo_specs(tm, C, T) + [row_blk, pl.BlockSpec(pool_w_bf.shape, lambda i: (0, 0, 0)), full2, full2],
        out_specs=[row_blk, row_blk, pl.BlockSpec((nw, POOL_GROUP, POOL_GROUP), lambda i: (0, 0, 0)), full2, full2],
        out_shape=[jax.ShapeDtypeStruct((T, C), F32), jax.ShapeDtypeStruct((T, C), F32),
                   jax.ShapeDtypeStruct((nw, POOL_GROUP, POOL_GROUP), F32),
                   jax.ShapeDtypeStruct((1, C), F32), jax.ShapeDtypeStruct((1, C), F32)],
        compiler_params=_params(("arbitrary",)),
    )(u, u, u, d_y, pool_w_bf, pool_scale, g_pool)


def _pool_bwd_band(q, dpooled, du_other, after, tm):
    T, C = dpooled.shape
    C2 = du_other.shape[1]
    n = T // tm

    def kern(qb_ref, q_ref, qa_ref, dp_ref, other_ref, after_ref, o_ref):
        ext = _with_halo(qb_ref, q_ref, qa_ref, pl.program_id(0), n)
        for gi, w in enumerate(POOL_WINDOWS):
            half = w // 2
            cols = slice(gi * POOL_GROUP, (gi + 1) * POOL_GROUP)
            du = _window_sums(ext[:, cols], tm, -half + 1, half + 1) - dp_ref[:, cols]
            o_ref[:, cols] = du.astype(o_ref.dtype)
        o_ref[:, C:] = other_ref[...].astype(o_ref.dtype)

    return pl.pallas_call(
        kern, name="pool_bwd_band", grid=(n,),
        in_specs=_halo_specs(tm, C, T) + [pl.BlockSpec((tm, C), lambda i: (i, 0)), pl.BlockSpec((tm, C2), lambda i: (i, 0)),
                                          pl.BlockSpec(after.shape, lambda i: (0, 0))],
        out_specs=pl.BlockSpec((tm, C + C2), lambda i: (i, 0)),
        out_shape=jax.ShapeDtypeStruct((T, C + C2), BF16),
        compiler_params=_params(("arbitrary",)),
    )(q, q, q, dpooled, du_other, after)


def _to_chunk_rows(a):
    T, C = a.shape
    return a.reshape(N_CHUNK, T // N_CHUNK, C).transpose(1, 0, 2).reshape(T, C)


def _from_chunk_rows(a):
    T, C = a.shape
    return a.reshape(T // N_CHUNK, N_CHUNK, C).transpose(1, 0, 2).reshape(T, C)


def _cmul(ar, ai, br, bi):
    return ar * br - ai * bi, ar * bi + ai * br


def _ssm_discretise(log_neg_a_re, a_im, log_dt):
    dt = jnp.exp(log_dt)
    a_re = -jnp.exp(log_neg_a_re)
    mag = jnp.exp(a_re * dt)
    ang = a_im * dt
    lam_re, lam_im = mag * jnp.cos(ang), mag * jnp.sin(ang)
    den = a_re * a_re + a_im * a_im
    f_re = ((lam_re - 1.0) * a_re + lam_im * a_im) / den
    f_im = (lam_im * a_re - (lam_re - 1.0) * a_im) / den
    return lam_re, lam_im, f_re, f_im


def _ssm_params_fwd(log_neg_a_re, a_im, log_dt):
    rows, n = log_neg_a_re.shape

    def kern(a_ref, b_ref, c_ref, o1, o2, o3, o4):
        for o, v in zip((o1, o2, o3, o4), _ssm_discretise(a_ref[...], b_ref[...], c_ref[...])):
            o[...] = v

    return pl.pallas_call(kern, name="ssm_params_fwd", out_shape=[jax.ShapeDtypeStruct((rows, n), F32)] * 4)(
        log_neg_a_re, a_im, log_dt)


def _ssm_params_bwd(log_neg_a_re, a_im, log_dt, cots):
    rows, n = log_neg_a_re.shape

    def kern(a_ref, b_ref, c_ref, g1, g2, g3, g4, o1, o2, o3):
        _, vjp = jax.vjp(_ssm_discretise, a_ref[...], b_ref[...], c_ref[...])
        d1, d2, d3 = vjp((g1[...], g2[...], g3[...], g4[...]))
        o1[...] = d1
        o2[...] = d2
        o3[...] = d3

    return pl.pallas_call(
        kern, name="ssm_params_bwd",
        out_shape=[jax.ShapeDtypeStruct((rows, n), F32), jax.ShapeDtypeStruct((rows, n), F32),
                   jax.ShapeDtypeStruct((rows, 1), F32)])(log_neg_a_re, a_im, log_dt, *cots)


def _cpow(lr, li, n):
    out = None
    br, bi = lr, li
    while n:
        if n & 1:
            out = (br, bi) if out is None else _cmul(out[0], out[1], br, bi)
        n >>= 1
        if n:
            br, bi = _cmul(br, bi, br, bi)
    return out


def _slab(t):
    return pl.ds(pl.multiple_of(t * N_CHUNK, N_CHUNK), N_CHUNK)


def _steps(n, step, carry):
    main = n // SCAN_UNROLL

    def body(i, c):
        for r in range(SCAN_UNROLL):
            c = step(i * SCAN_UNROLL + r, c)
        return c

    carry = lax.fori_loop(0, main, body, carry)
    for k in range(main * SCAN_UNROLL, n):
        carry = step(jnp.int32(k), carry)
    return carry


def _shift_chunks(vr, vi, reverse):
    sub = lax.broadcasted_iota(jnp.int32, vr.shape, 0)
    if reverse:
        keep = sub != N_CHUNK - 1
        return jnp.where(keep, pltpu.roll(vr, N_CHUNK - 1, 0), 0.0), jnp.where(keep, pltpu.roll(vi, N_CHUNK - 1, 0), 0.0)
    keep = sub != 0
    return jnp.where(keep, pltpu.roll(vr, 1, 0), 0.0), jnp.where(keep, pltpu.roll(vi, 1, 0), 0.0)


def _chunk_scan(xr, xi, lr, li, Lc, reverse, prev=None):
    C = xr.shape[1]
    lrb, lib = jnp.broadcast_to(lr, (N_CHUNK, C)), jnp.broadcast_to(li, (N_CHUNK, C))
    zero = jnp.zeros((N_CHUNK, C), F32)

    def step_of(k):
        return (Lc - 1 - k) if reverse else k

    def advance(sr, si, t):
        return lrb * sr - lib * si + xr[_slab(t), :], lrb * si + lib * sr + xi[_slab(t), :]

    def pass1(k, c):
        return advance(c[0], c[1], step_of(k))

    er, ei = _steps(Lc, pass1, (zero, zero))
    pr, pi = _cpow(lr, li, Lc)
    prb, pib = jnp.broadcast_to(pr, (N_CHUNK, C)), jnp.broadcast_to(pi, (N_CHUNK, C))
    cr, ci = zero, zero
    for _ in range(N_CHUNK - 1):
        cr, ci = _shift_chunks(er + prb * cr - pib * ci, ei + prb * ci + pib * cr, reverse)

    if prev is None:
        def pass2(k, c):
            t = step_of(k)
            nr, ni = advance(c[0], c[1], t)
            xr[_slab(t), :] = nr
            xi[_slab(t), :] = ni
            return nr, ni

        _steps(Lc, pass2, (cr, ci))
        return None

    qr, qi = prev
    def pair(ar, ai, wr, wi, acc):
        return acc[0] + ar * wr + ai * wi, acc[1] + ai * wr - ar * wi

    def pass2(k, c):
        sr, si, accr, acci = c
        t = step_of(k)
        nr, ni = advance(sr, si, t)
        xr[_slab(t), :] = nr
        xi[_slab(t), :] = ni
        tp = t - 1 if reverse else t + 1
        accr, acci = pair(nr, ni, qr[_slab(tp), :], qi[_slab(tp), :], (accr, acci))
        return nr, ni, accr, acci

    sr, si, accr, acci = _steps(Lc - 1, pass2, (cr, ci, zero, zero))
    t = step_of(Lc - 1)
    nr, ni = advance(sr, si, t)
    xr[_slab(t), :] = nr
    xi[_slab(t), :] = ni
    edge = step_of(0)
    wr, wi = _shift_chunks(qr[_slab(edge), :], qi[_slab(edge), :], not reverse)
    return pair(nr, ni, wr, wi, (accr, acci))


def _ssm_dims(T):
    assert T % ROW_BLOCK == 0
    return T // N_CHUNK, T // ROW_BLOCK


def _group_mask():
    shape = (SSM_BLOCK_GROUPS * SSM_GROUP, SSM_BLOCK_GROUPS * SSM_STATE)
    rows = lax.broadcasted_iota(jnp.int32, shape, 0) >> GROUP_SHIFT
    cols = lax.broadcasted_iota(jnp.int32, shape, 1) >> STATE_SHIFT
    return rows == cols


def _expand_groups(compact):
    return jnp.where(_group_mask(), jnp.tile(compact, (SSM_BLOCK_GROUPS, 1)), 0.0).astype(BF16)


def _collect_groups(block):
    kept = jnp.where(_group_mask(), block, 0.0)
    out = kept[0:SSM_GROUP]
    for g in range(1, SSM_BLOCK_GROUPS):
        out = out + kept[g * SSM_GROUP:(g + 1) * SSM_GROUP]
    return out


def _ssm_specs(T, direction):
    SC = SSM_BLOCK_GROUPS * SSM_STATE
    col = pl.BlockSpec((T, LANES), lambda m: (0, m))
    mat = pl.BlockSpec((None, None, SSM_GROUP, SC), lambda m: (direction, m, 0, 0))
    vec = pl.BlockSpec((None, None, 1, SC), lambda m: (direction, m, 0, 0))
    return col, mat, vec


def _ssm_project_in(u_ref, bre, bim, xr, xi, nblk):
    for blk in range(nblk):
        rows = pl.ds(blk * ROW_BLOCK, ROW_BLOCK)
        ub = u_ref[rows, :].astype(BF16)
        xr[rows, :] = jnp.dot(ub, bre, preferred_element_type=F32)
        xi[rows, :] = jnp.dot(ub, bim, preferred_element_type=F32)


def _ssm_fwd(name, u_bf, mats, vecs, direction):
    T, W = u_bf.shape
    SC = SSM_BLOCK_GROUPS * SSM_STATE
    nb = W // LANES
    Lc, nblk = _ssm_dims(T)
    reverse = bool(direction)

    def kern(u_ref, bre_ref, bim_ref, cre_ref, cim_ref, lr_ref, li_ref, fr_ref, fi_ref, y_ref, xr, xi):
        _ssm_project_in(u_ref, _expand_groups(bre_ref[...]), _expand_groups(bim_ref[...]), xr, xi, nblk)
        _chunk_scan(xr, xi, lr_ref[...], li_ref[...], Lc, reverse)
        fr, fi = fr_ref[...], fi_ref[...]
        cre, cim = _expand_groups(cre_ref[...]), _expand_groups(cim_ref[...])
        for blk in range(nblk):
            rows = pl.ds(blk * ROW_BLOCK, ROW_BLOCK)
            sr, si = _cmul(fr, fi, xr[rows, :], xi[rows, :])
            y_ref[rows, :] = (lax.dot_general(sr.astype(BF16), cre, _NT, preferred_element_type=F32)
                              - lax.dot_general(si.astype(BF16), cim, _NT, preferred_element_type=F32))

    col, mat, vec = _ssm_specs(T, direction)
    return pl.pallas_call(
        kern, name=name, grid=(nb,),
        in_specs=[col, mat, mat, mat, mat, vec, vec, vec, vec],
        out_specs=col, out_shape=jax.ShapeDtypeStruct((T, W), F32),
        scratch_shapes=[pltpu.VMEM((T, SC), F32), pltpu.VMEM((T, SC), F32)],
        compiler_params=_params(("arbitrary",)),
    )(u_bf, *mats, *vecs)


def _ssm_bwd(name, u_bf, dy_bf, du_in, mats, vecs, direction, du_dtype):
    T, W = u_bf.shape
    SC = SSM_BLOCK_GROUPS * SSM_STATE
    nb = W // LANES
    Lc, nblk = _ssm_dims(T)
    reverse = bool(direction)

    def kern(u_ref, dy_ref, dui_ref, bre_ref, bim_ref, cre_ref, cim_ref, lr_ref, li_ref, fr_ref, fi_ref,
             du_ref, dbre_ref, dbim_ref, dcre_ref, dcim_ref, dlr_ref, dli_ref, dfr_ref, dfi_ref,
             xr, xi, gr, gi):
        lr, li, fr, fi = lr_ref[...], li_ref[...], fr_ref[...], fi_ref[...]
        bre, bim = _expand_groups(bre_ref[...]), _expand_groups(bim_ref[...])
        cre, cim = _expand_groups(cre_ref[...]), _expand_groups(cim_ref[...])
        _ssm_project_in(u_ref, bre, bim, xr, xi, nblk)
        _chunk_scan(xr, xi, lr, li, Lc, reverse)
        mre = jnp.zeros((LANES, SC), F32)
        mim = jnp.zeros((LANES, SC), F32)
        dfr = jnp.zeros((1, SC), F32)
        dfi = jnp.zeros((1, SC), F32)
        for blk in range(nblk):
            rows = pl.ds(blk * ROW_BLOCK, ROW_BLOCK)
            dy = dy_ref[rows, :].astype(BF16)
            x_r, x_i = xr[rows, :], xi[rows, :]
            g_r = jnp.dot(dy, cre, preferred_element_type=F32)
            g_i = -jnp.dot(dy, cim, preferred_element_type=F32)
            mre += lax.dot_general(dy, x_r.astype(BF16), _TN, preferred_element_type=F32)
            mim += lax.dot_general(dy, x_i.astype(BF16), _TN, preferred_element_type=F32)
            dfr += _colsum(g_r * x_r + g_i * x_i)
            dfi += _colsum(g_i * x_r - g_r * x_i)
            gr[rows, :] = fr * g_r + fi * g_i
            gi[rows, :] = fr * g_i - fi * g_r
        dcre_ref[...] = _collect_groups(fr * mre - fi * mim)
        dcim_ref[...] = _collect_groups(-(fr * mim + fi * mre))
        dfr_ref[...] = dfr
        dfi_ref[...] = dfi
        accr, acci = _chunk_scan(gr, gi, lr, -li, Lc, not reverse, prev=(xr, xi))
        dlr_ref[...] = _colsum(accr)
        dli_ref[...] = _colsum(acci)
        dbre = jnp.zeros((LANES, SC), F32)
        dbim = jnp.zeros((LANES, SC), F32)
        for blk in range(nblk):
            rows = pl.ds(blk * ROW_BLOCK, ROW_BLOCK)
            ub = u_ref[rows, :].astype(BF16)
            a_r, a_i = gr[rows, :].astype(BF16), gi[rows, :].astype(BF16)
            dbre += lax.dot_general(ub, a_r, _TN, preferred_element_type=F32)
            dbim += lax.dot_general(ub, a_i, _TN, preferred_element_type=F32)
            du_ref[rows, :] = (dui_ref[rows, :] + lax.dot_general(a_r, bre, _NT, preferred_element_type=F32)
                               + lax.dot_general(a_i, bim, _NT, preferred_element_type=F32)).astype(du_ref.dtype)
        dbre_ref[...] = _collect_groups(dbre)
        dbim_ref[...] = _collect_groups(dbim)

    col, mat, vec = _ssm_specs(T, direction)
    mat_out = pl.BlockSpec((None, SSM_GROUP, SC), lambda m: (m, 0, 0))
    vec_out = pl.BlockSpec((None, 1, SC), lambda m: (m, 0, 0))
    mat_shape = jax.ShapeDtypeStruct((nb, SSM_GROUP, SC), F32)
    vec_shape = jax.ShapeDtypeStruct((nb, 1, SC), F32)
    return pl.pallas_call(
        kern, name=name, grid=(nb,),
        in_specs=[col, col, col, mat, mat, mat, mat, vec, vec, vec, vec],
        out_specs=[col, mat_out, mat_out, mat_out, mat_out, vec_out, vec_out, vec_out, vec_out],
        out_shape=[jax.ShapeDtypeStruct((T, W), du_dtype), mat_shape, mat_shape, mat_shape, mat_shape,
                   vec_shape, vec_shape, vec_shape, vec_shape],
        scratch_shapes=[pltpu.VMEM((T, SC), F32)] * 4,
        compiler_params=_params(("arbitrary",)),
    )(u_bf, dy_bf, du_in, *mats, *vecs)


def _groups_side_by_side(p, channel_axis):
    d, G = p.shape[:2]
    nb = G // SSM_BLOCK_GROUPS
    v = p.reshape(d, nb, SSM_BLOCK_GROUPS, p.shape[2], p.shape[3])
    v = v.transpose(0, 1, 4, 2, 3) if channel_axis == 3 else v.transpose(0, 1, 3, 2, 4)
    return v.reshape(d, nb, SSM_GROUP, SSM_BLOCK_GROUPS * SSM_STATE)


def _groups_apart(m, channel_axis):
    d, nb = m.shape[:2]
    v = m.reshape(d, nb, SSM_GROUP, SSM_BLOCK_GROUPS, SSM_STATE)
    v = v.transpose(0, 1, 3, 4, 2) if channel_axis == 3 else v.transpose(0, 1, 3, 2, 4)
    return v.reshape((d, nb * SSM_BLOCK_GROUPS) + v.shape[3:])


def _conv_taps(ref, c, R, T):
    n = T // R
    r0 = pl.multiple_of(c * R, R)
    main = ref[pl.ds(r0, R), :]
    before = ref[pl.ds(pl.multiple_of(jnp.maximum(r0 - 8, 0), 8), 8), :]
    after = ref[pl.ds(pl.multiple_of(jnp.minimum(r0 + R, T - 8), 8), 8), :]
    ext = jnp.concatenate([jnp.where(c > 0, before, 0.0), main, jnp.where(c < n - 1, after, 0.0)], axis=0)
    m1 = pltpu.roll(ext, 1, 0)[8:8 + R]
    p1 = pltpu.roll(ext, R + 15, 0)[8:8 + R]
    return m1, main, p1, r0


def _conv_specs(T, F, cb):
    nj = F // cb
    lo = lambda rows: pl.BlockSpec((rows, cb), lambda j: (0, j))
    hi = lambda rows: pl.BlockSpec((rows, cb), lambda j: (0, j + nj))
    return nj, lo, hi


def _convact_fwd(up, conv_w, conv_b, F):
    T = up.shape[0]
    cb = _pick(F, (256, 128))
    R = _pick(T, (256, 128, 64))
    nj, lo, hi = _conv_specs(T, F, cb)

    def kern(uv, ug, wv, wg, bv, bg, o_ref):
        def body(c, carry):
            v1, v0, v2, r0 = _conv_taps(uv, c, R, T)
            g1, g0, g2, _ = _conv_taps(ug, c, R, T)
            val = v1 * wv[0:1, :] + v0 * wv[1:2, :] + v2 * wv[2:3, :] + bv[...]
            gate = g1 * wg[0:1, :] + g0 * wg[1:2, :] + g2 * wg[2:3, :] + bg[...]
            half = 0.5 * gate
            o_ref[pl.ds(r0, R), :] = (val * (half * jnp.tanh(half) + half)).astype(o_ref.dtype)
            return carry

        lax.fori_loop(0, T // R, body, 0)

    return pl.pallas_call(
        kern, name="convact_fwd", grid=(nj,),
        in_specs=[lo(T), hi(T), lo(3), hi(3), lo(1), hi(1)],
        out_specs=lo(T), out_shape=jax.ShapeDtypeStruct((T, F), BF16),
        compiler_params=_params(("arbitrary",)),
    )(up, up, conv_w, conv_w, conv_b, conv_b)


def _convact_bwd(up, dact, conv_w, conv_b, F):
    T = up.shape[0]
    cb = _pick(F, (256, 128))
    R = _pick(T, (256, 128, 64))
    nj, lo, hi = _conv_specs(T, F, cb)

    def kern(uv, ug, da, wv, wg, bv, bg, dup, dwv, dwg, dbv, dbg, sv, sg):
        zero = jnp.zeros((1, cb), F32)

        def pass_a(c, acc):
            v1, v0, v2, r0 = _conv_taps(uv, c, R, T)
            g1, g0, g2, _ = _conv_taps(ug, c, R, T)
            val = v1 * wv[0:1, :] + v0 * wv[1:2, :] + v2 * wv[2:3, :] + bv[...]
            gate = g1 * wg[0:1, :] + g0 * wg[1:2, :] + g2 * wg[2:3, :] + bg[...]
            sig = _sigmoid(gate)
            d = da[pl.ds(r0, R), :]
            silu = gate * sig
            dval = d * silu
            dgate = d * val * (sig + silu - silu * sig)
            sv[pl.ds(r0, R), :] = dval
            sg[pl.ds(r0, R), :] = dgate
            terms = (dval * v1, dval * v0, dval * v2, dval, dgate * g1, dgate * g0, dgate * g2, dgate)
            return tuple(a + _colsum(t) for a, t in zip(acc, terms))

        acc = lax.fori_loop(0, T // R, pass_a, (zero,) * 8)
        for k in range(3):
            dwv[k:k + 1, :] = acc[k]
            dwg[k:k + 1, :] = acc[4 + k]
        dbv[...] = acc[3]
        dbg[...] = acc[7]

        def pass_b(c, carry):
            v1, v0, v2, r0 = _conv_taps(sv, c, R, T)
            g1, g0, g2, _ = _conv_taps(sg, c, R, T)
            dup[0, pl.ds(r0, R), :] = (v2 * wv[0:1, :] + v0 * wv[1:2, :] + v1 * wv[2:3, :]).astype(dup.dtype)
            dup[1, pl.ds(r0, R), :] = (g2 * wg[0:1, :] + g0 * wg[1:2, :] + g1 * wg[2:3, :]).astype(dup.dtype)
            return carry

        lax.fori_loop(0, T // R, pass_b, 0)

    dup, dwv, dwg, dbv, dbg = pl.pallas_call(
        kern, name="convact_bwd", grid=(nj,),
        in_specs=[lo(T), hi(T), lo(T), lo(3), hi(3), lo(1), hi(1)],
        out_specs=[pl.BlockSpec((2, T, cb), lambda j: (0, 0, j)), lo(3), lo(3), lo(1), lo(1)],
        out_shape=[jax.ShapeDtypeStruct((2, T, F), BF16),
                   jax.ShapeDtypeStruct((3, F), F32), jax.ShapeDtypeStruct((3, F), F32),
                   jax.ShapeDtypeStruct((1, F), F32), jax.ShapeDtypeStruct((1, F), F32)],
        scratch_shapes=[pltpu.VMEM((T, cb), F32), pltpu.VMEM((T, cb), F32)],
        compiler_params=_params(("arbitrary",)),
    )(up, up, dact, conv_w, conv_w, conv_b, conv_b)
    return dup, (dwv, dwg), jnp.concatenate([dbv, dbg], axis=1)


def _peers(x, y, c, with_self=False):
    out = []
    for k in range(0 if with_self else 1, N_DEV):
        px = 1 - x if k & 4 else x
        py = 1 - y if k & 2 else y
        pc = 1 - c if k & 1 else c
        out.append((k, (px, py, pc), 4 * px + 2 * py + pc))
    return out


def _exchange_start(name, bufs, modes, after):
    handles, token = _exchange_start_groups(name, [(bufs, modes)], after)
    return handles[0], token


def _exchange_start_groups(name, groups, after):
    bufs = [b for g, _ in groups for b in g]
    modes = [m for _, ms in groups for m in ms]
    first = [int(v) for v in np.cumsum([0] + [len(g) for g, _ in groups])]
    n, ng = len(bufs), len(groups)
    lands = [lax.empty((N_DEV,) + tuple(b.shape[-2:]), b.dtype) for b in bufs]

    def body(*refs):
        ins, land_in = refs[:n], refs[n:2 * n]
        sems = refs[2 * n + 1:2 * n + 1 + 2 * ng]
        token = refs[-1]
        x, y, c = lax.axis_index("x"), lax.axis_index("y"), lax.axis_index("c")
        me = 4 * x + 2 * y + c
        for g in range(ng):
            for b in range(first[g], first[g + 1]):
                for k, peer, slot in _peers(x, y, c, with_self=True):
                    sem = (b - first[g]) * N_DEV + k
                    pltpu.make_async_remote_copy(
                        src_ref=ins[b] if modes[b] == 'gather' else ins[b].at[slot], dst_ref=land_in[b].at[me],
                        send_sem=sems[2 * g].at[sem], recv_sem=sems[2 * g + 1].at[sem],
                        device_id=peer, device_id_type=MESH_ID).start()
        token[...] = jnp.zeros_like(token)

    hbm = pl.BlockSpec(memory_space=pltpu.HBM)
    sem_spec = pl.BlockSpec(memory_space=pltpu.SEMAPHORE)
    operands = [pltpu.with_memory_space_constraint(a, pltpu.HBM) for a in bufs + lands]
    sem_shapes = [pltpu.SemaphoreType.DMA((len(g) * N_DEV,)) for g, _ in groups for _ in range(2)]
    res = pl.pallas_call(
        body, name=name,
        out_shape=(*sem_shapes, *[pltpu.HBM(a.shape, a.dtype) for a in operands], jax.ShapeDtypeStruct((8, LANES), F32)),
        in_specs=[hbm] * (2 * n) + [pl.BlockSpec(memory_space=pl.ANY)],
        out_specs=(*([sem_spec] * (2 * ng)), *([hbm] * (2 * n)), pl.BlockSpec(memory_space=pltpu.VMEM)),
        input_output_aliases={i: 2 * ng + i for i in range(2 * n)},
        compiler_params=pltpu.CompilerParams(has_side_effects=pltpu.SideEffectType.DATAFLOW_SIDE_EFFECTING),
    )(*operands, after)
    srcs, landed = res[2 * ng:2 * ng + n], res[2 * ng + n:2 * ng + 2 * n]
    handles = [(res[2 * g], res[2 * g + 1], list(srcs[first[g]:first[g + 1]]), list(landed[first[g]:first[g + 1]]),
                tuple(modes[first[g]:first[g + 1]])) for g in range(ng)]
    return handles, res[-1]


def _exchange_wait(name, handle, after):
    send_sems, recv_sems, srcs, lands, modes = handle
    n = len(srcs)

    def body(*refs):
        src_in, land_in = refs[:n], refs[n:2 * n]
        send_ref, recv_ref = refs[2 * n], refs[2 * n + 1]
        x, y, c = lax.axis_index("x"), lax.axis_index("y"), lax.axis_index("c")
        for b in range(n):
            for k, peer, slot in _peers(x, y, c, with_self=True):
                sem = b * N_DEV + k
                copy = pltpu.make_async_remote_copy(
                    src_ref=src_in[b] if modes[b] == 'gather' else src_in[b].at[slot], dst_ref=land_in[b].at[slot],
                    send_sem=send_ref.at[sem], recv_sem=recv_ref.at[sem],
                    device_id=peer, device_id_type=MESH_ID)
                copy.wait_send()
                copy.wait_recv()

    hbm = pl.BlockSpec(memory_space=pltpu.HBM)
    sem_spec = pl.BlockSpec(memory_space=pltpu.SEMAPHORE)
    res = pl.pallas_call(
        body, name=name,
        out_shape=tuple(pltpu.HBM(a.shape, a.dtype) for a in srcs + lands),
        in_specs=[hbm] * (2 * n) + [sem_spec, sem_spec, pl.BlockSpec(memory_space=pl.ANY)],
        out_specs=tuple([hbm] * (2 * n)),
        input_output_aliases={i: i for i in range(2 * n)},
        compiler_params=pltpu.CompilerParams(has_side_effects=pltpu.SideEffectType.DATAFLOW_SIDE_EFFECTING),
    )(*srcs, *lands, send_sems, recv_sems, after)
    return list(res[n:])


def _adamw(w, g, m, v):
    m2 = ADAM_B1 * m + (1.0 - ADAM_B1) * g
    v2 = ADAM_B2 * v + (1.0 - ADAM_B2) * (g * g)
    m_hat = m2 / (1.0 - ADAM_B1 ** ADAM_STEP)
    v_hat = v2 / (1.0 - ADAM_B2 ** ADAM_STEP)
    return -ADAM_LR * (m_hat / (jnp.sqrt(v_hat) + ADAM_EPS) + ADAM_WD * w), m2, v2


def _sum_adamw(name, landed, w, m, v):
    R, C = w.shape
    row_bytes = N_DEV * (-(-C // LANES) * LANES) * landed.dtype.itemsize
    tiles = [d for d in range(16, R, 16) if R % d == 0 and d * row_bytes <= ADAM_TILE_BYTES]
    tr = max(tiles) if tiles and R * row_bytes > ADAM_TILE_BYTES else R

    def kern(x_ref, w_ref, m_ref, v_ref, g_out, d_out, m_out, v_out):
        g = x_ref[0].astype(F32)
        for q in range(1, N_DEV):
            g = g + x_ref[q].astype(F32)
        g_out[...] = g
        d_out[...], m_out[...], v_out[...] = _adamw(w_ref[...], g, m_ref[...], v_ref[...])

    blk = pl.BlockSpec((tr, C), lambda i: (i, 0))
    return pl.pallas_call(
        kern, name=name, grid=(R // tr,),
        in_specs=[pl.BlockSpec((N_DEV, tr, C), lambda i: (0, i, 0)), blk, blk, blk],
        out_specs=[blk] * 4, out_shape=[jax.ShapeDtypeStruct((R, C), F32)] * 4,
        compiler_params=_params(("arbitrary",)),
    )(landed, w, m, v)


def _sum_adamw_many(name, items):
    n = len(items)

    def kern(*refs):
        ins, outs = refs[:4 * n], refs[4 * n:]
        for k in range(n):
            x_ref, w_ref, m_ref, v_ref = ins[4 * k:4 * k + 4]
            g = x_ref[0].astype(F32)
            for q in range(1, N_DEV):
                g = g + x_ref[q].astype(F32)
            outs[4 * k][...] = g
            outs[4 * k + 1][...], outs[4 * k + 2][...], outs[4 * k + 3][...] = _adamw(w_ref[...], g, m_ref[...], v_ref[...])

    out_shape = [jax.ShapeDtypeStruct(w.shape, F32) for _, w, _, _ in items for _ in range(4)]
    res = pl.pallas_call(kern, name=name, out_shape=out_shape)(*[a for item in items for a in item])
    return [res[4 * k:4 * k + 4] for k in range(n)]


def _view2d(a):
    if a.ndim == 1:
        return a.reshape(1, -1)
    return a.reshape(-1, a.shape[-1])


def kernel(x, norm_mix_g, w_in, pool_w, pool_scale, ssm_log_neg_a_re, ssm_a_im, ssm_log_dt, ssm_b_re, ssm_b_im, ssm_c_re, ssm_c_im, ssm_d, glu_w, glu_b, out_norm_pool_g, out_norm_ssm_g, w_out, norm_ffn_g, w_up, conv_w, conv_b, w_down, final_norm_g, loss_target, m_norm_mix_g, m_w_in, m_pool_w, m_pool_scale, m_ssm_log_neg_a_re, m_ssm_a_im, m_ssm_log_dt, m_ssm_b_re, m_ssm_b_im, m_ssm_c_re, m_ssm_c_im, m_ssm_d, m_glu_w, m_glu_b, m_out_norm_pool_g, m_out_norm_ssm_g, m_w_out, m_norm_ffn_g, m_w_up, m_conv_w, m_conv_b, m_w_down, m_final_norm_g, v_norm_mix_g, v_w_in, v_pool_w, v_pool_scale, v_ssm_log_neg_a_re, v_ssm_a_im, v_ssm_log_dt, v_ssm_b_re, v_ssm_b_im, v_ssm_c_re, v_ssm_c_im, v_ssm_d, v_glu_w, v_glu_b, v_out_norm_pool_g, v_out_norm_ssm_g, v_w_out, v_norm_ffn_g, v_w_up, v_conv_w, v_conv_b, v_w_down, v_final_norm_g):
    given = dict(locals())
    weights = {n: given[n] for n in WEIGHTS}
    mom1 = {n: given["m_" + n] for n in WEIGHTS}
    mom2 = {n: given["v_" + n] for n in WEIGHTS}

    xs = x[0]
    tgt = loss_target[0]
    T, D = xs.shape
    DP = len(POOL_WINDOWS) * POOL_GROUP
    DS = D - DP
    G = DS // SSM_GROUP
    N, H = SSM_STATE, SSM_GROUP
    F2 = w_up.shape[1] * N_DEV
    F = F2 // 2
    fs = w_up.shape[1]
    row = lambda a: a.reshape(1, -1)

    in_handle, token = _exchange_start("gather_in_start", [w_in.astype(BF16)], ['gather'], after=w_in)
    later = lambda w: (w + token[0, 0]).astype(BF16)
    gather_groups = [([later(glu_w), later(w_out)], ['gather'] * 2),
                     ([later(w_up.T), conv_w], ['gather'] * 2), ([later(w_down)], ['gather'])]
    (mix_handle, up_handle, down_handle), token = _exchange_start_groups("gather_rest_start", gather_groups, token)
    CB = row(conv_b)

    g1, g2, g3 = row(norm_mix_g), row(norm_ffn_g), row(final_norm_g)
    gp, gs = row(out_norm_pool_g), row(out_norm_ssm_g)
    (xn,) = _rowmap("norm_mix", _rms_fwd, [xs], [g1 + token[0:1, 0:1]], [(D, BF16)])
    W_in = _exchange_wait("gather_in_wait", in_handle, xn)[0].reshape(D, D)
    halves = lambda r: (r[:, :DP], r[:, DP:])
    u, u_s5 = _mm_rows("proj_in", xn, W_in, 'nn', halves, [], [], [(DP, F32), (DS, F32)])
    tmp = _pick(T, (512, 256, 128))
    pool_w_bf = pool_w.astype(BF16)
    ypn = _pool_fwd(u, pool_w_bf, row(pool_scale), gp, tmp)

    u_ssm = _to_chunk_rows(u_s5)
    u_ssm_bf = u_ssm.astype(BF16)
    a_rows = (ssm_log_neg_a_re.reshape(2 * G, N), ssm_a_im.reshape(2 * G, N), ssm_log_dt.reshape(2 * G, 1))
    disc = _ssm_params_fwd(*a_rows)
    nb = G // SSM_BLOCK_GROUPS
    vecs = [d.reshape(2, nb, 1, SSM_BLOCK_GROUPS * N) for d in disc]
    mats = [_groups_side_by_side(ssm_b_re, 3), _groups_side_by_side(ssm_b_im, 3),
            _groups_side_by_side(ssm_c_re, 2), _groups_side_by_side(ssm_c_im, 2)]
    y_dir = [_ssm_fwd("ssm_fwd_%d" % d, u_ssm_bf, mats, vecs, d) for d in range(2)]

    def mix_post(yf, yb, us, d, gw, gb, g):
        y = yf + yb + d * us
        z = _gelu(y)
        gate = _sigmoid(jnp.dot(z.astype(BF16), gw, preferred_element_type=F32) + gb)
        return _rms_fwd(z * gate, g), y

    mix_landed = _exchange_wait("gather_mix_wait", mix_handle, y_dir[1])
    W_glu = mix_landed[0].reshape(DS, DS)
    W_out = mix_landed[1].reshape(D, D)
    ysn, y_ssm = _rowmap("ssm_post", mix_post, [y_dir[0], y_dir[1], u_ssm], [row(ssm_d), W_glu, row(glu_b), gs],
                         [(DS, BF16), (DS, F32)])
    ycat = jnp.concatenate([ypn, _from_chunk_rows(ysn)], axis=1)
    h1, hn = _mm_rows("proj_out", ycat, W_out, 'nn', lambda h, g: (h, _rms_fwd(h, g)), [], [g2],
                      [(D, F32), (D, BF16)], add=xs)
    up_landed = _exchange_wait("gather_up_wait", up_handle, hn)
    W_up_t = up_landed[0].reshape(F2, D)
    CW = up_landed[1].transpose(1, 0, 2).reshape(3, F2)
    up = _mm("ffn_up", hn, W_up_t, 'nt', F32)
    act = _convact_fwd(up, CW, CB, F)
    W_down = _exchange_wait("gather_down_wait", down_handle, act)[0].reshape(F, D)

    def head(h, t, g):
        r = lax.rsqrt(_rowmean(h * h) + EPS)
        hh = h * r
        e = hh * g - t
        loss = 0.5 * jnp.sum(_rowmean(e * e), keepdims=True)
        dy = e * (1.0 / D)
        dxh = dy * g
        dh = r * (dxh - hh * _rowmean(dxh * hh))
        return dh, dh, jnp.broadcast_to(loss, (1, LANES)), _colsum(dy * hh)

    dh2, dh2_bf, loss_acc, dg3 = _mm_rows("ffn_down", act, W_down, 'nn', head, [tgt], [g3],
                                          [(D, F32), (D, BF16)], [(1, LANES), (1, D)], add=h1)

    dact = _mm("ffn_down_dx", dh2_bf, W_down, 'nt', F32)
    dW_down = _mm("ffn_down_dw", act, dh2_bf, 'tn', BF16)
    dup, dCW, dCB = _convact_bwd(up, dact, CW, CB, F)
    dhn_val = _mm("ffn_up_dx_val", dup, W_up_t, 'nn', F32, b_rows=(2, 0), a_plane=0)
    dW_up_t = _mm("ffn_up_dw", dup, hn, 'tn', BF16).reshape(F2, D)

    def shards(full_grad):
        return full_grad.reshape((N_DEV, full_grad.shape[0] // N_DEV) + full_grad.shape[1:])

    half = N_DEV // 2
    dCW_sh = jnp.concatenate([h.reshape(3, half, fs).transpose(1, 0, 2) for h in dCW], axis=0)
    ffn_sent = [shards(dW_down), shards(dW_up_t), dCW_sh]
    ffn_handle, token = _exchange_start("scatter_ffn_start", ffn_sent, ['scatter'] * 3, after=dhn_val)
    g2_late = g2 + token[0:1, 0:1]

    def norm_bwd_add(dy, hx, res, g):
        dx, dg = _rms_bwd(hx, g, dy)
        return dx + res, dg

    def norm_bwd_add2(dy, hx, res, g):
        dx, dg = _rms_bwd(hx, g, dy)
        return dx + res, dx + res, dg

    dh1, dh1_bf, dg2 = _mm_rows("ffn_up_dx_gate", dup, W_up_t, 'nn', norm_bwd_add2, [h1, dh2], [g2_late],
                                [(D, F32), (D, BF16)], [(1, D)], add=dhn_val, b_rows=(2, 1), a_plane=1)
    dycat, d_ysn = _mm_rows("proj_out_dx", dh1_bf, W_out, 'nt', halves, [], [], [(DP, F32), (DS, F32)])
    dW_out = _mm("proj_out_dw", ycat, dh1_bf, 'tn', BF16)

    out_sent = [shards(dW_out)]
    out_handle, token = _exchange_start("scatter_out_start", out_sent, ['scatter'], after=dycat)
    q, dpooled, dpool_w, dpool_scale, dgp = _pool_bwd_rows(u, dycat, pool_w_bf, row(pool_scale),
                                                           gp + token[0:1, 0:1], tmp)

    def mix_post_bwd(y, us, dyn, d, gw, gb, g):
        z, dz_dy = _gelu_and_grad(y)
        gz = z.astype(BF16)
        sig = _sigmoid(jnp.dot(gz, gw, preferred_element_type=F32) + gb)
        dys, dg = _rms_bwd(z * sig, g, dyn)
        dgl = dys * z * sig * (1.0 - sig)
        dgl_bf = dgl.astype(BF16)
        dz = dys * sig + lax.dot_general(dgl_bf, gw, (((1,), (1,)), ((), ())), preferred_element_type=F32)
        dgw = lax.dot_general(gz, dgl_bf, (((0,), (0,)), ((), ())), preferred_element_type=F32)
        dyv = dz * dz_dy
        return dyv, dyv * d, dgw, _colsum(dgl), _colsum(dyv * us), dg

    dyv, du_dir, dglu_w, dglu_b, dssm_d, dgs = _rowmap(
        "ssm_post_bwd", mix_post_bwd, [y_ssm, u_ssm, _to_chunk_rows(d_ysn)], [row(ssm_d), W_glu, row(glu_b), gs],
        [(DS, BF16), (DS, F32)], [(DS, DS), (1, DS), (1, DS), (1, DS)])

    du_ssm = du_dir
    dB, dC, dvec = [], [], []
    for d in range(2):
        res = _ssm_bwd("ssm_bwd_%d" % d, u_ssm_bf, dyv, du_ssm, mats, vecs, d, BF16 if d == 1 else F32)
        du_ssm = res[0]
        dB.append(res[1:3])
        dC.append(res[3:5])
        dvec.append(res[5:9])
    grad_b = [_groups_apart(jnp.stack([dB[0][k], dB[1][k]]), 3) for k in range(2)]
    grad_c = [_groups_apart(jnp.stack([dC[0][k], dC[1][k]]), 2) for k in range(2)]
    cots = [jnp.stack([dvec[0][k], dvec[1][k]]).reshape(2 * G, N) for k in range(4)]
    d_a_re, d_a_im, d_log_dt = _ssm_params_bwd(*a_rows, cots)

    rep_grads = {
        'pool_w': dpool_w, 'pool_scale': dpool_scale, 'ssm_log_neg_a_re': d_a_re, 'ssm_a_im': d_a_im,
        'ssm_log_dt': d_log_dt, 'ssm_b_re': grad_b[0], 'ssm_b_im': grad_b[1], 'ssm_c_re': grad_c[0], 'ssm_c_im': grad_c[1],
        'ssm_d': dssm_d, 'glu_b': dglu_b, 'out_norm_pool_g': dgp, 'out_norm_ssm_g': dgs, 'norm_ffn_g': dg2,
        'conv_b': dCB, 'final_norm_g': dg3}
    wide = ('pool_w', 'ssm_b_re', 'ssm_b_im', 'ssm_c_re', 'ssm_c_im')

    def narrow(n):
        return weights[n].ndim == 4 and weights[n].shape[-1] < LANES // 2

    def travel(n):
        g = rep_grads[n]
        if narrow(n):
            g = jnp.swapaxes(g, 2, 3)
        if n in wide and g.shape[-1] % LANES == 0:
            return _view2d(g).astype(BF16)
        if n in wide:
            return g.reshape(-1, PACK_W if g.size % PACK_W == 0 else LANES).astype(BF16)
        return _view2d(g.reshape(weights[n].shape))

    small = tuple(n for n in REPLICATED if n != 'norm_mix_g')
    small_sent = [shards(dglu_w.astype(BF16))] + [travel(n) for n in small] + [loss_acc]
    small_handle, token = _exchange_start("reduce_small_start", small_sent, ['scatter'] + ['gather'] * (len(small) + 1),
                                          after=d_a_re)

    du = _pool_bwd_band(q, dpooled, _from_chunk_rows(du_ssm), token, tmp)
    dW_in = _mm("proj_in_dw", xn, du, 'tn', BF16)
    in_handle, token = _exchange_start("scatter_in_start", [shards(dW_in)], ['scatter'], after=dW_in)
    dx, dg1 = _mm_rows("proj_in_dx", du, W_in + token[0, 0].astype(BF16), 'nt', norm_bwd_add, [xs, dh1], [g1],
                       [(D, F32)], [(1, D)])
    rep_grads['norm_mix_g'] = dg1
    last_handle, token = _exchange_start("reduce_last_start", [travel('norm_mix_g')], ['gather'], after=dx)

    grads, delta, new_m, new_v = {}, {}, {}, {}

    def update(n, landed_n):
        shape = weights[n].shape
        if narrow(n):
            swapped = lambda a: _view2d(jnp.swapaxes(a, 2, 3))
            landed_n = landed_n.reshape((N_DEV,) + swapped(weights[n]).shape)
            res = _sum_adamw("adamw_" + n, landed_n, swapped(weights[n]), swapped(mom1[n]), swapped(mom2[n]))
            back = lambda r: jnp.swapaxes(r.reshape(shape[:2] + (shape[3], shape[2])), 2, 3)
            grads[n], delta[n], new_m[n], new_v[n] = [back(r) for r in res]
            return
        if n == 'w_up':
            res = _sum_adamw("adamw_" + n, landed_n, weights[n].T, mom1[n].T, mom2[n].T)
            grads[n], delta[n], new_m[n], new_v[n] = [r.T for r in res]
            return
        landed_n = landed_n.reshape((N_DEV,) + _view2d(weights[n]).shape)
        res = _sum_adamw("adamw_" + n, landed_n, _view2d(weights[n]), _view2d(mom1[n]), _view2d(mom2[n]))
        grads[n], delta[n], new_m[n], new_v[n] = [r.reshape(shape) for r in res]

    def update_many(name, named_landed):
        items = [(l.reshape((N_DEV,) + _view2d(weights[n]).shape), _view2d(weights[n]), _view2d(mom1[n]), _view2d(mom2[n]))
                 for n, l in named_landed]
        for (n, _), res in zip(named_landed, _sum_adamw_many(name, items)):
            grads[n], delta[n], new_m[n], new_v[n] = [r.reshape(weights[n].shape) for r in res]

    ffn_landed = _exchange_wait("scatter_ffn_wait", ffn_handle, token)
    for n, l in zip(('w_down', 'w_up'), ffn_landed):
        update(n, l)
    (out_landed,) = _exchange_wait("scatter_out_wait", out_handle, token)
    update('w_out', out_landed)
    small_landed = _exchange_wait("reduce_small_wait", small_handle, token)
    tiny = [('conv_w', ffn_landed[2])]
    for n, l in zip(('glu_w',) + small, small_landed):
        if weights[n].size <= ADAM_TINY:
            tiny.append((n, l))
        else:
            update(n, l)
    update_many("adamw_tiny", tiny)

    def loss_sum(parts_ref, o_ref):
        s = parts_ref[0]
        for q in range(1, N_DEV):
            s = s + parts_ref[q]
        o_ref[...] = s

    loss = pl.pallas_call(loss_sum, name="loss_sum", out_shape=jax.ShapeDtypeStruct((1, LANES), F32))(small_landed[-1])[0, 0]

    update('w_in', _exchange_wait("scatter_in_wait", in_handle, grads[small[-1]])[0])
    update('norm_mix_g', _exchange_wait("reduce_last_wait", last_handle, grads['w_in'])[0])

    return (loss, dx[None], *[grads[n] for n in WEIGHTS], *[delta[n] for n in WEIGHTS],
            *[new_m[n] for n in WEIGHTS], *[new_v[n] for n in WEIGHTS])
```

```python
import functools

import jax
import jax.numpy as jnp
import numpy as np
from jax import lax
from jax.experimental import pallas as pl
from jax.experimental.pallas import tpu as pltpu

F32 = jnp.float32
BF16 = jnp.bfloat16
MESH_ID = pl.DeviceIdType.MESH

N_DEV = 8
EPS = 1e-6
POOL_WINDOWS = (2, 4, 8, 16)
POOL_GROUP = 128
POOL_PAD = 64
SSM_GROUP = 16
SSM_STATE = 64
SSM_BLOCK_GROUPS = 8
N_CHUNK = 8
ROW_BLOCK = 512
SCAN_UNROLL = 8
GROUP_SHIFT = 4
STATE_SHIFT = 6
_NT = (((1,), (1,)), ((), ()))
_TN = (((0,), (0,)), ((), ()))
LANES = 128
PACK_W = 1024
VMEM_LIMIT = 56 * 1024 * 1024
MM_VMEM_BUDGET = 40 * 1024 * 1024
MM_TILE_CAP = 1408
MM_ROWS_TILES = (512, 1024)
ADAM_TILE_BYTES = 4 * 1024 * 1024
ADAM_TINY = 8192

ADAM_LR = 0.001
ADAM_B1 = 0.9
ADAM_B2 = 0.999
ADAM_EPS = 1e-08
ADAM_WD = 0.01
ADAM_STEP = 10

WEIGHTS = ['norm_mix_g', 'w_in', 'pool_w', 'pool_scale', 'ssm_log_neg_a_re', 'ssm_a_im', 'ssm_log_dt',
           'ssm_b_re', 'ssm_b_im', 'ssm_c_re', 'ssm_c_im', 'ssm_d', 'glu_w', 'glu_b', 'out_norm_pool_g',
           'out_norm_ssm_g', 'w_out', 'norm_ffn_g', 'w_up', 'conv_w', 'conv_b', 'w_down', 'final_norm_g']
SHARDED = ('w_in', 'w_out', 'w_down', 'glu_w', 'w_up', 'conv_w')
REPLICATED = tuple(n for n in WEIGHTS if n not in SHARDED)


def _pick(n, prefs):
    for p in prefs:
        if n % p == 0:
            return p
    return n


def _params(sem, vmem=None):
    return pltpu.CompilerParams(dimension_semantics=sem, vmem_limit_bytes=vmem or VMEM_LIMIT)


def _tiles(n, cap):
    return [d for d in range(LANES, min(n, cap) + 1, LANES) if n % d == 0] or [n]


def _mm(name, a, b, mode, out_dtype, add=None, b_rows=(1, 0), a_plane=None):
    batch = a.shape[0] if a.ndim == 3 and a_plane is None else None
    if mode == 'nn':
        (M, K), N = a.shape[-2:], b.shape[1]
        assert b.shape[0] == K * b_rows[0]
    elif mode == 'nt':
        (M, K), (N, _) = a.shape[-2:], b.shape
    else:
        (K, M), (_, N) = a.shape[-2:], b.shape
    dims = {'nn': (((1,), (0,)), ((), ())), 'nt': (((1,), (1,)), ((), ())), 'tn': (((0,), (0,)), ((), ()))}[mode]
    sa, sb, so = a.dtype.itemsize, b.dtype.itemsize, jnp.dtype(out_dtype).itemsize
    best = None
    for tm in _tiles(M, MM_TILE_CAP):
        for tn in _tiles(N, MM_TILE_CAP):
            need = 2 * (tm * K * sa + tn * K * sb + tm * tn * (so + (4 if add is not None else 0)))
            if need <= MM_VMEM_BUDGET:
                key = (tm * tn / (tm + tn), tm * tn)
                if best is None or key > best[0]:
                    best = (key, tm, tn)
    _, tm, tn = best
    rows_inner = a.size * sa * (N // tn) + b.size * sb < a.size * sa + b.size * sb * (M // tm)

    def kern(*refs):
        a_ref, b_ref = refs[:2]
        o_ref = refs[-1]
        r = lax.dot_general(a_ref[...].astype(BF16), b_ref[...].astype(BF16), dims, preferred_element_type=F32)
        if add is not None:
            r = r + refs[2][...]
        o_ref[...] = r.astype(o_ref.dtype)

    if rows_inner:
        grid = (N // tn, M // tm)
        ij = lambda g0, g1: (g1, g0)
    else:
        grid = (M // tm, N // tn)
        ij = lambda g0, g1: (g0, g1)

    def spec(block, index, batched=False):
        if batch is None:
            return pl.BlockSpec(block, lambda g0, g1: index(*ij(g0, g1)))
        if batched:
            return pl.BlockSpec((None,) + block, lambda p, g0, g1: (p,) + index(*ij(g0, g1)))
        return pl.BlockSpec(block, lambda p, g0, g1: index(*ij(g0, g1)))

    if a_plane is not None:
        assert mode != 'tn'
        a_spec = pl.BlockSpec((None, tm, K), lambda g0, g1: (a_plane, ij(g0, g1)[0], 0))
    else:
        a_spec = (spec((K, tm), lambda i, j: (0, i), True) if mode == 'tn' else spec((tm, K), lambda i, j: (i, 0), True))
    b_spec = (spec((tn, K), lambda i, j: (j, 0)) if mode == 'nt' else spec((K, tn), lambda i, j: (b_rows[1], j)))
    in_specs = [a_spec, b_spec]
    args = [a, b]
    if add is not None:
        in_specs.append(spec((tm, tn), lambda i, j: (i, j), True))
        args.append(add)
    lead = () if batch is None else (batch,)
    return pl.pallas_call(
        kern, name=name, grid=lead + grid, in_specs=in_specs,
        out_specs=spec((tm, tn), lambda i, j: (i, j), True),
        out_shape=jax.ShapeDtypeStruct(lead + (M, N), out_dtype),
        compiler_params=_params(("parallel",) * (len(lead) + 2)),
    )(*args)


def _store_rows_and_sums(i, outs, res, n_row):
    for o, v in zip(outs[:n_row], res[:n_row]):
        o[...] = v.astype(o.dtype)
    for o, v in zip(outs[n_row:], res[n_row:]):
        @pl.when(i == 0)
        def _(o=o, v=v):
            o[...] = v

        @pl.when(i != 0)
        def _(o=o, v=v):
            o[...] += v


def _mm_rows(name, a, b, mode, fn, rows, fulls, out_rows, out_accs=(), add=None, b_rows=(1, 0), a_plane=None):
    M, K = a.shape[-2:]
    N = b.shape[1] if mode == 'nn' else b.shape[0]
    assert b.shape[0] == (K * b_rows[0] if mode == 'nn' else N)
    dims = _NT if mode == 'nt' else (((1,), (0,)), ((), ()))
    n_extra, n_row = len(rows) + len(fulls), len(out_rows)
    row_bytes = (K * a.dtype.itemsize + (4 * N if add is not None else 0) + sum(4 * r.shape[1] for r in rows)
                 + sum(c * jnp.dtype(dt).itemsize for c, dt in out_rows))
    fits = [t for t in MM_ROWS_TILES if M % t == 0 and 2 * (t * row_bytes + K * N * b.dtype.itemsize) <= MM_VMEM_BUDGET]
    tm = max(fits) if fits else min(MM_ROWS_TILES)

    def kern(*refs):
        i = pl.program_id(0)
        r = lax.dot_general(refs[0][...].astype(BF16), refs[1][...].astype(BF16), dims, preferred_element_type=F32)
        pos = 2
        if add is not None:
            r = r + refs[pos][...]
            pos += 1
        res = fn(r, *[x[...] for x in refs[pos:pos + n_extra]])
        _store_rows_and_sums(i, refs[pos + n_extra:], res, n_row)

    def full_spec(shape):
        nd = len(shape)
        return pl.BlockSpec(tuple(shape), lambda i: (0,) * nd)

    a_spec = (pl.BlockSpec((tm, K), lambda i: (i, 0)) if a_plane is None
              else pl.BlockSpec((None, tm, K), lambda i: (a_plane, i, 0)))
    b_spec = pl.BlockSpec((K, N), lambda i: (b_rows[1], 0)) if mode == 'nn' else pl.BlockSpec((N, K), lambda i: (0, 0))
    ins = [_row_in(r, tm, M) for r in ([add] if add is not None else []) + list(rows)]
    outs = [_row_out(o, tm, M) for o in out_rows]
    res = pl.pallas_call(
        kern, name=name, grid=(M // tm,),
        in_specs=[a_spec, b_spec] + [s for s, _ in ins] + [full_spec(f.shape) for f in fulls],
        out_specs=[s for s, _ in outs] + [full_spec(s) for s in out_accs],
        out_shape=[sh for _, sh in outs] + [jax.ShapeDtypeStruct(tuple(s), F32) for s in out_accs],
        compiler_params=_params(("arbitrary",)),
    )(a, b, *[x for _, x in ins], *fulls)
    return res


def _row_in(r, tm, T):
    arr, w, k = r if isinstance(r, tuple) else (r, r.shape[1], 0)
    return pl.BlockSpec((tm, w), lambda i: (i, k)), arr


def _row_out(o, tm, T):
    c, dt = o
    return pl.BlockSpec((tm, c), lambda i: (i, 0)), jax.ShapeDtypeStruct((T, c), dt)


def _rowmap(name, fn, rows, fulls, out_rows, out_accs=(), tm=1024):
    T = (rows[0][0] if isinstance(rows[0], tuple) else rows[0]).shape[0]
    tm = min(tm, T)
    assert T % tm == 0
    n_in, n_row = len(rows) + len(fulls), len(out_rows)

    def kern(*refs):
        i = pl.program_id(0)
        res = fn(*[r[...] for r in refs[:n_in]])
        res = res if isinstance(res, (tuple, list)) else (res,)
        _store_rows_and_sums(i, refs[n_in:], res, n_row)

    def full_spec(shape):
        nd = len(shape)
        return pl.BlockSpec(tuple(shape), lambda i: (0,) * nd)

    ins = [_row_in(r, tm, T) for r in rows]
    outs = [_row_out(o, tm, T) for o in out_rows]
    res = pl.pallas_call(
        kern, name=name, grid=(T // tm,),
        in_specs=[s for s, _ in ins] + [full_spec(f.shape) for f in fulls],
        out_specs=[s for s, _ in outs] + [full_spec(s) for s in out_accs],
        out_shape=[sh for _, sh in outs] + [jax.ShapeDtypeStruct(tuple(s), F32) for s in out_accs],
        compiler_params=_params(("arbitrary",)),
    )(*[x for _, x in ins], *fulls)
    return res


def _colsum(v):
    return jnp.sum(v, axis=0, keepdims=True)


def _rowmean(v):
    return jnp.mean(v, axis=-1, keepdims=True)


def _rms_fwd(x, g):
    r = lax.rsqrt(_rowmean(x * x) + EPS)
    return x * r * g


def _rms_bwd(x, g, dy):
    r = lax.rsqrt(_rowmean(x * x) + EPS)
    xh = x * r
    dxh = dy * g
    return r * (dxh - xh * _rowmean(dxh * xh)), _colsum(dy * xh)


def _sigmoid(v):
    return 0.5 * jnp.tanh(0.5 * v) + 0.5


def _gelu(y):
    c = np.sqrt(2.0 / np.pi).astype(np.float32)
    return 0.5 * y * (1.0 + jnp.tanh(c * (y + 0.044715 * (y * y * y))))


def _gelu_and_grad(y):
    c = np.sqrt(2.0 / np.pi).astype(np.float32)
    th = jnp.tanh(c * (y + 0.044715 * (y * y * y)))
    return (0.5 * y * (1.0 + th),
            0.5 * (1.0 + th) + 0.5 * y * (1.0 - th * th) * c * (1.0 + 3.0 * 0.044715 * (y * y)))


def _split3(v):
    hi = v.astype(BF16)
    r1 = v - hi.astype(F32)
    mid = r1.astype(BF16)
    lo = (r1 - mid.astype(F32)).astype(BF16)
    return hi, mid, lo


def _band(tm, lo_off, hi_off):
    shape = (tm, tm + 2 * POOL_PAD)
    d = lax.broadcasted_iota(jnp.int32, shape, 1) - lax.broadcasted_iota(jnp.int32, shape, 0) - POOL_PAD
    return ((d >= lo_off) & (d < hi_off)).astype(BF16)


def _window_sums(ext, tm, lo_off, hi_off):
    hi, mid, lo = _split3(ext)
    band = _band(LANES, lo_off, hi_off)
    dot = functools.partial(jnp.dot, preferred_element_type=F32)
    out = []
    for s in range(tm // LANES):
        rows = slice(s * LANES, (s + 2) * LANES)
        out.append(dot(band, hi[rows]) + dot(band, mid[rows]) + dot(band, lo[rows]))
    return jnp.concatenate(out, axis=0)


def _window_count(r0, tm, half, T):
    t = r0 + lax.broadcasted_iota(jnp.int32, (tm, 1), 0)
    return (jnp.minimum(t + half, T) - jnp.maximum(t - half, 0)).astype(F32)


def _halo_specs(tm, C, T):
    per = tm // POOL_PAD
    last = T // POOL_PAD - 1
    return [pl.BlockSpec((POOL_PAD, C), lambda i: (jnp.maximum(i * per - 1, 0), 0)),
            pl.BlockSpec((tm, C), lambda i: (i, 0)),
            pl.BlockSpec((POOL_PAD, C), lambda i: (jnp.minimum((i + 1) * per, last), 0))]


def _with_halo(before_ref, main_ref, after_ref, i, n):
    before = jnp.where(i > 0, before_ref[...], 0.0)
    after = jnp.where(i < n - 1, after_ref[...], 0.0)
    return jnp.concatenate([before, main_ref[...], after], axis=0)


def _pool_block(ext, ctr, w_ref, r0, tm, T):
    pooled, conc = [], []
    for gi, w in enumerate(POOL_WINDOWS):
        half = w // 2
        cols = slice(gi * POOL_GROUP, (gi + 1) * POOL_GROUP)
        ws = _window_sums(ext[:, cols], tm, -half, half)
        p = ws / _window_count(r0, tm, half, T) - ctr[:, cols]
        pooled.append(p)
        conc.append(jnp.dot(p.astype(BF16), w_ref[gi], preferred_element_type=F32))
    return pooled, jnp.concatenate(conc, axis=1)


def _pool_fwd(u, pool_w_bf, pool_scale, g_pool, tm):
    T = u.shape[0]
    nw = len(POOL_WINDOWS)
    C = nw * POOL_GROUP
    n = T // tm

    def kern(ub_ref, u_ref, ua_ref, w_ref, sc_ref, g_ref, o_ref):
        i = pl.program_id(0)
        r0 = pl.multiple_of(i * tm, tm)
        _, conc = _pool_block(_with_halo(ub_ref, u_ref, ua_ref, i, n), u_ref[...], w_ref, r0, tm, T)
        o_ref[...] = _rms_fwd(conc * sc_ref[...], g_ref[...]).astype(o_ref.dtype)

    return pl.pallas_call(
        kern, name="pool_fwd", grid=(n,),
        in_specs=_halo_specs(tm, C, T) + [pl.BlockSpec(pool_w_bf.shape, lambda i: (0, 0, 0)),
                                          pl.BlockSpec((1, C), lambda i: (0, 0)), pl.BlockSpec((1, C), lambda i: (0, 0))],
        out_specs=pl.BlockSpec((tm, C), lambda i: (i, 0)),
        out_shape=jax.ShapeDtypeStruct((T, C), BF16),
        compiler_params=_params(("arbitrary",)),
    )(u, u, u, pool_w_bf, pool_scale, g_pool)


def _pool_bwd_rows(u, d_y, pool_w_bf, pool_scale, g_pool, tm):
    T = u.shape[0]
    nw = len(POOL_WINDOWS)
    C = nw * POOL_GROUP
    n = T // tm

    def kern(ub_ref, u_ref, ua_ref, dy_ref, w_ref, sc_ref, g_ref, q_ref, dp_ref, dw_ref, dsc_ref, dg_ref):
        i = pl.program_id(0)
        r0 = pl.multiple_of(i * tm, tm)
        pooled, conc = _pool_block(_with_halo(ub_ref, u_ref, ua_ref, i, n), u_ref[...], w_ref, r0, tm, T)
        sc = sc_ref[...]
        dyp, dg = _rms_bwd(conc * sc, g_ref[...], dy_ref[...])
        dsc = _colsum(dyp * conc)
        dconc = (dyp * sc).astype(BF16)
        dws = []
        for gi, w in enumerate(POOL_WINDOWS):
            cols = slice(gi * POOL_GROUP, (gi + 1) * POOL_GROUP)
            dc = dconc[:, cols]
            dp = lax.dot_general(dc, w_ref[gi], (((1,), (1,)), ((), ())), preferred_element_type=F32)
            dp_ref[:, cols] = dp
            q_ref[:, cols] = dp / _window_count(r0, tm, w // 2, T)
            dws.append(lax.dot_general(pooled[gi].astype(BF16), dc, (((0,), (0,)), ((), ())), preferred_element_type=F32))

        @pl.when(i == 0)
        def _():
            for gi in range(nw):
                dw_ref[gi] = dws[gi]
            dsc_ref[...] = dsc
            dg_ref[...] = dg

        @pl.when(i != 0)
        def _():
            for gi in range(nw):
                dw_ref[gi] += dws[gi]
            dsc_ref[...] += dsc
            dg_ref[...] += dg

    full2 = pl.BlockSpec((1, C), lambda i: (0, 0))
    row_blk = pl.BlockSpec((tm, C), lambda i: (i, 0))
    return pl.pallas_call(
        kern, name="pool_bwd_rows", grid=(n,),
        in_specs=_halo_specs(tm, C, T) + [row_blk, pl.BlockSpec(pool_w_bf.shape, lambda i: (0, 0, 0)), full2, full2],
        out_specs=[row_blk, row_blk, pl.BlockSpec((nw, POOL_GROUP, POOL_GROUP), lambda i: (0, 0, 0)), full2, full2],
        out_shape=[jax.ShapeDtypeStruct((T, C), F32), jax.ShapeDtypeStruct((T, C), F32),
                   jax.ShapeDtypeStruct((nw, POOL_GROUP, POOL_GROUP), F32),
                   jax.ShapeDtypeStruct((1, C), F32), jax.ShapeDtypeStruct((1, C), F32)],
        compiler_params=_params(("arbitrary",)),
    )(u, u, u, d_y, pool_w_bf, pool_scale, g_pool)


def _pool_bwd_band(q, dpooled, du_other, after, tm):
    T, C = dpooled.shape
    C2 = du_other.shape[1]
    n = T // tm

    def kern(qb_ref, q_ref, qa_ref, dp_ref, other_ref, after_ref, o_ref):
        ext = _with_halo(qb_ref, q_ref, qa_ref, pl.program_id(0), n)
        for gi, w in enumerate(POOL_WINDOWS):
            half = w // 2
            cols = slice(gi * POOL_GROUP, (gi + 1) * POOL_GROUP)
            du = _window_sums(ext[:, cols], tm, -half + 1, half + 1) - dp_ref[:, cols]
            o_ref[:, cols] = du.astype(o_ref.dtype)
        o_ref[:, C:] = other_ref[...].astype(o_ref.dtype)

    return pl.pallas_call(
        kern, name="pool_bwd_band", grid=(n,),
        in_specs=_halo_specs(tm, C, T) + [pl.BlockSpec((tm, C), lambda i: (i, 0)), pl.BlockSpec((tm, C2), lambda i: (i, 0)),
                                          pl.BlockSpec(after.shape, lambda i: (0, 0))],
        out_specs=pl.BlockSpec((tm, C + C2), lambda i: (i, 0)),
        out_shape=jax.ShapeDtypeStruct((T, C + C2), BF16),
        compiler_params=_params(("arbitrary",)),
    )(q, q, q, dpooled, du_other, after)


def _to_chunk_rows(a):
    T, C = a.shape
    return a.reshape(N_CHUNK, T // N_CHUNK, C).transpose(1, 0, 2).reshape(T, C)


def _from_chunk_rows(a):
    T, C = a.shape
    return a.reshape(T // N_CHUNK, N_CHUNK, C).transpose(1, 0, 2).reshape(T, C)


def _cmul(ar, ai, br, bi):
    return ar * br - ai * bi, ar * bi + ai * br


def _ssm_discretise(log_neg_a_re, a_im, log_dt):
    dt = jnp.exp(log_dt)
    a_re = -jnp.exp(log_neg_a_re)
    mag = jnp.exp(a_re * dt)
    ang = a_im * dt
    lam_re, lam_im = mag * jnp.cos(ang), mag * jnp.sin(ang)
    den = a_re * a_re + a_im * a_im
    f_re = ((lam_re - 1.0) * a_re + lam_im * a_im) / den
    f_im = (lam_im * a_re - (lam_re - 1.0) * a_im) / den
    return lam_re, lam_im, f_re, f_im


def _ssm_params_fwd(log_neg_a_re, a_im, log_dt):
    rows, n = log_neg_a_re.shape

    def kern(a_ref, b_ref, c_ref, o1, o2, o3, o4):
        for o, v in zip((o1, o2, o3, o4), _ssm_discretise(a_ref[...], b_ref[...], c_ref[...])):
            o[...] = v

    return pl.pallas_call(kern, name="ssm_params_fwd", out_shape=[jax.ShapeDtypeStruct((rows, n), F32)] * 4)(
        log_neg_a_re, a_im, log_dt)


def _ssm_params_bwd(log_neg_a_re, a_im, log_dt, cots):
    rows, n = log_neg_a_re.shape

    def kern(a_ref, b_ref, c_ref, g1, g2, g3, g4, o1, o2, o3):
        _, vjp = jax.vjp(_ssm_discretise, a_ref[...], b_ref[...], c_ref[...])
        d1, d2, d3 = vjp((g1[...], g2[...], g3[...], g4[...]))
        o1[...] = d1
        o2[...] = d2
        o3[...] = d3

    return pl.pallas_call(
        kern, name="ssm_params_bwd",
        out_shape=[jax.ShapeDtypeStruct((rows, n), F32), jax.ShapeDtypeStruct((rows, n), F32),
                   jax.ShapeDtypeStruct((rows, 1), F32)])(log_neg_a_re, a_im, log_dt, *cots)


def _cpow(lr, li, n):
    out = None
    br, bi = lr, li
    while n:
        if n & 1:
            out = (br, bi) if out is None else _cmul(out[0], out[1], br, bi)
        n >>= 1
        if n:
            br, bi = _cmul(br, bi, br, bi)
    return out


def _slab(t):
    return pl.ds(pl.multiple_of(t * N_CHUNK, N_CHUNK), N_CHUNK)


def _steps(n, step, carry):
    main = n // SCAN_UNROLL

    def body(i, c):
        for r in range(SCAN_UNROLL):
            c = step(i * SCAN_UNROLL + r, c)
        return c

    carry = lax.fori_loop(0, main, body, carry)
    for k in range(main * SCAN_UNROLL, n):
        carry = step(jnp.int32(k), carry)
    return carry


def _shift_chunks(vr, vi, reverse):
    sub = lax.broadcasted_iota(jnp.int32, vr.shape, 0)
    if reverse:
        keep = sub != N_CHUNK - 1
        return jnp.where(keep, pltpu.roll(vr, N_CHUNK - 1, 0), 0.0), jnp.where(keep, pltpu.roll(vi, N_CHUNK - 1, 0), 0.0)
    keep = sub != 0
    return jnp.where(keep, pltpu.roll(vr, 1, 0), 0.0), jnp.where(keep, pltpu.roll(vi, 1, 0), 0.0)


def _chunk_scan(xr, xi, lr, li, Lc, reverse, prev=None):
    C = xr.shape[1]
    lrb, lib = jnp.broadcast_to(lr, (N_CHUNK, C)), jnp.broadcast_to(li, (N_CHUNK, C))
    zero = jnp.zeros((N_CHUNK, C), F32)

    def step_of(k):
        return (Lc - 1 - k) if reverse else k

    def advance(sr, si, t):
        return lrb * sr - lib * si + xr[_slab(t), :], lrb * si + lib * sr + xi[_slab(t), :]

    def pass1(k, c):
        return advance(c[0], c[1], step_of(k))

    er, ei = _steps(Lc, pass1, (zero, zero))
    pr, pi = _cpow(lr, li, Lc)
    prb, pib = jnp.broadcast_to(pr, (N_CHUNK, C)), jnp.broadcast_to(pi, (N_CHUNK, C))
    cr, ci = zero, zero
    for _ in range(N_CHUNK - 1):
        cr, ci = _shift_chunks(er + prb * cr - pib * ci, ei + prb * ci + pib * cr, reverse)

    if prev is None:
        def pass2(k, c):
            t = step_of(k)
            nr, ni = advance(c[0], c[1], t)
            xr[_slab(t), :] = nr
            xi[_slab(t), :] = ni
            return nr, ni

        _steps(Lc, pass2, (cr, ci))
        return None

    qr, qi = prev
    def pair(ar, ai, wr, wi, acc):
        return acc[0] + ar * wr + ai * wi, acc[1] + ai * wr - ar * wi

    def pass2(k, c):
        sr, si, accr, acci = c
        t = step_of(k)
        nr, ni = advance(sr, si, t)
        xr[_slab(t), :] = nr
        xi[_slab(t), :] = ni
        tp = t - 1 if reverse else t + 1
        accr, acci = pair(nr, ni, qr[_slab(tp), :], qi[_slab(tp), :], (accr, acci))
        return nr, ni, accr, acci

    sr, si, accr, acci = _steps(Lc - 1, pass2, (cr, ci, zero, zero))
    t = step_of(Lc - 1)
    nr, ni = advance(sr, si, t)
    xr[_slab(t), :] = nr
    xi[_slab(t), :] = ni
    edge = step_of(0)
    wr, wi = _shift_chunks(qr[_slab(edge), :], qi[_slab(edge), :], not reverse)
    return pair(nr, ni, wr, wi, (accr, acci))


def _ssm_dims(T):
    assert T % ROW_BLOCK == 0
    return T // N_CHUNK, T // ROW_BLOCK


def _group_mask():
    shape = (SSM_BLOCK_GROUPS * SSM_GROUP, SSM_BLOCK_GROUPS * SSM_STATE)
    rows = lax.broadcasted_iota(jnp.int32, shape, 0) >> GROUP_SHIFT
    cols = lax.broadcasted_iota(jnp.int32, shape, 1) >> STATE_SHIFT
    return rows == cols


def _expand_groups(compact):
    return jnp.where(_group_mask(), jnp.tile(compact, (SSM_BLOCK_GROUPS, 1)), 0.0).astype(BF16)


def _collect_groups(block):
    kept = jnp.where(_group_mask(), block, 0.0)
    out = kept[0:SSM_GROUP]
    for g in range(1, SSM_BLOCK_GROUPS):
        out = out + kept[g * SSM_GROUP:(g + 1) * SSM_GROUP]
    return out


def _ssm_specs(T, direction):
    SC = SSM_BLOCK_GROUPS * SSM_STATE
    col = pl.BlockSpec((T, LANES), lambda m: (0, m))
    mat = pl.BlockSpec((None, None, SSM_GROUP, SC), lambda m: (direction, m, 0, 0))
    vec = pl.BlockSpec((None, None, 1, SC), lambda m: (direction, m, 0, 0))
    return col, mat, vec


def _ssm_project_in(u_ref, bre, bim, xr, xi, nblk):
    for blk in range(nblk):
        rows = pl.ds(blk * ROW_BLOCK, ROW_BLOCK)
        ub = u_ref[rows, :].astype(BF16)
        xr[rows, :] = jnp.dot(ub, bre, preferred_element_type=F32)
        xi[rows, :] = jnp.dot(ub, bim, preferred_element_type=F32)


def _ssm_fwd(name, u_bf, mats, vecs, direction):
    T, W = u_bf.shape
    SC = SSM_BLOCK_GROUPS * SSM_STATE
    nb = W // LANES
    Lc, nblk = _ssm_dims(T)
    reverse = bool(direction)

    def kern(u_ref, bre_ref, bim_ref, cre_ref, cim_ref, lr_ref, li_ref, fr_ref, fi_ref, y_ref, xr, xi):
        _ssm_project_in(u_ref, _expand_groups(bre_ref[...]), _expand_groups(bim_ref[...]), xr, xi, nblk)
        _chunk_scan(xr, xi, lr_ref[...], li_ref[...], Lc, reverse)
        fr, fi = fr_ref[...], fi_ref[...]
        cre, cim = _expand_groups(cre_ref[...]), _expand_groups(cim_ref[...])
        for blk in range(nblk):
            rows = pl.ds(blk * ROW_BLOCK, ROW_BLOCK)
            sr, si = _cmul(fr, fi, xr[rows, :], xi[rows, :])
            y_ref[rows, :] = (lax.dot_general(sr.astype(BF16), cre, _NT, preferred_element_type=F32)
                              - lax.dot_general(si.astype(BF16), cim, _NT, preferred_element_type=F32))

    col, mat, vec = _ssm_specs(T, direction)
    return pl.pallas_call(
        kern, name=name, grid=(nb,),
        in_specs=[col, mat, mat, mat, mat, vec, vec, vec, vec],
        out_specs=col, out_shape=jax.ShapeDtypeStruct((T, W), F32),
        scratch_shapes=[pltpu.VMEM((T, SC), F32), pltpu.VMEM((T, SC), F32)],
        compiler_params=_params(("arbitrary",)),
    )(u_bf, *mats, *vecs)


def _ssm_bwd(name, u_bf, dy_bf, du_in, mats, vecs, direction, du_dtype):
    T, W = u_bf.shape
    SC = SSM_BLOCK_GROUPS * SSM_STATE
    nb = W // LANES
    Lc, nblk = _ssm_dims(T)
    reverse = bool(direction)

    def kern(u_ref, dy_ref, dui_ref, bre_ref, bim_ref, cre_ref, cim_ref, lr_ref, li_ref, fr_ref, fi_ref,
             du_ref, dbre_ref, dbim_ref, dcre_ref, dcim_ref, dlr_ref, dli_ref, dfr_ref, dfi_ref,
             xr, xi, gr, gi):
        lr, li, fr, fi = lr_ref[...], li_ref[...], fr_ref[...], fi_ref[...]
        bre, bim = _expand_groups(bre_ref[...]), _expand_groups(bim_ref[...])
        cre, cim = _expand_groups(cre_ref[...]), _expand_groups(cim_ref[...])
        _ssm_project_in(u_ref, bre, bim, xr, xi, nblk)
        _chunk_scan(xr, xi, lr, li, Lc, reverse)
        mre = jnp.zeros((LANES, SC), F32)
        mim = jnp.zeros((LANES, SC), F32)
        dfr = jnp.zeros((1, SC), F32)
        dfi = jnp.zeros((1, SC), F32)
        for blk in range(nblk):
            rows = pl.ds(blk * ROW_BLOCK, ROW_BLOCK)
            dy = dy_ref[rows, :].astype(BF16)
            x_r, x_i = xr[rows, :], xi[rows, :]
            g_r = jnp.dot(dy, cre, preferred_element_type=F32)
            g_i = -jnp.dot(dy, cim, preferred_element_type=F32)
            mre += lax.dot_general(dy, x_r.astype(BF16), _TN, preferred_element_type=F32)
            mim += lax.dot_general(dy, x_i.astype(BF16), _TN, preferred_element_type=F32)
            dfr += _colsum(g_r * x_r + g_i * x_i)
            dfi += _colsum(g_i * x_r - g_r * x_i)
            gr[rows, :] = fr * g_r + fi * g_i
            gi[rows, :] = fr * g_i - fi * g_r
        dcre_ref[...] = _collect_groups(fr * mre - fi * mim)
        dcim_ref[...] = _collect_groups(-(fr * mim + fi * mre))
        dfr_ref[...] = dfr
        dfi_ref[...] = dfi
        accr, acci = _chunk_scan(gr, gi, lr, -li, Lc, not reverse, prev=(xr, xi))
        dlr_ref[...] = _colsum(accr)
        dli_ref[...] = _colsum(acci)
        dbre = jnp.zeros((LANES, SC), F32)
        dbim = jnp.zeros((LANES, SC), F32)
        for blk in range(nblk):
            rows = pl.ds(blk * ROW_BLOCK, ROW_BLOCK)
            ub = u_ref[rows, :].astype(BF16)
            a_r, a_i = gr[rows, :].astype(BF16), gi[rows, :].astype(BF16)
            dbre += lax.dot_general(ub, a_r, _TN, preferred_element_type=F32)
            dbim += lax.dot_general(ub, a_i, _TN, preferred_element_type=F32)
            du_ref[rows, :] = (dui_ref[rows, :] + lax.dot_general(a_r, bre, _NT, preferred_element_type=F32)
                               + lax.dot_general(a_i, bim, _NT, preferred_element_type=F32)).astype(du_ref.dtype)
        dbre_ref[...] = _collect_groups(dbre)
        dbim_ref[...] = _collect_groups(dbim)

    col, mat, vec = _ssm_specs(T, direction)
    mat_out = pl.BlockSpec((None, SSM_GROUP, SC), lambda m: (m, 0, 0))
    vec_out = pl.BlockSpec((None, 1, SC), lambda m: (m, 0, 0))
    mat_shape = jax.ShapeDtypeStruct((nb, SSM_GROUP, SC), F32)
    vec_shape = jax.ShapeDtypeStruct((nb, 1, SC), F32)
    return pl.pallas_call(
        kern, name=name, grid=(nb,),
        in_specs=[col, col, col, mat, mat, mat, mat, vec, vec, vec, vec],
        out_specs=[col, mat_out, mat_out, mat_out, mat_out, vec_out, vec_out, vec_out, vec_out],
        out_shape=[jax.ShapeDtypeStruct((T, W), du_dtype), mat_shape, mat_shape, mat_shape, mat_shape,
                   vec_shape, vec_shape, vec_shape, vec_shape],
        scratch_shapes=[pltpu.VMEM((T, SC), F32)] * 4,
        compiler_params=_params(("arbitrary",)),
    )(u_bf, dy_bf, du_in, *mats, *vecs)


def _groups_side_by_side(p, channel_axis):
    d, G = p.shape[:2]
    nb = G // SSM_BLOCK_GROUPS
    v = p.reshape(d, nb, SSM_BLOCK_GROUPS, p.shape[2], p.shape[3])
    v = v.transpose(0, 1, 4, 2, 3) if channel_axis == 3 else v.transpose(0, 1, 3, 2, 4)
    return v.reshape(d, nb, SSM_GROUP, SSM_BLOCK_GROUPS * SSM_STATE)


def _groups_apart(m, channel_axis):
    d, nb = m.shape[:2]
    v = m.reshape(d, nb, SSM_GROUP, SSM_BLOCK_GROUPS, SSM_STATE)
    v = v.transpose(0, 1, 3, 4, 2) if channel_axis == 3 else v.transpose(0, 1, 3, 2, 4)
    return v.reshape((d, nb * SSM_BLOCK_GROUPS) + v.shape[3:])


def _conv_taps(ref, c, R, T):
    n = T // R
    r0 = pl.multiple_of(c * R, R)
    main = ref[pl.ds(r0, R), :]
    before = ref[pl.ds(pl.multiple_of(jnp.maximum(r0 - 8, 0), 8), 8), :]
    after = ref[pl.ds(pl.multiple_of(jnp.minimum(r0 + R, T - 8), 8), 8), :]
    ext = jnp.concatenate([jnp.where(c > 0, before, 0.0), main, jnp.where(c < n - 1, after, 0.0)], axis=0)
    m1 = pltpu.roll(ext, 1, 0)[8:8 + R]
    p1 = pltpu.roll(ext, R + 15, 0)[8:8 + R]
    return m1, main, p1, r0


def _conv_specs(T, F, cb):
    nj = F // cb
    lo = lambda rows: pl.BlockSpec((rows, cb), lambda j: (0, j))
    hi = lambda rows: pl.BlockSpec((rows, cb), lambda j: (0, j + nj))
    return nj, lo, hi


def _convact_fwd(up, conv_w, conv_b, F):
    T = up.shape[0]
    cb = _pick(F, (256, 128))
    R = _pick(T, (256, 128, 64))
    nj, lo, hi = _conv_specs(T, F, cb)

    def kern(uv, ug, wv, wg, bv, bg, o_ref):
        def body(c, carry):
            v1, v0, v2, r0 = _conv_taps(uv, c, R, T)
            g1, g0, g2, _ = _conv_taps(ug, c, R, T)
            val = v1 * wv[0:1, :] + v0 * wv[1:2, :] + v2 * wv[2:3, :] + bv[...]
            gate = g1 * wg[0:1, :] + g0 * wg[1:2, :] + g2 * wg[2:3, :] + bg[...]
            half = 0.5 * gate
            o_ref[pl.ds(r0, R), :] = (val * (half * jnp.tanh(half) + half)).astype(o_ref.dtype)
            return carry

        lax.fori_loop(0, T // R, body, 0)

    return pl.pallas_call(
        kern, name="convact_fwd", grid=(nj,),
        in_specs=[lo(T), hi(T), lo(3), hi(3), lo(1), hi(1)],
        out_specs=lo(T), out_shape=jax.ShapeDtypeStruct((T, F), BF16),
        compiler_params=_params(("arbitrary",)),
    )(up, up, conv_w, conv_w, conv_b, conv_b)


def _convact_bwd(up, dact, conv_w, conv_b, F):
    T = up.shape[0]
    cb = _pick(F, (256, 128))
    R = _pick(T, (256, 128, 64))
    nj, lo, hi = _conv_specs(T, F, cb)

    def kern(uv, ug, da, wv, wg, bv, bg, dup, dwv, dwg, dbv, dbg, sv, sg):
        zero = jnp.zeros((1, cb), F32)

        def pass_a(c, acc):
            v1, v0, v2, r0 = _conv_taps(uv, c, R, T)
            g1, g0, g2, _ = _conv_taps(ug, c, R, T)
            val = v1 * wv[0:1, :] + v0 * wv[1:2, :] + v2 * wv[2:3, :] + bv[...]
            gate = g1 * wg[0:1, :] + g0 * wg[1:2, :] + g2 * wg[2:3, :] + bg[...]
            sig = _sigmoid(gate)
            d = da[pl.ds(r0, R), :]
            silu = gate * sig
            dval = d * silu
            dgate = d * val * (sig + silu - silu * sig)
            sv[pl.ds(r0, R), :] = dval
            sg[pl.ds(r0, R), :] = dgate
            terms = (dval * v1, dval * v0, dval * v2, dval, dgate * g1, dgate * g0, dgate * g2, dgate)
            return tuple(a + _colsum(t) for a, t in zip(acc, terms))

        acc = lax.fori_loop(0, T // R, pass_a, (zero,) * 8)
        for k in range(3):
            dwv[k:k + 1, :] = acc[k]
            dwg[k:k + 1, :] = acc[4 + k]
        dbv[...] = acc[3]
        dbg[...] = acc[7]

        def pass_b(c, carry):
            v1, v0, v2, r0 = _conv_taps(sv, c, R, T)
            g1, g0, g2, _ = _conv_taps(sg, c, R, T)
            dup[0, pl.ds(r0, R), :] = (v2 * wv[0:1, :] + v0 * wv[1:2, :] + v1 * wv[2:3, :]).astype(dup.dtype)
            dup[1, pl.ds(r0, R), :] = (g2 * wg[0:1, :] + g0 * wg[1:2, :] + g1 * wg[2:3, :]).astype(dup.dtype)
            return carry

        lax.fori_loop(0, T // R, pass_b, 0)

    dup, dwv, dwg, dbv, dbg = pl.pallas_call(
        kern, name="convact_bwd", grid=(nj,),
        in_specs=[lo(T), hi(T), lo(T), lo(3), hi(3), lo(1), hi(1)],
        out_specs=[pl.BlockSpec((2, T, cb), lambda j: (0, 0, j)), lo(3), lo(3), lo(1), lo(1)],
        out_shape=[jax.ShapeDtypeStruct((2, T, F), BF16),
                   jax.ShapeDtypeStruct((3, F), F32), jax.ShapeDtypeStruct((3, F), F32),
                   jax.ShapeDtypeStruct((1, F), F32), jax.ShapeDtypeStruct((1, F), F32)],
        scratch_shapes=[pltpu.VMEM((T, cb), F32), pltpu.VMEM((T, cb), F32)],
        compiler_params=_params(("arbitrary",)),
    )(up, up, dact, conv_w, conv_w, conv_b, conv_b)
    return dup, (dwv, dwg), jnp.concatenate([dbv, dbg], axis=1)


def _peers(x, y, c, with_self=False):
    out = []
    for k in range(0 if with_self else 1, N_DEV):
        px = 1 - x if k & 4 else x
        py = 1 - y if k & 2 else y
        pc = 1 - c if k & 1 else c
        out.append((k, (px, py, pc), 4 * px + 2 * py + pc))
    return out


def _exchange_start(name, bufs, modes, after):
    handles, token = _exchange_start_groups(name, [(bufs, modes)], after)
    return handles[0], token


def _exchange_start_groups(name, groups, after):
    bufs = [b for g, _ in groups for b in g]
    modes = [m for _, ms in groups for m in ms]
    first = [int(v) for v in np.cumsum([0] + [len(g) for g, _ in groups])]
    n, ng = len(bufs), len(groups)
    lands = [lax.empty((N_DEV,) + tuple(b.shape[-2:]), b.dtype) for b in bufs]

    def body(*refs):
        ins, land_in = refs[:n], refs[n:2 * n]
        sems = refs[2 * n + 1:2 * n + 1 + 2 * ng]
        token = refs[-1]
        x, y, c = lax.axis_index("x"), lax.axis_index("y"), lax.axis_index("c")
        me = 4 * x + 2 * y + c
        for g in range(ng):
            for b in range(first[g], first[g + 1]):
                for k, peer, slot in _peers(x, y, c, with_self=True):
                    sem = (b - first[g]) * N_DEV + k
                    pltpu.make_async_remote_copy(
                        src_ref=ins[b] if modes[b] == 'gather' else ins[b].at[slot], dst_ref=land_in[b].at[me],
                        send_sem=sems[2 * g].at[sem], recv_sem=sems[2 * g + 1].at[sem],
                        device_id=peer, device_id_type=MESH_ID).start()
        token[...] = jnp.zeros_like(token)

    hbm = pl.BlockSpec(memory_space=pltpu.HBM)
    sem_spec = pl.BlockSpec(memory_space=pltpu.SEMAPHORE)
    operands = [pltpu.with_memory_space_constraint(a, pltpu.HBM) for a in bufs + lands]
    sem_shapes = [pltpu.SemaphoreType.DMA((len(g) * N_DEV,)) for g, _ in groups for _ in range(2)]
    res = pl.pallas_call(
        body, name=name,
        out_shape=(*sem_shapes, *[pltpu.HBM(a.shape, a.dtype) for a in operands], jax.ShapeDtypeStruct((8, LANES), F32)),
        in_specs=[hbm] * (2 * n) + [pl.BlockSpec(memory_space=pl.ANY)],
        out_specs=(*([sem_spec] * (2 * ng)), *([hbm] * (2 * n)), pl.BlockSpec(memory_space=pltpu.VMEM)),
        input_output_aliases={i: 2 * ng + i for i in range(2 * n)},
        compiler_params=pltpu.CompilerParams(has_side_effects=pltpu.SideEffectType.DATAFLOW_SIDE_EFFECTING),
    )(*operands, after)
    srcs, landed = res[2 * ng:2 * ng + n], res[2 * ng + n:2 * ng + 2 * n]
    handles = [(res[2 * g], res[2 * g + 1], list(srcs[first[g]:first[g + 1]]), list(landed[first[g]:first[g + 1]]),
                tuple(modes[first[g]:first[g + 1]])) for g in range(ng)]
    return handles, res[-1]


def _exchange_wait(name, handle, after):
    send_sems, recv_sems, srcs, lands, modes = handle
    n = len(srcs)

    def body(*refs):
        src_in, land_in = refs[:n], refs[n:2 * n]
        send_ref, recv_ref = refs[2 * n], refs[2 * n + 1]
        x, y, c = lax.axis_index("x"), lax.axis_index("y"), lax.axis_index("c")
        for b in range(n):
            for k, peer, slot in _peers(x, y, c, with_self=True):
                sem = b * N_DEV + k
                copy = pltpu.make_async_remote_copy(
                    src_ref=src_in[b] if modes[b] == 'gather' else src_in[b].at[slot], dst_ref=land_in[b].at[slot],
                    send_sem=send_ref.at[sem], recv_sem=recv_ref.at[sem],
                    device_id=peer, device_id_type=MESH_ID)
                copy.wait_send()
                copy.wait_recv()

    hbm = pl.BlockSpec(memory_space=pltpu.HBM)
    sem_spec = pl.BlockSpec(memory_space=pltpu.SEMAPHORE)
    res = pl.pallas_call(
        body, name=name,
        out_shape=tuple(pltpu.HBM(a.shape, a.dtype) for a in srcs + lands),
        in_specs=[hbm] * (2 * n) + [sem_spec, sem_spec, pl.BlockSpec(memory_space=pl.ANY)],
        out_specs=tuple([hbm] * (2 * n)),
        input_output_aliases={i: i for i in range(2 * n)},
        compiler_params=pltpu.CompilerParams(has_side_effects=pltpu.SideEffectType.DATAFLOW_SIDE_EFFECTING),
    )(*srcs, *lands, send_sems, recv_sems, after)
    return list(res[n:])


def _adamw(w, g, m, v):
    m2 = ADAM_B1 * m + (1.0 - ADAM_B1) * g
    v2 = ADAM_B2 * v + (1.0 - ADAM_B2) * (g * g)
    m_hat = m2 / (1.0 - ADAM_B1 ** ADAM_STEP)
    v_hat = v2 / (1.0 - ADAM_B2 ** ADAM_STEP)
    return -ADAM_LR * (m_hat / (jnp.sqrt(v_hat) + ADAM_EPS) + ADAM_WD * w), m2, v2


def _sum_adamw(name, landed, w, m, v):
    R, C = w.shape
    row_bytes = N_DEV * (-(-C // LANES) * LANES) * landed.dtype.itemsize
    tiles = [d for d in range(16, R, 16) if R % d == 0 and d * row_bytes <= ADAM_TILE_BYTES]
    tr = max(tiles) if tiles and R * row_bytes > ADAM_TILE_BYTES else R

    def kern(x_ref, w_ref, m_ref, v_ref, g_out, d_out, m_out, v_out):
        g = x_ref[0].astype(F32)
        for q in range(1, N_DEV):
            g = g + x_ref[q].astype(F32)
        g_out[...] = g
        d_out[...], m_out[...], v_out[...] = _adamw(w_ref[...], g, m_ref[...], v_ref[...])

    blk = pl.BlockSpec((tr, C), lambda i: (i, 0))
    return pl.pallas_call(
        kern, name=name, grid=(R // tr,),
        in_specs=[pl.BlockSpec((N_DEV, tr, C), lambda i: (0, i, 0)), blk, blk, blk],
        out_specs=[blk] * 4, out_shape=[jax.ShapeDtypeStruct((R, C), F32)] * 4,
        compiler_params=_params(("arbitrary",)),
    )(landed, w, m, v)


def _sum_adamw_many(name, items):
    n = len(items)

    def kern(*refs):
        ins, outs = refs[:4 * n], refs[4 * n:]
        for k in range(n):
            x_ref, w_ref, m_ref, v_ref = ins[4 * k:4 * k + 4]
            g = x_ref[0].astype(F32)
            for q in range(1, N_DEV):
                g = g + x_ref[q].astype(F32)
            outs[4 * k][...] = g
            outs[4 * k + 1][...], outs[4 * k + 2][...], outs[4 * k + 3][...] = _adamw(w_ref[...], g, m_ref[...], v_ref[...])

    out_shape = [jax.ShapeDtypeStruct(w.shape, F32) for _, w, _, _ in items for _ in range(4)]
    res = pl.pallas_call(kern, name=name, out_shape=out_shape)(*[a for item in items for a in item])
    return [res[4 * k:4 * k + 4] for k in range(n)]


def _view2d(a):
    if a.ndim == 1:
        return a.reshape(1, -1)
    return a.reshape(-1, a.shape[-1])


def kernel(x, norm_mix_g, w_in, pool_w, pool_scale, ssm_log_neg_a_re, ssm_a_im, ssm_log_dt, ssm_b_re, ssm_b_im, ssm_c_re, ssm_c_im, ssm_d, glu_w, glu_b, out_norm_pool_g, out_norm_ssm_g, w_out, norm_ffn_g, w_up, conv_w, conv_b, w_down, final_norm_g, loss_target, m_norm_mix_g, m_w_in, m_pool_w, m_pool_scale, m_ssm_log_neg_a_re, m_ssm_a_im, m_ssm_log_dt, m_ssm_b_re, m_ssm_b_im, m_ssm_c_re, m_ssm_c_im, m_ssm_d, m_glu_w, m_glu_b, m_out_norm_pool_g, m_out_norm_ssm_g, m_w_out, m_norm_ffn_g, m_w_up, m_conv_w, m_conv_b, m_w_down, m_final_norm_g, v_norm_mix_g, v_w_in, v_pool_w, v_pool_scale, v_ssm_log_neg_a_re, v_ssm_a_im, v_ssm_log_dt, v_ssm_b_re, v_ssm_b_im, v_ssm_c_re, v_ssm_c_im, v_ssm_d, v_glu_w, v_glu_b, v_out_norm_pool_g, v_out_norm_ssm_g, v_w_out, v_norm_ffn_g, v_w_up, v_conv_w, v_conv_b, v_w_down, v_final_norm_g):
    given = dict(locals())
    weights = {n: given[n] for n in WEIGHTS}
    mom1 = {n: given["m_" + n] for n in WEIGHTS}
    mom2 = {n: given["v_" + n] for n in WEIGHTS}

    xs = x[0]
    tgt = loss_target[0]
    T, D = xs.shape
    DP = len(POOL_WINDOWS) * POOL_GROUP
    DS = D - DP
    G = DS // SSM_GROUP
    N, H = SSM_STATE, SSM_GROUP
    F2 = w_up.shape[1] * N_DEV
    F = F2 // 2
    fs = w_up.shape[1]
    row = lambda a: a.reshape(1, -1)

    in_handle, token = _exchange_start("gather_in_start", [w_in.astype(BF16)], ['gather'], after=w_in)
    later = lambda w: (w + token[0, 0]).astype(BF16)
    gather_groups = [([later(glu_w), later(w_out)], ['gather'] * 2),
                     ([later(w_up.T), conv_w], ['gather'] * 2), ([later(w_down)], ['gather'])]
    (mix_handle, up_handle, down_handle), token = _exchange_start_groups("gather_rest_start", gather_groups, token)
    CB = row(conv_b)

    g1, g2, g3 = row(norm_mix_g), row(norm_ffn_g), row(final_norm_g)
    gp, gs = row(out_norm_pool_g), row(out_norm_ssm_g)
    (xn,) = _rowmap("norm_mix", _rms_fwd, [xs], [g1 + token[0:1, 0:1]], [(D, BF16)])
    W_in = _exchange_wait("gather_in_wait", in_handle, xn)[0].reshape(D, D)
    halves = lambda r: (r[:, :DP], r[:, DP:])
    u, u_s5 = _mm_rows("proj_in", xn, W_in, 'nn', halves, [], [], [(DP, F32), (DS, F32)])
    tmp = _pick(T, (1024, 512, 256, 128))
    pool_w_bf = pool_w.astype(BF16)
    ypn = _pool_fwd(u, pool_w_bf, row(pool_scale), gp, tmp)

    u_ssm = _to_chunk_rows(u_s5)
    u_ssm_bf = u_ssm.astype(BF16)
    a_rows = (ssm_log_neg_a_re.reshape(2 * G, N), ssm_a_im.reshape(2 * G, N), ssm_log_dt.reshape(2 * G, 1))
    disc = _ssm_params_fwd(*a_rows)
    nb = G // SSM_BLOCK_GROUPS
    vecs = [d.reshape(2, nb, 1, SSM_BLOCK_GROUPS * N) for d in disc]
    mats = [_groups_side_by_side(ssm_b_re, 3), _groups_side_by_side(ssm_b_im, 3),
            _groups_side_by_side(ssm_c_re, 2), _groups_side_by_side(ssm_c_im, 2)]
    y_dir = [_ssm_fwd("ssm_fwd_%d" % d, u_ssm_bf, mats, vecs, d) for d in range(2)]

    def mix_post(yf, yb, us, d, gw, gb, g):
        y = yf + yb + d * us
        z = _gelu(y)
        gate = _sigmoid(jnp.dot(z.astype(BF16), gw, preferred_element_type=F32) + gb)
        return _rms_fwd(z * gate, g), y

    mix_landed = _exchange_wait("gather_mix_wait", mix_handle, y_dir[1])
    W_glu = mix_landed[0].reshape(DS, DS)
    W_out = mix_landed[1].reshape(D, D)
    ysn, y_ssm = _rowmap("ssm_post", mix_post, [y_dir[0], y_dir[1], u_ssm], [row(ssm_d), W_glu, row(glu_b), gs],
                         [(DS, BF16), (DS, F32)])
    ycat = jnp.concatenate([ypn, _from_chunk_rows(ysn)], axis=1)
    h1, hn = _mm_rows("proj_out", ycat, W_out, 'nn', lambda h, g: (h, _rms_fwd(h, g)), [], [g2],
                      [(D, F32), (D, BF16)], add=xs)
    up_landed = _exchange_wait("gather_up_wait", up_handle, hn)
    W_up_t = up_landed[0].reshape(F2, D)
    CW = up_landed[1].transpose(1, 0, 2).reshape(3, F2)
    up = _mm("ffn_up", hn, W_up_t, 'nt', F32)
    act = _convact_fwd(up, CW, CB, F)
    W_down = _exchange_wait("gather_down_wait", down_handle, act)[0].reshape(F, D)

    def head(h, t, g):
        r = lax.rsqrt(_rowmean(h * h) + EPS)
        hh = h * r
        e = hh * g - t
        loss = 0.5 * jnp.sum(_rowmean(e * e), keepdims=True)
        dy = e * (1.0 / D)
        dxh = dy * g
        dh = r * (dxh - hh * _rowmean(dxh * hh))
        return dh, dh, jnp.broadcast_to(loss, (1, LANES)), _colsum(dy * hh)

    dh2, dh2_bf, loss_acc, dg3 = _mm_rows("ffn_down", act, W_down, 'nn', head, [tgt], [g3],
                                          [(D, F32), (D, BF16)], [(1, LANES), (1, D)], add=h1)

    dact = _mm("ffn_down_dx", dh2_bf, W_down, 'nt', F32)
    dW_down = _mm("ffn_down_dw", act, dh2_bf, 'tn', BF16)
    dup, dCW, dCB = _convact_bwd(up, dact, CW, CB, F)
    dhn_val = _mm("ffn_up_dx_val", dup, W_up_t, 'nn', F32, b_rows=(2, 0), a_plane=0)
    dW_up_t = _mm("ffn_up_dw", dup, hn, 'tn', BF16).reshape(F2, D)

    def shards(full_grad):
        return full_grad.reshape((N_DEV, full_grad.shape[0] // N_DEV) + full_grad.shape[1:])

    half = N_DEV // 2
    dCW_sh = jnp.concatenate([h.reshape(3, half, fs).transpose(1, 0, 2) for h in dCW], axis=0)
    ffn_sent = [shards(dW_down), shards(dW_up_t), dCW_sh]
    ffn_handle, token = _exchange_start("scatter_ffn_start", ffn_sent, ['scatter'] * 3, after=dhn_val)
    g2_late = g2 + token[0:1, 0:1]

    def norm_bwd_add(dy, hx, res, g):
        dx, dg = _rms_bwd(hx, g, dy)
        return dx + res, dg

    def norm_bwd_add2(dy, hx, res, g):
        dx, dg = _rms_bwd(hx, g, dy)
        return dx + res, dx + res, dg

    dh1, dh1_bf, dg2 = _mm_rows("ffn_up_dx_gate", dup, W_up_t, 'nn', norm_bwd_add2, [h1, dh2], [g2_late],
                                [(D, F32), (D, BF16)], [(1, D)], add=dhn_val, b_rows=(2, 1), a_plane=1)
    dycat, d_ysn = _mm_rows("proj_out_dx", dh1_bf, W_out, 'nt', halves, [], [], [(DP, F32), (DS, F32)])
    dW_out = _mm("proj_out_dw", ycat, dh1_bf, 'tn', BF16)

    out_sent = [shards(dW_out)]
    out_handle, token = _exchange_start("scatter_out_start", out_sent, ['scatter'], after=dycat)
    q, dpooled, dpool_w, dpool_scale, dgp = _pool_bwd_rows(u, dycat, pool_w_bf, row(pool_scale),
                                                           gp + token[0:1, 0:1], tmp)

    def mix_post_bwd(y, us, dyn, d, gw, gb, g):
        z, dz_dy = _gelu_and_grad(y)
        gz = z.astype(BF16)
        sig = _sigmoid(jnp.dot(gz, gw, preferred_element_type=F32) + gb)
        dys, dg = _rms_bwd(z * sig, g, dyn)
        dgl = dys * z * sig * (1.0 - sig)
        dgl_bf = dgl.astype(BF16)
        dz = dys * sig + lax.dot_general(dgl_bf, gw, (((1,), (1,)), ((), ())), preferred_element_type=F32)
        dgw = lax.dot_general(gz, dgl_bf, (((0,), (0,)), ((), ())), preferred_element_type=F32)
        dyv = dz * dz_dy
        return dyv, dyv * d, dgw, _colsum(dgl), _colsum(dyv * us), dg

    dyv, du_dir, dglu_w, dglu_b, dssm_d, dgs = _rowmap(
        "ssm_post_bwd", mix_post_bwd, [y_ssm, u_ssm, _to_chunk_rows(d_ysn)], [row(ssm_d), W_glu, row(glu_b), gs],
        [(DS, BF16), (DS, F32)], [(DS, DS), (1, DS), (1, DS), (1, DS)])

    du_ssm = du_dir
    dB, dC, dvec = [], [], []
    for d in range(2):
        res = _ssm_bwd("ssm_bwd_%d" % d, u_ssm_bf, dyv, du_ssm, mats, vecs, d, BF16 if d == 1 else F32)
        du_ssm = res[0]
        dB.append(res[1:3])
        dC.append(res[3:5])
        dvec.append(res[5:9])
    grad_b = [_groups_apart(jnp.stack([dB[0][k], dB[1][k]]), 3) for k in range(2)]
    grad_c = [_groups_apart(jnp.stack([dC[0][k], dC[1][k]]), 2) for k in range(2)]
    cots = [jnp.stack([dvec[0][k], dvec[1][k]]).reshape(2 * G, N) for k in range(4)]
    d_a_re, d_a_im, d_log_dt = _ssm_params_bwd(*a_rows, cots)

    rep_grads = {
        'pool_w': dpool_w, 'pool_scale': dpool_scale, 'ssm_log_neg_a_re': d_a_re, 'ssm_a_im': d_a_im,
        'ssm_log_dt': d_log_dt, 'ssm_b_re': grad_b[0], 'ssm_b_im': grad_b[1], 'ssm_c_re': grad_c[0], 'ssm_c_im': grad_c[1],
        'ssm_d': dssm_d, 'glu_b': dglu_b, 'out_norm_pool_g': dgp, 'out_norm_ssm_g': dgs, 'norm_ffn_g': dg2,
        'conv_b': dCB, 'final_norm_g': dg3}
    wide = ('pool_w', 'ssm_b_re', 'ssm_b_im', 'ssm_c_re', 'ssm_c_im')

    def narrow(n):
        return weights[n].ndim == 4 and weights[n].shape[-1] < LANES // 2

    def travel(n):
        g = rep_grads[n]
        if narrow(n):
            g = jnp.swapaxes(g, 2, 3)
        if n in wide and g.shape[-1] % LANES == 0:
            return _view2d(g).astype(BF16)
        if n in wide:
            return g.reshape(-1, PACK_W if g.size % PACK_W == 0 else LANES).astype(BF16)
        return _view2d(g.reshape(weights[n].shape))

    small = tuple(n for n in REPLICATED if n != 'norm_mix_g')
    small_sent = [shards(dglu_w.astype(BF16))] + [travel(n) for n in small] + [loss_acc]
    small_handle, token = _exchange_start("reduce_small_start", small_sent, ['scatter'] + ['gather'] * (len(small) + 1),
                                          after=d_a_re)

    du = _pool_bwd_band(q, dpooled, _from_chunk_rows(du_ssm), token, tmp)
    dW_in = _mm("proj_in_dw", xn, du, 'tn', BF16)
    in_handle, token = _exchange_start("scatter_in_start", [shards(dW_in)], ['scatter'], after=dW_in)
    dx, dg1 = _mm_rows("proj_in_dx", du, W_in + token[0, 0].astype(BF16), 'nt', norm_bwd_add, [xs, dh1], [g1],
                       [(D, F32)], [(1, D)])
    rep_grads['norm_mix_g'] = dg1
    last_handle, token = _exchange_start("reduce_last_start", [travel('norm_mix_g')], ['gather'], after=dx)

    grads, delta, new_m, new_v = {}, {}, {}, {}

    def update(n, landed_n):
        shape = weights[n].shape
        if narrow(n):
            swapped = lambda a: _view2d(jnp.swapaxes(a, 2, 3))
            landed_n = landed_n.reshape((N_DEV,) + swapped(weights[n]).shape)
            res = _sum_adamw("adamw_" + n, landed_n, swapped(weights[n]), swapped(mom1[n]), swapped(mom2[n]))
            back = lambda r: jnp.swapaxes(r.reshape(shape[:2] + (shape[3], shape[2])), 2, 3)
            grads[n], delta[n], new_m[n], new_v[n] = [back(r) for r in res]
            return
        if n == 'w_up':
            res = _sum_adamw("adamw_" + n, landed_n, weights[n].T, mom1[n].T, mom2[n].T)
            grads[n], delta[n], new_m[n], new_v[n] = [r.T for r in res]
            return
        landed_n = landed_n.reshape((N_DEV,) + _view2d(weights[n]).shape)
        res = _sum_adamw("adamw_" + n, landed_n, _view2d(weights[n]), _view2d(mom1[n]), _view2d(mom2[n]))
        grads[n], delta[n], new_m[n], new_v[n] = [r.reshape(shape) for r in res]

    def update_many(name, named_landed):
        items = [(l.reshape((N_DEV,) + _view2d(weights[n]).shape), _view2d(weights[n]), _view2d(mom1[n]), _view2d(mom2[n]))
                 for n, l in named_landed]
        for (n, _), res in zip(named_landed, _sum_adamw_many(name, items)):
            grads[n], delta[n], new_m[n], new_v[n] = [r.reshape(weights[n].shape) for r in res]

    ffn_landed = _exchange_wait("scatter_ffn_wait", ffn_handle, token)
    for n, l in zip(('w_down', 'w_up'), ffn_landed):
        update(n, l)
    (out_landed,) = _exchange_wait("scatter_out_wait", out_handle, token)
    update('w_out', out_landed)
    small_landed = _exchange_wait("reduce_small_wait", small_handle, token)
    tiny = [('conv_w', ffn_landed[2])]
    for n, l in zip(('glu_w',) + small, small_landed):
        if weights[n].size <= ADAM_TINY:
            tiny.append((n, l))
        else:
            update(n, l)
    update_many("adamw_tiny", tiny)

    def loss_sum(parts_ref, o_ref):
        s = parts_ref[0]
        for q in range(1, N_DEV):
            s = s + parts_ref[q]
        o_ref[...] = s

    loss = pl.pallas_call(loss_sum, name="loss_sum", out_shape=jax.ShapeDtypeStruct((1, LANES), F32))(small_landed[-1])[0, 0]

    update('w_in', _exchange_wait("scatter_in_wait", in_handle, grads[small[-1]])[0])
    update('norm_mix_g', _exchange_wait("reduce_last_wait", last_handle, grads['w_in'])[0])

    return (loss, dx[None], *[grads[n] for n in WEIGHTS], *[delta[n] for n in WEIGHTS],
            *[new_m[n] for n in WEIGHTS], *[new_v[n] for n in WEIGHTS])
```

```python
import functools

import jax
import jax.numpy as jnp
import numpy as np
from jax import lax
from jax.experimental import pallas as pl
from jax.experimental.pallas import tpu as pltpu

F32 = jnp.float32
BF16 = jnp.bfloat16
MESH_ID = pl.DeviceIdType.MESH

N_DEV = 8
EPS = 1e-6
POOL_WINDOWS = (2, 4, 8, 16)
POOL_GROUP = 128
POOL_PAD = 64
SSM_GROUP = 16
SSM_STATE = 64
SSM_BLOCK_GROUPS = 8
N_CHUNK = 8
ROW_BLOCK = 512
SCAN_UNROLL = 8
GROUP_SHIFT = 4
STATE_SHIFT = 6
_NT = (((1,), (1,)), ((), ()))
_TN = (((0,), (0,)), ((), ()))
LANES = 128
PACK_W = 1024
VMEM_LIMIT = 56 * 1024 * 1024
MM_VMEM_BUDGET = 40 * 1024 * 1024
MM_TILE_CAP = 1408
MM_ROWS_TILES = (512, 1024)
ADAM_TILE_BYTES = 4 * 1024 * 1024
ADAM_TINY = 8192

ADAM_LR = 0.001
ADAM_B1 = 0.9
ADAM_B2 = 0.999
ADAM_EPS = 1e-08
ADAM_WD = 0.01
ADAM_STEP = 10

WEIGHTS = ['norm_mix_g', 'w_in', 'pool_w', 'pool_scale', 'ssm_log_neg_a_re', 'ssm_a_im', 'ssm_log_dt',
           'ssm_b_re', 'ssm_b_im', 'ssm_c_re', 'ssm_c_im', 'ssm_d', 'glu_w', 'glu_b', 'out_norm_pool_g',
           'out_norm_ssm_g', 'w_out', 'norm_ffn_g', 'w_up', 'conv_w', 'conv_b', 'w_down', 'final_norm_g']
SHARDED = ('w_in', 'w_out', 'w_down', 'glu_w', 'w_up', 'conv_w')
REPLICATED = tuple(n for n in WEIGHTS if n not in SHARDED)


def _pick(n, prefs):
    for p in prefs:
        if n % p == 0:
            return p
    return n


def _params(sem, vmem=None):
    return pltpu.CompilerParams(dimension_semantics=sem, vmem_limit_bytes=vmem or VMEM_LIMIT)


def _tiles(n, cap):
    return [d for d in range(LANES, min(n, cap) + 1, LANES) if n % d == 0] or [n]


def _mm(name, a, b, mode, out_dtype, add=None, b_rows=(1, 0), a_plane=None):
    batch = a.shape[0] if a.ndim == 3 and a_plane is None else None
    if mode == 'nn':
        (M, K), N = a.shape[-2:], b.shape[1]
        assert b.shape[0] == K * b_rows[0]
    elif mode == 'nt':
        (M, K), (N, _) = a.shape[-2:], b.shape
    else:
        (K, M), (_, N) = a.shape[-2:], b.shape
    dims = {'nn': (((1,), (0,)), ((), ())), 'nt': (((1,), (1,)), ((), ())), 'tn': (((0,), (0,)), ((), ()))}[mode]
    sa, sb, so = a.dtype.itemsize, b.dtype.itemsize, jnp.dtype(out_dtype).itemsize
    best = None
    for tm in _tiles(M, MM_TILE_CAP):
        for tn in _tiles(N, MM_TILE_CAP):
            need = 2 * (tm * K * sa + tn * K * sb + tm * tn * (so + (4 if add is not None else 0)))
            if need <= MM_VMEM_BUDGET:
                key = (tm * tn / (tm + tn), tm * tn)
                if best is None or key > best[0]:
                    best = (key, tm, tn)
    _, tm, tn = best
    rows_inner = a.size * sa * (N // tn) + b.size * sb < a.size * sa + b.size * sb * (M // tm)

    def kern(*refs):
        a_ref, b_ref = refs[:2]
        o_ref = refs[-1]
        r = lax.dot_general(a_ref[...].astype(BF16), b_ref[...].astype(BF16), dims, preferred_element_type=F32)
        if add is not None:
            r = r + refs[2][...]
        o_ref[...] = r.astype(o_ref.dtype)

    if rows_inner:
        grid = (N // tn, M // tm)
        ij = lambda g0, g1: (g1, g0)
    else:
        grid = (M // tm, N // tn)
        ij = lambda g0, g1: (g0, g1)

    def spec(block, index, batched=False):
        if batch is None:
            return pl.BlockSpec(block, lambda g0, g1: index(*ij(g0, g1)))
        if batched:
            return pl.BlockSpec((None,) + block, lambda p, g0, g1: (p,) + index(*ij(g0, g1)))
        return pl.BlockSpec(block, lambda p, g0, g1: index(*ij(g0, g1)))

    if a_plane is not None:
        assert mode != 'tn'
        a_spec = pl.BlockSpec((None, tm, K), lambda g0, g1: (a_plane, ij(g0, g1)[0], 0))
    else:
        a_spec = (spec((K, tm), lambda i, j: (0, i), True) if mode == 'tn' else spec((tm, K), lambda i, j: (i, 0), True))
    b_spec = (spec((tn, K), lambda i, j: (j, 0)) if mode == 'nt' else spec((K, tn), lambda i, j: (b_rows[1], j)))
    in_specs = [a_spec, b_spec]
    args = [a, b]
    if add is not None:
        in_specs.append(spec((tm, tn), lambda i, j: (i, j), True))
        args.append(add)
    lead = () if batch is None else (batch,)
    return pl.pallas_call(
        kern, name=name, grid=lead + grid, in_specs=in_specs,
        out_specs=spec((tm, tn), lambda i, j: (i, j), True),
        out_shape=jax.ShapeDtypeStruct(lead + (M, N), out_dtype),
        compiler_params=_params(("parallel",) * (len(lead) + 2)),
    )(*args)


def _store_rows_and_sums(i, outs, res, n_row):
    for o, v in zip(outs[:n_row], res[:n_row]):
        o[...] = v.astype(o.dtype)
    for o, v in zip(outs[n_row:], res[n_row:]):
        @pl.when(i == 0)
        def _(o=o, v=v):
            o[...] = v

        @pl.when(i != 0)
        def _(o=o, v=v):
            o[...] += v


def _mm_rows(name, a, b, mode, fn, rows, fulls, out_rows, out_accs=(), add=None, b_rows=(1, 0), a_plane=None):
    M, K = a.shape[-2:]
    N = b.shape[1] if mode == 'nn' else b.shape[0]
    assert b.shape[0] == (K * b_rows[0] if mode == 'nn' else N)
    dims = _NT if mode == 'nt' else (((1,), (0,)), ((), ()))
    n_extra, n_row = len(rows) + len(fulls), len(out_rows)
    row_bytes = (K * a.dtype.itemsize + (4 * N if add is not None else 0) + sum(4 * r.shape[1] for r in rows)
                 + sum(c * jnp.dtype(dt).itemsize for c, dt in out_rows))
    fits = [t for t in MM_ROWS_TILES if M % t == 0 and 2 * (t * row_bytes + K * N * b.dtype.itemsize) <= MM_VMEM_BUDGET]
    tm = max(fits) if fits else min(MM_ROWS_TILES)

    def kern(*refs):
        i = pl.program_id(0)
        r = lax.dot_general(refs[0][...].astype(BF16), refs[1][...].astype(BF16), dims, preferred_element_type=F32)
        pos = 2
        if add is not None:
            r = r + refs[pos][...]
            pos += 1
        res = fn(r, *[x[...] for x in refs[pos:pos + n_extra]])
        _store_rows_and_sums(i, refs[pos + n_extra:], res, n_row)

    def full_spec(shape):
        nd = len(shape)
        return pl.BlockSpec(tuple(shape), lambda i: (0,) * nd)

    a_spec = (pl.BlockSpec((tm, K), lambda i: (i, 0)) if a_plane is None
              else pl.BlockSpec((None, tm, K), lambda i: (a_plane, i, 0)))
    b_spec = pl.BlockSpec((K, N), lambda i: (b_rows[1], 0)) if mode == 'nn' else pl.BlockSpec((N, K), lambda i: (0, 0))
    ins = [_row_in(r, tm, M) for r in ([add] if add is not None else []) + list(rows)]
    outs = [_row_out(o, tm, M) for o in out_rows]
    res = pl.pallas_call(
        kern, name=name, grid=(M // tm,),
        in_specs=[a_spec, b_spec] + [s for s, _ in ins] + [full_spec(f.shape) for f in fulls],
        out_specs=[s for s, _ in outs] + [full_spec(s) for s in out_accs],
        out_shape=[sh for _, sh in outs] + [jax.ShapeDtypeStruct(tuple(s), F32) for s in out_accs],
        compiler_params=_params(("arbitrary",)),
    )(a, b, *[x for _, x in ins], *fulls)
    return res


def _row_in(r, tm, T):
    arr, w, k = r if isinstance(r, tuple) else (r, r.shape[1], 0)
    return pl.BlockSpec((tm, w), lambda i: (i, k)), arr


def _row_out(o, tm, T):
    c, dt = o
    return pl.BlockSpec((tm, c), lambda i: (i, 0)), jax.ShapeDtypeStruct((T, c), dt)


def _rowmap(name, fn, rows, fulls, out_rows, out_accs=(), tm=1024):
    T = (rows[0][0] if isinstance(rows[0], tuple) else rows[0]).shape[0]
    tm = min(tm, T)
    assert T % tm == 0
    n_in, n_row = len(rows) + len(fulls), len(out_rows)

    def kern(*refs):
        i = pl.program_id(0)
        res = fn(*[r[...] for r in refs[:n_in]])
        res = res if isinstance(res, (tuple, list)) else (res,)
        _store_rows_and_sums(i, refs[n_in:], res, n_row)

    def full_spec(shape):
        nd = len(shape)
        return pl.BlockSpec(tuple(shape), lambda i: (0,) * nd)

    ins = [_row_in(r, tm, T) for r in rows]
    outs = [_row_out(o, tm, T) for o in out_rows]
    res = pl.pallas_call(
        kern, name=name, grid=(T // tm,),
        in_specs=[s for s, _ in ins] + [full_spec(f.shape) for f in fulls],
        out_specs=[s for s, _ in outs] + [full_spec(s) for s in out_accs],
        out_shape=[sh for _, sh in outs] + [jax.ShapeDtypeStruct(tuple(s), F32) for s in out_accs],
        compiler_params=_params(("arbitrary",)),
    )(*[x for _, x in ins], *fulls)
    return res


def _colsum(v):
    return jnp.sum(v, axis=0, keepdims=True)


def _rowmean(v):
    return jnp.mean(v, axis=-1, keepdims=True)


def _rms_fwd(x, g):
    r = lax.rsqrt(_rowmean(x * x) + EPS)
    return x * r * g


def _rms_bwd(x, g, dy):
    r = lax.rsqrt(_rowmean(x * x) + EPS)
    xh = x * r
    dxh = dy * g
    return r * (dxh - xh * _rowmean(dxh * xh)), _colsum(dy * xh)


def _sigmoid(v):
    return 0.5 * jnp.tanh(0.5 * v) + 0.5


def _gelu(y):
    c = np.sqrt(2.0 / np.pi).astype(np.float32)
    return 0.5 * y * (1.0 + jnp.tanh(c * (y + 0.044715 * (y * y * y))))


def _gelu_and_grad(y):
    c = np.sqrt(2.0 / np.pi).astype(np.float32)
    th = jnp.tanh(c * (y + 0.044715 * (y * y * y)))
    return (0.5 * y * (1.0 + th),
            0.5 * (1.0 + th) + 0.5 * y * (1.0 - th * th) * c * (1.0 + 3.0 * 0.044715 * (y * y)))


def _split3(v):
    hi = v.astype(BF16)
    r1 = v - hi.astype(F32)
    mid = r1.astype(BF16)
    lo = (r1 - mid.astype(F32)).astype(BF16)
    return hi, mid, lo


def _band(tm, lo_off, hi_off):
    shape = (tm, tm + 2 * POOL_PAD)
    d = lax.broadcasted_iota(jnp.int32, shape, 1) - lax.broadcasted_iota(jnp.int32, shape, 0) - POOL_PAD
    return ((d >= lo_off) & (d < hi_off)).astype(BF16)


def _window_sums(ext, tm, lo_off, hi_off):
    hi, mid, lo = _split3(ext)
    band = _band(LANES, lo_off, hi_off)
    dot = functools.partial(jnp.dot, preferred_element_type=F32)
    out = []
    for s in range(tm // LANES):
        rows = slice(s * LANES, (s + 2) * LANES)
        out.append(dot(band, hi[rows]) + dot(band, mid[rows]) + dot(band, lo[rows]))
    return jnp.concatenate(out, axis=0)


def _window_count(r0, tm, half, T):
    t = r0 + lax.broadcasted_iota(jnp.int32, (tm, 1), 0)
    return (jnp.minimum(t + half, T) - jnp.maximum(t - half, 0)).astype(F32)


def _halo_specs(tm, C, T):
    per = tm // POOL_PAD
    last = T // POOL_PAD - 1
    return [pl.BlockSpec((POOL_PAD, C), lambda i: (jnp.maximum(i * per - 1, 0), 0)),
            pl.BlockSpec((tm, C), lambda i: (i, 0)),
            pl.BlockSpec((POOL_PAD, C), lambda i: (jnp.minimum((i + 1) * per, last), 0))]


def _with_halo(before_ref, main_ref, after_ref, i, n):
    before = jnp.where(i > 0, before_ref[...], 0.0)
    after = jnp.where(i < n - 1, after_ref[...], 0.0)
    return jnp.concatenate([before, main_ref[...], after], axis=0)


def _pool_block(ext, ctr, w_ref, r0, tm, T):
    pooled, conc = [], []
    for gi, w in enumerate(POOL_WINDOWS):
        half = w // 2
        cols = slice(gi * POOL_GROUP, (gi + 1) * POOL_GROUP)
        ws = _window_sums(ext[:, cols], tm, -half, half)
        p = ws / _window_count(r0, tm, half, T) - ctr[:, cols]
        pooled.append(p)
        conc.append(jnp.dot(p.astype(BF16), w_ref[gi], preferred_element_type=F32))
    return pooled, jnp.concatenate(conc, axis=1)


def _pool_fwd(u, pool_w_bf, pool_scale, g_pool, tm):
    T = u.shape[0]
    nw = len(POOL_WINDOWS)
    C = nw * POOL_GROUP
    n = T // tm

    def kern(ub_ref, u_ref, ua_ref, w_ref, sc_ref, g_ref, o_ref):
        i = pl.program_id(0)
        r0 = pl.multiple_of(i * tm, tm)
        _, conc = _pool_block(_with_halo(ub_ref, u_ref, ua_ref, i, n), u_ref[...], w_ref, r0, tm, T)
        o_ref[...] = _rms_fwd(conc * sc_ref[...], g_ref[...]).astype(o_ref.dtype)

    return pl.pallas_call(
        kern, name="pool_fwd", grid=(n,),
        in_specs=_halo_specs(tm, C, T) + [pl.BlockSpec(pool_w_bf.shape, lambda i: (0, 0, 0)),
                                          pl.BlockSpec((1, C), lambda i: (0, 0)), pl.BlockSpec((1, C), lambda i: (0, 0))],
        out_specs=pl.BlockSpec((tm, C), lambda i: (i, 0)),
        out_shape=jax.ShapeDtypeStruct((T, C), BF16),
        compiler_params=_params(("arbitrary",)),
    )(u, u, u, pool_w_bf, pool_scale, g_pool)


def _pool_bwd_rows(u, d_y, pool_w_bf, pool_scale, g_pool, tm):
    T = u.shape[0]
    nw = len(POOL_WINDOWS)
    C = nw * POOL_GROUP
    n = T // tm

    def kern(ub_ref, u_ref, ua_ref, dy_ref, w_ref, sc_ref, g_ref, q_ref, dp_ref, dw_ref, dsc_ref, dg_ref):
        i = pl.program_id(0)
        r0 = pl.multiple_of(i * tm, tm)
        pooled, conc = _pool_block(_with_halo(ub_ref, u_ref, ua_ref, i, n), u_ref[...], w_ref, r0, tm, T)
        sc = sc_ref[...]
        dyp, dg = _rms_bwd(conc * sc, g_ref[...], dy_ref[...])
        dsc = _colsum(dyp * conc)
        dconc = (dyp * sc).astype(BF16)
        dws = []
        for gi, w in enumerate(POOL_WINDOWS):
            cols = slice(gi * POOL_GROUP, (gi + 1) * POOL_GROUP)
            dc = dconc[:, cols]
            dp = lax.dot_general(dc, w_ref[gi], (((1,), (1,)), ((), ())), preferred_element_type=F32)
            dp_ref[:, cols] = dp
            q_ref[:, cols] = dp / _window_count(r0, tm, w // 2, T)
            dws.append(lax.dot_general(pooled[gi].astype(BF16), dc, (((0,), (0,)), ((), ())), preferred_element_type=F32))

        @pl.when(i == 0)
        def _():
            for gi in range(nw):
                dw_ref[gi] = dws[gi]
            dsc_ref[...] = dsc
            dg_ref[...] = dg

        @pl.when(i != 0)
        def _():
            for gi in range(nw):
                dw_ref[gi] += dws[gi]
            dsc_ref[...] += dsc
            dg_ref[...] += dg

    full2 = pl.BlockSpec((1, C), lambda i: (0, 0))
    row_blk = pl.BlockSpec((tm, C), lambda i: (i, 0))
    return pl.pallas_call(
        kern, name="pool_bwd_rows", grid=(n,),
        in_specs=_halo_specs(tm, C, T) + [row_blk, pl.BlockSpec(pool_w_bf.shape, lambda i: (0, 0, 0)), full2, full2],
        out_specs=[row_blk, row_blk, pl.BlockSpec((nw, POOL_GROUP, POOL_GROUP), lambda i: (0, 0, 0)), full2, full2],
        out_shape=[jax.ShapeDtypeStruct((T, C), F32), jax.ShapeDtypeStruct((T, C), F32),
                   jax.ShapeDtypeStruct((nw, POOL_GROUP, POOL_GROUP), F32),
                   jax.ShapeDtypeStruct((1, C), F32), jax.ShapeDtypeStruct((1, C), F32)],
        compiler_params=_params(("arbitrary",)),
    )(u, u, u, d_y, pool_w_bf, pool_scale, g_pool)


def _pool_bwd_band(q, dpooled, du_other, after, tm):
    T, C = dpooled.shape
    C2 = du_other.shape[1]
    n = T // tm

    def kern(qb_ref, q_ref, qa_ref, dp_ref, other_ref, after_ref, o_ref):
        ext = _with_halo(qb_ref, q_ref, qa_ref, pl.program_id(0), n)
        for gi, w in enumerate(POOL_WINDOWS):
            half = w // 2
            cols = slice(gi * POOL_GROUP, (gi + 1) * POOL_GROUP)
            du = _window_sums(ext[:, cols], tm, -half + 1, half + 1) - dp_ref[:, cols]
            o_ref[:, cols] = du.astype(o_ref.dtype)
        o_ref[:, C:] = other_ref[...].astype(o_ref.dtype)

    return pl.pallas_call(
        kern, name="pool_bwd_band", grid=(n,),
        in_specs=_halo_specs(tm, C, T) + [pl.BlockSpec((tm, C), lambda i: (i, 0)), pl.BlockSpec((tm, C2), lambda i: (i, 0)),
                                          pl.BlockSpec(after.shape, lambda i: (0, 0))],
        out_specs=pl.BlockSpec((tm, C + C2), lambda i: (i, 0)),
        out_shape=jax.ShapeDtypeStruct((T, C + C2), BF16),
        compiler_params=_params(("arbitrary",)),
    )(q, q, q, dpooled, du_other, after)


def _to_chunk_rows(a):
    T, C = a.shape
    return a.reshape(N_CHUNK, T // N_CHUNK, C).transpose(1, 0, 2).reshape(T, C)


def _from_chunk_rows(a):
    T, C = a.shape
    return a.reshape(T // N_CHUNK, N_CHUNK, C).transpose(1, 0, 2).reshape(T, C)


def _cmul(ar, ai, br, bi):
    return ar * br - ai * bi, ar * bi + ai * br


def _ssm_discretise(log_neg_a_re, a_im, log_dt):
    dt = jnp.exp(log_dt)
    a_re = -jnp.exp(log_neg_a_re)
    mag = jnp.exp(a_re * dt)
    ang = a_im * dt
    lam_re, lam_im = mag * jnp.cos(ang), mag * jnp.sin(ang)
    den = a_re * a_re + a_im * a_im
    f_re = ((lam_re - 1.0) * a_re + lam_im * a_im) / den
    f_im = (lam_im * a_re - (lam_re - 1.0) * a_im) / den
    return lam_re, lam_im, f_re, f_im


def _ssm_params_fwd(log_neg_a_re, a_im, log_dt):
    rows, n = log_neg_a_re.shape

    def kern(a_ref, b_ref, c_ref, o1, o2, o3, o4):
        for o, v in zip((o1, o2, o3, o4), _ssm_discretise(a_ref[...], b_ref[...], c_ref[...])):
            o[...] = v

    return pl.pallas_call(kern, name="ssm_params_fwd", out_shape=[jax.ShapeDtypeStruct((rows, n), F32)] * 4)(
        log_neg_a_re, a_im, log_dt)


def _ssm_params_bwd(log_neg_a_re, a_im, log_dt, cots):
    rows, n = log_neg_a_re.shape

    def kern(a_ref, b_ref, c_ref, g1, g2, g3, g4, o1, o2, o3):
        _, vjp = jax.vjp(_ssm_discretise, a_ref[...], b_ref[...], c_ref[...])
        d1, d2, d3 = vjp((g1[...], g2[...], g3[...], g4[...]))
        o1[...] = d1
        o2[...] = d2
        o3[...] = d3

    return pl.pallas_call(
        kern, name="ssm_params_bwd",
        out_shape=[jax.ShapeDtypeStruct((rows, n), F32), jax.ShapeDtypeStruct((rows, n), F32),
                   jax.ShapeDtypeStruct((rows, 1), F32)])(log_neg_a_re, a_im, log_dt, *cots)


def _cpow(lr, li, n):
    out = None
    br, bi = lr, li
    while n:
        if n & 1:
            out = (br, bi) if out is None else _cmul(out[0], out[1], br, bi)
        n >>= 1
        if n:
            br, bi = _cmul(br, bi, br, bi)
    return out


def _slab(t):
    return pl.ds(pl.multiple_of(t * N_CHUNK, N_CHUNK), N_CHUNK)


def _steps(n, step, carry):
    main = n // SCAN_UNROLL

    def body(i, c):
        for r in range(SCAN_UNROLL):
            c = step(i * SCAN_UNROLL + r, c)
        return c

    carry = lax.fori_loop(0, main, body, carry)
    for k in range(main * SCAN_UNROLL, n):
        carry = step(jnp.int32(k), carry)
    return carry


def _shift_chunks(vr, vi, reverse):
    sub = lax.broadcasted_iota(jnp.int32, vr.shape, 0)
    if reverse:
        keep = sub != N_CHUNK - 1
        return jnp.where(keep, pltpu.roll(vr, N_CHUNK - 1, 0), 0.0), jnp.where(keep, pltpu.roll(vi, N_CHUNK - 1, 0), 0.0)
    keep = sub != 0
    return jnp.where(keep, pltpu.roll(vr, 1, 0), 0.0), jnp.where(keep, pltpu.roll(vi, 1, 0), 0.0)


def _chunk_scan(xr, xi, lr, li, Lc, reverse, prev=None):
    C = xr.shape[1]
    lrb, lib = jnp.broadcast_to(lr, (N_CHUNK, C)), jnp.broadcast_to(li, (N_CHUNK, C))
    zero = jnp.zeros((N_CHUNK, C), F32)

    def step_of(k):
        return (Lc - 1 - k) if reverse else k

    def advance(sr, si, t):
        return lrb * sr - lib * si + xr[_slab(t), :], lrb * si + lib * sr + xi[_slab(t), :]

    def pass1(k, c):
        return advance(c[0], c[1], step_of(k))

    er, ei = _steps(Lc, pass1, (zero, zero))
    pr, pi = _cpow(lr, li, Lc)
    prb, pib = jnp.broadcast_to(pr, (N_CHUNK, C)), jnp.broadcast_to(pi, (N_CHUNK, C))
    cr, ci = zero, zero
    for _ in range(N_CHUNK - 1):
        cr, ci = _shift_chunks(er + prb * cr - pib * ci, ei + prb * ci + pib * cr, reverse)

    if prev is None:
        def pass2(k, c):
            t = step_of(k)
            nr, ni = advance(c[0], c[1], t)
            xr[_slab(t), :] = nr
            xi[_slab(t), :] = ni
            return nr, ni

        _steps(Lc, pass2, (cr, ci))
        return None

    qr, qi = prev
    def pair(ar, ai, wr, wi, acc):
        return acc[0] + ar * wr + ai * wi, acc[1] + ai * wr - ar * wi

    def pass2(k, c):
        sr, si, accr, acci = c
        t = step_of(k)
        nr, ni = advance(sr, si, t)
        xr[_slab(t), :] = nr
        xi[_slab(t), :] = ni
        tp = t - 1 if reverse else t + 1
        accr, acci = pair(nr, ni, qr[_slab(tp), :], qi[_slab(tp), :], (accr, acci))
        return nr, ni, accr, acci

    sr, si, accr, acci = _steps(Lc - 1, pass2, (cr, ci, zero, zero))
    t = step_of(Lc - 1)
    nr, ni = advance(sr, si, t)
    xr[_slab(t), :] = nr
    xi[_slab(t), :] = ni
    edge = step_of(0)
    wr, wi = _shift_chunks(qr[_slab(edge), :], qi[_slab(edge), :], not reverse)
    return pair(nr, ni, wr, wi, (accr, acci))


def _ssm_dims(T):
    assert T % ROW_BLOCK == 0
    return T // N_CHUNK, T // ROW_BLOCK


def _group_mask():
    shape = (SSM_BLOCK_GROUPS * SSM_GROUP, SSM_BLOCK_GROUPS * SSM_STATE)
    rows = lax.broadcasted_iota(jnp.int32, shape, 0) >> GROUP_SHIFT
    cols = lax.broadcasted_iota(jnp.int32, shape, 1) >> STATE_SHIFT
    return rows == cols


def _expand_groups(compact):
    return jnp.where(_group_mask(), jnp.tile(compact, (SSM_BLOCK_GROUPS, 1)), 0.0).astype(BF16)


def _collect_groups(block):
    kept = jnp.where(_group_mask(), block, 0.0)
    out = kept[0:SSM_GROUP]
    for g in range(1, SSM_BLOCK_GROUPS):
        out = out + kept[g * SSM_GROUP:(g + 1) * SSM_GROUP]
    return out


def _ssm_specs(T, direction):
    SC = SSM_BLOCK_GROUPS * SSM_STATE
    col = pl.BlockSpec((T, LANES), lambda m: (0, m))
    mat = pl.BlockSpec((None, None, SSM_GROUP, SC), lambda m: (direction, m, 0, 0))
    vec = pl.BlockSpec((None, None, 1, SC), lambda m: (direction, m, 0, 0))
    return col, mat, vec


def _ssm_project_in(u_ref, bre, bim, xr, xi, nblk):
    for blk in range(nblk):
        rows = pl.ds(blk * ROW_BLOCK, ROW_BLOCK)
        ub = u_ref[rows, :].astype(BF16)
        xr[rows, :] = jnp.dot(ub, bre, preferred_element_type=F32)
        xi[rows, :] = jnp.dot(ub, bim, preferred_element_type=F32)


def _ssm_fwd(name, u_bf, mats, vecs, direction):
    T, W = u_bf.shape
    SC = SSM_BLOCK_GROUPS * SSM_STATE
    nb = W // LANES
    Lc, nblk = _ssm_dims(T)
    reverse = bool(direction)

    def kern(u_ref, bre_ref, bim_ref, cre_ref, cim_ref, lr_ref, li_ref, fr_ref, fi_ref, y_ref, xr, xi):
        _ssm_project_in(u_ref, _expand_groups(bre_ref[...]), _expand_groups(bim_ref[...]), xr, xi, nblk)
        _chunk_scan(xr, xi, lr_ref[...], li_ref[...], Lc, reverse)
        fr, fi = fr_ref[...], fi_ref[...]
        cre, cim = _expand_groups(cre_ref[...]), _expand_groups(cim_ref[...])
        for blk in range(nblk):
            rows = pl.ds(blk * ROW_BLOCK, ROW_BLOCK)
            sr, si = _cmul(fr, fi, xr[rows, :], xi[rows, :])
            y_ref[rows, :] = (lax.dot_general(sr.astype(BF16), cre, _NT, preferred_element_type=F32)
                              - lax.dot_general(si.astype(BF16), cim, _NT, preferred_element_type=F32))

    col, mat, vec = _ssm_specs(T, direction)
    return pl.pallas_call(
        kern, name=name, grid=(nb,),
        in_specs=[col, mat, mat, mat, mat, vec, vec, vec, vec],
        out_specs=col, out_shape=jax.ShapeDtypeStruct((T, W), F32),
        scratch_shapes=[pltpu.VMEM((T, SC), F32), pltpu.VMEM((T, SC), F32)],
        compiler_params=_params(("arbitrary",)),
    )(u_bf, *mats, *vecs)


def _ssm_bwd(name, u_bf, dy_bf, du_in, mats, vecs, direction, du_dtype):
    T, W = u_bf.shape
    SC = SSM_BLOCK_GROUPS * SSM_STATE
    nb = W // LANES
    Lc, nblk = _ssm_dims(T)
    reverse = bool(direction)

    def kern(u_ref, dy_ref, dui_ref, bre_ref, bim_ref, cre_ref, cim_ref, lr_ref, li_ref, fr_ref, fi_ref,
             du_ref, dbre_ref, dbim_ref, dcre_ref, dcim_ref, dlr_ref, dli_ref, dfr_ref, dfi_ref,
             xr, xi, gr, gi):
        lr, li, fr, fi = lr_ref[...], li_ref[...], fr_ref[...], fi_ref[...]
        bre, bim = _expand_groups(bre_ref[...]), _expand_groups(bim_ref[...])
        cre, cim = _expand_groups(cre_ref[...]), _expand_groups(cim_ref[...])
        _ssm_project_in(u_ref, bre, bim, xr, xi, nblk)
        _chunk_scan(xr, xi, lr, li, Lc, reverse)
        mre = jnp.zeros((LANES, SC), F32)
        mim = jnp.zeros((LANES, SC), F32)
        dfr = jnp.zeros((1, SC), F32)
        dfi = jnp.zeros((1, SC), F32)
        for blk in range(nblk):
            rows = pl.ds(blk * ROW_BLOCK, ROW_BLOCK)
            dy = dy_ref[rows, :].astype(BF16)
            x_r, x_i = xr[rows, :], xi[rows, :]
            g_r = jnp.dot(dy, cre, preferred_element_type=F32)
            g_i = -jnp.dot(dy, cim, preferred_element_type=F32)
            mre += lax.dot_general(dy, x_r.astype(BF16), _TN, preferred_element_type=F32)
            mim += lax.dot_general(dy, x_i.astype(BF16), _TN, preferred_element_type=F32)
            dfr += _colsum(g_r * x_r + g_i * x_i)
            dfi += _colsum(g_i * x_r - g_r * x_i)
            gr[rows, :] = fr * g_r + fi * g_i
            gi[rows, :] = fr * g_i - fi * g_r
        dcre_ref[...] = _collect_groups(fr * mre - fi * mim)
        dcim_ref[...] = _collect_groups(-(fr * mim + fi * mre))
        dfr_ref[...] = dfr
        dfi_ref[...] = dfi
        accr, acci = _chunk_scan(gr, gi, lr, -li, Lc, not reverse, prev=(xr, xi))
        dlr_ref[...] = _colsum(accr)
        dli_ref[...] = _colsum(acci)
        dbre = jnp.zeros((LANES, SC), F32)
        dbim = jnp.zeros((LANES, SC), F32)
        for blk in range(nblk):
            rows = pl.ds(blk * ROW_BLOCK, ROW_BLOCK)
            ub = u_ref[rows, :].astype(BF16)
            a_r, a_i = gr[rows, :].astype(BF16), gi[rows, :].astype(BF16)
            dbre += lax.dot_general(ub, a_r, _TN, preferred_element_type=F32)
            dbim += lax.dot_general(ub, a_i, _TN, preferred_element_type=F32)
            du_ref[rows, :] = (dui_ref[rows, :] + lax.dot_general(a_r, bre, _NT, preferred_element_type=F32)
                               + lax.dot_general(a_i, bim, _NT, preferred_element_type=F32)).astype(du_ref.dtype)
        dbre_ref[...] = _collect_groups(dbre)
        dbim_ref[...] = _collect_groups(dbim)

    col, mat, vec = _ssm_specs(T, direction)
    mat_out = pl.BlockSpec((None, SSM_GROUP, SC), lambda m: (m, 0, 0))
    vec_out = pl.BlockSpec((None, 1, SC), lambda m: (m, 0, 0))
    mat_shape = jax.ShapeDtypeStruct((nb, SSM_GROUP, SC), F32)
    vec_shape = jax.ShapeDtypeStruct((nb, 1, SC), F32)
    return pl.pallas_call(
        kern, name=name, grid=(nb,),
        in_specs=[col, col, col, mat, mat, mat, mat, vec, vec, vec, vec],
        out_specs=[col, mat_out, mat_out, mat_out, mat_out, vec_out, vec_out, vec_out, vec_out],
        out_shape=[jax.ShapeDtypeStruct((T, W), du_dtype), mat_shape, mat_shape, mat_shape, mat_shape,
                   vec_shape, vec_shape, vec_shape, vec_shape],
        scratch_shapes=[pltpu.VMEM((T, SC), F32)] * 4,
        compiler_params=_params(("arbitrary",)),
    )(u_bf, dy_bf, du_in, *mats, *vecs)


def _groups_side_by_side(p, channel_axis):
    d, G = p.shape[:2]
    nb = G // SSM_BLOCK_GROUPS
    v = p.reshape(d, nb, SSM_BLOCK_GROUPS, p.shape[2], p.shape[3])
    v = v.transpose(0, 1, 4, 2, 3) if channel_axis == 3 else v.transpose(0, 1, 3, 2, 4)
    return v.reshape(d, nb, SSM_GROUP, SSM_BLOCK_GROUPS * SSM_STATE)


def _groups_apart(m, channel_axis):
    d, nb = m.shape[:2]
    v = m.reshape(d, nb, SSM_GROUP, SSM_BLOCK_GROUPS, SSM_STATE)
    v = v.transpose(0, 1, 3, 4, 2) if channel_axis == 3 else v.transpose(0, 1, 3, 2, 4)
    return v.reshape((d, nb * SSM_BLOCK_GROUPS) + v.shape[3:])


def _conv_taps(ref, c, R, T):
    n = T // R
    r0 = pl.multiple_of(c * R, R)
    main = ref[pl.ds(r0, R), :]
    before = ref[pl.ds(pl.multiple_of(jnp.maximum(r0 - 8, 0), 8), 8), :]
    after = ref[pl.ds(pl.multiple_of(jnp.minimum(r0 + R, T - 8), 8), 8), :]
    ext = jnp.concatenate([jnp.where(c > 0, before, 0.0), main, jnp.where(c < n - 1, after, 0.0)], axis=0)
    m1 = pltpu.roll(ext, 1, 0)[8:8 + R]
    p1 = pltpu.roll(ext, R + 15, 0)[8:8 + R]
    return m1, main, p1, r0


def _conv_specs(T, F, cb):
    nj = F // cb
    lo = lambda rows: pl.BlockSpec((rows, cb), lambda j: (0, j))
    hi = lambda rows: pl.BlockSpec((rows, cb), lambda j: (0, j + nj))
    return nj, lo, hi


def _convact_fwd(up, conv_w, conv_b, F):
    T = up.shape[0]
    cb = _pick(F, (256, 128))
    R = _pick(T, (512, 256, 128, 64))
    nj, lo, hi = _conv_specs(T, F, cb)

    def kern(uv, ug, wv, wg, bv, bg, o_ref):
        def body(c, carry):
            v1, v0, v2, r0 = _conv_taps(uv, c, R, T)
            g1, g0, g2, _ = _conv_taps(ug, c, R, T)
            val = v1 * wv[0:1, :] + v0 * wv[1:2, :] + v2 * wv[2:3, :] + bv[...]
            gate = g1 * wg[0:1, :] + g0 * wg[1:2, :] + g2 * wg[2:3, :] + bg[...]
            half = 0.5 * gate
            o_ref[pl.ds(r0, R), :] = (val * (half * jnp.tanh(half) + half)).astype(o_ref.dtype)
            return carry

        lax.fori_loop(0, T // R, body, 0)

    return pl.pallas_call(
        kern, name="convact_fwd", grid=(nj,),
        in_specs=[lo(T), hi(T), lo(3), hi(3), lo(1), hi(1)],
        out_specs=lo(T), out_shape=jax.ShapeDtypeStruct((T, F), BF16),
        compiler_params=_params(("arbitrary",)),
    )(up, up, conv_w, conv_w, conv_b, conv_b)


def _convact_bwd(up, dact, conv_w, conv_b, F):
    T = up.shape[0]
    cb = _pick(F, (256, 128))
    R = _pick(T, (512, 256, 128, 64))
    nj, lo, hi = _conv_specs(T, F, cb)

    def kern(uv, ug, da, wv, wg, bv, bg, dup, dwv, dwg, dbv, dbg, sv, sg):
        zero = jnp.zeros((1, cb), F32)

        def pass_a(c, acc):
            v1, v0, v2, r0 = _conv_taps(uv, c, R, T)
            g1, g0, g2, _ = _conv_taps(ug, c, R, T)
            val = v1 * wv[0:1, :] + v0 * wv[1:2, :] + v2 * wv[2:3, :] + bv[...]
            gate = g1 * wg[0:1, :] + g0 * wg[1:2, :] + g2 * wg[2:3, :] + bg[...]
            sig = _sigmoid(gate)
            d = da[pl.ds(r0, R), :]
            silu = gate * sig
            dval = d * silu
            dgate = d * val * (sig + silu - silu * sig)
            sv[pl.ds(r0, R), :] = dval
            sg[pl.ds(r0, R), :] = dgate
            terms = (dval * v1, dval * v0, dval * v2, dval, dgate * g1, dgate * g0, dgate * g2, dgate)
            return tuple(a + _colsum(t) for a, t in zip(acc, terms))

        acc = lax.fori_loop(0, T // R, pass_a, (zero,) * 8)
        for k in range(3):
            dwv[k:k + 1, :] = acc[k]
            dwg[k:k + 1, :] = acc[4 + k]
        dbv[...] = acc[3]
        dbg[...] = acc[7]

        def pass_b(c, carry):
            v1, v0, v2, r0 = _conv_taps(sv, c, R, T)
            g1, g0, g2, _ = _conv_taps(sg, c, R, T)
            dup[0, pl.ds(r0, R), :] = (v2 * wv[0:1, :] + v0 * wv[1:2, :] + v1 * wv[2:3, :]).astype(dup.dtype)
            dup[1, pl.ds(r0, R), :] = (g2 * wg[0:1, :] + g0 * wg[1:2, :] + g1 * wg[2:3, :]).astype(dup.dtype)
            return carry

        lax.fori_loop(0, T // R, pass_b, 0)

    dup, dwv, dwg, dbv, dbg = pl.pallas_call(
        kern, name="convact_bwd", grid=(nj,),
        in_specs=[lo(T), hi(T), lo(T), lo(3), hi(3), lo(1), hi(1)],
        out_specs=[pl.BlockSpec((2, T, cb), lambda j: (0, 0, j)), lo(3), lo(3), lo(1), lo(1)],
        out_shape=[jax.ShapeDtypeStruct((2, T, F), BF16),
                   jax.ShapeDtypeStruct((3, F), F32), jax.ShapeDtypeStruct((3, F), F32),
                   jax.ShapeDtypeStruct((1, F), F32), jax.ShapeDtypeStruct((1, F), F32)],
        scratch_shapes=[pltpu.VMEM((T, cb), F32), pltpu.VMEM((T, cb), F32)],
        compiler_params=_params(("arbitrary",)),
    )(up, up, dact, conv_w, conv_w, conv_b, conv_b)
    return dup, (dwv, dwg), jnp.concatenate([dbv, dbg], axis=1)


def _peers(x, y, c, with_self=False):
    out = []
    for k in range(0 if with_self else 1, N_DEV):
        px = 1 - x if k & 4 else x
        py = 1 - y if k & 2 else y
        pc = 1 - c if k & 1 else c
        out.append((k, (px, py, pc), 4 * px + 2 * py + pc))
    return out


def _exchange_start(name, bufs, modes, after):
    handles, token = _exchange_start_groups(name, [(bufs, modes)], after)
    return handles[0], token


def _exchange_start_groups(name, groups, after):
    bufs = [b for g, _ in groups for b in g]
    modes = [m for _, ms in groups for m in ms]
    first = [int(v) for v in np.cumsum([0] + [len(g) for g, _ in groups])]
    n, ng = len(bufs), len(groups)
    lands = [lax.empty((N_DEV,) + tuple(b.shape[-2:]), b.dtype) for b in bufs]

    def body(*refs):
        ins, land_in = refs[:n], refs[n:2 * n]
        sems = refs[2 * n + 1:2 * n + 1 + 2 * ng]
        token = refs[-1]
        x, y, c = lax.axis_index("x"), lax.axis_index("y"), lax.axis_index("c")
        me = 4 * x + 2 * y + c
        for g in range(ng):
            for b in range(first[g], first[g + 1]):
                for k, peer, slot in _peers(x, y, c, with_self=True):
                    sem = (b - first[g]) * N_DEV + k
                    pltpu.make_async_remote_copy(
                        src_ref=ins[b] if modes[b] == 'gather' else ins[b].at[slot], dst_ref=land_in[b].at[me],
                        send_sem=sems[2 * g].at[sem], recv_sem=sems[2 * g + 1].at[sem],
                        device_id=peer, device_id_type=MESH_ID).start()
        token[...] = jnp.zeros_like(token)

    hbm = pl.BlockSpec(memory_space=pltpu.HBM)
    sem_spec = pl.BlockSpec(memory_space=pltpu.SEMAPHORE)
    operands = [pltpu.with_memory_space_constraint(a, pltpu.HBM) for a in bufs + lands]
    sem_shapes = [pltpu.SemaphoreType.DMA((len(g) * N_DEV,)) for g, _ in groups for _ in range(2)]
    res = pl.pallas_call(
        body, name=name,
        out_shape=(*sem_shapes, *[pltpu.HBM(a.shape, a.dtype) for a in operands], jax.ShapeDtypeStruct((8, LANES), F32)),
        in_specs=[hbm] * (2 * n) + [pl.BlockSpec(memory_space=pl.ANY)],
        out_specs=(*([sem_spec] * (2 * ng)), *([hbm] * (2 * n)), pl.BlockSpec(memory_space=pltpu.VMEM)),
        input_output_aliases={i: 2 * ng + i for i in range(2 * n)},
        compiler_params=pltpu.CompilerParams(has_side_effects=pltpu.SideEffectType.DATAFLOW_SIDE_EFFECTING),
    )(*operands, after)
    srcs, landed = res[2 * ng:2 * ng + n], res[2 * ng + n:2 * ng + 2 * n]
    handles = [(res[2 * g], res[2 * g + 1], list(srcs[first[g]:first[g + 1]]), list(landed[first[g]:first[g + 1]]),
                tuple(modes[first[g]:first[g + 1]])) for g in range(ng)]
    return handles, res[-1]


def _exchange_wait(name, handle, after):
    send_sems, recv_sems, srcs, lands, modes = handle
    n = len(srcs)

    def body(*refs):
        src_in, land_in = refs[:n], refs[n:2 * n]
        send_ref, recv_ref = refs[2 * n], refs[2 * n + 1]
        x, y, c = lax.axis_index("x"), lax.axis_index("y"), lax.axis_index("c")
        for b in range(n):
            for k, peer, slot in _peers(x, y, c, with_self=True):
                sem = b * N_DEV + k
                copy = pltpu.make_async_remote_copy(
                    src_ref=src_in[b] if modes[b] == 'gather' else src_in[b].at[slot], dst_ref=land_in[b].at[slot],
                    send_sem=send_ref.at[sem], recv_sem=recv_ref.at[sem],
                    device_id=peer, device_id_type=MESH_ID)
                copy.wait_send()
                copy.wait_recv()

    hbm = pl.BlockSpec(memory_space=pltpu.HBM)
    sem_spec = pl.BlockSpec(memory_space=pltpu.SEMAPHORE)
    res = pl.pallas_call(
        body, name=name,
        out_shape=tuple(pltpu.HBM(a.shape, a.dtype) for a in srcs + lands),
        in_specs=[hbm] * (2 * n) + [sem_spec, sem_spec, pl.BlockSpec(memory_space=pl.ANY)],
        out_specs=tuple([hbm] * (2 * n)),
        input_output_aliases={i: i for i in range(2 * n)},
        compiler_params=pltpu.CompilerParams(has_side_effects=pltpu.SideEffectType.DATAFLOW_SIDE_EFFECTING),
    )(*srcs, *lands, send_sems, recv_sems, after)
    return list(res[n:])


def _adamw(w, g, m, v):
    m2 = ADAM_B1 * m + (1.0 - ADAM_B1) * g
    v2 = ADAM_B2 * v + (1.0 - ADAM_B2) * (g * g)
    m_hat = m2 / (1.0 - ADAM_B1 ** ADAM_STEP)
    v_hat = v2 / (1.0 - ADAM_B2 ** ADAM_STEP)
    return -ADAM_LR * (m_hat / (jnp.sqrt(v_hat) + ADAM_EPS) + ADAM_WD * w), m2, v2


def _sum_adamw(name, landed, w, m, v):
    R, C = w.shape
    row_bytes = N_DEV * (-(-C // LANES) * LANES) * landed.dtype.itemsize
    tiles = [d for d in range(16, R, 16) if R % d == 0 and d * row_bytes <= ADAM_TILE_BYTES]
    tr = max(tiles) if tiles and R * row_bytes > ADAM_TILE_BYTES else R

    def kern(x_ref, w_ref, m_ref, v_ref, g_out, d_out, m_out, v_out):
        g = x_ref[0].astype(F32)
        for q in range(1, N_DEV):
            g = g + x_ref[q].astype(F32)
        g_out[...] = g
        d_out[...], m_out[...], v_out[...] = _adamw(w_ref[...], g, m_ref[...], v_ref[...])

    blk = pl.BlockSpec((tr, C), lambda i: (i, 0))
    return pl.pallas_call(
        kern, name=name, grid=(R // tr,),
        in_specs=[pl.BlockSpec((N_DEV, tr, C), lambda i: (0, i, 0)), blk, blk, blk],
        out_specs=[blk] * 4, out_shape=[jax.ShapeDtypeStruct((R, C), F32)] * 4,
        compiler_params=_params(("arbitrary",)),
    )(landed, w, m, v)


def _sum_adamw_many(name, items):
    n = len(items)

    def kern(*refs):
        ins, outs = refs[:4 * n], refs[4 * n:]
        for k in range(n):
            x_ref, w_ref, m_ref, v_ref = ins[4 * k:4 * k + 4]
            g = x_ref[0].astype(F32)
            for q in range(1, N_DEV):
                g = g + x_ref[q].astype(F32)
            outs[4 * k][...] = g
            outs[4 * k + 1][...], outs[4 * k + 2][...], outs[4 * k + 3][...] = _adamw(w_ref[...], g, m_ref[...], v_ref[...])

    out_shape = [jax.ShapeDtypeStruct(w.shape, F32) for _, w, _, _ in items for _ in range(4)]
    res = pl.pallas_call(kern, name=name, out_shape=out_shape)(*[a for item in items for a in item])
    return [res[4 * k:4 * k + 4] for k in range(n)]


def _view2d(a):
    if a.ndim == 1:
        return a.reshape(1, -1)
    return a.reshape(-1, a.shape[-1])


def kernel(x, norm_mix_g, w_in, pool_w, pool_scale, ssm_log_neg_a_re, ssm_a_im, ssm_log_dt, ssm_b_re, ssm_b_im, ssm_c_re, ssm_c_im, ssm_d, glu_w, glu_b, out_norm_pool_g, out_norm_ssm_g, w_out, norm_ffn_g, w_up, conv_w, conv_b, w_down, final_norm_g, loss_target, m_norm_mix_g, m_w_in, m_pool_w, m_pool_scale, m_ssm_log_neg_a_re, m_ssm_a_im, m_ssm_log_dt, m_ssm_b_re, m_ssm_b_im, m_ssm_c_re, m_ssm_c_im, m_ssm_d, m_glu_w, m_glu_b, m_out_norm_pool_g, m_out_norm_ssm_g, m_w_out, m_norm_ffn_g, m_w_up, m_conv_w, m_conv_b, m_w_down, m_final_norm_g, v_norm_mix_g, v_w_in, v_pool_w, v_pool_scale, v_ssm_log_neg_a_re, v_ssm_a_im, v_ssm_log_dt, v_ssm_b_re, v_ssm_b_im, v_ssm_c_re, v_ssm_c_im, v_ssm_d, v_glu_w, v_glu_b, v_out_norm_pool_g, v_out_norm_ssm_g, v_w_out, v_norm_ffn_g, v_w_up, v_conv_w, v_conv_b, v_w_down, v_final_norm_g):
    given = dict(locals())
    weights = {n: given[n] for n in WEIGHTS}
    mom1 = {n: given["m_" + n] for n in WEIGHTS}
    mom2 = {n: given["v_" + n] for n in WEIGHTS}

    xs = x[0]
    tgt = loss_target[0]
    T, D = xs.shape
    DP = len(POOL_WINDOWS) * POOL_GROUP
    DS = D - DP
    G = DS // SSM_GROUP
    N, H = SSM_STATE, SSM_GROUP
    F2 = w_up.shape[1] * N_DEV
    F = F2 // 2
    fs = w_up.shape[1]
    row = lambda a: a.reshape(1, -1)

    in_handle, token = _exchange_start("gather_in_start", [w_in.astype(BF16)], ['gather'], after=w_in)
    later = lambda w: (w + token[0, 0]).astype(BF16)
    gather_groups = [([later(glu_w), later(w_out)], ['gather'] * 2),
                     ([later(w_up.T), conv_w], ['gather'] * 2), ([later(w_down)], ['gather'])]
    (mix_handle, up_handle, down_handle), token = _exchange_start_groups("gather_rest_start", gather_groups, token)
    CB = row(conv_b)

    g1, g2, g3 = row(norm_mix_g), row(norm_ffn_g), row(final_norm_g)
    gp, gs = row(out_norm_pool_g), row(out_norm_ssm_g)
    (xn,) = _rowmap("norm_mix", _rms_fwd, [xs], [g1 + token[0:1, 0:1]], [(D, BF16)])
    W_in = _exchange_wait("gather_in_wait", in_handle, xn)[0].reshape(D, D)
    halves = lambda r: (r[:, :DP], r[:, DP:])
    u, u_s5 = _mm_rows("proj_in", xn, W_in, 'nn', halves, [], [], [(DP, F32), (DS, F32)])
    tmp = _pick(T, (1024, 512, 256, 128))
    pool_w_bf = pool_w.astype(BF16)
    ypn = _pool_fwd(u, pool_w_bf, row(pool_scale), gp, tmp)

    u_ssm = _to_chunk_rows(u_s5)
    u_ssm_bf = u_ssm.astype(BF16)
    a_rows = (ssm_log_neg_a_re.reshape(2 * G, N), ssm_a_im.reshape(2 * G, N), ssm_log_dt.reshape(2 * G, 1))
    disc = _ssm_params_fwd(*a_rows)
    nb = G // SSM_BLOCK_GROUPS
    vecs = [d.reshape(2, nb, 1, SSM_BLOCK_GROUPS * N) for d in disc]
    mats = [_groups_side_by_side(ssm_b_re, 3), _groups_side_by_side(ssm_b_im, 3),
            _groups_side_by_side(ssm_c_re, 2), _groups_side_by_side(ssm_c_im, 2)]
    y_dir = [_ssm_fwd("ssm_fwd_%d" % d, u_ssm_bf, mats, vecs, d) for d in range(2)]

    def mix_post(yf, yb, us, d, gw, gb, g):
        y = yf + yb + d * us
        z = _gelu(y)
        gate = _sigmoid(jnp.dot(z.astype(BF16), gw, preferred_element_type=F32) + gb)
        return _rms_fwd(z * gate, g), y

    mix_landed = _exchange_wait("gather_mix_wait", mix_handle, y_dir[1])
    W_glu = mix_landed[0].reshape(DS, DS)
    W_out = mix_landed[1].reshape(D, D)
    ysn, y_ssm = _rowmap("ssm_post", mix_post, [y_dir[0], y_dir[1], u_ssm], [row(ssm_d), W_glu, row(glu_b), gs],
                         [(DS, BF16), (DS, F32)])
    ycat = jnp.concatenate([ypn, _from_chunk_rows(ysn)], axis=1)
    h1, hn = _mm_rows("proj_out", ycat, W_out, 'nn', lambda h, g: (h, _rms_fwd(h, g)), [], [g2],
                      [(D, F32), (D, BF16)], add=xs)
    up_landed = _exchange_wait("gather_up_wait", up_handle, hn)
    W_up_t = up_landed[0].reshape(F2, D)
    CW = up_landed[1].transpose(1, 0, 2).reshape(3, F2)
    up = _mm("ffn_up", hn, W_up_t, 'nt', F32)
    act = _convact_fwd(up, CW, CB, F)
    W_down = _exchange_wait("gather_down_wait", down_handle, act)[0].reshape(F, D)

    def head(h, t, g):
        r = lax.rsqrt(_rowmean(h * h) + EPS)
        hh = h * r
        e = hh * g - t
        loss = 0.5 * jnp.sum(_rowmean(e * e), keepdims=True)
        dy = e * (1.0 / D)
        dxh = dy * g
        dh = r * (dxh - hh * _rowmean(dxh * hh))
        return dh, dh, jnp.broadcast_to(loss, (1, LANES)), _colsum(dy * hh)

    dh2, dh2_bf, loss_acc, dg3 = _mm_rows("ffn_down", act, W_down, 'nn', head, [tgt], [g3],
                                          [(D, F32), (D, BF16)], [(1, LANES), (1, D)], add=h1)

    dact = _mm("ffn_down_dx", dh2_bf, W_down, 'nt', F32)
    dW_down = _mm("ffn_down_dw", act, dh2_bf, 'tn', BF16)
    dup, dCW, dCB = _convact_bwd(up, dact, CW, CB, F)
    dhn_val = _mm("ffn_up_dx_val", dup, W_up_t, 'nn', F32, b_rows=(2, 0), a_plane=0)
    dW_up_t = _mm("ffn_up_dw", dup, hn, 'tn', BF16).reshape(F2, D)

    def shards(full_grad):
        return full_grad.reshape((N_DEV, full_grad.shape[0] // N_DEV) + full_grad.shape[1:])

    half = N_DEV // 2
    dCW_sh = jnp.concatenate([h.reshape(3, half, fs).transpose(1, 0, 2) for h in dCW], axis=0)
    ffn_sent = [shards(dW_down), shards(dW_up_t), dCW_sh]
    ffn_handle, token = _exchange_start("scatter_ffn_start", ffn_sent, ['scatter'] * 3, after=dhn_val)
    g2_late = g2 + token[0:1, 0:1]

    def norm_bwd_add(dy, hx, res, g):
        dx, dg = _rms_bwd(hx, g, dy)
        return dx + res, dg

    def norm_bwd_add2(dy, hx, res, g):
        dx, dg = _rms_bwd(hx, g, dy)
        return dx + res, dx + res, dg

    dh1, dh1_bf, dg2 = _mm_rows("ffn_up_dx_gate", dup, W_up_t, 'nn', norm_bwd_add2, [h1, dh2], [g2_late],
                                [(D, F32), (D, BF16)], [(1, D)], add=dhn_val, b_rows=(2, 1), a_plane=1)
    dycat, d_ysn = _mm_rows("proj_out_dx", dh1_bf, W_out, 'nt', halves, [], [], [(DP, F32), (DS, F32)])
    dW_out = _mm("proj_out_dw", ycat, dh1_bf, 'tn', BF16)

    out_sent = [shards(dW_out)]
    out_handle, token = _exchange_start("scatter_out_start", out_sent, ['scatter'], after=dycat)
    q, dpooled, dpool_w, dpool_scale, dgp = _pool_bwd_rows(u, dycat, pool_w_bf, row(pool_scale),
                                                           gp + token[0:1, 0:1], tmp)

    def mix_post_bwd(y, us, dyn, d, gw, gb, g):
        z, dz_dy = _gelu_and_grad(y)
        gz = z.astype(BF16)
        sig = _sigmoid(jnp.dot(gz, gw, preferred_element_type=F32) + gb)
        dys, dg = _rms_bwd(z * sig, g, dyn)
        dgl = dys * z * sig * (1.0 - sig)
        dgl_bf = dgl.astype(BF16)
        dz = dys * sig + lax.dot_general(dgl_bf, gw, (((1,), (1,)), ((), ())), preferred_element_type=F32)
        dgw = lax.dot_general(gz, dgl_bf, (((0,), (0,)), ((), ())), preferred_element_type=F32)
        dyv = dz * dz_dy
        return dyv, dyv * d, dgw, _colsum(dgl), _colsum(dyv * us), dg

    dyv, du_dir, dglu_w, dglu_b, dssm_d, dgs = _rowmap(
        "ssm_post_bwd", mix_post_bwd, [y_ssm, u_ssm, _to_chunk_rows(d_ysn)], [row(ssm_d), W_glu, row(glu_b), gs],
        [(DS, BF16), (DS, F32)], [(DS, DS), (1, DS), (1, DS), (1, DS)])

    du_ssm = du_dir
    dB, dC, dvec = [], [], []
    for d in range(2):
        res = _ssm_bwd("ssm_bwd_%d" % d, u_ssm_bf, dyv, du_ssm, mats, vecs, d, BF16 if d == 1 else F32)
        du_ssm = res[0]
        dB.append(res[1:3])
        dC.append(res[3:5])
        dvec.append(res[5:9])
    grad_b = [_groups_apart(jnp.stack([dB[0][k], dB[1][k]]), 3) for k in range(2)]
    grad_c = [_groups_apart(jnp.stack([dC[0][k], dC[1][k]]), 2) for k in range(2)]
    cots = [jnp.stack([dvec[0][k], dvec[1][k]]).reshape(2 * G, N) for k in range(4)]
    d_a_re, d_a_im, d_log_dt = _ssm_params_bwd(*a_rows, cots)

    rep_grads = {
        'pool_w': dpool_w, 'pool_scale': dpool_scale, 'ssm_log_neg_a_re': d_a_re, 'ssm_a_im': d_a_im,
        'ssm_log_dt': d_log_dt, 'ssm_b_re': grad_b[0], 'ssm_b_im': grad_b[1], 'ssm_c_re': grad_c[0], 'ssm_c_im': grad_c[1],
        'ssm_d': dssm_d, 'glu_b': dglu_b, 'out_norm_pool_g': dgp, 'out_norm_ssm_g': dgs, 'norm_ffn_g': dg2,
        'conv_b': dCB, 'final_norm_g': dg3}
    wide = ('pool_w', 'ssm_b_re', 'ssm_b_im', 'ssm_c_re', 'ssm_c_im')

    def narrow(n):
        return weights[n].ndim == 4 and weights[n].shape[-1] < LANES // 2

    def travel(n):
        g = rep_grads[n]
        if narrow(n):
            g = jnp.swapaxes(g, 2, 3)
        if n in wide and g.shape[-1] % LANES == 0:
            return _view2d(g).astype(BF16)
        if n in wide:
            return g.reshape(-1, PACK_W if g.size % PACK_W == 0 else LANES).astype(BF16)
        return _view2d(g.reshape(weights[n].shape))

    small = tuple(n for n in REPLICATED if n != 'norm_mix_g')
    small_sent = [shards(dglu_w.astype(BF16))] + [travel(n) for n in small] + [loss_acc]
    small_handle, token = _exchange_start("reduce_small_start", small_sent, ['scatter'] + ['gather'] * (len(small) + 1),
                                          after=d_a_re)

    du = _pool_bwd_band(q, dpooled, _from_chunk_rows(du_ssm), token, tmp)
    dW_in = _mm("proj_in_dw", xn, du, 'tn', BF16)
    in_handle, token = _exchange_start("scatter_in_start", [shards(dW_in)], ['scatter'], after=dW_in)
    dx, dg1 = _mm_rows("proj_in_dx", du, W_in + token[0, 0].astype(BF16), 'nt', norm_bwd_add, [xs, dh1], [g1],
                       [(D, F32)], [(1, D)])
    rep_grads['norm_mix_g'] = dg1
    last_handle, token = _exchange_start("reduce_last_start", [travel('norm_mix_g')], ['gather'], after=dx)

    grads, delta, new_m, new_v = {}, {}, {}, {}

    def update(n, landed_n):
        shape = weights[n].shape
        if narrow(n):
            swapped = lambda a: _view2d(jnp.swapaxes(a, 2, 3))
            landed_n = landed_n.reshape((N_DEV,) + swapped(weights[n]).shape)
            res = _sum_adamw("adamw_" + n, landed_n, swapped(weights[n]), swapped(mom1[n]), swapped(mom2[n]))
            back = lambda r: jnp.swapaxes(r.reshape(shape[:2] + (shape[3], shape[2])), 2, 3)
            grads[n], delta[n], new_m[n], new_v[n] = [back(r) for r in res]
            return
        if n == 'w_up':
            res = _sum_adamw("adamw_" + n, landed_n, weights[n].T, mom1[n].T, mom2[n].T)
            grads[n], delta[n], new_m[n], new_v[n] = [r.T for r in res]
            return
        landed_n = landed_n.reshape((N_DEV,) + _view2d(weights[n]).shape)
        res = _sum_adamw("adamw_" + n, landed_n, _view2d(weights[n]), _view2d(mom1[n]), _view2d(mom2[n]))
        grads[n], delta[n], new_m[n], new_v[n] = [r.reshape(shape) for r in res]

    def update_many(name, named_landed):
        items = [(l.reshape((N_DEV,) + _view2d(weights[n]).shape), _view2d(weights[n]), _view2d(mom1[n]), _view2d(mom2[n]))
                 for n, l in named_landed]
        for (n, _), res in zip(named_landed, _sum_adamw_many(name, items)):
            grads[n], delta[n], new_m[n], new_v[n] = [r.reshape(weights[n].shape) for r in res]

    ffn_landed = _exchange_wait("scatter_ffn_wait", ffn_handle, token)
    for n, l in zip(('w_down', 'w_up'), ffn_landed):
        update(n, l)
    (out_landed,) = _exchange_wait("scatter_out_wait", out_handle, token)
    update('w_out', out_landed)
    small_landed = _exchange_wait("reduce_small_wait", small_handle, token)
    tiny = [('conv_w', ffn_landed[2])]
    for n, l in zip(('glu_w',) + small, small_landed):
        if weights[n].size <= ADAM_TINY:
            tiny.append((n, l))
        else:
            update(n, l)
    update_many("adamw_tiny", tiny)

    def loss_sum(parts_ref, o_ref):
        s = parts_ref[0]
        for q in range(1, N_DEV):
            s = s + parts_ref[q]
        o_ref[...] = s

    loss = pl.pallas_call(loss_sum, name="loss_sum", out_shape=jax.ShapeDtypeStruct((1, LANES), F32))(small_landed[-1])[0, 0]

    update('w_in', _exchange_wait("scatter_in_wait", in_handle, grads[small[-1]])[0])
    update('norm_mix_g', _exchange_wait("reduce_last_wait", last_handle, grads['w_in'])[0])

    return (loss, dx[None], *[grads[n] for n in WEIGHTS], *[delta[n] for n in WEIGHTS],
            *[new_m[n] for n in WEIGHTS], *[new_v[n] for n in WEIGHTS])
```

```python
import functools

import jax
import jax.numpy as jnp
import numpy as np
from jax import lax
from jax.experimental import pallas as pl
from jax.experimental.pallas import tpu as pltpu

F32 = jnp.float32
BF16 = jnp.bfloat16
MESH_ID = pl.DeviceIdType.MESH

N_DEV = 8
EPS = 1e-6
POOL_WINDOWS = (2, 4, 8, 16)
POOL_GROUP = 128
POOL_PAD = 64
SSM_GROUP = 16
SSM_STATE = 64
SSM_BLOCK_GROUPS = 8
N_CHUNK = 8
ROW_BLOCK = 512
SCAN_UNROLL = 8
GROUP_SHIFT = 4
STATE_SHIFT = 6
_NT = (((1,), (1,)), ((), ()))
_TN = (((0,), (0,)), ((), ()))
LANES = 128
PACK_W = 1024
VMEM_LIMIT = 56 * 1024 * 1024
MM_VMEM_BUDGET = 40 * 1024 * 1024
MM_TILE_CAP = 1408
MM_ROWS_TILES = (512, 1024)
ADAM_TILE_BYTES = 4 * 1024 * 1024
CONV_COLS = 128
ADAM_TINY = 8192

ADAM_LR = 0.001
ADAM_B1 = 0.9
ADAM_B2 = 0.999
ADAM_EPS = 1e-08
ADAM_WD = 0.01
ADAM_STEP = 10

WEIGHTS = ['norm_mix_g', 'w_in', 'pool_w', 'pool_scale', 'ssm_log_neg_a_re', 'ssm_a_im', 'ssm_log_dt',
           'ssm_b_re', 'ssm_b_im', 'ssm_c_re', 'ssm_c_im', 'ssm_d', 'glu_w', 'glu_b', 'out_norm_pool_g',
           'out_norm_ssm_g', 'w_out', 'norm_ffn_g', 'w_up', 'conv_w', 'conv_b', 'w_down', 'final_norm_g']
SHARDED = ('w_in', 'w_out', 'w_down', 'glu_w', 'w_up', 'conv_w')
REPLICATED = tuple(n for n in WEIGHTS if n not in SHARDED)


def _pick(n, prefs):
    for p in prefs:
        if n % p == 0:
            return p
    return n


def _params(sem, vmem=None):
    return pltpu.CompilerParams(dimension_semantics=sem, vmem_limit_bytes=vmem or VMEM_LIMIT)


def _tiles(n, cap):
    return [d for d in range(LANES, min(n, cap) + 1, LANES) if n % d == 0] or [n]


def _mm(name, a, b, mode, out_dtype, add=None, b_rows=(1, 0), a_plane=None):
    batch = a.shape[0] if a.ndim == 3 and a_plane is None else None
    if mode == 'nn':
        (M, K), N = a.shape[-2:], b.shape[1]
        assert b.shape[0] == K * b_rows[0]
    elif mode == 'nt':
        (M, K), (N, _) = a.shape[-2:], b.shape
    else:
        (K, M), (_, N) = a.shape[-2:], b.shape
    dims = {'nn': (((1,), (0,)), ((), ())), 'nt': (((1,), (1,)), ((), ())), 'tn': (((0,), (0,)), ((), ()))}[mode]
    sa, sb, so = a.dtype.itemsize, b.dtype.itemsize, jnp.dtype(out_dtype).itemsize
    best = None
    for tm in _tiles(M, MM_TILE_CAP):
        for tn in _tiles(N, MM_TILE_CAP):
            need = 2 * (tm * K * sa + tn * K * sb + tm * tn * (so + (4 if add is not None else 0)))
            if need <= MM_VMEM_BUDGET:
                key = (tm * tn / (tm + tn), tm * tn)
                if best is None or key > best[0]:
                    best = (key, tm, tn)
    _, tm, tn = best
    rows_inner = a.size * sa * (N // tn) + b.size * sb < a.size * sa + b.size * sb * (M // tm)

    def kern(*refs):
        a_ref, b_ref = refs[:2]
        o_ref = refs[-1]
        r = lax.dot_general(a_ref[...].astype(BF16), b_ref[...].astype(BF16), dims, preferred_element_type=F32)
        if add is not None:
            r = r + refs[2][...]
        o_ref[...] = r.astype(o_ref.dtype)

    if rows_inner:
        grid = (N // tn, M // tm)
        ij = lambda g0, g1: (g1, g0)
    else:
        grid = (M // tm, N // tn)
        ij = lambda g0, g1: (g0, g1)

    def spec(block, index, batched=False):
        if batch is None:
            return pl.BlockSpec(block, lambda g0, g1: index(*ij(g0, g1)))
        if batched:
            return pl.BlockSpec((None,) + block, lambda p, g0, g1: (p,) + index(*ij(g0, g1)))
        return pl.BlockSpec(block, lambda p, g0, g1: index(*ij(g0, g1)))

    if a_plane is not None:
        assert mode != 'tn'
        a_spec = pl.BlockSpec((None, tm, K), lambda g0, g1: (a_plane, ij(g0, g1)[0], 0))
    else:
        a_spec = (spec((K, tm), lambda i, j: (0, i), True) if mode == 'tn' else spec((tm, K), lambda i, j: (i, 0), True))
    b_spec = (spec((tn, K), lambda i, j: (j, 0)) if mode == 'nt' else spec((K, tn), lambda i, j: (b_rows[1], j)))
    in_specs = [a_spec, b_spec]
    args = [a, b]
    if add is not None:
        in_specs.append(spec((tm, tn), lambda i, j: (i, j), True))
        args.append(add)
    lead = () if batch is None else (batch,)
    return pl.pallas_call(
        kern, name=name, grid=lead + grid, in_specs=in_specs,
        out_specs=spec((tm, tn), lambda i, j: (i, j), True),
        out_shape=jax.ShapeDtypeStruct(lead + (M, N), out_dtype),
        compiler_params=_params(("parallel",) * (len(lead) + 2)),
    )(*args)


def _store_rows_and_sums(i, outs, res, n_row):
    for o, v in zip(outs[:n_row], res[:n_row]):
        o[...] = v.astype(o.dtype)
    for o, v in zip(outs[n_row:], res[n_row:]):
        @pl.when(i == 0)
        def _(o=o, v=v):
            o[...] = v

        @pl.when(i != 0)
        def _(o=o, v=v):
            o[...] += v


def _mm_rows(name, a, b, mode, fn, rows, fulls, out_rows, out_accs=(), add=None, b_rows=(1, 0), a_plane=None):
    M, K = a.shape[-2:]
    N = b.shape[1] if mode == 'nn' else b.shape[0]
    assert b.shape[0] == (K * b_rows[0] if mode == 'nn' else N)
    dims = _NT if mode == 'nt' else (((1,), (0,)), ((), ()))
    n_extra, n_row = len(rows) + len(fulls), len(out_rows)
    row_bytes = (K * a.dtype.itemsize + (4 * N if add is not None else 0) + sum(4 * r.shape[1] for r in rows)
                 + sum(c * jnp.dtype(dt).itemsize for c, dt in out_rows))
    fits = [t for t in MM_ROWS_TILES if M % t == 0 and 2 * (t * row_bytes + K * N * b.dtype.itemsize) <= MM_VMEM_BUDGET]
    tm = max(fits) if fits else min(MM_ROWS_TILES)

    def kern(*refs):
        i = pl.program_id(0)
        r = lax.dot_general(refs[0][...].astype(BF16), refs[1][...].astype(BF16), dims, preferred_element_type=F32)
        pos = 2
        if add is not None:
            r = r + refs[pos][...]
            pos += 1
        res = fn(r, *[x[...] for x in refs[pos:pos + n_extra]])
        _store_rows_and_sums(i, refs[pos + n_extra:], res, n_row)

    def full_spec(shape):
        nd = len(shape)
        return pl.BlockSpec(tuple(shape), lambda i: (0,) * nd)

    a_spec = (pl.BlockSpec((tm, K), lambda i: (i, 0)) if a_plane is None
              else pl.BlockSpec((None, tm, K), lambda i: (a_plane, i, 0)))
    b_spec = pl.BlockSpec((K, N), lambda i: (b_rows[1], 0)) if mode == 'nn' else pl.BlockSpec((N, K), lambda i: (0, 0))
    ins = [_row_in(r, tm, M) for r in ([add] if add is not None else []) + list(rows)]
    outs = [_row_out(o, tm, M) for o in out_rows]
    res = pl.pallas_call(
        kern, name=name, grid=(M // tm,),
        in_specs=[a_spec, b_spec] + [s for s, _ in ins] + [full_spec(f.shape) for f in fulls],
        out_specs=[s for s, _ in outs] + [full_spec(s) for s in out_accs],
        out_shape=[sh for _, sh in outs] + [jax.ShapeDtypeStruct(tuple(s), F32) for s in out_accs],
        compiler_params=_params(("arbitrary",)),
    )(a, b, *[x for _, x in ins], *fulls)
    return res


def _row_in(r, tm, T):
    arr, w, k = r if isinstance(r, tuple) else (r, r.shape[1], 0)
    return pl.BlockSpec((tm, w), lambda i: (i, k)), arr


def _row_out(o, tm, T):
    c, dt = o
    return pl.BlockSpec((tm, c), lambda i: (i, 0)), jax.ShapeDtypeStruct((T, c), dt)


def _rowmap(name, fn, rows, fulls, out_rows, out_accs=(), tm=1024):
    T = (rows[0][0] if isinstance(rows[0], tuple) else rows[0]).shape[0]
    tm = min(tm, T)
    assert T % tm == 0
    n_in, n_row = len(rows) + len(fulls), len(out_rows)

    def kern(*refs):
        i = pl.program_id(0)
        res = fn(*[r[...] for r in refs[:n_in]])
        res = res if isinstance(res, (tuple, list)) else (res,)
        _store_rows_and_sums(i, refs[n_in:], res, n_row)

    def full_spec(shape):
        nd = len(shape)
        return pl.BlockSpec(tuple(shape), lambda i: (0,) * nd)

    ins = [_row_in(r, tm, T) for r in rows]
    outs = [_row_out(o, tm, T) for o in out_rows]
    res = pl.pallas_call(
        kern, name=name, grid=(T // tm,),
        in_specs=[s for s, _ in ins] + [full_spec(f.shape) for f in fulls],
        out_specs=[s for s, _ in outs] + [full_spec(s) for s in out_accs],
        out_shape=[sh for _, sh in outs] + [jax.ShapeDtypeStruct(tuple(s), F32) for s in out_accs],
        compiler_params=_params(("arbitrary",)),
    )(*[x for _, x in ins], *fulls)
    return res


def _colsum(v):
    return jnp.sum(v, axis=0, keepdims=True)


def _rowmean(v):
    return jnp.mean(v, axis=-1, keepdims=True)


def _rms_fwd(x, g):
    r = lax.rsqrt(_rowmean(x * x) + EPS)
    return x * r * g


def _rms_bwd(x, g, dy):
    r = lax.rsqrt(_rowmean(x * x) + EPS)
    xh = x * r
    dxh = dy * g
    return r * (dxh - xh * _rowmean(dxh * xh)), _colsum(dy * xh)


def _sigmoid(v):
    return 0.5 * jnp.tanh(0.5 * v) + 0.5


def _gelu(y):
    c = np.sqrt(2.0 / np.pi).astype(np.float32)
    return 0.5 * y * (1.0 + jnp.tanh(c * (y + 0.044715 * (y * y * y))))


def _gelu_and_grad(y):
    c = np.sqrt(2.0 / np.pi).astype(np.float32)
    th = jnp.tanh(c * (y + 0.044715 * (y * y * y)))
    return (0.5 * y * (1.0 + th),
            0.5 * (1.0 + th) + 0.5 * y * (1.0 - th * th) * c * (1.0 + 3.0 * 0.044715 * (y * y)))


def _split3(v):
    hi = v.astype(BF16)
    r1 = v - hi.astype(F32)
    mid = r1.astype(BF16)
    lo = (r1 - mid.astype(F32)).astype(BF16)
    return hi, mid, lo


def _band(tm, lo_off, hi_off):
    shape = (tm, tm + 2 * POOL_PAD)
    d = lax.broadcasted_iota(jnp.int32, shape, 1) - lax.broadcasted_iota(jnp.int32, shape, 0) - POOL_PAD
    return ((d >= lo_off) & (d < hi_off)).astype(BF16)


def _window_sums(ext, tm, lo_off, hi_off):
    hi, mid, lo = _split3(ext)
    band = _band(LANES, lo_off, hi_off)
    dot = functools.partial(jnp.dot, preferred_element_type=F32)
    out = []
    for s in range(tm // LANES):
        rows = slice(s * LANES, (s + 2) * LANES)
        out.append(dot(band, hi[rows]) + dot(band, mid[rows]) + dot(band, lo[rows]))
    return jnp.concatenate(out, axis=0)


def _window_count(r0, tm, half, T):
    t = r0 + lax.broadcasted_iota(jnp.int32, (tm, 1), 0)
    return (jnp.minimum(t + half, T) - jnp.maximum(t - half, 0)).astype(F32)


def _halo_specs(tm, C, T):
    per = tm // POOL_PAD
    last = T // POOL_PAD - 1
    return [pl.BlockSpec((POOL_PAD, C), lambda i: (jnp.maximum(i * per - 1, 0), 0)),
            pl.BlockSpec((tm, C), lambda i: (i, 0)),
            pl.BlockSpec((POOL_PAD, C), lambda i: (jnp.minimum((i + 1) * per, last), 0))]


def _with_halo(before_ref, main_ref, after_ref, i, n):
    before = jnp.where(i > 0, before_ref[...], 0.0)
    after = jnp.where(i < n - 1, after_ref[...], 0.0)
    return jnp.concatenate([before, main_ref[...], after], axis=0)


def _pool_block(ext, ctr, w_ref, r0, tm, T):
    pooled, conc = [], []
    for gi, w in enumerate(POOL_WINDOWS):
        half = w // 2
        cols = slice(gi * POOL_GROUP, (gi + 1) * POOL_GROUP)
        ws = _window_sums(ext[:, cols], tm, -half, half)
        p = ws / _window_count(r0, tm, half, T) - ctr[:, cols]
        pooled.append(p)
        conc.append(jnp.dot(p.astype(BF16), w_ref[gi], preferred_element_type=F32))
    return pooled, jnp.concatenate(conc, axis=1)


def _pool_fwd(u, pool_w_bf, pool_scale, g_pool, tm):
    T = u.shape[0]
    nw = len(POOL_WINDOWS)
    C = nw * POOL_GROUP
    n = T // tm

    def kern(ub_ref, u_ref, ua_ref, w_ref, sc_ref, g_ref, o_ref):
        i = pl.program_id(0)
        r0 = pl.multiple_of(i * tm, tm)
        _, conc = _pool_block(_with_halo(ub_ref, u_ref, ua_ref, i, n), u_ref[...], w_ref, r0, tm, T)
        o_ref[...] = _rms_fwd(conc * sc_ref[...], g_ref[...]).astype(o_ref.dtype)

    return pl.pallas_call(
        kern, name="pool_fwd", grid=(n,),
        in_specs=_halo_specs(tm, C, T) + [pl.BlockSpec(pool_w_bf.shape, lambda i: (0, 0, 0)),
                                          pl.BlockSpec((1, C), lambda i: (0, 0)), pl.BlockSpec((1, C), lambda i: (0, 0))],
        out_specs=pl.BlockSpec((tm, C), lambda i: (i, 0)),
        out_shape=jax.ShapeDtypeStruct((T, C), BF16),
        compiler_params=_params(("arbitrary",)),
    )(u, u, u, pool_w_bf, pool_scale, g_pool)


def _pool_bwd_rows(u, d_y, pool_w_bf, pool_scale, g_pool, tm):
    T = u.shape[0]
    nw = len(POOL_WINDOWS)
    C = nw * POOL_GROUP
    n = T // tm

    def kern(ub_ref, u_ref, ua_ref, dy_ref, w_ref, sc_ref, g_ref, q_ref, dp_ref, dw_ref, dsc_ref, dg_ref):
        i = pl.program_id(0)
        r0 = pl.multiple_of(i * tm, tm)
        pooled, conc = _pool_block(_with_halo(ub_ref, u_ref, ua_ref, i, n), u_ref[...], w_ref, r0, tm, T)
        sc = sc_ref[...]
        dyp, dg = _rms_bwd(conc * sc, g_ref[...], dy_ref[...])
        dsc = _colsum(dyp * conc)
        dconc = (dyp * sc).astype(BF16)
        dws = []
        for gi, w in enumerate(POOL_WINDOWS):
            cols = slice(gi * POOL_GROUP, (gi + 1) * POOL_GROUP)
            dc = dconc[:, cols]
            dp = lax.dot_general(dc, w_ref[gi], (((1,), (1,)), ((), ())), preferred_element_type=F32)
            dp_ref[:, cols] = dp
            q_ref[:, cols] = dp / _window_count(r0, tm, w // 2, T)
            dws.append(lax.dot_general(pooled[gi].astype(BF16), dc, (((0,), (0,)), ((), ())), preferred_element_type=F32))

        @pl.when(i == 0)
        def _():
            for gi in range(nw):
                dw_ref[gi] = dws[gi]
            dsc_ref[...] = dsc
            dg_ref[...] = dg

        @pl.when(i != 0)
        def _():
            for gi in range(nw):
                dw_ref[gi] += dws[gi]
            dsc_ref[...] += dsc
            dg_ref[...] += dg

    full2 = pl.BlockSpec((1, C), lambda i: (0, 0))
    row_blk = pl.BlockSpec((tm, C), lambda i: (i, 0))
    return pl.pallas_call(
        kern, name="pool_bwd_rows", grid=(n,),
        in_specs=_halo_specs(tm, C, T) + [row_blk, pl.BlockSpec(pool_w_bf.shape, lambda i: (0, 0, 0)), full2, full2],
        out_specs=[row_blk, row_blk, pl.BlockSpec((nw, POOL_GROUP, POOL_GROUP), lambda i: (0, 0, 0)), full2, full2],
        out_shape=[jax.ShapeDtypeStruct((T, C), F32), jax.ShapeDtypeStruct((T, C), F32),
                   jax.ShapeDtypeStruct((nw, POOL_GROUP, POOL_GROUP), F32),
                   jax.ShapeDtypeStruct((1, C), F32), jax.ShapeDtypeStruct((1, C), F32)],
        compiler_params=_params(("arbitrary",)),
    )(u, u, u, d_y, pool_w_bf, pool_scale, g_pool)


def _pool_bwd_band(q, dpooled, du_other, after, tm):
    T, C = dpooled.shape
    C2 = du_other.shape[1]
    n = T // tm

    def kern(qb_ref, q_ref, qa_ref, dp_ref, other_ref, after_ref, o_ref):
        ext = _with_halo(qb_ref, q_ref, qa_ref, pl.program_id(0), n)
        for gi, w in enumerate(POOL_WINDOWS):
            half = w // 2
            cols = slice(gi * POOL_GROUP, (gi + 1) * POOL_GROUP)
            du = _window_sums(ext[:, cols], tm, -half + 1, half + 1) - dp_ref[:, cols]
            o_ref[:, cols] = du.astype(o_ref.dtype)
        o_ref[:, C:] = other_ref[...].astype(o_ref.dtype)

    return pl.pallas_call(
        kern, name="pool_bwd_band", grid=(n,),
        in_specs=_halo_specs(tm, C, T) + [pl.BlockSpec((tm, C), lambda i: (i, 0)), pl.BlockSpec((tm, C2), lambda i: (i, 0)),
                                          pl.BlockSpec(after.shape, lambda i: (0, 0))],
        out_specs=pl.BlockSpec((tm, C + C2), lambda i: (i, 0)),
        out_shape=jax.ShapeDtypeStruct((T, C + C2), BF16),
        compiler_params=_params(("arbitrary",)),
    )(q, q, q, dpooled, du_other, after)


def _to_chunk_rows(a):
    T, C = a.shape
    return a.reshape(N_CHUNK, T // N_CHUNK, C).transpose(1, 0, 2).reshape(T, C)


def _from_chunk_rows(a):
    T, C = a.shape
    return a.reshape(T // N_CHUNK, N_CHUNK, C).transpose(1, 0, 2).reshape(T, C)


def _cmul(ar, ai, br, bi):
    return ar * br - ai * bi, ar * bi + ai * br


def _ssm_discretise(log_neg_a_re, a_im, log_dt):
    dt = jnp.exp(log_dt)
    a_re = -jnp.exp(log_neg_a_re)
    mag = jnp.exp(a_re * dt)
    ang = a_im * dt
    lam_re, lam_im = mag * jnp.cos(ang), mag * jnp.sin(ang)
    den = a_re * a_re + a_im * a_im
    f_re = ((lam_re - 1.0) * a_re + lam_im * a_im) / den
    f_im = (lam_im * a_re - (lam_re - 1.0) * a_im) / den
    return lam_re, lam_im, f_re, f_im


def _ssm_params_fwd(log_neg_a_re, a_im, log_dt):
    rows, n = log_neg_a_re.shape

    def kern(a_ref, b_ref, c_ref, o1, o2, o3, o4):
        for o, v in zip((o1, o2, o3, o4), _ssm_discretise(a_ref[...], b_ref[...], c_ref[...])):
            o[...] = v

    return pl.pallas_call(kern, name="ssm_params_fwd", out_shape=[jax.ShapeDtypeStruct((rows, n), F32)] * 4)(
        log_neg_a_re, a_im, log_dt)


def _ssm_params_bwd(log_neg_a_re, a_im, log_dt, cots):
    rows, n = log_neg_a_re.shape

    def kern(a_ref, b_ref, c_ref, g1, g2, g3, g4, o1, o2, o3):
        _, vjp = jax.vjp(_ssm_discretise, a_ref[...], b_ref[...], c_ref[...])
        d1, d2, d3 = vjp((g1[...], g2[...], g3[...], g4[...]))
        o1[...] = d1
        o2[...] = d2
        o3[...] = d3

    return pl.pallas_call(
        kern, name="ssm_params_bwd",
        out_shape=[jax.ShapeDtypeStruct((rows, n), F32), jax.ShapeDtypeStruct((rows, n), F32),
                   jax.ShapeDtypeStruct((rows, 1), F32)])(log_neg_a_re, a_im, log_dt, *cots)


def _cpow(lr, li, n):
    out = None
    br, bi = lr, li
    while n:
        if n & 1:
            out = (br, bi) if out is None else _cmul(out[0], out[1], br, bi)
        n >>= 1
        if n:
            br, bi = _cmul(br, bi, br, bi)
    return out


def _slab(t):
    return pl.ds(pl.multiple_of(t * N_CHUNK, N_CHUNK), N_CHUNK)


def _steps(n, step, carry):
    main = n // SCAN_UNROLL

    def body(i, c):
        for r in range(SCAN_UNROLL):
            c = step(i * SCAN_UNROLL + r, c)
        return c

    carry = lax.fori_loop(0, main, body, carry)
    for k in range(main * SCAN_UNROLL, n):
        carry = step(jnp.int32(k), carry)
    return carry


def _shift_chunks(vr, vi, reverse):
    sub = lax.broadcasted_iota(jnp.int32, vr.shape, 0)
    if reverse:
        keep = sub != N_CHUNK - 1
        return jnp.where(keep, pltpu.roll(vr, N_CHUNK - 1, 0), 0.0), jnp.where(keep, pltpu.roll(vi, N_CHUNK - 1, 0), 0.0)
    keep = sub != 0
    return jnp.where(keep, pltpu.roll(vr, 1, 0), 0.0), jnp.where(keep, pltpu.roll(vi, 1, 0), 0.0)


def _chunk_scan(xr, xi, lr, li, Lc, reverse, prev=None):
    C = xr.shape[1]
    lrb, lib = jnp.broadcast_to(lr, (N_CHUNK, C)), jnp.broadcast_to(li, (N_CHUNK, C))
    zero = jnp.zeros((N_CHUNK, C), F32)

    def step_of(k):
        return (Lc - 1 - k) if reverse else k

    def advance(sr, si, t):
        return lrb * sr - lib * si + xr[_slab(t), :], lrb * si + lib * sr + xi[_slab(t), :]

    def pass1(k, c):
        return advance(c[0], c[1], step_of(k))

    er, ei = _steps(Lc, pass1, (zero, zero))
    pr, pi = _cpow(lr, li, Lc)
    prb, pib = jnp.broadcast_to(pr, (N_CHUNK, C)), jnp.broadcast_to(pi, (N_CHUNK, C))
    cr, ci = zero, zero
    for _ in range(N_CHUNK - 1):
        cr, ci = _shift_chunks(er + prb * cr - pib * ci, ei + prb * ci + pib * cr, reverse)

    if prev is None:
        def pass2(k, c):
            t = step_of(k)
            nr, ni = advance(c[0], c[1], t)
            xr[_slab(t), :] = nr
            xi[_slab(t), :] = ni
            return nr, ni

        _steps(Lc, pass2, (cr, ci))
        return None

    qr, qi = prev
    def pair(ar, ai, wr, wi, acc):
        return acc[0] + ar * wr + ai * wi, acc[1] + ai * wr - ar * wi

    def pass2(k, c):
        sr, si, accr, acci = c
        t = step_of(k)
        nr, ni = advance(sr, si, t)
        xr[_slab(t), :] = nr
        xi[_slab(t), :] = ni
        tp = t - 1 if reverse else t + 1
        accr, acci = pair(nr, ni, qr[_slab(tp), :], qi[_slab(tp), :], (accr, acci))
        return nr, ni, accr, acci

    sr, si, accr, acci = _steps(Lc - 1, pass2, (cr, ci, zero, zero))
    t = step_of(Lc - 1)
    nr, ni = advance(sr, si, t)
    xr[_slab(t), :] = nr
    xi[_slab(t), :] = ni
    edge = step_of(0)
    wr, wi = _shift_chunks(qr[_slab(edge), :], qi[_slab(edge), :], not reverse)
    return pair(nr, ni, wr, wi, (accr, acci))


def _ssm_dims(T):
    assert T % ROW_BLOCK == 0
    return T // N_CHUNK, T // ROW_BLOCK


def _group_mask():
    shape = (SSM_BLOCK_GROUPS * SSM_GROUP, SSM_BLOCK_GROUPS * SSM_STATE)
    rows = lax.broadcasted_iota(jnp.int32, shape, 0) >> GROUP_SHIFT
    cols = lax.broadcasted_iota(jnp.int32, shape, 1) >> STATE_SHIFT
    return rows == cols


def _expand_groups(compact):
    return jnp.where(_group_mask(), jnp.tile(compact, (SSM_BLOCK_GROUPS, 1)), 0.0).astype(BF16)


def _collect_groups(block):
    kept = jnp.where(_group_mask(), block, 0.0)
    out = kept[0:SSM_GROUP]
    for g in range(1, SSM_BLOCK_GROUPS):
        out = out + kept[g * SSM_GROUP:(g + 1) * SSM_GROUP]
    return out


def _ssm_specs(T, direction):
    SC = SSM_BLOCK_GROUPS * SSM_STATE
    col = pl.BlockSpec((T, LANES), lambda m: (0, m))
    mat = pl.BlockSpec((None, None, SSM_GROUP, SC), lambda m: (direction, m, 0, 0))
    vec = pl.BlockSpec((None, None, 1, SC), lambda m: (direction, m, 0, 0))
    return col, mat, vec


def _ssm_project_in(u_ref, bre, bim, xr, xi, nblk):
    for blk in range(nblk):
        rows = pl.ds(blk * ROW_BLOCK, ROW_BLOCK)
        ub = u_ref[rows, :].astype(BF16)
        xr[rows, :] = jnp.dot(ub, bre, preferred_element_type=F32)
        xi[rows, :] = jnp.dot(ub, bim, preferred_element_type=F32)


def _ssm_fwd(name, u_bf, mats, vecs, direction):
    T, W = u_bf.shape
    SC = SSM_BLOCK_GROUPS * SSM_STATE
    nb = W // LANES
    Lc, nblk = _ssm_dims(T)
    reverse = bool(direction)

    def kern(u_ref, bre_ref, bim_ref, cre_ref, cim_ref, lr_ref, li_ref, fr_ref, fi_ref, y_ref, xr, xi):
        _ssm_project_in(u_ref, _expand_groups(bre_ref[...]), _expand_groups(bim_ref[...]), xr, xi, nblk)
        _chunk_scan(xr, xi, lr_ref[...], li_ref[...], Lc, reverse)
        fr, fi = fr_ref[...], fi_ref[...]
        cre, cim = _expand_groups(cre_ref[...]), _expand_groups(cim_ref[...])
        for blk in range(nblk):
            rows = pl.ds(blk * ROW_BLOCK, ROW_BLOCK)
            sr, si = _cmul(fr, fi, xr[rows, :], xi[rows, :])
            y_ref[rows, :] = (lax.dot_general(sr.astype(BF16), cre, _NT, preferred_element_type=F32)
                              - lax.dot_general(si.astype(BF16), cim, _NT, preferred_element_type=F32))

    col, mat, vec = _ssm_specs(T, direction)
    return pl.pallas_call(
        kern, name=name, grid=(nb,),
        in_specs=[col, mat, mat, mat, mat, vec, vec, vec, vec],
        out_specs=col, out_shape=jax.ShapeDtypeStruct((T, W), F32),
        scratch_shapes=[pltpu.VMEM((T, SC), F32), pltpu.VMEM((T, SC), F32)],
        compiler_params=_params(("arbitrary",)),
    )(u_bf, *mats, *vecs)


def _ssm_bwd(name, u_bf, dy_bf, du_in, mats, vecs, direction, du_dtype):
    T, W = u_bf.shape
    SC = SSM_BLOCK_GROUPS * SSM_STATE
    nb = W // LANES
    Lc, nblk = _ssm_dims(T)
    reverse = bool(direction)

    def kern(u_ref, dy_ref, dui_ref, bre_ref, bim_ref, cre_ref, cim_ref, lr_ref, li_ref, fr_ref, fi_ref,
             du_ref, dbre_ref, dbim_ref, dcre_ref, dcim_ref, dlr_ref, dli_ref, dfr_ref, dfi_ref,
             xr, xi, gr, gi):
        lr, li, fr, fi = lr_ref[...], li_ref[...], fr_ref[...], fi_ref[...]
        bre, bim = _expand_groups(bre_ref[...]), _expand_groups(bim_ref[...])
        cre, cim = _expand_groups(cre_ref[...]), _expand_groups(cim_ref[...])
        _ssm_project_in(u_ref, bre, bim, xr, xi, nblk)
        _chunk_scan(xr, xi, lr, li, Lc, reverse)
        mre = jnp.zeros((LANES, SC), F32)
        mim = jnp.zeros((LANES, SC), F32)
        dfr = jnp.zeros((1, SC), F32)
        dfi = jnp.zeros((1, SC), F32)
        for blk in range(nblk):
            rows = pl.ds(blk * ROW_BLOCK, ROW_BLOCK)
            dy = dy_ref[rows, :].astype(BF16)
            x_r, x_i = xr[rows, :], xi[rows, :]
            g_r = jnp.dot(dy, cre, preferred_element_type=F32)
            g_i = -jnp.dot(dy, cim, preferred_element_type=F32)
            mre += lax.dot_general(dy, x_r.astype(BF16), _TN, preferred_element_type=F32)
            mim += lax.dot_general(dy, x_i.astype(BF16), _TN, preferred_element_type=F32)
            dfr += _colsum(g_r * x_r + g_i * x_i)
            dfi += _colsum(g_i * x_r - g_r * x_i)
            gr[rows, :] = fr * g_r + fi * g_i
            gi[rows, :] = fr * g_i - fi * g_r
        dcre_ref[...] = _collect_groups(fr * mre - fi * mim)
        dcim_ref[...] = _collect_groups(-(fr * mim + fi * mre))
        dfr_ref[...] = dfr
        dfi_ref[...] = dfi
        accr, acci = _chunk_scan(gr, gi, lr, -li, Lc, not reverse, prev=(xr, xi))
        dlr_ref[...] = _colsum(accr)
        dli_ref[...] = _colsum(acci)
        dbre = jnp.zeros((LANES, SC), F32)
        dbim = jnp.zeros((LANES, SC), F32)
        for blk in range(nblk):
            rows = pl.ds(blk * ROW_BLOCK, ROW_BLOCK)
            ub = u_ref[rows, :].astype(BF16)
            a_r, a_i = gr[rows, :].astype(BF16), gi[rows, :].astype(BF16)
            dbre += lax.dot_general(ub, a_r, _TN, preferred_element_type=F32)
            dbim += lax.dot_general(ub, a_i, _TN, preferred_element_type=F32)
            du_ref[rows, :] = (dui_ref[rows, :] + lax.dot_general(a_r, bre, _NT, preferred_element_type=F32)
                               + lax.dot_general(a_i, bim, _NT, preferred_element_type=F32)).astype(du_ref.dtype)
        dbre_ref[...] = _collect_groups(dbre)
        dbim_ref[...] = _collect_groups(dbim)

    col, mat, vec = _ssm_specs(T, direction)
    mat_out = pl.BlockSpec((None, SSM_GROUP, SC), lambda m: (m, 0, 0))
    vec_out = pl.BlockSpec((None, 1, SC), lambda m: (m, 0, 0))
    mat_shape = jax.ShapeDtypeStruct((nb, SSM_GROUP, SC), F32)
    vec_shape = jax.ShapeDtypeStruct((nb, 1, SC), F32)
    return pl.pallas_call(
        kern, name=name, grid=(nb,),
        in_specs=[col, col, col, mat, mat, mat, mat, vec, vec, vec, vec],
        out_specs=[col, mat_out, mat_out, mat_out, mat_out, vec_out, vec_out, vec_out, vec_out],
        out_shape=[jax.ShapeDtypeStruct((T, W), du_dtype), mat_shape, mat_shape, mat_shape, mat_shape,
                   vec_shape, vec_shape, vec_shape, vec_shape],
        scratch_shapes=[pltpu.VMEM((T, SC), F32)] * 4,
        compiler_params=_params(("arbitrary",)),
    )(u_bf, dy_bf, du_in, *mats, *vecs)


def _groups_side_by_side(p, channel_axis):
    d, G = p.shape[:2]
    nb = G // SSM_BLOCK_GROUPS
    v = p.reshape(d, nb, SSM_BLOCK_GROUPS, p.shape[2], p.shape[3])
    v = v.transpose(0, 1, 4, 2, 3) if channel_axis == 3 else v.transpose(0, 1, 3, 2, 4)
    return v.reshape(d, nb, SSM_GROUP, SSM_BLOCK_GROUPS * SSM_STATE)


def _groups_apart(m, channel_axis):
    d, nb = m.shape[:2]
    v = m.reshape(d, nb, SSM_GROUP, SSM_BLOCK_GROUPS, SSM_STATE)
    v = v.transpose(0, 1, 3, 4, 2) if channel_axis == 3 else v.transpose(0, 1, 3, 2, 4)
    return v.reshape((d, nb * SSM_BLOCK_GROUPS) + v.shape[3:])


def _conv_taps(ref, c, R, T):
    n = T // R
    r0 = pl.multiple_of(c * R, R)
    main = ref[pl.ds(r0, R), :]
    before = ref[pl.ds(pl.multiple_of(jnp.maximum(r0 - 8, 0), 8), 8), :]
    after = ref[pl.ds(pl.multiple_of(jnp.minimum(r0 + R, T - 8), 8), 8), :]
    ext = jnp.concatenate([jnp.where(c > 0, before, 0.0), main, jnp.where(c < n - 1, after, 0.0)], axis=0)
    m1 = pltpu.roll(ext, 1, 0)[8:8 + R]
    p1 = pltpu.roll(ext, R + 15, 0)[8:8 + R]
    return m1, main, p1, r0


def _conv_specs(T, F, cb):
    nj = F // cb
    lo = lambda rows: pl.BlockSpec((rows, cb), lambda j: (0, j))
    hi = lambda rows: pl.BlockSpec((rows, cb), lambda j: (0, j + nj))
    return nj, lo, hi


def _convact_fwd(up, conv_w, conv_b, F):
    T = up.shape[0]
    cb = CONV_COLS
    R = _pick(T, (256, 128, 64))
    nj, lo, hi = _conv_specs(T, F, cb)

    def kern(uv, ug, wv, wg, bv, bg, o_ref):
        def body(c, carry):
            v1, v0, v2, r0 = _conv_taps(uv, c, R, T)
            g1, g0, g2, _ = _conv_taps(ug, c, R, T)
            val = v1 * wv[0:1, :] + v0 * wv[1:2, :] + v2 * wv[2:3, :] + bv[...]
            gate = g1 * wg[0:1, :] + g0 * wg[1:2, :] + g2 * wg[2:3, :] + bg[...]
            half = 0.5 * gate
            o_ref[pl.ds(r0, R), :] = (val * (half * jnp.tanh(half) + half)).astype(o_ref.dtype)
            return carry

        lax.fori_loop(0, T // R, body, 0)

    return pl.pallas_call(
        kern, name="convact_fwd", grid=(nj,),
        in_specs=[lo(T), hi(T), lo(3), hi(3), lo(1), hi(1)],
        out_specs=lo(T), out_shape=jax.ShapeDtypeStruct((T, F), BF16),
        compiler_params=_params(("arbitrary",)),
    )(up, up, conv_w, conv_w, conv_b, conv_b)


def _convact_bwd(up, dact, conv_w, conv_b, F):
    T = up.shape[0]
    cb = CONV_COLS
    R = _pick(T, (256, 128, 64))
    nj, lo, hi = _conv_specs(T, F, cb)

    def kern(uv, ug, da, wv, wg, bv, bg, dup, dwv, dwg, dbv, dbg, sv, sg):
        zero = jnp.zeros((1, cb), F32)

        def pass_a(c, acc):
            v1, v0, v2, r0 = _conv_taps(uv, c, R, T)
            g1, g0, g2, _ = _conv_taps(ug, c, R, T)
            val = v1 * wv[0:1, :] + v0 * wv[1:2, :] + v2 * wv[2:3, :] + bv[...]
            gate = g1 * wg[0:1, :] + g0 * wg[1:2, :] + g2 * wg[2:3, :] + bg[...]
            sig = _sigmoid(gate)
            d = da[pl.ds(r0, R), :]
            silu = gate * sig
            dval = d * silu
            dgate = d * val * (sig + silu - silu * sig)
            sv[pl.ds(r0, R), :] = dval
            sg[pl.ds(r0, R), :] = dgate
            terms = (dval * v1, dval * v0, dval * v2, dval, dgate * g1, dgate * g0, dgate * g2, dgate)
            return tuple(a + _colsum(t) for a, t in zip(acc, terms))

        acc = lax.fori_loop(0, T // R, pass_a, (zero,) * 8)
        for k in range(3):
            dwv[k:k + 1, :] = acc[k]
            dwg[k:k + 1, :] = acc[4 + k]
        dbv[...] = acc[3]
        dbg[...] = acc[7]

        def pass_b(c, carry):
            v1, v0, v2, r0 = _conv_taps(sv, c, R, T)
            g1, g0, g2, _ = _conv_taps(sg, c, R, T)
            dup[0, pl.ds(r0, R), :] = (v2 * wv[0:1, :] + v0 * wv[1:2, :] + v1 * wv[2:3, :]).astype(dup.dtype)
            dup[1, pl.ds(r0, R), :] = (g2 * wg[0:1, :] + g0 * wg[1:2, :] + g1 * wg[2:3, :]).astype(dup.dtype)
            return carry

        lax.fori_loop(0, T // R, pass_b, 0)

    dup, dwv, dwg, dbv, dbg = pl.pallas_call(
        kern, name="convact_bwd", grid=(nj,),
        in_specs=[lo(T), hi(T), lo(T), lo(3), hi(3), lo(1), hi(1)],
        out_specs=[pl.BlockSpec((2, T, cb), lambda j: (0, 0, j)), lo(3), lo(3), lo(1), lo(1)],
        out_shape=[jax.ShapeDtypeStruct((2, T, F), BF16),
                   jax.ShapeDtypeStruct((3, F), F32), jax.ShapeDtypeStruct((3, F), F32),
                   jax.ShapeDtypeStruct((1, F), F32), jax.ShapeDtypeStruct((1, F), F32)],
        scratch_shapes=[pltpu.VMEM((T, cb), F32), pltpu.VMEM((T, cb), F32)],
        compiler_params=_params(("arbitrary",)),
    )(up, up, dact, conv_w, conv_w, conv_b, conv_b)
    return dup, (dwv, dwg), jnp.concatenate([dbv, dbg], axis=1)


def _peers(x, y, c, with_self=False):
    out = []
    for k in range(0 if with_self else 1, N_DEV):
        px = 1 - x if k & 4 else x
        py = 1 - y if k & 2 else y
        pc = 1 - c if k & 1 else c
        out.append((k, (px, py, pc), 4 * px + 2 * py + pc))
    return out


def _exchange_start(name, bufs, modes, after):
    handles, token = _exchange_start_groups(name, [(bufs, modes)], after)
    return handles[0], token


def _exchange_start_groups(name, groups, after):
    bufs = [b for g, _ in groups for b in g]
    modes = [m for _, ms in groups for m in ms]
    first = [int(v) for v in np.cumsum([0] + [len(g) for g, _ in groups])]
    n, ng = len(bufs), len(groups)
    lands = [lax.empty((N_DEV,) + tuple(b.shape[-2:]), b.dtype) for b in bufs]

    def body(*refs):
        ins, land_in = refs[:n], refs[n:2 * n]
        sems = refs[2 * n + 1:2 * n + 1 + 2 * ng]
        token = refs[-1]
        x, y, c = lax.axis_index("x"), lax.axis_index("y"), lax.axis_index("c")
        me = 4 * x + 2 * y + c
        for g in range(ng):
            for b in range(first[g], first[g + 1]):
                for k, peer, slot in _peers(x, y, c, with_self=True):
                    sem = (b - first[g]) * N_DEV + k
                    pltpu.make_async_remote_copy(
                        src_ref=ins[b] if modes[b] == 'gather' else ins[b].at[slot], dst_ref=land_in[b].at[me],
                        send_sem=sems[2 * g].at[sem], recv_sem=sems[2 * g + 1].at[sem],
                        device_id=peer, device_id_type=MESH_ID).start()
        token[...] = jnp.zeros_like(token)

    hbm = pl.BlockSpec(memory_space=pltpu.HBM)
    sem_spec = pl.BlockSpec(memory_space=pltpu.SEMAPHORE)
    operands = [pltpu.with_memory_space_constraint(a, pltpu.HBM) for a in bufs + lands]
    sem_shapes = [pltpu.SemaphoreType.DMA((len(g) * N_DEV,)) for g, _ in groups for _ in range(2)]
    res = pl.pallas_call(
        body, name=name,
        out_shape=(*sem_shapes, *[pltpu.HBM(a.shape, a.dtype) for a in operands], jax.ShapeDtypeStruct((8, LANES), F32)),
        in_specs=[hbm] * (2 * n) + [pl.BlockSpec(memory_space=pl.ANY)],
        out_specs=(*([sem_spec] * (2 * ng)), *([hbm] * (2 * n)), pl.BlockSpec(memory_space=pltpu.VMEM)),
        input_output_aliases={i: 2 * ng + i for i in range(2 * n)},
        compiler_params=pltpu.CompilerParams(has_side_effects=pltpu.SideEffectType.DATAFLOW_SIDE_EFFECTING),
    )(*operands, after)
    srcs, landed = res[2 * ng:2 * ng + n], res[2 * ng + n:2 * ng + 2 * n]
    handles = [(res[2 * g], res[2 * g + 1], list(srcs[first[g]:first[g + 1]]), list(landed[first[g]:first[g + 1]]),
                tuple(modes[first[g]:first[g + 1]])) for g in range(ng)]
    return handles, res[-1]


def _exchange_wait(name, handle, after):
    send_sems, recv_sems, srcs, lands, modes = handle
    n = len(srcs)

    def body(*refs):
        src_in, land_in = refs[:n], refs[n:2 * n]
        send_ref, recv_ref = refs[2 * n], refs[2 * n + 1]
        x, y, c = lax.axis_index("x"), lax.axis_index("y"), lax.axis_index("c")
        for b in range(n):
            for k, peer, slot in _peers(x, y, c, with_self=True):
                sem = b * N_DEV + k
                copy = pltpu.make_async_remote_copy(
                    src_ref=src_in[b] if modes[b] == 'gather' else src_in[b].at[slot], dst_ref=land_in[b].at[slot],
                    send_sem=send_ref.at[sem], recv_sem=recv_ref.at[sem],
                    device_id=peer, device_id_type=MESH_ID)
                copy.wait_send()
                copy.wait_recv()

    hbm = pl.BlockSpec(memory_space=pltpu.HBM)
    sem_spec = pl.BlockSpec(memory_space=pltpu.SEMAPHORE)
    res = pl.pallas_call(
        body, name=name,
        out_shape=tuple(pltpu.HBM(a.shape, a.dtype) for a in srcs + lands),
        in_specs=[hbm] * (2 * n) + [sem_spec, sem_spec, pl.BlockSpec(memory_space=pl.ANY)],
        out_specs=tuple([hbm] * (2 * n)),
        input_output_aliases={i: i for i in range(2 * n)},
        compiler_params=pltpu.CompilerParams(has_side_effects=pltpu.SideEffectType.DATAFLOW_SIDE_EFFECTING),
    )(*srcs, *lands, send_sems, recv_sems, after)
    return list(res[n:])


def _adamw(w, g, m, v):
    m2 = ADAM_B1 * m + (1.0 - ADAM_B1) * g
    v2 = ADAM_B2 * v + (1.0 - ADAM_B2) * (g * g)
    m_hat = m2 / (1.0 - ADAM_B1 ** ADAM_STEP)
    v_hat = v2 / (1.0 - ADAM_B2 ** ADAM_STEP)
    return -ADAM_LR * (m_hat / (jnp.sqrt(v_hat) + ADAM_EPS) + ADAM_WD * w), m2, v2


def _sum_adamw(name, landed, w, m, v):
    R, C = w.shape
    row_bytes = N_DEV * (-(-C // LANES) * LANES) * landed.dtype.itemsize
    tiles = [d for d in range(16, R, 16) if R % d == 0 and d * row_bytes <= ADAM_TILE_BYTES]
    tr = max(tiles) if tiles and R * row_bytes > ADAM_TILE_BYTES else R

    def kern(x_ref, w_ref, m_ref, v_ref, g_out, d_out, m_out, v_out):
        g = x_ref[0].astype(F32)
        for q in range(1, N_DEV):
            g = g + x_ref[q].astype(F32)
        g_out[...] = g
        d_out[...], m_out[...], v_out[...] = _adamw(w_ref[...], g, m_ref[...], v_ref[...])

    blk = pl.BlockSpec((tr, C), lambda i: (i, 0))
    return pl.pallas_call(
        kern, name=name, grid=(R // tr,),
        in_specs=[pl.BlockSpec((N_DEV, tr, C), lambda i: (0, i, 0)), blk, blk, blk],
        out_specs=[blk] * 4, out_shape=[jax.ShapeDtypeStruct((R, C), F32)] * 4,
        compiler_params=_params(("arbitrary",)),
    )(landed, w, m, v)


def _sum_adamw_many(name, items):
    n = len(items)

    def kern(*refs):
        ins, outs = refs[:4 * n], refs[4 * n:]
        for k in range(n):
            x_ref, w_ref, m_ref, v_ref = ins[4 * k:4 * k + 4]
            g = x_ref[0].astype(F32)
            for q in range(1, N_DEV):
                g = g + x_ref[q].astype(F32)
            outs[4 * k][...] = g
            outs[4 * k + 1][...], outs[4 * k + 2][...], outs[4 * k + 3][...] = _adamw(w_ref[...], g, m_ref[...], v_ref[...])

    out_shape = [jax.ShapeDtypeStruct(w.shape, F32) for _, w, _, _ in items for _ in range(4)]
    res = pl.pallas_call(kern, name=name, out_shape=out_shape)(*[a for item in items for a in item])
    return [res[4 * k:4 * k + 4] for k in range(n)]


def _view2d(a):
    if a.ndim == 1:
        return a.reshape(1, -1)
    return a.reshape(-1, a.shape[-1])


def kernel(x, norm_mix_g, w_in, pool_w, pool_scale, ssm_log_neg_a_re, ssm_a_im, ssm_log_dt, ssm_b_re, ssm_b_im, ssm_c_re, ssm_c_im, ssm_d, glu_w, glu_b, out_norm_pool_g, out_norm_ssm_g, w_out, norm_ffn_g, w_up, conv_w, conv_b, w_down, final_norm_g, loss_target, m_norm_mix_g, m_w_in, m_pool_w, m_pool_scale, m_ssm_log_neg_a_re, m_ssm_a_im, m_ssm_log_dt, m_ssm_b_re, m_ssm_b_im, m_ssm_c_re, m_ssm_c_im, m_ssm_d, m_glu_w, m_glu_b, m_out_norm_pool_g, m_out_norm_ssm_g, m_w_out, m_norm_ffn_g, m_w_up, m_conv_w, m_conv_b, m_w_down, m_final_norm_g, v_norm_mix_g, v_w_in, v_pool_w, v_pool_scale, v_ssm_log_neg_a_re, v_ssm_a_im, v_ssm_log_dt, v_ssm_b_re, v_ssm_b_im, v_ssm_c_re, v_ssm_c_im, v_ssm_d, v_glu_w, v_glu_b, v_out_norm_pool_g, v_out_norm_ssm_g, v_w_out, v_norm_ffn_g, v_w_up, v_conv_w, v_conv_b, v_w_down, v_final_norm_g):
    given = dict(locals())
    weights = {n: given[n] for n in WEIGHTS}
    mom1 = {n: given["m_" + n] for n in WEIGHTS}
    mom2 = {n: given["v_" + n] for n in WEIGHTS}

    xs = x[0]
    tgt = loss_target[0]
    T, D = xs.shape
    DP = len(POOL_WINDOWS) * POOL_GROUP
    DS = D - DP
    G = DS // SSM_GROUP
    N, H = SSM_STATE, SSM_GROUP
    F2 = w_up.shape[1] * N_DEV
    F = F2 // 2
    fs = w_up.shape[1]
    row = lambda a: a.reshape(1, -1)

    in_handle, token = _exchange_start("gather_in_start", [w_in.astype(BF16)], ['gather'], after=w_in)
    later = lambda w: (w + token[0, 0]).astype(BF16)
    gather_groups = [([later(glu_w), later(w_out)], ['gather'] * 2),
                     ([later(w_up.T), conv_w], ['gather'] * 2), ([later(w_down)], ['gather'])]
    (mix_handle, up_handle, down_handle), token = _exchange_start_groups("gather_rest_start", gather_groups, token)
    CB = row(conv_b)

    g1, g2, g3 = row(norm_mix_g), row(norm_ffn_g), row(final_norm_g)
    gp, gs = row(out_norm_pool_g), row(out_norm_ssm_g)
    (xn,) = _rowmap("norm_mix", _rms_fwd, [xs], [g1 + token[0:1, 0:1]], [(D, BF16)])
    W_in = _exchange_wait("gather_in_wait", in_handle, xn)[0].reshape(D, D)
    halves = lambda r: (r[:, :DP], r[:, DP:])
    u, u_s5 = _mm_rows("proj_in", xn, W_in, 'nn', halves, [], [], [(DP, F32), (DS, F32)])
    tmp = _pick(T, (1024, 512, 256, 128))
    pool_w_bf = pool_w.astype(BF16)
    ypn = _pool_fwd(u, pool_w_bf, row(pool_scale), gp, tmp)

    u_ssm = _to_chunk_rows(u_s5)
    u_ssm_bf = u_ssm.astype(BF16)
    a_rows = (ssm_log_neg_a_re.reshape(2 * G, N), ssm_a_im.reshape(2 * G, N), ssm_log_dt.reshape(2 * G, 1))
    disc = _ssm_params_fwd(*a_rows)
    nb = G // SSM_BLOCK_GROUPS
    vecs = [d.reshape(2, nb, 1, SSM_BLOCK_GROUPS * N) for d in disc]
    mats = [_groups_side_by_side(ssm_b_re, 3), _groups_side_by_side(ssm_b_im, 3),
            _groups_side_by_side(ssm_c_re, 2), _groups_side_by_side(ssm_c_im, 2)]
    y_dir = [_ssm_fwd("ssm_fwd_%d" % d, u_ssm_bf, mats, vecs, d) for d in range(2)]

    def mix_post(yf, yb, us, d, gw, gb, g):
        y = yf + yb + d * us
        z = _gelu(y)
        gate = _sigmoid(jnp.dot(z.astype(BF16), gw, preferred_element_type=F32) + gb)
        return _rms_fwd(z * gate, g), y

    mix_landed = _exchange_wait("gather_mix_wait", mix_handle, y_dir[1])
    W_glu = mix_landed[0].reshape(DS, DS)
    W_out = mix_landed[1].reshape(D, D)
    ysn, y_ssm = _rowmap("ssm_post", mix_post, [y_dir[0], y_dir[1], u_ssm], [row(ssm_d), W_glu, row(glu_b), gs],
                         [(DS, BF16), (DS, F32)])
    ycat = jnp.concatenate([ypn, _from_chunk_rows(ysn)], axis=1)
    h1, hn = _mm_rows("proj_out", ycat, W_out, 'nn', lambda h, g: (h, _rms_fwd(h, g)), [], [g2],
                      [(D, F32), (D, BF16)], add=xs)
    up_landed = _exchange_wait("gather_up_wait", up_handle, hn)
    W_up_t = up_landed[0].reshape(F2, D)
    CW = up_landed[1].transpose(1, 0, 2).reshape(3, F2)
    up = _mm("ffn_up", hn, W_up_t, 'nt', F32)
    act = _convact_fwd(up, CW, CB, F)
    W_down = _exchange_wait("gather_down_wait", down_handle, act)[0].reshape(F, D)

    def head(h, t, g):
        r = lax.rsqrt(_rowmean(h * h) + EPS)
        hh = h * r
        e = hh * g - t
        loss = 0.5 * jnp.sum(_rowmean(e * e), keepdims=True)
        dy = e * (1.0 / D)
        dxh = dy * g
        dh = r * (dxh - hh * _rowmean(dxh * hh))
        return dh, dh, jnp.broadcast_to(loss, (1, LANES)), _colsum(dy * hh)

    dh2, dh2_bf, loss_acc, dg3 = _mm_rows("ffn_down", act, W_down, 'nn', head, [tgt], [g3],
                                          [(D, F32), (D, BF16)], [(1, LANES), (1, D)], add=h1)

    dact = _mm("ffn_down_dx", dh2_bf, W_down, 'nt', F32)
    dW_down = _mm("ffn_down_dw", act, dh2_bf, 'tn', BF16)
    dup, dCW, dCB = _convact_bwd(up, dact, CW, CB, F)
    dhn_val = _mm("ffn_up_dx_val", dup, W_up_t, 'nn', F32, b_rows=(2, 0), a_plane=0)
    dW_up_t = _mm("ffn_up_dw", dup, hn, 'tn', BF16).reshape(F2, D)

    def shards(full_grad):
        return full_grad.reshape((N_DEV, full_grad.shape[0] // N_DEV) + full_grad.shape[1:])

    half = N_DEV // 2
    dCW_sh = jnp.concatenate([h.reshape(3, half, fs).transpose(1, 0, 2) for h in dCW], axis=0)
    ffn_sent = [shards(dW_down), shards(dW_up_t), dCW_sh]
    ffn_handle, token = _exchange_start("scatter_ffn_start", ffn_sent, ['scatter'] * 3, after=dhn_val)
    g2_late = g2 + token[0:1, 0:1]

    def norm_bwd_add(dy, hx, res, g):
        dx, dg = _rms_bwd(hx, g, dy)
        return dx + res, dg

    def norm_bwd_add2(dy, hx, res, g):
        dx, dg = _rms_bwd(hx, g, dy)
        return dx + res, dx + res, dg

    dh1, dh1_bf, dg2 = _mm_rows("ffn_up_dx_gate", dup, W_up_t, 'nn', norm_bwd_add2, [h1, dh2], [g2_late],
                                [(D, F32), (D, BF16)], [(1, D)], add=dhn_val, b_rows=(2, 1), a_plane=1)
    dycat, d_ysn = _mm_rows("proj_out_dx", dh1_bf, W_out, 'nt', halves, [], [], [(DP, F32), (DS, F32)])
    dW_out = _mm("proj_out_dw", ycat, dh1_bf, 'tn', BF16)

    out_sent = [shards(dW_out)]
    out_handle, token = _exchange_start("scatter_out_start", out_sent, ['scatter'], after=dycat)
    q, dpooled, dpool_w, dpool_scale, dgp = _pool_bwd_rows(u, dycat, pool_w_bf, row(pool_scale),
                                                           gp + token[0:1, 0:1], tmp)

    def mix_post_bwd(y, us, dyn, d, gw, gb, g):
        z, dz_dy = _gelu_and_grad(y)
        gz = z.astype(BF16)
        sig = _sigmoid(jnp.dot(gz, gw, preferred_element_type=F32) + gb)
        dys, dg = _rms_bwd(z * sig, g, dyn)
        dgl = dys * z * sig * (1.0 - sig)
        dgl_bf = dgl.astype(BF16)
        dz = dys * sig + lax.dot_general(dgl_bf, gw, (((1,), (1,)), ((), ())), preferred_element_type=F32)
        dgw = lax.dot_general(gz, dgl_bf, (((0,), (0,)), ((), ())), preferred_element_type=F32)
        dyv = dz * dz_dy
        return dyv, dyv * d, dgw, _colsum(dgl), _colsum(dyv * us), dg

    dyv, du_dir, dglu_w, dglu_b, dssm_d, dgs = _rowmap(
        "ssm_post_bwd", mix_post_bwd, [y_ssm, u_ssm, _to_chunk_rows(d_ysn)], [row(ssm_d), W_glu, row(glu_b), gs],
        [(DS, BF16), (DS, F32)], [(DS, DS), (1, DS), (1, DS), (1, DS)])

    du_ssm = du_dir
    dB, dC, dvec = [], [], []
    for d in range(2):
        res = _ssm_bwd("ssm_bwd_%d" % d, u_ssm_bf, dyv, du_ssm, mats, vecs, d, BF16 if d == 1 else F32)
        du_ssm = res[0]
        dB.append(res[1:3])
        dC.append(res[3:5])
        dvec.append(res[5:9])
    grad_b = [_groups_apart(jnp.stack([dB[0][k], dB[1][k]]), 3) for k in range(2)]
    grad_c = [_groups_apart(jnp.stack([dC[0][k], dC[1][k]]), 2) for k in range(2)]
    cots = [jnp.stack([dvec[0][k], dvec[1][k]]).reshape(2 * G, N) for k in range(4)]
    d_a_re, d_a_im, d_log_dt = _ssm_params_bwd(*a_rows, cots)

    rep_grads = {
        'pool_w': dpool_w, 'pool_scale': dpool_scale, 'ssm_log_neg_a_re': d_a_re, 'ssm_a_im': d_a_im,
        'ssm_log_dt': d_log_dt, 'ssm_b_re': grad_b[0], 'ssm_b_im': grad_b[1], 'ssm_c_re': grad_c[0], 'ssm_c_im': grad_c[1],
        'ssm_d': dssm_d, 'glu_b': dglu_b, 'out_norm_pool_g': dgp, 'out_norm_ssm_g': dgs, 'norm_ffn_g': dg2,
        'conv_b': dCB, 'final_norm_g': dg3}
    wide = ('pool_w', 'ssm_b_re', 'ssm_b_im', 'ssm_c_re', 'ssm_c_im')

    def narrow(n):
        return weights[n].ndim == 4 and weights[n].shape[-1] < LANES // 2

    def travel(n):
        g = rep_grads[n]
        if narrow(n):
            g = jnp.swapaxes(g, 2, 3)
        if n in wide and g.shape[-1] % LANES == 0:
            return _view2d(g).astype(BF16)
        if n in wide:
            return g.reshape(-1, PACK_W if g.size % PACK_W == 0 else LANES).astype(BF16)
        return _view2d(g.reshape(weights[n].shape))

    small = tuple(n for n in REPLICATED if n != 'norm_mix_g')
    small_sent = [shards(dglu_w.astype(BF16))] + [travel(n) for n in small] + [loss_acc]
    small_handle, token = _exchange_start("reduce_small_start", small_sent, ['scatter'] + ['gather'] * (len(small) + 1),
                                          after=d_a_re)

    du = _pool_bwd_band(q, dpooled, _from_chunk_rows(du_ssm), token, tmp)
    dW_in = _mm("proj_in_dw", xn, du, 'tn', BF16)
    in_handle, token = _exchange_start("scatter_in_start", [shards(dW_in)], ['scatter'], after=dW_in)
    dx, dg1 = _mm_rows("proj_in_dx", du, W_in + token[0, 0].astype(BF16), 'nt', norm_bwd_add, [xs, dh1], [g1],
                       [(D, F32)], [(1, D)])
    rep_grads['norm_mix_g'] = dg1
    last_handle, token = _exchange_start("reduce_last_start", [travel('norm_mix_g')], ['gather'], after=dx)

    grads, delta, new_m, new_v = {}, {}, {}, {}

    def update(n, landed_n):
        shape = weights[n].shape
        if narrow(n):
            swapped = lambda a: _view2d(jnp.swapaxes(a, 2, 3))
            landed_n = landed_n.reshape((N_DEV,) + swapped(weights[n]).shape)
            res = _sum_adamw("adamw_" + n, landed_n, swapped(weights[n]), swapped(mom1[n]), swapped(mom2[n]))
            back = lambda r: jnp.swapaxes(r.reshape(shape[:2] + (shape[3], shape[2])), 2, 3)
            grads[n], delta[n], new_m[n], new_v[n] = [back(r) for r in res]
            return
        if n == 'w_up':
            res = _sum_adamw("adamw_" + n, landed_n, weights[n].T, mom1[n].T, mom2[n].T)
            grads[n], delta[n], new_m[n], new_v[n] = [r.T for r in res]
            return
        landed_n = landed_n.reshape((N_DEV,) + _view2d(weights[n]).shape)
        res = _sum_adamw("adamw_" + n, landed_n, _view2d(weights[n]), _view2d(mom1[n]), _view2d(mom2[n]))
        grads[n], delta[n], new_m[n], new_v[n] = [r.reshape(shape) for r in res]

    def update_many(name, named_landed):
        items = [(l.reshape((N_DEV,) + _view2d(weights[n]).shape), _view2d(weights[n]), _view2d(mom1[n]), _view2d(mom2[n]))
                 for n, l in named_landed]
        for (n, _), res in zip(named_landed, _sum_adamw_many(name, items)):
            grads[n], delta[n], new_m[n], new_v[n] = [r.reshape(weights[n].shape) for r in res]

    ffn_landed = _exchange_wait("scatter_ffn_wait", ffn_handle, token)
    for n, l in zip(('w_down', 'w_up'), ffn_landed):
        update(n, l)
    (out_landed,) = _exchange_wait("scatter_out_wait", out_handle, token)
    update('w_out', out_landed)
    small_landed = _exchange_wait("reduce_small_wait", small_handle, token)
    tiny = [('conv_w', ffn_landed[2])]
    for n, l in zip(('glu_w',) + small, small_landed):
        if weights[n].size <= ADAM_TINY:
            tiny.append((n, l))
        else:
            update(n, l)
    update_many("adamw_tiny", tiny)

    def loss_sum(parts_ref, o_ref):
        s = parts_ref[0]
        for q in range(1, N_DEV):
            s = s + parts_ref[q]
        o_ref[...] = s

    loss = pl.pallas_call(loss_sum, name="loss_sum", out_shape=jax.ShapeDtypeStruct((1, LANES), F32))(small_landed[-1])[0, 0]

    update('w_in', _exchange_wait("scatter_in_wait", in_handle, grads[small[-1]])[0])
    update('norm_mix_g', _exchange_wait("reduce_last_wait", last_handle, grads['w_in'])[0])

    return (loss, dx[None], *[grads[n] for n in WEIGHTS], *[delta[n] for n in WEIGHTS],
            *[new_m[n] for n in WEIGHTS], *[new_v[n] for n in WEIGHTS])
```
